```python
import jax, jax.numpy as jnp
from jax import lax
import numpy as np

D_MODEL = 1024
BATCH = 8
SEQ = 8192
DEPTH = 2

D_FF = 2816
HG_DK = 128
HG_HEADS = D_MODEL // HG_DK
HG_DV = D_MODEL // HG_HEADS
HG_WK = HG_HEADS * HG_DK
HG_WV = HG_HEADS * HG_DV
HG_CHUNK = 64
ATT_PATTERNS = ((128, 1), (512, 4), (2048, 16))
ATT_GROUPS = 3
ATT_HEADS = 4
ATT_DH = 128
ATT_W = ATT_GROUPS * ATT_HEADS * ATT_DH
ATT_OUT = ATT_HEADS * ATT_DH
ROPE_THETA = 10000.0
EPS = 1e-6
SPLIT_SIZES = (HG_WK, HG_WK, HG_WV, HG_WV, ATT_W, ATT_W, ATT_W, D_MODEL, D_MODEL)
P_IN = sum(SPLIT_SIZES)

kernel_name = "hybrid_hgrn2_dilated_attn_macaron"


def rms(x):
    xf = x.astype(jnp.float32)
    return xf * lax.rsqrt(jnp.mean(xf * xf, axis=-1, keepdims=True) + EPS)


def rmsnorm(x, g):
    return (rms(x) * g.astype(jnp.float32)).astype(x.dtype)


def swiglu(h, w_in, w_out):
    a, b = jnp.split(h @ w_in, 2, axis=-1)
    return (jax.nn.silu(a) * b) @ w_out


def rope_tables(t):
    pos = jnp.arange(t, dtype=jnp.float32)
    inv = ROPE_THETA ** (-jnp.arange(0, ATT_DH, 2, dtype=jnp.float32) / ATT_DH)
    ang = pos[:, None] * inv[None, :]
    ang = jnp.concatenate([ang, ang], axis=-1)
    return jnp.cos(ang), jnp.sin(ang)


def apply_rope(x, cos, sin):
    x1, x2 = jnp.split(x, 2, axis=-1)
    return x * cos + jnp.concatenate([-x2, x1], axis=-1) * sin


def hgrn2_chunk_scan(q, k, v, log_f):
    b, t, h, dk = q.shape
    dv = v.shape[-1]
    n = t // HG_CHUNK

    def chunks(a):
        return a.reshape(b, n, HG_CHUNK, h, a.shape[-1]).transpose(1, 0, 3, 2, 4)

    causal = jnp.tril(jnp.ones((HG_CHUNK, HG_CHUNK), dtype=bool))[:, :, None]

    def step(state, inp):
        qc, kc, vc, gc = inp
        gcum = jnp.cumsum(gc, axis=2)
        diff = gcum[:, :, :, None, :] - gcum[:, :, None, :, :]
        decay = jnp.exp(jnp.where(causal, diff, -jnp.inf))
        attn = jnp.einsum('bhtk,bhsk,bhtsk->bhts', qc, kc, decay)
        o = jnp.einsum('bhts,bhsv->bhtv', attn, vc) + jnp.einsum(
            'bhtk,bhkv->bhtv', qc * jnp.exp(gcum), state)
        g_last = gcum[:, :, -1:, :]
        state = jnp.exp(g_last[:, :, 0, :, None]) * state + jnp.einsum(
            'bhsk,bhsv->bhkv', kc * jnp.exp(g_last - gcum), vc)
        return state, o

    s0 = jnp.zeros((b, h, dk, dv), jnp.float32)
    _, o = lax.scan(step, s0, (chunks(q), chunks(k), chunks(v), chunks(log_f)))
    return o.transpose(1, 0, 3, 2, 4).reshape(b, t, h * dv)


def dilated_window_attention(q, k, v, window, dilation):
    b, h, t, dh = q.shape
    back = window // dilation
    blk = back
    L = t // dilation
    nb = -(-L // blk)
    Lp = nb * blk

    def to_res(a):
        a = a.reshape(b, h, L, dilation, dh).transpose(0, 1, 3, 2, 4)
        return jnp.pad(a, ((0, 0), (0, 0), (0, 0), (0, Lp - L), (0, 0)))

    def kv_blocks(a):
        a = jnp.pad(a, ((0, 0), (0, 0), (0, 0), (blk, 0), (0, 0)))
        a = a.reshape(b, h, dilation, nb + 1, blk, dh)
        return jnp.concatenate([a[:, :, :, :-1], a[:, :, :, 1:]], axis=4)

    qb = to_res(q).reshape(b, h, dilation, nb, blk, dh)
    kb = kv_blocks(to_res(k))
    vb = kv_blocks(to_res(v))
    s = jnp.einsum('bhrnqd,bhrnkd->bhrnqk', qb, kb) * (dh ** -0.5)
    qi = jnp.arange(blk)[:, None]
    ki = jnp.arange(2 * blk)[None, :]
    band = (ki >= qi) & (ki <= qi + back)
    valid = (ki >= blk)[None] | (jnp.arange(nb)[:, None, None] > 0)
    mask = band[None] & valid
    s = jnp.where(mask, s, -jnp.inf)
    lse = jax.nn.logsumexp(s, axis=-1)
    p = jnp.exp(s - lse[..., None])
    o = jnp.einsum('bhrnqk,bhrnkd->bhrnqd', p, vb)
    o = o.reshape(b, h, dilation, Lp, dh)[:, :, :, :L].transpose(0, 1, 3, 2, 4)
    lse = lse.reshape(b, h, dilation, Lp)[:, :, :, :L].transpose(0, 1, 3, 2)
    return o.reshape(b, h, t, dh), lse.reshape(b, h, t)


def _fwd_setup_inputs(seed: int = 0) -> dict:
    key = jax.random.key(seed)
    ks = jax.random.split(key, 16)
    f32 = jnp.float32

    def nrm(k, shape, fan_in):
        return jax.random.normal(k, shape, f32) * (fan_in ** -0.5)

    def gain(k, shape):
        return 1.0 + 0.05 * jax.random.normal(k, shape, f32)

    return {
        "x": jax.random.normal(ks[0], (BATCH, SEQ, D_MODEL), f32),
        "ffn1_norm": gain(ks[1], (DEPTH, D_MODEL)),
        "ffn1_w_in": nrm(ks[2], (DEPTH, D_MODEL, 2 * D_FF), D_MODEL),
        "ffn1_w_out": nrm(ks[3], (DEPTH, D_FF, D_MODEL), D_FF),
        "mix_norm": gain(ks[4], (DEPTH, D_MODEL)),
        "w_in": nrm(ks[5], (DEPTH, D_MODEL, P_IN), D_MODEL),
        "hgrn_lb_logits": 0.5 * jax.random.normal(ks[6], (DEPTH, HG_WK), f32),
        "hgrn_out_norm": gain(ks[7], (DEPTH, HG_WV)),
        "attn_q_norm": gain(ks[8], (DEPTH, ATT_GROUPS, ATT_DH)),
        "attn_k_norm": gain(ks[9], (DEPTH, ATT_GROUPS, ATT_DH)),
        "w_branch_a": nrm(ks[10], (DEPTH, HG_WV, D_MODEL), HG_WV),
        "w_branch_b": nrm(ks[11], (DEPTH, ATT_OUT, D_MODEL), ATT_OUT),
        "w_out": nrm(ks[12], (DEPTH, D_MODEL, D_MODEL), D_MODEL),
        "ffn2_norm": gain(ks[13], (DEPTH, D_MODEL)),
        "ffn2_w_in": nrm(ks[14], (DEPTH, D_MODEL, 2 * D_FF), D_MODEL),
        "ffn2_w_out": nrm(ks[15], (DEPTH, D_FF, D_MODEL), D_FF),
    }


def _fwd_reference(x, ffn1_norm, ffn1_w_in, ffn1_w_out, mix_norm, w_in, hgrn_lb_logits,
              hgrn_out_norm, attn_q_norm, attn_k_norm, w_branch_a, w_branch_b, w_out,
              ffn2_norm, ffn2_w_in, ffn2_w_out):
    b, t, _ = x.shape
    f32 = jnp.float32
    cos, sin = rope_tables(t)
    lb_all = jnp.cumsum(jax.nn.softmax(hgrn_lb_logits.astype(f32), axis=0), axis=0)
    lb_all = lb_all - lb_all[0:1]
    split_idx = [int(s) for s in np.cumsum(SPLIT_SIZES)[:-1]]

    for l in range(DEPTH):
        x = x + 0.5 * swiglu(rmsnorm(x, ffn1_norm[l]), ffn1_w_in[l], ffn1_w_out[l])

        h = rmsnorm(x, mix_norm[l])
        hq, hf, hi, hg, aq, ak, av, ga, gb = jnp.split(h @ w_in[l], split_idx, axis=-1)

        lb = lb_all[l]
        f = lb + (1.0 - lb) * jax.nn.sigmoid(hf.astype(f32))
        q_a = jax.nn.silu(hq.astype(f32)).reshape(b, t, HG_HEADS, HG_DK)
        k_a = (1.0 - f).reshape(b, t, HG_HEADS, HG_DK)
        v_a = hi.astype(f32).reshape(b, t, HG_HEADS, HG_DV)
        log_f = jnp.log(f).reshape(b, t, HG_HEADS, HG_DK)
        o_a = hgrn2_chunk_scan(q_a, k_a, v_a, log_f)
        o_a = rms(o_a.reshape(b, t, HG_HEADS, HG_DV)).reshape(b, t, HG_WV)
        o_a = o_a * hgrn_out_norm[l].astype(f32) * jax.nn.silu(hg.astype(f32))
        y_a = o_a.astype(x.dtype) @ w_branch_a[l]

        def heads(a):
            return a.reshape(b, t, ATT_GROUPS, ATT_HEADS, ATT_DH).transpose(2, 0, 3, 1, 4).astype(f32)

        qn = attn_q_norm[l].astype(f32)[:, None, None, None, :]
        kn = attn_k_norm[l].astype(f32)[:, None, None, None, :]
        q_b = apply_rope(rms(heads(aq)) * qn, cos, sin)
        k_b = apply_rope(rms(heads(ak)) * kn, cos, sin)
        v_b = heads(av)
        outs, lses = [], []
        for g, (window, dilation) in enumerate(ATT_PATTERNS):
            o_g, lse_g = dilated_window_attention(q_b[g], k_b[g], v_b[g], window, dilation)
            outs.append(o_g)
            lses.append(lse_g)
        alpha = jax.nn.softmax(jnp.stack(lses, axis=0), axis=0)
        o_b = jnp.einsum('gbht,gbhtd->bthd', alpha, jnp.stack(outs, axis=0)).reshape(b, t, ATT_OUT)
        y_b = o_b.astype(x.dtype) @ w_branch_b[l]

        merged = jax.nn.sigmoid(ga) * y_a + jax.nn.sigmoid(gb) * y_b
        x = x + merged @ w_out[l]

        x = x + 0.5 * swiglu(rmsnorm(x, ffn2_norm[l]), ffn2_w_in[l], ffn2_w_out[l])
    return x


import jax as _jax
import jax.numpy as _jnp

TWIN_FORMAT = 'train_step'
FWD_PARAMS = ['x', 'ffn1_norm', 'ffn1_w_in', 'ffn1_w_out', 'mix_norm', 'w_in', 'hgrn_lb_logits', 'hgrn_out_norm', 'attn_q_norm', 'attn_k_norm', 'w_branch_a', 'w_branch_b', 'w_out', 'ffn2_norm', 'ffn2_w_in', 'ffn2_w_out']
TWIN_WEIGHTS = ['ffn1_norm', 'ffn1_w_in', 'ffn1_w_out', 'mix_norm', 'w_in', 'hgrn_lb_logits', 'hgrn_out_norm', 'attn_q_norm', 'attn_k_norm', 'w_branch_a', 'w_branch_b', 'w_out', 'ffn2_norm', 'ffn2_w_in', 'ffn2_w_out']
TWIN_DIFF_INPUT = 'x'
TWIN_INPUTS = ['x', 'ffn1_norm', 'ffn1_w_in', 'ffn1_w_out', 'mix_norm', 'w_in', 'hgrn_lb_logits', 'hgrn_out_norm', 'attn_q_norm', 'attn_k_norm', 'w_branch_a', 'w_branch_b', 'w_out', 'ffn2_norm', 'ffn2_w_in', 'ffn2_w_out', 'loss_target', 'm_ffn1_norm', 'm_ffn1_w_in', 'm_ffn1_w_out', 'm_mix_norm', 'm_w_in', 'm_hgrn_lb_logits', 'm_hgrn_out_norm', 'm_attn_q_norm', 'm_attn_k_norm', 'm_w_branch_a', 'm_w_branch_b', 'm_w_out', 'm_ffn2_norm', 'm_ffn2_w_in', 'm_ffn2_w_out', 'v_ffn1_norm', 'v_ffn1_w_in', 'v_ffn1_w_out', 'v_mix_norm', 'v_w_in', 'v_hgrn_lb_logits', 'v_hgrn_out_norm', 'v_attn_q_norm', 'v_attn_k_norm', 'v_w_branch_a', 'v_w_branch_b', 'v_w_out', 'v_ffn2_norm', 'v_ffn2_w_in', 'v_ffn2_w_out']
TWIN_OUTPUTS = ['loss', 'grad_x', 'grad_ffn1_norm', 'grad_ffn1_w_in', 'grad_ffn1_w_out', 'grad_mix_norm', 'grad_w_in', 'grad_hgrn_lb_logits', 'grad_hgrn_out_norm', 'grad_attn_q_norm', 'grad_attn_k_norm', 'grad_w_branch_a', 'grad_w_branch_b', 'grad_w_out', 'grad_ffn2_norm', 'grad_ffn2_w_in', 'grad_ffn2_w_out', 'delta_ffn1_norm', 'delta_ffn1_w_in', 'delta_ffn1_w_out', 'delta_mix_norm', 'delta_w_in', 'delta_hgrn_lb_logits', 'delta_hgrn_out_norm', 'delta_attn_q_norm', 'delta_attn_k_norm', 'delta_w_branch_a', 'delta_w_branch_b', 'delta_w_out', 'delta_ffn2_norm', 'delta_ffn2_w_in', 'delta_ffn2_w_out', 'new_m_ffn1_norm', 'new_m_ffn1_w_in', 'new_m_ffn1_w_out', 'new_m_mix_norm', 'new_m_w_in', 'new_m_hgrn_lb_logits', 'new_m_hgrn_out_norm', 'new_m_attn_q_norm', 'new_m_attn_k_norm', 'new_m_w_branch_a', 'new_m_w_branch_b', 'new_m_w_out', 'new_m_ffn2_norm', 'new_m_ffn2_w_in', 'new_m_ffn2_w_out', 'new_v_ffn1_norm', 'new_v_ffn1_w_in', 'new_v_ffn1_w_out', 'new_v_mix_norm', 'new_v_w_in', 'new_v_hgrn_lb_logits', 'new_v_hgrn_out_norm', 'new_v_attn_q_norm', 'new_v_attn_k_norm', 'new_v_w_branch_a', 'new_v_w_branch_b', 'new_v_w_out', 'new_v_ffn2_norm', 'new_v_ffn2_w_in', 'new_v_ffn2_w_out']
TWIN_LEAF_KINDS = {'loss': 'loss', 'grad_x': 'grad_x', 'grad_ffn1_norm': 'grad_w', 'grad_ffn1_w_in': 'grad_w', 'grad_ffn1_w_out': 'grad_w', 'grad_mix_norm': 'grad_w', 'grad_w_in': 'grad_w', 'grad_hgrn_lb_logits': 'grad_w', 'grad_hgrn_out_norm': 'grad_w', 'grad_attn_q_norm': 'grad_w', 'grad_attn_k_norm': 'grad_w', 'grad_w_branch_a': 'grad_w', 'grad_w_branch_b': 'grad_w', 'grad_w_out': 'grad_w', 'grad_ffn2_norm': 'grad_w', 'grad_ffn2_w_in': 'grad_w', 'grad_ffn2_w_out': 'grad_w', 'delta_ffn1_norm': 'delta_w', 'delta_ffn1_w_in': 'delta_w', 'delta_ffn1_w_out': 'delta_w', 'delta_mix_norm': 'delta_w', 'delta_w_in': 'delta_w', 'delta_hgrn_lb_logits': 'delta_w', 'delta_hgrn_out_norm': 'delta_w', 'delta_attn_q_norm': 'delta_w', 'delta_attn_k_norm': 'delta_w', 'delta_w_branch_a': 'delta_w', 'delta_w_branch_b': 'delta_w', 'delta_w_out': 'delta_w', 'delta_ffn2_norm': 'delta_w', 'delta_ffn2_w_in': 'delta_w', 'delta_ffn2_w_out': 'delta_w', 'new_m_ffn1_norm': 'new_m', 'new_m_ffn1_w_in': 'new_m', 'new_m_ffn1_w_out': 'new_m', 'new_m_mix_norm': 'new_m', 'new_m_w_in': 'new_m', 'new_m_hgrn_lb_logits': 'new_m', 'new_m_hgrn_out_norm': 'new_m', 'new_m_attn_q_norm': 'new_m', 'new_m_attn_k_norm': 'new_m', 'new_m_w_branch_a': 'new_m', 'new_m_w_branch_b': 'new_m', 'new_m_w_out': 'new_m', 'new_m_ffn2_norm': 'new_m', 'new_m_ffn2_w_in': 'new_m', 'new_m_ffn2_w_out': 'new_m', 'new_v_ffn1_norm': 'new_v', 'new_v_ffn1_w_in': 'new_v', 'new_v_ffn1_w_out': 'new_v', 'new_v_mix_norm': 'new_v', 'new_v_w_in': 'new_v', 'new_v_hgrn_lb_logits': 'new_v', 'new_v_hgrn_out_norm': 'new_v', 'new_v_attn_q_norm': 'new_v', 'new_v_attn_k_norm': 'new_v', 'new_v_w_branch_a': 'new_v', 'new_v_w_branch_b': 'new_v', 'new_v_w_out': 'new_v', 'new_v_ffn2_norm': 'new_v', 'new_v_ffn2_w_in': 'new_v', 'new_v_ffn2_w_out': 'new_v'}


def _forward(args):
    return _fwd_reference(*[args[k] for k in FWD_PARAMS])


def _output_shape():
    def fwd():
        inp = _fwd_setup_inputs(0)
        return _fwd_reference(*[inp[k] for k in FWD_PARAMS])
    out = _jax.eval_shape(fwd)
    return out.shape, out.dtype

N_MICROBATCH = 1
ADAM_LR = 0.001
ADAM_B1 = 0.9
ADAM_B2 = 0.999
ADAM_EPS = 1e-08
ADAM_WD = 0.01
ADAM_STEP = 10
PER_EXAMPLE_BATCH_AXIS = {'x': 0, 'loss_target': 0}
SHARED_INPUTS = []
_WEIGHT_DTYPES = {'ffn1_norm': _jnp.float32, 'ffn1_w_in': _jnp.float32, 'ffn1_w_out': _jnp.float32, 'mix_norm': _jnp.float32, 'w_in': _jnp.float32, 'hgrn_lb_logits': _jnp.float32, 'hgrn_out_norm': _jnp.float32, 'attn_q_norm': _jnp.float32, 'attn_k_norm': _jnp.float32, 'w_branch_a': _jnp.float32, 'w_branch_b': _jnp.float32, 'w_out': _jnp.float32, 'ffn2_norm': _jnp.float32, 'ffn2_w_in': _jnp.float32, 'ffn2_w_out': _jnp.float32}
MOMENT_SCALE = {'ffn1_norm': 1.232333e+01, 'ffn1_w_in': 1.237197e-01, 'ffn1_w_out': 2.191332e-01, 'mix_norm': 8.202666e+00, 'w_in': 9.542174e-02, 'hgrn_lb_logits': 1.387749e-02, 'hgrn_out_norm': 6.624222e+00, 'attn_q_norm': 2.760677e-01, 'attn_k_norm': 2.764450e-01, 'w_branch_a': 2.535229e-01, 'w_branch_b': 5.914418e-02, 'w_out': 2.465841e-01, 'ffn2_norm': 1.249094e+01, 'ffn2_w_in': 1.140470e-01, 'ffn2_w_out': 2.098778e-01}


def _to_microbatches(a, axis):
    t = _jnp.moveaxis(a, axis, 0)
    t = t.reshape((N_MICROBATCH, t.shape[0] // N_MICROBATCH) + t.shape[1:])
    return _jnp.moveaxis(t, 1, axis + 1)


def setup_inputs(seed: int = 0) -> dict:
    inp = _fwd_setup_inputs(seed)
    key = _jax.random.fold_in(_jax.random.key(seed), 7919)
    shape, _ = _output_shape()
    out = dict(inp)
    out["loss_target"] = _jax.random.normal(_jax.random.fold_in(key, 0), shape, _jnp.float32)
    for i, name in enumerate(TWIN_WEIGHTS):
        w = inp[name].astype(_jnp.float32)
        if MOMENT_SCALE is None:
            s = _jnp.sqrt(_jnp.mean(_jnp.square(w)) + 1e-30)
        else:
            s = MOMENT_SCALE[name]
        km, kv = _jax.random.split(_jax.random.fold_in(key, i + 1))
        out[name] = w
        out["m_" + name] = s * _jax.random.normal(km, w.shape, _jnp.float32)
        out["v_" + name] = (s * s) * _jax.random.uniform(kv, w.shape, _jnp.float32, 0.5, 1.5)
    if N_MICROBATCH > 1:
        for name, axis in PER_EXAMPLE_BATCH_AXIS.items():
            out[name] = _to_microbatches(out[name], axis)
    return {'x': out['x'], 'ffn1_norm': out['ffn1_norm'], 'ffn1_w_in': out['ffn1_w_in'], 'ffn1_w_out': out['ffn1_w_out'], 'mix_norm': out['mix_norm'], 'w_in': out['w_in'], 'hgrn_lb_logits': out['hgrn_lb_logits'], 'hgrn_out_norm': out['hgrn_out_norm'], 'attn_q_norm': out['attn_q_norm'], 'attn_k_norm': out['attn_k_norm'], 'w_branch_a': out['w_branch_a'], 'w_branch_b': out['w_branch_b'], 'w_out': out['w_out'], 'ffn2_norm': out['ffn2_norm'], 'ffn2_w_in': out['ffn2_w_in'], 'ffn2_w_out': out['ffn2_w_out'], 'loss_target': out['loss_target'], 'm_ffn1_norm': out['m_ffn1_norm'], 'm_ffn1_w_in': out['m_ffn1_w_in'], 'm_ffn1_w_out': out['m_ffn1_w_out'], 'm_mix_norm': out['m_mix_norm'], 'm_w_in': out['m_w_in'], 'm_hgrn_lb_logits': out['m_hgrn_lb_logits'], 'm_hgrn_out_norm': out['m_hgrn_out_norm'], 'm_attn_q_norm': out['m_attn_q_norm'], 'm_attn_k_norm': out['m_attn_k_norm'], 'm_w_branch_a': out['m_w_branch_a'], 'm_w_branch_b': out['m_w_branch_b'], 'm_w_out': out['m_w_out'], 'm_ffn2_norm': out['m_ffn2_norm'], 'm_ffn2_w_in': out['m_ffn2_w_in'], 'm_ffn2_w_out': out['m_ffn2_w_out'], 'v_ffn1_norm': out['v_ffn1_norm'], 'v_ffn1_w_in': out['v_ffn1_w_in'], 'v_ffn1_w_out': out['v_ffn1_w_out'], 'v_mix_norm': out['v_mix_norm'], 'v_w_in': out['v_w_in'], 'v_hgrn_lb_logits': out['v_hgrn_lb_logits'], 'v_hgrn_out_norm': out['v_hgrn_out_norm'], 'v_attn_q_norm': out['v_attn_q_norm'], 'v_attn_k_norm': out['v_attn_k_norm'], 'v_w_branch_a': out['v_w_branch_a'], 'v_w_branch_b': out['v_w_branch_b'], 'v_w_out': out['v_w_out'], 'v_ffn2_norm': out['v_ffn2_norm'], 'v_ffn2_w_in': out['v_ffn2_w_in'], 'v_ffn2_w_out': out['v_ffn2_w_out']}


def _loss(weights, diff, rest, loss_target):
    with _jax.named_scope("forward"):
        args = {**rest, TWIN_DIFF_INPUT: diff, **{k: w.astype(_WEIGHT_DTYPES[k]) for k, w in weights.items()}}
        y = _forward(args)
    with _jax.named_scope("loss_head"):
        err = _jnp.square(y.astype(_jnp.float32) - loss_target)
        return 0.5 * _jnp.sum(_jnp.mean(err, axis=-1)) if err.ndim else 0.5 * err


def _adamw(w, g, m, v):
    m = ADAM_B1 * m + (1.0 - ADAM_B1) * g
    v = ADAM_B2 * v + (1.0 - ADAM_B2) * _jnp.square(g)
    m_hat = m / (1.0 - ADAM_B1 ** ADAM_STEP)
    v_hat = v / (1.0 - ADAM_B2 ** ADAM_STEP)
    delta = -ADAM_LR * (m_hat / (_jnp.sqrt(v_hat) + ADAM_EPS) + ADAM_WD * w)
    return delta, m, v


def reference(x, ffn1_norm, ffn1_w_in, ffn1_w_out, mix_norm, w_in, hgrn_lb_logits, hgrn_out_norm, attn_q_norm, attn_k_norm, w_branch_a, w_branch_b, w_out, ffn2_norm, ffn2_w_in, ffn2_w_out, loss_target, m_ffn1_norm, m_ffn1_w_in, m_ffn1_w_out, m_mix_norm, m_w_in, m_hgrn_lb_logits, m_hgrn_out_norm, m_attn_q_norm, m_attn_k_norm, m_w_branch_a, m_w_branch_b, m_w_out, m_ffn2_norm, m_ffn2_w_in, m_ffn2_w_out, v_ffn1_norm, v_ffn1_w_in, v_ffn1_w_out, v_mix_norm, v_w_in, v_hgrn_lb_logits, v_hgrn_out_norm, v_attn_q_norm, v_attn_k_norm, v_w_branch_a, v_w_branch_b, v_w_out, v_ffn2_norm, v_ffn2_w_in, v_ffn2_w_out):
    given = dict(x=x, ffn1_norm=ffn1_norm, ffn1_w_in=ffn1_w_in, ffn1_w_out=ffn1_w_out, mix_norm=mix_norm, w_in=w_in, hgrn_lb_logits=hgrn_lb_logits, hgrn_out_norm=hgrn_out_norm, attn_q_norm=attn_q_norm, attn_k_norm=attn_k_norm, w_branch_a=w_branch_a, w_branch_b=w_branch_b, w_out=w_out, ffn2_norm=ffn2_norm, ffn2_w_in=ffn2_w_in, ffn2_w_out=ffn2_w_out, loss_target=loss_target, m_ffn1_norm=m_ffn1_norm, m_ffn1_w_in=m_ffn1_w_in, m_ffn1_w_out=m_ffn1_w_out, m_mix_norm=m_mix_norm, m_w_in=m_w_in, m_hgrn_lb_logits=m_hgrn_lb_logits, m_hgrn_out_norm=m_hgrn_out_norm, m_attn_q_norm=m_attn_q_norm, m_attn_k_norm=m_attn_k_norm, m_w_branch_a=m_w_branch_a, m_w_branch_b=m_w_branch_b, m_w_out=m_w_out, m_ffn2_norm=m_ffn2_norm, m_ffn2_w_in=m_ffn2_w_in, m_ffn2_w_out=m_ffn2_w_out, v_ffn1_norm=v_ffn1_norm, v_ffn1_w_in=v_ffn1_w_in, v_ffn1_w_out=v_ffn1_w_out, v_mix_norm=v_mix_norm, v_w_in=v_w_in, v_hgrn_lb_logits=v_hgrn_lb_logits, v_hgrn_out_norm=v_hgrn_out_norm, v_attn_q_norm=v_attn_q_norm, v_attn_k_norm=v_attn_k_norm, v_w_branch_a=v_w_branch_a, v_w_branch_b=v_w_branch_b, v_w_out=v_w_out, v_ffn2_norm=v_ffn2_norm, v_ffn2_w_in=v_ffn2_w_in, v_ffn2_w_out=v_ffn2_w_out)
    weights = {n: given[n] for n in TWIN_WEIGHTS}
    shared = {n: given[n] for n in SHARED_INPUTS}
    per_example = {n: given[n] for n in ['x']}
    grad_fn = _jax.value_and_grad(_loss, argnums=(0, 1))

    def one_microbatch(ex, loss_target):
        ex = dict(ex)
        diff = ex.pop(TWIN_DIFF_INPUT)
        return grad_fn(weights, diff, {**shared, **ex}, loss_target)

    if N_MICROBATCH == 1:
        loss, (grad_w, grad_x) = one_microbatch(per_example, given["loss_target"])
    else:
        def body(carry, xs):
            loss_sum, grad_sum = carry
            l_k, (gw_k, gx_k) = one_microbatch(xs[0], xs[1])
            with _jax.named_scope("update"):
                return (loss_sum + l_k, _jax.tree.map(_jnp.add, grad_sum, gw_k)), gx_k

        init = (_jnp.zeros((), _jnp.float32), _jax.tree.map(_jnp.zeros_like, weights))
        (loss, grad_w), grad_x = _jax.lax.scan(body, init, (per_example, given["loss_target"]))
    with _jax.named_scope("update"):
        delta_w, new_m, new_v = {}, {}, {}
        for n in TWIN_WEIGHTS:
            delta_w[n], new_m[n], new_v[n] = _adamw(weights[n], grad_w[n], given["m_" + n], given["v_" + n])
    return (loss, grad_x, *[grad_w[n] for n in TWIN_WEIGHTS], *[delta_w[n] for n in TWIN_WEIGHTS],
            *[new_m[n] for n in TWIN_WEIGHTS], *[new_v[n] for n in TWIN_WEIGHTS])
```

```python
import functools
import math

import jax
import jax.numpy as jnp
import numpy as np
from jax import lax
from jax.experimental import pallas as pl
from jax.experimental.pallas import tpu as pltpu

F32 = jnp.float32
BF16 = jnp.bfloat16

N_DEV = 8
EPS = 1e-6
HG_DK = 128
HG_CHUNK = 64
HG_SUB = 16
ATT_PATTERNS = ((128, 1), (512, 4), (2048, 16))
ATT_GROUPS = 3
ATT_HEADS = 4
ATT_DH = 128
ATT_BLK = 128
ROPE_THETA = 10000.0
ADAM_LR, ADAM_B1, ADAM_B2, ADAM_EPS, ADAM_WD, ADAM_STEP = 0.001, 0.9, 0.999, 1e-08, 0.01, 10
VMEM_LIMIT_BYTES = 56 * 1024 * 1024
MESH = pl.DeviceIdType.MESH


def _cparams(sem, **kw):
    return pltpu.CompilerParams(dimension_semantics=sem, vmem_limit_bytes=VMEM_LIMIT_BYTES, **kw)


def _sigmoid(x):
    return 1.0 / (1.0 + jnp.exp(-x))


def _mm(a_list, b_list, pairs, n_acc, fin, out_dtypes, *, m, n, k, ta=False, tb=False, bm, bn, bk,
        b_off=None, extras=(), e_off=None, name):
    bm, bn, bk = min(bm, m), min(bn, n), min(bk, k)
    assert m % bm == 0 and n % bn == 0 and k % bk == 0, (name, m, n, k, bm, bn, bk)
    nk = k // bk
    b_off = b_off or [(0, 0)] * len(b_list)
    e_off = e_off or [0] * len(extras)
    na, nb, ne = len(a_list), len(b_list), len(extras)
    dn = (((0,) if ta else (1,), (1,) if tb else (0,)), ((), ()))

    def body(*refs):
        a_refs, b_refs = refs[:na], refs[na:na + nb]
        e_refs = refs[na + nb:na + nb + ne]
        o_refs = refs[na + nb + ne:na + nb + ne + len(out_dtypes)]
        acc_refs = refs[na + nb + ne + len(out_dtypes):]
        kk = pl.program_id(2)
        parts = [None] * n_acc
        for ai, bi, ci in pairs:
            p = lax.dot_general(a_refs[ai][...], b_refs[bi][...], dn, preferred_element_type=F32)
            parts[ci] = p if parts[ci] is None else parts[ci] + p

        def finish(accs):
            outs = fin(accs, [e[...] for e in e_refs])
            for o_ref, o in zip(o_refs, outs):
                o_ref[...] = o.astype(o_ref.dtype)

        if nk == 1:
            finish(parts)
        else:
            @pl.when(kk == 0)
            def _():
                for c in range(n_acc):
                    acc_refs[c][...] = parts[c]

            @pl.when(kk > 0)
            def _():
                for c in range(n_acc):
                    acc_refs[c][...] += parts[c]

            @pl.when(kk == nk - 1)
            def _():
                finish([acc_refs[c][...] for c in range(n_acc)])

    a_spec = pl.BlockSpec((bk, bm), lambda i, j, q: (q, i)) if ta else pl.BlockSpec((bm, bk), lambda i, j, q: (i, q))

    def b_spec(off):
        on, ok = off
        if tb:
            return pl.BlockSpec((bn, bk), lambda i, j, q: (j + on, q + ok))
        return pl.BlockSpec((bk, bn), lambda i, j, q: (q + ok, j + on))

    mn_spec = pl.BlockSpec((bm, bn), lambda i, j, q: (i, j))
    outs = pl.pallas_call(
        body,
        out_shape=[jax.ShapeDtypeStruct((m, n), d) for d in out_dtypes],
        grid=(m // bm, n // bn, nk),
        in_specs=[a_spec] * na + [b_spec(o) for o in b_off]
        + [pl.BlockSpec((bm, bn), lambda i, j, q, o=o: (i, j + o)) for o in e_off],
        out_specs=[mn_spec] * len(out_dtypes),
        scratch_shapes=[pltpu.VMEM((bm, bn), F32) for _ in range(n_acc if nk > 1 else 0)],
        compiler_params=_cparams(("parallel", "parallel", "arbitrary")),
        name=name,
    )(*a_list, *b_list, *extras)
    return outs


def _first(accs, ex):
    return (accs[0],)


def _rowwise(fn, ins, consts, out_defs, sum_widths, *, bm, name):
    t = ins[0][0].shape[0]
    bm = min(bm, t)
    assert t % bm == 0, (name, t, bm)
    ni, nc, no, ns = len(ins), len(consts), len(out_defs), len(sum_widths)

    def body(*refs):
        i_refs, c_refs = refs[:ni], refs[ni:ni + nc]
        o_refs, s_refs = refs[ni + nc:ni + nc + no], refs[ni + nc + no:]
        outs, sums = fn([r[...] for r in i_refs], [r[...] for r in c_refs])
        for o_ref, o in zip(o_refs, outs):
            o_ref[...] = o.astype(o_ref.dtype)
        if ns:
            first = pl.program_id(0) == 0

            @pl.when(first)
            def _():
                for s_ref, s in zip(s_refs, sums):
                    s_ref[...] = s

            @pl.when(jnp.logical_not(first))
            def _():
                for s_ref, s in zip(s_refs, sums):
                    s_ref[...] += s

    def win(width, cb):
        return pl.BlockSpec((bm, width), lambda i: (i, cb))

    res = pl.pallas_call(
        body,
        out_shape=[jax.ShapeDtypeStruct((t, w), d) for w, d in out_defs]
        + [jax.ShapeDtypeStruct((8, w), F32) for w in sum_widths],
        grid=(t // bm,),
        in_specs=[win(w, cb) for _, w, cb in ins] + [pl.BlockSpec(c.shape, lambda i, nd=c.ndim: (0,) * nd) for c in consts],
        out_specs=[win(w, 0) for w, _ in out_defs] + [pl.BlockSpec((8, w), lambda i: (0, 0)) for w in sum_widths],
        compiler_params=_cparams(("arbitrary",) if ns else ("parallel",)),
        name=name,
    )(*[a for a, _, _ in ins], *consts)
    return res[:no], [jnp.sum(s, axis=0) for s in res[no:]]


def _colsum8(x):
    bm, w = x.shape
    return jnp.sum(x.reshape(bm // 8, 8, w), axis=0)


def _tri(n, upper=False):
    r = lax.broadcasted_iota(jnp.int32, (n, n), 0)
    c = lax.broadcasted_iota(jnp.int32, (n, n), 1)
    return (c >= r) if upper else (c <= r)


def _exact_tri_matmul(tri_bf16, x):
    x0 = x.astype(BF16)
    r1 = x - x0.astype(F32)
    x1 = r1.astype(BF16)
    x2 = (r1 - x1.astype(F32)).astype(BF16)
    w = x.shape[1]
    y = jnp.dot(tri_bf16, jnp.concatenate([x0, x1, x2], axis=1), preferred_element_type=F32)
    return y[:, :w] + y[:, w:2 * w] + y[:, 2 * w:]


def _dot_nt(a, b):
    return lax.dot_general(a, b, (((1,), (1,)), ((), ())), preferred_element_type=F32)


def _dot_tn(a, b):
    return lax.dot_general(a, b, (((0,), (0,)), ((), ())), preferred_element_type=F32)


def _dot(a, b):
    return jnp.dot(a, b, preferred_element_type=F32)


def _hg_gates(hq, hf, lb):
    sq = _sigmoid(hq)
    q = hq * sq
    sg = _sigmoid(hf)
    f = lb + (1.0 - lb) * sg
    return q, sq, sg, f


def _hg_intra(q, kk, g):
    c = q.shape[0]
    rows = lax.broadcasted_iota(jnp.int32, (c, 1), 0)
    a_rows, qts, kts, eqs, eks = [], [], [], [], []
    for i in range(c // HG_SUB):
        lo = i * HG_SUB
        ref = g[lo - 1:lo, :] if i else jnp.zeros_like(g[0:1, :])
        eq = jnp.exp(g[lo:lo + HG_SUB, :] - ref)
        ek = jnp.exp(jnp.where(rows < lo + HG_SUB, ref - g, 0.0))
        qt = q[lo:lo + HG_SUB, :] * eq
        kt = kk * ek
        a = _dot_nt(qt.astype(BF16), kt.astype(BF16))
        tpos = lo + lax.broadcasted_iota(jnp.int32, (HG_SUB, c), 0)
        spos = lax.broadcasted_iota(jnp.int32, (HG_SUB, c), 1)
        a_rows.append(jnp.where(spos <= tpos, a, 0.0))
        qts.append(qt), kts.append(kt), eqs.append(eq), eks.append(ek)
    return jnp.concatenate(a_rows, axis=0), qts, kts, eqs, eks


def _hgrn_fwd(zh, lb3, *, tb=512):
    t = zh.shape[0]
    nh = lb3.shape[0]
    c = HG_CHUNK
    tb = min(tb, t)
    nchunk = tb // c

    def body(hq_ref, hf_ref, hi_ref, lb_ref, o_ref, st_ref, state):
        @pl.when(pl.program_id(1) == 0)
        def _():
            state[...] = jnp.zeros_like(state)

        lb = lb_ref[0]
        tril = _tri(c).astype(BF16)

        def chunk(ci, carry):
            sl = pl.ds(pl.multiple_of(ci * c, c), c)
            q, _, _, f = _hg_gates(hq_ref[sl, :], hf_ref[sl, :], lb)
            v = hi_ref[sl, :]
            kk = 1.0 - f
            g = _exact_tri_matmul(tril, jnp.log(f))
            a, _, _, _, _ = _hg_intra(q, kk, g)
            st = state[...]
            st_ref[0, ci] = st
            vb = v.astype(BF16)
            o = _dot(a.astype(BF16), vb) + _dot_nt((q * jnp.exp(g)).astype(BF16), st.astype(BF16))
            o_ref[sl, :] = o
            glast = g[c - 1:c, :]
            kg = kk * jnp.exp(glast - g)
            state[...] = st * jnp.exp(glast) + _dot_tn(vb, kg.astype(BF16))
            return carry

        lax.fori_loop(0, nchunk, chunk, 0)

    def col(cb):
        return pl.BlockSpec((tb, HG_DK), lambda h, i: (i, cb * nh + h))

    return pl.pallas_call(
        body,
        out_shape=[jax.ShapeDtypeStruct((t, nh * HG_DK), F32), jax.ShapeDtypeStruct((nh, t // c, HG_DK, HG_DK), F32)],
        grid=(nh, t // tb),
        in_specs=[col(0), col(1), col(2), pl.BlockSpec((1, 1, HG_DK), lambda h, i: (h, 0, 0))],
        out_specs=[pl.BlockSpec((tb, HG_DK), lambda h, i: (i, h)),
                   pl.BlockSpec((1, nchunk, HG_DK, HG_DK), lambda h, i: (h, i, 0, 0))],
        scratch_shapes=[pltpu.VMEM((HG_DK, HG_DK), F32)],
        compiler_params=_cparams(("parallel", "arbitrary")),
        name="hgrn_fwd",
    )(zh, zh, zh, lb3)


def _hgrn_bwd(zh, lb3, states, d_o, *, tb=512):
    t = zh.shape[0]
    nh = lb3.shape[0]
    c = HG_CHUNK
    tb = min(tb, t)
    nchunk = tb // c
    nblk = t // tb

    def body(hq_ref, hf_ref, hi_ref, lb_ref, st_ref, do_ref, dq_ref, df_ref, dv_ref, dlb_ref, dstate):
        @pl.when(pl.program_id(1) == 0)
        def _():
            dstate[...] = jnp.zeros_like(dstate)
            dlb_ref[...] = jnp.zeros_like(dlb_ref)

        lb = lb_ref[0]
        tril = _tri(c).astype(BF16)
        triu = _tri(c, upper=True).astype(BF16)
        last_row = lax.broadcasted_iota(jnp.int32, (c, 1), 0) == c - 1

        def chunk(j, carry):
            ci = nchunk - 1 - j
            sl = pl.ds(pl.multiple_of(ci * c, c), c)
            hq, hf = hq_ref[sl, :], hf_ref[sl, :]
            q, sq, sg, f = _hg_gates(hq, hf, lb)
            v = hi_ref[sl, :]
            kk = 1.0 - f
            g = _exact_tri_matmul(tril, jnp.log(f))
            a, qts, kts, eqs, eks = _hg_intra(q, kk, g)
            st = st_ref[0, ci]
            dst = dstate[...]
            do = do_ref[sl, :]
            dob, vb = do.astype(BF16), v.astype(BF16)
            glast = g[c - 1:c, :]
            eg = jnp.exp(g)
            egl = jnp.exp(glast - g)
            qg = q * eg
            kg = kk * egl
            dv = _dot_tn(a.astype(BF16), dob) + _dot_nt(kg.astype(BF16), dst.astype(BF16))
            da = jnp.where(_tri(c), _dot_nt(dob, vb), 0.0).astype(BF16)
            dq_parts, dgq_parts = [], []
            dk = jnp.zeros_like(kk)
            dgk = jnp.zeros_like(kk)
            for i in range(c // HG_SUB):
                da_i = da[i * HG_SUB:(i + 1) * HG_SUB, :]
                ktb, qtb = kts[i].astype(BF16), qts[i].astype(BF16)
                xi = _dot(da_i, ktb)
                yi = _dot_tn(da_i, qtb)
                dq_parts.append(xi * eqs[i])
                dk = dk + yi * eks[i]
                dgq_parts.append(xi * qtb.astype(F32))
                dgk = dgk + yi * ktb.astype(F32)
            dq_inter = _dot(dob, st.astype(BF16)) * eg
            dq = jnp.concatenate(dq_parts, axis=0) + dq_inter
            dk_state = _dot(vb, dst.astype(BF16)) * egl
            dk = dk + dk_state
            dg = jnp.concatenate(dgq_parts, axis=0) - dgk + q * dq_inter - kk * dk_state
            dgl = jnp.sum(kk * dk_state, axis=0, keepdims=True) + jnp.exp(glast) * jnp.sum(st * dst, axis=0, keepdims=True)
            dg = dg + jnp.where(last_row, dgl, 0.0)
            dlogf = _exact_tri_matmul(triu, dg)
            dfv = dlogf / f - dk
            dq_ref[sl, :] = (dq * (sq * (1.0 + hq * (1.0 - sq)))).astype(dq_ref.dtype)
            df_ref[sl, :] = (dfv * (1.0 - lb) * sg * (1.0 - sg)).astype(df_ref.dtype)
            dv_ref[sl, :] = dv.astype(dv_ref.dtype)
            dlb_ref[0] += jnp.sum(dfv * (1.0 - sg), axis=0, keepdims=True)
            dstate[...] = dst * jnp.exp(glast) + _dot_tn(dob, qg.astype(BF16))
            return carry

        lax.fori_loop(0, nchunk, chunk, 0)

    def col(cb):
        return pl.BlockSpec((tb, HG_DK), lambda h, i: (nblk - 1 - i, cb * nh + h))

    ocol = pl.BlockSpec((tb, HG_DK), lambda h, i: (nblk - 1 - i, h))
    w = nh * HG_DK
    dq, df, dv, dlb = pl.pallas_call(
        body,
        out_shape=[jax.ShapeDtypeStruct((t, w), BF16)] * 3 + [jax.ShapeDtypeStruct((nh, 1, HG_DK), F32)],
        grid=(nh, nblk),
        in_specs=[col(0), col(1), col(2), pl.BlockSpec((1, 1, HG_DK), lambda h, i: (h, 0, 0)),
                  pl.BlockSpec((1, nchunk, HG_DK, HG_DK), lambda h, i: (h, nblk - 1 - i, 0, 0)), ocol],
        out_specs=[ocol, ocol, ocol, pl.BlockSpec((1, 1, HG_DK), lambda h, i: (h, 0, 0))],
        scratch_shapes=[pltpu.VMEM((HG_DK, HG_DK), F32)],
        compiler_params=_cparams(("parallel", "arbitrary")),
        name="hgrn_bwd",
    )(zh, zh, zh, lb3, states, d_o)
    return dq, df, dv, dlb.reshape(w)


NEG = -1e30
ATT_GW = ATT_HEADS * ATT_DH


def _att_scores(q, kp, kc, has_prev):
    scale = ATT_DH ** -0.5
    i = lax.broadcasted_iota(jnp.int32, (ATT_BLK, ATT_BLK), 0)
    j = lax.broadcasted_iota(jnp.int32, (ATT_BLK, ATT_BLK), 1)
    s_p = jnp.where(jnp.logical_and(j >= i, has_prev), _dot_nt(q, kp) * scale, NEG)
    s_c = jnp.where(j <= i, _dot_nt(q, kc) * scale, NEG)
    return s_p, s_c


def _att_views(arrs, d):
    return [a.reshape(a.shape[0] // d, d * a.shape[1]) for a in arrs]


def _attn_fwd(qb, kb, vb, g):
    t = qb.shape[0]
    d = ATT_PATTERNS[g][1]
    nb = t // d // ATT_BLK
    q2, k2, v2 = _att_views([qb, kb, vb], d)

    def body(q_ref, kc_ref, kp_ref, vc_ref, vp_ref, o_ref, l_ref):
        has_prev = pl.program_id(1) > 0
        for h in range(ATT_HEADS):
            hs = slice(h * ATT_DH, (h + 1) * ATT_DH)
            s_p, s_c = _att_scores(q_ref[:, hs], kp_ref[:, hs], kc_ref[:, hs], has_prev)
            m = jnp.maximum(jnp.max(s_p, axis=1, keepdims=True), jnp.max(s_c, axis=1, keepdims=True))
            p_p, p_c = jnp.exp(s_p - m), jnp.exp(s_c - m)
            l = jnp.sum(p_p, axis=1, keepdims=True) + jnp.sum(p_c, axis=1, keepdims=True)
            o = _dot(p_p.astype(BF16), vp_ref[:, hs]) + _dot(p_c.astype(BF16), vc_ref[:, hs])
            o_ref[:, hs] = o / l
            l_ref[:, hs] = jnp.broadcast_to(m + jnp.log(l), (ATT_BLK, ATT_DH))

    cur = pl.BlockSpec((ATT_BLK, ATT_GW), lambda r, n: (n, r * ATT_GROUPS + g))
    prev = pl.BlockSpec((ATT_BLK, ATT_GW), lambda r, n: (jnp.maximum(n - 1, 0), r * ATT_GROUPS + g))
    out = pl.BlockSpec((ATT_BLK, ATT_GW), lambda r, n: (n, r))
    o, lse = pl.pallas_call(
        body,
        out_shape=[jax.ShapeDtypeStruct((t // d, d * ATT_GW), F32)] * 2,
        grid=(d, nb),
        in_specs=[cur, cur, prev, cur, prev],
        out_specs=[out, out],
        compiler_params=_cparams(("parallel", "arbitrary")),
        name=f"attn_fwd_g{g}",
    )(q2, k2, k2, v2, v2)
    return o.reshape(t, ATT_GW), lse.reshape(t, ATT_GW)


def _attn_bwd(qb, kb, vb, o, lse, d_o, d_lse, g):
    t = qb.shape[0]
    d = ATT_PATTERNS[g][1]
    nb = t // d // ATT_BLK
    q2, k2, v2 = _att_views([qb, kb, vb], d)
    o2, l2, do2, dl2 = _att_views([o, lse, d_o, d_lse], d)

    def body(q_ref, kc_ref, kp_ref, vc_ref, vp_ref, o_ref, l_ref, do_ref, dl_ref, dq_ref, dk_ref, dv_ref, ck, cv):
        n = pl.program_id(1)
        active = n < nb

        @pl.when(n == 0)
        def _():
            ck[...] = jnp.zeros_like(ck)
            cv[...] = jnp.zeros_like(cv)

        @pl.when(jnp.logical_not(active))
        def _():
            dk_ref[...] = ck[...]
            dv_ref[...] = cv[...]

        @pl.when(active)
        def _():
            has_prev = n > 0
            for h in range(ATT_HEADS):
                hs = slice(h * ATT_DH, (h + 1) * ATT_DH)
                q, kp, kc, vp, vc = q_ref[:, hs], kp_ref[:, hs], kc_ref[:, hs], vp_ref[:, hs], vc_ref[:, hs]
                s_p, s_c = _att_scores(q, kp, kc, has_prev)
                lse_h = l_ref[:, hs][:, 0:1]
                p_p, p_c = jnp.exp(s_p - lse_h), jnp.exp(s_c - lse_h)
                do = do_ref[:, hs]
                delta = jnp.sum(do * o_ref[:, hs] - dl_ref[:, hs], axis=1, keepdims=True)
                dob = do.astype(BF16)
                scale = ATT_DH ** -0.5
                ds_p = (p_p * (_dot_nt(dob, vp) - delta) * scale).astype(BF16)
                ds_c = (p_c * (_dot_nt(dob, vc) - delta) * scale).astype(BF16)
                dq_ref[:, hs] = _dot(ds_p, kp) + _dot(ds_c, kc)
                dk_ref[:, hs] = ck[:, hs] + _dot_tn(ds_p, q)
                dv_ref[:, hs] = cv[:, hs] + _dot_tn(p_p.astype(BF16), dob)
                ck[:, hs] = _dot_tn(ds_c, q)
                cv[:, hs] = _dot_tn(p_c.astype(BF16), dob)

    def qn(n):
        return jnp.minimum(n, nb - 1)

    cur = pl.BlockSpec((ATT_BLK, ATT_GW), lambda r, n: (qn(n), r * ATT_GROUPS + g))
    prev = pl.BlockSpec((ATT_BLK, ATT_GW), lambda r, n: (jnp.maximum(qn(n) - 1, 0), r * ATT_GROUPS + g))
    own = pl.BlockSpec((ATT_BLK, ATT_GW), lambda r, n: (qn(n), r))
    behind = pl.BlockSpec((ATT_BLK, ATT_GW), lambda r, n: (jnp.maximum(n - 1, 0), r))
    shp = jax.ShapeDtypeStruct((t // d, d * ATT_GW), F32)
    dq, dk, dv = pl.pallas_call(
        body,
        out_shape=[shp, shp, shp],
        grid=(d, nb + 1),
        in_specs=[cur, cur, prev, cur, prev, own, own, own, own],
        out_specs=[own, behind, behind],
        scratch_shapes=[pltpu.VMEM((ATT_BLK, ATT_GW), F32), pltpu.VMEM((ATT_BLK, ATT_GW), F32)],
        compiler_params=_cparams(("parallel", "arbitrary")),
        name=f"attn_bwd_g{g}",
    )(q2, k2, k2, v2, v2, o2, l2, do2, dl2)
    return dq.reshape(t, ATT_GW), dk.reshape(t, ATT_GW), dv.reshape(t, ATT_GW)


def _rms_parts(x, width):
    outs = []
    for lo in range(0, x.shape[1], width):
        xs = x[:, lo:lo + width]
        r = lax.rsqrt(jnp.mean(xs * xs, axis=1, keepdims=True) + EPS)
        outs.append((xs * r, r))
    return outs


def _rms_bwd_part(xh, r, dxh):
    return r * (dxh - xh * jnp.mean(dxh * xh, axis=1, keepdims=True))


def _norm_fwd(x, gain):
    d = x.shape[1]

    def fn(ins, consts):
        (xh, _), = _rms_parts(ins[0], d)
        return [xh * consts[0]], []

    (h,), _ = _rowwise(fn, [(x, d, 0)], [gain.reshape(1, d)], [(d, BF16)], [], bm=512, name="norm_fwd")
    return h


def _norm_bwd(x, gain, dh, dres):
    d = x.shape[1]

    def fn(ins, consts):
        (xh, r), = _rms_parts(ins[0], d)
        dx = ins[2] + _rms_bwd_part(xh, r, ins[1] * consts[0])
        return [dx], [_colsum8(ins[1] * xh)]

    (dx,), (dg,) = _rowwise(fn, [(x, d, 0), (dh, d, 0), (dres, d, 0)], [gain.reshape(1, d)], [(d, F32)], [d],
                            bm=512, name="norm_bwd")
    return dx, dg


def _rot_sign():
    lane = lax.broadcasted_iota(jnp.int32, (1, ATT_DH), 1)
    return jnp.where(lane < ATT_DH // 2, -1.0, 1.0).astype(F32)


def _rope(y, cos, sin):
    return y * cos + pltpu.roll(y, ATT_DH // 2, axis=1) * _rot_sign() * sin


def _rope_t(dy, cos, sin):
    return dy * cos - pltpu.roll(dy * sin, ATT_DH // 2, axis=1) * _rot_sign()


def _qk_prep(zq, zk, zv, qn, kn, cos, sin):
    w = zq.shape[1]

    def fn(ins, consts):
        cs, sn = ins[3], ins[4]
        outs = []
        for z, gain in ((ins[0], consts[0]), (ins[1], consts[1])):
            parts = _rms_parts(z, ATT_DH)
            outs.append(jnp.concatenate(
                [_rope(xh * gain[:, i * ATT_DH:(i + 1) * ATT_DH], cs, sn) for i, (xh, _) in enumerate(parts)], axis=1))
        return outs + [ins[2]], []

    (qb, kb, vb), _ = _rowwise(fn, [(zq, w, 0), (zk, w, 0), (zv, w, 0), (cos, ATT_DH, 0), (sin, ATT_DH, 0)], [qn, kn],
                               [(w, BF16)] * 3, [], bm=256, name="qk_prep")
    return qb, kb, vb


def _qk_prep_bwd(zq, zk, dq_g, dk_g, dv_g, qn, kn, cos, sin):
    w = zq.shape[1]

    def fn(ins, consts):
        cs, sn = ins[2], ins[3]
        outs, sums = [], []
        for z, gain, dparts in ((ins[0], consts[0], ins[4:7]), (ins[1], consts[1], ins[7:10])):
            dout = jnp.concatenate(dparts, axis=1)
            dz, dgain = [], []
            for i, (xh, r) in enumerate(_rms_parts(z, ATT_DH)):
                hs = slice(i * ATT_DH, (i + 1) * ATT_DH)
                dy = _rope_t(dout[:, hs], cs, sn)
                dgain.append(_colsum8(dy * xh))
                dz.append(_rms_bwd_part(xh, r, dy * gain[:, hs]))
            outs.append(jnp.concatenate(dz, axis=1))
            sums.append(jnp.concatenate(dgain, axis=1))
        outs.append(jnp.concatenate(ins[10:13], axis=1))
        return outs, sums

    gw = ATT_GW
    ins = [(zq, w, 0), (zk, w, 0), (cos, ATT_DH, 0), (sin, ATT_DH, 0)]
    ins += [(a, gw, 0) for a in dq_g] + [(a, gw, 0) for a in dk_g] + [(a, gw, 0) for a in dv_g]
    (dzq, dzk, dzv), (dqn, dkn) = _rowwise(fn, ins, [qn, kn], [(w, BF16)] * 3, [w, w], bm=256, name="qk_prep_bwd")
    return dzq, dzk, dzv, dqn, dkn


def _post_a(o_raw, zh, gout):
    w = o_raw.shape[1]

    def fn(ins, consts):
        oh = jnp.concatenate([xh for xh, _ in _rms_parts(ins[0], HG_DK)], axis=1)
        hg = ins[1]
        return [oh * consts[0] * (hg * _sigmoid(hg))], []

    (y,), _ = _rowwise(fn, [(o_raw, w, 0), (zh, w, 3)], [gout.reshape(1, w)], [(w, BF16)], [], bm=512, name="post_a")
    return y


def _post_a_bwd(o_raw, zh, gout, dy):
    w = o_raw.shape[1]

    def fn(ins, consts):
        parts = _rms_parts(ins[0], HG_DK)
        oh = jnp.concatenate([xh for xh, _ in parts], axis=1)
        hg, dyv, gain = ins[1], ins[2], consts[0]
        sg = _sigmoid(hg)
        s = hg * sg
        doh = dyv * gain * s
        do = jnp.concatenate([_rms_bwd_part(xh, r, doh[:, i * HG_DK:(i + 1) * HG_DK]) for i, (xh, r) in enumerate(parts)], axis=1)
        dhg = dyv * oh * gain * (sg * (1.0 + hg * (1.0 - sg)))
        return [do, dhg], [_colsum8(dyv * oh * s)]

    (do, dhg), (dgain,) = _rowwise(fn, [(o_raw, w, 0), (zh, w, 3), (dy, w, 0)], [gout.reshape(1, w)],
                                   [(w, F32), (w, BF16)], [w], bm=512, name="post_a_bwd")
    return do, dhg, dgain


def _merge_alpha(lses):
    m = jnp.maximum(jnp.maximum(lses[0], lses[1]), lses[2])
    e = [jnp.exp(l - m) for l in lses]
    inv = 1.0 / (e[0] + e[1] + e[2])
    return [x * inv for x in e]


def _merge_b(o_g, lse_g):
    def fn(ins, consts):
        al = _merge_alpha(ins[3:6])
        return [al[0] * ins[0] + al[1] * ins[1] + al[2] * ins[2]], []

    (y,), _ = _rowwise(fn, [(a, ATT_GW, 0) for a in list(o_g) + list(lse_g)], [], [(ATT_GW, BF16)], [], bm=512, name="merge_b")
    return y


def _merge_b_bwd(o_g, lse_g, dy):
    def fn(ins, consts):
        al = _merge_alpha(ins[3:6])
        dyv = ins[6]
        dal = [dyv * ins[i] for i in range(3)]
        tot = al[0] * dal[0] + al[1] * dal[1] + al[2] * dal[2]
        return [al[i] * dyv for i in range(3)] + [al[i] * (dal[i] - tot) for i in range(3)], []

    outs, _ = _rowwise(fn, [(a, ATT_GW, 0) for a in list(o_g) + list(lse_g) + [dy]], [], [(ATT_GW, F32)] * 6, [],
                       bm=512, name="merge_b_bwd")
    return outs[:3], outs[3:]


def _loss_head(y, target):
    d = y.shape[1]

    def fn(ins, consts):
        e = ins[0] - ins[1]
        return [e * (1.0 / d)], [_colsum8(e * e)]

    (dy,), (sq,) = _rowwise(fn, [(y, d, 0), (target, d, 0)], [], [(d, F32)], [d], bm=512, name="loss_head")
    return 0.5 * jnp.sum(sq) / d, dy


def _silu_grad(a):
    s = _sigmoid(a)
    return s * (1.0 + a * (1.0 - s))


def _ffn_fwd(x, gain, wt, wo, tag):
    t, d = x.shape
    f = wo.shape[0]
    h = _norm_fwd(x, gain)

    def act(accs, ex):
        a, b = accs
        return (a * _sigmoid(a) * b, a, b)

    u, a, b = _mm([h], [wt, wt], [(0, 0, 0), (0, 1, 1)], 2, act, [BF16, BF16, BF16], m=t, n=f, k=d, tb=True,
                  bm=1024, bn=256, bk=d, b_off=[(0, 0), (f // min(256, f), 0)], name=f"ffn_in_{tag}")
    (y,) = _mm([u], [wo], [(0, 0, 0)], 1, lambda accs, ex: (ex[0] + 0.5 * accs[0],), [F32], m=t, n=d, k=f,
               bm=512, bn=d, bk=f, extras=[x], name=f"ffn_out_{tag}")
    return y, (x, h, u, a, b)


def _ffn_bwd(dy, saved, gain, wt, wo, tag):
    x, h, u, a, b = saved
    t, d = x.shape
    f = wo.shape[0]
    dyb = dy.astype(BF16)

    def dact(accs, ex):
        du = 0.5 * accs[0]
        av, bv = ex[0].astype(F32), ex[1].astype(F32)
        return (du * bv * _silu_grad(av), du * av * _sigmoid(av))

    da, db = _mm([dyb], [wo], [(0, 0, 0)], 1, dact, [BF16, BF16], m=t, n=f, k=d, tb=True, bm=1024, bn=256, bk=d,
                 extras=[a, b], name=f"ffn_dact_{tag}")
    (dwo,) = _mm([u], [dyb], [(0, 0, 0)], 1, lambda accs, ex: (0.5 * accs[0],), [BF16], m=f, n=d, k=t, ta=True,
                 bm=1408, bn=d, bk=1024, name=f"ffn_dwo_{tag}")
    (dh,) = _mm([da, db], [wt, wt], [(0, 0, 0), (1, 1, 0)], 1, _first, [F32], m=t, n=d, k=f, bm=512, bn=d, bk=f,
                b_off=[(0, 0), (0, 1)], name=f"ffn_dh_{tag}")
    dwt = [_mm([g], [h], [(0, 0, 0)], 1, _first, [BF16], m=f, n=d, k=t, ta=True, bm=1408, bn=d, bk=1024,
               name=f"ffn_dwt{i}_{tag}")[0] for i, g in enumerate((da, db))]
    dx, dgain = _norm_bwd(x, gain, dh, dy)
    return dx, dgain, jnp.concatenate(dwt, axis=0), dwo


Z_SPLITS = (("h", 4096), ("q", 1536), ("k", 1536), ("v", 1536), ("g", 2048))


def _mix_fwd(x, p, cos, sin):
    t, d = x.shape
    hm = _norm_fwd(x, p["gm"])
    z, off = {}, 0
    for nm, width in Z_SPLITS:
        (z[nm],) = _mm([hm], [p["wint"]], [(0, 0, 0)], 1, _first, [F32], m=t, n=width, k=d, tb=True, bm=1024, bn=512, bk=d,
                       b_off=[(off // 512, 0)], name=f"mix_in_{nm}")
        off += width
    o_raw, states = _hgrn_fwd(z["h"], p["lb3"])
    qb, kb, vb = _qk_prep(z["q"], z["k"], z["v"], p["qn"], p["kn"], cos, sin)
    o_g, lse_g = zip(*[_attn_fwd(qb, kb, vb, g) for g in range(ATT_GROUPS)])
    oa = _post_a(o_raw, z["h"], p["gout"])
    ob = _merge_b(o_g, lse_g)
    (ya,) = _mm([oa], [p["wa"]], [(0, 0, 0)], 1, _first, [F32], m=t, n=d, k=oa.shape[1], bm=1024, bn=d, bk=oa.shape[1],
                name="branch_a")

    def gate(accs, ex):
        return (_sigmoid(ex[0]) * ex[2] + _sigmoid(ex[1]) * accs[0], accs[0])

    merged, yb = _mm([ob], [p["wbt"]], [(0, 0, 0)], 1, gate, [BF16, F32], m=t, n=d, k=ATT_GW, tb=True, bm=512, bn=d,
                     bk=ATT_GW, extras=[z["g"], z["g"], ya], e_off=[0, 1, 0], name="branch_b_gate")
    (y,) = _mm([merged], [p["wo"]], [(0, 0, 0)], 1, lambda accs, ex: (ex[0] + accs[0],), [F32], m=t, n=d, k=d,
               bm=1024, bn=d, bk=d, extras=[x], name="mix_out")
    return y, (x, hm, z, o_raw, states, qb, kb, vb, o_g, lse_g, oa, ob, ya, yb, merged)


def _mix_bwd(dy, saved, p, cos, sin):
    x, hm, z, o_raw, states, qb, kb, vb, o_g, lse_g, oa, ob, ya, yb, merged = saved
    t, d = x.shape
    w = oa.shape[1]
    dyb = dy.astype(BF16)

    def dgate(accs, ex):
        dm = accs[0]
        sa, sb = _sigmoid(ex[0]), _sigmoid(ex[1])
        return (sa * dm, sb * dm, dm * ex[2] * sa * (1.0 - sa), dm * ex[3] * sb * (1.0 - sb))

    dya, dyb_, dga, dgb = _mm([dyb], [p["wo"]], [(0, 0, 0)], 1, dgate, [BF16] * 4, m=t, n=d, k=d, tb=True, bm=512, bn=d,
                              bk=d, extras=[z["g"], z["g"], ya, yb], e_off=[0, 1, 0, 0], name="mix_out_bwd")
    (dwo,) = _mm([merged], [dyb], [(0, 0, 0)], 1, _first, [BF16], m=d, n=d, k=t, ta=True, bm=d, bn=d, bk=1024, name="mix_dwo")
    (doa,) = _mm([dya], [p["wa"]], [(0, 0, 0)], 1, _first, [F32], m=t, n=w, k=d, tb=True, bm=1024, bn=w, bk=d, name="branch_a_bwd")
    (dwa,) = _mm([oa], [dya], [(0, 0, 0)], 1, _first, [BF16], m=w, n=d, k=t, ta=True, bm=w, bn=d, bk=1024, name="branch_a_dw")
    (dob,) = _mm([dyb_], [p["wbt"]], [(0, 0, 0)], 1, _first, [F32], m=t, n=ATT_GW, k=d, bm=1024, bn=ATT_GW, bk=d,
                 name="branch_b_bwd")
    (dwbt,) = _mm([dyb_], [ob], [(0, 0, 0)], 1, _first, [BF16], m=d, n=ATT_GW, k=t, ta=True, bm=d, bn=ATT_GW, bk=1024,
                  name="branch_b_dw")
    do_raw, dhg, dgout = _post_a_bwd(o_raw, z["h"], p["gout"], doa)
    do_g, dlse_g = _merge_b_bwd(o_g, lse_g, dob)
    dq_g, dk_g, dv_g = zip(*[_attn_bwd(qb, kb, vb, o_g[g], lse_g[g], do_g[g], dlse_g[g], g) for g in range(ATT_GROUPS)])
    dzq, dzk, dzv, dqn, dkn = _qk_prep_bwd(z["q"], z["k"], dq_g, dk_g, dv_g, p["qn"], p["kn"], cos, sin)
    dhq, dhf, dhi, lbsum = _hgrn_bwd(z["h"], p["lb3"], states, do_raw)
    dz = jnp.concatenate([dhq, dhf, dhi, dhg, dzq, dzk, dzv, dga, dgb], axis=1)
    pw = dz.shape[1]
    (dhm,) = _mm([dz], [p["wint"]], [(0, 0, 0)], 1, _first, [F32], m=t, n=d, k=pw, bm=1024, bn=d, bk=1536, name="mix_in_bwd")
    (dwint,) = _mm([dz], [hm], [(0, 0, 0)], 1, _first, [BF16], m=pw, n=d, k=t, ta=True, bm=1536, bn=d, bk=1024, name="mix_in_dw")
    dx, dgm = _norm_bwd(x, p["gm"], dhm, dy)
    return dx, dict(gm=dgm, wint=dwint, lbsum=lbsum, gout=dgout, qn=dqn, kn=dkn, wa=dwa, wbt=dwbt, wo=dwo)


def _rope_tables(t):
    pos = jnp.arange(t, dtype=F32)
    inv = ROPE_THETA ** (-jnp.arange(0, ATT_DH, 2, dtype=F32) / ATT_DH)
    ang = pos[:, None] * inv[None, :]
    ang = jnp.concatenate([ang, ang], axis=-1)
    return jnp.cos(ang), jnp.sin(ang)


def _lower_bounds(logits):
    lb = jnp.cumsum(jax.nn.softmax(logits, axis=0), axis=0)
    return lb - lb[0:1]


def _head_gain(g):
    return jnp.tile(g[:, None, :], (1, ATT_HEADS, 1)).reshape(1, ATT_GROUPS * ATT_GW)


def _local_step(x, target, small, big):
    t = x.shape[0]
    depth = small["ffn1_norm"].shape[0]
    cos, sin = _rope_tables(t)
    lb_all = _lower_bounds(small["hgrn_lb_logits"])
    saved = []
    for l in range(depth):
        p = dict(gm=small["mix_norm"][l], wint=big["wint"][l], lb3=lb_all[l].reshape(-1, 1, HG_DK),
                 gout=small["hgrn_out_norm"][l], qn=_head_gain(small["attn_q_norm"][l]),
                 kn=_head_gain(small["attn_k_norm"][l]), wa=big["wa"][l], wbt=big["wbt"][l], wo=big["wo"][l])
        x, s1 = _ffn_fwd(x, small["ffn1_norm"][l], big["w1t"][l], big["w1o"][l], "1")
        x, sm = _mix_fwd(x, p, cos, sin)
        x, s2 = _ffn_fwd(x, small["ffn2_norm"][l], big["w2t"][l], big["w2o"][l], "2")
        saved.append((p, s1, sm, s2))
    loss, dx = _loss_head(x, target)
    gbig = {k: [None] * depth for k in big}
    gsmall = {k: [None] * depth for k in ("ffn1_norm", "mix_norm", "lbsum", "hgrn_out_norm", "attn_q_norm", "attn_k_norm", "ffn2_norm")}
    for l in reversed(range(depth)):
        p, s1, sm, s2 = saved[l]
        dx, gsmall["ffn2_norm"][l], gbig["w2t"][l], gbig["w2o"][l] = _ffn_bwd(dx, s2, small["ffn2_norm"][l], big["w2t"][l], big["w2o"][l], "2")
        dx, gm = _mix_bwd(dx, sm, p, cos, sin)
        dx, gsmall["ffn1_norm"][l], gbig["w1t"][l], gbig["w1o"][l] = _ffn_bwd(dx, s1, small["ffn1_norm"][l], big["w1t"][l], big["w1o"][l], "1")
        for k in ("wint", "wa", "wbt", "wo"):
            gbig[k][l] = gm[k]
        gsmall["mix_norm"][l], gsmall["lbsum"][l], gsmall["hgrn_out_norm"][l] = gm["gm"], gm["lbsum"], gm["gout"]
        for k, src in (("attn_q_norm", "qn"), ("attn_k_norm", "kn")):
            gsmall[k][l] = jnp.sum(gm[src].reshape(ATT_GROUPS, ATT_HEADS, ATT_DH), axis=1)
    return loss, dx, gbig, {k: jnp.stack(v) for k, v in gsmall.items()}


def _exchange(srcs, modes, name):
    n = len(srcs)

    def body(*refs):
        src, dst = refs[:n], refs[n:2 * n]
        send_sems, recv_sems, local_sems = refs[2 * n:]
        x, y, c = lax.axis_index("x"), lax.axis_index("y"), lax.axis_index("c")
        me = 4 * x + 2 * y + c

        def block(k, j):
            return src[k] if modes[k] == "gather" else src[k].at[:, j]

        local = [pltpu.make_async_copy(block(k, me), dst[k].at[:, me], local_sems.at[k]) for k in range(n)]
        for cp in local:
            cp.start()
        remote = []
        for p in range(1, N_DEV):
            peer = (1 - x if p & 4 else x, 1 - y if p & 2 else y, 1 - c if p & 1 else c)
            peer_id = jnp.bitwise_xor(me, p)
            for k in range(n):
                cp = pltpu.make_async_remote_copy(
                    src_ref=block(k, peer_id), dst_ref=dst[k].at[:, me], send_sem=send_sems.at[k, p - 1],
                    recv_sem=recv_sems.at[k, p - 1], device_id=peer, device_id_type=MESH)
                cp.start()
                remote.append((cp, pltpu.make_async_remote_copy(
                    src_ref=block(k, peer_id), dst_ref=dst[k].at[:, peer_id], send_sem=send_sems.at[k, p - 1],
                    recv_sem=recv_sems.at[k, p - 1], device_id=peer, device_id_type=MESH)))
        for sent, arriving in remote:
            arriving.wait_recv()
        for sent, arriving in remote:
            sent.wait_send()
        for cp in local:
            cp.wait()

    def dst_shape(a, mode):
        return (a.shape[0], N_DEV) + a.shape[1:] if mode == "gather" else a.shape

    anyspec = pl.BlockSpec(memory_space=pl.ANY)
    return pl.pallas_call(
        body,
        out_shape=[jax.ShapeDtypeStruct(dst_shape(a, m), a.dtype) for a, m in zip(srcs, modes)],
        in_specs=[anyspec] * n,
        out_specs=[anyspec] * n,
        scratch_shapes=[pltpu.SemaphoreType.DMA((n, N_DEV - 1)), pltpu.SemaphoreType.DMA((n, N_DEV - 1)),
                        pltpu.SemaphoreType.DMA((n,))],
        compiler_params=pltpu.CompilerParams(has_side_effects=True),
        name=name,
    )(*srcs)


def _sum_slots(land):
    g, _, r, c = land.shape
    br = r // 2 if (r % 32 == 0 and r >= 256) else r

    def body(l_ref, o_ref):
        acc = l_ref[0, 0].astype(F32)
        for j in range(1, N_DEV):
            acc = acc + l_ref[0, j].astype(F32)
        o_ref[0] = acc

    return pl.pallas_call(
        body,
        out_shape=jax.ShapeDtypeStruct((g, r, c), F32),
        grid=(g, r // br),
        in_specs=[pl.BlockSpec((1, N_DEV, br, c), lambda i, j: (i, 0, j, 0))],
        out_specs=pl.BlockSpec((1, br, c), lambda i, j: (i, j, 0)),
        compiler_params=_cparams(("parallel", "parallel")),
        name="sum_slots",
    )(land)


def _adamw(w, g, m, v):
    shape = w.shape
    cols = shape[-1]
    rows = int(np.prod(shape[:-1]))
    bm = max(b for b in range(8, 257, 8) if rows % b == 0) if rows % 8 == 0 else rows
    c1 = 1.0 - ADAM_B1 ** ADAM_STEP
    c2 = 1.0 - ADAM_B2 ** ADAM_STEP

    def fn(ins, consts):
        wv, gv, mv, vv = ins
        m2 = ADAM_B1 * mv + (1.0 - ADAM_B1) * gv
        v2 = ADAM_B2 * vv + (1.0 - ADAM_B2) * (gv * gv)
        delta = -ADAM_LR * ((m2 / c1) / (jnp.sqrt(v2 / c2) + ADAM_EPS) + ADAM_WD * wv)
        return [delta, m2, v2], []

    outs, _ = _rowwise(fn, [(a.reshape(rows, cols), cols, 0) for a in (w, g, m, v)], [], [(cols, F32)] * 3, [],
                       bm=bm, name="adamw")
    return [o.reshape(shape) for o in outs]


BIG = ("w1t", "w1o", "wint", "wa", "wbt", "wo", "w2t", "w2o")
SMALL_ROWS = (("ffn1_norm", 0), ("mix_norm", 2), ("lbsum", 4), ("hgrn_out_norm", 6), ("ffn2_norm", 8),
              ("attn_q_norm", 10), ("attn_k_norm", 12))
SMALL_PACK_ROWS = 16


def kernel(x, ffn1_norm, ffn1_w_in, ffn1_w_out, mix_norm, w_in, hgrn_lb_logits, hgrn_out_norm, attn_q_norm, attn_k_norm, w_branch_a, w_branch_b, w_out, ffn2_norm, ffn2_w_in, ffn2_w_out, loss_target, m_ffn1_norm, m_ffn1_w_in, m_ffn1_w_out, m_mix_norm, m_w_in, m_hgrn_lb_logits, m_hgrn_out_norm, m_attn_q_norm, m_attn_k_norm, m_w_branch_a, m_w_branch_b, m_w_out, m_ffn2_norm, m_ffn2_w_in, m_ffn2_w_out, v_ffn1_norm, v_ffn1_w_in, v_ffn1_w_out, v_mix_norm, v_w_in, v_hgrn_lb_logits, v_hgrn_out_norm, v_attn_q_norm, v_attn_k_norm, v_w_branch_a, v_w_branch_b, v_w_out, v_ffn2_norm, v_ffn2_w_in, v_ffn2_w_out):
    names = ("ffn1_norm", "ffn1_w_in", "ffn1_w_out", "mix_norm", "w_in", "hgrn_lb_logits", "hgrn_out_norm", "attn_q_norm",
             "attn_k_norm", "w_branch_a", "w_branch_b", "w_out", "ffn2_norm", "ffn2_w_in", "ffn2_w_out")
    w = dict(zip(names, (ffn1_norm, ffn1_w_in, ffn1_w_out, mix_norm, w_in, hgrn_lb_logits, hgrn_out_norm, attn_q_norm,
                         attn_k_norm, w_branch_a, w_branch_b, w_out, ffn2_norm, ffn2_w_in, ffn2_w_out)))
    m = dict(zip(names, (m_ffn1_norm, m_ffn1_w_in, m_ffn1_w_out, m_mix_norm, m_w_in, m_hgrn_lb_logits, m_hgrn_out_norm,
                         m_attn_q_norm, m_attn_k_norm, m_w_branch_a, m_w_branch_b, m_w_out, m_ffn2_norm, m_ffn2_w_in, m_ffn2_w_out)))
    v = dict(zip(names, (v_ffn1_norm, v_ffn1_w_in, v_ffn1_w_out, v_mix_norm, v_w_in, v_hgrn_lb_logits, v_hgrn_out_norm,
                         v_attn_q_norm, v_attn_k_norm, v_w_branch_a, v_w_branch_b, v_w_out, v_ffn2_norm, v_ffn2_w_in, v_ffn2_w_out)))
    depth, d = ffn1_norm.shape

    def tr(a):
        return jnp.swapaxes(a, 1, 2)

    shard = dict(w1t=tr(ffn1_w_in), w1o=ffn1_w_out, wint=tr(w_in), wa=w_branch_a,
                 wbt=tr(w_branch_b).reshape(depth, -1, d), wo=w_out, w2t=tr(ffn2_w_in), w2o=ffn2_w_out)
    gathered = _exchange([shard[k].astype(BF16) for k in BIG], ["gather"] * len(BIG), "gather_weights")
    big = {}
    for k, g in zip(BIG, gathered):
        full = g.reshape(depth, -1, d)
        if k == "wbt":
            full = full.reshape(depth, d, -1)
        big[k] = [full[l] for l in range(depth)]
    small = {k: w[k] for k in ("ffn1_norm", "mix_norm", "hgrn_lb_logits", "hgrn_out_norm", "attn_q_norm", "attn_k_norm", "ffn2_norm")}

    loss, dx, gbig, gsmall = _local_step(x[0], loss_target[0], small, big)

    pack = jnp.zeros((SMALL_PACK_ROWS, d), F32)
    for k, r0 in SMALL_ROWS:
        flat = gsmall[k].reshape(depth, -1)
        pack = pack.at[r0:r0 + depth, :flat.shape[1]].set(flat)
    pack = pack.at[14, :].set(loss)
    srcs = [jnp.stack(gbig[k]).astype(BF16).reshape(depth, N_DEV, -1, d) for k in BIG] + [pack[None]]
    landed = _exchange(srcs, ["scatter"] * len(BIG) + ["gather"], "exchange_grads")
    gsum = {k: _sum_slots(l) for k, l in zip(BIG, landed)}
    tot = _sum_slots(landed[-1])[0]

    grads = {}
    for k, r0 in SMALL_ROWS:
        shp = (depth,) + (w[k].shape[1:] if k != "lbsum" else (d,))
        grads[k] = tot[r0:r0 + depth, :int(np.prod(shp[1:]))].reshape(shp)
    _, lb_vjp = jax.vjp(_lower_bounds, hgrn_lb_logits)
    grads["hgrn_lb_logits"] = lb_vjp(grads.pop("lbsum"))[0]
    grads["ffn1_w_in"], grads["ffn1_w_out"] = tr(gsum["w1t"]), gsum["w1o"]
    grads["w_in"], grads["w_branch_a"] = tr(gsum["wint"]), gsum["wa"]
    grads["w_branch_b"] = tr(gsum["wbt"].reshape(depth, d // N_DEV, -1))
    grads["w_out"] = gsum["wo"]
    grads["ffn2_w_in"], grads["ffn2_w_out"] = tr(gsum["w2t"]), gsum["w2o"]

    upd = {k: _adamw(w[k], grads[k], m[k], v[k]) for k in names}
    return (tot[14, 0], dx[None], *[grads[k] for k in names], *[upd[k][0] for k in names],
            *[upd[k][1] for k in names], *[upd[k][2] for k in names])
```

```python
import functools
import math

import jax
import jax.numpy as jnp
import numpy as np
from jax import lax
from jax.experimental import pallas as pl
from jax.experimental.pallas import tpu as pltpu

F32 = jnp.float32
BF16 = jnp.bfloat16

N_DEV = 8
EPS = 1e-6
HG_DK = 128
HG_CHUNK = 64
HG_SUB = 16
ATT_PATTERNS = ((128, 1), (512, 4), (2048, 16))
ATT_GROUPS = 3
ATT_HEADS = 4
ATT_DH = 128
ATT_BLK = 128
ROPE_THETA = 10000.0
ADAM_LR, ADAM_B1, ADAM_B2, ADAM_EPS, ADAM_WD, ADAM_STEP = 0.001, 0.9, 0.999, 1e-08, 0.01, 10
VMEM_LIMIT_BYTES = 56 * 1024 * 1024
MESH = pl.DeviceIdType.MESH


def _cparams(sem, **kw):
    return pltpu.CompilerParams(dimension_semantics=sem, vmem_limit_bytes=VMEM_LIMIT_BYTES, **kw)


def _sigmoid(x):
    return 1.0 / (1.0 + jnp.exp(-x))


def _mm(a_list, b_list, pairs, n_acc, fin, out_dtypes, *, m, n, k, ta=False, tb=False, bm, bn, bk,
        b_off=None, extras=(), e_off=None, name):
    bm, bn, bk = min(bm, m), min(bn, n), min(bk, k)
    assert m % bm == 0 and n % bn == 0 and k % bk == 0, (name, m, n, k, bm, bn, bk)
    nk = k // bk
    b_off = b_off or [(0, 0)] * len(b_list)
    e_off = e_off or [0] * len(extras)
    na, nb, ne = len(a_list), len(b_list), len(extras)
    dn = (((0,) if ta else (1,), (1,) if tb else (0,)), ((), ()))

    def body(*refs):
        a_refs, b_refs = refs[:na], refs[na:na + nb]
        e_refs = refs[na + nb:na + nb + ne]
        o_refs = refs[na + nb + ne:na + nb + ne + len(out_dtypes)]
        acc_refs = refs[na + nb + ne + len(out_dtypes):]
        kk = pl.program_id(2)
        parts = [None] * n_acc
        for ai, bi, ci in pairs:
            p = lax.dot_general(a_refs[ai][...], b_refs[bi][...], dn, preferred_element_type=F32)
            parts[ci] = p if parts[ci] is None else parts[ci] + p

        def finish(accs):
            outs = fin(accs, [e[...] for e in e_refs])
            for o_ref, o in zip(o_refs, outs):
                o_ref[...] = o.astype(o_ref.dtype)

        if nk == 1:
            finish(parts)
        else:
            @pl.when(kk == 0)
            def _():
                for c in range(n_acc):
                    acc_refs[c][...] = parts[c]

            @pl.when(kk > 0)
            def _():
                for c in range(n_acc):
                    acc_refs[c][...] += parts[c]

            @pl.when(kk == nk - 1)
            def _():
                finish([acc_refs[c][...] for c in range(n_acc)])

    a_spec = pl.BlockSpec((bk, bm), lambda i, j, q: (q, i)) if ta else pl.BlockSpec((bm, bk), lambda i, j, q: (i, q))

    def b_spec(off):
        on, ok = off
        if tb:
            return pl.BlockSpec((bn, bk), lambda i, j, q: (j + on, q + ok))
        return pl.BlockSpec((bk, bn), lambda i, j, q: (q + ok, j + on))

    mn_spec = pl.BlockSpec((bm, bn), lambda i, j, q: (i, j))
    outs = pl.pallas_call(
        body,
        out_shape=[jax.ShapeDtypeStruct((m, n), d) for d in out_dtypes],
        grid=(m // bm, n // bn, nk),
        in_specs=[a_spec] * na + [b_spec(o) for o in b_off]
        + [pl.BlockSpec((bm, bn), lambda i, j, q, o=o: (i, j + o)) for o in e_off],
        out_specs=[mn_spec] * len(out_dtypes),
        scratch_shapes=[pltpu.VMEM((bm, bn), F32) for _ in range(n_acc if nk > 1 else 0)],
        compiler_params=_cparams(("parallel", "parallel", "arbitrary")),
        name=name,
    )(*a_list, *b_list, *extras)
    return outs


def _first(accs, ex):
    return (accs[0],)


def _rowwise(fn, ins, consts, out_defs, sum_widths, *, bm, name):
    t = ins[0][0].shape[0]
    bm = min(bm, t)
    assert t % bm == 0, (name, t, bm)
    ni, nc, no, ns = len(ins), len(consts), len(out_defs), len(sum_widths)

    def body(*refs):
        i_refs, c_refs = refs[:ni], refs[ni:ni + nc]
        o_refs, s_refs = refs[ni + nc:ni + nc + no], refs[ni + nc + no:]
        outs, sums = fn([r[...] for r in i_refs], [r[...] for r in c_refs])
        for o_ref, o in zip(o_refs, outs):
            o_ref[...] = o.astype(o_ref.dtype)
        if ns:
            first = pl.program_id(0) == 0

            @pl.when(first)
            def _():
                for s_ref, s in zip(s_refs, sums):
                    s_ref[...] = s

            @pl.when(jnp.logical_not(first))
            def _():
                for s_ref, s in zip(s_refs, sums):
                    s_ref[...] += s

    def win(width, cb):
        return pl.BlockSpec((bm, width), lambda i: (i, cb))

    res = pl.pallas_call(
        body,
        out_shape=[jax.ShapeDtypeStruct((t, w), d) for w, d in out_defs]
        + [jax.ShapeDtypeStruct((8, w), F32) for w in sum_widths],
        grid=(t // bm,),
        in_specs=[win(w, cb) for _, w, cb in ins] + [pl.BlockSpec(c.shape, lambda i, nd=c.ndim: (0,) * nd) for c in consts],
        out_specs=[win(w, 0) for w, _ in out_defs] + [pl.BlockSpec((8, w), lambda i: (0, 0)) for w in sum_widths],
        compiler_params=_cparams(("arbitrary",) if ns else ("parallel",)),
        name=name,
    )(*[a for a, _, _ in ins], *consts)
    return res[:no], [jnp.sum(s, axis=0) for s in res[no:]]


def _colsum8(x):
    bm, w = x.shape
    return jnp.sum(x.reshape(bm // 8, 8, w), axis=0)


def _tri(n, upper=False):
    r = lax.broadcasted_iota(jnp.int32, (n, n), 0)
    c = lax.broadcasted_iota(jnp.int32, (n, n), 1)
    return (c >= r) if upper else (c <= r)


def _exact_tri_matmul(tri_bf16, x):
    x0 = x.astype(BF16)
    r1 = x - x0.astype(F32)
    x1 = r1.astype(BF16)
    x2 = (r1 - x1.astype(F32)).astype(BF16)
    w = x.shape[1]
    y = jnp.dot(tri_bf16, jnp.concatenate([x0, x1, x2], axis=1), preferred_element_type=F32)
    return y[:, :w] + y[:, w:2 * w] + y[:, 2 * w:]


def _dot_nt(a, b):
    return lax.dot_general(a, b, (((1,), (1,)), ((), ())), preferred_element_type=F32)


def _dot_tn(a, b):
    return lax.dot_general(a, b, (((0,), (0,)), ((), ())), preferred_element_type=F32)


def _dot(a, b):
    return jnp.dot(a, b, preferred_element_type=F32)


def _hg_gates(hq, hf, lb):
    sq = _sigmoid(hq)
    q = hq * sq
    sg = _sigmoid(hf)
    f = lb + (1.0 - lb) * sg
    return q, sq, sg, f


def _hg_intra(q, kk, g):
    c = q.shape[0]
    rows = lax.broadcasted_iota(jnp.int32, (c, 1), 0)
    a_rows, qts, kts, eqs, eks = [], [], [], [], []
    for i in range(c // HG_SUB):
        lo = i * HG_SUB
        ref = g[lo - 1:lo, :] if i else jnp.zeros_like(g[0:1, :])
        eq = jnp.exp(g[lo:lo + HG_SUB, :] - ref)
        ek = jnp.exp(jnp.where(rows < lo + HG_SUB, ref - g, 0.0))
        qt = q[lo:lo + HG_SUB, :] * eq
        kt = kk * ek
        a = _dot_nt(qt.astype(BF16), kt.astype(BF16))
        tpos = lo + lax.broadcasted_iota(jnp.int32, (HG_SUB, c), 0)
        spos = lax.broadcasted_iota(jnp.int32, (HG_SUB, c), 1)
        a_rows.append(jnp.where(spos <= tpos, a, 0.0))
        qts.append(qt), kts.append(kt), eqs.append(eq), eks.append(ek)
    return jnp.concatenate(a_rows, axis=0), qts, kts, eqs, eks


def _hgrn_fwd(zh, lb3, *, tb=512):
    t = zh.shape[0]
    nh = lb3.shape[0]
    c = HG_CHUNK
    tb = min(tb, t)
    nchunk = tb // c

    def body(hq_ref, hf_ref, hi_ref, lb_ref, o_ref, st_ref, state):
        @pl.when(pl.program_id(1) == 0)
        def _():
            state[...] = jnp.zeros_like(state)

        lb = lb_ref[0]
        tril = _tri(c).astype(BF16)

        def chunk(ci, carry):
            sl = pl.ds(pl.multiple_of(ci * c, c), c)
            q, _, _, f = _hg_gates(hq_ref[sl, :], hf_ref[sl, :], lb)
            v = hi_ref[sl, :]
            kk = 1.0 - f
            g = _exact_tri_matmul(tril, jnp.log(f))
            a, _, _, _, _ = _hg_intra(q, kk, g)
            st = state[...]
            st_ref[0, ci] = st
            vb = v.astype(BF16)
            o = _dot(a.astype(BF16), vb) + _dot_nt((q * jnp.exp(g)).astype(BF16), st.astype(BF16))
            o_ref[sl, :] = o
            glast = g[c - 1:c, :]
            kg = kk * jnp.exp(glast - g)
            state[...] = st * jnp.exp(glast) + _dot_tn(vb, kg.astype(BF16))
            return carry

        lax.fori_loop(0, nchunk, chunk, 0)

    def col(cb):
        return pl.BlockSpec((tb, HG_DK), lambda h, i: (i, cb * nh + h))

    return pl.pallas_call(
        body,
        out_shape=[jax.ShapeDtypeStruct((t, nh * HG_DK), F32), jax.ShapeDtypeStruct((nh, t // c, HG_DK, HG_DK), F32)],
        grid=(nh, t // tb),
        in_specs=[col(0), col(1), col(2), pl.BlockSpec((1, 1, HG_DK), lambda h, i: (h, 0, 0))],
        out_specs=[pl.BlockSpec((tb, HG_DK), lambda h, i: (i, h)),
                   pl.BlockSpec((1, nchunk, HG_DK, HG_DK), lambda h, i: (h, i, 0, 0))],
        scratch_shapes=[pltpu.VMEM((HG_DK, HG_DK), F32)],
        compiler_params=_cparams(("parallel", "arbitrary")),
        name="hgrn_fwd",
    )(zh, zh, zh, lb3)


def _hgrn_bwd(zh, lb3, states, d_o, *, tb=512):
    t = zh.shape[0]
    nh = lb3.shape[0]
    c = HG_CHUNK
    tb = min(tb, t)
    nchunk = tb // c
    nblk = t // tb

    def body(hq_ref, hf_ref, hi_ref, lb_ref, st_ref, do_ref, dq_ref, df_ref, dv_ref, dlb_ref, dstate):
        @pl.when(pl.program_id(1) == 0)
        def _():
            dstate[...] = jnp.zeros_like(dstate)
            dlb_ref[...] = jnp.zeros_like(dlb_ref)

        lb = lb_ref[0]
        tril = _tri(c).astype(BF16)
        triu = _tri(c, upper=True).astype(BF16)
        last_row = lax.broadcasted_iota(jnp.int32, (c, 1), 0) == c - 1

        def chunk(j, carry):
            ci = nchunk - 1 - j
            sl = pl.ds(pl.multiple_of(ci * c, c), c)
            hq, hf = hq_ref[sl, :], hf_ref[sl, :]
            q, sq, sg, f = _hg_gates(hq, hf, lb)
            v = hi_ref[sl, :]
            kk = 1.0 - f
            g = _exact_tri_matmul(tril, jnp.log(f))
            a, qts, kts, eqs, eks = _hg_intra(q, kk, g)
            st = st_ref[0, ci]
            dst = dstate[...]
            do = do_ref[sl, :]
            dob, vb = do.astype(BF16), v.astype(BF16)
            glast = g[c - 1:c, :]
            eg = jnp.exp(g)
            egl = jnp.exp(glast - g)
            qg = q * eg
            kg = kk * egl
            dv = _dot_tn(a.astype(BF16), dob) + _dot_nt(kg.astype(BF16), dst.astype(BF16))
            da = jnp.where(_tri(c), _dot_nt(dob, vb), 0.0).astype(BF16)
            dq_parts, dgq_parts = [], []
            dk = jnp.zeros_like(kk)
            dgk = jnp.zeros_like(kk)
            for i in range(c // HG_SUB):
                da_i = da[i * HG_SUB:(i + 1) * HG_SUB, :]
                ktb, qtb = kts[i].astype(BF16), qts[i].astype(BF16)
                xi = _dot(da_i, ktb)
                yi = _dot_tn(da_i, qtb)
                dq_parts.append(xi * eqs[i])
                dk = dk + yi * eks[i]
                dgq_parts.append(xi * qtb.astype(F32))
                dgk = dgk + yi * ktb.astype(F32)
            dq_inter = _dot(dob, st.astype(BF16)) * eg
            dq = jnp.concatenate(dq_parts, axis=0) + dq_inter
            dk_state = _dot(vb, dst.astype(BF16)) * egl
            dk = dk + dk_state
            dg = jnp.concatenate(dgq_parts, axis=0) - dgk + q * dq_inter - kk * dk_state
            dgl = jnp.sum(kk * dk_state, axis=0, keepdims=True) + jnp.exp(glast) * jnp.sum(st * dst, axis=0, keepdims=True)
            dg = dg + jnp.where(last_row, dgl, 0.0)
            dlogf = _exact_tri_matmul(triu, dg)
            dfv = dlogf / f - dk
            dq_ref[sl, :] = (dq * (sq * (1.0 + hq * (1.0 - sq)))).astype(dq_ref.dtype)
            df_ref[sl, :] = (dfv * (1.0 - lb) * sg * (1.0 - sg)).astype(df_ref.dtype)
            dv_ref[sl, :] = dv.astype(dv_ref.dtype)
            dlb_ref[0] += jnp.sum(dfv * (1.0 - sg), axis=0, keepdims=True)
            dstate[...] = dst * jnp.exp(glast) + _dot_tn(dob, qg.astype(BF16))
            return carry

        lax.fori_loop(0, nchunk, chunk, 0)

    def col(cb):
        return pl.BlockSpec((tb, HG_DK), lambda h, i: (nblk - 1 - i, cb * nh + h))

    ocol = pl.BlockSpec((tb, HG_DK), lambda h, i: (nblk - 1 - i, h))
    w = nh * HG_DK
    dq, df, dv, dlb = pl.pallas_call(
        body,
        out_shape=[jax.ShapeDtypeStruct((t, w), BF16)] * 3 + [jax.ShapeDtypeStruct((nh, 1, HG_DK), F32)],
        grid=(nh, nblk),
        in_specs=[col(0), col(1), col(2), pl.BlockSpec((1, 1, HG_DK), lambda h, i: (h, 0, 0)),
                  pl.BlockSpec((1, nchunk, HG_DK, HG_DK), lambda h, i: (h, nblk - 1 - i, 0, 0)), ocol],
        out_specs=[ocol, ocol, ocol, pl.BlockSpec((1, 1, HG_DK), lambda h, i: (h, 0, 0))],
        scratch_shapes=[pltpu.VMEM((HG_DK, HG_DK), F32)],
        compiler_params=_cparams(("parallel", "arbitrary")),
        name="hgrn_bwd",
    )(zh, zh, zh, lb3, states, d_o)
    return dq, df, dv, dlb.reshape(w)


NEG = -1e30
ATT_GW = ATT_HEADS * ATT_DH


def _att_scores(q, kp, kc, has_prev):
    scale = ATT_DH ** -0.5
    i = lax.broadcasted_iota(jnp.int32, (ATT_BLK, ATT_BLK), 0)
    j = lax.broadcasted_iota(jnp.int32, (ATT_BLK, ATT_BLK), 1)
    s_p = jnp.where(jnp.logical_and(j >= i, has_prev), _dot_nt(q, kp) * scale, NEG)
    s_c = jnp.where(j <= i, _dot_nt(q, kc) * scale, NEG)
    return s_p, s_c


def _att_views(arrs, d):
    return [a.reshape(a.shape[0] // d, d * a.shape[1]) for a in arrs]


def _attn_fwd(qb, kb, vb, g):
    t = qb.shape[0]
    d = ATT_PATTERNS[g][1]
    nb = t // d // ATT_BLK
    q2, k2, v2 = _att_views([qb, kb, vb], d)

    def body(q_ref, kc_ref, kp_ref, vc_ref, vp_ref, o_ref, l_ref):
        has_prev = pl.program_id(1) > 0
        for h in range(ATT_HEADS):
            hs = slice(h * ATT_DH, (h + 1) * ATT_DH)
            s_p, s_c = _att_scores(q_ref[:, hs], kp_ref[:, hs], kc_ref[:, hs], has_prev)
            m = jnp.maximum(jnp.max(s_p, axis=1, keepdims=True), jnp.max(s_c, axis=1, keepdims=True))
            p_p, p_c = jnp.exp(s_p - m), jnp.exp(s_c - m)
            l = jnp.sum(p_p, axis=1, keepdims=True) + jnp.sum(p_c, axis=1, keepdims=True)
            o = _dot(p_p.astype(BF16), vp_ref[:, hs]) + _dot(p_c.astype(BF16), vc_ref[:, hs])
            o_ref[:, hs] = o / l
            l_ref[:, hs] = jnp.broadcast_to(m + jnp.log(l), (ATT_BLK, ATT_DH))

    cur = pl.BlockSpec((ATT_BLK, ATT_GW), lambda r, n: (n, r * ATT_GROUPS + g))
    prev = pl.BlockSpec((ATT_BLK, ATT_GW), lambda r, n: (jnp.maximum(n - 1, 0), r * ATT_GROUPS + g))
    out = pl.BlockSpec((ATT_BLK, ATT_GW), lambda r, n: (n, r))
    o, lse = pl.pallas_call(
        body,
        out_shape=[jax.ShapeDtypeStruct((t // d, d * ATT_GW), F32)] * 2,
        grid=(d, nb),
        in_specs=[cur, cur, prev, cur, prev],
        out_specs=[out, out],
        compiler_params=_cparams(("parallel", "arbitrary")),
        name=f"attn_fwd_g{g}",
    )(q2, k2, k2, v2, v2)
    return o.reshape(t, ATT_GW), lse.reshape(t, ATT_GW)


def _attn_bwd(qb, kb, vb, o, lse, d_o, d_lse, g):
    t = qb.shape[0]
    d = ATT_PATTERNS[g][1]
    nb = t // d // ATT_BLK
    q2, k2, v2 = _att_views([qb, kb, vb], d)
    o2, l2, do2, dl2 = _att_views([o, lse, d_o, d_lse], d)

    def body(q_ref, kc_ref, kp_ref, vc_ref, vp_ref, o_ref, l_ref, do_ref, dl_ref, dq_ref, dk_ref, dv_ref, ck, cv):
        n = pl.program_id(1)
        active = n < nb

        @pl.when(n == 0)
        def _():
            ck[...] = jnp.zeros_like(ck)
            cv[...] = jnp.zeros_like(cv)

        @pl.when(jnp.logical_not(active))
        def _():
            dk_ref[...] = ck[...]
            dv_ref[...] = cv[...]

        @pl.when(active)
        def _():
            has_prev = n > 0
            for h in range(ATT_HEADS):
                hs = slice(h * ATT_DH, (h + 1) * ATT_DH)
                q, kp, kc, vp, vc = q_ref[:, hs], kp_ref[:, hs], kc_ref[:, hs], vp_ref[:, hs], vc_ref[:, hs]
                s_p, s_c = _att_scores(q, kp, kc, has_prev)
                lse_h = l_ref[:, hs][:, 0:1]
                p_p, p_c = jnp.exp(s_p - lse_h), jnp.exp(s_c - lse_h)
                do = do_ref[:, hs]
                delta = jnp.sum(do * o_ref[:, hs] - dl_ref[:, hs], axis=1, keepdims=True)
                dob = do.astype(BF16)
                scale = ATT_DH ** -0.5
                ds_p = (p_p * (_dot_nt(dob, vp) - delta) * scale).astype(BF16)
                ds_c = (p_c * (_dot_nt(dob, vc) - delta) * scale).astype(BF16)
                dq_ref[:, hs] = _dot(ds_p, kp) + _dot(ds_c, kc)
                dk_ref[:, hs] = ck[:, hs] + _dot_tn(ds_p, q)
                dv_ref[:, hs] = cv[:, hs] + _dot_tn(p_p.astype(BF16), dob)
                ck[:, hs] = _dot_tn(ds_c, q)
                cv[:, hs] = _dot_tn(p_c.astype(BF16), dob)

    def qn(n):
        return jnp.minimum(n, nb - 1)

    cur = pl.BlockSpec((ATT_BLK, ATT_GW), lambda r, n: (qn(n), r * ATT_GROUPS + g))
    prev = pl.BlockSpec((ATT_BLK, ATT_GW), lambda r, n: (jnp.maximum(qn(n) - 1, 0), r * ATT_GROUPS + g))
    own = pl.BlockSpec((ATT_BLK, ATT_GW), lambda r, n: (qn(n), r))
    behind = pl.BlockSpec((ATT_BLK, ATT_GW), lambda r, n: (jnp.maximum(n - 1, 0), r))
    shp = jax.ShapeDtypeStruct((t // d, d * ATT_GW), F32)
    dq, dk, dv = pl.pallas_call(
        body,
        out_shape=[shp, shp, shp],
        grid=(d, nb + 1),
        in_specs=[cur, cur, prev, cur, prev, own, own, own, own],
        out_specs=[own, behind, behind],
        scratch_shapes=[pltpu.VMEM((ATT_BLK, ATT_GW), F32), pltpu.VMEM((ATT_BLK, ATT_GW), F32)],
        compiler_params=_cparams(("parallel", "arbitrary")),
        name=f"attn_bwd_g{g}",
    )(q2, k2, k2, v2, v2, o2, l2, do2, dl2)
    return dq.reshape(t, ATT_GW), dk.reshape(t, ATT_GW), dv.reshape(t, ATT_GW)


def _rms_parts(x, width):
    outs = []
    for lo in range(0, x.shape[1], width):
        xs = x[:, lo:lo + width]
        r = lax.rsqrt(jnp.mean(xs * xs, axis=1, keepdims=True) + EPS)
        outs.append((xs * r, r))
    return outs


def _rms_bwd_part(xh, r, dxh):
    return r * (dxh - xh * jnp.mean(dxh * xh, axis=1, keepdims=True))


def _norm_fwd(x, gain):
    d = x.shape[1]

    def fn(ins, consts):
        (xh, _), = _rms_parts(ins[0], d)
        return [xh * consts[0]], []

    (h,), _ = _rowwise(fn, [(x, d, 0)], [gain.reshape(1, d)], [(d, BF16)], [], bm=512, name="norm_fwd")
    return h


def _norm_bwd(x, gain, dh, dres):
    d = x.shape[1]

    def fn(ins, consts):
        (xh, r), = _rms_parts(ins[0], d)
        dx = ins[2] + _rms_bwd_part(xh, r, ins[1] * consts[0])
        return [dx], [_colsum8(ins[1] * xh)]

    (dx,), (dg,) = _rowwise(fn, [(x, d, 0), (dh, d, 0), (dres, d, 0)], [gain.reshape(1, d)], [(d, F32)], [d],
                            bm=512, name="norm_bwd")
    return dx, dg


def _rot_sign():
    lane = lax.broadcasted_iota(jnp.int32, (1, ATT_DH), 1)
    return jnp.where(lane < ATT_DH // 2, -1.0, 1.0).astype(F32)


def _rope(y, cos, sin):
    return y * cos + pltpu.roll(y, ATT_DH // 2, axis=1) * _rot_sign() * sin


def _rope_t(dy, cos, sin):
    return dy * cos - pltpu.roll(dy * sin, ATT_DH // 2, axis=1) * _rot_sign()


def _qk_prep(zq, zk, zv, qn, kn, cos, sin):
    w = zq.shape[1]

    def fn(ins, consts):
        cs, sn = ins[3], ins[4]
        outs = []
        for z, gain in ((ins[0], consts[0]), (ins[1], consts[1])):
            parts = _rms_parts(z, ATT_DH)
            outs.append(jnp.concatenate(
                [_rope(xh * gain[:, i * ATT_DH:(i + 1) * ATT_DH], cs, sn) for i, (xh, _) in enumerate(parts)], axis=1))
        return outs + [ins[2]], []

    (qb, kb, vb), _ = _rowwise(fn, [(zq, w, 0), (zk, w, 0), (zv, w, 0), (cos, ATT_DH, 0), (sin, ATT_DH, 0)], [qn, kn],
                               [(w, BF16)] * 3, [], bm=256, name="qk_prep")
    return qb, kb, vb


def _qk_prep_bwd(zq, zk, dq_g, dk_g, dv_g, qn, kn, cos, sin):
    w = zq.shape[1]

    def fn(ins, consts):
        cs, sn = ins[2], ins[3]
        outs, sums = [], []
        for z, gain, dparts in ((ins[0], consts[0], ins[4:7]), (ins[1], consts[1], ins[7:10])):
            dout = jnp.concatenate(dparts, axis=1)
            dz, dgain = [], []
            for i, (xh, r) in enumerate(_rms_parts(z, ATT_DH)):
                hs = slice(i * ATT_DH, (i + 1) * ATT_DH)
                dy = _rope_t(dout[:, hs], cs, sn)
                dgain.append(_colsum8(dy * xh))
                dz.append(_rms_bwd_part(xh, r, dy * gain[:, hs]))
            outs.append(jnp.concatenate(dz, axis=1))
            sums.append(jnp.concatenate(dgain, axis=1))
        outs.append(jnp.concatenate(ins[10:13], axis=1))
        return outs, sums

    gw = ATT_GW
    ins = [(zq, w, 0), (zk, w, 0), (cos, ATT_DH, 0), (sin, ATT_DH, 0)]
    ins += [(a, gw, 0) for a in dq_g] + [(a, gw, 0) for a in dk_g] + [(a, gw, 0) for a in dv_g]
    (dzq, dzk, dzv), (dqn, dkn) = _rowwise(fn, ins, [qn, kn], [(w, BF16)] * 3, [w, w], bm=256, name="qk_prep_bwd")
    return dzq, dzk, dzv, dqn, dkn


def _post_a(o_raw, zh, gout):
    w = o_raw.shape[1]

    def fn(ins, consts):
        oh = jnp.concatenate([xh for xh, _ in _rms_parts(ins[0], HG_DK)], axis=1)
        hg = ins[1]
        return [oh * consts[0] * (hg * _sigmoid(hg))], []

    (y,), _ = _rowwise(fn, [(o_raw, w, 0), (zh, w, 3)], [gout.reshape(1, w)], [(w, BF16)], [], bm=512, name="post_a")
    return y


def _post_a_bwd(o_raw, zh, gout, dy):
    w = o_raw.shape[1]

    def fn(ins, consts):
        parts = _rms_parts(ins[0], HG_DK)
        oh = jnp.concatenate([xh for xh, _ in parts], axis=1)
        hg, dyv, gain = ins[1], ins[2], consts[0]
        sg = _sigmoid(hg)
        s = hg * sg
        doh = dyv * gain * s
        do = jnp.concatenate([_rms_bwd_part(xh, r, doh[:, i * HG_DK:(i + 1) * HG_DK]) for i, (xh, r) in enumerate(parts)], axis=1)
        dhg = dyv * oh * gain * (sg * (1.0 + hg * (1.0 - sg)))
        return [do, dhg], [_colsum8(dyv * oh * s)]

    (do, dhg), (dgain,) = _rowwise(fn, [(o_raw, w, 0), (zh, w, 3), (dy, w, 0)], [gout.reshape(1, w)],
                                   [(w, F32), (w, BF16)], [w], bm=512, name="post_a_bwd")
    return do, dhg, dgain


def _merge_alpha(lses):
    m = jnp.maximum(jnp.maximum(lses[0], lses[1]), lses[2])
    e = [jnp.exp(l - m) for l in lses]
    inv = 1.0 / (e[0] + e[1] + e[2])
    return [x * inv for x in e]


def _merge_b(o_g, lse_g):
    def fn(ins, consts):
        al = _merge_alpha(ins[3:6])
        return [al[0] * ins[0] + al[1] * ins[1] + al[2] * ins[2]], []

    (y,), _ = _rowwise(fn, [(a, ATT_GW, 0) for a in list(o_g) + list(lse_g)], [], [(ATT_GW, BF16)], [], bm=512, name="merge_b")
    return y


def _merge_b_bwd(o_g, lse_g, dy):
    def fn(ins, consts):
        al = _merge_alpha(ins[3:6])
        dyv = ins[6]
        dal = [dyv * ins[i] for i in range(3)]
        tot = al[0] * dal[0] + al[1] * dal[1] + al[2] * dal[2]
        return [al[i] * dyv for i in range(3)] + [al[i] * (dal[i] - tot) for i in range(3)], []

    outs, _ = _rowwise(fn, [(a, ATT_GW, 0) for a in list(o_g) + list(lse_g) + [dy]], [], [(ATT_GW, F32)] * 6, [],
                       bm=512, name="merge_b_bwd")
    return outs[:3], outs[3:]


def _loss_head(y, target):
    d = y.shape[1]

    def fn(ins, consts):
        e = ins[0] - ins[1]
        return [e * (1.0 / d)], [_colsum8(e * e)]

    (dy,), (sq,) = _rowwise(fn, [(y, d, 0), (target, d, 0)], [], [(d, F32)], [d], bm=512, name="loss_head")
    return 0.5 * jnp.sum(sq) / d, dy


def _silu_grad(a):
    s = _sigmoid(a)
    return s * (1.0 + a * (1.0 - s))


def _ffn_fwd(x, gain, wt, wo, tag):
    t, d = x.shape
    f = wo.shape[0]
    h = _norm_fwd(x, gain)

    def act(accs, ex):
        a, b = accs
        return (a * _sigmoid(a) * b, a, b)

    u, a, b = _mm([h], [wt, wt], [(0, 0, 0), (0, 1, 1)], 2, act, [BF16, BF16, BF16], m=t, n=f, k=d, tb=True,
                  bm=1024, bn=256, bk=d, b_off=[(0, 0), (f // min(256, f), 0)], name=f"ffn_in_{tag}")
    (y,) = _mm([u], [wo], [(0, 0, 0)], 1, lambda accs, ex: (ex[0] + 0.5 * accs[0],), [F32], m=t, n=d, k=f,
               bm=512, bn=d, bk=f, extras=[x], name=f"ffn_out_{tag}")
    return y, (x, h, u, a, b)


def _ffn_bwd(dy, saved, gain, wt, wo, tag, tok):
    x, h, u, a, b = saved
    t, d = x.shape
    f = wo.shape[0]
    dyb = (dy + tok).astype(BF16)

    def dact(accs, ex):
        du = 0.5 * accs[0]
        av, bv = ex[0].astype(F32), ex[1].astype(F32)
        return (du * bv * _silu_grad(av), du * av * _sigmoid(av))

    da, db = _mm([dyb], [wo], [(0, 0, 0)], 1, dact, [BF16, BF16], m=t, n=f, k=d, tb=True, bm=1024, bn=256, bk=d,
                 extras=[a, b], name=f"ffn_dact_{tag}")
    (dwo,) = _mm([u], [dyb], [(0, 0, 0)], 1, lambda accs, ex: (0.5 * accs[0],), [BF16], m=f, n=d, k=t, ta=True,
                 bm=1408, bn=d, bk=1024, name=f"ffn_dwo_{tag}")
    (dh,) = _mm([da, db], [wt, wt], [(0, 0, 0), (1, 1, 0)], 1, _first, [F32], m=t, n=d, k=f, bm=512, bn=d, bk=f,
                b_off=[(0, 0), (0, 1)], name=f"ffn_dh_{tag}")
    dwt = [_mm([g], [h], [(0, 0, 0)], 1, _first, [BF16], m=f, n=d, k=t, ta=True, bm=1408, bn=d, bk=1024,
               name=f"ffn_dwt{i}_{tag}")[0] for i, g in enumerate((da, db))]
    dx, dgain = _norm_bwd(x, gain, dh, dy)
    return dx, dgain, jnp.concatenate(dwt, axis=0), dwo


Z_SPLITS = (("h", 4096), ("q", 1536), ("k", 1536), ("v", 1536), ("g", 2048))


def _mix_fwd(x, p, cos, sin):
    t, d = x.shape
    hm = _norm_fwd(x, p["gm"])
    z, off = {}, 0
    for nm, width in Z_SPLITS:
        (z[nm],) = _mm([hm], [p["wint"]], [(0, 0, 0)], 1, _first, [F32], m=t, n=width, k=d, tb=True, bm=1024, bn=512, bk=d,
                       b_off=[(off // 512, 0)], name=f"mix_in_{nm}")
        off += width
    o_raw, states = _hgrn_fwd(z["h"], p["lb3"])
    qb, kb, vb = _qk_prep(z["q"], z["k"], z["v"], p["qn"], p["kn"], cos, sin)
    o_g, lse_g = zip(*[_attn_fwd(qb, kb, vb, g) for g in range(ATT_GROUPS)])
    oa = _post_a(o_raw, z["h"], p["gout"])
    ob = _merge_b(o_g, lse_g)
    late = p["late"](ob)
    p = dict(p, **late)
    (ya,) = _mm([oa], [p["wa"]], [(0, 0, 0)], 1, _first, [F32], m=t, n=d, k=oa.shape[1], bm=1024, bn=d, bk=oa.shape[1],
                name="branch_a")

    def gate(accs, ex):
        return (_sigmoid(ex[0]) * ex[2] + _sigmoid(ex[1]) * accs[0], accs[0])

    merged, yb = _mm([ob], [p["wbt"]], [(0, 0, 0)], 1, gate, [BF16, F32], m=t, n=d, k=ATT_GW, tb=True, bm=512, bn=d,
                     bk=ATT_GW, extras=[z["g"], z["g"], ya], e_off=[0, 1, 0], name="branch_b_gate")
    (y,) = _mm([merged], [p["wo"]], [(0, 0, 0)], 1, lambda accs, ex: (ex[0] + accs[0],), [F32], m=t, n=d, k=d,
               bm=1024, bn=d, bk=d, extras=[x], name="mix_out")
    return y, (x, hm, z, o_raw, states, qb, kb, vb, o_g, lse_g, oa, ob, ya, yb, merged, late)


def _mix_bwd(dy, saved, p, cos, sin, tok):
    x, hm, z, o_raw, states, qb, kb, vb, o_g, lse_g, oa, ob, ya, yb, merged, late = saved
    p = dict(p, **late)
    t, d = x.shape
    w = oa.shape[1]
    dyb = (dy + tok).astype(BF16)

    def dgate(accs, ex):
        dm = accs[0]
        sa, sb = _sigmoid(ex[0]), _sigmoid(ex[1])
        return (sa * dm, sb * dm, dm * ex[2] * sa * (1.0 - sa), dm * ex[3] * sb * (1.0 - sb))

    dya, dyb_, dga, dgb = _mm([dyb], [p["wo"]], [(0, 0, 0)], 1, dgate, [BF16] * 4, m=t, n=d, k=d, tb=True, bm=512, bn=d,
                              bk=d, extras=[z["g"], z["g"], ya, yb], e_off=[0, 1, 0, 0], name="mix_out_bwd")
    (dwo,) = _mm([merged], [dyb], [(0, 0, 0)], 1, _first, [BF16], m=d, n=d, k=t, ta=True, bm=d, bn=d, bk=1024, name="mix_dwo")
    (doa,) = _mm([dya], [p["wa"]], [(0, 0, 0)], 1, _first, [F32], m=t, n=w, k=d, tb=True, bm=1024, bn=w, bk=d, name="branch_a_bwd")
    (dwa,) = _mm([oa], [dya], [(0, 0, 0)], 1, _first, [BF16], m=w, n=d, k=t, ta=True, bm=w, bn=d, bk=1024, name="branch_a_dw")
    (dob,) = _mm([dyb_], [p["wbt"]], [(0, 0, 0)], 1, _first, [F32], m=t, n=ATT_GW, k=d, bm=1024, bn=ATT_GW, bk=d,
                 name="branch_b_bwd")
    (dwbt,) = _mm([dyb_], [ob], [(0, 0, 0)], 1, _first, [BF16], m=d, n=ATT_GW, k=t, ta=True, bm=d, bn=ATT_GW, bk=1024,
                  name="branch_b_dw")
    do_raw, dhg, dgout = _post_a_bwd(o_raw, z["h"], p["gout"], doa)
    do_g, dlse_g = _merge_b_bwd(o_g, lse_g, dob)
    dq_g, dk_g, dv_g = zip(*[_attn_bwd(qb, kb, vb, o_g[g], lse_g[g], do_g[g], dlse_g[g], g) for g in range(ATT_GROUPS)])
    dzq, dzk, dzv, dqn, dkn = _qk_prep_bwd(z["q"], z["k"], dq_g, dk_g, dv_g, p["qn"], p["kn"], cos, sin)
    dhq, dhf, dhi, lbsum = _hgrn_bwd(z["h"], p["lb3"], states, do_raw)
    dz = jnp.concatenate([dhq, dhf, dhi, dhg, dzq, dzk, dzv, dga, dgb], axis=1)
    pw = dz.shape[1]
    (dhm,) = _mm([dz], [p["wint"]], [(0, 0, 0)], 1, _first, [F32], m=t, n=d, k=pw, bm=1024, bn=d, bk=1536, name="mix_in_bwd")
    (dwint,) = _mm([dz], [hm], [(0, 0, 0)], 1, _first, [BF16], m=pw, n=d, k=t, ta=True, bm=1536, bn=d, bk=1024, name="mix_in_dw")
    dx, dgm = _norm_bwd(x, p["gm"], dhm, dy)
    return dx, dict(gm=dgm, wint=dwint, lbsum=lbsum, gout=dgout, qn=dqn, kn=dkn, wa=dwa, wbt=dwbt, wo=dwo)


def _rope_tables(t):
    pos = jnp.arange(t, dtype=F32)
    inv = ROPE_THETA ** (-jnp.arange(0, ATT_DH, 2, dtype=F32) / ATT_DH)
    ang = pos[:, None] * inv[None, :]
    ang = jnp.concatenate([ang, ang], axis=-1)
    return jnp.cos(ang), jnp.sin(ang)


def _lower_bounds(logits):
    lb = jnp.cumsum(jax.nn.softmax(logits, axis=0), axis=0)
    return lb - lb[0:1]


def _head_gain(g):
    return jnp.tile(g[:, None, :], (1, ATT_HEADS, 1)).reshape(1, ATT_GROUPS * ATT_GW)


SMALL_GRADS = ("ffn1_norm", "mix_norm", "lbsum", "hgrn_out_norm", "attn_q_norm", "attn_k_norm", "ffn2_norm")


def _local_step(x, target, small, fetch, emit):
    t = x.shape[0]
    depth = small["ffn1_norm"].shape[0]
    cos, sin = _rope_tables(t)
    lb_all = _lower_bounds(small["hgrn_lb_logits"])
    saved = []
    for l in range(depth):
        w1 = fetch("ffn1", l, x)
        x, s1 = _ffn_fwd(x, small["ffn1_norm"][l], w1["w1t"], w1["w1o"], "1")
        p = dict(gm=small["mix_norm"][l], wint=fetch("min", l, x)["wint"], lb3=lb_all[l].reshape(-1, 1, HG_DK),
                 gout=small["hgrn_out_norm"][l], qn=_head_gain(small["attn_q_norm"][l]),
                 kn=_head_gain(small["attn_k_norm"][l]), late=functools.partial(fetch, "mout", l))
        x, sm = _mix_fwd(x, p, cos, sin)
        w2 = fetch("ffn2", l, x)
        x, s2 = _ffn_fwd(x, small["ffn2_norm"][l], w2["w2t"], w2["w2o"], "2")
        saved.append((p, w1, w2, s1, sm, s2))
    loss, dx = _loss_head(x, target)
    gsmall = {k: [None] * depth for k in SMALL_GRADS}
    tok = jnp.zeros((), F32)
    for l in reversed(range(depth)):
        p, w1, w2, s1, sm, s2 = saved[l]
        dx, gsmall["ffn2_norm"][l], dw2t, dw2o = _ffn_bwd(dx, s2, small["ffn2_norm"][l], w2["w2t"], w2["w2o"], "2", tok)
        tok = emit("ffn2", l, dict(w2t=dw2t, w2o=dw2o), None)
        dx, gm = _mix_bwd(dx, sm, p, cos, sin, tok)
        tok = emit("mix", l, {k: gm[k] for k in ("wint", "wa", "wbt", "wo")}, None)
        gsmall["mix_norm"][l], gsmall["lbsum"][l], gsmall["hgrn_out_norm"][l] = gm["gm"], gm["lbsum"], gm["gout"]
        for k, src in (("attn_q_norm", "qn"), ("attn_k_norm", "kn")):
            gsmall[k][l] = jnp.sum(gm[src].reshape(ATT_GROUPS, ATT_HEADS, ATT_DH), axis=1)
        dx, gsmall["ffn1_norm"][l], dw1t, dw1o = _ffn_bwd(dx, s1, small["ffn1_norm"][l], w1["w1t"], w1["w1o"], "1", tok)
        final = ({k: jnp.stack(v) for k, v in gsmall.items()}, loss) if l == 0 else None
        tok = emit("ffn1", l, dict(w1t=dw1t, w1o=dw1o), final)
    return dx


def _exchange(srcs, modes, name):
    n = len(srcs)

    def body(*refs):
        src, dst = refs[:n], refs[n:2 * n]
        send_sems, recv_sems, local_sems = refs[2 * n:]
        x, y, c = lax.axis_index("x"), lax.axis_index("y"), lax.axis_index("c")
        me = 4 * x + 2 * y + c

        def block(k, j):
            return src[k] if modes[k] == "gather" else src[k].at[:, j]

        local = [pltpu.make_async_copy(block(k, me), dst[k].at[:, me], local_sems.at[k]) for k in range(n)]
        for cp in local:
            cp.start()
        remote = []
        for p in range(1, N_DEV):
            peer = (1 - x if p & 4 else x, 1 - y if p & 2 else y, 1 - c if p & 1 else c)
            peer_id = jnp.bitwise_xor(me, p)
            for k in range(n):
                cp = pltpu.make_async_remote_copy(
                    src_ref=block(k, peer_id), dst_ref=dst[k].at[:, me], send_sem=send_sems.at[k, p - 1],
                    recv_sem=recv_sems.at[k, p - 1], device_id=peer, device_id_type=MESH)
                cp.start()
                remote.append((cp, pltpu.make_async_remote_copy(
                    src_ref=block(k, peer_id), dst_ref=dst[k].at[:, peer_id], send_sem=send_sems.at[k, p - 1],
                    recv_sem=recv_sems.at[k, p - 1], device_id=peer, device_id_type=MESH)))
        for sent, arriving in remote:
            arriving.wait_recv()
        for sent, arriving in remote:
            sent.wait_send()
        for cp in local:
            cp.wait()

    def dst_shape(a, mode):
        return (a.shape[0], N_DEV) + a.shape[1:] if mode == "gather" else a.shape

    anyspec = pl.BlockSpec(memory_space=pl.ANY)
    return pl.pallas_call(
        body,
        out_shape=[jax.ShapeDtypeStruct(dst_shape(a, m), a.dtype) for a, m in zip(srcs, modes)],
        in_specs=[anyspec] * n,
        out_specs=[anyspec] * n,
        scratch_shapes=[pltpu.SemaphoreType.DMA((n, N_DEV - 1)), pltpu.SemaphoreType.DMA((n, N_DEV - 1)),
                        pltpu.SemaphoreType.DMA((n,))],
        compiler_params=pltpu.CompilerParams(has_side_effects=True),
        name=name,
    )(*srcs)


_HBM = pl.BlockSpec(memory_space=pltpu.HBM)
_SEM = pl.BlockSpec(memory_space=pltpu.SEMAPHORE)
_EFFECT = pltpu.SideEffectType.DATAFLOW_SIDE_EFFECTING


def _peer(p):
    x, y, c = lax.axis_index("x"), lax.axis_index("y"), lax.axis_index("c")
    me = 4 * x + 2 * y + c
    return (1 - x if p & 4 else x, 1 - y if p & 2 else y, 1 - c if p & 1 else c), jnp.bitwise_xor(me, p), me


def _xchg_copy(src, land, mode, send_sems, recv_sems, k, p, arriving):
    peer, peer_id, me = _peer(p)
    block = src if mode == "gather" else src.at[peer_id]
    return pltpu.make_async_remote_copy(
        src_ref=block, dst_ref=land.at[peer_id if arriving else me], send_sem=send_sems.at[k * (N_DEV - 1) + p - 1],
        recv_sem=recv_sems.at[k * (N_DEV - 1) + p - 1], device_id=peer, device_id_type=MESH)


def _xchg_start(srcs, modes, groups, name):
    n, ng = len(srcs), len(groups)
    me = 4 * lax.axis_index("x") + 2 * lax.axis_index("y") + lax.axis_index("c")
    lands = []
    for a, mode in zip(srcs, modes):
        own = a[None] if mode == "gather" else lax.dynamic_slice_in_dim(a, me, 1, axis=0)
        r, c = a.shape[-2:]
        lands.append(lax.dynamic_update_slice(lax.empty((N_DEV, r, c), a.dtype), own, (me, 0, 0)))

    def body(*refs):
        src, land = refs[:n], refs[n:2 * n]
        sems = refs[2 * n:2 * n + 2 * ng]
        token = refs[2 * n + 2 * ng + 2 * n]
        for p in range(1, N_DEV):
            for gi, idx in enumerate(groups):
                for ki, k in enumerate(idx):
                    _xchg_copy(src[k], land[k], modes[k], sems[2 * gi], sems[2 * gi + 1], ki, p, False).start()
        token[...] = jnp.zeros_like(token)

    sem_shapes = []
    for idx in groups:
        sem_shapes += [pltpu.SemaphoreType.DMA((len(idx) * (N_DEV - 1),))] * 2
    outs = pl.pallas_call(
        body,
        out_shape=sem_shapes + [pltpu.HBM(a.shape, a.dtype) for a in srcs] + [pltpu.HBM(a.shape, a.dtype) for a in lands]
        + [jax.ShapeDtypeStruct((8, 128), F32)],
        in_specs=[_HBM] * (2 * n),
        out_specs=[_SEM] * (2 * ng) + [_HBM] * (2 * n) + [pl.BlockSpec(memory_space=pltpu.VMEM)],
        input_output_aliases={i: 2 * ng + i for i in range(2 * n)},
        compiler_params=pltpu.CompilerParams(has_side_effects=_EFFECT),
        name=name,
    )(*[pltpu.with_memory_space_constraint(a, pltpu.HBM) for a in list(srcs) + lands])
    sems = [(outs[2 * gi], outs[2 * gi + 1]) for gi in range(ng)]
    return sems, outs[2 * ng:2 * ng + n], outs[2 * ng + n:2 * ng + 2 * n], outs[-1]


def _xchg_wait(srcs, lands, modes, sems, after, name):
    n = len(srcs)

    def body(*refs):
        src, land = refs[:n], refs[n:2 * n]
        send_sems, recv_sems = refs[2 * n], refs[2 * n + 1]
        for p in range(1, N_DEV):
            for k in range(n):
                cp = _xchg_copy(src[k], land[k], modes[k], send_sems, recv_sems, k, p, True)
                cp.wait_send()
                cp.wait_recv()

    outs = pl.pallas_call(
        body,
        out_shape=[pltpu.HBM(a.shape, a.dtype) for a in list(srcs) + list(lands)],
        in_specs=[_HBM] * (2 * n) + [_SEM, _SEM, pl.BlockSpec(memory_space=pl.ANY)],
        out_specs=[_HBM] * (2 * n),
        input_output_aliases={i: i for i in range(2 * n)},
        compiler_params=pltpu.CompilerParams(has_side_effects=_EFFECT),
        name=name,
    )(*srcs, *lands, sems[0], sems[1], after)
    return outs[n:]


def _sum_slots(land):
    g, _, r, c = land.shape
    br = r // 2 if (r % 32 == 0 and r >= 256) else r

    def body(l_ref, o_ref):
        acc = l_ref[0, 0].astype(F32)
        for j in range(1, N_DEV):
            acc = acc + l_ref[0, j].astype(F32)
        o_ref[0] = acc

    return pl.pallas_call(
        body,
        out_shape=jax.ShapeDtypeStruct((g, r, c), F32),
        grid=(g, r // br),
        in_specs=[pl.BlockSpec((1, N_DEV, br, c), lambda i, j: (i, 0, j, 0))],
        out_specs=pl.BlockSpec((1, br, c), lambda i, j: (i, j, 0)),
        compiler_params=_cparams(("parallel", "parallel")),
        name="sum_slots",
    )(land)


def _adamw(w, g, m, v):
    shape = w.shape
    cols = shape[-1]
    rows = int(np.prod(shape[:-1]))
    bm = max(b for b in range(8, 257, 8) if rows % b == 0) if rows % 8 == 0 else rows
    c1 = 1.0 - ADAM_B1 ** ADAM_STEP
    c2 = 1.0 - ADAM_B2 ** ADAM_STEP

    def fn(ins, consts):
        wv, gv, mv, vv = ins
        m2 = ADAM_B1 * mv + (1.0 - ADAM_B1) * gv
        v2 = ADAM_B2 * vv + (1.0 - ADAM_B2) * (gv * gv)
        delta = -ADAM_LR * ((m2 / c1) / (jnp.sqrt(v2 / c2) + ADAM_EPS) + ADAM_WD * wv)
        return [delta, m2, v2], []

    outs, _ = _rowwise(fn, [(a.reshape(rows, cols), cols, 0) for a in (w, g, m, v)], [], [(cols, F32)] * 3, [],
                       bm=bm, name="adamw")
    return [o.reshape(shape) for o in outs]


BIG = ("w1t", "w1o", "wint", "wa", "wbt", "wo", "w2t", "w2o")
FETCH_GROUPS = dict(ffn1=("w1t", "w1o"), min=("wint",), mout=("wa", "wbt", "wo"), ffn2=("w2t", "w2o"))
SMALL_ROWS = (("ffn1_norm", 0), ("mix_norm", 2), ("lbsum", 4), ("hgrn_out_norm", 6), ("ffn2_norm", 8),
              ("attn_q_norm", 10), ("attn_k_norm", 12))
SMALL_PACK_ROWS = 16


def kernel(x, ffn1_norm, ffn1_w_in, ffn1_w_out, mix_norm, w_in, hgrn_lb_logits, hgrn_out_norm, attn_q_norm, attn_k_norm, w_branch_a, w_branch_b, w_out, ffn2_norm, ffn2_w_in, ffn2_w_out, loss_target, m_ffn1_norm, m_ffn1_w_in, m_ffn1_w_out, m_mix_norm, m_w_in, m_hgrn_lb_logits, m_hgrn_out_norm, m_attn_q_norm, m_attn_k_norm, m_w_branch_a, m_w_branch_b, m_w_out, m_ffn2_norm, m_ffn2_w_in, m_ffn2_w_out, v_ffn1_norm, v_ffn1_w_in, v_ffn1_w_out, v_mix_norm, v_w_in, v_hgrn_lb_logits, v_hgrn_out_norm, v_attn_q_norm, v_attn_k_norm, v_w_branch_a, v_w_branch_b, v_w_out, v_ffn2_norm, v_ffn2_w_in, v_ffn2_w_out):
    names = ("ffn1_norm", "ffn1_w_in", "ffn1_w_out", "mix_norm", "w_in", "hgrn_lb_logits", "hgrn_out_norm", "attn_q_norm",
             "attn_k_norm", "w_branch_a", "w_branch_b", "w_out", "ffn2_norm", "ffn2_w_in", "ffn2_w_out")
    w = dict(zip(names, (ffn1_norm, ffn1_w_in, ffn1_w_out, mix_norm, w_in, hgrn_lb_logits, hgrn_out_norm, attn_q_norm,
                         attn_k_norm, w_branch_a, w_branch_b, w_out, ffn2_norm, ffn2_w_in, ffn2_w_out)))
    m = dict(zip(names, (m_ffn1_norm, m_ffn1_w_in, m_ffn1_w_out, m_mix_norm, m_w_in, m_hgrn_lb_logits, m_hgrn_out_norm,
                         m_attn_q_norm, m_attn_k_norm, m_w_branch_a, m_w_branch_b, m_w_out, m_ffn2_norm, m_ffn2_w_in, m_ffn2_w_out)))
    v = dict(zip(names, (v_ffn1_norm, v_ffn1_w_in, v_ffn1_w_out, v_mix_norm, v_w_in, v_hgrn_lb_logits, v_hgrn_out_norm,
                         v_attn_q_norm, v_attn_k_norm, v_w_branch_a, v_w_branch_b, v_w_out, v_ffn2_norm, v_ffn2_w_in, v_ffn2_w_out)))
    depth, d = ffn1_norm.shape

    def tr(a):
        return jnp.swapaxes(a, 1, 2)

    shard = dict(w1t=tr(ffn1_w_in), w1o=ffn1_w_out, wint=tr(w_in), wa=w_branch_a,
                 wbt=tr(w_branch_b).reshape(depth, -1, d), wo=w_out, w2t=tr(ffn2_w_in), w2o=ffn2_w_out)
    order = [(g, l) for l in range(depth) for g in ("ffn1", "min", "mout", "ffn2")]
    flat = [(g, l, k) for g, l in order for k in FETCH_GROUPS[g]]
    groups, pos = [], 0
    for g, l in order:
        groups.append(list(range(pos, pos + len(FETCH_GROUPS[g]))))
        pos += len(FETCH_GROUPS[g])
    g_sems, g_srcs, g_lands, _ = _xchg_start([shard[k][l].astype(BF16) for _, l, k in flat], ["gather"] * len(flat),
                                             groups, "gather_start")

    def fetch(group, l, after):
        gi = order.index((group, l))
        idx = groups[gi]
        lands = _xchg_wait([g_srcs[i] for i in idx], [g_lands[i] for i in idx], ["gather"] * len(idx), g_sems[gi], after,
                           f"gather_wait_{group}{l}")
        out = {}
        for k, land in zip(FETCH_GROUPS[group], lands):
            out[k] = land.reshape(d, -1) if k == "wbt" else land.reshape(-1, d)
        return out

    pending = []

    def emit(group, l, g, final):
        keys = list(g)
        srcs = [g[k].reshape(N_DEV, -1, d) for k in keys]
        modes = ["scatter"] * len(keys)
        if final is not None:
            gsmall, loss = final
            pack = jnp.zeros((SMALL_PACK_ROWS, d), F32)
            for k, r0 in SMALL_ROWS:
                rows = gsmall[k].reshape(depth, -1)
                pack = pack.at[r0:r0 + depth, :rows.shape[1]].set(rows)
            srcs.append(pack.at[14, :].set(loss))
            modes.append("gather")
            keys.append("small")
        sems, s_thru, l_thru, token = _xchg_start(srcs, modes, [list(range(len(srcs)))], f"grads_start_{group}{l}")
        pending.append((group, l, keys, modes, sems[0], s_thru, l_thru))
        return token[0, 0]

    small = {k: w[k] for k in ("ffn1_norm", "mix_norm", "hgrn_lb_logits", "hgrn_out_norm", "attn_q_norm", "attn_k_norm", "ffn2_norm")}
    dx = _local_step(x[0], loss_target[0], small, fetch, emit)

    summed = {}
    for group, l, keys, modes, sems, s_thru, l_thru in pending:
        lands = _xchg_wait(s_thru, l_thru, modes, sems, dx, f"grads_wait_{group}{l}")
        for k, land in zip(keys, lands):
            summed[k, l] = _sum_slots(land[None])[0]
    gsum = {k: jnp.stack([summed[k, l] for l in range(depth)]) for k in BIG}
    tot = summed["small", 0]

    grads = {}
    for k, r0 in SMALL_ROWS:
        shp = (depth,) + (w[k].shape[1:] if k != "lbsum" else (d,))
        grads[k] = tot[r0:r0 + depth, :int(np.prod(shp[1:]))].reshape(shp)
    _, lb_vjp = jax.vjp(_lower_bounds, hgrn_lb_logits)
    grads["hgrn_lb_logits"] = lb_vjp(grads.pop("lbsum"))[0]
    grads["ffn1_w_in"], grads["ffn1_w_out"] = tr(gsum["w1t"]), gsum["w1o"]
    grads["w_in"], grads["w_branch_a"] = tr(gsum["wint"]), gsum["wa"]
    grads["w_branch_b"] = tr(gsum["wbt"].reshape(depth, d // N_DEV, -1))
    grads["w_out"] = gsum["wo"]
    grads["ffn2_w_in"], grads["ffn2_w_out"] = tr(gsum["w2t"]), gsum["w2o"]

    upd = {k: _adamw(w[k], grads[k], m[k], v[k]) for k in names}
    return (tot[14, 0], dx[None], *[grads[k] for k in names], *[upd[k][0] for k in names],
            *[upd[k][1] for k in names], *[upd[k][2] for k in names])
```

```python
import functools
import math

import jax
import jax.numpy as jnp
import numpy as np
from jax import lax
from jax.experimental import pallas as pl
from jax.experimental.pallas import tpu as pltpu

F32 = jnp.float32
BF16 = jnp.bfloat16

N_DEV = 8
EPS = 1e-6
HG_DK = 128
HG_CHUNK = 64
HG_SUB = 16
ATT_PATTERNS = ((128, 1), (512, 4), (2048, 16))
ATT_GROUPS = 3
ATT_HEADS = 4
ATT_DH = 128
ATT_BLK = 128
ROPE_THETA = 10000.0
ADAM_LR, ADAM_B1, ADAM_B2, ADAM_EPS, ADAM_WD, ADAM_STEP = 0.001, 0.9, 0.999, 1e-08, 0.01, 10
VMEM_LIMIT_BYTES = 56 * 1024 * 1024
MESH = pl.DeviceIdType.MESH


def _cparams(sem, **kw):
    return pltpu.CompilerParams(dimension_semantics=sem, vmem_limit_bytes=VMEM_LIMIT_BYTES, **kw)


def _sigmoid(x):
    return 1.0 / (1.0 + jnp.exp(-x))


def _mm(a_list, b_list, pairs, n_acc, fin, out_dtypes, *, m, n, k, ta=False, tb=False, bm, bn, bk,
        b_off=None, extras=(), e_off=None, name):
    bm, bn, bk = min(bm, m), min(bn, n), min(bk, k)
    assert m % bm == 0 and n % bn == 0 and k % bk == 0, (name, m, n, k, bm, bn, bk)
    nk = k // bk
    b_off = b_off or [(0, 0)] * len(b_list)
    e_off = e_off or [0] * len(extras)
    na, nb, ne = len(a_list), len(b_list), len(extras)
    dn = (((0,) if ta else (1,), (1,) if tb else (0,)), ((), ()))

    def body(*refs):
        a_refs, b_refs = refs[:na], refs[na:na + nb]
        e_refs = refs[na + nb:na + nb + ne]
        o_refs = refs[na + nb + ne:na + nb + ne + len(out_dtypes)]
        acc_refs = refs[na + nb + ne + len(out_dtypes):]
        kk = pl.program_id(2)
        parts = [None] * n_acc
        for ai, bi, ci in pairs:
            p = lax.dot_general(a_refs[ai][...], b_refs[bi][...], dn, preferred_element_type=F32)
            parts[ci] = p if parts[ci] is None else parts[ci] + p

        def finish(accs):
            outs = fin(accs, [e[...] for e in e_refs])
            for o_ref, o in zip(o_refs, outs):
                o_ref[...] = o.astype(o_ref.dtype)

        if nk == 1:
            finish(parts)
        else:
            @pl.when(kk == 0)
            def _():
                for c in range(n_acc):
                    acc_refs[c][...] = parts[c]

            @pl.when(kk > 0)
            def _():
                for c in range(n_acc):
                    acc_refs[c][...] += parts[c]

            @pl.when(kk == nk - 1)
            def _():
                finish([acc_refs[c][...] for c in range(n_acc)])

    a_spec = pl.BlockSpec((bk, bm), lambda i, j, q: (q, i)) if ta else pl.BlockSpec((bm, bk), lambda i, j, q: (i, q))

    def b_spec(off):
        on, ok = off
        if tb:
            return pl.BlockSpec((bn, bk), lambda i, j, q: (j + on, q + ok))
        return pl.BlockSpec((bk, bn), lambda i, j, q: (q + ok, j + on))

    mn_spec = pl.BlockSpec((bm, bn), lambda i, j, q: (i, j))
    outs = pl.pallas_call(
        body,
        out_shape=[jax.ShapeDtypeStruct((m, n), d) for d in out_dtypes],
        grid=(m // bm, n // bn, nk),
        in_specs=[a_spec] * na + [b_spec(o) for o in b_off]
        + [pl.BlockSpec((bm, bn), lambda i, j, q, o=o: (i, j + o)) for o in e_off],
        out_specs=[mn_spec] * len(out_dtypes),
        scratch_shapes=[pltpu.VMEM((bm, bn), F32) for _ in range(n_acc if nk > 1 else 0)],
        compiler_params=_cparams(("parallel", "parallel", "arbitrary")),
        name=name,
    )(*a_list, *b_list, *extras)
    return outs


def _first(accs, ex):
    return (accs[0],)


def _rowwise(fn, ins, consts, out_defs, sum_widths, *, bm, name):
    ins = [tuple(e) + (1,) * (4 - len(e)) for e in ins]
    out_defs = [tuple(e) + (1,) * (3 - len(e)) for e in out_defs]
    t = ins[0][0].shape[-2] * ins[0][3]
    bm = min(bm, t)
    assert t % bm == 0, (name, t, bm)
    ni, nc, no, ns = len(ins), len(consts), len(out_defs), len(sum_widths)
    strided = [w for _, w, _, d in ins if d > 1] + [w for w, _, d in out_defs if d > 1]

    def body(*refs):
        i_refs, c_refs = refs[:ni], refs[ni:ni + nc]
        o_refs, s_refs = refs[ni + nc:ni + nc + no], refs[ni + nc + no:ni + nc + no + ns]
        scratch = list(refs[ni + nc + no + ns:])
        vals = []
        for ref, (_, w, _, d) in zip(i_refs, ins):
            if d == 1:
                vals.append(ref[...])
                continue
            s = scratch.pop(0)
            for r in range(d):
                for c in range(w // 128):
                    s.at[c][pl.ds(r, bm // d, stride=d), :] = ref[r, :, c * 128:(c + 1) * 128].astype(F32)
            vals.append(jnp.concatenate([s[c] for c in range(w // 128)], axis=1))
        outs, sums = fn(vals, [r[...] for r in c_refs])
        for o_ref, o, (w, _, d) in zip(o_refs, outs, out_defs):
            if d == 1:
                o_ref[...] = o.astype(o_ref.dtype)
                continue
            s = scratch.pop(0)
            for c in range(w // 128):
                s[c] = o[:, c * 128:(c + 1) * 128].astype(F32)
            for r in range(d):
                for c in range(w // 128):
                    o_ref[r, :, c * 128:(c + 1) * 128] = s.at[c][pl.ds(r, bm // d, stride=d), :].astype(o_ref.dtype)
        if ns:
            first = pl.program_id(0) == 0

            @pl.when(first)
            def _():
                for s_ref, s in zip(s_refs, sums):
                    s_ref[...] = s

            @pl.when(jnp.logical_not(first))
            def _():
                for s_ref, s in zip(s_refs, sums):
                    s_ref[...] += s

    def win(width, cb, d):
        if d > 1:
            return pl.BlockSpec((d, bm // d, width), lambda i: (0, i, 0))
        return pl.BlockSpec((bm, width), lambda i: (i, cb))

    res = pl.pallas_call(
        body,
        out_shape=[jax.ShapeDtypeStruct((t, w) if d == 1 else (d, t // d, w), dt) for w, dt, d in out_defs]
        + [jax.ShapeDtypeStruct((8, w), F32) for w in sum_widths],
        grid=(t // bm,),
        in_specs=[win(w, cb, d) for _, w, cb, d in ins] + [pl.BlockSpec(c.shape, lambda i, nd=c.ndim: (0,) * nd) for c in consts],
        out_specs=[win(w, 0, d) for w, _, d in out_defs] + [pl.BlockSpec((8, w), lambda i: (0, 0)) for w in sum_widths],
        scratch_shapes=[pltpu.VMEM((w // 128, bm, 128), F32) for w in strided],
        compiler_params=_cparams(("arbitrary",) if ns else ("parallel",)),
        name=name,
    )(*[e[0] for e in ins], *consts)
    return res[:no], [jnp.sum(s, axis=0) for s in res[no:]]


def _colsum8(x):
    bm, w = x.shape
    return jnp.sum(x.reshape(bm // 8, 8, w), axis=0)


def _tri(n, upper=False):
    r = lax.broadcasted_iota(jnp.int32, (n, n), 0)
    c = lax.broadcasted_iota(jnp.int32, (n, n), 1)
    return (c >= r) if upper else (c <= r)


def _exact_tri_matmul(tri_bf16, x):
    x0 = x.astype(BF16)
    r1 = x - x0.astype(F32)
    x1 = r1.astype(BF16)
    x2 = (r1 - x1.astype(F32)).astype(BF16)
    w = x.shape[1]
    y = jnp.dot(tri_bf16, jnp.concatenate([x0, x1, x2], axis=1), preferred_element_type=F32)
    return y[:, :w] + y[:, w:2 * w] + y[:, 2 * w:]


def _dot_nt(a, b):
    return lax.dot_general(a, b, (((1,), (1,)), ((), ())), preferred_element_type=F32)


def _dot_tn(a, b):
    return lax.dot_general(a, b, (((0,), (0,)), ((), ())), preferred_element_type=F32)


def _dot(a, b):
    return jnp.dot(a, b, preferred_element_type=F32)


def _hg_gates(hq, hf, lb):
    sq = _sigmoid(hq)
    q = hq * sq
    sg = _sigmoid(hf)
    f = lb + (1.0 - lb) * sg
    return q, sq, sg, f


def _hg_intra(q, kk, g):
    c = q.shape[0]
    rows = lax.broadcasted_iota(jnp.int32, (c, 1), 0)
    a_rows, qts, kts, eqs, eks = [], [], [], [], []
    for i in range(c // HG_SUB):
        lo = i * HG_SUB
        ref = g[lo - 1:lo, :] if i else jnp.zeros_like(g[0:1, :])
        eq = jnp.exp(g[lo:lo + HG_SUB, :] - ref)
        ek = jnp.exp(jnp.where(rows < lo + HG_SUB, ref - g, 0.0))
        qt = q[lo:lo + HG_SUB, :] * eq
        kt = kk * ek
        a = _dot_nt(qt.astype(BF16), kt.astype(BF16))
        tpos = lo + lax.broadcasted_iota(jnp.int32, (HG_SUB, c), 0)
        spos = lax.broadcasted_iota(jnp.int32, (HG_SUB, c), 1)
        a_rows.append(jnp.where(spos <= tpos, a, 0.0))
        qts.append(qt), kts.append(kt), eqs.append(eq), eks.append(ek)
    return jnp.concatenate(a_rows, axis=0), qts, kts, eqs, eks


def _hgrn_fwd(zh, lb3, *, tb=512):
    t = zh.shape[0]
    nh = lb3.shape[0]
    c = HG_CHUNK
    tb = min(tb, t)
    nchunk = tb // c

    def body(hq_ref, hf_ref, hi_ref, lb_ref, o_ref, st_ref, state):
        @pl.when(pl.program_id(1) == 0)
        def _():
            state[...] = jnp.zeros_like(state)

        lb = lb_ref[0]
        tril = _tri(c).astype(BF16)

        def chunk(ci, carry):
            sl = pl.ds(pl.multiple_of(ci * c, c), c)
            q, _, _, f = _hg_gates(hq_ref[sl, :], hf_ref[sl, :], lb)
            v = hi_ref[sl, :]
            kk = 1.0 - f
            g = _exact_tri_matmul(tril, jnp.log(f))
            a, _, _, _, _ = _hg_intra(q, kk, g)
            st = state[...]
            st_ref[0, ci] = st
            vb = v.astype(BF16)
            o = _dot(a.astype(BF16), vb) + _dot_nt((q * jnp.exp(g)).astype(BF16), st.astype(BF16))
            o_ref[sl, :] = o
            glast = g[c - 1:c, :]
            kg = kk * jnp.exp(glast - g)
            state[...] = st * jnp.exp(glast) + _dot_tn(vb, kg.astype(BF16))
            return carry

        lax.fori_loop(0, nchunk, chunk, 0)

    def col(cb):
        return pl.BlockSpec((tb, HG_DK), lambda h, i: (i, cb * nh + h))

    return pl.pallas_call(
        body,
        out_shape=[jax.ShapeDtypeStruct((t, nh * HG_DK), F32), jax.ShapeDtypeStruct((nh, t // c, HG_DK, HG_DK), F32)],
        grid=(nh, t // tb),
        in_specs=[col(0), col(1), col(2), pl.BlockSpec((1, 1, HG_DK), lambda h, i: (h, 0, 0))],
        out_specs=[pl.BlockSpec((tb, HG_DK), lambda h, i: (i, h)),
                   pl.BlockSpec((1, nchunk, HG_DK, HG_DK), lambda h, i: (h, i, 0, 0))],
        scratch_shapes=[pltpu.VMEM((HG_DK, HG_DK), F32)],
        compiler_params=_cparams(("parallel", "arbitrary")),
        name="hgrn_fwd",
    )(zh, zh, zh, lb3)


def _hgrn_bwd(zh, lb3, states, d_o, *, tb=512):
    t = zh.shape[0]
    nh = lb3.shape[0]
    c = HG_CHUNK
    tb = min(tb, t)
    nchunk = tb // c
    nblk = t // tb

    def body(hq_ref, hf_ref, hi_ref, lb_ref, st_ref, do_ref, dq_ref, df_ref, dv_ref, dlb_ref, dstate):
        @pl.when(pl.program_id(1) == 0)
        def _():
            dstate[...] = jnp.zeros_like(dstate)
            dlb_ref[...] = jnp.zeros_like(dlb_ref)

        lb = lb_ref[0]
        tril = _tri(c).astype(BF16)
        triu = _tri(c, upper=True).astype(BF16)
        last_row = lax.broadcasted_iota(jnp.int32, (c, 1), 0) == c - 1

        def chunk(j, carry):
            ci = nchunk - 1 - j
            sl = pl.ds(pl.multiple_of(ci * c, c), c)
            hq, hf = hq_ref[sl, :], hf_ref[sl, :]
            q, sq, sg, f = _hg_gates(hq, hf, lb)
            v = hi_ref[sl, :]
            kk = 1.0 - f
            g = _exact_tri_matmul(tril, jnp.log(f))
            a, qts, kts, eqs, eks = _hg_intra(q, kk, g)
            st = st_ref[0, ci]
            dst = dstate[...]
            do = do_ref[sl, :]
            dob, vb = do.astype(BF16), v.astype(BF16)
            glast = g[c - 1:c, :]
            eg = jnp.exp(g)
            egl = jnp.exp(glast - g)
            qg = q * eg
            kg = kk * egl
            dv = _dot_tn(a.astype(BF16), dob) + _dot_nt(kg.astype(BF16), dst.astype(BF16))
            da = jnp.where(_tri(c), _dot_nt(dob, vb), 0.0).astype(BF16)
            dq_parts, dgq_parts = [], []
            dk = jnp.zeros_like(kk)
            dgk = jnp.zeros_like(kk)
            for i in range(c // HG_SUB):
                da_i = da[i * HG_SUB:(i + 1) * HG_SUB, :]
                ktb, qtb = kts[i].astype(BF16), qts[i].astype(BF16)
                xi = _dot(da_i, ktb)
                yi = _dot_tn(da_i, qtb)
                dq_parts.append(xi * eqs[i])
                dk = dk + yi * eks[i]
                dgq_parts.append(xi * qtb.astype(F32))
                dgk = dgk + yi * ktb.astype(F32)
            dq_inter = _dot(dob, st.astype(BF16)) * eg
            dq = jnp.concatenate(dq_parts, axis=0) + dq_inter
            dk_state = _dot(vb, dst.astype(BF16)) * egl
            dk = dk + dk_state
            dg = jnp.concatenate(dgq_parts, axis=0) - dgk + q * dq_inter - kk * dk_state
            dgl = jnp.sum(kk * dk_state, axis=0, keepdims=True) + jnp.exp(glast) * jnp.sum(st * dst, axis=0, keepdims=True)
            dg = dg + jnp.where(last_row, dgl, 0.0)
            dlogf = _exact_tri_matmul(triu, dg)
            dfv = dlogf / f - dk
            dq_ref[sl, :] = (dq * (sq * (1.0 + hq * (1.0 - sq)))).astype(dq_ref.dtype)
            df_ref[sl, :] = (dfv * (1.0 - lb) * sg * (1.0 - sg)).astype(df_ref.dtype)
            dv_ref[sl, :] = dv.astype(dv_ref.dtype)
            dlb_ref[0] += jnp.sum(dfv * (1.0 - sg), axis=0, keepdims=True)
            dstate[...] = dst * jnp.exp(glast) + _dot_tn(dob, qg.astype(BF16))
            return carry

        lax.fori_loop(0, nchunk, chunk, 0)

    def col(cb):
        return pl.BlockSpec((tb, HG_DK), lambda h, i: (nblk - 1 - i, cb * nh + h))

    ocol = pl.BlockSpec((tb, HG_DK), lambda h, i: (nblk - 1 - i, h))
    w = nh * HG_DK
    dq, df, dv, dlb = pl.pallas_call(
        body,
        out_shape=[jax.ShapeDtypeStruct((t, w), BF16)] * 3 + [jax.ShapeDtypeStruct((nh, 1, HG_DK), F32)],
        grid=(nh, nblk),
        in_specs=[col(0), col(1), col(2), pl.BlockSpec((1, 1, HG_DK), lambda h, i: (h, 0, 0)),
                  pl.BlockSpec((1, nchunk, HG_DK, HG_DK), lambda h, i: (h, nblk - 1 - i, 0, 0)), ocol],
        out_specs=[ocol, ocol, ocol, pl.BlockSpec((1, 1, HG_DK), lambda h, i: (h, 0, 0))],
        scratch_shapes=[pltpu.VMEM((HG_DK, HG_DK), F32)],
        compiler_params=_cparams(("parallel", "arbitrary")),
        name="hgrn_bwd",
    )(zh, zh, zh, lb3, states, d_o)
    return dq, df, dv, dlb.reshape(w)


NEG = -1e30
ATT_GW = ATT_HEADS * ATT_DH


def _att_scores(q, kp, kc, has_prev):
    scale = ATT_DH ** -0.5
    i = lax.broadcasted_iota(jnp.int32, (ATT_BLK, ATT_BLK), 0)
    j = lax.broadcasted_iota(jnp.int32, (ATT_BLK, ATT_BLK), 1)
    s_p = jnp.where(jnp.logical_and(j >= i, has_prev), _dot_nt(q, kp) * scale, NEG)
    s_c = jnp.where(j <= i, _dot_nt(q, kc) * scale, NEG)
    return s_p, s_c


def _att_views(arrs, d):
    return [a.reshape(d, -1, ATT_GW) for a in arrs]


def _att_unview(a, d):
    return a.reshape(-1, ATT_GW) if d == 1 else a


def _attn_fwd(qb, kb, vb, g):
    d = ATT_PATTERNS[g][1]
    q2, k2, v2 = _att_views([qb, kb, vb], d)
    nb = q2.shape[1] // ATT_BLK

    def body(q_ref, kc_ref, kp_ref, vc_ref, vp_ref, o_ref, l_ref):
        has_prev = pl.program_id(1) > 0
        for h in range(ATT_HEADS):
            hs = slice(h * ATT_DH, (h + 1) * ATT_DH)
            s_p, s_c = _att_scores(q_ref[:, hs], kp_ref[:, hs], kc_ref[:, hs], has_prev)
            m = jnp.maximum(jnp.max(s_p, axis=1, keepdims=True), jnp.max(s_c, axis=1, keepdims=True))
            p_p, p_c = jnp.exp(s_p - m), jnp.exp(s_c - m)
            l = jnp.sum(p_p, axis=1, keepdims=True) + jnp.sum(p_c, axis=1, keepdims=True)
            o = _dot(p_p.astype(BF16), vp_ref[:, hs]) + _dot(p_c.astype(BF16), vc_ref[:, hs])
            o_ref[:, hs] = o / l
            l_ref[:, hs] = jnp.broadcast_to(m + jnp.log(l), (ATT_BLK, ATT_DH))

    cur = pl.BlockSpec((None, ATT_BLK, ATT_GW), lambda r, n: (r, n, 0))
    prev = pl.BlockSpec((None, ATT_BLK, ATT_GW), lambda r, n: (r, jnp.maximum(n - 1, 0), 0))
    o, lse = pl.pallas_call(
        body,
        out_shape=[jax.ShapeDtypeStruct(q2.shape, F32)] * 2,
        grid=(d, nb),
        in_specs=[cur, cur, prev, cur, prev],
        out_specs=[cur, cur],
        compiler_params=_cparams(("parallel", "arbitrary")),
        name=f"attn_fwd_g{g}",
    )(q2, k2, k2, v2, v2)
    return _att_unview(o, d), _att_unview(lse, d)


def _attn_bwd(qb, kb, vb, o, lse, d_o, d_lse, g):
    d = ATT_PATTERNS[g][1]
    q2, k2, v2 = _att_views([qb, kb, vb], d)
    o2, l2, do2, dl2 = _att_views([o, lse, d_o, d_lse], d)
    nb = q2.shape[1] // ATT_BLK

    def body(q_ref, kc_ref, kp_ref, vc_ref, vp_ref, o_ref, l_ref, do_ref, dl_ref, dq_ref, dk_ref, dv_ref, ck, cv):
        n = pl.program_id(1)
        active = n < nb

        @pl.when(n == 0)
        def _():
            ck[...] = jnp.zeros_like(ck)
            cv[...] = jnp.zeros_like(cv)

        @pl.when(jnp.logical_not(active))
        def _():
            dk_ref[...] = ck[...]
            dv_ref[...] = cv[...]

        @pl.when(active)
        def _():
            has_prev = n > 0
            for h in range(ATT_HEADS):
                hs = slice(h * ATT_DH, (h + 1) * ATT_DH)
                q, kp, kc, vp, vc = q_ref[:, hs], kp_ref[:, hs], kc_ref[:, hs], vp_ref[:, hs], vc_ref[:, hs]
                s_p, s_c = _att_scores(q, kp, kc, has_prev)
                lse_h = l_ref[:, hs][:, 0:1]
                p_p, p_c = jnp.exp(s_p - lse_h), jnp.exp(s_c - lse_h)
                do = do_ref[:, hs]
                delta = jnp.sum(do * o_ref[:, hs] - dl_ref[:, hs], axis=1, keepdims=True)
                dob = do.astype(BF16)
                scale = ATT_DH ** -0.5
                ds_p = (p_p * (_dot_nt(dob, vp) - delta) * scale).astype(BF16)
                ds_c = (p_c * (_dot_nt(dob, vc) - delta) * scale).astype(BF16)
                dq_ref[:, hs] = _dot(ds_p, kp) + _dot(ds_c, kc)
                dk_ref[:, hs] = ck[:, hs] + _dot_tn(ds_p, q)
                dv_ref[:, hs] = cv[:, hs] + _dot_tn(p_p.astype(BF16), dob)
                ck[:, hs] = _dot_tn(ds_c, q)
                cv[:, hs] = _dot_tn(p_c.astype(BF16), dob)

    def qn(n):
        return jnp.minimum(n, nb - 1)

    cur = pl.BlockSpec((None, ATT_BLK, ATT_GW), lambda r, n: (r, qn(n), 0))
    prev = pl.BlockSpec((None, ATT_BLK, ATT_GW), lambda r, n: (r, jnp.maximum(qn(n) - 1, 0), 0))
    behind = pl.BlockSpec((None, ATT_BLK, ATT_GW), lambda r, n: (r, jnp.maximum(n - 1, 0), 0))
    shp = jax.ShapeDtypeStruct(q2.shape, F32)
    dq, dk, dv = pl.pallas_call(
        body,
        out_shape=[shp, shp, shp],
        grid=(d, nb + 1),
        in_specs=[cur, cur, prev, cur, prev, cur, cur, cur, cur],
        out_specs=[cur, behind, behind],
        scratch_shapes=[pltpu.VMEM((ATT_BLK, ATT_GW), F32), pltpu.VMEM((ATT_BLK, ATT_GW), F32)],
        compiler_params=_cparams(("parallel", "arbitrary")),
        name=f"attn_bwd_g{g}",
    )(q2, k2, k2, v2, v2, o2, l2, do2, dl2)
    return _att_unview(dq, d), _att_unview(dk, d), _att_unview(dv, d)


def _rms_parts(x, width):
    outs = []
    for lo in range(0, x.shape[1], width):
        xs = x[:, lo:lo + width]
        r = lax.rsqrt(jnp.mean(xs * xs, axis=1, keepdims=True) + EPS)
        outs.append((xs * r, r))
    return outs


def _rms_bwd_part(xh, r, dxh):
    return r * (dxh - xh * jnp.mean(dxh * xh, axis=1, keepdims=True))


def _norm_fwd(x, gain):
    d = x.shape[1]

    def fn(ins, consts):
        (xh, _), = _rms_parts(ins[0], d)
        return [xh * consts[0]], []

    (h,), _ = _rowwise(fn, [(x, d, 0)], [gain.reshape(1, d)], [(d, BF16)], [], bm=512, name="norm_fwd")
    return h


def _norm_bwd(x, gain, dh, dres):
    d = x.shape[1]

    def fn(ins, consts):
        (xh, r), = _rms_parts(ins[0], d)
        dx = ins[2] + _rms_bwd_part(xh, r, ins[1] * consts[0])
        return [dx], [_colsum8(ins[1] * xh)]

    (dx,), (dg,) = _rowwise(fn, [(x, d, 0), (dh, d, 0), (dres, d, 0)], [gain.reshape(1, d)], [(d, F32)], [d],
                            bm=512, name="norm_bwd")
    return dx, dg


def _rot_sign():
    lane = lax.broadcasted_iota(jnp.int32, (1, ATT_DH), 1)
    return jnp.where(lane < ATT_DH // 2, -1.0, 1.0).astype(F32)


def _rope(y, cos, sin):
    return y * cos + pltpu.roll(y, ATT_DH // 2, axis=1) * _rot_sign() * sin


def _rope_t(dy, cos, sin):
    return dy * cos - pltpu.roll(dy * sin, ATT_DH // 2, axis=1) * _rot_sign()


def _qk_prep(zq, zk, zv, qn, kn, cos, sin):
    w = zq.shape[1]

    def fn(ins, consts):
        cs, sn = ins[3], ins[4]
        outs = []
        for z, gain in ((ins[0], consts[0]), (ins[1], consts[1])):
            for i, (xh, _) in enumerate(_rms_parts(z, ATT_DH)):
                outs.append(_rope(xh * gain[:, i * ATT_DH:(i + 1) * ATT_DH], cs, sn))
        outs += [ins[2][:, i * ATT_DH:(i + 1) * ATT_DH] for i in range(w // ATT_DH)]
        groups = [jnp.concatenate(outs[i:i + ATT_HEADS], axis=1) for i in range(0, len(outs), ATT_HEADS)]
        return groups, []

    outs, _ = _rowwise(fn, [(zq, w, 0), (zk, w, 0), (zv, w, 0), (cos, ATT_DH, 0), (sin, ATT_DH, 0)], [qn, kn],
                       [(ATT_GW, BF16, ATT_PATTERNS[g][1]) for g in range(ATT_GROUPS)] * 3, [], bm=256, name="qk_prep")
    return outs[0:3], outs[3:6], outs[6:9]


def _qk_prep_bwd(zq, zk, dq_g, dk_g, dv_g, qn, kn, cos, sin):
    w = zq.shape[1]

    def fn(ins, consts):
        cs, sn = ins[2], ins[3]
        outs, sums = [], []
        for z, gain, dparts in ((ins[0], consts[0], ins[4:7]), (ins[1], consts[1], ins[7:10])):
            dout = jnp.concatenate(dparts, axis=1)
            dz, dgain = [], []
            for i, (xh, r) in enumerate(_rms_parts(z, ATT_DH)):
                hs = slice(i * ATT_DH, (i + 1) * ATT_DH)
                dy = _rope_t(dout[:, hs], cs, sn)
                dgain.append(_colsum8(dy * xh))
                dz.append(_rms_bwd_part(xh, r, dy * gain[:, hs]))
            outs.append(jnp.concatenate(dz, axis=1))
            sums.append(jnp.concatenate(dgain, axis=1))
        outs.append(jnp.concatenate(ins[10:13], axis=1))
        return outs, sums

    ins = [(zq, w, 0), (zk, w, 0), (cos, ATT_DH, 0), (sin, ATT_DH, 0)]
    for parts in (dq_g, dk_g, dv_g):
        ins += [(a, ATT_GW, 0, ATT_PATTERNS[g][1]) for g, a in enumerate(parts)]
    (dzq, dzk, dzv), (dqn, dkn) = _rowwise(fn, ins, [qn, kn], [(w, BF16)] * 3, [w, w], bm=256, name="qk_prep_bwd")
    return dzq, dzk, dzv, dqn, dkn


def _post_a(o_raw, zh, gout):
    w = o_raw.shape[1]

    def fn(ins, consts):
        oh = jnp.concatenate([xh for xh, _ in _rms_parts(ins[0], HG_DK)], axis=1)
        hg = ins[1]
        return [oh * consts[0] * (hg * _sigmoid(hg))], []

    (y,), _ = _rowwise(fn, [(o_raw, w, 0), (zh, w, 3)], [gout.reshape(1, w)], [(w, BF16)], [], bm=512, name="post_a")
    return y


def _post_a_bwd(o_raw, zh, gout, dy):
    w = o_raw.shape[1]

    def fn(ins, consts):
        parts = _rms_parts(ins[0], HG_DK)
        oh = jnp.concatenate([xh for xh, _ in parts], axis=1)
        hg, dyv, gain = ins[1], ins[2], consts[0]
        sg = _sigmoid(hg)
        s = hg * sg
        doh = dyv * gain * s
        do = jnp.concatenate([_rms_bwd_part(xh, r, doh[:, i * HG_DK:(i + 1) * HG_DK]) for i, (xh, r) in enumerate(parts)], axis=1)
        dhg = dyv * oh * gain * (sg * (1.0 + hg * (1.0 - sg)))
        return [do, dhg], [_colsum8(dyv * oh * s)]

    (do, dhg), (dgain,) = _rowwise(fn, [(o_raw, w, 0), (zh, w, 3), (dy, w, 0)], [gout.reshape(1, w)],
                                   [(w, F32), (w, BF16)], [w], bm=512, name="post_a_bwd")
    return do, dhg, dgain


def _merge_alpha(lses):
    m = jnp.maximum(jnp.maximum(lses[0], lses[1]), lses[2])
    e = [jnp.exp(l - m) for l in lses]
    inv = 1.0 / (e[0] + e[1] + e[2])
    return [x * inv for x in e]


def _group_ins(parts):
    return [(a, ATT_GW, 0, ATT_PATTERNS[g][1]) for g, a in enumerate(parts)]


def _merge_b(o_g, lse_g):
    def fn(ins, consts):
        al = _merge_alpha(ins[3:6])
        return [al[0] * ins[0] + al[1] * ins[1] + al[2] * ins[2]], []

    (y,), _ = _rowwise(fn, _group_ins(o_g) + _group_ins(lse_g), [], [(ATT_GW, BF16)], [], bm=512, name="merge_b")
    return y


def _merge_b_bwd(o_g, lse_g, dy):
    def fn(ins, consts):
        al = _merge_alpha(ins[3:6])
        dyv = ins[6]
        dal = [dyv * ins[i] for i in range(3)]
        tot = al[0] * dal[0] + al[1] * dal[1] + al[2] * dal[2]
        return [al[i] * dyv for i in range(3)] + [al[i] * (dal[i] - tot) for i in range(3)], []

    outs, _ = _rowwise(fn, _group_ins(o_g) + _group_ins(lse_g) + [(dy, ATT_GW, 0)], [],
                       [(ATT_GW, F32, ATT_PATTERNS[g][1]) for g in range(ATT_GROUPS)] * 2, [], bm=512, name="merge_b_bwd")
    return outs[:3], outs[3:]


def _loss_head(y, target):
    d = y.shape[1]

    def fn(ins, consts):
        e = ins[0] - ins[1]
        return [e * (1.0 / d)], [_colsum8(e * e)]

    (dy,), (sq,) = _rowwise(fn, [(y, d, 0), (target, d, 0)], [], [(d, F32)], [d], bm=512, name="loss_head")
    return 0.5 * jnp.sum(sq) / d, dy


def _silu_grad(a):
    s = _sigmoid(a)
    return s * (1.0 + a * (1.0 - s))


def _ffn_fwd(x, gain, wt, wo_fn, tag):
    t, d = x.shape
    f = wt.shape[0] // 2
    h = _norm_fwd(x, gain)

    def act(accs, ex):
        a, b = accs
        return (a * _sigmoid(a) * b, a, b)

    u, a, b = _mm([h], [wt, wt], [(0, 0, 0), (0, 1, 1)], 2, act, [BF16, BF16, BF16], m=t, n=f, k=d, tb=True,
                  bm=1024, bn=256, bk=d, b_off=[(0, 0), (f // min(256, f), 0)], name=f"ffn_in_{tag}")
    wo = wo_fn(u)
    (y,) = _mm([u], [wo], [(0, 0, 0)], 1, lambda accs, ex: (ex[0] + 0.5 * accs[0],), [F32], m=t, n=d, k=f,
               bm=512, bn=d, bk=f, extras=[x], name=f"ffn_out_{tag}")
    return y, (x, h, u, a, b, wo)


def _ffn_bwd(dy, saved, gain, wt, tag, tok):
    x, h, u, a, b, wo = saved
    t, d = x.shape
    f = wo.shape[0]
    dyb = (dy + tok).astype(BF16)

    def dact(accs, ex):
        du = 0.5 * accs[0]
        av, bv = ex[0].astype(F32), ex[1].astype(F32)
        return (du * bv * _silu_grad(av), du * av * _sigmoid(av))

    da, db = _mm([dyb], [wo], [(0, 0, 0)], 1, dact, [BF16, BF16], m=t, n=f, k=d, tb=True, bm=1024, bn=256, bk=d,
                 extras=[a, b], name=f"ffn_dact_{tag}")
    (dwo,) = _mm([u], [dyb], [(0, 0, 0)], 1, lambda accs, ex: (0.5 * accs[0],), [BF16], m=f, n=d, k=t, ta=True,
                 bm=1408, bn=d, bk=1024, name=f"ffn_dwo_{tag}")
    (dh,) = _mm([da, db], [wt, wt], [(0, 0, 0), (1, 1, 0)], 1, _first, [F32], m=t, n=d, k=f, bm=512, bn=d, bk=f,
                b_off=[(0, 0), (0, 1)], name=f"ffn_dh_{tag}")
    dwt = [_mm([g], [h], [(0, 0, 0)], 1, _first, [BF16], m=f, n=d, k=t, ta=True, bm=1408, bn=d, bk=1024,
               name=f"ffn_dwt{i}_{tag}")[0] for i, g in enumerate((da, db))]
    dx, dgain = _norm_bwd(x, gain, dh, dy)
    return dx, dgain, jnp.concatenate(dwt, axis=0), dwo


Z_SPLITS = (("h", 4096), ("q", 1536), ("k", 1536), ("v", 1536), ("g", 2048))


def _mix_fwd(x, p, cos, sin):
    t, d = x.shape
    hm = _norm_fwd(x, p["gm"])
    z, off = {}, 0
    for nm, width in Z_SPLITS:
        (z[nm],) = _mm([hm], [p["wint"]], [(0, 0, 0)], 1, _first, [F32], m=t, n=width, k=d, tb=True, bm=1024, bn=512, bk=d,
                       b_off=[(off // 512, 0)], name=f"mix_in_{nm}")
        off += width
    o_raw, states = _hgrn_fwd(z["h"], p["lb3"])
    qb, kb, vb = _qk_prep(z["q"], z["k"], z["v"], p["qn"], p["kn"], cos, sin)
    o_g, lse_g = zip(*[_attn_fwd(qb[g], kb[g], vb[g], g) for g in range(ATT_GROUPS)])
    oa = _post_a(o_raw, z["h"], p["gout"])
    ob = _merge_b(o_g, lse_g)
    late = p["late"](ob)
    p = dict(p, **late)
    (ya,) = _mm([oa], [p["wa"]], [(0, 0, 0)], 1, _first, [F32], m=t, n=d, k=oa.shape[1], bm=1024, bn=d, bk=oa.shape[1],
                name="branch_a")

    def gate(accs, ex):
        return (_sigmoid(ex[0]) * ex[2] + _sigmoid(ex[1]) * accs[0], accs[0])

    merged, yb = _mm([ob], [p["wbt"]], [(0, 0, 0)], 1, gate, [BF16, F32], m=t, n=d, k=ATT_GW, tb=True, bm=512, bn=d,
                     bk=ATT_GW, extras=[z["g"], z["g"], ya], e_off=[0, 1, 0], name="branch_b_gate")
    (y,) = _mm([merged], [p["wo"]], [(0, 0, 0)], 1, lambda accs, ex: (ex[0] + accs[0],), [F32], m=t, n=d, k=d,
               bm=1024, bn=d, bk=d, extras=[x], name="mix_out")
    return y, (x, hm, z, o_raw, states, qb, kb, vb, o_g, lse_g, oa, ob, ya, yb, merged, late)


def _mix_bwd(dy, saved, p, cos, sin, tok):
    x, hm, z, o_raw, states, qb, kb, vb, o_g, lse_g, oa, ob, ya, yb, merged, late = saved
    p = dict(p, **late)
    t, d = x.shape
    w = oa.shape[1]
    dyb = (dy + tok).astype(BF16)

    def dgate(accs, ex):
        dm = accs[0]
        sa, sb = _sigmoid(ex[0]), _sigmoid(ex[1])
        return (sa * dm, sb * dm, dm * ex[2] * sa * (1.0 - sa), dm * ex[3] * sb * (1.0 - sb))

    dya, dyb_, dga, dgb = _mm([dyb], [p["wo"]], [(0, 0, 0)], 1, dgate, [BF16] * 4, m=t, n=d, k=d, tb=True, bm=512, bn=d,
                              bk=d, extras=[z["g"], z["g"], ya, yb], e_off=[0, 1, 0, 0], name="mix_out_bwd")
    (dwo,) = _mm([merged], [dyb], [(0, 0, 0)], 1, _first, [BF16], m=d, n=d, k=t, ta=True, bm=d, bn=d, bk=1024, name="mix_dwo")
    (doa,) = _mm([dya], [p["wa"]], [(0, 0, 0)], 1, _first, [F32], m=t, n=w, k=d, tb=True, bm=1024, bn=w, bk=d, name="branch_a_bwd")
    (dwa,) = _mm([oa], [dya], [(0, 0, 0)], 1, _first, [BF16], m=w, n=d, k=t, ta=True, bm=w, bn=d, bk=1024, name="branch_a_dw")
    (dob,) = _mm([dyb_], [p["wbt"]], [(0, 0, 0)], 1, _first, [F32], m=t, n=ATT_GW, k=d, bm=1024, bn=ATT_GW, bk=d,
                 name="branch_b_bwd")
    (dwbt,) = _mm([dyb_], [ob], [(0, 0, 0)], 1, _first, [BF16], m=d, n=ATT_GW, k=t, ta=True, bm=d, bn=ATT_GW, bk=1024,
                  name="branch_b_dw")
    do_raw, dhg, dgout = _post_a_bwd(o_raw, z["h"], p["gout"], doa)
    do_g, dlse_g = _merge_b_bwd(o_g, lse_g, dob)
    dq_g, dk_g, dv_g = zip(*[_attn_bwd(qb[g], kb[g], vb[g], o_g[g], lse_g[g], do_g[g], dlse_g[g], g)
                             for g in range(ATT_GROUPS)])
    dzq, dzk, dzv, dqn, dkn = _qk_prep_bwd(z["q"], z["k"], dq_g, dk_g, dv_g, p["qn"], p["kn"], cos, sin)
    dhq, dhf, dhi, lbsum = _hgrn_bwd(z["h"], p["lb3"], states, do_raw)
    dz = jnp.concatenate([dhq, dhf, dhi, dhg, dzq, dzk, dzv, dga, dgb], axis=1)
    pw = dz.shape[1]
    (dhm,) = _mm([dz], [p["wint"]], [(0, 0, 0)], 1, _first, [F32], m=t, n=d, k=pw, bm=1024, bn=d, bk=1536, name="mix_in_bwd")
    (dwint,) = _mm([dz], [hm], [(0, 0, 0)], 1, _first, [BF16], m=pw, n=d, k=t, ta=True, bm=1536, bn=d, bk=1024, name="mix_in_dw")
    dx, dgm = _norm_bwd(x, p["gm"], dhm, dy)
    return dx, dict(gm=dgm, wint=dwint, lbsum=lbsum, gout=dgout, qn=dqn, kn=dkn, wa=dwa, wbt=dwbt, wo=dwo)


def _rope_tables(t):
    pos = jnp.arange(t, dtype=F32)
    inv = ROPE_THETA ** (-jnp.arange(0, ATT_DH, 2, dtype=F32) / ATT_DH)
    ang = pos[:, None] * inv[None, :]
    ang = jnp.concatenate([ang, ang], axis=-1)
    return jnp.cos(ang), jnp.sin(ang)


def _lower_bounds(logits):
    lb = jnp.cumsum(jax.nn.softmax(logits, axis=0), axis=0)
    return lb - lb[0:1]


def _head_gain(g):
    return jnp.tile(g[:, None, :], (1, ATT_HEADS, 1)).reshape(1, ATT_GROUPS * ATT_GW)


SMALL_GRADS = ("ffn1_norm", "mix_norm", "lbsum", "hgrn_out_norm", "attn_q_norm", "attn_k_norm", "ffn2_norm")


def _local_step(x, target, small, fetch, emit):
    t = x.shape[0]
    depth = small["ffn1_norm"].shape[0]
    cos, sin = _rope_tables(t)
    lb_all = _lower_bounds(small["hgrn_lb_logits"])
    saved = []
    for l in range(depth):
        w1t = fetch("w1t", l, x)["w1t"]
        x, s1 = _ffn_fwd(x, small["ffn1_norm"][l], w1t, lambda after, l=l: fetch("w1o", l, after)["w1o"], "1")
        p = dict(gm=small["mix_norm"][l], wint=fetch("wint", l, x)["wint"], lb3=lb_all[l].reshape(-1, 1, HG_DK),
                 gout=small["hgrn_out_norm"][l], qn=_head_gain(small["attn_q_norm"][l]),
                 kn=_head_gain(small["attn_k_norm"][l]), late=functools.partial(fetch, "mout", l))
        x, sm = _mix_fwd(x, p, cos, sin)
        w2t = fetch("w2t", l, x)["w2t"]
        x, s2 = _ffn_fwd(x, small["ffn2_norm"][l], w2t, lambda after, l=l: fetch("w2o", l, after)["w2o"], "2")
        saved.append((p, w1t, w2t, s1, sm, s2))
    loss, dx = _loss_head(x, target)
    gsmall = {k: [None] * depth for k in SMALL_GRADS}
    tok = jnp.zeros((), F32)
    for l in reversed(range(depth)):
        p, w1t, w2t, s1, sm, s2 = saved[l]
        dx, gsmall["ffn2_norm"][l], dw2t, dw2o = _ffn_bwd(dx, s2, small["ffn2_norm"][l], w2t, "2", tok)
        tok = emit("ffn2", l, dict(w2t=dw2t, w2o=dw2o), None)
        dx, gm = _mix_bwd(dx, sm, p, cos, sin, tok)
        tok = emit("mix", l, {k: gm[k] for k in ("wint", "wa", "wbt", "wo")}, None)
        gsmall["mix_norm"][l], gsmall["lbsum"][l], gsmall["hgrn_out_norm"][l] = gm["gm"], gm["lbsum"], gm["gout"]
        for k, src in (("attn_q_norm", "qn"), ("attn_k_norm", "kn")):
            gsmall[k][l] = jnp.sum(gm[src].reshape(ATT_GROUPS, ATT_HEADS, ATT_DH), axis=1)
        dx, gsmall["ffn1_norm"][l], dw1t, dw1o = _ffn_bwd(dx, s1, small["ffn1_norm"][l], w1t, "1", tok)
        final = ({k: jnp.stack(v) for k, v in gsmall.items()}, loss) if l == 0 else None
        tok = emit("ffn1", l, dict(w1t=dw1t, w1o=dw1o), final)
    return dx


_HBM = pl.BlockSpec(memory_space=pltpu.HBM)
_SEM = pl.BlockSpec(memory_space=pltpu.SEMAPHORE)
_EFFECT = pltpu.SideEffectType.DATAFLOW_SIDE_EFFECTING


def _peer(p):
    x, y, c = lax.axis_index("x"), lax.axis_index("y"), lax.axis_index("c")
    me = 4 * x + 2 * y + c
    return (1 - x if p & 4 else x, 1 - y if p & 2 else y, 1 - c if p & 1 else c), jnp.bitwise_xor(me, p), me


def _xchg_copy(src, land, mode, send_sems, recv_sems, k, p, arriving):
    peer, peer_id, me = _peer(p)
    block = src if mode == "gather" else src.at[peer_id]
    return pltpu.make_async_remote_copy(
        src_ref=block, dst_ref=land.at[peer_id if arriving else me], send_sem=send_sems.at[k * (N_DEV - 1) + p - 1],
        recv_sem=recv_sems.at[k * (N_DEV - 1) + p - 1], device_id=peer, device_id_type=MESH)


def _xchg_start(srcs, modes, groups, name):
    n, ng = len(srcs), len(groups)
    me = 4 * lax.axis_index("x") + 2 * lax.axis_index("y") + lax.axis_index("c")
    lands = []
    for a, mode in zip(srcs, modes):
        own = a[None] if mode == "gather" else lax.dynamic_slice_in_dim(a, me, 1, axis=0)
        r, c = a.shape[-2:]
        lands.append(lax.dynamic_update_slice(lax.empty((N_DEV, r, c), a.dtype), own, (me, 0, 0)))

    def body(*refs):
        src, land = refs[:n], refs[n:2 * n]
        sems = refs[2 * n:2 * n + 2 * ng]
        token = refs[2 * n + 2 * ng + 2 * n]
        for gi, idx in enumerate(groups):
            for ki, k in enumerate(idx):
                for p in range(1, N_DEV):
                    _xchg_copy(src[k], land[k], modes[k], sems[2 * gi], sems[2 * gi + 1], ki, p, False).start()
        token[...] = jnp.zeros_like(token)

    sem_shapes = []
    for idx in groups:
        sem_shapes += [pltpu.SemaphoreType.DMA((len(idx) * (N_DEV - 1),))] * 2
    outs = pl.pallas_call(
        body,
        out_shape=sem_shapes + [pltpu.HBM(a.shape, a.dtype) for a in srcs] + [pltpu.HBM(a.shape, a.dtype) for a in lands]
        + [jax.ShapeDtypeStruct((8, 128), F32)],
        in_specs=[_HBM] * (2 * n),
        out_specs=[_SEM] * (2 * ng) + [_HBM] * (2 * n) + [pl.BlockSpec(memory_space=pltpu.VMEM)],
        input_output_aliases={i: 2 * ng + i for i in range(2 * n)},
        compiler_params=pltpu.CompilerParams(has_side_effects=_EFFECT),
        name=name,
    )(*[pltpu.with_memory_space_constraint(a, pltpu.HBM) for a in list(srcs) + lands])
    sems = [(outs[2 * gi], outs[2 * gi + 1]) for gi in range(ng)]
    return sems, outs[2 * ng:2 * ng + n], outs[2 * ng + n:2 * ng + 2 * n], outs[-1]


def _xchg_wait(srcs, lands, modes, sems, after, name):
    n = len(srcs)

    def body(*refs):
        src, land = refs[:n], refs[n:2 * n]
        send_sems, recv_sems = refs[2 * n], refs[2 * n + 1]
        for p in range(1, N_DEV):
            for k in range(n):
                cp = _xchg_copy(src[k], land[k], modes[k], send_sems, recv_sems, k, p, True)
                cp.wait_send()
                cp.wait_recv()

    outs = pl.pallas_call(
        body,
        out_shape=[pltpu.HBM(a.shape, a.dtype) for a in list(srcs) + list(lands)],
        in_specs=[_HBM] * (2 * n) + [_SEM, _SEM, pl.BlockSpec(memory_space=pl.ANY)],
        out_specs=[_HBM] * (2 * n),
        input_output_aliases={i: i for i in range(2 * n)},
        compiler_params=pltpu.CompilerParams(has_side_effects=_EFFECT),
        name=name,
    )(*srcs, *lands, sems[0], sems[1], after)
    return outs[n:]


def _sum_slots(land):
    g, _, r, c = land.shape
    br = r // 2 if (r % 32 == 0 and r >= 256) else r

    def body(l_ref, o_ref):
        acc = l_ref[0, 0].astype(F32)
        for j in range(1, N_DEV):
            acc = acc + l_ref[0, j].astype(F32)
        o_ref[0] = acc

    return pl.pallas_call(
        body,
        out_shape=jax.ShapeDtypeStruct((g, r, c), F32),
        grid=(g, r // br),
        in_specs=[pl.BlockSpec((1, N_DEV, br, c), lambda i, j: (i, 0, j, 0))],
        out_specs=pl.BlockSpec((1, br, c), lambda i, j: (i, j, 0)),
        compiler_params=_cparams(("parallel", "parallel")),
        name="sum_slots",
    )(land)


def _adamw(w, g, m, v):
    shape = w.shape
    cols = shape[-1]
    rows = int(np.prod(shape[:-1]))
    bm = max(b for b in range(8, 257, 8) if rows % b == 0) if rows % 8 == 0 else rows
    c1 = 1.0 - ADAM_B1 ** ADAM_STEP
    c2 = 1.0 - ADAM_B2 ** ADAM_STEP

    def fn(ins, consts):
        wv, gv, mv, vv = ins
        m2 = ADAM_B1 * mv + (1.0 - ADAM_B1) * gv
        v2 = ADAM_B2 * vv + (1.0 - ADAM_B2) * (gv * gv)
        delta = -ADAM_LR * ((m2 / c1) / (jnp.sqrt(v2 / c2) + ADAM_EPS) + ADAM_WD * wv)
        return [delta, m2, v2], []

    outs, _ = _rowwise(fn, [(a.reshape(rows, cols), cols, 0) for a in (w, g, m, v)], [], [(cols, F32)] * 3, [],
                       bm=bm, name="adamw")
    return [o.reshape(shape) for o in outs]


BIG = ("w1t", "w1o", "wint", "wa", "wbt", "wo", "w2t", "w2o")
FETCH_GROUPS = dict(w1t=("w1t",), w1o=("w1o",), wint=("wint",), mout=("wa", "wbt", "wo"), w2t=("w2t",), w2o=("w2o",))
SMALL_ROWS = (("ffn1_norm", 0), ("mix_norm", 2), ("lbsum", 4), ("hgrn_out_norm", 6), ("ffn2_norm", 8),
              ("attn_q_norm", 10), ("attn_k_norm", 12))
SMALL_PACK_ROWS = 16


def kernel(x, ffn1_norm, ffn1_w_in, ffn1_w_out, mix_norm, w_in, hgrn_lb_logits, hgrn_out_norm, attn_q_norm, attn_k_norm, w_branch_a, w_branch_b, w_out, ffn2_norm, ffn2_w_in, ffn2_w_out, loss_target, m_ffn1_norm, m_ffn1_w_in, m_ffn1_w_out, m_mix_norm, m_w_in, m_hgrn_lb_logits, m_hgrn_out_norm, m_attn_q_norm, m_attn_k_norm, m_w_branch_a, m_w_branch_b, m_w_out, m_ffn2_norm, m_ffn2_w_in, m_ffn2_w_out, v_ffn1_norm, v_ffn1_w_in, v_ffn1_w_out, v_mix_norm, v_w_in, v_hgrn_lb_logits, v_hgrn_out_norm, v_attn_q_norm, v_attn_k_norm, v_w_branch_a, v_w_branch_b, v_w_out, v_ffn2_norm, v_ffn2_w_in, v_ffn2_w_out):
    names = ("ffn1_norm", "ffn1_w_in", "ffn1_w_out", "mix_norm", "w_in", "hgrn_lb_logits", "hgrn_out_norm", "attn_q_norm",
             "attn_k_norm", "w_branch_a", "w_branch_b", "w_out", "ffn2_norm", "ffn2_w_in", "ffn2_w_out")
    w = dict(zip(names, (ffn1_norm, ffn1_w_in, ffn1_w_out, mix_norm, w_in, hgrn_lb_logits, hgrn_out_norm, attn_q_norm,
                         attn_k_norm, w_branch_a, w_branch_b, w_out, ffn2_norm, ffn2_w_in, ffn2_w_out)))
    m = dict(zip(names, (m_ffn1_norm, m_ffn1_w_in, m_ffn1_w_out, m_mix_norm, m_w_in, m_hgrn_lb_logits, m_hgrn_out_norm,
                         m_attn_q_norm, m_attn_k_norm, m_w_branch_a, m_w_branch_b, m_w_out, m_ffn2_norm, m_ffn2_w_in, m_ffn2_w_out)))
    v = dict(zip(names, (v_ffn1_norm, v_ffn1_w_in, v_ffn1_w_out, v_mix_norm, v_w_in, v_hgrn_lb_logits, v_hgrn_out_norm,
                         v_attn_q_norm, v_attn_k_norm, v_w_branch_a, v_w_branch_b, v_w_out, v_ffn2_norm, v_ffn2_w_in, v_ffn2_w_out)))
    depth, d = ffn1_norm.shape

    def tr(a):
        return jnp.swapaxes(a, 1, 2)

    shard = dict(w1t=tr(ffn1_w_in), w1o=ffn1_w_out, wint=tr(w_in), wa=w_branch_a,
                 wbt=tr(w_branch_b).reshape(depth, -1, d), wo=w_out, w2t=tr(ffn2_w_in), w2o=ffn2_w_out)
    order = [(g, l) for l in range(depth) for g in FETCH_GROUPS]
    flat = [(g, l, k) for g, l in order for k in FETCH_GROUPS[g]]
    groups, pos = [], 0
    for g, l in order:
        groups.append(list(range(pos, pos + len(FETCH_GROUPS[g]))))
        pos += len(FETCH_GROUPS[g])
    g_sems, g_srcs, g_lands, _ = _xchg_start([shard[k][l].astype(BF16) for _, l, k in flat], ["gather"] * len(flat),
                                             groups, "gather_start")

    def fetch(group, l, after):
        gi = order.index((group, l))
        idx = groups[gi]
        lands = _xchg_wait([g_srcs[i] for i in idx], [g_lands[i] for i in idx], ["gather"] * len(idx), g_sems[gi], after,
                           f"gather_wait_{group}{l}")
        out = {}
        for k, land in zip(FETCH_GROUPS[group], lands):
            out[k] = land.reshape(d, -1) if k == "wbt" else land.reshape(-1, d)
        return out

    pending = []

    def emit(group, l, g, final):
        keys = list(g)
        srcs = [g[k].reshape(N_DEV, -1, d) for k in keys]
        modes = ["scatter"] * len(keys)
        if final is not None:
            gsmall, loss = final
            pack = jnp.zeros((SMALL_PACK_ROWS, d), F32)
            for k, r0 in SMALL_ROWS:
                rows = gsmall[k].reshape(depth, -1)
                pack = pack.at[r0:r0 + depth, :rows.shape[1]].set(rows)
            srcs.append(pack.at[14, :].set(loss))
            modes.append("gather")
            keys.append("small")
        sems, s_thru, l_thru, token = _xchg_start(srcs, modes, [list(range(len(srcs)))], f"grads_start_{group}{l}")
        pending.append((group, l, keys, modes, sems[0], s_thru, l_thru))
        return token[0, 0]

    small = {k: w[k] for k in ("ffn1_norm", "mix_norm", "hgrn_lb_logits", "hgrn_out_norm", "attn_q_norm", "attn_k_norm", "ffn2_norm")}
    dx = _local_step(x[0], loss_target[0], small, fetch, emit)

    summed = {}
    for group, l, keys, modes, sems, s_thru, l_thru in pending:
        lands = _xchg_wait(s_thru, l_thru, modes, sems, dx, f"grads_wait_{group}{l}")
        for k, land in zip(keys, lands):
            summed[k, l] = _sum_slots(land[None])[0]
    gsum = {k: jnp.stack([summed[k, l] for l in range(depth)]) for k in BIG}
    tot = summed["small", 0]

    grads = {}
    for k, r0 in SMALL_ROWS:
        shp = (depth,) + (w[k].shape[1:] if k != "lbsum" else (d,))
        grads[k] = tot[r0:r0 + depth, :int(np.prod(shp[1:]))].reshape(shp)
    _, lb_vjp = jax.vjp(_lower_bounds, hgrn_lb_logits)
    grads["hgrn_lb_logits"] = lb_vjp(grads.pop("lbsum"))[0]
    grads["ffn1_w_in"], grads["ffn1_w_out"] = tr(gsum["w1t"]), gsum["w1o"]
    grads["w_in"], grads["w_branch_a"] = tr(gsum["wint"]), gsum["wa"]
    grads["w_branch_b"] = tr(gsum["wbt"].reshape(depth, d // N_DEV, -1))
    grads["w_out"] = gsum["wo"]
    grads["ffn2_w_in"], grads["ffn2_w_out"] = tr(gsum["w2t"]), gsum["w2o"]

    upd = {k: _adamw(w[k], grads[k], m[k], v[k]) for k in names}
    return (tot[14, 0], dx[None], *[grads[k] for k in names], *[upd[k][0] for k in names],
            *[upd[k][1] for k in names], *[upd[k][2] for k in names])
```

```python
import functools
import math

import jax
import jax.numpy as jnp
import numpy as np
from jax import lax
from jax.experimental import pallas as pl
from jax.experimental.pallas import tpu as pltpu

F32 = jnp.float32
BF16 = jnp.bfloat16

N_DEV = 8
EPS = 1e-6
HG_DK = 128
HG_CHUNK = 64
HG_SUB = 16
HG_HP = 4
ATT_PATTERNS = ((128, 1), (512, 4), (2048, 16))
ATT_GROUPS = 3
ATT_HEADS = 4
ATT_DH = 128
ATT_BLK = 128
ROPE_THETA = 10000.0
ADAM_LR, ADAM_B1, ADAM_B2, ADAM_EPS, ADAM_WD, ADAM_STEP = 0.001, 0.9, 0.999, 1e-08, 0.01, 10
VMEM_LIMIT_BYTES = 56 * 1024 * 1024
MESH = pl.DeviceIdType.MESH


def _cparams(sem, **kw):
    return pltpu.CompilerParams(dimension_semantics=sem, vmem_limit_bytes=VMEM_LIMIT_BYTES, **kw)


def _sigmoid(x):
    return 1.0 / (1.0 + jnp.exp(-x))


def _mm(a_list, b_list, pairs, n_acc, fin, out_dtypes, *, m, n, k, ta=False, tb=False, bm, bn, bk,
        b_off=None, extras=(), e_off=None, name):
    bm, bn, bk = min(bm, m), min(bn, n), min(bk, k)
    assert m % bm == 0 and n % bn == 0 and k % bk == 0, (name, m, n, k, bm, bn, bk)
    nk = k // bk
    b_off = b_off or [(0, 0)] * len(b_list)
    e_off = e_off or [0] * len(extras)
    na, nb, ne = len(a_list), len(b_list), len(extras)
    dn = (((0,) if ta else (1,), (1,) if tb else (0,)), ((), ()))

    def body(*refs):
        a_refs, b_refs = refs[:na], refs[na:na + nb]
        e_refs = refs[na + nb:na + nb + ne]
        o_refs = refs[na + nb + ne:na + nb + ne + len(out_dtypes)]
        acc_refs = refs[na + nb + ne + len(out_dtypes):]
        kk = pl.program_id(2)
        parts = [None] * n_acc
        for ai, bi, ci in pairs:
            p = lax.dot_general(a_refs[ai][...], b_refs[bi][...], dn, preferred_element_type=F32)
            parts[ci] = p if parts[ci] is None else parts[ci] + p

        def finish(accs):
            outs = fin(accs, [e[...] for e in e_refs])
            for o_ref, o in zip(o_refs, outs):
                o_ref[...] = o.astype(o_ref.dtype)

        if nk == 1:
            finish(parts)
        else:
            @pl.when(kk == 0)
            def _():
                for c in range(n_acc):
                    acc_refs[c][...] = parts[c]

            @pl.when(kk > 0)
            def _():
                for c in range(n_acc):
                    acc_refs[c][...] += parts[c]

            @pl.when(kk == nk - 1)
            def _():
                finish([acc_refs[c][...] for c in range(n_acc)])

    a_spec = pl.BlockSpec((bk, bm), lambda i, j, q: (q, i)) if ta else pl.BlockSpec((bm, bk), lambda i, j, q: (i, q))

    def b_spec(off):
        on, ok = off
        if tb:
            return pl.BlockSpec((bn, bk), lambda i, j, q: (j + on, q + ok))
        return pl.BlockSpec((bk, bn), lambda i, j, q: (q + ok, j + on))

    mn_spec = pl.BlockSpec((bm, bn), lambda i, j, q: (i, j))
    outs = pl.pallas_call(
        body,
        out_shape=[jax.ShapeDtypeStruct((m, n), d) for d in out_dtypes],
        grid=(m // bm, n // bn, nk),
        in_specs=[a_spec] * na + [b_spec(o) for o in b_off]
        + [pl.BlockSpec((bm, bn), lambda i, j, q, o=o: (i, j + o)) for o in e_off],
        out_specs=[mn_spec] * len(out_dtypes),
        scratch_shapes=[pltpu.VMEM((bm, bn), F32) for _ in range(n_acc if nk > 1 else 0)],
        compiler_params=_cparams(("parallel", "parallel", "arbitrary")),
        name=name,
    )(*a_list, *b_list, *extras)
    return outs


def _first(accs, ex):
    return (accs[0],)


def _rowwise(fn, ins, consts, out_defs, sum_widths, *, bm, name):
    ins = [tuple(e) + (1,) * (4 - len(e)) for e in ins]
    out_defs = [tuple(e) + (1,) * (3 - len(e)) for e in out_defs]
    t = ins[0][0].shape[-2] * ins[0][3]
    bm = min(bm, t)
    assert t % bm == 0, (name, t, bm)
    ni, nc, no, ns = len(ins), len(consts), len(out_defs), len(sum_widths)
    strided = [w for _, w, _, d in ins if d > 1] + [w for w, _, d in out_defs if d > 1]

    def body(*refs):
        i_refs, c_refs = refs[:ni], refs[ni:ni + nc]
        o_refs, s_refs = refs[ni + nc:ni + nc + no], refs[ni + nc + no:ni + nc + no + ns]
        scratch = list(refs[ni + nc + no + ns:])
        vals = []
        for ref, (_, w, _, d) in zip(i_refs, ins):
            if d == 1:
                vals.append(ref[...])
                continue
            s = scratch.pop(0)
            for r in range(d):
                for c in range(w // 128):
                    s.at[c][pl.ds(r, bm // d, stride=d), :] = ref[r, :, c * 128:(c + 1) * 128].astype(F32)
            vals.append(jnp.concatenate([s[c] for c in range(w // 128)], axis=1))
        outs, sums = fn(vals, [r[...] for r in c_refs])
        for o_ref, o, (w, _, d) in zip(o_refs, outs, out_defs):
            if d == 1:
                o_ref[...] = o.astype(o_ref.dtype)
                continue
            s = scratch.pop(0)
            for c in range(w // 128):
                s[c] = o[:, c * 128:(c + 1) * 128].astype(F32)
            for r in range(d):
                for c in range(w // 128):
                    o_ref[r, :, c * 128:(c + 1) * 128] = s.at[c][pl.ds(r, bm // d, stride=d), :].astype(o_ref.dtype)
        if ns:
            first = pl.program_id(0) == 0

            @pl.when(first)
            def _():
                for s_ref, s in zip(s_refs, sums):
                    s_ref[...] = s

            @pl.when(jnp.logical_not(first))
            def _():
                for s_ref, s in zip(s_refs, sums):
                    s_ref[...] += s

    def win(width, cb, d):
        if d > 1:
            return pl.BlockSpec((d, bm // d, width), lambda i: (0, i, 0))
        return pl.BlockSpec((bm, width), lambda i: (i, cb))

    res = pl.pallas_call(
        body,
        out_shape=[jax.ShapeDtypeStruct((t, w) if d == 1 else (d, t // d, w), dt) for w, dt, d in out_defs]
        + [jax.ShapeDtypeStruct((8, w), F32) for w in sum_widths],
        grid=(t // bm,),
        in_specs=[win(w, cb, d) for _, w, cb, d in ins] + [pl.BlockSpec(c.shape, lambda i, nd=c.ndim: (0,) * nd) for c in consts],
        out_specs=[win(w, 0, d) for w, _, d in out_defs] + [pl.BlockSpec((8, w), lambda i: (0, 0)) for w in sum_widths],
        scratch_shapes=[pltpu.VMEM((w // 128, bm, 128), F32) for w in strided],
        compiler_params=_cparams(("arbitrary",) if ns else ("parallel",)),
        name=name,
    )(*[e[0] for e in ins], *consts)
    return res[:no], [jnp.sum(s, axis=0) for s in res[no:]]


def _colsum8(x):
    bm, w = x.shape
    return jnp.sum(x.reshape(bm // 8, 8, w), axis=0)


def _tri(n, upper=False):
    r = lax.broadcasted_iota(jnp.int32, (n, n), 0)
    c = lax.broadcasted_iota(jnp.int32, (n, n), 1)
    return (c >= r) if upper else (c <= r)


def _exact_tri_matmul(tri_bf16, x):
    x0 = x.astype(BF16)
    r1 = x - x0.astype(F32)
    x1 = r1.astype(BF16)
    x2 = (r1 - x1.astype(F32)).astype(BF16)
    w = x.shape[1]
    y = jnp.dot(tri_bf16, jnp.concatenate([x0, x1, x2], axis=1), preferred_element_type=F32)
    return y[:, :w] + y[:, w:2 * w] + y[:, 2 * w:]


def _dot_nt(a, b):
    return lax.dot_general(a, b, (((1,), (1,)), ((), ())), preferred_element_type=F32)


def _dot_tn(a, b):
    return lax.dot_general(a, b, (((0,), (0,)), ((), ())), preferred_element_type=F32)


def _dot(a, b):
    return jnp.dot(a, b, preferred_element_type=F32)


def _hg_gates(hq, hf, lb):
    sq = _sigmoid(hq)
    q = hq * sq
    sg = _sigmoid(hf)
    f = lb + (1.0 - lb) * sg
    return q, sq, sg, f


def _hg_intra(q, kk, g):
    c = q.shape[0]
    rows = lax.broadcasted_iota(jnp.int32, (c, 1), 0)
    a_rows, qts, kts, eqs, eks = [], [], [], [], []
    for i in range(c // HG_SUB):
        lo = i * HG_SUB
        ref = g[lo - 1:lo, :] if i else jnp.zeros_like(g[0:1, :])
        eq = jnp.exp(g[lo:lo + HG_SUB, :] - ref)
        ek = jnp.exp(jnp.where(rows < lo + HG_SUB, ref - g, 0.0))
        qt = q[lo:lo + HG_SUB, :] * eq
        kt = kk * ek
        a = _dot_nt(qt.astype(BF16), kt.astype(BF16))
        tpos = lo + lax.broadcasted_iota(jnp.int32, (HG_SUB, c), 0)
        spos = lax.broadcasted_iota(jnp.int32, (HG_SUB, c), 1)
        a_rows.append(jnp.where(spos <= tpos, a, 0.0))
        qts.append(qt), kts.append(kt), eqs.append(eq), eks.append(ek)
    return jnp.concatenate(a_rows, axis=0), qts, kts, eqs, eks


def _hgrn_fwd(zh, lb3, *, tb=512):
    t = zh.shape[0]
    nh = lb3.shape[0]
    c = HG_CHUNK
    tb = min(tb, t)
    nchunk = tb // c
    hp = HG_HP if nh % HG_HP == 0 else 1

    def body(hq_ref, hf_ref, hi_ref, lb_ref, o_ref, st_ref, state):
        @pl.when(pl.program_id(1) == 0)
        def _():
            state[...] = jnp.zeros_like(state)

        tril = _tri(c).astype(BF16)

        def one_head(hh, ci, sl):
            ls = slice(hh * HG_DK, (hh + 1) * HG_DK)
            q, _, _, f = _hg_gates(hq_ref[sl, ls], hf_ref[sl, ls], lb_ref[hh])
            v = hi_ref[sl, ls]
            kk = 1.0 - f
            g = _exact_tri_matmul(tril, jnp.log(f))
            a, _, _, _, _ = _hg_intra(q, kk, g)
            st = state[hh]
            st_ref[hh, ci] = st
            vb = v.astype(BF16)
            o = _dot(a.astype(BF16), vb) + _dot_nt((q * jnp.exp(g)).astype(BF16), st.astype(BF16))
            o_ref[sl, ls] = o
            glast = g[c - 1:c, :]
            kg = kk * jnp.exp(glast - g)
            state[hh] = st * jnp.exp(glast) + _dot_tn(vb, kg.astype(BF16))

        def chunk(ci, carry):
            sl = pl.ds(pl.multiple_of(ci * c, c), c)
            for hh in range(hp):
                one_head(hh, ci, sl)
            return carry

        lax.fori_loop(0, nchunk, chunk, 0)

    def col(cb):
        return pl.BlockSpec((tb, hp * HG_DK), lambda h, i: (i, cb * (nh // hp) + h))

    return pl.pallas_call(
        body,
        out_shape=[jax.ShapeDtypeStruct((t, nh * HG_DK), F32), jax.ShapeDtypeStruct((nh, t // c, HG_DK, HG_DK), F32)],
        grid=(nh // hp, t // tb),
        in_specs=[col(0), col(1), col(2), pl.BlockSpec((hp, 1, HG_DK), lambda h, i: (h, 0, 0))],
        out_specs=[pl.BlockSpec((tb, hp * HG_DK), lambda h, i: (i, h)),
                   pl.BlockSpec((hp, nchunk, HG_DK, HG_DK), lambda h, i: (h, i, 0, 0))],
        scratch_shapes=[pltpu.VMEM((hp, HG_DK, HG_DK), F32)],
        compiler_params=_cparams(("parallel", "arbitrary")),
        name="hgrn_fwd",
    )(zh, zh, zh, lb3)


def _hgrn_bwd(zh, lb3, states, d_o, *, tb=512):
    t = zh.shape[0]
    nh = lb3.shape[0]
    c = HG_CHUNK
    tb = min(tb, t)
    nchunk = tb // c
    nblk = t // tb
    hp = HG_HP if nh % HG_HP == 0 else 1

    def body(hq_ref, hf_ref, hi_ref, lb_ref, st_ref, do_ref, dq_ref, df_ref, dv_ref, dlb_ref, dstate):
        @pl.when(pl.program_id(1) == 0)
        def _():
            dstate[...] = jnp.zeros_like(dstate)
            dlb_ref[...] = jnp.zeros_like(dlb_ref)

        tril = _tri(c).astype(BF16)
        triu = _tri(c, upper=True).astype(BF16)
        last_row = lax.broadcasted_iota(jnp.int32, (c, 1), 0) == c - 1

        def one_head(hh, ci, sl):
            ls = slice(hh * HG_DK, (hh + 1) * HG_DK)
            lb = lb_ref[hh]
            hq, hf = hq_ref[sl, ls], hf_ref[sl, ls]
            q, sq, sg, f = _hg_gates(hq, hf, lb)
            v = hi_ref[sl, ls]
            kk = 1.0 - f
            g = _exact_tri_matmul(tril, jnp.log(f))
            a, qts, kts, eqs, eks = _hg_intra(q, kk, g)
            st = st_ref[hh, ci]
            dst = dstate[hh]
            do = do_ref[sl, ls]
            dob, vb = do.astype(BF16), v.astype(BF16)
            glast = g[c - 1:c, :]
            eg = jnp.exp(g)
            egl = jnp.exp(glast - g)
            qg = q * eg
            kg = kk * egl
            dv = _dot_tn(a.astype(BF16), dob) + _dot_nt(kg.astype(BF16), dst.astype(BF16))
            da = jnp.where(_tri(c), _dot_nt(dob, vb), 0.0).astype(BF16)
            dq_parts, dgq_parts = [], []
            dk = jnp.zeros_like(kk)
            dgk = jnp.zeros_like(kk)
            for i in range(c // HG_SUB):
                da_i = da[i * HG_SUB:(i + 1) * HG_SUB, :]
                ktb, qtb = kts[i].astype(BF16), qts[i].astype(BF16)
                xi = _dot(da_i, ktb)
                yi = _dot_tn(da_i, qtb)
                dq_parts.append(xi * eqs[i])
                dk = dk + yi * eks[i]
                dgq_parts.append(xi * qtb.astype(F32))
                dgk = dgk + yi * ktb.astype(F32)
            dq_inter = _dot(dob, st.astype(BF16)) * eg
            dq = jnp.concatenate(dq_parts, axis=0) + dq_inter
            dk_state = _dot(vb, dst.astype(BF16)) * egl
            dk = dk + dk_state
            dg = jnp.concatenate(dgq_parts, axis=0) - dgk + q * dq_inter - kk * dk_state
            dgl = jnp.sum(kk * dk_state, axis=0, keepdims=True) + jnp.exp(glast) * jnp.sum(st * dst, axis=0, keepdims=True)
            dg = dg + jnp.where(last_row, dgl, 0.0)
            dlogf = _exact_tri_matmul(triu, dg)
            dfv = dlogf / f - dk
            dq_ref[sl, ls] = (dq * (sq * (1.0 + hq * (1.0 - sq)))).astype(dq_ref.dtype)
            df_ref[sl, ls] = (dfv * (1.0 - lb) * sg * (1.0 - sg)).astype(df_ref.dtype)
            dv_ref[sl, ls] = dv.astype(dv_ref.dtype)
            dlb_ref[hh] += jnp.sum(dfv * (1.0 - sg), axis=0, keepdims=True)
            dstate[hh] = dst * jnp.exp(glast) + _dot_tn(dob, qg.astype(BF16))

        def chunk(j, carry):
            ci = nchunk - 1 - j
            sl = pl.ds(pl.multiple_of(ci * c, c), c)
            for hh in range(hp):
                one_head(hh, ci, sl)
            return carry

        lax.fori_loop(0, nchunk, chunk, 0)

    def col(cb):
        return pl.BlockSpec((tb, hp * HG_DK), lambda h, i: (nblk - 1 - i, cb * (nh // hp) + h))

    ocol = pl.BlockSpec((tb, hp * HG_DK), lambda h, i: (nblk - 1 - i, h))
    w = nh * HG_DK
    dq, df, dv, dlb = pl.pallas_call(
        body,
        out_shape=[jax.ShapeDtypeStruct((t, w), BF16)] * 3 + [jax.ShapeDtypeStruct((nh, 1, HG_DK), F32)],
        grid=(nh // hp, nblk),
        in_specs=[col(0), col(1), col(2), pl.BlockSpec((hp, 1, HG_DK), lambda h, i: (h, 0, 0)),
                  pl.BlockSpec((hp, nchunk, HG_DK, HG_DK), lambda h, i: (h, nblk - 1 - i, 0, 0)), ocol],
        out_specs=[ocol, ocol, ocol, pl.BlockSpec((hp, 1, HG_DK), lambda h, i: (h, 0, 0))],
        scratch_shapes=[pltpu.VMEM((hp, HG_DK, HG_DK), F32)],
        compiler_params=_cparams(("parallel", "arbitrary")),
        name="hgrn_bwd",
    )(zh, zh, zh, lb3, states, d_o)
    return dq, df, dv, dlb.reshape(w)


NEG = -1e30
ATT_GW = ATT_HEADS * ATT_DH


def _att_scores(q, kp, kc, has_prev):
    scale = ATT_DH ** -0.5
    i = lax.broadcasted_iota(jnp.int32, (ATT_BLK, ATT_BLK), 0)
    j = lax.broadcasted_iota(jnp.int32, (ATT_BLK, ATT_BLK), 1)
    s_p = jnp.where(jnp.logical_and(j >= i, has_prev), _dot_nt(q, kp) * scale, NEG)
    s_c = jnp.where(j <= i, _dot_nt(q, kc) * scale, NEG)
    return s_p, s_c


def _att_views(arrs, d):
    return [a.reshape(d, -1, ATT_GW) for a in arrs]


def _att_unview(a, d):
    return a.reshape(-1, ATT_GW) if d == 1 else a


def _attn_fwd(qb, kb, vb, g):
    d = ATT_PATTERNS[g][1]
    q2, k2, v2 = _att_views([qb, kb, vb], d)
    nb = q2.shape[1] // ATT_BLK

    def body(q_ref, kc_ref, kp_ref, vc_ref, vp_ref, o_ref, l_ref):
        has_prev = pl.program_id(1) > 0
        for h in range(ATT_HEADS):
            hs = slice(h * ATT_DH, (h + 1) * ATT_DH)
            s_p, s_c = _att_scores(q_ref[:, hs], kp_ref[:, hs], kc_ref[:, hs], has_prev)
            m = jnp.maximum(jnp.max(s_p, axis=1, keepdims=True), jnp.max(s_c, axis=1, keepdims=True))
            p_p, p_c = jnp.exp(s_p - m), jnp.exp(s_c - m)
            l = jnp.sum(p_p, axis=1, keepdims=True) + jnp.sum(p_c, axis=1, keepdims=True)
            o = _dot(p_p.astype(BF16), vp_ref[:, hs]) + _dot(p_c.astype(BF16), vc_ref[:, hs])
            o_ref[:, hs] = o / l
            l_ref[:, hs] = jnp.broadcast_to(m + jnp.log(l), (ATT_BLK, ATT_DH))

    cur = pl.BlockSpec((None, ATT_BLK, ATT_GW), lambda r, n: (r, n, 0))
    prev = pl.BlockSpec((None, ATT_BLK, ATT_GW), lambda r, n: (r, jnp.maximum(n - 1, 0), 0))
    o, lse = pl.pallas_call(
        body,
        out_shape=[jax.ShapeDtypeStruct(q2.shape, F32)] * 2,
        grid=(d, nb),
        in_specs=[cur, cur, prev, cur, prev],
        out_specs=[cur, cur],
        compiler_params=_cparams(("parallel", "arbitrary")),
        name=f"attn_fwd_g{g}",
    )(q2, k2, k2, v2, v2)
    return _att_unview(o, d), _att_unview(lse, d)


def _attn_bwd(qb, kb, vb, o, lse, d_o, d_lse, g):
    d = ATT_PATTERNS[g][1]
    q2, k2, v2 = _att_views([qb, kb, vb], d)
    o2, l2, do2, dl2 = _att_views([o, lse, d_o, d_lse], d)
    nb = q2.shape[1] // ATT_BLK

    def body(q_ref, kc_ref, kp_ref, vc_ref, vp_ref, o_ref, l_ref, do_ref, dl_ref, dq_ref, dk_ref, dv_ref, ck, cv):
        n = pl.program_id(1)
        active = n < nb

        @pl.when(n == 0)
        def _():
            ck[...] = jnp.zeros_like(ck)
            cv[...] = jnp.zeros_like(cv)

        @pl.when(jnp.logical_not(active))
        def _():
            dk_ref[...] = ck[...]
            dv_ref[...] = cv[...]

        @pl.when(active)
        def _():
            has_prev = n > 0
            for h in range(ATT_HEADS):
                hs = slice(h * ATT_DH, (h + 1) * ATT_DH)
                q, kp, kc, vp, vc = q_ref[:, hs], kp_ref[:, hs], kc_ref[:, hs], vp_ref[:, hs], vc_ref[:, hs]
                s_p, s_c = _att_scores(q, kp, kc, has_prev)
                lse_h = l_ref[:, hs][:, 0:1]
                p_p, p_c = jnp.exp(s_p - lse_h), jnp.exp(s_c - lse_h)
                do = do_ref[:, hs]
                delta = jnp.sum(do * o_ref[:, hs] - dl_ref[:, hs], axis=1, keepdims=True)
                dob = do.astype(BF16)
                scale = ATT_DH ** -0.5
                ds_p = (p_p * (_dot_nt(dob, vp) - delta) * scale).astype(BF16)
                ds_c = (p_c * (_dot_nt(dob, vc) - delta) * scale).astype(BF16)
                dq_ref[:, hs] = _dot(ds_p, kp) + _dot(ds_c, kc)
                dk_ref[:, hs] = ck[:, hs] + _dot_tn(ds_p, q)
                dv_ref[:, hs] = cv[:, hs] + _dot_tn(p_p.astype(BF16), dob)
                ck[:, hs] = _dot_tn(ds_c, q)
                cv[:, hs] = _dot_tn(p_c.astype(BF16), dob)

    def qn(n):
        return jnp.minimum(n, nb - 1)

    cur = pl.BlockSpec((None, ATT_BLK, ATT_GW), lambda r, n: (r, qn(n), 0))
    prev = pl.BlockSpec((None, ATT_BLK, ATT_GW), lambda r, n: (r, jnp.maximum(qn(n) - 1, 0), 0))
    behind = pl.BlockSpec((None, ATT_BLK, ATT_GW), lambda r, n: (r, jnp.maximum(n - 1, 0), 0))
    shp = jax.ShapeDtypeStruct(q2.shape, F32)
    dq, dk, dv = pl.pallas_call(
        body,
        out_shape=[shp, shp, shp],
        grid=(d, nb + 1),
        in_specs=[cur, cur, prev, cur, prev, cur, cur, cur, cur],
        out_specs=[cur, behind, behind],
        scratch_shapes=[pltpu.VMEM((ATT_BLK, ATT_GW), F32), pltpu.VMEM((ATT_BLK, ATT_GW), F32)],
        compiler_params=_cparams(("parallel", "arbitrary")),
        name=f"attn_bwd_g{g}",
    )(q2, k2, k2, v2, v2, o2, l2, do2, dl2)
    return _att_unview(dq, d), _att_unview(dk, d), _att_unview(dv, d)


def _rms_parts(x, width):
    outs = []
    for lo in range(0, x.shape[1], width):
        xs = x[:, lo:lo + width]
        r = lax.rsqrt(jnp.mean(xs * xs, axis=1, keepdims=True) + EPS)
        outs.append((xs * r, r))
    return outs


def _rms_bwd_part(xh, r, dxh):
    return r * (dxh - xh * jnp.mean(dxh * xh, axis=1, keepdims=True))


def _norm_fwd(x, gain):
    d = x.shape[1]

    def fn(ins, consts):
        (xh, _), = _rms_parts(ins[0], d)
        return [xh * consts[0]], []

    (h,), _ = _rowwise(fn, [(x, d, 0)], [gain.reshape(1, d)], [(d, BF16)], [], bm=512, name="norm_fwd")
    return h


def _norm_bwd(x, gain, dh, dres):
    d = x.shape[1]

    def fn(ins, consts):
        (xh, r), = _rms_parts(ins[0], d)
        dx = ins[2] + _rms_bwd_part(xh, r, ins[1] * consts[0])
        return [dx], [_colsum8(ins[1] * xh)]

    (dx,), (dg,) = _rowwise(fn, [(x, d, 0), (dh, d, 0), (dres, d, 0)], [gain.reshape(1, d)], [(d, F32)], [d],
                            bm=512, name="norm_bwd")
    return dx, dg


def _rot_sign():
    lane = lax.broadcasted_iota(jnp.int32, (1, ATT_DH), 1)
    return jnp.where(lane < ATT_DH // 2, -1.0, 1.0).astype(F32)


def _rope(y, cos, sin):
    return y * cos + pltpu.roll(y, ATT_DH // 2, axis=1) * _rot_sign() * sin


def _rope_t(dy, cos, sin):
    return dy * cos - pltpu.roll(dy * sin, ATT_DH // 2, axis=1) * _rot_sign()


def _qk_prep(zq, zk, zv, qn, kn, cos, sin):
    w = zq.shape[1]

    def fn(ins, consts):
        cs, sn = ins[3], ins[4]
        outs = []
        for z, gain in ((ins[0], consts[0]), (ins[1], consts[1])):
            for i, (xh, _) in enumerate(_rms_parts(z, ATT_DH)):
                outs.append(_rope(xh * gain[:, i * ATT_DH:(i + 1) * ATT_DH], cs, sn))
        outs += [ins[2][:, i * ATT_DH:(i + 1) * ATT_DH] for i in range(w // ATT_DH)]
        groups = [jnp.concatenate(outs[i:i + ATT_HEADS], axis=1) for i in range(0, len(outs), ATT_HEADS)]
        return groups, []

    outs, _ = _rowwise(fn, [(zq, w, 0), (zk, w, 0), (zv, w, 0), (cos, ATT_DH, 0), (sin, ATT_DH, 0)], [qn, kn],
                       [(ATT_GW, BF16, ATT_PATTERNS[g][1]) for g in range(ATT_GROUPS)] * 3, [], bm=256, name="qk_prep")
    return outs[0:3], outs[3:6], outs[6:9]


def _qk_prep_bwd(zq, zk, dq_g, dk_g, dv_g, qn, kn, cos, sin):
    w = zq.shape[1]

    def fn(ins, consts):
        cs, sn = ins[2], ins[3]
        outs, sums = [], []
        for z, gain, dparts in ((ins[0], consts[0], ins[4:7]), (ins[1], consts[1], ins[7:10])):
            dout = jnp.concatenate(dparts, axis=1)
            dz, dgain = [], []
            for i, (xh, r) in enumerate(_rms_parts(z, ATT_DH)):
                hs = slice(i * ATT_DH, (i + 1) * ATT_DH)
                dy = _rope_t(dout[:, hs], cs, sn)
                dgain.append(_colsum8(dy * xh))
                dz.append(_rms_bwd_part(xh, r, dy * gain[:, hs]))
            outs.append(jnp.concatenate(dz, axis=1))
            sums.append(jnp.concatenate(dgain, axis=1))
        outs.append(jnp.concatenate(ins[10:13], axis=1))
        return outs, sums

    ins = [(zq, w, 0), (zk, w, 0), (cos, ATT_DH, 0), (sin, ATT_DH, 0)]
    for parts in (dq_g, dk_g, dv_g):
        ins += [(a, ATT_GW, 0, ATT_PATTERNS[g][1]) for g, a in enumerate(parts)]
    (dzq, dzk, dzv), (dqn, dkn) = _rowwise(fn, ins, [qn, kn], [(w, BF16)] * 3, [w, w], bm=256, name="qk_prep_bwd")
    return dzq, dzk, dzv, dqn, dkn


def _post_a(o_raw, zh, gout):
    w = o_raw.shape[1]

    def fn(ins, consts):
        oh = jnp.concatenate([xh for xh, _ in _rms_parts(ins[0], HG_DK)], axis=1)
        hg = ins[1]
        return [oh * consts[0] * (hg * _sigmoid(hg))], []

    (y,), _ = _rowwise(fn, [(o_raw, w, 0), (zh, w, 3)], [gout.reshape(1, w)], [(w, BF16)], [], bm=512, name="post_a")
    return y


def _post_a_bwd(o_raw, zh, gout, dy):
    w = o_raw.shape[1]

    def fn(ins, consts):
        parts = _rms_parts(ins[0], HG_DK)
        oh = jnp.concatenate([xh for xh, _ in parts], axis=1)
        hg, dyv, gain = ins[1], ins[2], consts[0]
        sg = _sigmoid(hg)
        s = hg * sg
        doh = dyv * gain * s
        do = jnp.concatenate([_rms_bwd_part(xh, r, doh[:, i * HG_DK:(i + 1) * HG_DK]) for i, (xh, r) in enumerate(parts)], axis=1)
        dhg = dyv * oh * gain * (sg * (1.0 + hg * (1.0 - sg)))
        return [do, dhg], [_colsum8(dyv * oh * s)]

    (do, dhg), (dgain,) = _rowwise(fn, [(o_raw, w, 0), (zh, w, 3), (dy, w, 0)], [gout.reshape(1, w)],
                                   [(w, F32), (w, BF16)], [w], bm=512, name="post_a_bwd")
    return do, dhg, dgain


def _merge_alpha(lses):
    m = jnp.maximum(jnp.maximum(lses[0], lses[1]), lses[2])
    e = [jnp.exp(l - m) for l in lses]
    inv = 1.0 / (e[0] + e[1] + e[2])
    return [x * inv for x in e]


def _group_ins(parts):
    return [(a, ATT_GW, 0, ATT_PATTERNS[g][1]) for g, a in enumerate(parts)]


def _merge_b(o_g, lse_g):
    def fn(ins, consts):
        al = _merge_alpha(ins[3:6])
        return [al[0] * ins[0] + al[1] * ins[1] + al[2] * ins[2]], []

    (y,), _ = _rowwise(fn, _group_ins(o_g) + _group_ins(lse_g), [], [(ATT_GW, BF16)], [], bm=512, name="merge_b")
    return y


def _merge_b_bwd(o_g, lse_g, dy):
    def fn(ins, consts):
        al = _merge_alpha(ins[3:6])
        dyv = ins[6]
        dal = [dyv * ins[i] for i in range(3)]
        tot = al[0] * dal[0] + al[1] * dal[1] + al[2] * dal[2]
        return [al[i] * dyv for i in range(3)] + [al[i] * (dal[i] - tot) for i in range(3)], []

    outs, _ = _rowwise(fn, _group_ins(o_g) + _group_ins(lse_g) + [(dy, ATT_GW, 0)], [],
                       [(ATT_GW, F32, ATT_PATTERNS[g][1]) for g in range(ATT_GROUPS)] * 2, [], bm=512, name="merge_b_bwd")
    return outs[:3], outs[3:]


def _loss_head(y, target):
    d = y.shape[1]

    def fn(ins, consts):
        e = ins[0] - ins[1]
        return [e * (1.0 / d)], [_colsum8(e * e)]

    (dy,), (sq,) = _rowwise(fn, [(y, d, 0), (target, d, 0)], [], [(d, F32)], [d], bm=512, name="loss_head")
    return 0.5 * jnp.sum(sq) / d, dy


def _silu_grad(a):
    s = _sigmoid(a)
    return s * (1.0 + a * (1.0 - s))


def _ffn_fwd(x, gain, wt, wo_fn, tag):
    t, d = x.shape
    f = wt.shape[0] // 2
    h = _norm_fwd(x, gain)

    def act(accs, ex):
        a, b = accs
        return (a * _sigmoid(a) * b, a, b)

    u, a, b = _mm([h], [wt, wt], [(0, 0, 0), (0, 1, 1)], 2, act, [BF16, BF16, BF16], m=t, n=f, k=d, tb=True,
                  bm=1024, bn=256, bk=d, b_off=[(0, 0), (f // min(256, f), 0)], name=f"ffn_in_{tag}")
    wo = wo_fn(u)
    (y,) = _mm([u], [wo], [(0, 0, 0)], 1, lambda accs, ex: (ex[0] + 0.5 * accs[0],), [F32], m=t, n=d, k=f,
               bm=512, bn=d, bk=f, extras=[x], name=f"ffn_out_{tag}")
    return y, (x, h, u, a, b, wo)


def _ffn_bwd(dy, saved, gain, wt, tag, tok):
    x, h, u, a, b, wo = saved
    t, d = x.shape
    f = wo.shape[0]
    dyb = (dy + tok).astype(BF16)

    def dact(accs, ex):
        du = 0.5 * accs[0]
        av, bv = ex[0].astype(F32), ex[1].astype(F32)
        return (du * bv * _silu_grad(av), du * av * _sigmoid(av))

    da, db = _mm([dyb], [wo], [(0, 0, 0)], 1, dact, [BF16, BF16], m=t, n=f, k=d, tb=True, bm=1024, bn=256, bk=d,
                 extras=[a, b], name=f"ffn_dact_{tag}")
    (dwo,) = _mm([u], [dyb], [(0, 0, 0)], 1, lambda accs, ex: (0.5 * accs[0],), [BF16], m=f, n=d, k=t, ta=True,
                 bm=1408, bn=d, bk=1024, name=f"ffn_dwo_{tag}")
    (dh,) = _mm([da, db], [wt, wt], [(0, 0, 0), (1, 1, 0)], 1, _first, [F32], m=t, n=d, k=f, bm=512, bn=d, bk=f,
                b_off=[(0, 0), (0, 1)], name=f"ffn_dh_{tag}")
    dwt = [_mm([g], [h], [(0, 0, 0)], 1, _first, [BF16], m=f, n=d, k=t, ta=True, bm=1408, bn=d, bk=1024,
               name=f"ffn_dwt{i}_{tag}")[0] for i, g in enumerate((da, db))]
    dx, dgain = _norm_bwd(x, gain, dh, dy)
    return dx, dgain, jnp.concatenate(dwt, axis=0), dwo


Z_SPLITS = (("h", 4096), ("q", 1536), ("k", 1536), ("v", 1536), ("g", 2048))


def _mix_fwd(x, p, cos, sin):
    t, d = x.shape
    hm = _norm_fwd(x, p["gm"])
    z, off = {}, 0
    for nm, width in Z_SPLITS:
        (z[nm],) = _mm([hm], [p["wint"]], [(0, 0, 0)], 1, _first, [F32], m=t, n=width, k=d, tb=True, bm=1024, bn=512, bk=d,
                       b_off=[(off // 512, 0)], name=f"mix_in_{nm}")
        off += width
    o_raw, states = _hgrn_fwd(z["h"], p["lb3"])
    qb, kb, vb = _qk_prep(z["q"], z["k"], z["v"], p["qn"], p["kn"], cos, sin)
    o_g, lse_g = zip(*[_attn_fwd(qb[g], kb[g], vb[g], g) for g in range(ATT_GROUPS)])
    oa = _post_a(o_raw, z["h"], p["gout"])
    ob = _merge_b(o_g, lse_g)
    late = p["late"](ob)
    p = dict(p, **late)
    (ya,) = _mm([oa], [p["wa"]], [(0, 0, 0)], 1, _first, [F32], m=t, n=d, k=oa.shape[1], bm=1024, bn=d, bk=oa.shape[1],
                name="branch_a")

    def gate(accs, ex):
        return (_sigmoid(ex[0]) * ex[2] + _sigmoid(ex[1]) * accs[0], accs[0])

    merged, yb = _mm([ob], [p["wbt"]], [(0, 0, 0)], 1, gate, [BF16, F32], m=t, n=d, k=ATT_GW, tb=True, bm=512, bn=d,
                     bk=ATT_GW, extras=[z["g"], z["g"], ya], e_off=[0, 1, 0], name="branch_b_gate")
    (y,) = _mm([merged], [p["wo"]], [(0, 0, 0)], 1, lambda accs, ex: (ex[0] + accs[0],), [F32], m=t, n=d, k=d,
               bm=1024, bn=d, bk=d, extras=[x], name="mix_out")
    return y, (x, hm, z, o_raw, states, qb, kb, vb, o_g, lse_g, oa, ob, ya, yb, merged, late)


def _mix_bwd(dy, saved, p, cos, sin, tok):
    x, hm, z, o_raw, states, qb, kb, vb, o_g, lse_g, oa, ob, ya, yb, merged, late = saved
    p = dict(p, **late)
    t, d = x.shape
    w = oa.shape[1]
    dyb = (dy + tok).astype(BF16)

    def dgate(accs, ex):
        dm = accs[0]
        sa, sb = _sigmoid(ex[0]), _sigmoid(ex[1])
        return (sa * dm, sb * dm, dm * ex[2] * sa * (1.0 - sa), dm * ex[3] * sb * (1.0 - sb))

    dya, dyb_, dga, dgb = _mm([dyb], [p["wo"]], [(0, 0, 0)], 1, dgate, [BF16] * 4, m=t, n=d, k=d, tb=True, bm=512, bn=d,
                              bk=d, extras=[z["g"], z["g"], ya, yb], e_off=[0, 1, 0, 0], name="mix_out_bwd")
    (dwo,) = _mm([merged], [dyb], [(0, 0, 0)], 1, _first, [BF16], m=d, n=d, k=t, ta=True, bm=d, bn=d, bk=1024, name="mix_dwo")
    (doa,) = _mm([dya], [p["wa"]], [(0, 0, 0)], 1, _first, [F32], m=t, n=w, k=d, tb=True, bm=1024, bn=w, bk=d, name="branch_a_bwd")
    (dwa,) = _mm([oa], [dya], [(0, 0, 0)], 1, _first, [BF16], m=w, n=d, k=t, ta=True, bm=w, bn=d, bk=1024, name="branch_a_dw")
    (dob,) = _mm([dyb_], [p["wbt"]], [(0, 0, 0)], 1, _first, [F32], m=t, n=ATT_GW, k=d, bm=1024, bn=ATT_GW, bk=d,
                 name="branch_b_bwd")
    (dwbt,) = _mm([dyb_], [ob], [(0, 0, 0)], 1, _first, [BF16], m=d, n=ATT_GW, k=t, ta=True, bm=d, bn=ATT_GW, bk=1024,
                  name="branch_b_dw")
    do_raw, dhg, dgout = _post_a_bwd(o_raw, z["h"], p["gout"], doa)
    do_g, dlse_g = _merge_b_bwd(o_g, lse_g, dob)
    dq_g, dk_g, dv_g = zip(*[_attn_bwd(qb[g], kb[g], vb[g], o_g[g], lse_g[g], do_g[g], dlse_g[g], g)
                             for g in range(ATT_GROUPS)])
    dzq, dzk, dzv, dqn, dkn = _qk_prep_bwd(z["q"], z["k"], dq_g, dk_g, dv_g, p["qn"], p["kn"], cos, sin)
    dhq, dhf, dhi, lbsum = _hgrn_bwd(z["h"], p["lb3"], states, do_raw)
    dz = jnp.concatenate([dhq, dhf, dhi, dhg, dzq, dzk, dzv, dga, dgb], axis=1)
    pw = dz.shape[1]
    (dhm,) = _mm([dz], [p["wint"]], [(0, 0, 0)], 1, _first, [F32], m=t, n=d, k=pw, bm=1024, bn=d, bk=1536, name="mix_in_bwd")
    (dwint,) = _mm([dz], [hm], [(0, 0, 0)], 1, _first, [BF16], m=pw, n=d, k=t, ta=True, bm=1536, bn=d, bk=1024, name="mix_in_dw")
    dx, dgm = _norm_bwd(x, p["gm"], dhm, dy)
    return dx, dict(gm=dgm, wint=dwint, lbsum=lbsum, gout=dgout, qn=dqn, kn=dkn, wa=dwa, wbt=dwbt, wo=dwo)


def _rope_tables(t):
    pos = jnp.arange(t, dtype=F32)
    inv = ROPE_THETA ** (-jnp.arange(0, ATT_DH, 2, dtype=F32) / ATT_DH)
    ang = pos[:, None] * inv[None, :]
    ang = jnp.concatenate([ang, ang], axis=-1)
    return jnp.cos(ang), jnp.sin(ang)


def _lower_bounds(logits):
    lb = jnp.cumsum(jax.nn.softmax(logits, axis=0), axis=0)
    return lb - lb[0:1]


def _head_gain(g):
    return jnp.tile(g[:, None, :], (1, ATT_HEADS, 1)).reshape(1, ATT_GROUPS * ATT_GW)


SMALL_GRADS = ("ffn1_norm", "mix_norm", "lbsum", "hgrn_out_norm", "attn_q_norm", "attn_k_norm", "ffn2_norm")


def _local_step(x, target, small, fetch, emit):
    t = x.shape[0]
    depth = small["ffn1_norm"].shape[0]
    cos, sin = _rope_tables(t)
    lb_all = _lower_bounds(small["hgrn_lb_logits"])
    saved = []
    for l in range(depth):
        w1t = fetch("w1t", l, x)["w1t"]
        x, s1 = _ffn_fwd(x, small["ffn1_norm"][l], w1t, lambda after, l=l: fetch("w1o", l, after)["w1o"], "1")
        p = dict(gm=small["mix_norm"][l], wint=fetch("wint", l, x)["wint"], lb3=lb_all[l].reshape(-1, 1, HG_DK),
                 gout=small["hgrn_out_norm"][l], qn=_head_gain(small["attn_q_norm"][l]),
                 kn=_head_gain(small["attn_k_norm"][l]), late=functools.partial(fetch, "mout", l))
        x, sm = _mix_fwd(x, p, cos, sin)
        w2t = fetch("w2t", l, x)["w2t"]
        x, s2 = _ffn_fwd(x, small["ffn2_norm"][l], w2t, lambda after, l=l: fetch("w2o", l, after)["w2o"], "2")
        saved.append((p, w1t, w2t, s1, sm, s2))
    loss, dx = _loss_head(x, target)
    gsmall = {k: [None] * depth for k in SMALL_GRADS}
    tok = jnp.zeros((), F32)
    for l in reversed(range(depth)):
        p, w1t, w2t, s1, sm, s2 = saved[l]
        dx, gsmall["ffn2_norm"][l], dw2t, dw2o = _ffn_bwd(dx, s2, small["ffn2_norm"][l], w2t, "2", tok)
        tok = emit("ffn2", l, dict(w2t=dw2t, w2o=dw2o), None)
        dx, gm = _mix_bwd(dx, sm, p, cos, sin, tok)
        tok = emit("mix", l, {k: gm[k] for k in ("wint", "wa", "wbt", "wo")}, None)
        gsmall["mix_norm"][l], gsmall["lbsum"][l], gsmall["hgrn_out_norm"][l] = gm["gm"], gm["lbsum"], gm["gout"]
        for k, src in (("attn_q_norm", "qn"), ("attn_k_norm", "kn")):
            gsmall[k][l] = jnp.sum(gm[src].reshape(ATT_GROUPS, ATT_HEADS, ATT_DH), axis=1)
        dx, gsmall["ffn1_norm"][l], dw1t, dw1o = _ffn_bwd(dx, s1, small["ffn1_norm"][l], w1t, "1", tok)
        final = ({k: jnp.stack(v) for k, v in gsmall.items()}, loss) if l == 0 else None
        tok = emit("ffn1", l, dict(w1t=dw1t, w1o=dw1o), final)
    return dx


_HBM = pl.BlockSpec(memory_space=pltpu.HBM)
_SEM = pl.BlockSpec(memory_space=pltpu.SEMAPHORE)
_EFFECT = pltpu.SideEffectType.DATAFLOW_SIDE_EFFECTING


def _peer(p):
    x, y, c = lax.axis_index("x"), lax.axis_index("y"), lax.axis_index("c")
    me = 4 * x + 2 * y + c
    return (1 - x if p & 4 else x, 1 - y if p & 2 else y, 1 - c if p & 1 else c), jnp.bitwise_xor(me, p), me


def _xchg_copy(src, land, mode, send_sems, recv_sems, k, p, arriving):
    peer, peer_id, me = _peer(p)
    block = src if mode == "gather" else src.at[peer_id]
    return pltpu.make_async_remote_copy(
        src_ref=block, dst_ref=land.at[peer_id if arriving else me], send_sem=send_sems.at[k * (N_DEV - 1) + p - 1],
        recv_sem=recv_sems.at[k * (N_DEV - 1) + p - 1], device_id=peer, device_id_type=MESH)


def _xchg_start(srcs, modes, groups, name):
    n, ng = len(srcs), len(groups)
    lands = [lax.empty((N_DEV,) + a.shape[-2:], a.dtype) for a in srcs]

    def body(*refs):
        src, land = refs[:n], refs[n:2 * n]
        sems = refs[2 * n:2 * n + 2 * ng]
        token = refs[2 * n + 2 * ng + 2 * n]
        own_sems = refs[2 * n + 2 * ng + 2 * n + 1]
        me = _peer(0)[2]
        own = [pltpu.make_async_copy(src[k] if modes[k] == "gather" else src[k].at[me], land[k].at[me], own_sems.at[k])
               for k in range(n)]
        for cp in own:
            cp.start()
        for gi, idx in enumerate(groups):
            for ki, k in enumerate(idx):
                for p in range(1, N_DEV):
                    _xchg_copy(src[k], land[k], modes[k], sems[2 * gi], sems[2 * gi + 1], ki, p, False).start()
        token[...] = jnp.zeros_like(token)
        for cp in own:
            cp.wait()

    sem_shapes = []
    for idx in groups:
        sem_shapes += [pltpu.SemaphoreType.DMA((len(idx) * (N_DEV - 1),))] * 2
    outs = pl.pallas_call(
        body,
        out_shape=sem_shapes + [pltpu.HBM(a.shape, a.dtype) for a in srcs] + [pltpu.HBM(a.shape, a.dtype) for a in lands]
        + [jax.ShapeDtypeStruct((8, 128), F32)],
        in_specs=[_HBM] * (2 * n),
        out_specs=[_SEM] * (2 * ng) + [_HBM] * (2 * n) + [pl.BlockSpec(memory_space=pltpu.VMEM)],
        input_output_aliases={i: 2 * ng + i for i in range(2 * n)},
        scratch_shapes=[pltpu.SemaphoreType.DMA((n,))],
        compiler_params=pltpu.CompilerParams(has_side_effects=_EFFECT),
        name=name,
    )(*[pltpu.with_memory_space_constraint(a, pltpu.HBM) for a in list(srcs) + lands])
    sems = [(outs[2 * gi], outs[2 * gi + 1]) for gi in range(ng)]
    return sems, outs[2 * ng:2 * ng + n], outs[2 * ng + n:2 * ng + 2 * n], outs[-1]


def _xchg_wait(srcs, lands, modes, sems, after, name):
    n = len(srcs)

    def body(*refs):
        src, land = refs[:n], refs[n:2 * n]
        send_sems, recv_sems = refs[2 * n], refs[2 * n + 1]
        for p in range(1, N_DEV):
            for k in range(n):
                cp = _xchg_copy(src[k], land[k], modes[k], send_sems, recv_sems, k, p, True)
                cp.wait_send()
                cp.wait_recv()

    outs = pl.pallas_call(
        body,
        out_shape=[pltpu.HBM(a.shape, a.dtype) for a in list(srcs) + list(lands)],
        in_specs=[_HBM] * (2 * n) + [_SEM, _SEM, pl.BlockSpec(memory_space=pl.ANY)],
        out_specs=[_HBM] * (2 * n),
        input_output_aliases={i: i for i in range(2 * n)},
        compiler_params=pltpu.CompilerParams(has_side_effects=_EFFECT),
        name=name,
    )(*srcs, *lands, sems[0], sems[1], after)
    return outs[n:]


def _sum_slots(land):
    g, _, r, c = land.shape
    br = r // 2 if (r % 32 == 0 and r >= 256) else r

    def body(l_ref, o_ref):
        acc = l_ref[0, 0].astype(F32)
        for j in range(1, N_DEV):
            acc = acc + l_ref[0, j].astype(F32)
        o_ref[0] = acc

    return pl.pallas_call(
        body,
        out_shape=jax.ShapeDtypeStruct((g, r, c), F32),
        grid=(g, r // br),
        in_specs=[pl.BlockSpec((1, N_DEV, br, c), lambda i, j: (i, 0, j, 0))],
        out_specs=pl.BlockSpec((1, br, c), lambda i, j: (i, j, 0)),
        compiler_params=_cparams(("parallel", "parallel")),
        name="sum_slots",
    )(land)


def _adamw(w, g, m, v):
    shape = w.shape
    cols = shape[-1]
    rows = int(np.prod(shape[:-1]))
    bm = max(b for b in range(8, 257, 8) if rows % b == 0) if rows % 8 == 0 else rows
    c1 = 1.0 - ADAM_B1 ** ADAM_STEP
    c2 = 1.0 - ADAM_B2 ** ADAM_STEP

    def fn(ins, consts):
        wv, gv, mv, vv = ins
        m2 = ADAM_B1 * mv + (1.0 - ADAM_B1) * gv
        v2 = ADAM_B2 * vv + (1.0 - ADAM_B2) * (gv * gv)
        delta = -ADAM_LR * ((m2 / c1) / (jnp.sqrt(v2 / c2) + ADAM_EPS) + ADAM_WD * wv)
        return [delta, m2, v2], []

    outs, _ = _rowwise(fn, [(a.reshape(rows, cols), cols, 0) for a in (w, g, m, v)], [], [(cols, F32)] * 3, [],
                       bm=bm, name="adamw")
    return [o.reshape(shape) for o in outs]


BIG = ("w1t", "w1o", "wint", "wa", "wbt", "wo", "w2t", "w2o")
FETCH_GROUPS = dict(w1t=("w1t",), w1o=("w1o",), wint=("wint",), mout=("wa", "wbt", "wo"), w2t=("w2t",), w2o=("w2o",))
SMALL_ROWS = (("ffn1_norm", 0), ("mix_norm", 2), ("lbsum", 4), ("hgrn_out_norm", 6), ("ffn2_norm", 8),
              ("attn_q_norm", 10), ("attn_k_norm", 12))
SMALL_PACK_ROWS = 16


def kernel(x, ffn1_norm, ffn1_w_in, ffn1_w_out, mix_norm, w_in, hgrn_lb_logits, hgrn_out_norm, attn_q_norm, attn_k_norm, w_branch_a, w_branch_b, w_out, ffn2_norm, ffn2_w_in, ffn2_w_out, loss_target, m_ffn1_norm, m_ffn1_w_in, m_ffn1_w_out, m_mix_norm, m_w_in, m_hgrn_lb_logits, m_hgrn_out_norm, m_attn_q_norm, m_attn_k_norm, m_w_branch_a, m_w_branch_b, m_w_out, m_ffn2_norm, m_ffn2_w_in, m_ffn2_w_out, v_ffn1_norm, v_ffn1_w_in, v_ffn1_w_out, v_mix_norm, v_w_in, v_hgrn_lb_logits, v_hgrn_out_norm, v_attn_q_norm, v_attn_k_norm, v_w_branch_a, v_w_branch_b, v_w_out, v_ffn2_norm, v_ffn2_w_in, v_ffn2_w_out):
    names = ("ffn1_norm", "ffn1_w_in", "ffn1_w_out", "mix_norm", "w_in", "hgrn_lb_logits", "hgrn_out_norm", "attn_q_norm",
             "attn_k_norm", "w_branch_a", "w_branch_b", "w_out", "ffn2_norm", "ffn2_w_in", "ffn2_w_out")
    w = dict(zip(names, (ffn1_norm, ffn1_w_in, ffn1_w_out, mix_norm, w_in, hgrn_lb_logits, hgrn_out_norm, attn_q_norm,
                         attn_k_norm, w_branch_a, w_branch_b, w_out, ffn2_norm, ffn2_w_in, ffn2_w_out)))
    m = dict(zip(names, (m_ffn1_norm, m_ffn1_w_in, m_ffn1_w_out, m_mix_norm, m_w_in, m_hgrn_lb_logits, m_hgrn_out_norm,
                         m_attn_q_norm, m_attn_k_norm, m_w_branch_a, m_w_branch_b, m_w_out, m_ffn2_norm, m_ffn2_w_in, m_ffn2_w_out)))
    v = dict(zip(names, (v_ffn1_norm, v_ffn1_w_in, v_ffn1_w_out, v_mix_norm, v_w_in, v_hgrn_lb_logits, v_hgrn_out_norm,
                         v_attn_q_norm, v_attn_k_norm, v_w_branch_a, v_w_branch_b, v_w_out, v_ffn2_norm, v_ffn2_w_in, v_ffn2_w_out)))
    depth, d = ffn1_norm.shape

    def tr(a):
        return jnp.swapaxes(a, 1, 2)

    shard = dict(w1t=tr(ffn1_w_in), w1o=ffn1_w_out, wint=tr(w_in), wa=w_branch_a,
                 wbt=tr(w_branch_b).reshape(depth, -1, d), wo=w_out, w2t=tr(ffn2_w_in), w2o=ffn2_w_out)
    order = [(g, l) for l in range(depth) for g in FETCH_GROUPS]
    flat = [(g, l, k) for g, l in order for k in FETCH_GROUPS[g]]
    groups, pos = [], 0
    for g, l in order:
        groups.append(list(range(pos, pos + len(FETCH_GROUPS[g]))))
        pos += len(FETCH_GROUPS[g])
    g_sems, g_srcs, g_lands, _ = _xchg_start([shard[k][l].astype(BF16) for _, l, k in flat], ["gather"] * len(flat),
                                             groups, "gather_start")

    def fetch(group, l, after):
        gi = order.index((group, l))
        idx = groups[gi]
        lands = _xchg_wait([g_srcs[i] for i in idx], [g_lands[i] for i in idx], ["gather"] * len(idx), g_sems[gi], after,
                           f"gather_wait_{group}{l}")
        out = {}
        for k, land in zip(FETCH_GROUPS[group], lands):
            out[k] = land.reshape(d, -1) if k == "wbt" else land.reshape(-1, d)
        return out

    pending = []

    def emit(group, l, g, final):
        keys = list(g)
        srcs = [g[k].reshape(N_DEV, -1, d) for k in keys]
        modes = ["scatter"] * len(keys)
        if final is not None:
            gsmall, loss = final
            pack = jnp.zeros((SMALL_PACK_ROWS, d), F32)
            for k, r0 in SMALL_ROWS:
                rows = gsmall[k].reshape(depth, -1)
                pack = pack.at[r0:r0 + depth, :rows.shape[1]].set(rows)
            srcs.append(pack.at[14, :].set(loss))
            modes.append("gather")
            keys.append("small")
        sems, s_thru, l_thru, token = _xchg_start(srcs, modes, [list(range(len(srcs)))], f"grads_start_{group}{l}")
        pending.append((group, l, keys, modes, sems[0], s_thru, l_thru))
        return token[0, 0]

    small = {k: w[k] for k in ("ffn1_norm", "mix_norm", "hgrn_lb_logits", "hgrn_out_norm", "attn_q_norm", "attn_k_norm", "ffn2_norm")}
    dx = _local_step(x[0], loss_target[0], small, fetch, emit)

    summed = {}
    for group, l, keys, modes, sems, s_thru, l_thru in pending:
        lands = _xchg_wait(s_thru, l_thru, modes, sems, dx, f"grads_wait_{group}{l}")
        for k, land in zip(keys, lands):
            summed[k, l] = _sum_slots(land[None])[0]
    gsum = {k: jnp.stack([summed[k, l] for l in range(depth)]) for k in BIG}
    tot = summed["small", 0]

    grads = {}
    for k, r0 in SMALL_ROWS:
        shp = (depth,) + (w[k].shape[1:] if k != "lbsum" else (d,))
        grads[k] = tot[r0:r0 + depth, :int(np.prod(shp[1:]))].reshape(shp)
    _, lb_vjp = jax.vjp(_lower_bounds, hgrn_lb_logits)
    grads["hgrn_lb_logits"] = lb_vjp(grads.pop("lbsum"))[0]
    grads["ffn1_w_in"], grads["ffn1_w_out"] = tr(gsum["w1t"]), gsum["w1o"]
    grads["w_in"], grads["w_branch_a"] = tr(gsum["wint"]), gsum["wa"]
    grads["w_branch_b"] = tr(gsum["wbt"].reshape(depth, d // N_DEV, -1))
    grads["w_out"] = gsum["wo"]
    grads["ffn2_w_in"], grads["ffn2_w_out"] = tr(gsum["w2t"]), gsum["w2o"]

    upd = {k: _adamw(w[k], grads[k], m[k], v[k]) for k in names}
    return (tot[14, 0], dx[None], *[grads[k] for k in names], *[upd[k][0] for k in names],
            *[upd[k][1] for k in names], *[upd[k][2] for k in names])
```

```python
import functools
import math

import jax
import jax.numpy as jnp
import numpy as np
from jax import lax
from jax.experimental import pallas as pl
from jax.experimental.pallas import tpu as pltpu

F32 = jnp.float32
BF16 = jnp.bfloat16

N_DEV = 8
EPS = 1e-6
HG_DK = 128
HG_CHUNK = 64
HG_SUB = 16
HG_HP = 4
ATT_PATTERNS = ((128, 1), (512, 4), (2048, 16))
ATT_GROUPS = 3
ATT_HEADS = 4
ATT_DH = 128
ATT_BLK = 128
ROPE_THETA = 10000.0
ADAM_LR, ADAM_B1, ADAM_B2, ADAM_EPS, ADAM_WD, ADAM_STEP = 0.001, 0.9, 0.999, 1e-08, 0.01, 10
VMEM_LIMIT_BYTES = 56 * 1024 * 1024
MESH = pl.DeviceIdType.MESH


def _cparams(sem, **kw):
    return pltpu.CompilerParams(dimension_semantics=sem, vmem_limit_bytes=VMEM_LIMIT_BYTES, **kw)


def _sigmoid(x):
    return 1.0 / (1.0 + jnp.exp(-x))


def _mm(a_list, b_list, pairs, n_acc, fin, out_dtypes, *, m, n, k, ta=False, tb=False, bm, bn, bk,
        b_off=None, extras=(), e_off=None, name):
    bm, bn, bk = min(bm, m), min(bn, n), min(bk, k)
    assert m % bm == 0 and n % bn == 0 and k % bk == 0, (name, m, n, k, bm, bn, bk)
    nk = k // bk
    b_off = b_off or [(0, 0)] * len(b_list)
    e_off = e_off or [0] * len(extras)
    na, nb, ne = len(a_list), len(b_list), len(extras)
    dn = (((0,) if ta else (1,), (1,) if tb else (0,)), ((), ()))

    def body(*refs):
        a_refs, b_refs = refs[:na], refs[na:na + nb]
        e_refs = refs[na + nb:na + nb + ne]
        o_refs = refs[na + nb + ne:na + nb + ne + len(out_dtypes)]
        acc_refs = refs[na + nb + ne + len(out_dtypes):]
        kk = pl.program_id(2)
        parts = [None] * n_acc
        for ai, bi, ci in pairs:
            p = lax.dot_general(a_refs[ai][...], b_refs[bi][...], dn, preferred_element_type=F32)
            parts[ci] = p if parts[ci] is None else parts[ci] + p

        def finish(accs):
            outs = fin(accs, [e[...] for e in e_refs])
            for o_ref, o in zip(o_refs, outs):
                o_ref[...] = o.astype(o_ref.dtype)

        if nk == 1:
            finish(parts)
        else:
            @pl.when(kk == 0)
            def _():
                for c in range(n_acc):
                    acc_refs[c][...] = parts[c]

            @pl.when(kk > 0)
            def _():
                for c in range(n_acc):
                    acc_refs[c][...] += parts[c]

            @pl.when(kk == nk - 1)
            def _():
                finish([acc_refs[c][...] for c in range(n_acc)])

    a_spec = pl.BlockSpec((bk, bm), lambda i, j, q: (q, i)) if ta else pl.BlockSpec((bm, bk), lambda i, j, q: (i, q))

    def b_spec(off):
        on, ok = off
        if tb:
            return pl.BlockSpec((bn, bk), lambda i, j, q: (j + on, q + ok))
        return pl.BlockSpec((bk, bn), lambda i, j, q: (q + ok, j + on))

    mn_spec = pl.BlockSpec((bm, bn), lambda i, j, q: (i, j))
    outs = pl.pallas_call(
        body,
        out_shape=[jax.ShapeDtypeStruct((m, n), d) for d in out_dtypes],
        grid=(m // bm, n // bn, nk),
        in_specs=[a_spec] * na + [b_spec(o) for o in b_off]
        + [pl.BlockSpec((bm, bn), lambda i, j, q, o=o: (i, j + o)) for o in e_off],
        out_specs=[mn_spec] * len(out_dtypes),
        scratch_shapes=[pltpu.VMEM((bm, bn), F32) for _ in range(n_acc if nk > 1 else 0)],
        compiler_params=_cparams(("parallel", "parallel", "arbitrary")),
        name=name,
    )(*a_list, *b_list, *extras)
    return outs


def _first(accs, ex):
    return (accs[0],)


def _rowwise(fn, ins, consts, out_defs, sum_widths, *, bm, name):
    ins = [tuple(e) + (1,) * (4 - len(e)) for e in ins]
    out_defs = [tuple(e) + (1,) * (3 - len(e)) for e in out_defs]
    t = ins[0][0].shape[-2] * ins[0][3]
    bm = min(bm, t)
    assert t % bm == 0, (name, t, bm)
    ni, nc, no, ns = len(ins), len(consts), len(out_defs), len(sum_widths)
    strided = [w for _, w, _, d in ins if d > 1] + [w for w, _, d in out_defs if d > 1]

    def body(*refs):
        i_refs, c_refs = refs[:ni], refs[ni:ni + nc]
        o_refs, s_refs = refs[ni + nc:ni + nc + no], refs[ni + nc + no:ni + nc + no + ns]
        scratch = list(refs[ni + nc + no + ns:])
        vals = []
        for ref, (_, w, _, d) in zip(i_refs, ins):
            if d == 1:
                vals.append(ref[...])
                continue
            s = scratch.pop(0)
            for r in range(d):
                for c in range(w // 128):
                    s.at[c][pl.ds(r, bm // d, stride=d), :] = ref[r, :, c * 128:(c + 1) * 128].astype(F32)
            vals.append(jnp.concatenate([s[c] for c in range(w // 128)], axis=1))
        outs, sums = fn(vals, [r[...] for r in c_refs])
        for o_ref, o, (w, _, d) in zip(o_refs, outs, out_defs):
            if d == 1:
                o_ref[...] = o.astype(o_ref.dtype)
                continue
            s = scratch.pop(0)
            for c in range(w // 128):
                s[c] = o[:, c * 128:(c + 1) * 128].astype(F32)
            for r in range(d):
                for c in range(w // 128):
                    o_ref[r, :, c * 128:(c + 1) * 128] = s.at[c][pl.ds(r, bm // d, stride=d), :].astype(o_ref.dtype)
        if ns:
            first = pl.program_id(0) == 0

            @pl.when(first)
            def _():
                for s_ref, s in zip(s_refs, sums):
                    s_ref[...] = s

            @pl.when(jnp.logical_not(first))
            def _():
                for s_ref, s in zip(s_refs, sums):
                    s_ref[...] += s

    def win(width, cb, d):
        if d > 1:
            return pl.BlockSpec((d, bm // d, width), lambda i: (0, i, 0))
        return pl.BlockSpec((bm, width), lambda i: (i, cb))

    res = pl.pallas_call(
        body,
        out_shape=[jax.ShapeDtypeStruct((t, w) if d == 1 else (d, t // d, w), dt) for w, dt, d in out_defs]
        + [jax.ShapeDtypeStruct((8, w), F32) for w in sum_widths],
        grid=(t // bm,),
        in_specs=[win(w, cb, d) for _, w, cb, d in ins] + [pl.BlockSpec(c.shape, lambda i, nd=c.ndim: (0,) * nd) for c in consts],
        out_specs=[win(w, 0, d) for w, _, d in out_defs] + [pl.BlockSpec((8, w), lambda i: (0, 0)) for w in sum_widths],
        scratch_shapes=[pltpu.VMEM((w // 128, bm, 128), F32) for w in strided],
        compiler_params=_cparams(("arbitrary",) if ns else ("parallel",)),
        name=name,
    )(*[e[0] for e in ins], *consts)
    return res[:no], [jnp.sum(s, axis=0) for s in res[no:]]


def _colsum8(x):
    bm, w = x.shape
    return jnp.sum(x.reshape(bm // 8, 8, w), axis=0)


def _tri(n, upper=False):
    r = lax.broadcasted_iota(jnp.int32, (n, n), 0)
    c = lax.broadcasted_iota(jnp.int32, (n, n), 1)
    return (c >= r) if upper else (c <= r)


def _exact_tri_matmul(tri_bf16, x):
    x0 = x.astype(BF16)
    r1 = x - x0.astype(F32)
    x1 = r1.astype(BF16)
    x2 = (r1 - x1.astype(F32)).astype(BF16)
    w = x.shape[1]
    y = jnp.dot(tri_bf16, jnp.concatenate([x0, x1, x2], axis=1), preferred_element_type=F32)
    return y[:, :w] + y[:, w:2 * w] + y[:, 2 * w:]


def _dot_nt(a, b):
    return lax.dot_general(a, b, (((1,), (1,)), ((), ())), preferred_element_type=F32)


def _dot_tn(a, b):
    return lax.dot_general(a, b, (((0,), (0,)), ((), ())), preferred_element_type=F32)


def _dot(a, b):
    return jnp.dot(a, b, preferred_element_type=F32)


def _hg_gates(hq, hf, lb):
    sq = _sigmoid(hq)
    q = hq * sq
    sg = _sigmoid(hf)
    f = lb + (1.0 - lb) * sg
    return q, sq, sg, f


def _hg_intra(q, kk, g):
    c = q.shape[0]
    rows = lax.broadcasted_iota(jnp.int32, (c, 1), 0)
    a_rows, qts, kts, eqs, eks = [], [], [], [], []
    for i in range(c // HG_SUB):
        lo = i * HG_SUB
        ref = g[lo - 1:lo, :] if i else jnp.zeros_like(g[0:1, :])
        eq = jnp.exp(g[lo:lo + HG_SUB, :] - ref)
        ek = jnp.exp(jnp.where(rows < lo + HG_SUB, ref - g, 0.0))
        qt = q[lo:lo + HG_SUB, :] * eq
        kt = kk * ek
        a = _dot_nt(qt.astype(BF16), kt.astype(BF16))
        tpos = lo + lax.broadcasted_iota(jnp.int32, (HG_SUB, c), 0)
        spos = lax.broadcasted_iota(jnp.int32, (HG_SUB, c), 1)
        a_rows.append(jnp.where(spos <= tpos, a, 0.0))
        qts.append(qt), kts.append(kt), eqs.append(eq), eks.append(ek)
    return jnp.concatenate(a_rows, axis=0), qts, kts, eqs, eks


def _hgrn_fwd_serial(zh, lb3, *, tb=512):
    t = zh.shape[0]
    nh = lb3.shape[0]
    c = HG_CHUNK
    tb = min(tb, t)
    nchunk = tb // c
    hp = HG_HP if nh % HG_HP == 0 else 1

    def body(hq_ref, hf_ref, hi_ref, lb_ref, o_ref, st_ref, state):
        @pl.when(pl.program_id(1) == 0)
        def _():
            state[...] = jnp.zeros_like(state)

        tril = _tri(c).astype(BF16)

        def one_head(hh, ci, sl):
            ls = slice(hh * HG_DK, (hh + 1) * HG_DK)
            q, _, _, f = _hg_gates(hq_ref[sl, ls], hf_ref[sl, ls], lb_ref[hh])
            v = hi_ref[sl, ls]
            kk = 1.0 - f
            g = _exact_tri_matmul(tril, jnp.log(f))
            a, _, _, _, _ = _hg_intra(q, kk, g)
            st = state[hh]
            st_ref[hh, ci] = st
            vb = v.astype(BF16)
            o = _dot(a.astype(BF16), vb) + _dot_nt((q * jnp.exp(g)).astype(BF16), st.astype(BF16))
            o_ref[sl, ls] = o
            glast = g[c - 1:c, :]
            kg = kk * jnp.exp(glast - g)
            state[hh] = st * jnp.exp(glast) + _dot_tn(vb, kg.astype(BF16))

        def chunk(ci, carry):
            sl = pl.ds(pl.multiple_of(ci * c, c), c)
            for hh in range(hp):
                one_head(hh, ci, sl)
            return carry

        lax.fori_loop(0, nchunk, chunk, 0)

    def col(cb):
        return pl.BlockSpec((tb, hp * HG_DK), lambda h, i: (i, cb * (nh // hp) + h))

    return pl.pallas_call(
        body,
        out_shape=[jax.ShapeDtypeStruct((t, nh * HG_DK), F32), jax.ShapeDtypeStruct((nh, t // c, HG_DK, HG_DK), F32)],
        grid=(nh // hp, t // tb),
        in_specs=[col(0), col(1), col(2), pl.BlockSpec((hp, 1, HG_DK), lambda h, i: (h, 0, 0))],
        out_specs=[pl.BlockSpec((tb, hp * HG_DK), lambda h, i: (i, h)),
                   pl.BlockSpec((hp, nchunk, HG_DK, HG_DK), lambda h, i: (h, i, 0, 0))],
        scratch_shapes=[pltpu.VMEM((hp, HG_DK, HG_DK), F32)],
        compiler_params=_cparams(("parallel", "arbitrary")),
        name="hgrn_fwd",
    )(zh, zh, zh, lb3)


def _hgrn_bwd_serial(zh, lb3, states, d_o, *, tb=512):
    t = zh.shape[0]
    nh = lb3.shape[0]
    c = HG_CHUNK
    tb = min(tb, t)
    nchunk = tb // c
    nblk = t // tb
    hp = HG_HP if nh % HG_HP == 0 else 1

    def body(hq_ref, hf_ref, hi_ref, lb_ref, st_ref, do_ref, dq_ref, df_ref, dv_ref, dlb_ref, dstate):
        @pl.when(pl.program_id(1) == 0)
        def _():
            dstate[...] = jnp.zeros_like(dstate)
            dlb_ref[...] = jnp.zeros_like(dlb_ref)

        tril = _tri(c).astype(BF16)
        triu = _tri(c, upper=True).astype(BF16)
        last_row = lax.broadcasted_iota(jnp.int32, (c, 1), 0) == c - 1

        def one_head(hh, ci, sl):
            ls = slice(hh * HG_DK, (hh + 1) * HG_DK)
            lb = lb_ref[hh]
            hq, hf = hq_ref[sl, ls], hf_ref[sl, ls]
            q, sq, sg, f = _hg_gates(hq, hf, lb)
            v = hi_ref[sl, ls]
            kk = 1.0 - f
            g = _exact_tri_matmul(tril, jnp.log(f))
            a, qts, kts, eqs, eks = _hg_intra(q, kk, g)
            st = st_ref[hh, ci]
            dst = dstate[hh]
            do = do_ref[sl, ls]
            dob, vb = do.astype(BF16), v.astype(BF16)
            glast = g[c - 1:c, :]
            eg = jnp.exp(g)
            egl = jnp.exp(glast - g)
            qg = q * eg
            kg = kk * egl
            dv = _dot_tn(a.astype(BF16), dob) + _dot_nt(kg.astype(BF16), dst.astype(BF16))
            da = jnp.where(_tri(c), _dot_nt(dob, vb), 0.0).astype(BF16)
            dq_parts, dgq_parts = [], []
            dk = jnp.zeros_like(kk)
            dgk = jnp.zeros_like(kk)
            for i in range(c // HG_SUB):
                da_i = da[i * HG_SUB:(i + 1) * HG_SUB, :]
                ktb, qtb = kts[i].astype(BF16), qts[i].astype(BF16)
                xi = _dot(da_i, ktb)
                yi = _dot_tn(da_i, qtb)
                dq_parts.append(xi * eqs[i])
                dk = dk + yi * eks[i]
                dgq_parts.append(xi * qtb.astype(F32))
                dgk = dgk + yi * ktb.astype(F32)
            dq_inter = _dot(dob, st.astype(BF16)) * eg
            dq = jnp.concatenate(dq_parts, axis=0) + dq_inter
            dk_state = _dot(vb, dst.astype(BF16)) * egl
            dk = dk + dk_state
            dg = jnp.concatenate(dgq_parts, axis=0) - dgk + q * dq_inter - kk * dk_state
            dgl = jnp.sum(kk * dk_state, axis=0, keepdims=True) + jnp.exp(glast) * jnp.sum(st * dst, axis=0, keepdims=True)
            dg = dg + jnp.where(last_row, dgl, 0.0)
            dlogf = _exact_tri_matmul(triu, dg)
            dfv = dlogf / f - dk
            dq_ref[sl, ls] = (dq * (sq * (1.0 + hq * (1.0 - sq)))).astype(dq_ref.dtype)
            df_ref[sl, ls] = (dfv * (1.0 - lb) * sg * (1.0 - sg)).astype(df_ref.dtype)
            dv_ref[sl, ls] = dv.astype(dv_ref.dtype)
            dlb_ref[hh] += jnp.sum(dfv * (1.0 - sg), axis=0, keepdims=True)
            dstate[hh] = dst * jnp.exp(glast) + _dot_tn(dob, qg.astype(BF16))

        def chunk(j, carry):
            ci = nchunk - 1 - j
            sl = pl.ds(pl.multiple_of(ci * c, c), c)
            for hh in range(hp):
                one_head(hh, ci, sl)
            return carry

        lax.fori_loop(0, nchunk, chunk, 0)

    def col(cb):
        return pl.BlockSpec((tb, hp * HG_DK), lambda h, i: (nblk - 1 - i, cb * (nh // hp) + h))

    ocol = pl.BlockSpec((tb, hp * HG_DK), lambda h, i: (nblk - 1 - i, h))
    w = nh * HG_DK
    dq, df, dv, dlb = pl.pallas_call(
        body,
        out_shape=[jax.ShapeDtypeStruct((t, w), BF16)] * 3 + [jax.ShapeDtypeStruct((nh, 1, HG_DK), F32)],
        grid=(nh // hp, nblk),
        in_specs=[col(0), col(1), col(2), pl.BlockSpec((hp, 1, HG_DK), lambda h, i: (h, 0, 0)),
                  pl.BlockSpec((hp, nchunk, HG_DK, HG_DK), lambda h, i: (h, nblk - 1 - i, 0, 0)), ocol],
        out_specs=[ocol, ocol, ocol, pl.BlockSpec((hp, 1, HG_DK), lambda h, i: (h, 0, 0))],
        scratch_shapes=[pltpu.VMEM((hp, HG_DK, HG_DK), F32)],
        compiler_params=_cparams(("parallel", "arbitrary")),
        name="hgrn_bwd",
    )(zh, zh, zh, lb3, states, d_o)
    return dq, df, dv, dlb.reshape(w)


def _hg_heads(x, hp):
    return [x[:, h * HG_DK:(h + 1) * HG_DK] for h in range(hp)]


def _hg_intra_wide(q, kk, g, hp):
    c = q.shape[0]
    rows = lax.broadcasted_iota(jnp.int32, (c, 1), 0)
    a_rows = [[] for _ in range(hp)]
    qts, kts, eqs, eks = [], [], [], []
    for i in range(c // HG_SUB):
        lo = i * HG_SUB
        ref = g[lo - 1:lo, :] if i else jnp.zeros_like(g[0:1, :])
        eq = jnp.exp(g[lo:lo + HG_SUB, :] - ref)
        ek = jnp.exp(jnp.where(rows < lo + HG_SUB, ref - g, 0.0))
        qtb = (q[lo:lo + HG_SUB, :] * eq).astype(BF16)
        ktb = (kk * ek).astype(BF16)
        tpos = lo + lax.broadcasted_iota(jnp.int32, (HG_SUB, c), 0)
        spos = lax.broadcasted_iota(jnp.int32, (HG_SUB, c), 1)
        for h, (qh, kh) in enumerate(zip(_hg_heads(qtb, hp), _hg_heads(ktb, hp))):
            a_rows[h].append(jnp.where(spos <= tpos, _dot_nt(qh, kh), 0.0))
        qts.append(qtb), kts.append(ktb), eqs.append(eq), eks.append(ek)
    return [jnp.concatenate(r, axis=0) for r in a_rows], qts, kts, eqs, eks


def _hgrn_fwd(zh, lb3, *, tb=512):
    t = zh.shape[0]
    nh = lb3.shape[0]
    c = HG_CHUNK
    tb = min(tb, t)
    nchunk = tb // c
    hp = HG_HP if nh % HG_HP == 0 else 1
    wp = hp * HG_DK

    def body(hq_ref, hf_ref, hi_ref, lb_ref, o_ref, st_ref, state):
        @pl.when(pl.program_id(1) == 0)
        def _():
            state[...] = jnp.zeros_like(state)

        tril = _tri(c).astype(BF16)

        def chunk(ci, carry):
            sl = pl.ds(pl.multiple_of(ci * c, c), c)
            q, _, _, f = _hg_gates(hq_ref[sl, :], hf_ref[sl, :], lb_ref[...])
            kk = 1.0 - f
            g = _exact_tri_matmul(tril, jnp.log(f))
            a, _, _, _, _ = _hg_intra_wide(q, kk, g, hp)
            vb = hi_ref[sl, :].astype(BF16)
            glast = g[c - 1:c, :]
            qgb = (q * jnp.exp(g)).astype(BF16)
            kgb = (kk * jnp.exp(glast - g)).astype(BF16)
            dec = jnp.exp(glast)
            sts = [state[h] for h in range(hp)]
            for h in range(hp):
                st_ref[h, ci] = sts[h]
            vh, qgh, kgh, dech = _hg_heads(vb, hp), _hg_heads(qgb, hp), _hg_heads(kgb, hp), _hg_heads(dec, hp)
            o = [_dot(a[h].astype(BF16), vh[h]) + _dot_nt(qgh[h], sts[h].astype(BF16)) for h in range(hp)]
            new = [_dot_tn(vh[h], kgh[h]) for h in range(hp)]
            o_ref[sl, :] = jnp.concatenate(o, axis=1)
            for h in range(hp):
                state[h] = sts[h] * dech[h] + new[h]
            return carry

        lax.fori_loop(0, nchunk, chunk, 0)

    def col(cb):
        return pl.BlockSpec((tb, wp), lambda h, i: (i, cb * (nh // hp) + h))

    return pl.pallas_call(
        body,
        out_shape=[jax.ShapeDtypeStruct((t, nh * HG_DK), F32), jax.ShapeDtypeStruct((nh, t // c, HG_DK, HG_DK), F32)],
        grid=(nh // hp, t // tb),
        in_specs=[col(0), col(1), col(2), pl.BlockSpec((1, wp), lambda h, i: (0, h))],
        out_specs=[pl.BlockSpec((tb, wp), lambda h, i: (i, h)),
                   pl.BlockSpec((hp, nchunk, HG_DK, HG_DK), lambda h, i: (h, i, 0, 0))],
        scratch_shapes=[pltpu.VMEM((hp, HG_DK, HG_DK), F32)],
        compiler_params=_cparams(("parallel", "arbitrary")),
        name="hgrn_fwd",
    )(zh, zh, zh, lb3.reshape(1, -1))


def _hgrn_bwd(zh, lb3, states, d_o, *, tb=512):
    t = zh.shape[0]
    nh = lb3.shape[0]
    c = HG_CHUNK
    tb = min(tb, t)
    nchunk = tb // c
    nblk = t // tb
    hp = HG_HP if nh % HG_HP == 0 else 1
    wp = hp * HG_DK

    def body(hq_ref, hf_ref, hi_ref, lb_ref, st_ref, do_ref, dq_ref, df_ref, dv_ref, dlb_ref, dstate):
        @pl.when(pl.program_id(1) == 0)
        def _():
            dstate[...] = jnp.zeros_like(dstate)
            dlb_ref[...] = jnp.zeros_like(dlb_ref)

        tril = _tri(c).astype(BF16)
        triu = _tri(c, upper=True).astype(BF16)
        last_row = lax.broadcasted_iota(jnp.int32, (c, 1), 0) == c - 1
        heads = range(hp)

        def chunk(j, carry):
            ci = nchunk - 1 - j
            sl = pl.ds(pl.multiple_of(ci * c, c), c)
            lb = lb_ref[...]
            hq, hf = hq_ref[sl, :], hf_ref[sl, :]
            q, sq, sg, f = _hg_gates(hq, hf, lb)
            kk = 1.0 - f
            g = _exact_tri_matmul(tril, jnp.log(f))
            a, qts, kts, eqs, eks = _hg_intra_wide(q, kk, g, hp)
            glast = g[c - 1:c, :]
            eg, egl, dec = jnp.exp(g), jnp.exp(glast - g), jnp.exp(glast)
            vb, dob = hi_ref[sl, :].astype(BF16), do_ref[sl, :].astype(BF16)
            qgb, kgb = (q * eg).astype(BF16), (kk * egl).astype(BF16)
            sts = [st_ref[h, ci] for h in heads]
            dsts = [dstate[h] for h in heads]
            stb, dstb = [s.astype(BF16) for s in sts], [s.astype(BF16) for s in dsts]
            vh, doh, qgh, kgh = _hg_heads(vb, hp), _hg_heads(dob, hp), _hg_heads(qgb, hp), _hg_heads(kgb, hp)
            dv = [_dot_tn(a[h].astype(BF16), doh[h]) + _dot_nt(kgh[h], dstb[h]) for h in heads]
            da = [jnp.where(_tri(c), _dot_nt(doh[h], vh[h]), 0.0).astype(BF16) for h in heads]
            dq_inter = jnp.concatenate([_dot(doh[h], stb[h]) for h in heads], axis=1) * eg
            dk_state = jnp.concatenate([_dot(vh[h], dstb[h]) for h in heads], axis=1) * egl
            new_dst = [_dot_tn(doh[h], qgh[h]) for h in heads]
            xs, dk, dgk = [], dk_state, 0.0
            for i in range(c // HG_SUB):
                rs = slice(i * HG_SUB, (i + 1) * HG_SUB)
                kth, qth = _hg_heads(kts[i], hp), _hg_heads(qts[i], hp)
                xi = jnp.concatenate([_dot(da[h][rs, :], kth[h]) for h in heads], axis=1)
                yi = jnp.concatenate([_dot_tn(da[h][rs, :], qth[h]) for h in heads], axis=1)
                xs.append(xi)
                dk = dk + yi * eks[i]
                dgk = dgk + yi * kts[i].astype(F32)
            dq = jnp.concatenate([x * e for x, e in zip(xs, eqs)], axis=0) + dq_inter
            dgq = jnp.concatenate([x * qt.astype(F32) for x, qt in zip(xs, qts)], axis=0)
            dg = dgq - dgk + q * dq_inter - kk * dk_state
            sdot = jnp.concatenate([jnp.sum(sts[h] * dsts[h], axis=0, keepdims=True) for h in heads], axis=1)
            dgl = jnp.sum(kk * dk_state, axis=0, keepdims=True) + dec * sdot
            dg = dg + jnp.where(last_row, dgl, 0.0)
            dlogf = _exact_tri_matmul(triu, dg)
            dfv = dlogf / f - dk
            dq_ref[sl, :] = (dq * (sq * (1.0 + hq * (1.0 - sq)))).astype(dq_ref.dtype)
            df_ref[sl, :] = (dfv * (1.0 - lb) * sg * (1.0 - sg)).astype(df_ref.dtype)
            dv_ref[sl, :] = jnp.concatenate(dv, axis=1).astype(dv_ref.dtype)
            dlb_ref[...] += jnp.sum(dfv * (1.0 - sg), axis=0, keepdims=True)
            dech = _hg_heads(dec, hp)
            for h in heads:
                dstate[h] = dsts[h] * dech[h] + new_dst[h]
            return carry

        lax.fori_loop(0, nchunk, chunk, 0)

    def col(cb):
        return pl.BlockSpec((tb, wp), lambda h, i: (nblk - 1 - i, cb * (nh // hp) + h))

    ocol = pl.BlockSpec((tb, wp), lambda h, i: (nblk - 1 - i, h))
    lbspec = pl.BlockSpec((1, wp), lambda h, i: (0, h))
    w = nh * HG_DK
    dq, df, dv, dlb = pl.pallas_call(
        body,
        out_shape=[jax.ShapeDtypeStruct((t, w), BF16)] * 3 + [jax.ShapeDtypeStruct((1, w), F32)],
        grid=(nh // hp, nblk),
        in_specs=[col(0), col(1), col(2), lbspec,
                  pl.BlockSpec((hp, nchunk, HG_DK, HG_DK), lambda h, i: (h, nblk - 1 - i, 0, 0)), ocol],
        out_specs=[ocol, ocol, ocol, lbspec],
        scratch_shapes=[pltpu.VMEM((hp, HG_DK, HG_DK), F32)],
        compiler_params=_cparams(("parallel", "arbitrary")),
        name="hgrn_bwd",
    )(zh, zh, zh, lb3.reshape(1, -1), states, d_o)
    return dq, df, dv, dlb.reshape(w)


NEG = -1e30
ATT_GW = ATT_HEADS * ATT_DH


def _att_scores(q, kp, kc, has_prev):
    scale = ATT_DH ** -0.5
    i = lax.broadcasted_iota(jnp.int32, (ATT_BLK, ATT_BLK), 0)
    j = lax.broadcasted_iota(jnp.int32, (ATT_BLK, ATT_BLK), 1)
    s_p = jnp.where(jnp.logical_and(j >= i, has_prev), _dot_nt(q, kp) * scale, NEG)
    s_c = jnp.where(j <= i, _dot_nt(q, kc) * scale, NEG)
    return s_p, s_c


def _att_views(arrs, d):
    return [a.reshape(d, -1, ATT_GW) for a in arrs]


def _att_unview(a, d):
    return a.reshape(-1, ATT_GW) if d == 1 else a


def _attn_fwd(qb, kb, vb, g):
    d = ATT_PATTERNS[g][1]
    q2, k2, v2 = _att_views([qb, kb, vb], d)
    nb = q2.shape[1] // ATT_BLK

    def body(q_ref, kc_ref, kp_ref, vc_ref, vp_ref, o_ref, l_ref):
        has_prev = pl.program_id(1) > 0
        for h in range(ATT_HEADS):
            hs = slice(h * ATT_DH, (h + 1) * ATT_DH)
            s_p, s_c = _att_scores(q_ref[:, hs], kp_ref[:, hs], kc_ref[:, hs], has_prev)
            m = jnp.maximum(jnp.max(s_p, axis=1, keepdims=True), jnp.max(s_c, axis=1, keepdims=True))
            p_p, p_c = jnp.exp(s_p - m), jnp.exp(s_c - m)
            l = jnp.sum(p_p, axis=1, keepdims=True) + jnp.sum(p_c, axis=1, keepdims=True)
            o = _dot(p_p.astype(BF16), vp_ref[:, hs]) + _dot(p_c.astype(BF16), vc_ref[:, hs])
            o_ref[:, hs] = o / l
            l_ref[:, hs] = jnp.broadcast_to(m + jnp.log(l), (ATT_BLK, ATT_DH))

    cur = pl.BlockSpec((None, ATT_BLK, ATT_GW), lambda r, n: (r, n, 0))
    prev = pl.BlockSpec((None, ATT_BLK, ATT_GW), lambda r, n: (r, jnp.maximum(n - 1, 0), 0))
    o, lse = pl.pallas_call(
        body,
        out_shape=[jax.ShapeDtypeStruct(q2.shape, F32)] * 2,
        grid=(d, nb),
        in_specs=[cur, cur, prev, cur, prev],
        out_specs=[cur, cur],
        compiler_params=_cparams(("parallel", "arbitrary")),
        name=f"attn_fwd_g{g}",
    )(q2, k2, k2, v2, v2)
    return _att_unview(o, d), _att_unview(lse, d)


def _attn_bwd(qb, kb, vb, o, lse, d_o, d_lse, g):
    d = ATT_PATTERNS[g][1]
    q2, k2, v2 = _att_views([qb, kb, vb], d)
    o2, l2, do2, dl2 = _att_views([o, lse, d_o, d_lse], d)
    nb = q2.shape[1] // ATT_BLK

    def body(q_ref, kc_ref, kp_ref, vc_ref, vp_ref, o_ref, l_ref, do_ref, dl_ref, dq_ref, dk_ref, dv_ref, ck, cv):
        n = pl.program_id(1)
        active = n < nb

        @pl.when(n == 0)
        def _():
            ck[...] = jnp.zeros_like(ck)
            cv[...] = jnp.zeros_like(cv)

        @pl.when(jnp.logical_not(active))
        def _():
            dk_ref[...] = ck[...]
            dv_ref[...] = cv[...]

        @pl.when(active)
        def _():
            has_prev = n > 0
            for h in range(ATT_HEADS):
                hs = slice(h * ATT_DH, (h + 1) * ATT_DH)
                q, kp, kc, vp, vc = q_ref[:, hs], kp_ref[:, hs], kc_ref[:, hs], vp_ref[:, hs], vc_ref[:, hs]
                s_p, s_c = _att_scores(q, kp, kc, has_prev)
                lse_h = l_ref[:, hs][:, 0:1]
                p_p, p_c = jnp.exp(s_p - lse_h), jnp.exp(s_c - lse_h)
                do = do_ref[:, hs]
                delta = jnp.sum(do * o_ref[:, hs] - dl_ref[:, hs], axis=1, keepdims=True)
                dob = do.astype(BF16)
                scale = ATT_DH ** -0.5
                ds_p = (p_p * (_dot_nt(dob, vp) - delta) * scale).astype(BF16)
                ds_c = (p_c * (_dot_nt(dob, vc) - delta) * scale).astype(BF16)
                dq_ref[:, hs] = _dot(ds_p, kp) + _dot(ds_c, kc)
                dk_ref[:, hs] = ck[:, hs] + _dot_tn(ds_p, q)
                dv_ref[:, hs] = cv[:, hs] + _dot_tn(p_p.astype(BF16), dob)
                ck[:, hs] = _dot_tn(ds_c, q)
                cv[:, hs] = _dot_tn(p_c.astype(BF16), dob)

    def qn(n):
        return jnp.minimum(n, nb - 1)

    cur = pl.BlockSpec((None, ATT_BLK, ATT_GW), lambda r, n: (r, qn(n), 0))
    prev = pl.BlockSpec((None, ATT_BLK, ATT_GW), lambda r, n: (r, jnp.maximum(qn(n) - 1, 0), 0))
    behind = pl.BlockSpec((None, ATT_BLK, ATT_GW), lambda r, n: (r, jnp.maximum(n - 1, 0), 0))
    shp = jax.ShapeDtypeStruct(q2.shape, F32)
    dq, dk, dv = pl.pallas_call(
        body,
        out_shape=[shp, shp, shp],
        grid=(d, nb + 1),
        in_specs=[cur, cur, prev, cur, prev, cur, cur, cur, cur],
        out_specs=[cur, behind, behind],
        scratch_shapes=[pltpu.VMEM((ATT_BLK, ATT_GW), F32), pltpu.VMEM((ATT_BLK, ATT_GW), F32)],
        compiler_params=_cparams(("parallel", "arbitrary")),
        name=f"attn_bwd_g{g}",
    )(q2, k2, k2, v2, v2, o2, l2, do2, dl2)
    return _att_unview(dq, d), _att_unview(dk, d), _att_unview(dv, d)


def _rms_parts(x, width):
    outs = []
    for lo in range(0, x.shape[1], width):
        xs = x[:, lo:lo + width]
        r = lax.rsqrt(jnp.mean(xs * xs, axis=1, keepdims=True) + EPS)
        outs.append((xs * r, r))
    return outs


def _rms_bwd_part(xh, r, dxh):
    return r * (dxh - xh * jnp.mean(dxh * xh, axis=1, keepdims=True))


def _norm_fwd(x, gain):
    d = x.shape[1]

    def fn(ins, consts):
        (xh, _), = _rms_parts(ins[0], d)
        return [xh * consts[0]], []

    (h,), _ = _rowwise(fn, [(x, d, 0)], [gain.reshape(1, d)], [(d, BF16)], [], bm=512, name="norm_fwd")
    return h


def _norm_bwd(x, gain, dh, dres):
    d = x.shape[1]

    def fn(ins, consts):
        (xh, r), = _rms_parts(ins[0], d)
        dx = ins[2] + _rms_bwd_part(xh, r, ins[1] * consts[0])
        return [dx], [_colsum8(ins[1] * xh)]

    (dx,), (dg,) = _rowwise(fn, [(x, d, 0), (dh, d, 0), (dres, d, 0)], [gain.reshape(1, d)], [(d, F32)], [d],
                            bm=512, name="norm_bwd")
    return dx, dg


def _rot_sign():
    lane = lax.broadcasted_iota(jnp.int32, (1, ATT_DH), 1)
    return jnp.where(lane < ATT_DH // 2, -1.0, 1.0).astype(F32)


def _rope(y, cos, sin):
    return y * cos + pltpu.roll(y, ATT_DH // 2, axis=1) * _rot_sign() * sin


def _rope_t(dy, cos, sin):
    return dy * cos - pltpu.roll(dy * sin, ATT_DH // 2, axis=1) * _rot_sign()


def _qk_prep(zq, zk, zv, qn, kn, cos, sin):
    w = zq.shape[1]

    def fn(ins, consts):
        cs, sn = ins[3], ins[4]
        outs = []
        for z, gain in ((ins[0], consts[0]), (ins[1], consts[1])):
            for i, (xh, _) in enumerate(_rms_parts(z, ATT_DH)):
                outs.append(_rope(xh * gain[:, i * ATT_DH:(i + 1) * ATT_DH], cs, sn))
        outs += [ins[2][:, i * ATT_DH:(i + 1) * ATT_DH] for i in range(w // ATT_DH)]
        groups = [jnp.concatenate(outs[i:i + ATT_HEADS], axis=1) for i in range(0, len(outs), ATT_HEADS)]
        return groups, []

    outs, _ = _rowwise(fn, [(zq, w, 0), (zk, w, 0), (zv, w, 0), (cos, ATT_DH, 0), (sin, ATT_DH, 0)], [qn, kn],
                       [(ATT_GW, BF16, ATT_PATTERNS[g][1]) for g in range(ATT_GROUPS)] * 3, [], bm=256, name="qk_prep")
    return outs[0:3], outs[3:6], outs[6:9]


def _qk_prep_bwd(zq, zk, dq_g, dk_g, dv_g, qn, kn, cos, sin):
    w = zq.shape[1]

    def fn(ins, consts):
        cs, sn = ins[2], ins[3]
        outs, sums = [], []
        for z, gain, dparts in ((ins[0], consts[0], ins[4:7]), (ins[1], consts[1], ins[7:10])):
            dout = jnp.concatenate(dparts, axis=1)
            dz, dgain = [], []
            for i, (xh, r) in enumerate(_rms_parts(z, ATT_DH)):
                hs = slice(i * ATT_DH, (i + 1) * ATT_DH)
                dy = _rope_t(dout[:, hs], cs, sn)
                dgain.append(_colsum8(dy * xh))
                dz.append(_rms_bwd_part(xh, r, dy * gain[:, hs]))
            outs.append(jnp.concatenate(dz, axis=1))
            sums.append(jnp.concatenate(dgain, axis=1))
        outs.append(jnp.concatenate(ins[10:13], axis=1))
        return outs, sums

    ins = [(zq, w, 0), (zk, w, 0), (cos, ATT_DH, 0), (sin, ATT_DH, 0)]
    for parts in (dq_g, dk_g, dv_g):
        ins += [(a, ATT_GW, 0, ATT_PATTERNS[g][1]) for g, a in enumerate(parts)]
    (dzq, dzk, dzv), (dqn, dkn) = _rowwise(fn, ins, [qn, kn], [(w, BF16)] * 3, [w, w], bm=256, name="qk_prep_bwd")
    return dzq, dzk, dzv, dqn, dkn


def _post_a(o_raw, zh, gout):
    w = o_raw.shape[1]

    def fn(ins, consts):
        oh = jnp.concatenate([xh for xh, _ in _rms_parts(ins[0], HG_DK)], axis=1)
        hg = ins[1]
        return [oh * consts[0] * (hg * _sigmoid(hg))], []

    (y,), _ = _rowwise(fn, [(o_raw, w, 0), (zh, w, 3)], [gout.reshape(1, w)], [(w, BF16)], [], bm=512, name="post_a")
    return y


def _post_a_bwd(o_raw, zh, gout, dy):
    w = o_raw.shape[1]

    def fn(ins, consts):
        parts = _rms_parts(ins[0], HG_DK)
        oh = jnp.concatenate([xh for xh, _ in parts], axis=1)
        hg, dyv, gain = ins[1], ins[2], consts[0]
        sg = _sigmoid(hg)
        s = hg * sg
        doh = dyv * gain * s
        do = jnp.concatenate([_rms_bwd_part(xh, r, doh[:, i * HG_DK:(i + 1) * HG_DK]) for i, (xh, r) in enumerate(parts)], axis=1)
        dhg = dyv * oh * gain * (sg * (1.0 + hg * (1.0 - sg)))
        return [do, dhg], [_colsum8(dyv * oh * s)]

    (do, dhg), (dgain,) = _rowwise(fn, [(o_raw, w, 0), (zh, w, 3), (dy, w, 0)], [gout.reshape(1, w)],
                                   [(w, F32), (w, BF16)], [w], bm=512, name="post_a_bwd")
    return do, dhg, dgain


def _merge_alpha(lses):
    m = jnp.maximum(jnp.maximum(lses[0], lses[1]), lses[2])
    e = [jnp.exp(l - m) for l in lses]
    inv = 1.0 / (e[0] + e[1] + e[2])
    return [x * inv for x in e]


def _group_ins(parts):
    return [(a, ATT_GW, 0, ATT_PATTERNS[g][1]) for g, a in enumerate(parts)]


def _merge_b(o_g, lse_g):
    def fn(ins, consts):
        al = _merge_alpha(ins[3:6])
        return [al[0] * ins[0] + al[1] * ins[1] + al[2] * ins[2]], []

    (y,), _ = _rowwise(fn, _group_ins(o_g) + _group_ins(lse_g), [], [(ATT_GW, BF16)], [], bm=512, name="merge_b")
    return y


def _merge_b_bwd(o_g, lse_g, dy):
    def fn(ins, consts):
        al = _merge_alpha(ins[3:6])
        dyv = ins[6]
        dal = [dyv * ins[i] for i in range(3)]
        tot = al[0] * dal[0] + al[1] * dal[1] + al[2] * dal[2]
        return [al[i] * dyv for i in range(3)] + [al[i] * (dal[i] - tot) for i in range(3)], []

    outs, _ = _rowwise(fn, _group_ins(o_g) + _group_ins(lse_g) + [(dy, ATT_GW, 0)], [],
                       [(ATT_GW, F32, ATT_PATTERNS[g][1]) for g in range(ATT_GROUPS)] * 2, [], bm=512, name="merge_b_bwd")
    return outs[:3], outs[3:]


def _loss_head(y, target):
    d = y.shape[1]

    def fn(ins, consts):
        e = ins[0] - ins[1]
        return [e * (1.0 / d)], [_colsum8(e * e)]

    (dy,), (sq,) = _rowwise(fn, [(y, d, 0), (target, d, 0)], [], [(d, F32)], [d], bm=512, name="loss_head")
    return 0.5 * jnp.sum(sq) / d, dy


def _silu_grad(a):
    s = _sigmoid(a)
    return s * (1.0 + a * (1.0 - s))


def _ffn_fwd(x, gain, wt, wo_fn, tag):
    t, d = x.shape
    f = wt.shape[0] // 2
    h = _norm_fwd(x, gain)

    def act(accs, ex):
        a, b = accs
        return (a * _sigmoid(a) * b, a, b)

    u, a, b = _mm([h], [wt, wt], [(0, 0, 0), (0, 1, 1)], 2, act, [BF16, BF16, BF16], m=t, n=f, k=d, tb=True,
                  bm=1024, bn=256, bk=d, b_off=[(0, 0), (f // min(256, f), 0)], name=f"ffn_in_{tag}")
    wo = wo_fn(u)
    (y,) = _mm([u], [wo], [(0, 0, 0)], 1, lambda accs, ex: (ex[0] + 0.5 * accs[0],), [F32], m=t, n=d, k=f,
               bm=512, bn=d, bk=f, extras=[x], name=f"ffn_out_{tag}")
    return y, (x, h, u, a, b, wo)


def _ffn_bwd(dy, saved, gain, wt, tag, tok):
    x, h, u, a, b, wo = saved
    t, d = x.shape
    f = wo.shape[0]
    dyb = (dy + tok).astype(BF16)

    def dact(accs, ex):
        du = 0.5 * accs[0]
        av, bv = ex[0].astype(F32), ex[1].astype(F32)
        return (du * bv * _silu_grad(av), du * av * _sigmoid(av))

    da, db = _mm([dyb], [wo], [(0, 0, 0)], 1, dact, [BF16, BF16], m=t, n=f, k=d, tb=True, bm=1024, bn=256, bk=d,
                 extras=[a, b], name=f"ffn_dact_{tag}")
    (dwo,) = _mm([u], [dyb], [(0, 0, 0)], 1, lambda accs, ex: (0.5 * accs[0],), [BF16], m=f, n=d, k=t, ta=True,
                 bm=1408, bn=d, bk=1024, name=f"ffn_dwo_{tag}")
    (dh,) = _mm([da, db], [wt, wt], [(0, 0, 0), (1, 1, 0)], 1, _first, [F32], m=t, n=d, k=f, bm=512, bn=d, bk=f,
                b_off=[(0, 0), (0, 1)], name=f"ffn_dh_{tag}")
    dwt = [_mm([g], [h], [(0, 0, 0)], 1, _first, [BF16], m=f, n=d, k=t, ta=True, bm=1408, bn=d, bk=1024,
               name=f"ffn_dwt{i}_{tag}")[0] for i, g in enumerate((da, db))]
    dx, dgain = _norm_bwd(x, gain, dh, dy)
    return dx, dgain, jnp.concatenate(dwt, axis=0), dwo


Z_SPLITS = (("h", 4096), ("q", 1536), ("k", 1536), ("v", 1536), ("g", 2048))


def _mix_fwd(x, p, cos, sin):
    t, d = x.shape
    hm = _norm_fwd(x, p["gm"])
    z, off = {}, 0
    for nm, width in Z_SPLITS:
        (z[nm],) = _mm([hm], [p["wint"]], [(0, 0, 0)], 1, _first, [F32], m=t, n=width, k=d, tb=True, bm=1024, bn=512, bk=d,
                       b_off=[(off // 512, 0)], name=f"mix_in_{nm}")
        off += width
    o_raw, states = _hgrn_fwd(z["h"], p["lb3"])
    qb, kb, vb = _qk_prep(z["q"], z["k"], z["v"], p["qn"], p["kn"], cos, sin)
    o_g, lse_g = zip(*[_attn_fwd(qb[g], kb[g], vb[g], g) for g in range(ATT_GROUPS)])
    oa = _post_a(o_raw, z["h"], p["gout"])
    ob = _merge_b(o_g, lse_g)
    late = p["late"](ob)
    p = dict(p, **late)
    (ya,) = _mm([oa], [p["wa"]], [(0, 0, 0)], 1, _first, [F32], m=t, n=d, k=oa.shape[1], bm=1024, bn=d, bk=oa.shape[1],
                name="branch_a")

    def gate(accs, ex):
        return (_sigmoid(ex[0]) * ex[2] + _sigmoid(ex[1]) * accs[0], accs[0])

    merged, yb = _mm([ob], [p["wbt"]], [(0, 0, 0)], 1, gate, [BF16, F32], m=t, n=d, k=ATT_GW, tb=True, bm=512, bn=d,
                     bk=ATT_GW, extras=[z["g"], z["g"], ya], e_off=[0, 1, 0], name="branch_b_gate")
    (y,) = _mm([merged], [p["wo"]], [(0, 0, 0)], 1, lambda accs, ex: (ex[0] + accs[0],), [F32], m=t, n=d, k=d,
               bm=1024, bn=d, bk=d, extras=[x], name="mix_out")
    return y, (x, hm, z, o_raw, states, qb, kb, vb, o_g, lse_g, oa, ob, ya, yb, merged, late)


def _mix_bwd(dy, saved, p, cos, sin, tok):
    x, hm, z, o_raw, states, qb, kb, vb, o_g, lse_g, oa, ob, ya, yb, merged, late = saved
    p = dict(p, **late)
    t, d = x.shape
    w = oa.shape[1]
    dyb = (dy + tok).astype(BF16)

    def dgate(accs, ex):
        dm = accs[0]
        sa, sb = _sigmoid(ex[0]), _sigmoid(ex[1])
        return (sa * dm, sb * dm, dm * ex[2] * sa * (1.0 - sa), dm * ex[3] * sb * (1.0 - sb))

    dya, dyb_, dga, dgb = _mm([dyb], [p["wo"]], [(0, 0, 0)], 1, dgate, [BF16] * 4, m=t, n=d, k=d, tb=True, bm=512, bn=d,
                              bk=d, extras=[z["g"], z["g"], ya, yb], e_off=[0, 1, 0, 0], name="mix_out_bwd")
    (dwo,) = _mm([merged], [dyb], [(0, 0, 0)], 1, _first, [BF16], m=d, n=d, k=t, ta=True, bm=d, bn=d, bk=1024, name="mix_dwo")
    (doa,) = _mm([dya], [p["wa"]], [(0, 0, 0)], 1, _first, [F32], m=t, n=w, k=d, tb=True, bm=1024, bn=w, bk=d, name="branch_a_bwd")
    (dwa,) = _mm([oa], [dya], [(0, 0, 0)], 1, _first, [BF16], m=w, n=d, k=t, ta=True, bm=w, bn=d, bk=1024, name="branch_a_dw")
    (dob,) = _mm([dyb_], [p["wbt"]], [(0, 0, 0)], 1, _first, [F32], m=t, n=ATT_GW, k=d, bm=1024, bn=ATT_GW, bk=d,
                 name="branch_b_bwd")
    (dwbt,) = _mm([dyb_], [ob], [(0, 0, 0)], 1, _first, [BF16], m=d, n=ATT_GW, k=t, ta=True, bm=d, bn=ATT_GW, bk=1024,
                  name="branch_b_dw")
    do_raw, dhg, dgout = _post_a_bwd(o_raw, z["h"], p["gout"], doa)
    do_g, dlse_g = _merge_b_bwd(o_g, lse_g, dob)
    dq_g, dk_g, dv_g = zip(*[_attn_bwd(qb[g], kb[g], vb[g], o_g[g], lse_g[g], do_g[g], dlse_g[g], g)
                             for g in range(ATT_GROUPS)])
    dzq, dzk, dzv, dqn, dkn = _qk_prep_bwd(z["q"], z["k"], dq_g, dk_g, dv_g, p["qn"], p["kn"], cos, sin)
    dhq, dhf, dhi, lbsum = _hgrn_bwd(z["h"], p["lb3"], states, do_raw)
    dz = jnp.concatenate([dhq, dhf, dhi, dhg, dzq, dzk, dzv, dga, dgb], axis=1)
    pw = dz.shape[1]
    (dhm,) = _mm([dz], [p["wint"]], [(0, 0, 0)], 1, _first, [F32], m=t, n=d, k=pw, bm=1024, bn=d, bk=1536, name="mix_in_bwd")
    (dwint,) = _mm([dz], [hm], [(0, 0, 0)], 1, _first, [BF16], m=pw, n=d, k=t, ta=True, bm=1536, bn=d, bk=1024, name="mix_in_dw")
    dx, dgm = _norm_bwd(x, p["gm"], dhm, dy)
    return dx, dict(gm=dgm, wint=dwint, lbsum=lbsum, gout=dgout, qn=dqn, kn=dkn, wa=dwa, wbt=dwbt, wo=dwo)


def _rope_tables(t):
    pos = jnp.arange(t, dtype=F32)
    inv = ROPE_THETA ** (-jnp.arange(0, ATT_DH, 2, dtype=F32) / ATT_DH)
    ang = pos[:, None] * inv[None, :]
    ang = jnp.concatenate([ang, ang], axis=-1)
    return jnp.cos(ang), jnp.sin(ang)


def _lower_bounds(logits):
    lb = jnp.cumsum(jax.nn.softmax(logits, axis=0), axis=0)
    return lb - lb[0:1]


def _head_gain(g):
    return jnp.tile(g[:, None, :], (1, ATT_HEADS, 1)).reshape(1, ATT_GROUPS * ATT_GW)


SMALL_GRADS = ("ffn1_norm", "mix_norm", "lbsum", "hgrn_out_norm", "attn_q_norm", "attn_k_norm", "ffn2_norm")


def _local_step(x, target, small, fetch, emit):
    t = x.shape[0]
    depth = small["ffn1_norm"].shape[0]
    cos, sin = _rope_tables(t)
    lb_all = _lower_bounds(small["hgrn_lb_logits"])
    saved = []
    for l in range(depth):
        w1t = fetch("w1t", l, x)["w1t"]
        x, s1 = _ffn_fwd(x, small["ffn1_norm"][l], w1t, lambda after, l=l: fetch("w1o", l, after)["w1o"], "1")
        p = dict(gm=small["mix_norm"][l], wint=fetch("wint", l, x)["wint"], lb3=lb_all[l].reshape(-1, 1, HG_DK),
                 gout=small["hgrn_out_norm"][l], qn=_head_gain(small["attn_q_norm"][l]),
                 kn=_head_gain(small["attn_k_norm"][l]), late=functools.partial(fetch, "mout", l))
        x, sm = _mix_fwd(x, p, cos, sin)
        w2t = fetch("w2t", l, x)["w2t"]
        x, s2 = _ffn_fwd(x, small["ffn2_norm"][l], w2t, lambda after, l=l: fetch("w2o", l, after)["w2o"], "2")
        saved.append((p, w1t, w2t, s1, sm, s2))
    loss, dx = _loss_head(x, target)
    gsmall = {k: [None] * depth for k in SMALL_GRADS}
    tok = jnp.zeros((), F32)
    for l in reversed(range(depth)):
        p, w1t, w2t, s1, sm, s2 = saved[l]
        dx, gsmall["ffn2_norm"][l], dw2t, dw2o = _ffn_bwd(dx, s2, small["ffn2_norm"][l], w2t, "2", tok)
        tok = emit("ffn2", l, dict(w2t=dw2t, w2o=dw2o), None)
        dx, gm = _mix_bwd(dx, sm, p, cos, sin, tok)
        tok = emit("mix", l, {k: gm[k] for k in ("wint", "wa", "wbt", "wo")}, None)
        gsmall["mix_norm"][l], gsmall["lbsum"][l], gsmall["hgrn_out_norm"][l] = gm["gm"], gm["lbsum"], gm["gout"]
        for k, src in (("attn_q_norm", "qn"), ("attn_k_norm", "kn")):
            gsmall[k][l] = jnp.sum(gm[src].reshape(ATT_GROUPS, ATT_HEADS, ATT_DH), axis=1)
        dx, gsmall["ffn1_norm"][l], dw1t, dw1o = _ffn_bwd(dx, s1, small["ffn1_norm"][l], w1t, "1", tok)
        final = ({k: jnp.stack(v) for k, v in gsmall.items()}, loss) if l == 0 else None
        tok = emit("ffn1", l, dict(w1t=dw1t, w1o=dw1o), final)
    return dx


_HBM = pl.BlockSpec(memory_space=pltpu.HBM)
_SEM = pl.BlockSpec(memory_space=pltpu.SEMAPHORE)
_EFFECT = pltpu.SideEffectType.DATAFLOW_SIDE_EFFECTING


def _peer(p):
    x, y, c = lax.axis_index("x"), lax.axis_index("y"), lax.axis_index("c")
    me = 4 * x + 2 * y + c
    return (1 - x if p & 4 else x, 1 - y if p & 2 else y, 1 - c if p & 1 else c), jnp.bitwise_xor(me, p), me


def _xchg_copy(src, land, mode, send_sems, recv_sems, k, p, arriving):
    peer, peer_id, me = _peer(p)
    block = src if mode == "gather" else src.at[peer_id]
    return pltpu.make_async_remote_copy(
        src_ref=block, dst_ref=land.at[peer_id if arriving else me], send_sem=send_sems.at[k * (N_DEV - 1) + p - 1],
        recv_sem=recv_sems.at[k * (N_DEV - 1) + p - 1], device_id=peer, device_id_type=MESH)


def _xchg_start(srcs, modes, groups, name):
    n, ng = len(srcs), len(groups)
    lands = [lax.empty((N_DEV,) + a.shape[-2:], a.dtype) for a in srcs]

    def body(*refs):
        src, land = refs[:n], refs[n:2 * n]
        sems = refs[2 * n:2 * n + 2 * ng]
        token = refs[2 * n + 2 * ng + 2 * n]
        for gi, idx in enumerate(groups):
            for ki, k in enumerate(idx):
                for p in range(1, N_DEV):
                    _xchg_copy(src[k], land[k], modes[k], sems[2 * gi], sems[2 * gi + 1], ki, p, False).start()
        token[...] = jnp.zeros_like(token)

    sem_shapes = []
    for idx in groups:
        sem_shapes += [pltpu.SemaphoreType.DMA((len(idx) * (N_DEV - 1),))] * 2
    outs = pl.pallas_call(
        body,
        out_shape=sem_shapes + [pltpu.HBM(a.shape, a.dtype) for a in srcs] + [pltpu.HBM(a.shape, a.dtype) for a in lands]
        + [jax.ShapeDtypeStruct((8, 128), F32)],
        in_specs=[_HBM] * (2 * n),
        out_specs=[_SEM] * (2 * ng) + [_HBM] * (2 * n) + [pl.BlockSpec(memory_space=pltpu.VMEM)],
        input_output_aliases={i: 2 * ng + i for i in range(2 * n)},
        compiler_params=pltpu.CompilerParams(has_side_effects=_EFFECT),
        name=name,
    )(*[pltpu.with_memory_space_constraint(a, pltpu.HBM) for a in list(srcs) + lands])
    sems = [(outs[2 * gi], outs[2 * gi + 1]) for gi in range(ng)]
    return sems, outs[2 * ng:2 * ng + n], outs[2 * ng + n:2 * ng + 2 * n], outs[-1]


def _xchg_wait_call(srcs, lands, modes, sems, after, name):
    n = len(srcs)

    def body(*refs):
        src, land = refs[:n], refs[n:2 * n]
        send_sems, recv_sems = refs[2 * n], refs[2 * n + 1]
        for p in range(1, N_DEV):
            for k in range(n):
                cp = _xchg_copy(src[k], land[k], modes[k], send_sems, recv_sems, k, p, True)
                cp.wait_send()
                cp.wait_recv()

    outs = pl.pallas_call(
        body,
        out_shape=[pltpu.HBM(a.shape, a.dtype) for a in list(srcs) + list(lands)],
        in_specs=[_HBM] * (2 * n) + [_SEM, _SEM, pl.BlockSpec(memory_space=pl.ANY)],
        out_specs=[_HBM] * (2 * n),
        input_output_aliases={i: i for i in range(2 * n)},
        compiler_params=pltpu.CompilerParams(has_side_effects=_EFFECT),
        name=name,
    )(*srcs, *lands, sems[0], sems[1], after)
    return outs[:n], outs[n:]


def _xchg_wait(srcs, lands, modes, sems, after, name):
    srcs, lands = _xchg_wait_call(srcs, lands, modes, sems, after, name)
    me = 4 * lax.axis_index("x") + 2 * lax.axis_index("y") + lax.axis_index("c")
    done = []
    for a, land, mode in zip(srcs, lands, modes):
        own = a[None] if mode == "gather" else lax.dynamic_slice_in_dim(a, me, 1, axis=0)
        done.append(lax.dynamic_update_slice(land, own, (me, 0, 0)))
    return done


def _sum_slots(land):
    g, _, r, c = land.shape
    br = r // 2 if (r % 32 == 0 and r >= 256) else r

    def body(l_ref, o_ref):
        acc = l_ref[0, 0].astype(F32)
        for j in range(1, N_DEV):
            acc = acc + l_ref[0, j].astype(F32)
        o_ref[0] = acc

    return pl.pallas_call(
        body,
        out_shape=jax.ShapeDtypeStruct((g, r, c), F32),
        grid=(g, r // br),
        in_specs=[pl.BlockSpec((1, N_DEV, br, c), lambda i, j: (i, 0, j, 0))],
        out_specs=pl.BlockSpec((1, br, c), lambda i, j: (i, j, 0)),
        compiler_params=_cparams(("parallel", "parallel")),
        name="sum_slots",
    )(land)


def _adamw(w, g, m, v):
    shape = w.shape
    cols = shape[-1]
    rows = int(np.prod(shape[:-1]))
    bm = max(b for b in range(8, 257, 8) if rows % b == 0) if rows % 8 == 0 else rows
    c1 = 1.0 - ADAM_B1 ** ADAM_STEP
    c2 = 1.0 - ADAM_B2 ** ADAM_STEP

    def fn(ins, consts):
        wv, gv, mv, vv = ins
        m2 = ADAM_B1 * mv + (1.0 - ADAM_B1) * gv
        v2 = ADAM_B2 * vv + (1.0 - ADAM_B2) * (gv * gv)
        delta = -ADAM_LR * ((m2 / c1) / (jnp.sqrt(v2 / c2) + ADAM_EPS) + ADAM_WD * wv)
        return [delta, m2, v2], []

    outs, _ = _rowwise(fn, [(a.reshape(rows, cols), cols, 0) for a in (w, g, m, v)], [], [(cols, F32)] * 3, [],
                       bm=bm, name="adamw")
    return [o.reshape(shape) for o in outs]


BIG = ("w1t", "w1o", "wint", "wa", "wbt", "wo", "w2t", "w2o")
FETCH_GROUPS = dict(w1t=("w1t",), w1o=("w1o",), wint=("wint",), mout=("wa", "wbt", "wo"), w2t=("w2t",), w2o=("w2o",))
SMALL_ROWS = (("ffn1_norm", 0), ("mix_norm", 2), ("lbsum", 4), ("hgrn_out_norm", 6), ("ffn2_norm", 8),
              ("attn_q_norm", 10), ("attn_k_norm", 12))
SMALL_PACK_ROWS = 16


def kernel(x, ffn1_norm, ffn1_w_in, ffn1_w_out, mix_norm, w_in, hgrn_lb_logits, hgrn_out_norm, attn_q_norm, attn_k_norm, w_branch_a, w_branch_b, w_out, ffn2_norm, ffn2_w_in, ffn2_w_out, loss_target, m_ffn1_norm, m_ffn1_w_in, m_ffn1_w_out, m_mix_norm, m_w_in, m_hgrn_lb_logits, m_hgrn_out_norm, m_attn_q_norm, m_attn_k_norm, m_w_branch_a, m_w_branch_b, m_w_out, m_ffn2_norm, m_ffn2_w_in, m_ffn2_w_out, v_ffn1_norm, v_ffn1_w_in, v_ffn1_w_out, v_mix_norm, v_w_in, v_hgrn_lb_logits, v_hgrn_out_norm, v_attn_q_norm, v_attn_k_norm, v_w_branch_a, v_w_branch_b, v_w_out, v_ffn2_norm, v_ffn2_w_in, v_ffn2_w_out):
    names = ("ffn1_norm", "ffn1_w_in", "ffn1_w_out", "mix_norm", "w_in", "hgrn_lb_logits", "hgrn_out_norm", "attn_q_norm",
             "attn_k_norm", "w_branch_a", "w_branch_b", "w_out", "ffn2_norm", "ffn2_w_in", "ffn2_w_out")
    w = dict(zip(names, (ffn1_norm, ffn1_w_in, ffn1_w_out, mix_norm, w_in, hgrn_lb_logits, hgrn_out_norm, attn_q_norm,
                         attn_k_norm, w_branch_a, w_branch_b, w_out, ffn2_norm, ffn2_w_in, ffn2_w_out)))
    m = dict(zip(names, (m_ffn1_norm, m_ffn1_w_in, m_ffn1_w_out, m_mix_norm, m_w_in, m_hgrn_lb_logits, m_hgrn_out_norm,
                         m_attn_q_norm, m_attn_k_norm, m_w_branch_a, m_w_branch_b, m_w_out, m_ffn2_norm, m_ffn2_w_in, m_ffn2_w_out)))
    v = dict(zip(names, (v_ffn1_norm, v_ffn1_w_in, v_ffn1_w_out, v_mix_norm, v_w_in, v_hgrn_lb_logits, v_hgrn_out_norm,
                         v_attn_q_norm, v_attn_k_norm, v_w_branch_a, v_w_branch_b, v_w_out, v_ffn2_norm, v_ffn2_w_in, v_ffn2_w_out)))
    depth, d = ffn1_norm.shape

    def tr(a):
        return jnp.swapaxes(a, 1, 2)

    shard = dict(w1t=tr(ffn1_w_in), w1o=ffn1_w_out, wint=tr(w_in), wa=w_branch_a,
                 wbt=tr(w_branch_b).reshape(depth, -1, d), wo=w_out, w2t=tr(ffn2_w_in), w2o=ffn2_w_out)
    order = [(g, l) for l in range(depth) for g in FETCH_GROUPS]
    flat = [(g, l, k) for g, l in order for k in FETCH_GROUPS[g]]
    groups, pos = [], 0
    for g, l in order:
        groups.append(list(range(pos, pos + len(FETCH_GROUPS[g]))))
        pos += len(FETCH_GROUPS[g])
    g_sems, g_srcs, g_lands, _ = _xchg_start([shard[k][l].astype(BF16) for _, l, k in flat], ["gather"] * len(flat),
                                             groups, "gather_start")

    def fetch(group, l, after):
        gi = order.index((group, l))
        idx = groups[gi]
        lands = _xchg_wait([g_srcs[i] for i in idx], [g_lands[i] for i in idx], ["gather"] * len(idx), g_sems[gi], after,
                           f"gather_wait_{group}{l}")
        out = {}
        for k, land in zip(FETCH_GROUPS[group], lands):
            out[k] = land.reshape(d, -1) if k == "wbt" else land.reshape(-1, d)
        return out

    pending = []

    def emit(group, l, g, final):
        keys = list(g)
        srcs = [g[k].reshape(N_DEV, -1, d) for k in keys]
        modes = ["scatter"] * len(keys)
        if final is not None:
            gsmall, loss = final
            pack = jnp.zeros((SMALL_PACK_ROWS, d), F32)
            for k, r0 in SMALL_ROWS:
                rows = gsmall[k].reshape(depth, -1)
                pack = pack.at[r0:r0 + depth, :rows.shape[1]].set(rows)
            srcs.append(pack.at[14, :].set(loss))
            modes.append("gather")
            keys.append("small")
        sems, s_thru, l_thru, token = _xchg_start(srcs, modes, [list(range(len(srcs)))], f"grads_start_{group}{l}")
        pending.append((group, l, keys, modes, sems[0], s_thru, l_thru))
        return token[0, 0]

    small = {k: w[k] for k in ("ffn1_norm", "mix_norm", "hgrn_lb_logits", "hgrn_out_norm", "attn_q_norm", "attn_k_norm", "ffn2_norm")}
    dx = _local_step(x[0], loss_target[0], small, fetch, emit)

    summed = {}
    for group, l, keys, modes, sems, s_thru, l_thru in pending:
        lands = _xchg_wait(s_thru, l_thru, modes, sems, dx, f"grads_wait_{group}{l}")
        for k, land in zip(keys, lands):
            summed[k, l] = _sum_slots(land[None])[0]
    gsum = {k: jnp.stack([summed[k, l] for l in range(depth)]) for k in BIG}
    tot = summed["small", 0]

    grads = {}
    for k, r0 in SMALL_ROWS:
        shp = (depth,) + (w[k].shape[1:] if k != "lbsum" else (d,))
        grads[k] = tot[r0:r0 + depth, :int(np.prod(shp[1:]))].reshape(shp)
    _, lb_vjp = jax.vjp(_lower_bounds, hgrn_lb_logits)
    grads["hgrn_lb_logits"] = lb_vjp(grads.pop("lbsum"))[0]
    grads["ffn1_w_in"], grads["ffn1_w_out"] = tr(gsum["w1t"]), gsum["w1o"]
    grads["w_in"], grads["w_branch_a"] = tr(gsum["wint"]), gsum["wa"]
    grads["w_branch_b"] = tr(gsum["wbt"].reshape(depth, d // N_DEV, -1))
    grads["w_out"] = gsum["wo"]
    grads["ffn2_w_in"], grads["ffn2_w_out"] = tr(gsum["w2t"]), gsum["w2o"]

    upd = {k: _adamw(w[k], grads[k], m[k], v[k]) for k in names}
    return (tot[14, 0], dx[None], *[grads[k] for k in names], *[upd[k][0] for k in names],
            *[upd[k][1] for k in names], *[upd[k][2] for k in names])
```

```python
import functools
import math

import jax
import jax.numpy as jnp
import numpy as np
from jax import lax
from jax.experimental import pallas as pl
from jax.experimental.pallas import tpu as pltpu

F32 = jnp.float32
BF16 = jnp.bfloat16

N_DEV = 8
EPS = 1e-6
HG_DK = 128
HG_CHUNK = 64
HG_SUB = 16
HG_HP = 4
ATT_PATTERNS = ((128, 1), (512, 4), (2048, 16))
ATT_GROUPS = 3
ATT_HEADS = 4
ATT_DH = 128
ATT_BLK = 128
ROPE_THETA = 10000.0
ADAM_LR, ADAM_B1, ADAM_B2, ADAM_EPS, ADAM_WD, ADAM_STEP = 0.001, 0.9, 0.999, 1e-08, 0.01, 10
VMEM_LIMIT_BYTES = 56 * 1024 * 1024
MESH = pl.DeviceIdType.MESH


def _cparams(sem, **kw):
    return pltpu.CompilerParams(dimension_semantics=sem, vmem_limit_bytes=VMEM_LIMIT_BYTES, **kw)


def _sigmoid(x):
    return 1.0 / (1.0 + jnp.exp(-x))


def _mm(a_list, b_list, pairs, n_acc, fin, out_dtypes, *, m, n, k, ta=False, tb=False, bm, bn, bk,
        b_off=None, extras=(), e_off=None, n_outer=False, name):
    bm, bn, bk = min(bm, m), min(bn, n), min(bk, k)
    assert m % bm == 0 and n % bn == 0 and k % bk == 0, (name, m, n, k, bm, bn, bk)
    nk = k // bk
    b_off = b_off or [(0, 0)] * len(b_list)
    e_off = e_off or [0] * len(extras)
    na, nb, ne = len(a_list), len(b_list), len(extras)
    dn = (((0,) if ta else (1,), (1,) if tb else (0,)), ((), ()))

    def body(*refs):
        a_refs, b_refs = refs[:na], refs[na:na + nb]
        e_refs = refs[na + nb:na + nb + ne]
        o_refs = refs[na + nb + ne:na + nb + ne + len(out_dtypes)]
        acc_refs = refs[na + nb + ne + len(out_dtypes):]
        kk = pl.program_id(2)
        parts = [None] * n_acc
        for ai, bi, ci in pairs:
            p = lax.dot_general(a_refs[ai][...], b_refs[bi][...], dn, preferred_element_type=F32)
            parts[ci] = p if parts[ci] is None else parts[ci] + p

        def finish(accs):
            outs = fin(accs, [e[...] for e in e_refs])
            for o_ref, o in zip(o_refs, outs):
                o_ref[...] = o.astype(o_ref.dtype)

        if nk == 1:
            finish(parts)
        else:
            @pl.when(kk == 0)
            def _():
                for c in range(n_acc):
                    acc_refs[c][...] = parts[c]

            @pl.when(kk > 0)
            def _():
                for c in range(n_acc):
                    acc_refs[c][...] += parts[c]

            @pl.when(kk == nk - 1)
            def _():
                finish([acc_refs[c][...] for c in range(n_acc)])

    def ij(f):
        return (lambda j, i, q: f(i, j, q)) if n_outer else f

    a_spec = pl.BlockSpec((bk, bm), ij(lambda i, j, q: (q, i))) if ta else pl.BlockSpec((bm, bk), ij(lambda i, j, q: (i, q)))

    def b_spec(off):
        on, ok = off
        if tb:
            return pl.BlockSpec((bn, bk), ij(lambda i, j, q: (j + on, q + ok)))
        return pl.BlockSpec((bk, bn), ij(lambda i, j, q: (q + ok, j + on)))

    mn_spec = pl.BlockSpec((bm, bn), ij(lambda i, j, q: (i, j)))
    outs = pl.pallas_call(
        body,
        out_shape=[jax.ShapeDtypeStruct((m, n), d) for d in out_dtypes],
        grid=(n // bn, m // bm, nk) if n_outer else (m // bm, n // bn, nk),
        in_specs=[a_spec] * na + [b_spec(o) for o in b_off]
        + [pl.BlockSpec((bm, bn), ij(lambda i, j, q, o=o: (i, j + o))) for o in e_off],
        out_specs=[mn_spec] * len(out_dtypes),
        scratch_shapes=[pltpu.VMEM((bm, bn), F32) for _ in range(n_acc if nk > 1 else 0)],
        compiler_params=_cparams(("parallel", "parallel", "arbitrary")),
        name=name,
    )(*a_list, *b_list, *extras)
    return outs


def _first(accs, ex):
    return (accs[0],)


def _rowwise(fn, ins, consts, out_defs, sum_widths, *, bm, name):
    ins = [tuple(e) + (1,) * (4 - len(e)) for e in ins]
    out_defs = [tuple(e) + (1,) * (3 - len(e)) for e in out_defs]
    t = ins[0][0].shape[-2] * ins[0][3]
    bm = min(bm, t)
    assert t % bm == 0, (name, t, bm)
    ni, nc, no, ns = len(ins), len(consts), len(out_defs), len(sum_widths)
    strided = [w for _, w, _, d in ins if d > 1] + [w for w, _, d in out_defs if d > 1]

    def body(*refs):
        i_refs, c_refs = refs[:ni], refs[ni:ni + nc]
        o_refs, s_refs = refs[ni + nc:ni + nc + no], refs[ni + nc + no:ni + nc + no + ns]
        scratch = list(refs[ni + nc + no + ns:])
        vals = []
        for ref, (_, w, _, d) in zip(i_refs, ins):
            if d == 1:
                vals.append(ref[...])
                continue
            s = scratch.pop(0)
            for r in range(d):
                for c in range(w // 128):
                    s.at[c][pl.ds(r, bm // d, stride=d), :] = ref[r, :, c * 128:(c + 1) * 128].astype(F32)
            vals.append(jnp.concatenate([s[c] for c in range(w // 128)], axis=1))
        outs, sums = fn(vals, [r[...] for r in c_refs])
        for o_ref, o, (w, _, d) in zip(o_refs, outs, out_defs):
            if d == 1:
                o_ref[...] = o.astype(o_ref.dtype)
                continue
            s = scratch.pop(0)
            for c in range(w // 128):
                s[c] = o[:, c * 128:(c + 1) * 128].astype(F32)
            for r in range(d):
                for c in range(w // 128):
                    o_ref[r, :, c * 128:(c + 1) * 128] = s.at[c][pl.ds(r, bm // d, stride=d), :].astype(o_ref.dtype)
        if ns:
            first = pl.program_id(0) == 0

            @pl.when(first)
            def _():
                for s_ref, s in zip(s_refs, sums):
                    s_ref[...] = s

            @pl.when(jnp.logical_not(first))
            def _():
                for s_ref, s in zip(s_refs, sums):
                    s_ref[...] += s

    def win(width, cb, d):
        if d > 1:
            return pl.BlockSpec((d, bm // d, width), lambda i: (0, i, 0))
        return pl.BlockSpec((bm, width), lambda i: (i, cb))

    res = pl.pallas_call(
        body,
        out_shape=[jax.ShapeDtypeStruct((t, w) if d == 1 else (d, t // d, w), dt) for w, dt, d in out_defs]
        + [jax.ShapeDtypeStruct((8, w), F32) for w in sum_widths],
        grid=(t // bm,),
        in_specs=[win(w, cb, d) for _, w, cb, d in ins] + [pl.BlockSpec(c.shape, lambda i, nd=c.ndim: (0,) * nd) for c in consts],
        out_specs=[win(w, 0, d) for w, _, d in out_defs] + [pl.BlockSpec((8, w), lambda i: (0, 0)) for w in sum_widths],
        scratch_shapes=[pltpu.VMEM((w // 128, bm, 128), F32) for w in strided],
        compiler_params=_cparams(("arbitrary",) if ns else ("parallel",)),
        name=name,
    )(*[e[0] for e in ins], *consts)
    return res[:no], [jnp.sum(s, axis=0) for s in res[no:]]


def _colsum8(x):
    bm, w = x.shape
    return jnp.sum(x.reshape(bm // 8, 8, w), axis=0)


def _tri(n, upper=False):
    r = lax.broadcasted_iota(jnp.int32, (n, n), 0)
    c = lax.broadcasted_iota(jnp.int32, (n, n), 1)
    return (c >= r) if upper else (c <= r)


def _exact_tri_matmul(tri_bf16, x):
    x0 = x.astype(BF16)
    r1 = x - x0.astype(F32)
    x1 = r1.astype(BF16)
    x2 = (r1 - x1.astype(F32)).astype(BF16)
    w = x.shape[1]
    y = jnp.dot(tri_bf16, jnp.concatenate([x0, x1, x2], axis=1), preferred_element_type=F32)
    return y[:, :w] + y[:, w:2 * w] + y[:, 2 * w:]


def _dot_nt(a, b):
    return lax.dot_general(a, b, (((1,), (1,)), ((), ())), preferred_element_type=F32)


def _dot_tn(a, b):
    return lax.dot_general(a, b, (((0,), (0,)), ((), ())), preferred_element_type=F32)


def _dot(a, b):
    return jnp.dot(a, b, preferred_element_type=F32)


def _hg_gates(hq, hf, lb):
    sq = _sigmoid(hq)
    q = hq * sq
    sg = _sigmoid(hf)
    f = lb + (1.0 - lb) * sg
    return q, sq, sg, f


def _hg_intra(q, kk, g):
    c = q.shape[0]
    rows = lax.broadcasted_iota(jnp.int32, (c, 1), 0)
    a_rows, qts, kts, eqs, eks = [], [], [], [], []
    for i in range(c // HG_SUB):
        lo = i * HG_SUB
        ref = g[lo - 1:lo, :] if i else jnp.zeros_like(g[0:1, :])
        eq = jnp.exp(g[lo:lo + HG_SUB, :] - ref)
        ek = jnp.exp(jnp.where(rows < lo + HG_SUB, ref - g, 0.0))
        qt = q[lo:lo + HG_SUB, :] * eq
        kt = kk * ek
        a = _dot_nt(qt.astype(BF16), kt.astype(BF16))
        tpos = lo + lax.broadcasted_iota(jnp.int32, (HG_SUB, c), 0)
        spos = lax.broadcasted_iota(jnp.int32, (HG_SUB, c), 1)
        a_rows.append(jnp.where(spos <= tpos, a, 0.0))
        qts.append(qt), kts.append(kt), eqs.append(eq), eks.append(ek)
    return jnp.concatenate(a_rows, axis=0), qts, kts, eqs, eks


def _hgrn_fwd_serial(zh, lb3, *, tb=512):
    t = zh.shape[0]
    nh = lb3.shape[0]
    c = HG_CHUNK
    tb = min(tb, t)
    nchunk = tb // c
    hp = HG_HP if nh % HG_HP == 0 else 1

    def body(hq_ref, hf_ref, hi_ref, lb_ref, o_ref, st_ref, state):
        @pl.when(pl.program_id(1) == 0)
        def _():
            state[...] = jnp.zeros_like(state)

        tril = _tri(c).astype(BF16)

        def one_head(hh, ci, sl):
            ls = slice(hh * HG_DK, (hh + 1) * HG_DK)
            q, _, _, f = _hg_gates(hq_ref[sl, ls], hf_ref[sl, ls], lb_ref[hh])
            v = hi_ref[sl, ls]
            kk = 1.0 - f
            g = _exact_tri_matmul(tril, jnp.log(f))
            a, _, _, _, _ = _hg_intra(q, kk, g)
            st = state[hh]
            st_ref[hh, ci] = st
            vb = v.astype(BF16)
            o = _dot(a.astype(BF16), vb) + _dot_nt((q * jnp.exp(g)).astype(BF16), st.astype(BF16))
            o_ref[sl, ls] = o
            glast = g[c - 1:c, :]
            kg = kk * jnp.exp(glast - g)
            state[hh] = st * jnp.exp(glast) + _dot_tn(vb, kg.astype(BF16))

        def chunk(ci, carry):
            sl = pl.ds(pl.multiple_of(ci * c, c), c)
            for hh in range(hp):
                one_head(hh, ci, sl)
            return carry

        lax.fori_loop(0, nchunk, chunk, 0)

    def col(cb):
        return pl.BlockSpec((tb, hp * HG_DK), lambda h, i: (i, cb * (nh // hp) + h))

    return pl.pallas_call(
        body,
        out_shape=[jax.ShapeDtypeStruct((t, nh * HG_DK), F32), jax.ShapeDtypeStruct((nh, t // c, HG_DK, HG_DK), F32)],
        grid=(nh // hp, t // tb),
        in_specs=[col(0), col(1), col(2), pl.BlockSpec((hp, 1, HG_DK), lambda h, i: (h, 0, 0))],
        out_specs=[pl.BlockSpec((tb, hp * HG_DK), lambda h, i: (i, h)),
                   pl.BlockSpec((hp, nchunk, HG_DK, HG_DK), lambda h, i: (h, i, 0, 0))],
        scratch_shapes=[pltpu.VMEM((hp, HG_DK, HG_DK), F32)],
        compiler_params=_cparams(("parallel", "arbitrary")),
        name="hgrn_fwd",
    )(zh, zh, zh, lb3)


def _hgrn_bwd_serial(zh, lb3, states, d_o, *, tb=512):
    t = zh.shape[0]
    nh = lb3.shape[0]
    c = HG_CHUNK
    tb = min(tb, t)
    nchunk = tb // c
    nblk = t // tb
    hp = HG_HP if nh % HG_HP == 0 else 1

    def body(hq_ref, hf_ref, hi_ref, lb_ref, st_ref, do_ref, dq_ref, df_ref, dv_ref, dlb_ref, dstate):
        @pl.when(pl.program_id(1) == 0)
        def _():
            dstate[...] = jnp.zeros_like(dstate)
            dlb_ref[...] = jnp.zeros_like(dlb_ref)

        tril = _tri(c).astype(BF16)
        triu = _tri(c, upper=True).astype(BF16)
        last_row = lax.broadcasted_iota(jnp.int32, (c, 1), 0) == c - 1

        def one_head(hh, ci, sl):
            ls = slice(hh * HG_DK, (hh + 1) * HG_DK)
            lb = lb_ref[hh]
            hq, hf = hq_ref[sl, ls], hf_ref[sl, ls]
            q, sq, sg, f = _hg_gates(hq, hf, lb)
            v = hi_ref[sl, ls]
            kk = 1.0 - f
            g = _exact_tri_matmul(tril, jnp.log(f))
            a, qts, kts, eqs, eks = _hg_intra(q, kk, g)
            st = st_ref[hh, ci]
            dst = dstate[hh]
            do = do_ref[sl, ls]
            dob, vb = do.astype(BF16), v.astype(BF16)
            glast = g[c - 1:c, :]
            eg = jnp.exp(g)
            egl = jnp.exp(glast - g)
            qg = q * eg
            kg = kk * egl
            dv = _dot_tn(a.astype(BF16), dob) + _dot_nt(kg.astype(BF16), dst.astype(BF16))
            da = jnp.where(_tri(c), _dot_nt(dob, vb), 0.0).astype(BF16)
            dq_parts, dgq_parts = [], []
            dk = jnp.zeros_like(kk)
            dgk = jnp.zeros_like(kk)
            for i in range(c // HG_SUB):
                da_i = da[i * HG_SUB:(i + 1) * HG_SUB, :]
                ktb, qtb = kts[i].astype(BF16), qts[i].astype(BF16)
                xi = _dot(da_i, ktb)
                yi = _dot_tn(da_i, qtb)
                dq_parts.append(xi * eqs[i])
                dk = dk + yi * eks[i]
                dgq_parts.append(xi * qtb.astype(F32))
                dgk = dgk + yi * ktb.astype(F32)
            dq_inter = _dot(dob, st.astype(BF16)) * eg
            dq = jnp.concatenate(dq_parts, axis=0) + dq_inter
            dk_state = _dot(vb, dst.astype(BF16)) * egl
            dk = dk + dk_state
            dg = jnp.concatenate(dgq_parts, axis=0) - dgk + q * dq_inter - kk * dk_state
            dgl = jnp.sum(kk * dk_state, axis=0, keepdims=True) + jnp.exp(glast) * jnp.sum(st * dst, axis=0, keepdims=True)
            dg = dg + jnp.where(last_row, dgl, 0.0)
            dlogf = _exact_tri_matmul(triu, dg)
            dfv = dlogf / f - dk
            dq_ref[sl, ls] = (dq * (sq * (1.0 + hq * (1.0 - sq)))).astype(dq_ref.dtype)
            df_ref[sl, ls] = (dfv * (1.0 - lb) * sg * (1.0 - sg)).astype(df_ref.dtype)
            dv_ref[sl, ls] = dv.astype(dv_ref.dtype)
            dlb_ref[hh] += jnp.sum(dfv * (1.0 - sg), axis=0, keepdims=True)
            dstate[hh] = dst * jnp.exp(glast) + _dot_tn(dob, qg.astype(BF16))

        def chunk(j, carry):
            ci = nchunk - 1 - j
            sl = pl.ds(pl.multiple_of(ci * c, c), c)
            for hh in range(hp):
                one_head(hh, ci, sl)
            return carry

        lax.fori_loop(0, nchunk, chunk, 0)

    def col(cb):
        return pl.BlockSpec((tb, hp * HG_DK), lambda h, i: (nblk - 1 - i, cb * (nh // hp) + h))

    ocol = pl.BlockSpec((tb, hp * HG_DK), lambda h, i: (nblk - 1 - i, h))
    w = nh * HG_DK
    dq, df, dv, dlb = pl.pallas_call(
        body,
        out_shape=[jax.ShapeDtypeStruct((t, w), BF16)] * 3 + [jax.ShapeDtypeStruct((nh, 1, HG_DK), F32)],
        grid=(nh // hp, nblk),
        in_specs=[col(0), col(1), col(2), pl.BlockSpec((hp, 1, HG_DK), lambda h, i: (h, 0, 0)),
                  pl.BlockSpec((hp, nchunk, HG_DK, HG_DK), lambda h, i: (h, nblk - 1 - i, 0, 0)), ocol],
        out_specs=[ocol, ocol, ocol, pl.BlockSpec((hp, 1, HG_DK), lambda h, i: (h, 0, 0))],
        scratch_shapes=[pltpu.VMEM((hp, HG_DK, HG_DK), F32)],
        compiler_params=_cparams(("parallel", "arbitrary")),
        name="hgrn_bwd",
    )(zh, zh, zh, lb3, states, d_o)
    return dq, df, dv, dlb.reshape(w)


def _hg_heads(x, hp):
    return [x[:, h * HG_DK:(h + 1) * HG_DK] for h in range(hp)]


def _hg_intra_wide(q, kk, g, hp):
    c = q.shape[0]
    rows = lax.broadcasted_iota(jnp.int32, (c, 1), 0)
    a_rows = [[] for _ in range(hp)]
    qts, kts, eqs, eks = [], [], [], []
    for i in range(c // HG_SUB):
        lo = i * HG_SUB
        ref = g[lo - 1:lo, :] if i else jnp.zeros_like(g[0:1, :])
        eq = jnp.exp(g[lo:lo + HG_SUB, :] - ref)
        ek = jnp.exp(jnp.where(rows < lo + HG_SUB, ref - g, 0.0))
        qtb = (q[lo:lo + HG_SUB, :] * eq).astype(BF16)
        ktb = (kk * ek).astype(BF16)
        tpos = lo + lax.broadcasted_iota(jnp.int32, (HG_SUB, c), 0)
        spos = lax.broadcasted_iota(jnp.int32, (HG_SUB, c), 1)
        for h, (qh, kh) in enumerate(zip(_hg_heads(qtb, hp), _hg_heads(ktb, hp))):
            a_rows[h].append(jnp.where(spos <= tpos, _dot_nt(qh, kh), 0.0))
        qts.append(qtb), kts.append(ktb), eqs.append(eq), eks.append(ek)
    return [jnp.concatenate(r, axis=0) for r in a_rows], qts, kts, eqs, eks


def _hgrn_fwd(zh, lb3, *, tb=512):
    t = zh.shape[0]
    nh = lb3.shape[0]
    c = HG_CHUNK
    tb = min(tb, t)
    nchunk = tb // c
    hp = HG_HP if nh % HG_HP == 0 else 1
    wp = hp * HG_DK

    def body(hq_ref, hf_ref, hi_ref, lb_ref, o_ref, st_ref, state):
        @pl.when(pl.program_id(1) == 0)
        def _():
            state[...] = jnp.zeros_like(state)

        tril = _tri(c).astype(BF16)

        def chunk(ci, carry):
            sl = pl.ds(pl.multiple_of(ci * c, c), c)
            q, _, _, f = _hg_gates(hq_ref[sl, :], hf_ref[sl, :], lb_ref[...])
            kk = 1.0 - f
            g = _exact_tri_matmul(tril, jnp.log(f))
            a, _, _, _, _ = _hg_intra_wide(q, kk, g, hp)
            vb = hi_ref[sl, :].astype(BF16)
            glast = g[c - 1:c, :]
            qgb = (q * jnp.exp(g)).astype(BF16)
            kgb = (kk * jnp.exp(glast - g)).astype(BF16)
            dec = jnp.exp(glast)
            sts = [state[h] for h in range(hp)]
            for h in range(hp):
                st_ref[h, ci] = sts[h]
            vh, qgh, kgh, dech = _hg_heads(vb, hp), _hg_heads(qgb, hp), _hg_heads(kgb, hp), _hg_heads(dec, hp)
            o = [_dot(a[h].astype(BF16), vh[h]) + _dot_nt(qgh[h], sts[h].astype(BF16)) for h in range(hp)]
            new = [_dot_tn(vh[h], kgh[h]) for h in range(hp)]
            o_ref[sl, :] = jnp.concatenate(o, axis=1)
            for h in range(hp):
                state[h] = sts[h] * dech[h] + new[h]
            return carry

        lax.fori_loop(0, nchunk, chunk, 0)

    def col(cb):
        return pl.BlockSpec((tb, wp), lambda h, i: (i, cb * (nh // hp) + h))

    return pl.pallas_call(
        body,
        out_shape=[jax.ShapeDtypeStruct((t, nh * HG_DK), F32), jax.ShapeDtypeStruct((nh, t // c, HG_DK, HG_DK), F32)],
        grid=(nh // hp, t // tb),
        in_specs=[col(0), col(1), col(2), pl.BlockSpec((1, wp), lambda h, i: (0, h))],
        out_specs=[pl.BlockSpec((tb, wp), lambda h, i: (i, h)),
                   pl.BlockSpec((hp, nchunk, HG_DK, HG_DK), lambda h, i: (h, i, 0, 0))],
        scratch_shapes=[pltpu.VMEM((hp, HG_DK, HG_DK), F32)],
        compiler_params=_cparams(("parallel", "arbitrary")),
        name="hgrn_fwd",
    )(zh, zh, zh, lb3.reshape(1, -1))


def _hgrn_bwd(zh, lb3, states, d_o, *, tb=512):
    t = zh.shape[0]
    nh = lb3.shape[0]
    c = HG_CHUNK
    tb = min(tb, t)
    nchunk = tb // c
    nblk = t // tb
    hp = HG_HP if nh % HG_HP == 0 else 1
    wp = hp * HG_DK

    def body(hq_ref, hf_ref, hi_ref, lb_ref, st_ref, do_ref, dq_ref, df_ref, dv_ref, dlb_ref, dstate):
        @pl.when(pl.program_id(1) == 0)
        def _():
            dstate[...] = jnp.zeros_like(dstate)
            dlb_ref[...] = jnp.zeros_like(dlb_ref)

        tril = _tri(c).astype(BF16)
        triu = _tri(c, upper=True).astype(BF16)
        last_row = lax.broadcasted_iota(jnp.int32, (c, 1), 0) == c - 1
        heads = range(hp)

        def chunk(j, carry):
            ci = nchunk - 1 - j
            sl = pl.ds(pl.multiple_of(ci * c, c), c)
            lb = lb_ref[...]
            hq, hf = hq_ref[sl, :], hf_ref[sl, :]
            q, sq, sg, f = _hg_gates(hq, hf, lb)
            kk = 1.0 - f
            g = _exact_tri_matmul(tril, jnp.log(f))
            a, qts, kts, eqs, eks = _hg_intra_wide(q, kk, g, hp)
            glast = g[c - 1:c, :]
            eg, egl, dec = jnp.exp(g), jnp.exp(glast - g), jnp.exp(glast)
            vb, dob = hi_ref[sl, :].astype(BF16), do_ref[sl, :].astype(BF16)
            qgb, kgb = (q * eg).astype(BF16), (kk * egl).astype(BF16)
            sts = [st_ref[h, ci] for h in heads]
            dsts = [dstate[h] for h in heads]
            stb, dstb = [s.astype(BF16) for s in sts], [s.astype(BF16) for s in dsts]
            vh, doh, qgh, kgh = _hg_heads(vb, hp), _hg_heads(dob, hp), _hg_heads(qgb, hp), _hg_heads(kgb, hp)
            dv = [_dot_tn(a[h].astype(BF16), doh[h]) + _dot_nt(kgh[h], dstb[h]) for h in heads]
            da = [jnp.where(_tri(c), _dot_nt(doh[h], vh[h]), 0.0).astype(BF16) for h in heads]
            dq_inter = jnp.concatenate([_dot(doh[h], stb[h]) for h in heads], axis=1) * eg
            dk_state = jnp.concatenate([_dot(vh[h], dstb[h]) for h in heads], axis=1) * egl
            new_dst = [_dot_tn(doh[h], qgh[h]) for h in heads]
            xs, dk, dgk = [], dk_state, 0.0
            for i in range(c // HG_SUB):
                rs = slice(i * HG_SUB, (i + 1) * HG_SUB)
                kth, qth = _hg_heads(kts[i], hp), _hg_heads(qts[i], hp)
                xi = jnp.concatenate([_dot(da[h][rs, :], kth[h]) for h in heads], axis=1)
                yi = jnp.concatenate([_dot_tn(da[h][rs, :], qth[h]) for h in heads], axis=1)
                xs.append(xi)
                dk = dk + yi * eks[i]
                dgk = dgk + yi * kts[i].astype(F32)
            dq = jnp.concatenate([x * e for x, e in zip(xs, eqs)], axis=0) + dq_inter
            dgq = jnp.concatenate([x * qt.astype(F32) for x, qt in zip(xs, qts)], axis=0)
            dg = dgq - dgk + q * dq_inter - kk * dk_state
            sdot = jnp.concatenate([jnp.sum(sts[h] * dsts[h], axis=0, keepdims=True) for h in heads], axis=1)
            dgl = jnp.sum(kk * dk_state, axis=0, keepdims=True) + dec * sdot
            dg = dg + jnp.where(last_row, dgl, 0.0)
            dlogf = _exact_tri_matmul(triu, dg)
            dfv = dlogf / f - dk
            dq_ref[sl, :] = (dq * (sq * (1.0 + hq * (1.0 - sq)))).astype(dq_ref.dtype)
            df_ref[sl, :] = (dfv * (1.0 - lb) * sg * (1.0 - sg)).astype(df_ref.dtype)
            dv_ref[sl, :] = jnp.concatenate(dv, axis=1).astype(dv_ref.dtype)
            dlb_ref[...] += jnp.sum(dfv * (1.0 - sg), axis=0, keepdims=True)
            dech = _hg_heads(dec, hp)
            for h in heads:
                dstate[h] = dsts[h] * dech[h] + new_dst[h]
            return carry

        lax.fori_loop(0, nchunk, chunk, 0)

    def col(cb):
        return pl.BlockSpec((tb, wp), lambda h, i: (nblk - 1 - i, cb * (nh // hp) + h))

    ocol = pl.BlockSpec((tb, wp), lambda h, i: (nblk - 1 - i, h))
    lbspec = pl.BlockSpec((1, wp), lambda h, i: (0, h))
    w = nh * HG_DK
    dq, df, dv, dlb = pl.pallas_call(
        body,
        out_shape=[jax.ShapeDtypeStruct((t, w), BF16)] * 3 + [jax.ShapeDtypeStruct((1, w), F32)],
        grid=(nh // hp, nblk),
        in_specs=[col(0), col(1), col(2), lbspec,
                  pl.BlockSpec((hp, nchunk, HG_DK, HG_DK), lambda h, i: (h, nblk - 1 - i, 0, 0)), ocol],
        out_specs=[ocol, ocol, ocol, lbspec],
        scratch_shapes=[pltpu.VMEM((hp, HG_DK, HG_DK), F32)],
        compiler_params=_cparams(("parallel", "arbitrary")),
        name="hgrn_bwd",
    )(zh, zh, zh, lb3.reshape(1, -1), states, d_o)
    return dq, df, dv, dlb.reshape(w)


NEG = -1e30
ATT_GW = ATT_HEADS * ATT_DH


def _att_scores(q, kp, kc, has_prev):
    scale = ATT_DH ** -0.5
    i = lax.broadcasted_iota(jnp.int32, (ATT_BLK, ATT_BLK), 0)
    j = lax.broadcasted_iota(jnp.int32, (ATT_BLK, ATT_BLK), 1)
    s_p = jnp.where(jnp.logical_and(j >= i, has_prev), _dot_nt(q, kp) * scale, NEG)
    s_c = jnp.where(j <= i, _dot_nt(q, kc) * scale, NEG)
    return s_p, s_c


def _att_views(arrs, d):
    return [a.reshape(d, -1, ATT_GW) for a in arrs]


def _att_unview(a, d):
    return a.reshape(-1, ATT_GW) if d == 1 else a


def _attn_fwd(qb, kb, vb, g):
    d = ATT_PATTERNS[g][1]
    q2, k2, v2 = _att_views([qb, kb, vb], d)
    nb = q2.shape[1] // ATT_BLK

    def body(q_ref, kc_ref, kp_ref, vc_ref, vp_ref, o_ref, l_ref):
        has_prev = pl.program_id(1) > 0
        for h in range(ATT_HEADS):
            hs = slice(h * ATT_DH, (h + 1) * ATT_DH)
            s_p, s_c = _att_scores(q_ref[:, hs], kp_ref[:, hs], kc_ref[:, hs], has_prev)
            m = jnp.maximum(jnp.max(s_p, axis=1, keepdims=True), jnp.max(s_c, axis=1, keepdims=True))
            p_p, p_c = jnp.exp(s_p - m), jnp.exp(s_c - m)
            l = jnp.sum(p_p, axis=1, keepdims=True) + jnp.sum(p_c, axis=1, keepdims=True)
            o = _dot(p_p.astype(BF16), vp_ref[:, hs]) + _dot(p_c.astype(BF16), vc_ref[:, hs])
            o_ref[:, hs] = o / l
            l_ref[:, hs] = jnp.broadcast_to(m + jnp.log(l), (ATT_BLK, ATT_DH))

    cur = pl.BlockSpec((None, ATT_BLK, ATT_GW), lambda r, n: (r, n, 0))
    prev = pl.BlockSpec((None, ATT_BLK, ATT_GW), lambda r, n: (r, jnp.maximum(n - 1, 0), 0))
    o, lse = pl.pallas_call(
        body,
        out_shape=[jax.ShapeDtypeStruct(q2.shape, F32)] * 2,
        grid=(d, nb),
        in_specs=[cur, cur, prev, cur, prev],
        out_specs=[cur, cur],
        compiler_params=_cparams(("parallel", "arbitrary")),
        name=f"attn_fwd_g{g}",
    )(q2, k2, k2, v2, v2)
    return _att_unview(o, d), _att_unview(lse, d)


def _attn_bwd(qb, kb, vb, o, lse, d_o, d_lse, g):
    d = ATT_PATTERNS[g][1]
    q2, k2, v2 = _att_views([qb, kb, vb], d)
    o2, l2, do2, dl2 = _att_views([o, lse, d_o, d_lse], d)
    nb = q2.shape[1] // ATT_BLK

    def body(q_ref, kc_ref, kp_ref, vc_ref, vp_ref, o_ref, l_ref, do_ref, dl_ref, dq_ref, dk_ref, dv_ref, ck, cv):
        n = pl.program_id(1)
        active = n < nb

        @pl.when(n == 0)
        def _():
            ck[...] = jnp.zeros_like(ck)
            cv[...] = jnp.zeros_like(cv)

        @pl.when(jnp.logical_not(active))
        def _():
            dk_ref[...] = ck[...]
            dv_ref[...] = cv[...]

        @pl.when(active)
        def _():
            has_prev = n > 0
            for h in range(ATT_HEADS):
                hs = slice(h * ATT_DH, (h + 1) * ATT_DH)
                q, kp, kc, vp, vc = q_ref[:, hs], kp_ref[:, hs], kc_ref[:, hs], vp_ref[:, hs], vc_ref[:, hs]
                s_p, s_c = _att_scores(q, kp, kc, has_prev)
                lse_h = l_ref[:, hs][:, 0:1]
                p_p, p_c = jnp.exp(s_p - lse_h), jnp.exp(s_c - lse_h)
                do = do_ref[:, hs]
                delta = jnp.sum(do * o_ref[:, hs] - dl_ref[:, hs], axis=1, keepdims=True)
                dob = do.astype(BF16)
                scale = ATT_DH ** -0.5
                ds_p = (p_p * (_dot_nt(dob, vp) - delta) * scale).astype(BF16)
                ds_c = (p_c * (_dot_nt(dob, vc) - delta) * scale).astype(BF16)
                dq_ref[:, hs] = _dot(ds_p, kp) + _dot(ds_c, kc)
                dk_ref[:, hs] = ck[:, hs] + _dot_tn(ds_p, q)
                dv_ref[:, hs] = cv[:, hs] + _dot_tn(p_p.astype(BF16), dob)
                ck[:, hs] = _dot_tn(ds_c, q)
                cv[:, hs] = _dot_tn(p_c.astype(BF16), dob)

    def qn(n):
        return jnp.minimum(n, nb - 1)

    cur = pl.BlockSpec((None, ATT_BLK, ATT_GW), lambda r, n: (r, qn(n), 0))
    prev = pl.BlockSpec((None, ATT_BLK, ATT_GW), lambda r, n: (r, jnp.maximum(qn(n) - 1, 0), 0))
    behind = pl.BlockSpec((None, ATT_BLK, ATT_GW), lambda r, n: (r, jnp.maximum(n - 1, 0), 0))
    shp = jax.ShapeDtypeStruct(q2.shape, F32)
    dq, dk, dv = pl.pallas_call(
        body,
        out_shape=[shp, shp, shp],
        grid=(d, nb + 1),
        in_specs=[cur, cur, prev, cur, prev, cur, cur, cur, cur],
        out_specs=[cur, behind, behind],
        scratch_shapes=[pltpu.VMEM((ATT_BLK, ATT_GW), F32), pltpu.VMEM((ATT_BLK, ATT_GW), F32)],
        compiler_params=_cparams(("parallel", "arbitrary")),
        name=f"attn_bwd_g{g}",
    )(q2, k2, k2, v2, v2, o2, l2, do2, dl2)
    return _att_unview(dq, d), _att_unview(dk, d), _att_unview(dv, d)


def _rms_parts(x, width):
    outs = []
    for lo in range(0, x.shape[1], width):
        xs = x[:, lo:lo + width]
        r = lax.rsqrt(jnp.mean(xs * xs, axis=1, keepdims=True) + EPS)
        outs.append((xs * r, r))
    return outs


def _rms_bwd_part(xh, r, dxh):
    return r * (dxh - xh * jnp.mean(dxh * xh, axis=1, keepdims=True))


def _norm_fwd(x, gain):
    d = x.shape[1]

    def fn(ins, consts):
        (xh, _), = _rms_parts(ins[0], d)
        return [xh * consts[0]], []

    (h,), _ = _rowwise(fn, [(x, d, 0)], [gain.reshape(1, d)], [(d, BF16)], [], bm=512, name="norm_fwd")
    return h


def _norm_bwd(x, gain, dh, dres):
    d = x.shape[1]

    def fn(ins, consts):
        (xh, r), = _rms_parts(ins[0], d)
        dx = ins[2] + _rms_bwd_part(xh, r, ins[1] * consts[0])
        return [dx], [_colsum8(ins[1] * xh)]

    (dx,), (dg,) = _rowwise(fn, [(x, d, 0), (dh, d, 0), (dres, d, 0)], [gain.reshape(1, d)], [(d, F32)], [d],
                            bm=512, name="norm_bwd")
    return dx, dg


def _rot_sign():
    lane = lax.broadcasted_iota(jnp.int32, (1, ATT_DH), 1)
    return jnp.where(lane < ATT_DH // 2, -1.0, 1.0).astype(F32)


def _rope(y, cos, sin):
    return y * cos + pltpu.roll(y, ATT_DH // 2, axis=1) * _rot_sign() * sin


def _rope_t(dy, cos, sin):
    return dy * cos - pltpu.roll(dy * sin, ATT_DH // 2, axis=1) * _rot_sign()


def _qk_prep(zq, zk, zv, qn, kn, cos, sin):
    w = zq.shape[1]

    def fn(ins, consts):
        cs, sn = ins[3], ins[4]
        outs = []
        for z, gain in ((ins[0], consts[0]), (ins[1], consts[1])):
            for i, (xh, _) in enumerate(_rms_parts(z, ATT_DH)):
                outs.append(_rope(xh * gain[:, i * ATT_DH:(i + 1) * ATT_DH], cs, sn))
        outs += [ins[2][:, i * ATT_DH:(i + 1) * ATT_DH] for i in range(w // ATT_DH)]
        groups = [jnp.concatenate(outs[i:i + ATT_HEADS], axis=1) for i in range(0, len(outs), ATT_HEADS)]
        return groups, []

    outs, _ = _rowwise(fn, [(zq, w, 0), (zk, w, 0), (zv, w, 0), (cos, ATT_DH, 0), (sin, ATT_DH, 0)], [qn, kn],
                       [(ATT_GW, BF16, ATT_PATTERNS[g][1]) for g in range(ATT_GROUPS)] * 3, [], bm=256, name="qk_prep")
    return outs[0:3], outs[3:6], outs[6:9]


def _qk_prep_bwd(zq, zk, dq_g, dk_g, dv_g, qn, kn, cos, sin):
    w = zq.shape[1]

    def fn(ins, consts):
        cs, sn = ins[2], ins[3]
        outs, sums = [], []
        for z, gain, dparts in ((ins[0], consts[0], ins[4:7]), (ins[1], consts[1], ins[7:10])):
            dout = jnp.concatenate(dparts, axis=1)
            dz, dgain = [], []
            for i, (xh, r) in enumerate(_rms_parts(z, ATT_DH)):
                hs = slice(i * ATT_DH, (i + 1) * ATT_DH)
                dy = _rope_t(dout[:, hs], cs, sn)
                dgain.append(_colsum8(dy * xh))
                dz.append(_rms_bwd_part(xh, r, dy * gain[:, hs]))
            outs.append(jnp.concatenate(dz, axis=1))
            sums.append(jnp.concatenate(dgain, axis=1))
        outs.append(jnp.concatenate(ins[10:13], axis=1))
        return outs, sums

    ins = [(zq, w, 0), (zk, w, 0), (cos, ATT_DH, 0), (sin, ATT_DH, 0)]
    for parts in (dq_g, dk_g, dv_g):
        ins += [(a, ATT_GW, 0, ATT_PATTERNS[g][1]) for g, a in enumerate(parts)]
    (dzq, dzk, dzv), (dqn, dkn) = _rowwise(fn, ins, [qn, kn], [(w, BF16)] * 3, [w, w], bm=256, name="qk_prep_bwd")
    return dzq, dzk, dzv, dqn, dkn


def _post_a(o_raw, zh, gout):
    w = o_raw.shape[1]

    def fn(ins, consts):
        oh = jnp.concatenate([xh for xh, _ in _rms_parts(ins[0], HG_DK)], axis=1)
        hg = ins[1]
        return [oh * consts[0] * (hg * _sigmoid(hg))], []

    (y,), _ = _rowwise(fn, [(o_raw, w, 0), (zh, w, 3)], [gout.reshape(1, w)], [(w, BF16)], [], bm=512, name="post_a")
    return y


def _post_a_bwd(o_raw, zh, gout, dy):
    w = o_raw.shape[1]

    def fn(ins, consts):
        parts = _rms_parts(ins[0], HG_DK)
        oh = jnp.concatenate([xh for xh, _ in parts], axis=1)
        hg, dyv, gain = ins[1], ins[2], consts[0]
        sg = _sigmoid(hg)
        s = hg * sg
        doh = dyv * gain * s
        do = jnp.concatenate([_rms_bwd_part(xh, r, doh[:, i * HG_DK:(i + 1) * HG_DK]) for i, (xh, r) in enumerate(parts)], axis=1)
        dhg = dyv * oh * gain * (sg * (1.0 + hg * (1.0 - sg)))
        return [do, dhg], [_colsum8(dyv * oh * s)]

    (do, dhg), (dgain,) = _rowwise(fn, [(o_raw, w, 0), (zh, w, 3), (dy, w, 0)], [gout.reshape(1, w)],
                                   [(w, F32), (w, BF16)], [w], bm=512, name="post_a_bwd")
    return do, dhg, dgain


def _merge_alpha(lses):
    m = jnp.maximum(jnp.maximum(lses[0], lses[1]), lses[2])
    e = [jnp.exp(l - m) for l in lses]
    inv = 1.0 / (e[0] + e[1] + e[2])
    return [x * inv for x in e]


def _group_ins(parts):
    return [(a, ATT_GW, 0, ATT_PATTERNS[g][1]) for g, a in enumerate(parts)]


def _merge_b(o_g, lse_g):
    def fn(ins, consts):
        al = _merge_alpha(ins[3:6])
        return [al[0] * ins[0] + al[1] * ins[1] + al[2] * ins[2]], []

    (y,), _ = _rowwise(fn, _group_ins(o_g) + _group_ins(lse_g), [], [(ATT_GW, BF16)], [], bm=512, name="merge_b")
    return y


def _merge_b_bwd(o_g, lse_g, dy):
    def fn(ins, consts):
        al = _merge_alpha(ins[3:6])
        dyv = ins[6]
        dal = [dyv * ins[i] for i in range(3)]
        tot = al[0] * dal[0] + al[1] * dal[1] + al[2] * dal[2]
        return [al[i] * dyv for i in range(3)] + [al[i] * (dal[i] - tot) for i in range(3)], []

    outs, _ = _rowwise(fn, _group_ins(o_g) + _group_ins(lse_g) + [(dy, ATT_GW, 0)], [],
                       [(ATT_GW, F32, ATT_PATTERNS[g][1]) for g in range(ATT_GROUPS)] * 2, [], bm=512, name="merge_b_bwd")
    return outs[:3], outs[3:]


def _loss_head(y, target):
    d = y.shape[1]

    def fn(ins, consts):
        e = ins[0] - ins[1]
        return [e * (1.0 / d)], [_colsum8(e * e)]

    (dy,), (sq,) = _rowwise(fn, [(y, d, 0), (target, d, 0)], [], [(d, F32)], [d], bm=512, name="loss_head")
    return 0.5 * jnp.sum(sq) / d, dy


def _silu_grad(a):
    s = _sigmoid(a)
    return s * (1.0 + a * (1.0 - s))


def _ffn_fwd(x, gain, wt, wo_fn, tag):
    t, d = x.shape
    f = wt.shape[0] // 2
    h = _norm_fwd(x, gain)

    def act(accs, ex):
        a, b = accs
        s = _sigmoid(a)
        sa = a * s
        return (sa * b, b, 0.5 * sa, 0.5 * (s + sa * (1.0 - s)))

    bn = FFN_BN if f % FFN_BN == 0 else 256
    u, b, sa, sp = _mm([h], [wt, wt], [(0, 0, 0), (0, 1, 1)], 2, act, [BF16] * 4, m=t, n=f, k=d, tb=True,
                       bm=512, bn=bn, bk=d, b_off=[(0, 0), (f // min(bn, f), 0)], n_outer=True, name=f"ffn_in_{tag}")
    wo = wo_fn(u)
    (y,) = _mm([u], [wo], [(0, 0, 0)], 1, lambda accs, ex: (ex[0] + 0.5 * accs[0],), [F32], m=t, n=d, k=f,
               bm=512, bn=d, bk=f, extras=[x], name=f"ffn_out_{tag}")
    return y, (x, h, u, b, sa, sp, wo)


def _ffn_bwd(dy, saved, gain, wt, tag, tok):
    x, h, u, b, sa, sp, wo = saved
    t, d = x.shape
    f = wo.shape[0]
    dyb = (dy + tok).astype(BF16)

    def dact(accs, ex):
        bv, sav, spv = (e.astype(F32) for e in ex)
        return (accs[0] * bv * spv, accs[0] * sav)

    bn = FFN_BN if f % FFN_BN == 0 else 256
    da, db = _mm([dyb], [wo], [(0, 0, 0)], 1, dact, [BF16, BF16], m=t, n=f, k=d, tb=True, bm=512, bn=bn, bk=d,
                 extras=[b, sa, sp], n_outer=True, name=f"ffn_dact_{tag}")
    (dwo,) = _mm([u], [dyb], [(0, 0, 0)], 1, lambda accs, ex: (0.5 * accs[0],), [BF16], m=f, n=d, k=t, ta=True,
                 bm=1408, bn=d, bk=1024, name=f"ffn_dwo_{tag}")
    (dh,) = _mm([da, db], [wt, wt], [(0, 0, 0), (1, 1, 0)], 1, _first, [F32], m=t, n=d, k=f, bm=512, bn=d, bk=f,
                b_off=[(0, 0), (0, 1)], name=f"ffn_dh_{tag}")
    dwt = [_mm([g], [h], [(0, 0, 0)], 1, _first, [BF16], m=f, n=d, k=t, ta=True, bm=1408, bn=d, bk=1024,
               name=f"ffn_dwt{i}_{tag}")[0] for i, g in enumerate((da, db))]
    dx, dgain = _norm_bwd(x, gain, dh, dy)
    return dx, dgain, jnp.concatenate(dwt, axis=0), dwo


FFN_BN = 1408
Z_SPLITS = (("h", 4096), ("q", 1536), ("k", 1536), ("v", 1536), ("g", 2048))


def _mix_fwd(x, p, cos, sin):
    t, d = x.shape
    hm = _norm_fwd(x, p["gm"])
    z, off = {}, 0
    for nm, width in Z_SPLITS:
        bn = 1024 if off % 1024 == 0 and width % 1024 == 0 else 512
        (z[nm],) = _mm([hm], [p["wint"]], [(0, 0, 0)], 1, _first, [F32], m=t, n=width, k=d, tb=True, bm=1024, bn=bn, bk=d,
                       b_off=[(off // bn, 0)], name=f"mix_in_{nm}")
        off += width
    o_raw, states = _hgrn_fwd(z["h"], p["lb3"])
    qb, kb, vb = _qk_prep(z["q"], z["k"], z["v"], p["qn"], p["kn"], cos, sin)
    o_g, lse_g = zip(*[_attn_fwd(qb[g], kb[g], vb[g], g) for g in range(ATT_GROUPS)])
    oa = _post_a(o_raw, z["h"], p["gout"])
    ob = _merge_b(o_g, lse_g)
    late = p["late"](ob)
    p = dict(p, **late)
    (ya,) = _mm([oa], [p["wa"]], [(0, 0, 0)], 1, _first, [F32], m=t, n=d, k=oa.shape[1], bm=1024, bn=d, bk=oa.shape[1],
                name="branch_a")

    def gate(accs, ex):
        return (_sigmoid(ex[0]) * ex[2] + _sigmoid(ex[1]) * accs[0], accs[0])

    merged, yb = _mm([ob], [p["wbt"]], [(0, 0, 0)], 1, gate, [BF16, F32], m=t, n=d, k=ATT_GW, tb=True, bm=512, bn=d,
                     bk=ATT_GW, extras=[z["g"], z["g"], ya], e_off=[0, 1, 0], name="branch_b_gate")
    (y,) = _mm([merged], [p["wo"]], [(0, 0, 0)], 1, lambda accs, ex: (ex[0] + accs[0],), [F32], m=t, n=d, k=d,
               bm=1024, bn=d, bk=d, extras=[x], name="mix_out")
    return y, (x, hm, z, o_raw, states, qb, kb, vb, o_g, lse_g, oa, ob, ya, yb, merged, late)


def _mix_bwd(dy, saved, p, cos, sin, tok):
    x, hm, z, o_raw, states, qb, kb, vb, o_g, lse_g, oa, ob, ya, yb, merged, late = saved
    p = dict(p, **late)
    t, d = x.shape
    w = oa.shape[1]
    dyb = (dy + tok).astype(BF16)

    def dgate(accs, ex):
        dm = accs[0]
        sa, sb = _sigmoid(ex[0]), _sigmoid(ex[1])
        return (sa * dm, sb * dm, dm * ex[2] * sa * (1.0 - sa), dm * ex[3] * sb * (1.0 - sb))

    dya, dyb_, dga, dgb = _mm([dyb], [p["wo"]], [(0, 0, 0)], 1, dgate, [BF16] * 4, m=t, n=d, k=d, tb=True, bm=512, bn=d,
                              bk=d, extras=[z["g"], z["g"], ya, yb], e_off=[0, 1, 0, 0], name="mix_out_bwd")
    (dwo,) = _mm([merged], [dyb], [(0, 0, 0)], 1, _first, [BF16], m=d, n=d, k=t, ta=True, bm=d, bn=d, bk=1024, name="mix_dwo")
    (doa,) = _mm([dya], [p["wa"]], [(0, 0, 0)], 1, _first, [F32], m=t, n=w, k=d, tb=True, bm=1024, bn=w, bk=d, name="branch_a_bwd")
    (dwa,) = _mm([oa], [dya], [(0, 0, 0)], 1, _first, [BF16], m=w, n=d, k=t, ta=True, bm=w, bn=d, bk=1024, name="branch_a_dw")
    (dob,) = _mm([dyb_], [p["wbt"]], [(0, 0, 0)], 1, _first, [F32], m=t, n=ATT_GW, k=d, bm=1024, bn=ATT_GW, bk=d,
                 name="branch_b_bwd")
    (dwbt,) = _mm([dyb_], [ob], [(0, 0, 0)], 1, _first, [BF16], m=d, n=ATT_GW, k=t, ta=True, bm=d, bn=ATT_GW, bk=1024,
                  name="branch_b_dw")
    do_raw, dhg, dgout = _post_a_bwd(o_raw, z["h"], p["gout"], doa)
    do_g, dlse_g = _merge_b_bwd(o_g, lse_g, dob)
    dq_g, dk_g, dv_g = zip(*[_attn_bwd(qb[g], kb[g], vb[g], o_g[g], lse_g[g], do_g[g], dlse_g[g], g)
                             for g in range(ATT_GROUPS)])
    dzq, dzk, dzv, dqn, dkn = _qk_prep_bwd(z["q"], z["k"], dq_g, dk_g, dv_g, p["qn"], p["kn"], cos, sin)
    dhq, dhf, dhi, lbsum = _hgrn_bwd(z["h"], p["lb3"], states, do_raw)
    dz = jnp.concatenate([dhq, dhf, dhi, dhg, dzq, dzk, dzv, dga, dgb], axis=1)
    pw = dz.shape[1]
    (dhm,) = _mm([dz], [p["wint"]], [(0, 0, 0)], 1, _first, [F32], m=t, n=d, k=pw, bm=1024, bn=d, bk=1536, name="mix_in_bwd")
    (dwint,) = _mm([dz], [hm], [(0, 0, 0)], 1, _first, [BF16], m=pw, n=d, k=t, ta=True, bm=1536, bn=d, bk=1024, name="mix_in_dw")
    dx, dgm = _norm_bwd(x, p["gm"], dhm, dy)
    return dx, dict(gm=dgm, wint=dwint, lbsum=lbsum, gout=dgout, qn=dqn, kn=dkn, wa=dwa, wbt=dwbt, wo=dwo)


def _rope_tables(t):
    pos = jnp.arange(t, dtype=F32)
    inv = ROPE_THETA ** (-jnp.arange(0, ATT_DH, 2, dtype=F32) / ATT_DH)
    ang = pos[:, None] * inv[None, :]
    ang = jnp.concatenate([ang, ang], axis=-1)
    return jnp.cos(ang), jnp.sin(ang)


def _lower_bounds(logits):
    lb = jnp.cumsum(jax.nn.softmax(logits, axis=0), axis=0)
    return lb - lb[0:1]


def _head_gain(g):
    return jnp.tile(g[:, None, :], (1, ATT_HEADS, 1)).reshape(1, ATT_GROUPS * ATT_GW)


SMALL_GRADS = ("ffn1_norm", "mix_norm", "lbsum", "hgrn_out_norm", "attn_q_norm", "attn_k_norm", "ffn2_norm")


def _local_step(x, target, small, fetch, emit):
    t = x.shape[0]
    depth = small["ffn1_norm"].shape[0]
    cos, sin = _rope_tables(t)
    lb_all = _lower_bounds(small["hgrn_lb_logits"])
    saved = []
    for l in range(depth):
        w1t = fetch("w1t", l, x)["w1t"]
        x, s1 = _ffn_fwd(x, small["ffn1_norm"][l], w1t, lambda after, l=l: fetch("w1o", l, after)["w1o"], "1")
        p = dict(gm=small["mix_norm"][l], wint=fetch("wint", l, x)["wint"], lb3=lb_all[l].reshape(-1, 1, HG_DK),
                 gout=small["hgrn_out_norm"][l], qn=_head_gain(small["attn_q_norm"][l]),
                 kn=_head_gain(small["attn_k_norm"][l]), late=functools.partial(fetch, "mout", l))
        x, sm = _mix_fwd(x, p, cos, sin)
        w2t = fetch("w2t", l, x)["w2t"]
        x, s2 = _ffn_fwd(x, small["ffn2_norm"][l], w2t, lambda after, l=l: fetch("w2o", l, after)["w2o"], "2")
        saved.append((p, w1t, w2t, s1, sm, s2))
    loss, dx = _loss_head(x, target)
    gsmall = {k: [None] * depth for k in SMALL_GRADS}
    tok = jnp.zeros((), F32)
    for l in reversed(range(depth)):
        p, w1t, w2t, s1, sm, s2 = saved[l]
        dx, gsmall["ffn2_norm"][l], dw2t, dw2o = _ffn_bwd(dx, s2, small["ffn2_norm"][l], w2t, "2", tok)
        tok = emit("ffn2", l, dict(w2t=dw2t, w2o=dw2o), None)
        dx, gm = _mix_bwd(dx, sm, p, cos, sin, tok)
        tok = emit("mix", l, {k: gm[k] for k in ("wint", "wa", "wbt", "wo")}, None)
        gsmall["mix_norm"][l], gsmall["lbsum"][l], gsmall["hgrn_out_norm"][l] = gm["gm"], gm["lbsum"], gm["gout"]
        for k, src in (("attn_q_norm", "qn"), ("attn_k_norm", "kn")):
            gsmall[k][l] = jnp.sum(gm[src].reshape(ATT_GROUPS, ATT_HEADS, ATT_DH), axis=1)
        dx, gsmall["ffn1_norm"][l], dw1t, dw1o = _ffn_bwd(dx, s1, small["ffn1_norm"][l], w1t, "1", tok)
        final = ({k: jnp.stack(v) for k, v in gsmall.items()}, loss) if l == 0 else None
        tok = emit("ffn1", l, dict(w1t=dw1t, w1o=dw1o), final)
    return dx


_HBM = pl.BlockSpec(memory_space=pltpu.HBM)
_SEM = pl.BlockSpec(memory_space=pltpu.SEMAPHORE)
_EFFECT = pltpu.SideEffectType.DATAFLOW_SIDE_EFFECTING


def _peer(p):
    x, y, c = lax.axis_index("x"), lax.axis_index("y"), lax.axis_index("c")
    me = 4 * x + 2 * y + c
    return (1 - x if p & 4 else x, 1 - y if p & 2 else y, 1 - c if p & 1 else c), jnp.bitwise_xor(me, p), me


def _xchg_copy(src, land, mode, send_sems, recv_sems, k, p, arriving):
    peer, peer_id, me = _peer(p)
    block = src if mode == "gather" else src.at[peer_id]
    return pltpu.make_async_remote_copy(
        src_ref=block, dst_ref=land.at[peer_id if arriving else me], send_sem=send_sems.at[k * (N_DEV - 1) + p - 1],
        recv_sem=recv_sems.at[k * (N_DEV - 1) + p - 1], device_id=peer, device_id_type=MESH)


def _xchg_start(srcs, modes, groups, name):
    n, ng = len(srcs), len(groups)

    def body(*refs):
        src = refs[:n]
        sems = refs[n:n + 2 * ng]
        land = refs[n + 2 * ng + n:n + 2 * ng + 2 * n]
        token = refs[n + 2 * ng + 2 * n]
        for gi, idx in enumerate(groups):
            for ki, k in enumerate(idx):
                for p in range(1, N_DEV):
                    _xchg_copy(src[k], land[k], modes[k], sems[2 * gi], sems[2 * gi + 1], ki, p, False).start()
        token[...] = jnp.zeros_like(token)

    sem_shapes = []
    for idx in groups:
        sem_shapes += [pltpu.SemaphoreType.DMA((len(idx) * (N_DEV - 1),))] * 2
    outs = pl.pallas_call(
        body,
        out_shape=sem_shapes + [pltpu.HBM(a.shape, a.dtype) for a in srcs]
        + [pltpu.HBM((N_DEV,) + a.shape[-2:], a.dtype) for a in srcs] + [jax.ShapeDtypeStruct((8, 128), F32)],
        in_specs=[_HBM] * n,
        out_specs=[_SEM] * (2 * ng) + [_HBM] * (2 * n) + [pl.BlockSpec(memory_space=pltpu.VMEM)],
        input_output_aliases={i: 2 * ng + i for i in range(n)},
        compiler_params=pltpu.CompilerParams(has_side_effects=_EFFECT),
        name=name,
    )(*[pltpu.with_memory_space_constraint(a, pltpu.HBM) for a in srcs])
    sems = [(outs[2 * gi], outs[2 * gi + 1]) for gi in range(ng)]
    return sems, outs[2 * ng:2 * ng + n], outs[2 * ng + n:2 * ng + 2 * n], outs[-1]


def _xchg_wait_call(srcs, lands, modes, sems, after, name):
    n = len(srcs)

    def body(*refs):
        src, land = refs[:n], refs[n:2 * n]
        send_sems, recv_sems = refs[2 * n], refs[2 * n + 1]
        for p in range(1, N_DEV):
            for k in range(n):
                cp = _xchg_copy(src[k], land[k], modes[k], send_sems, recv_sems, k, p, True)
                cp.wait_send()
                cp.wait_recv()

    outs = pl.pallas_call(
        body,
        out_shape=[pltpu.HBM(a.shape, a.dtype) for a in list(srcs) + list(lands)],
        in_specs=[_HBM] * (2 * n) + [_SEM, _SEM, pl.BlockSpec(memory_space=pl.ANY)],
        out_specs=[_HBM] * (2 * n),
        input_output_aliases={i: i for i in range(2 * n)},
        compiler_params=pltpu.CompilerParams(has_side_effects=_EFFECT),
        name=name,
    )(*srcs, *lands, sems[0], sems[1], after)
    return outs[:n], outs[n:]


def _xchg_wait(srcs, lands, modes, sems, after, name):
    srcs, lands = _xchg_wait_call(srcs, lands, modes, sems, after, name)
    me = 4 * lax.axis_index("x") + 2 * lax.axis_index("y") + lax.axis_index("c")
    done = []
    for a, land, mode in zip(srcs, lands, modes):
        own = a[None] if mode == "gather" else lax.dynamic_slice_in_dim(a, me, 1, axis=0)
        done.append(lax.dynamic_update_slice(land, own, (me, 0, 0)))
    return done


def _sum_slots(land):
    g, _, r, c = land.shape
    br = r // 2 if (r % 32 == 0 and r >= 256) else r

    def body(l_ref, o_ref):
        acc = l_ref[0, 0].astype(F32)
        for j in range(1, N_DEV):
            acc = acc + l_ref[0, j].astype(F32)
        o_ref[0] = acc

    return pl.pallas_call(
        body,
        out_shape=jax.ShapeDtypeStruct((g, r, c), F32),
        grid=(g, r // br),
        in_specs=[pl.BlockSpec((1, N_DEV, br, c), lambda i, j: (i, 0, j, 0))],
        out_specs=pl.BlockSpec((1, br, c), lambda i, j: (i, j, 0)),
        compiler_params=_cparams(("parallel", "parallel")),
        name="sum_slots",
    )(land)


def _adamw(w, g, m, v):
    shape = w.shape
    cols = shape[-1]
    rows = int(np.prod(shape[:-1]))
    bm = max(b for b in range(8, 257, 8) if rows % b == 0) if rows % 8 == 0 else rows
    c1 = 1.0 - ADAM_B1 ** ADAM_STEP
    c2 = 1.0 - ADAM_B2 ** ADAM_STEP

    def fn(ins, consts):
        wv, gv, mv, vv = ins
        m2 = ADAM_B1 * mv + (1.0 - ADAM_B1) * gv
        v2 = ADAM_B2 * vv + (1.0 - ADAM_B2) * (gv * gv)
        delta = -ADAM_LR * ((m2 / c1) / (jnp.sqrt(v2 / c2) + ADAM_EPS) + ADAM_WD * wv)
        return [delta, m2, v2], []

    outs, _ = _rowwise(fn, [(a.reshape(rows, cols), cols, 0) for a in (w, g, m, v)], [], [(cols, F32)] * 3, [],
                       bm=bm, name="adamw")
    return [o.reshape(shape) for o in outs]


BIG = ("w1t", "w1o", "wint", "wa", "wbt", "wo", "w2t", "w2o")
FETCH_GROUPS = dict(w1t=("w1t",), w1o=("w1o",), wint=("wint",), mout=("wa", "wbt", "wo"), w2t=("w2t",), w2o=("w2o",))
SMALL_ROWS = (("ffn1_norm", 0), ("mix_norm", 2), ("lbsum", 4), ("hgrn_out_norm", 6), ("ffn2_norm", 8),
              ("attn_q_norm", 10), ("attn_k_norm", 12))
SMALL_PACK_ROWS = 16


def kernel(x, ffn1_norm, ffn1_w_in, ffn1_w_out, mix_norm, w_in, hgrn_lb_logits, hgrn_out_norm, attn_q_norm, attn_k_norm, w_branch_a, w_branch_b, w_out, ffn2_norm, ffn2_w_in, ffn2_w_out, loss_target, m_ffn1_norm, m_ffn1_w_in, m_ffn1_w_out, m_mix_norm, m_w_in, m_hgrn_lb_logits, m_hgrn_out_norm, m_attn_q_norm, m_attn_k_norm, m_w_branch_a, m_w_branch_b, m_w_out, m_ffn2_norm, m_ffn2_w_in, m_ffn2_w_out, v_ffn1_norm, v_ffn1_w_in, v_ffn1_w_out, v_mix_norm, v_w_in, v_hgrn_lb_logits, v_hgrn_out_norm, v_attn_q_norm, v_attn_k_norm, v_w_branch_a, v_w_branch_b, v_w_out, v_ffn2_norm, v_ffn2_w_in, v_ffn2_w_out):
    names = ("ffn1_norm", "ffn1_w_in", "ffn1_w_out", "mix_norm", "w_in", "hgrn_lb_logits", "hgrn_out_norm", "attn_q_norm",
             "attn_k_norm", "w_branch_a", "w_branch_b", "w_out", "ffn2_norm", "ffn2_w_in", "ffn2_w_out")
    w = dict(zip(names, (ffn1_norm, ffn1_w_in, ffn1_w_out, mix_norm, w_in, hgrn_lb_logits, hgrn_out_norm, attn_q_norm,
                         attn_k_norm, w_branch_a, w_branch_b, w_out, ffn2_norm, ffn2_w_in, ffn2_w_out)))
    m = dict(zip(names, (m_ffn1_norm, m_ffn1_w_in, m_ffn1_w_out, m_mix_norm, m_w_in, m_hgrn_lb_logits, m_hgrn_out_norm,
                         m_attn_q_norm, m_attn_k_norm, m_w_branch_a, m_w_branch_b, m_w_out, m_ffn2_norm, m_ffn2_w_in, m_ffn2_w_out)))
    v = dict(zip(names, (v_ffn1_norm, v_ffn1_w_in, v_ffn1_w_out, v_mix_norm, v_w_in, v_hgrn_lb_logits, v_hgrn_out_norm,
                         v_attn_q_norm, v_attn_k_norm, v_w_branch_a, v_w_branch_b, v_w_out, v_ffn2_norm, v_ffn2_w_in, v_ffn2_w_out)))
    depth, d = ffn1_norm.shape

    def tr(a):
        return jnp.swapaxes(a, 1, 2)

    shard = dict(w1t=tr(ffn1_w_in), w1o=ffn1_w_out, wint=tr(w_in), wa=w_branch_a,
                 wbt=tr(w_branch_b).reshape(depth, -1, d), wo=w_out, w2t=tr(ffn2_w_in), w2o=ffn2_w_out)
    order = [(g, l) for l in range(depth) for g in FETCH_GROUPS]
    flat = [(g, l, k) for g, l in order for k in FETCH_GROUPS[g]]
    groups, pos = [], 0
    for g, l in order:
        groups.append(list(range(pos, pos + len(FETCH_GROUPS[g]))))
        pos += len(FETCH_GROUPS[g])
    g_sems, g_srcs, g_lands, _ = _xchg_start([shard[k][l].astype(BF16) for _, l, k in flat], ["gather"] * len(flat),
                                             groups, "gather_start")

    def fetch(group, l, after):
        gi = order.index((group, l))
        idx = groups[gi]
        lands = _xchg_wait([g_srcs[i] for i in idx], [g_lands[i] for i in idx], ["gather"] * len(idx), g_sems[gi], after,
                           f"gather_wait_{group}{l}")
        out = {}
        for k, land in zip(FETCH_GROUPS[group], lands):
            out[k] = land.reshape(d, -1) if k == "wbt" else land.reshape(-1, d)
        return out

    pending = []

    def emit(group, l, g, final):
        keys = list(g)
        srcs = [g[k].reshape(N_DEV, -1, d) for k in keys]
        modes = ["scatter"] * len(keys)
        if final is not None:
            gsmall, loss = final
            pack = jnp.zeros((SMALL_PACK_ROWS, d), F32)
            for k, r0 in SMALL_ROWS:
                rows = gsmall[k].reshape(depth, -1)
                pack = pack.at[r0:r0 + depth, :rows.shape[1]].set(rows)
            srcs.append(pack.at[14, :].set(loss))
            modes.append("gather")
            keys.append("small")
        sems, s_thru, l_thru, token = _xchg_start(srcs, modes, [list(range(len(srcs)))], f"grads_start_{group}{l}")
        pending.append((group, l, keys, modes, sems[0], s_thru, l_thru))
        return token[0, 0]

    small = {k: w[k] for k in ("ffn1_norm", "mix_norm", "hgrn_lb_logits", "hgrn_out_norm", "attn_q_norm", "attn_k_norm", "ffn2_norm")}
    dx = _local_step(x[0], loss_target[0], small, fetch, emit)

    summed = {}
    for group, l, keys, modes, sems, s_thru, l_thru in pending:
        lands = _xchg_wait(s_thru, l_thru, modes, sems, dx, f"grads_wait_{group}{l}")
        for k, land in zip(keys, lands):
            summed[k, l] = _sum_slots(land[None])[0]
    gsum = {k: jnp.stack([summed[k, l] for l in range(depth)]) for k in BIG}
    tot = summed["small", 0]

    grads = {}
    for k, r0 in SMALL_ROWS:
        shp = (depth,) + (w[k].shape[1:] if k != "lbsum" else (d,))
        grads[k] = tot[r0:r0 + depth, :int(np.prod(shp[1:]))].reshape(shp)
    _, lb_vjp = jax.vjp(_lower_bounds, hgrn_lb_logits)
    grads["hgrn_lb_logits"] = lb_vjp(grads.pop("lbsum"))[0]
    grads["ffn1_w_in"], grads["ffn1_w_out"] = tr(gsum["w1t"]), gsum["w1o"]
    grads["w_in"], grads["w_branch_a"] = tr(gsum["wint"]), gsum["wa"]
    grads["w_branch_b"] = tr(gsum["wbt"].reshape(depth, d // N_DEV, -1))
    grads["w_out"] = gsum["wo"]
    grads["ffn2_w_in"], grads["ffn2_w_out"] = tr(gsum["w2t"]), gsum["w2o"]

    upd = {k: _adamw(w[k], grads[k], m[k], v[k]) for k in names}
    return (tot[14, 0], dx[None], *[grads[k] for k in names], *[upd[k][0] for k in names],
            *[upd[k][1] for k in names], *[upd[k][2] for k in names])
```

```python
import functools
import math

import jax
import jax.numpy as jnp
import numpy as np
from jax import lax
from jax.experimental import pallas as pl
from jax.experimental.pallas import tpu as pltpu

F32 = jnp.float32
BF16 = jnp.bfloat16

N_DEV = 8
EPS = 1e-6
HG_DK = 128
HG_CHUNK = 64
HG_SUB = 16
HG_HP = 4
ATT_PATTERNS = ((128, 1), (512, 4), (2048, 16))
ATT_GROUPS = 3
ATT_HEADS = 4
ATT_DH = 128
ATT_BLK = 128
ROPE_THETA = 10000.0
ADAM_LR, ADAM_B1, ADAM_B2, ADAM_EPS, ADAM_WD, ADAM_STEP = 0.001, 0.9, 0.999, 1e-08, 0.01, 10
VMEM_LIMIT_BYTES = 56 * 1024 * 1024
MESH = pl.DeviceIdType.MESH


def _cparams(sem, **kw):
    return pltpu.CompilerParams(dimension_semantics=sem, vmem_limit_bytes=VMEM_LIMIT_BYTES, **kw)


def _sigmoid(x):
    return 1.0 / (1.0 + jnp.exp(-x))


def _mm(a_list, b_list, pairs, n_acc, fin, out_dtypes, *, m, n, k, ta=False, tb=False, bm, bn, bk,
        b_off=None, extras=(), e_off=None, n_outer=False, name):
    bm, bn, bk = min(bm, m), min(bn, n), min(bk, k)
    assert m % bm == 0 and n % bn == 0 and k % bk == 0, (name, m, n, k, bm, bn, bk)
    nk = k // bk
    b_off = b_off or [(0, 0)] * len(b_list)
    e_off = e_off or [0] * len(extras)
    na, nb, ne = len(a_list), len(b_list), len(extras)
    dn = (((0,) if ta else (1,), (1,) if tb else (0,)), ((), ()))

    def body(*refs):
        a_refs, b_refs = refs[:na], refs[na:na + nb]
        e_refs = refs[na + nb:na + nb + ne]
        o_refs = refs[na + nb + ne:na + nb + ne + len(out_dtypes)]
        acc_refs = refs[na + nb + ne + len(out_dtypes):]
        kk = pl.program_id(2)
        parts = [None] * n_acc
        for ai, bi, ci in pairs:
            p = lax.dot_general(a_refs[ai][...], b_refs[bi][...], dn, preferred_element_type=F32)
            parts[ci] = p if parts[ci] is None else parts[ci] + p

        def finish(accs):
            outs = fin(accs, [e[...] for e in e_refs])
            for o_ref, o in zip(o_refs, outs):
                o_ref[...] = o.astype(o_ref.dtype)

        if nk == 1:
            finish(parts)
        else:
            @pl.when(kk == 0)
            def _():
                for c in range(n_acc):
                    acc_refs[c][...] = parts[c]

            @pl.when(kk > 0)
            def _():
                for c in range(n_acc):
                    acc_refs[c][...] += parts[c]

            @pl.when(kk == nk - 1)
            def _():
                finish([acc_refs[c][...] for c in range(n_acc)])

    def ij(f):
        return (lambda j, i, q: f(i, j, q)) if n_outer else f

    a_spec = pl.BlockSpec((bk, bm), ij(lambda i, j, q: (q, i))) if ta else pl.BlockSpec((bm, bk), ij(lambda i, j, q: (i, q)))

    def b_spec(off):
        on, ok = off
        if tb:
            return pl.BlockSpec((bn, bk), ij(lambda i, j, q: (j + on, q + ok)))
        return pl.BlockSpec((bk, bn), ij(lambda i, j, q: (q + ok, j + on)))

    mn_spec = pl.BlockSpec((bm, bn), ij(lambda i, j, q: (i, j)))
    outs = pl.pallas_call(
        body,
        out_shape=[jax.ShapeDtypeStruct((m, n), d) for d in out_dtypes],
        grid=(n // bn, m // bm, nk) if n_outer else (m // bm, n // bn, nk),
        in_specs=[a_spec] * na + [b_spec(o) for o in b_off]
        + [pl.BlockSpec((bm, bn), ij(lambda i, j, q, o=o: (i, j + o))) for o in e_off],
        out_specs=[mn_spec] * len(out_dtypes),
        scratch_shapes=[pltpu.VMEM((bm, bn), F32) for _ in range(n_acc if nk > 1 else 0)],
        compiler_params=_cparams(("parallel", "parallel", "arbitrary")),
        name=name,
    )(*a_list, *b_list, *extras)
    return outs


def _first(accs, ex):
    return (accs[0],)


def _rowwise(fn, ins, consts, out_defs, sum_widths, *, bm, name):
    ins = [tuple(e) + (1,) * (4 - len(e)) for e in ins]
    out_defs = [tuple(e) + (1,) * (3 - len(e)) for e in out_defs]
    t = ins[0][0].shape[-2] * ins[0][3]
    bm = min(bm, t)
    assert t % bm == 0, (name, t, bm)
    ni, nc, no, ns = len(ins), len(consts), len(out_defs), len(sum_widths)
    strided = [w for _, w, _, d in ins if d > 1] + [w for w, _, d in out_defs if d > 1]

    def body(*refs):
        i_refs, c_refs = refs[:ni], refs[ni:ni + nc]
        o_refs, s_refs = refs[ni + nc:ni + nc + no], refs[ni + nc + no:ni + nc + no + ns]
        scratch = list(refs[ni + nc + no + ns:])
        vals = []
        for ref, (_, w, _, d) in zip(i_refs, ins):
            if d == 1:
                vals.append(ref[...])
                continue
            s = scratch.pop(0)
            for r in range(d):
                for c in range(w // 128):
                    s.at[c][pl.ds(r, bm // d, stride=d), :] = ref[r, :, c * 128:(c + 1) * 128].astype(F32)
            vals.append(jnp.concatenate([s[c] for c in range(w // 128)], axis=1))
        outs, sums = fn(vals, [r[...] for r in c_refs])
        for o_ref, o, (w, _, d) in zip(o_refs, outs, out_defs):
            if d == 1:
                o_ref[...] = o.astype(o_ref.dtype)
                continue
            s = scratch.pop(0)
            for c in range(w // 128):
                s[c] = o[:, c * 128:(c + 1) * 128].astype(F32)
            for r in range(d):
                for c in range(w // 128):
                    o_ref[r, :, c * 128:(c + 1) * 128] = s.at[c][pl.ds(r, bm // d, stride=d), :].astype(o_ref.dtype)
        if ns:
            first = pl.program_id(0) == 0

            @pl.when(first)
            def _():
                for s_ref, s in zip(s_refs, sums):
                    s_ref[...] = s

            @pl.when(jnp.logical_not(first))
            def _():
                for s_ref, s in zip(s_refs, sums):
                    s_ref[...] += s

    def win(width, cb, d):
        if d > 1:
            return pl.BlockSpec((d, bm // d, width), lambda i: (0, i, 0))
        return pl.BlockSpec((bm, width), lambda i: (i, cb))

    res = pl.pallas_call(
        body,
        out_shape=[jax.ShapeDtypeStruct((t, w) if d == 1 else (d, t // d, w), dt) for w, dt, d in out_defs]
        + [jax.ShapeDtypeStruct((8, w), F32) for w in sum_widths],
        grid=(t // bm,),
        in_specs=[win(w, cb, d) for _, w, cb, d in ins] + [pl.BlockSpec(c.shape, lambda i, nd=c.ndim: (0,) * nd) for c in consts],
        out_specs=[win(w, 0, d) for w, _, d in out_defs] + [pl.BlockSpec((8, w), lambda i: (0, 0)) for w in sum_widths],
        scratch_shapes=[pltpu.VMEM((w // 128, bm, 128), F32) for w in strided],
        compiler_params=_cparams(("arbitrary",) if ns else ("parallel",)),
        name=name,
    )(*[e[0] for e in ins], *consts)
    return res[:no], [jnp.sum(s, axis=0) for s in res[no:]]


def _colsum8(x):
    bm, w = x.shape
    return jnp.sum(x.reshape(bm // 8, 8, w), axis=0)


def _tri(n, upper=False):
    r = lax.broadcasted_iota(jnp.int32, (n, n), 0)
    c = lax.broadcasted_iota(jnp.int32, (n, n), 1)
    return (c >= r) if upper else (c <= r)


def _exact_tri_matmul(tri_bf16, x):
    x0 = x.astype(BF16)
    r1 = x - x0.astype(F32)
    x1 = r1.astype(BF16)
    x2 = (r1 - x1.astype(F32)).astype(BF16)
    w = x.shape[1]
    y = jnp.dot(tri_bf16, jnp.concatenate([x0, x1, x2], axis=1), preferred_element_type=F32)
    return y[:, :w] + y[:, w:2 * w] + y[:, 2 * w:]


def _dot_nt(a, b):
    return lax.dot_general(a, b, (((1,), (1,)), ((), ())), preferred_element_type=F32)


def _dot_tn(a, b):
    return lax.dot_general(a, b, (((0,), (0,)), ((), ())), preferred_element_type=F32)


def _dot(a, b):
    return jnp.dot(a, b, preferred_element_type=F32)


def _hg_gates(hq, hf, lb):
    sq = _sigmoid(hq)
    q = hq * sq
    sg = _sigmoid(hf)
    f = lb + (1.0 - lb) * sg
    return q, sq, sg, f


def _hg_intra(q, kk, g):
    c = q.shape[0]
    rows = lax.broadcasted_iota(jnp.int32, (c, 1), 0)
    a_rows, qts, kts, eqs, eks = [], [], [], [], []
    for i in range(c // HG_SUB):
        lo = i * HG_SUB
        ref = g[lo - 1:lo, :] if i else jnp.zeros_like(g[0:1, :])
        eq = jnp.exp(g[lo:lo + HG_SUB, :] - ref)
        ek = jnp.exp(jnp.where(rows < lo + HG_SUB, ref - g, 0.0))
        qt = q[lo:lo + HG_SUB, :] * eq
        kt = kk * ek
        a = _dot_nt(qt.astype(BF16), kt.astype(BF16))
        tpos = lo + lax.broadcasted_iota(jnp.int32, (HG_SUB, c), 0)
        spos = lax.broadcasted_iota(jnp.int32, (HG_SUB, c), 1)
        a_rows.append(jnp.where(spos <= tpos, a, 0.0))
        qts.append(qt), kts.append(kt), eqs.append(eq), eks.append(ek)
    return jnp.concatenate(a_rows, axis=0), qts, kts, eqs, eks


def _hgrn_fwd_serial(zh, lb3, *, tb=512):
    t = zh.shape[0]
    nh = lb3.shape[0]
    c = HG_CHUNK
    tb = min(tb, t)
    nchunk = tb // c
    hp = HG_HP if nh % HG_HP == 0 else 1

    def body(hq_ref, hf_ref, hi_ref, lb_ref, o_ref, st_ref, state):
        @pl.when(pl.program_id(1) == 0)
        def _():
            state[...] = jnp.zeros_like(state)

        tril = _tri(c).astype(BF16)

        def one_head(hh, ci, sl):
            ls = slice(hh * HG_DK, (hh + 1) * HG_DK)
            q, _, _, f = _hg_gates(hq_ref[sl, ls], hf_ref[sl, ls], lb_ref[hh])
            v = hi_ref[sl, ls]
            kk = 1.0 - f
            g = _exact_tri_matmul(tril, jnp.log(f))
            a, _, _, _, _ = _hg_intra(q, kk, g)
            st = state[hh]
            st_ref[hh, ci] = st
            vb = v.astype(BF16)
            o = _dot(a.astype(BF16), vb) + _dot_nt((q * jnp.exp(g)).astype(BF16), st.astype(BF16))
            o_ref[sl, ls] = o
            glast = g[c - 1:c, :]
            kg = kk * jnp.exp(glast - g)
            state[hh] = st * jnp.exp(glast) + _dot_tn(vb, kg.astype(BF16))

        def chunk(ci, carry):
            sl = pl.ds(pl.multiple_of(ci * c, c), c)
            for hh in range(hp):
                one_head(hh, ci, sl)
            return carry

        lax.fori_loop(0, nchunk, chunk, 0)

    def col(cb):
        return pl.BlockSpec((tb, hp * HG_DK), lambda h, i: (i, cb * (nh // hp) + h))

    return pl.pallas_call(
        body,
        out_shape=[jax.ShapeDtypeStruct((t, nh * HG_DK), F32), jax.ShapeDtypeStruct((nh, t // c, HG_DK, HG_DK), F32)],
        grid=(nh // hp, t // tb),
        in_specs=[col(0), col(1), col(2), pl.BlockSpec((hp, 1, HG_DK), lambda h, i: (h, 0, 0))],
        out_specs=[pl.BlockSpec((tb, hp * HG_DK), lambda h, i: (i, h)),
                   pl.BlockSpec((hp, nchunk, HG_DK, HG_DK), lambda h, i: (h, i, 0, 0))],
        scratch_shapes=[pltpu.VMEM((hp, HG_DK, HG_DK), F32)],
        compiler_params=_cparams(("parallel", "arbitrary")),
        name="hgrn_fwd",
    )(zh, zh, zh, lb3)


def _hgrn_bwd_serial(zh, lb3, states, d_o, *, tb=512):
    t = zh.shape[0]
    nh = lb3.shape[0]
    c = HG_CHUNK
    tb = min(tb, t)
    nchunk = tb // c
    nblk = t // tb
    hp = HG_HP if nh % HG_HP == 0 else 1

    def body(hq_ref, hf_ref, hi_ref, lb_ref, st_ref, do_ref, dq_ref, df_ref, dv_ref, dlb_ref, dstate):
        @pl.when(pl.program_id(1) == 0)
        def _():
            dstate[...] = jnp.zeros_like(dstate)
            dlb_ref[...] = jnp.zeros_like(dlb_ref)

        tril = _tri(c).astype(BF16)
        triu = _tri(c, upper=True).astype(BF16)
        last_row = lax.broadcasted_iota(jnp.int32, (c, 1), 0) == c - 1

        def one_head(hh, ci, sl):
            ls = slice(hh * HG_DK, (hh + 1) * HG_DK)
            lb = lb_ref[hh]
            hq, hf = hq_ref[sl, ls], hf_ref[sl, ls]
            q, sq, sg, f = _hg_gates(hq, hf, lb)
            v = hi_ref[sl, ls]
            kk = 1.0 - f
            g = _exact_tri_matmul(tril, jnp.log(f))
            a, qts, kts, eqs, eks = _hg_intra(q, kk, g)
            st = st_ref[hh, ci]
            dst = dstate[hh]
            do = do_ref[sl, ls]
            dob, vb = do.astype(BF16), v.astype(BF16)
            glast = g[c - 1:c, :]
            eg = jnp.exp(g)
            egl = jnp.exp(glast - g)
            qg = q * eg
            kg = kk * egl
            dv = _dot_tn(a.astype(BF16), dob) + _dot_nt(kg.astype(BF16), dst.astype(BF16))
            da = jnp.where(_tri(c), _dot_nt(dob, vb), 0.0).astype(BF16)
            dq_parts, dgq_parts = [], []
            dk = jnp.zeros_like(kk)
            dgk = jnp.zeros_like(kk)
            for i in range(c // HG_SUB):
                da_i = da[i * HG_SUB:(i + 1) * HG_SUB, :]
                ktb, qtb = kts[i].astype(BF16), qts[i].astype(BF16)
                xi = _dot(da_i, ktb)
                yi = _dot_tn(da_i, qtb)
                dq_parts.append(xi * eqs[i])
                dk = dk + yi * eks[i]
                dgq_parts.append(xi * qtb.astype(F32))
                dgk = dgk + yi * ktb.astype(F32)
            dq_inter = _dot(dob, st.astype(BF16)) * eg
            dq = jnp.concatenate(dq_parts, axis=0) + dq_inter
            dk_state = _dot(vb, dst.astype(BF16)) * egl
            dk = dk + dk_state
            dg = jnp.concatenate(dgq_parts, axis=0) - dgk + q * dq_inter - kk * dk_state
            dgl = jnp.sum(kk * dk_state, axis=0, keepdims=True) + jnp.exp(glast) * jnp.sum(st * dst, axis=0, keepdims=True)
            dg = dg + jnp.where(last_row, dgl, 0.0)
            dlogf = _exact_tri_matmul(triu, dg)
            dfv = dlogf / f - dk
            dq_ref[sl, ls] = (dq * (sq * (1.0 + hq * (1.0 - sq)))).astype(dq_ref.dtype)
            df_ref[sl, ls] = (dfv * (1.0 - lb) * sg * (1.0 - sg)).astype(df_ref.dtype)
            dv_ref[sl, ls] = dv.astype(dv_ref.dtype)
            dlb_ref[hh] += jnp.sum(dfv * (1.0 - sg), axis=0, keepdims=True)
            dstate[hh] = dst * jnp.exp(glast) + _dot_tn(dob, qg.astype(BF16))

        def chunk(j, carry):
            ci = nchunk - 1 - j
            sl = pl.ds(pl.multiple_of(ci * c, c), c)
            for hh in range(hp):
                one_head(hh, ci, sl)
            return carry

        lax.fori_loop(0, nchunk, chunk, 0)

    def col(cb):
        return pl.BlockSpec((tb, hp * HG_DK), lambda h, i: (nblk - 1 - i, cb * (nh // hp) + h))

    ocol = pl.BlockSpec((tb, hp * HG_DK), lambda h, i: (nblk - 1 - i, h))
    w = nh * HG_DK
    dq, df, dv, dlb = pl.pallas_call(
        body,
        out_shape=[jax.ShapeDtypeStruct((t, w), BF16)] * 3 + [jax.ShapeDtypeStruct((nh, 1, HG_DK), F32)],
        grid=(nh // hp, nblk),
        in_specs=[col(0), col(1), col(2), pl.BlockSpec((hp, 1, HG_DK), lambda h, i: (h, 0, 0)),
                  pl.BlockSpec((hp, nchunk, HG_DK, HG_DK), lambda h, i: (h, nblk - 1 - i, 0, 0)), ocol],
        out_specs=[ocol, ocol, ocol, pl.BlockSpec((hp, 1, HG_DK), lambda h, i: (h, 0, 0))],
        scratch_shapes=[pltpu.VMEM((hp, HG_DK, HG_DK), F32)],
        compiler_params=_cparams(("parallel", "arbitrary")),
        name="hgrn_bwd",
    )(zh, zh, zh, lb3, states, d_o)
    return dq, df, dv, dlb.reshape(w)


def _hg_heads(x, hp):
    return [x[:, h * HG_DK:(h + 1) * HG_DK] for h in range(hp)]


def _hg_intra_wide(q, kk, g, hp):
    c = q.shape[0]
    rows = lax.broadcasted_iota(jnp.int32, (c, 1), 0)
    a_rows = [[] for _ in range(hp)]
    qts, kts, eqs, eks = [], [], [], []
    for i in range(c // HG_SUB):
        lo = i * HG_SUB
        ref = g[lo - 1:lo, :] if i else jnp.zeros_like(g[0:1, :])
        eq = jnp.exp(g[lo:lo + HG_SUB, :] - ref)
        ek = jnp.exp(jnp.where(rows < lo + HG_SUB, ref - g, 0.0))
        qtb = (q[lo:lo + HG_SUB, :] * eq).astype(BF16)
        ktb = (kk * ek).astype(BF16)
        tpos = lo + lax.broadcasted_iota(jnp.int32, (HG_SUB, c), 0)
        spos = lax.broadcasted_iota(jnp.int32, (HG_SUB, c), 1)
        for h, (qh, kh) in enumerate(zip(_hg_heads(qtb, hp), _hg_heads(ktb, hp))):
            a_rows[h].append(jnp.where(spos <= tpos, _dot_nt(qh, kh), 0.0))
        qts.append(qtb), kts.append(ktb), eqs.append(eq), eks.append(ek)
    return [jnp.concatenate(r, axis=0) for r in a_rows], qts, kts, eqs, eks


def _hgrn_fwd(zh, lb3, *, tb=512):
    t = zh.shape[0]
    nh = lb3.shape[0]
    c = HG_CHUNK
    tb = min(tb, t)
    nchunk = tb // c
    hp = HG_HP if nh % HG_HP == 0 else 1
    wp = hp * HG_DK

    def body(hq_ref, hf_ref, hi_ref, lb_ref, o_ref, st_ref, state):
        @pl.when(pl.program_id(1) == 0)
        def _():
            state[...] = jnp.zeros_like(state)

        tril = _tri(c).astype(BF16)

        def chunk(ci, carry):
            sl = pl.ds(pl.multiple_of(ci * c, c), c)
            q, _, _, f = _hg_gates(hq_ref[sl, :], hf_ref[sl, :], lb_ref[...])
            kk = 1.0 - f
            g = _exact_tri_matmul(tril, jnp.log(f))
            a, _, _, _, _ = _hg_intra_wide(q, kk, g, hp)
            vb = hi_ref[sl, :].astype(BF16)
            glast = g[c - 1:c, :]
            qgb = (q * jnp.exp(g)).astype(BF16)
            kgb = (kk * jnp.exp(glast - g)).astype(BF16)
            dec = jnp.exp(glast)
            sts = [state[h] for h in range(hp)]
            for h in range(hp):
                st_ref[h, ci] = sts[h]
            vh, qgh, kgh, dech = _hg_heads(vb, hp), _hg_heads(qgb, hp), _hg_heads(kgb, hp), _hg_heads(dec, hp)
            o = [_dot(a[h].astype(BF16), vh[h]) + _dot_nt(qgh[h], sts[h].astype(BF16)) for h in range(hp)]
            new = [_dot_tn(vh[h], kgh[h]) for h in range(hp)]
            o_ref[sl, :] = jnp.concatenate(o, axis=1)
            for h in range(hp):
                state[h] = sts[h] * dech[h] + new[h]
            return carry

        lax.fori_loop(0, nchunk, chunk, 0)

    def col(cb):
        return pl.BlockSpec((tb, wp), lambda h, i: (i, cb * (nh // hp) + h))

    return pl.pallas_call(
        body,
        out_shape=[jax.ShapeDtypeStruct((t, nh * HG_DK), F32), jax.ShapeDtypeStruct((nh, t // c, HG_DK, HG_DK), F32)],
        grid=(nh // hp, t // tb),
        in_specs=[col(0), col(1), col(2), pl.BlockSpec((1, wp), lambda h, i: (0, h))],
        out_specs=[pl.BlockSpec((tb, wp), lambda h, i: (i, h)),
                   pl.BlockSpec((hp, nchunk, HG_DK, HG_DK), lambda h, i: (h, i, 0, 0))],
        scratch_shapes=[pltpu.VMEM((hp, HG_DK, HG_DK), F32)],
        compiler_params=_cparams(("parallel", "arbitrary")),
        name="hgrn_fwd",
    )(zh, zh, zh, lb3.reshape(1, -1))


def _hgrn_bwd(zh, lb3, states, d_o, *, tb=512):
    t = zh.shape[0]
    nh = lb3.shape[0]
    c = HG_CHUNK
    tb = min(tb, t)
    nchunk = tb // c
    nblk = t // tb
    hp = HG_HP if nh % HG_HP == 0 else 1
    wp = hp * HG_DK

    def body(hq_ref, hf_ref, hi_ref, lb_ref, st_ref, do_ref, dq_ref, df_ref, dv_ref, dlb_ref, dstate):
        @pl.when(pl.program_id(1) == 0)
        def _():
            dstate[...] = jnp.zeros_like(dstate)
            dlb_ref[...] = jnp.zeros_like(dlb_ref)

        tril = _tri(c).astype(BF16)
        triu = _tri(c, upper=True).astype(BF16)
        last_row = lax.broadcasted_iota(jnp.int32, (c, 1), 0) == c - 1
        heads = range(hp)

        def chunk(j, carry):
            ci = nchunk - 1 - j
            sl = pl.ds(pl.multiple_of(ci * c, c), c)
            lb = lb_ref[...]
            hq, hf = hq_ref[sl, :], hf_ref[sl, :]
            q, sq, sg, f = _hg_gates(hq, hf, lb)
            kk = 1.0 - f
            g = _exact_tri_matmul(tril, jnp.log(f))
            a, qts, kts, eqs, eks = _hg_intra_wide(q, kk, g, hp)
            glast = g[c - 1:c, :]
            eg, egl, dec = jnp.exp(g), jnp.exp(glast - g), jnp.exp(glast)
            vb, dob = hi_ref[sl, :].astype(BF16), do_ref[sl, :].astype(BF16)
            qgb, kgb = (q * eg).astype(BF16), (kk * egl).astype(BF16)
            sts = [st_ref[h, ci] for h in heads]
            dsts = [dstate[h] for h in heads]
            stb, dstb = [s.astype(BF16) for s in sts], [s.astype(BF16) for s in dsts]
            vh, doh, qgh, kgh = _hg_heads(vb, hp), _hg_heads(dob, hp), _hg_heads(qgb, hp), _hg_heads(kgb, hp)
            dv = [_dot_tn(a[h].astype(BF16), doh[h]) + _dot_nt(kgh[h], dstb[h]) for h in heads]
            da = [jnp.where(_tri(c), _dot_nt(doh[h], vh[h]), 0.0).astype(BF16) for h in heads]
            dq_inter = jnp.concatenate([_dot(doh[h], stb[h]) for h in heads], axis=1) * eg
            dk_state = jnp.concatenate([_dot(vh[h], dstb[h]) for h in heads], axis=1) * egl
            new_dst = [_dot_tn(doh[h], qgh[h]) for h in heads]
            xs, dk, dgk = [], dk_state, 0.0
            for i in range(c // HG_SUB):
                rs = slice(i * HG_SUB, (i + 1) * HG_SUB)
                kth, qth = _hg_heads(kts[i], hp), _hg_heads(qts[i], hp)
                xi = jnp.concatenate([_dot(da[h][rs, :], kth[h]) for h in heads], axis=1)
                yi = jnp.concatenate([_dot_tn(da[h][rs, :], qth[h]) for h in heads], axis=1)
                xs.append(xi)
                dk = dk + yi * eks[i]
                dgk = dgk + yi * kts[i].astype(F32)
            dq = jnp.concatenate([x * e for x, e in zip(xs, eqs)], axis=0) + dq_inter
            dgq = jnp.concatenate([x * qt.astype(F32) for x, qt in zip(xs, qts)], axis=0)
            dg = dgq - dgk + q * dq_inter - kk * dk_state
            sdot = jnp.concatenate([jnp.sum(sts[h] * dsts[h], axis=0, keepdims=True) for h in heads], axis=1)
            dgl = jnp.sum(kk * dk_state, axis=0, keepdims=True) + dec * sdot
            dg = dg + jnp.where(last_row, dgl, 0.0)
            dlogf = _exact_tri_matmul(triu, dg)
            dfv = dlogf / f - dk
            dq_ref[sl, :] = (dq * (sq * (1.0 + hq * (1.0 - sq)))).astype(dq_ref.dtype)
            df_ref[sl, :] = (dfv * (1.0 - lb) * sg * (1.0 - sg)).astype(df_ref.dtype)
            dv_ref[sl, :] = jnp.concatenate(dv, axis=1).astype(dv_ref.dtype)
            dlb_ref[...] += jnp.sum(dfv * (1.0 - sg), axis=0, keepdims=True)
            dech = _hg_heads(dec, hp)
            for h in heads:
                dstate[h] = dsts[h] * dech[h] + new_dst[h]
            return carry

        lax.fori_loop(0, nchunk, chunk, 0)

    def col(cb):
        return pl.BlockSpec((tb, wp), lambda h, i: (nblk - 1 - i, cb * (nh // hp) + h))

    ocol = pl.BlockSpec((tb, wp), lambda h, i: (nblk - 1 - i, h))
    lbspec = pl.BlockSpec((1, wp), lambda h, i: (0, h))
    w = nh * HG_DK
    dq, df, dv, dlb = pl.pallas_call(
        body,
        out_shape=[jax.ShapeDtypeStruct((t, w), BF16)] * 3 + [jax.ShapeDtypeStruct((1, w), F32)],
        grid=(nh // hp, nblk),
        in_specs=[col(0), col(1), col(2), lbspec,
                  pl.BlockSpec((hp, nchunk, HG_DK, HG_DK), lambda h, i: (h, nblk - 1 - i, 0, 0)), ocol],
        out_specs=[ocol, ocol, ocol, lbspec],
        scratch_shapes=[pltpu.VMEM((hp, HG_DK, HG_DK), F32)],
        compiler_params=_cparams(("parallel", "arbitrary")),
        name="hgrn_bwd",
    )(zh, zh, zh, lb3.reshape(1, -1), states, d_o)
    return dq, df, dv, dlb.reshape(w)


NEG = -1e30
ATT_GW = ATT_HEADS * ATT_DH


def _att_scores(q, kp, kc, has_prev):
    scale = ATT_DH ** -0.5
    i = lax.broadcasted_iota(jnp.int32, (ATT_BLK, ATT_BLK), 0)
    j = lax.broadcasted_iota(jnp.int32, (ATT_BLK, ATT_BLK), 1)
    s_p = jnp.where(jnp.logical_and(j >= i, has_prev), _dot_nt(q, kp) * scale, NEG)
    s_c = jnp.where(j <= i, _dot_nt(q, kc) * scale, NEG)
    return s_p, s_c


def _att_views(arrs, d):
    return [a.reshape(d, -1, ATT_GW) for a in arrs]


def _att_unview(a, d):
    return a.reshape(-1, ATT_GW) if d == 1 else a


ATT_QB = 4


def _attn_fwd(qb, kb, vb, g):
    d = ATT_PATTERNS[g][1]
    q2, k2, v2 = _att_views([qb, kb, vb], d)
    nblk = q2.shape[1] // ATT_BLK
    nq = ATT_QB if nblk % ATT_QB == 0 else 1
    rows = nq * ATT_BLK

    def body(q_ref, kc_ref, kp_ref, vc_ref, vp_ref, o_ref, l_ref):
        first = pl.program_id(1) == 0
        for b in range(nq):
            rs = slice(b * ATT_BLK, (b + 1) * ATT_BLK)
            ps = slice((b - 1) * ATT_BLK, b * ATT_BLK)
            for h in range(ATT_HEADS):
                hs = slice(h * ATT_DH, (h + 1) * ATT_DH)
                kp, vp = (kp_ref[:, hs], vp_ref[:, hs]) if b == 0 else (kc_ref[ps, hs], vc_ref[ps, hs])
                s_p, s_c = _att_scores(q_ref[rs, hs], kp, kc_ref[rs, hs], jnp.logical_not(first) if b == 0 else True)
                m = jnp.maximum(jnp.max(s_p, axis=1, keepdims=True), jnp.max(s_c, axis=1, keepdims=True))
                p_p, p_c = jnp.exp(s_p - m), jnp.exp(s_c - m)
                l = jnp.sum(p_p, axis=1, keepdims=True) + jnp.sum(p_c, axis=1, keepdims=True)
                o = _dot(p_p.astype(BF16), vp) + _dot(p_c.astype(BF16), vc_ref[rs, hs])
                o_ref[rs, hs] = o / l
                l_ref[rs, hs] = jnp.broadcast_to(m + jnp.log(l), (ATT_BLK, ATT_DH))

    cur = pl.BlockSpec((None, rows, ATT_GW), lambda r, n: (r, n, 0))
    prev = pl.BlockSpec((None, ATT_BLK, ATT_GW), lambda r, n: (r, jnp.maximum(n * nq - 1, 0), 0))
    o, lse = pl.pallas_call(
        body,
        out_shape=[jax.ShapeDtypeStruct(q2.shape, F32)] * 2,
        grid=(d, nblk // nq),
        in_specs=[cur, cur, prev, cur, prev],
        out_specs=[cur, cur],
        compiler_params=_cparams(("parallel", "arbitrary")),
        name=f"attn_fwd_g{g}",
    )(q2, k2, k2, v2, v2)
    return _att_unview(o, d), _att_unview(lse, d)


def _attn_bwd(qb, kb, vb, o, lse, d_o, d_lse, g):
    d = ATT_PATTERNS[g][1]
    q2, k2, v2 = _att_views([qb, kb, vb], d)
    o2, l2, do2, dl2 = _att_views([o, lse, d_o, d_lse], d)
    nblk = q2.shape[1] // ATT_BLK
    nq = ATT_QB if nblk % ATT_QB == 0 else 1
    rows = nq * ATT_BLK
    ns = nblk // nq
    scale = ATT_DH ** -0.5

    def body(q_ref, kc_ref, kp_ref, vc_ref, vp_ref, o_ref, l_ref, do_ref, dl_ref, dq_ref, dk_ref, dv_ref, ck, cv):
        n = pl.program_id(1)

        @pl.when(n == 0)
        def _():
            ck[...] = jnp.zeros_like(ck)
            cv[...] = jnp.zeros_like(cv)

        first = n == ns - 1
        for h in range(ATT_HEADS):
            hs = slice(h * ATT_DH, (h + 1) * ATT_DH)
            pend_k, pend_v = ck[:, hs], cv[:, hs]
            for b in reversed(range(nq)):
                rs = slice(b * ATT_BLK, (b + 1) * ATT_BLK)
                ps = slice((b - 1) * ATT_BLK, b * ATT_BLK)
                q, kc, vc = q_ref[rs, hs], kc_ref[rs, hs], vc_ref[rs, hs]
                kp, vp = (kp_ref[:, hs], vp_ref[:, hs]) if b == 0 else (kc_ref[ps, hs], vc_ref[ps, hs])
                s_p, s_c = _att_scores(q, kp, kc, jnp.logical_not(first) if b == 0 else True)
                lse_h = l_ref[rs, hs][:, 0:1]
                p_p, p_c = jnp.exp(s_p - lse_h), jnp.exp(s_c - lse_h)
                do = do_ref[rs, hs]
                delta = jnp.sum(do * o_ref[rs, hs] - dl_ref[rs, hs], axis=1, keepdims=True)
                dob = do.astype(BF16)
                ds_p = (p_p * (_dot_nt(dob, vp) - delta) * scale).astype(BF16)
                ds_c = (p_c * (_dot_nt(dob, vc) - delta) * scale).astype(BF16)
                dq_ref[rs, hs] = _dot(ds_p, kp) + _dot(ds_c, kc)
                dk_ref[rs, hs] = pend_k + _dot_tn(ds_c, q)
                dv_ref[rs, hs] = pend_v + _dot_tn(p_c.astype(BF16), dob)
                pend_k, pend_v = _dot_tn(ds_p, q), _dot_tn(p_p.astype(BF16), dob)
            ck[:, hs] = pend_k
            cv[:, hs] = pend_v

    cur = pl.BlockSpec((None, rows, ATT_GW), lambda r, n: (r, ns - 1 - n, 0))
    prev = pl.BlockSpec((None, ATT_BLK, ATT_GW), lambda r, n: (r, jnp.maximum((ns - 1 - n) * nq - 1, 0), 0))
    shp = jax.ShapeDtypeStruct(q2.shape, F32)
    dq, dk, dv = pl.pallas_call(
        body,
        out_shape=[shp, shp, shp],
        grid=(d, ns),
        in_specs=[cur, cur, prev, cur, prev, cur, cur, cur, cur],
        out_specs=[cur, cur, cur],
        scratch_shapes=[pltpu.VMEM((ATT_BLK, ATT_GW), F32), pltpu.VMEM((ATT_BLK, ATT_GW), F32)],
        compiler_params=_cparams(("parallel", "arbitrary")),
        name=f"attn_bwd_g{g}",
    )(q2, k2, k2, v2, v2, o2, l2, do2, dl2)
    return _att_unview(dq, d), _att_unview(dk, d), _att_unview(dv, d)


def _attn_fwd_1blk(qb, kb, vb, g):
    d = ATT_PATTERNS[g][1]
    q2, k2, v2 = _att_views([qb, kb, vb], d)
    nb = q2.shape[1] // ATT_BLK

    def body(q_ref, kc_ref, kp_ref, vc_ref, vp_ref, o_ref, l_ref):
        has_prev = pl.program_id(1) > 0
        for h in range(ATT_HEADS):
            hs = slice(h * ATT_DH, (h + 1) * ATT_DH)
            s_p, s_c = _att_scores(q_ref[:, hs], kp_ref[:, hs], kc_ref[:, hs], has_prev)
            m = jnp.maximum(jnp.max(s_p, axis=1, keepdims=True), jnp.max(s_c, axis=1, keepdims=True))
            p_p, p_c = jnp.exp(s_p - m), jnp.exp(s_c - m)
            l = jnp.sum(p_p, axis=1, keepdims=True) + jnp.sum(p_c, axis=1, keepdims=True)
            o = _dot(p_p.astype(BF16), vp_ref[:, hs]) + _dot(p_c.astype(BF16), vc_ref[:, hs])
            o_ref[:, hs] = o / l
            l_ref[:, hs] = jnp.broadcast_to(m + jnp.log(l), (ATT_BLK, ATT_DH))

    cur = pl.BlockSpec((None, ATT_BLK, ATT_GW), lambda r, n: (r, n, 0))
    prev = pl.BlockSpec((None, ATT_BLK, ATT_GW), lambda r, n: (r, jnp.maximum(n - 1, 0), 0))
    o, lse = pl.pallas_call(
        body,
        out_shape=[jax.ShapeDtypeStruct(q2.shape, F32)] * 2,
        grid=(d, nb),
        in_specs=[cur, cur, prev, cur, prev],
        out_specs=[cur, cur],
        compiler_params=_cparams(("parallel", "arbitrary")),
        name=f"attn_fwd_g{g}",
    )(q2, k2, k2, v2, v2)
    return _att_unview(o, d), _att_unview(lse, d)


def _attn_bwd_1blk(qb, kb, vb, o, lse, d_o, d_lse, g):
    d = ATT_PATTERNS[g][1]
    q2, k2, v2 = _att_views([qb, kb, vb], d)
    o2, l2, do2, dl2 = _att_views([o, lse, d_o, d_lse], d)
    nb = q2.shape[1] // ATT_BLK

    def body(q_ref, kc_ref, kp_ref, vc_ref, vp_ref, o_ref, l_ref, do_ref, dl_ref, dq_ref, dk_ref, dv_ref, ck, cv):
        n = pl.program_id(1)
        active = n < nb

        @pl.when(n == 0)
        def _():
            ck[...] = jnp.zeros_like(ck)
            cv[...] = jnp.zeros_like(cv)

        @pl.when(jnp.logical_not(active))
        def _():
            dk_ref[...] = ck[...]
            dv_ref[...] = cv[...]

        @pl.when(active)
        def _():
            has_prev = n > 0
            for h in range(ATT_HEADS):
                hs = slice(h * ATT_DH, (h + 1) * ATT_DH)
                q, kp, kc, vp, vc = q_ref[:, hs], kp_ref[:, hs], kc_ref[:, hs], vp_ref[:, hs], vc_ref[:, hs]
                s_p, s_c = _att_scores(q, kp, kc, has_prev)
                lse_h = l_ref[:, hs][:, 0:1]
                p_p, p_c = jnp.exp(s_p - lse_h), jnp.exp(s_c - lse_h)
                do = do_ref[:, hs]
                delta = jnp.sum(do * o_ref[:, hs] - dl_ref[:, hs], axis=1, keepdims=True)
                dob = do.astype(BF16)
                scale = ATT_DH ** -0.5
                ds_p = (p_p * (_dot_nt(dob, vp) - delta) * scale).astype(BF16)
                ds_c = (p_c * (_dot_nt(dob, vc) - delta) * scale).astype(BF16)
                dq_ref[:, hs] = _dot(ds_p, kp) + _dot(ds_c, kc)
                dk_ref[:, hs] = ck[:, hs] + _dot_tn(ds_p, q)
                dv_ref[:, hs] = cv[:, hs] + _dot_tn(p_p.astype(BF16), dob)
                ck[:, hs] = _dot_tn(ds_c, q)
                cv[:, hs] = _dot_tn(p_c.astype(BF16), dob)

    def qn(n):
        return jnp.minimum(n, nb - 1)

    cur = pl.BlockSpec((None, ATT_BLK, ATT_GW), lambda r, n: (r, qn(n), 0))
    prev = pl.BlockSpec((None, ATT_BLK, ATT_GW), lambda r, n: (r, jnp.maximum(qn(n) - 1, 0), 0))
    behind = pl.BlockSpec((None, ATT_BLK, ATT_GW), lambda r, n: (r, jnp.maximum(n - 1, 0), 0))
    shp = jax.ShapeDtypeStruct(q2.shape, F32)
    dq, dk, dv = pl.pallas_call(
        body,
        out_shape=[shp, shp, shp],
        grid=(d, nb + 1),
        in_specs=[cur, cur, prev, cur, prev, cur, cur, cur, cur],
        out_specs=[cur, behind, behind],
        scratch_shapes=[pltpu.VMEM((ATT_BLK, ATT_GW), F32), pltpu.VMEM((ATT_BLK, ATT_GW), F32)],
        compiler_params=_cparams(("parallel", "arbitrary")),
        name=f"attn_bwd_g{g}",
    )(q2, k2, k2, v2, v2, o2, l2, do2, dl2)
    return _att_unview(dq, d), _att_unview(dk, d), _att_unview(dv, d)


def _rms_parts(x, width):
    outs = []
    for lo in range(0, x.shape[1], width):
        xs = x[:, lo:lo + width]
        r = lax.rsqrt(jnp.mean(xs * xs, axis=1, keepdims=True) + EPS)
        outs.append((xs * r, r))
    return outs


def _rms_bwd_part(xh, r, dxh):
    return r * (dxh - xh * jnp.mean(dxh * xh, axis=1, keepdims=True))


def _norm_fwd(x, gain):
    d = x.shape[1]

    def fn(ins, consts):
        (xh, _), = _rms_parts(ins[0], d)
        return [xh * consts[0]], []

    (h,), _ = _rowwise(fn, [(x, d, 0)], [gain.reshape(1, d)], [(d, BF16)], [], bm=512, name="norm_fwd")
    return h


def _norm_bwd(x, gain, dh, dres):
    d = x.shape[1]

    def fn(ins, consts):
        (xh, r), = _rms_parts(ins[0], d)
        dx = ins[2] + _rms_bwd_part(xh, r, ins[1] * consts[0])
        return [dx], [_colsum8(ins[1] * xh)]

    (dx,), (dg,) = _rowwise(fn, [(x, d, 0), (dh, d, 0), (dres, d, 0)], [gain.reshape(1, d)], [(d, F32)], [d],
                            bm=512, name="norm_bwd")
    return dx, dg


def _rot_sign():
    lane = lax.broadcasted_iota(jnp.int32, (1, ATT_DH), 1)
    return jnp.where(lane < ATT_DH // 2, -1.0, 1.0).astype(F32)


def _rope(y, cos, sin):
    return y * cos + pltpu.roll(y, ATT_DH // 2, axis=1) * _rot_sign() * sin


def _rope_t(dy, cos, sin):
    return dy * cos - pltpu.roll(dy * sin, ATT_DH // 2, axis=1) * _rot_sign()


def _qk_prep(zq, zk, zv, qn, kn, cos, sin):
    w = zq.shape[1]

    def fn(ins, consts):
        cs, sn = ins[3], ins[4]
        outs = []
        for z, gain in ((ins[0], consts[0]), (ins[1], consts[1])):
            for i, (xh, _) in enumerate(_rms_parts(z, ATT_DH)):
                outs.append(_rope(xh * gain[:, i * ATT_DH:(i + 1) * ATT_DH], cs, sn))
        outs += [ins[2][:, i * ATT_DH:(i + 1) * ATT_DH] for i in range(w // ATT_DH)]
        groups = [jnp.concatenate(outs[i:i + ATT_HEADS], axis=1) for i in range(0, len(outs), ATT_HEADS)]
        return groups, []

    outs, _ = _rowwise(fn, [(zq, w, 0), (zk, w, 0), (zv, w, 0), (cos, ATT_DH, 0), (sin, ATT_DH, 0)], [qn, kn],
                       [(ATT_GW, BF16, ATT_PATTERNS[g][1]) for g in range(ATT_GROUPS)] * 3, [], bm=256, name="qk_prep")
    return outs[0:3], outs[3:6], outs[6:9]


def _qk_prep_bwd(zq, zk, dq_g, dk_g, dv_g, qn, kn, cos, sin):
    w = zq.shape[1]

    def fn(ins, consts):
        cs, sn = ins[2], ins[3]
        outs, sums = [], []
        for z, gain, dparts in ((ins[0], consts[0], ins[4:7]), (ins[1], consts[1], ins[7:10])):
            dout = jnp.concatenate(dparts, axis=1)
            dz, dgain = [], []
            for i, (xh, r) in enumerate(_rms_parts(z, ATT_DH)):
                hs = slice(i * ATT_DH, (i + 1) * ATT_DH)
                dy = _rope_t(dout[:, hs], cs, sn)
                dgain.append(_colsum8(dy * xh))
                dz.append(_rms_bwd_part(xh, r, dy * gain[:, hs]))
            outs.append(jnp.concatenate(dz, axis=1))
            sums.append(jnp.concatenate(dgain, axis=1))
        outs.append(jnp.concatenate(ins[10:13], axis=1))
        return outs, sums

    ins = [(zq, w, 0), (zk, w, 0), (cos, ATT_DH, 0), (sin, ATT_DH, 0)]
    for parts in (dq_g, dk_g, dv_g):
        ins += [(a, ATT_GW, 0, ATT_PATTERNS[g][1]) for g, a in enumerate(parts)]
    (dzq, dzk, dzv), (dqn, dkn) = _rowwise(fn, ins, [qn, kn], [(w, BF16)] * 3, [w, w], bm=256, name="qk_prep_bwd")
    return dzq, dzk, dzv, dqn, dkn


def _post_a(o_raw, zh, gout):
    w = o_raw.shape[1]

    def fn(ins, consts):
        oh = jnp.concatenate([xh for xh, _ in _rms_parts(ins[0], HG_DK)], axis=1)
        hg = ins[1]
        return [oh * consts[0] * (hg * _sigmoid(hg))], []

    (y,), _ = _rowwise(fn, [(o_raw, w, 0), (zh, w, 3)], [gout.reshape(1, w)], [(w, BF16)], [], bm=512, name="post_a")
    return y


def _post_a_bwd(o_raw, zh, gout, dy):
    w = o_raw.shape[1]

    def fn(ins, consts):
        parts = _rms_parts(ins[0], HG_DK)
        oh = jnp.concatenate([xh for xh, _ in parts], axis=1)
        hg, dyv, gain = ins[1], ins[2], consts[0]
        sg = _sigmoid(hg)
        s = hg * sg
        doh = dyv * gain * s
        do = jnp.concatenate([_rms_bwd_part(xh, r, doh[:, i * HG_DK:(i + 1) * HG_DK]) for i, (xh, r) in enumerate(parts)], axis=1)
        dhg = dyv * oh * gain * (sg * (1.0 + hg * (1.0 - sg)))
        return [do, dhg], [_colsum8(dyv * oh * s)]

    (do, dhg), (dgain,) = _rowwise(fn, [(o_raw, w, 0), (zh, w, 3), (dy, w, 0)], [gout.reshape(1, w)],
                                   [(w, F32), (w, BF16)], [w], bm=512, name="post_a_bwd")
    return do, dhg, dgain


def _merge_alpha(lses):
    m = jnp.maximum(jnp.maximum(lses[0], lses[1]), lses[2])
    e = [jnp.exp(l - m) for l in lses]
    inv = 1.0 / (e[0] + e[1] + e[2])
    return [x * inv for x in e]


def _group_ins(parts):
    return [(a, ATT_GW, 0, ATT_PATTERNS[g][1]) for g, a in enumerate(parts)]


def _merge_b(o_g, lse_g):
    def fn(ins, consts):
        al = _merge_alpha(ins[3:6])
        return [al[0] * ins[0] + al[1] * ins[1] + al[2] * ins[2]], []

    (y,), _ = _rowwise(fn, _group_ins(o_g) + _group_ins(lse_g), [], [(ATT_GW, BF16)], [], bm=512, name="merge_b")
    return y


def _merge_b_bwd(o_g, lse_g, dy):
    def fn(ins, consts):
        al = _merge_alpha(ins[3:6])
        dyv = ins[6]
        dal = [dyv * ins[i] for i in range(3)]
        tot = al[0] * dal[0] + al[1] * dal[1] + al[2] * dal[2]
        return [al[i] * dyv for i in range(3)] + [al[i] * (dal[i] - tot) for i in range(3)], []

    outs, _ = _rowwise(fn, _group_ins(o_g) + _group_ins(lse_g) + [(dy, ATT_GW, 0)], [],
                       [(ATT_GW, F32, ATT_PATTERNS[g][1]) for g in range(ATT_GROUPS)] * 2, [], bm=512, name="merge_b_bwd")
    return outs[:3], outs[3:]


def _loss_head(y, target):
    d = y.shape[1]

    def fn(ins, consts):
        e = ins[0] - ins[1]
        return [e * (1.0 / d)], [_colsum8(e * e)]

    (dy,), (sq,) = _rowwise(fn, [(y, d, 0), (target, d, 0)], [], [(d, F32)], [d], bm=512, name="loss_head")
    return 0.5 * jnp.sum(sq) / d, dy


def _silu_grad(a):
    s = _sigmoid(a)
    return s * (1.0 + a * (1.0 - s))


def _ffn_fwd(x, gain, wt, wo_fn, tag):
    t, d = x.shape
    f = wt.shape[0] // 2
    h = _norm_fwd(x, gain)

    def act(accs, ex):
        a, b = accs
        s = _sigmoid(a)
        sa = a * s
        return (sa * b, b, 0.5 * sa, 0.5 * (s + sa * (1.0 - s)))

    bn = FFN_BN if f % FFN_BN == 0 else 256
    u, b, sa, sp = _mm([h], [wt, wt], [(0, 0, 0), (0, 1, 1)], 2, act, [BF16] * 4, m=t, n=f, k=d, tb=True,
                       bm=512, bn=bn, bk=d, b_off=[(0, 0), (f // min(bn, f), 0)], n_outer=True, name=f"ffn_in_{tag}")
    wo = wo_fn(u)
    (y,) = _mm([u], [wo], [(0, 0, 0)], 1, lambda accs, ex: (ex[0] + 0.5 * accs[0],), [F32], m=t, n=d, k=f,
               bm=512, bn=d, bk=f, extras=[x], name=f"ffn_out_{tag}")
    return y, (x, h, u, b, sa, sp, wo)


def _ffn_bwd(dy, saved, gain, wt, tag, tok, emit):
    x, h, u, b, sa, sp, wo = saved
    t, d = x.shape
    f = wo.shape[0]
    dyb = (dy + tok).astype(BF16)

    def dact(accs, ex):
        bv, sav, spv = (e.astype(F32) for e in ex)
        return (accs[0] * bv * spv, accs[0] * sav)

    bn = FFN_BN if f % FFN_BN == 0 else 256
    da, db = _mm([dyb], [wo], [(0, 0, 0)], 1, dact, [BF16, BF16], m=t, n=f, k=d, tb=True, bm=512, bn=bn, bk=d,
                 extras=[b, sa, sp], n_outer=True, name=f"ffn_dact_{tag}")
    (dwo,) = _mm([u], [dyb], [(0, 0, 0)], 1, lambda accs, ex: (0.5 * accs[0],), [BF16], m=f, n=d, k=t, ta=True,
                 bm=1408, bn=d, bk=1024, name=f"ffn_dwo_{tag}")
    dwt = [_mm([g], [h], [(0, 0, 0)], 1, _first, [BF16], m=f, n=d, k=t, ta=True, bm=1408, bn=d, bk=1024,
               name=f"ffn_dwt{i}_{tag}")[0] for i, g in enumerate((da, db))]
    tok = emit(jnp.concatenate(dwt, axis=0), dwo)
    (dh,) = _mm([da, db], [wt, wt], [(0, 0, 0), (1, 1, 0)], 1, _first, [F32], m=t, n=d, k=f, bm=512, bn=d, bk=f,
                b_off=[(0, 0), (0, 1)], name=f"ffn_dh_{tag}")
    dx, dgain = _norm_bwd(x, gain + tok, dh, dy)
    return dx, dgain, tok


FFN_BN = 1408
Z_SPLITS = (("h", 4096), ("q", 1536), ("k", 1536), ("v", 1536), ("g", 2048))


def _mix_fwd(x, p, cos, sin):
    t, d = x.shape
    hm = _norm_fwd(x, p["gm"])
    z, off = {}, 0
    for nm, width in Z_SPLITS:
        bn = 1024 if off % 1024 == 0 and width % 1024 == 0 else 512
        (z[nm],) = _mm([hm], [p["wint"]], [(0, 0, 0)], 1, _first, [F32], m=t, n=width, k=d, tb=True, bm=1024, bn=bn, bk=d,
                       b_off=[(off // bn, 0)], name=f"mix_in_{nm}")
        off += width
    o_raw, states = _hgrn_fwd(z["h"], p["lb3"])
    qb, kb, vb = _qk_prep(z["q"], z["k"], z["v"], p["qn"], p["kn"], cos, sin)
    o_g, lse_g = zip(*[_attn_fwd(qb[g], kb[g], vb[g], g) for g in range(ATT_GROUPS)])
    oa = _post_a(o_raw, z["h"], p["gout"])
    ob = _merge_b(o_g, lse_g)
    late = p["late"](ob)
    p = dict(p, **late)
    (ya,) = _mm([oa], [p["wa"]], [(0, 0, 0)], 1, _first, [F32], m=t, n=d, k=oa.shape[1], bm=1024, bn=d, bk=oa.shape[1],
                name="branch_a")

    def gate(accs, ex):
        return (_sigmoid(ex[0]) * ex[2] + _sigmoid(ex[1]) * accs[0], accs[0])

    merged, yb = _mm([ob], [p["wbt"]], [(0, 0, 0)], 1, gate, [BF16, F32], m=t, n=d, k=ATT_GW, tb=True, bm=512, bn=d,
                     bk=ATT_GW, extras=[z["g"], z["g"], ya], e_off=[0, 1, 0], name="branch_b_gate")
    (y,) = _mm([merged], [p["wo"]], [(0, 0, 0)], 1, lambda accs, ex: (ex[0] + accs[0],), [F32], m=t, n=d, k=d,
               bm=1024, bn=d, bk=d, extras=[x], name="mix_out")
    return y, (x, hm, z, o_raw, states, qb, kb, vb, o_g, lse_g, oa, ob, ya, yb, merged, late)


def _mix_bwd(dy, saved, p, cos, sin, tok):
    x, hm, z, o_raw, states, qb, kb, vb, o_g, lse_g, oa, ob, ya, yb, merged, late = saved
    p = dict(p, **late)
    t, d = x.shape
    w = oa.shape[1]
    dyb = (dy + tok).astype(BF16)

    def dgate(accs, ex):
        dm = accs[0]
        sa, sb = _sigmoid(ex[0]), _sigmoid(ex[1])
        return (sa * dm, sb * dm, dm * ex[2] * sa * (1.0 - sa), dm * ex[3] * sb * (1.0 - sb))

    dya, dyb_, dga, dgb = _mm([dyb], [p["wo"]], [(0, 0, 0)], 1, dgate, [BF16] * 4, m=t, n=d, k=d, tb=True, bm=512, bn=d,
                              bk=d, extras=[z["g"], z["g"], ya, yb], e_off=[0, 1, 0, 0], name="mix_out_bwd")
    (dwo,) = _mm([merged], [dyb], [(0, 0, 0)], 1, _first, [BF16], m=d, n=d, k=t, ta=True, bm=d, bn=d, bk=1024, name="mix_dwo")
    (doa,) = _mm([dya], [p["wa"]], [(0, 0, 0)], 1, _first, [F32], m=t, n=w, k=d, tb=True, bm=1024, bn=w, bk=d, name="branch_a_bwd")
    (dwa,) = _mm([oa], [dya], [(0, 0, 0)], 1, _first, [BF16], m=w, n=d, k=t, ta=True, bm=w, bn=d, bk=1024, name="branch_a_dw")
    (dob,) = _mm([dyb_], [p["wbt"]], [(0, 0, 0)], 1, _first, [F32], m=t, n=ATT_GW, k=d, bm=1024, bn=ATT_GW, bk=d,
                 name="branch_b_bwd")
    (dwbt,) = _mm([dyb_], [ob], [(0, 0, 0)], 1, _first, [BF16], m=d, n=ATT_GW, k=t, ta=True, bm=d, bn=ATT_GW, bk=1024,
                  name="branch_b_dw")
    do_raw, dhg, dgout = _post_a_bwd(o_raw, z["h"], p["gout"], doa)
    do_g, dlse_g = _merge_b_bwd(o_g, lse_g, dob)
    dq_g, dk_g, dv_g = zip(*[_attn_bwd(qb[g], kb[g], vb[g], o_g[g], lse_g[g], do_g[g], dlse_g[g], g)
                             for g in range(ATT_GROUPS)])
    dzq, dzk, dzv, dqn, dkn = _qk_prep_bwd(z["q"], z["k"], dq_g, dk_g, dv_g, p["qn"], p["kn"], cos, sin)
    dhq, dhf, dhi, lbsum = _hgrn_bwd(z["h"], p["lb3"], states, do_raw)
    dz = jnp.concatenate([dhq, dhf, dhi, dhg, dzq, dzk, dzv, dga, dgb], axis=1)
    pw = dz.shape[1]
    (dhm,) = _mm([dz], [p["wint"]], [(0, 0, 0)], 1, _first, [F32], m=t, n=d, k=pw, bm=1024, bn=d, bk=1536, name="mix_in_bwd")
    (dwint,) = _mm([dz], [hm], [(0, 0, 0)], 1, _first, [BF16], m=pw, n=d, k=t, ta=True, bm=1536, bn=d, bk=1024, name="mix_in_dw")
    dx, dgm = _norm_bwd(x, p["gm"], dhm, dy)
    return dx, dict(gm=dgm, wint=dwint, lbsum=lbsum, gout=dgout, qn=dqn, kn=dkn, wa=dwa, wbt=dwbt, wo=dwo)


def _rope_tables(t):
    pos = jnp.arange(t, dtype=F32)
    inv = ROPE_THETA ** (-jnp.arange(0, ATT_DH, 2, dtype=F32) / ATT_DH)
    ang = pos[:, None] * inv[None, :]
    ang = jnp.concatenate([ang, ang], axis=-1)
    return jnp.cos(ang), jnp.sin(ang)


def _lower_bounds(logits):
    lb = jnp.cumsum(jax.nn.softmax(logits, axis=0), axis=0)
    return lb - lb[0:1]


def _head_gain(g):
    return jnp.tile(g[:, None, :], (1, ATT_HEADS, 1)).reshape(1, ATT_GROUPS * ATT_GW)


SMALL_GRADS = ("ffn1_norm", "mix_norm", "lbsum", "hgrn_out_norm", "attn_q_norm", "attn_k_norm", "ffn2_norm")


def _local_step(x, target, small, fetch, emit):
    t = x.shape[0]
    depth = small["ffn1_norm"].shape[0]
    cos, sin = _rope_tables(t)
    lb_all = _lower_bounds(small["hgrn_lb_logits"])
    saved = []
    for l in range(depth):
        w1t = fetch("w1t", l, x)["w1t"]
        x, s1 = _ffn_fwd(x, small["ffn1_norm"][l], w1t, lambda after, l=l: fetch("w1o", l, after)["w1o"], "1")
        p = dict(gm=small["mix_norm"][l], wint=fetch("wint", l, x)["wint"], lb3=lb_all[l].reshape(-1, 1, HG_DK),
                 gout=small["hgrn_out_norm"][l], qn=_head_gain(small["attn_q_norm"][l]),
                 kn=_head_gain(small["attn_k_norm"][l]), late=functools.partial(fetch, "mout", l))
        x, sm = _mix_fwd(x, p, cos, sin)
        w2t = fetch("w2t", l, x)["w2t"]
        x, s2 = _ffn_fwd(x, small["ffn2_norm"][l], w2t, lambda after, l=l: fetch("w2o", l, after)["w2o"], "2")
        saved.append((p, w1t, w2t, s1, sm, s2))
    loss, dx = _loss_head(x, target)
    gsmall = {k: [None] * depth for k in SMALL_GRADS}
    tok = jnp.zeros((), F32)
    for l in reversed(range(depth)):
        p, w1t, w2t, s1, sm, s2 = saved[l]
        dx, gsmall["ffn2_norm"][l], tok = _ffn_bwd(dx, s2, small["ffn2_norm"][l], w2t, "2", tok,
                                                   lambda dwt, dwo, l=l: emit("ffn2", l, dict(w2t=dwt, w2o=dwo), None))
        dx, gm = _mix_bwd(dx, sm, p, cos, sin, tok)
        tok = emit("mix", l, {k: gm[k] for k in ("wint", "wa", "wbt", "wo")}, None)
        gsmall["mix_norm"][l], gsmall["lbsum"][l], gsmall["hgrn_out_norm"][l] = gm["gm"], gm["lbsum"], gm["gout"]
        for k, src in (("attn_q_norm", "qn"), ("attn_k_norm", "kn")):
            gsmall[k][l] = jnp.sum(gm[src].reshape(ATT_GROUPS, ATT_HEADS, ATT_DH), axis=1)
        dx, gsmall["ffn1_norm"][l], tok = _ffn_bwd(dx, s1, small["ffn1_norm"][l], w1t, "1", tok,
                                                   lambda dwt, dwo, l=l: emit("ffn1", l, dict(w1t=dwt, w1o=dwo), None))
    emit("small", 0, {}, ({k: jnp.stack(v) for k, v in gsmall.items()}, loss))
    return dx


_HBM = pl.BlockSpec(memory_space=pltpu.HBM)
_SEM = pl.BlockSpec(memory_space=pltpu.SEMAPHORE)
_EFFECT = pltpu.SideEffectType.DATAFLOW_SIDE_EFFECTING


def _peer(p):
    x, y, c = lax.axis_index("x"), lax.axis_index("y"), lax.axis_index("c")
    me = 4 * x + 2 * y + c
    return (1 - x if p & 4 else x, 1 - y if p & 2 else y, 1 - c if p & 1 else c), jnp.bitwise_xor(me, p), me


def _xchg_copy(src, land, mode, send_sems, recv_sems, k, p, arriving):
    peer, peer_id, me = _peer(p)
    block = src if mode == "gather" else src.at[peer_id]
    return pltpu.make_async_remote_copy(
        src_ref=block, dst_ref=land.at[peer_id if arriving else me], send_sem=send_sems.at[k * (N_DEV - 1) + p - 1],
        recv_sem=recv_sems.at[k * (N_DEV - 1) + p - 1], device_id=peer, device_id_type=MESH)


def _xchg_start(srcs, modes, groups, name):
    n, ng = len(srcs), len(groups)

    def body(*refs):
        src = refs[:n]
        sems = refs[n:n + 2 * ng]
        land = refs[n + 2 * ng + n:n + 2 * ng + 2 * n]
        token = refs[n + 2 * ng + 2 * n]
        for gi, idx in enumerate(groups):
            for ki, k in enumerate(idx):
                for p in range(1, N_DEV):
                    _xchg_copy(src[k], land[k], modes[k], sems[2 * gi], sems[2 * gi + 1], ki, p, False).start()
        token[...] = jnp.zeros_like(token)

    sem_shapes = []
    for idx in groups:
        sem_shapes += [pltpu.SemaphoreType.DMA((len(idx) * (N_DEV - 1),))] * 2
    outs = pl.pallas_call(
        body,
        out_shape=sem_shapes + [pltpu.HBM(a.shape, a.dtype) for a in srcs]
        + [pltpu.HBM((N_DEV,) + a.shape[-2:], a.dtype) for a in srcs] + [jax.ShapeDtypeStruct((8, 128), F32)],
        in_specs=[_HBM] * n,
        out_specs=[_SEM] * (2 * ng) + [_HBM] * (2 * n) + [pl.BlockSpec(memory_space=pltpu.VMEM)],
        input_output_aliases={i: 2 * ng + i for i in range(n)},
        compiler_params=pltpu.CompilerParams(has_side_effects=_EFFECT),
        name=name,
    )(*[pltpu.with_memory_space_constraint(a, pltpu.HBM) for a in srcs])
    sems = [(outs[2 * gi], outs[2 * gi + 1]) for gi in range(ng)]
    return sems, outs[2 * ng:2 * ng + n], outs[2 * ng + n:2 * ng + 2 * n], outs[-1]


def _xchg_wait_call(srcs, lands, modes, sems, after, name):
    n = len(srcs)

    def body(*refs):
        src, land = refs[:n], refs[n:2 * n]
        send_sems, recv_sems = refs[2 * n], refs[2 * n + 1]
        for p in range(1, N_DEV):
            for k in range(n):
                cp = _xchg_copy(src[k], land[k], modes[k], send_sems, recv_sems, k, p, True)
                cp.wait_send()
                cp.wait_recv()

    outs = pl.pallas_call(
        body,
        out_shape=[pltpu.HBM(a.shape, a.dtype) for a in list(srcs) + list(lands)],
        in_specs=[_HBM] * (2 * n) + [_SEM, _SEM, pl.BlockSpec(memory_space=pl.ANY)],
        out_specs=[_HBM] * (2 * n),
        input_output_aliases={i: i for i in range(2 * n)},
        compiler_params=pltpu.CompilerParams(has_side_effects=_EFFECT),
        name=name,
    )(*srcs, *lands, sems[0], sems[1], after)
    return outs[:n], outs[n:]


def _xchg_wait(srcs, lands, modes, sems, after, name):
    srcs, lands = _xchg_wait_call(srcs, lands, modes, sems, after, name)
    me = 4 * lax.axis_index("x") + 2 * lax.axis_index("y") + lax.axis_index("c")
    done = []
    for a, land, mode in zip(srcs, lands, modes):
        own = a[None] if mode == "gather" else lax.dynamic_slice_in_dim(a, me, 1, axis=0)
        done.append(lax.dynamic_update_slice(land, own, (me, 0, 0)))
    return done


def _sum_slots(land):
    g, _, r, c = land.shape
    br = r // 2 if (r % 32 == 0 and r >= 256) else r

    def body(l_ref, o_ref):
        acc = l_ref[0, 0].astype(F32)
        for j in range(1, N_DEV):
            acc = acc + l_ref[0, j].astype(F32)
        o_ref[0] = acc

    return pl.pallas_call(
        body,
        out_shape=jax.ShapeDtypeStruct((g, r, c), F32),
        grid=(g, r // br),
        in_specs=[pl.BlockSpec((1, N_DEV, br, c), lambda i, j: (i, 0, j, 0))],
        out_specs=pl.BlockSpec((1, br, c), lambda i, j: (i, j, 0)),
        compiler_params=_cparams(("parallel", "parallel")),
        name="sum_slots",
    )(land)


def _adamw(w, g, m, v):
    shape = w.shape
    cols = shape[-1]
    rows = int(np.prod(shape[:-1]))
    bm = max(b for b in range(8, 257, 8) if rows % b == 0) if rows % 8 == 0 else rows
    c1 = 1.0 - ADAM_B1 ** ADAM_STEP
    c2 = 1.0 - ADAM_B2 ** ADAM_STEP

    def fn(ins, consts):
        wv, gv, mv, vv = ins
        m2 = ADAM_B1 * mv + (1.0 - ADAM_B1) * gv
        v2 = ADAM_B2 * vv + (1.0 - ADAM_B2) * (gv * gv)
        delta = -ADAM_LR * ((m2 / c1) / (jnp.sqrt(v2 / c2) + ADAM_EPS) + ADAM_WD * wv)
        return [delta, m2, v2], []

    outs, _ = _rowwise(fn, [(a.reshape(rows, cols), cols, 0) for a in (w, g, m, v)], [], [(cols, F32)] * 3, [],
                       bm=bm, name="adamw")
    return [o.reshape(shape) for o in outs]


BIG = ("w1t", "w1o", "wint", "wa", "wbt", "wo", "w2t", "w2o")
FETCH_GROUPS = dict(w1t=("w1t",), w1o=("w1o",), wint=("wint",), mout=("wa", "wbt", "wo"), w2t=("w2t",), w2o=("w2o",))
SMALL_ROWS = (("ffn1_norm", 0), ("mix_norm", 2), ("lbsum", 4), ("hgrn_out_norm", 6), ("ffn2_norm", 8),
              ("attn_q_norm", 10), ("attn_k_norm", 12))
SMALL_PACK_ROWS = 16


def kernel(x, ffn1_norm, ffn1_w_in, ffn1_w_out, mix_norm, w_in, hgrn_lb_logits, hgrn_out_norm, attn_q_norm, attn_k_norm, w_branch_a, w_branch_b, w_out, ffn2_norm, ffn2_w_in, ffn2_w_out, loss_target, m_ffn1_norm, m_ffn1_w_in, m_ffn1_w_out, m_mix_norm, m_w_in, m_hgrn_lb_logits, m_hgrn_out_norm, m_attn_q_norm, m_attn_k_norm, m_w_branch_a, m_w_branch_b, m_w_out, m_ffn2_norm, m_ffn2_w_in, m_ffn2_w_out, v_ffn1_norm, v_ffn1_w_in, v_ffn1_w_out, v_mix_norm, v_w_in, v_hgrn_lb_logits, v_hgrn_out_norm, v_attn_q_norm, v_attn_k_norm, v_w_branch_a, v_w_branch_b, v_w_out, v_ffn2_norm, v_ffn2_w_in, v_ffn2_w_out):
    names = ("ffn1_norm", "ffn1_w_in", "ffn1_w_out", "mix_norm", "w_in", "hgrn_lb_logits", "hgrn_out_norm", "attn_q_norm",
             "attn_k_norm", "w_branch_a", "w_branch_b", "w_out", "ffn2_norm", "ffn2_w_in", "ffn2_w_out")
    w = dict(zip(names, (ffn1_norm, ffn1_w_in, ffn1_w_out, mix_norm, w_in, hgrn_lb_logits, hgrn_out_norm, attn_q_norm,
                         attn_k_norm, w_branch_a, w_branch_b, w_out, ffn2_norm, ffn2_w_in, ffn2_w_out)))
    m = dict(zip(names, (m_ffn1_norm, m_ffn1_w_in, m_ffn1_w_out, m_mix_norm, m_w_in, m_hgrn_lb_logits, m_hgrn_out_norm,
                         m_attn_q_norm, m_attn_k_norm, m_w_branch_a, m_w_branch_b, m_w_out, m_ffn2_norm, m_ffn2_w_in, m_ffn2_w_out)))
    v = dict(zip(names, (v_ffn1_norm, v_ffn1_w_in, v_ffn1_w_out, v_mix_norm, v_w_in, v_hgrn_lb_logits, v_hgrn_out_norm,
                         v_attn_q_norm, v_attn_k_norm, v_w_branch_a, v_w_branch_b, v_w_out, v_ffn2_norm, v_ffn2_w_in, v_ffn2_w_out)))
    depth, d = ffn1_norm.shape

    def tr(a):
        return jnp.swapaxes(a, 1, 2)

    shard = dict(w1t=tr(ffn1_w_in), w1o=ffn1_w_out, wint=tr(w_in), wa=w_branch_a,
                 wbt=tr(w_branch_b).reshape(depth, -1, d), wo=w_out, w2t=tr(ffn2_w_in), w2o=ffn2_w_out)
    order = [(g, l) for l in range(depth) for g in FETCH_GROUPS]
    flat = [(g, l, k) for g, l in order for k in FETCH_GROUPS[g]]
    groups, pos = [], 0
    for g, l in order:
        groups.append(list(range(pos, pos + len(FETCH_GROUPS[g]))))
        pos += len(FETCH_GROUPS[g])
    g_sems, g_srcs, g_lands, _ = _xchg_start([shard[k][l].astype(BF16) for _, l, k in flat], ["gather"] * len(flat),
                                             groups, "gather_start")

    def fetch(group, l, after):
        gi = order.index((group, l))
        idx = groups[gi]
        lands = _xchg_wait([g_srcs[i] for i in idx], [g_lands[i] for i in idx], ["gather"] * len(idx), g_sems[gi], after,
                           f"gather_wait_{group}{l}")
        out = {}
        for k, land in zip(FETCH_GROUPS[group], lands):
            out[k] = land.reshape(d, -1) if k == "wbt" else land.reshape(-1, d)
        return out

    pending = []

    def emit(group, l, g, final):
        keys = list(g)
        srcs = [g[k].reshape(N_DEV, -1, d) for k in keys]
        modes = ["scatter"] * len(keys)
        if final is not None:
            gsmall, loss = final
            pack = jnp.zeros((SMALL_PACK_ROWS, d), F32)
            for k, r0 in SMALL_ROWS:
                rows = gsmall[k].reshape(depth, -1)
                pack = pack.at[r0:r0 + depth, :rows.shape[1]].set(rows)
            srcs.append(pack.at[14, :].set(loss))
            modes.append("gather")
            keys.append("small")
        sems, s_thru, l_thru, token = _xchg_start(srcs, modes, [list(range(len(srcs)))], f"grads_start_{group}{l}")
        pending.append((group, l, keys, modes, sems[0], s_thru, l_thru))
        return token[0, 0]

    small = {k: w[k] for k in ("ffn1_norm", "mix_norm", "hgrn_lb_logits", "hgrn_out_norm", "attn_q_norm", "attn_k_norm", "ffn2_norm")}
    dx = _local_step(x[0], loss_target[0], small, fetch, emit)

    summed = {}
    for group, l, keys, modes, sems, s_thru, l_thru in pending:
        lands = _xchg_wait(s_thru, l_thru, modes, sems, dx, f"grads_wait_{group}{l}")
        for k, land in zip(keys, lands):
            summed[k, l] = _sum_slots(land[None])[0]
    gsum = {k: jnp.stack([summed[k, l] for l in range(depth)]) for k in BIG}
    tot = summed["small", 0]

    grads = {}
    for k, r0 in SMALL_ROWS:
        shp = (depth,) + (w[k].shape[1:] if k != "lbsum" else (d,))
        grads[k] = tot[r0:r0 + depth, :int(np.prod(shp[1:]))].reshape(shp)
    _, lb_vjp = jax.vjp(_lower_bounds, hgrn_lb_logits)
    grads["hgrn_lb_logits"] = lb_vjp(grads.pop("lbsum"))[0]
    grads["ffn1_w_in"], grads["ffn1_w_out"] = tr(gsum["w1t"]), gsum["w1o"]
    grads["w_in"], grads["w_branch_a"] = tr(gsum["wint"]), gsum["wa"]
    grads["w_branch_b"] = tr(gsum["wbt"].reshape(depth, d // N_DEV, -1))
    grads["w_out"] = gsum["wo"]
    grads["ffn2_w_in"], grads["ffn2_w_out"] = tr(gsum["w2t"]), gsum["w2o"]

    upd = {k: _adamw(w[k], grads[k], m[k], v[k]) for k in names}
    return (tot[14, 0], dx[None], *[grads[k] for k in names], *[upd[k][0] for k in names],
            *[upd[k][1] for k in names], *[upd[k][2] for k in names])
```

```python
import functools
import math

import jax
import jax.numpy as jnp
import numpy as np
from jax import lax
from jax.experimental import pallas as pl
from jax.experimental.pallas import tpu as pltpu

F32 = jnp.float32
BF16 = jnp.bfloat16

N_DEV = 8
EPS = 1e-6
HG_DK = 128
HG_CHUNK = 64
HG_SUB = 16
HG_HP = 4
ATT_PATTERNS = ((128, 1), (512, 4), (2048, 16))
ATT_GROUPS = 3
ATT_HEADS = 4
ATT_DH = 128
ATT_BLK = 128
ROPE_THETA = 10000.0
ADAM_LR, ADAM_B1, ADAM_B2, ADAM_EPS, ADAM_WD, ADAM_STEP = 0.001, 0.9, 0.999, 1e-08, 0.01, 10
VMEM_LIMIT_BYTES = 56 * 1024 * 1024
MESH = pl.DeviceIdType.MESH


def _cparams(sem, **kw):
    return pltpu.CompilerParams(dimension_semantics=sem, vmem_limit_bytes=VMEM_LIMIT_BYTES, **kw)


def _sigmoid(x):
    return 1.0 / (1.0 + jnp.exp(-x))


def _mm(a_list, b_list, pairs, n_acc, fin, out_dtypes, *, m, n, k, ta=False, tb=False, bm, bn, bk,
        b_off=None, extras=(), e_off=None, n_outer=False, name):
    bm, bn, bk = min(bm, m), min(bn, n), min(bk, k)
    assert m % bm == 0 and n % bn == 0 and k % bk == 0, (name, m, n, k, bm, bn, bk)
    nk = k // bk
    b_off = b_off or [(0, 0)] * len(b_list)
    e_off = e_off or [0] * len(extras)
    na, nb, ne = len(a_list), len(b_list), len(extras)
    dn = (((0,) if ta else (1,), (1,) if tb else (0,)), ((), ()))

    def body(*refs):
        a_refs, b_refs = refs[:na], refs[na:na + nb]
        e_refs = refs[na + nb:na + nb + ne]
        o_refs = refs[na + nb + ne:na + nb + ne + len(out_dtypes)]
        acc_refs = refs[na + nb + ne + len(out_dtypes):]
        kk = pl.program_id(2)
        parts = [None] * n_acc
        for ai, bi, ci in pairs:
            p = lax.dot_general(a_refs[ai][...], b_refs[bi][...], dn, preferred_element_type=F32)
            parts[ci] = p if parts[ci] is None else parts[ci] + p

        def finish(accs):
            outs = fin(accs, [e[...] for e in e_refs])
            for o_ref, o in zip(o_refs, outs):
                o_ref[...] = o.astype(o_ref.dtype)

        if nk == 1:
            finish(parts)
        else:
            @pl.when(kk == 0)
            def _():
                for c in range(n_acc):
                    acc_refs[c][...] = parts[c]

            @pl.when(kk > 0)
            def _():
                for c in range(n_acc):
                    acc_refs[c][...] += parts[c]

            @pl.when(kk == nk - 1)
            def _():
                finish([acc_refs[c][...] for c in range(n_acc)])

    def ij(f):
        return (lambda j, i, q: f(i, j, q)) if n_outer else f

    a_spec = pl.BlockSpec((bk, bm), ij(lambda i, j, q: (q, i))) if ta else pl.BlockSpec((bm, bk), ij(lambda i, j, q: (i, q)))

    def b_spec(off):
        on, ok = off
        if tb:
            return pl.BlockSpec((bn, bk), ij(lambda i, j, q: (j + on, q + ok)))
        return pl.BlockSpec((bk, bn), ij(lambda i, j, q: (q + ok, j + on)))

    mn_spec = pl.BlockSpec((bm, bn), ij(lambda i, j, q: (i, j)))
    outs = pl.pallas_call(
        body,
        out_shape=[jax.ShapeDtypeStruct((m, n), d) for d in out_dtypes],
        grid=(n // bn, m // bm, nk) if n_outer else (m // bm, n // bn, nk),
        in_specs=[a_spec] * na + [b_spec(o) for o in b_off]
        + [pl.BlockSpec((bm, bn), ij(lambda i, j, q, o=o: (i, j + o))) for o in e_off],
        out_specs=[mn_spec] * len(out_dtypes),
        scratch_shapes=[pltpu.VMEM((bm, bn), F32) for _ in range(n_acc if nk > 1 else 0)],
        compiler_params=_cparams(("parallel", "parallel", "arbitrary")),
        name=name,
    )(*a_list, *b_list, *extras)
    return outs


def _first(accs, ex):
    return (accs[0],)


def _rowwise(fn, ins, consts, out_defs, sum_widths, *, bm, name):
    ins = [tuple(e) + (1,) * (4 - len(e)) for e in ins]
    out_defs = [tuple(e) + (1,) * (3 - len(e)) for e in out_defs]
    t = ins[0][0].shape[-2] * ins[0][3]
    bm = min(bm, t)
    assert t % bm == 0, (name, t, bm)
    ni, nc, no, ns = len(ins), len(consts), len(out_defs), len(sum_widths)
    strided = [w for _, w, _, d in ins if d > 1] + [w for w, _, d in out_defs if d > 1]

    def body(*refs):
        i_refs, c_refs = refs[:ni], refs[ni:ni + nc]
        o_refs, s_refs = refs[ni + nc:ni + nc + no], refs[ni + nc + no:ni + nc + no + ns]
        scratch = list(refs[ni + nc + no + ns:])
        vals = []
        for ref, (_, w, _, d) in zip(i_refs, ins):
            if d == 1:
                vals.append(ref[...])
                continue
            s = scratch.pop(0)
            for r in range(d):
                for c in range(w // 128):
                    s.at[c][pl.ds(r, bm // d, stride=d), :] = ref[r, :, c * 128:(c + 1) * 128].astype(F32)
            vals.append(jnp.concatenate([s[c] for c in range(w // 128)], axis=1))
        outs, sums = fn(vals, [r[...] for r in c_refs])
        for o_ref, o, (w, _, d) in zip(o_refs, outs, out_defs):
            if d == 1:
                o_ref[...] = o.astype(o_ref.dtype)
                continue
            s = scratch.pop(0)
            for c in range(w // 128):
                s[c] = o[:, c * 128:(c + 1) * 128].astype(F32)
            for r in range(d):
                for c in range(w // 128):
                    o_ref[r, :, c * 128:(c + 1) * 128] = s.at[c][pl.ds(r, bm // d, stride=d), :].astype(o_ref.dtype)
        if ns:
            first = pl.program_id(0) == 0

            @pl.when(first)
            def _():
                for s_ref, s in zip(s_refs, sums):
                    s_ref[...] = s

            @pl.when(jnp.logical_not(first))
            def _():
                for s_ref, s in zip(s_refs, sums):
                    s_ref[...] += s

    def win(width, cb, d):
        if d > 1:
            return pl.BlockSpec((d, bm // d, width), lambda i: (0, i, 0))
        return pl.BlockSpec((bm, width), lambda i: (i, cb))

    res = pl.pallas_call(
        body,
        out_shape=[jax.ShapeDtypeStruct((t, w) if d == 1 else (d, t // d, w), dt) for w, dt, d in out_defs]
        + [jax.ShapeDtypeStruct((8, w), F32) for w in sum_widths],
        grid=(t // bm,),
        in_specs=[win(w, cb, d) for _, w, cb, d in ins] + [pl.BlockSpec(c.shape, lambda i, nd=c.ndim: (0,) * nd) for c in consts],
        out_specs=[win(w, 0, d) for w, _, d in out_defs] + [pl.BlockSpec((8, w), lambda i: (0, 0)) for w in sum_widths],
        scratch_shapes=[pltpu.VMEM((w // 128, bm, 128), F32) for w in strided],
        compiler_params=_cparams(("arbitrary",) if ns else ("parallel",)),
        name=name,
    )(*[e[0] for e in ins], *consts)
    return res[:no], [jnp.sum(s, axis=0) for s in res[no:]]


def _colsum8(x):
    bm, w = x.shape
    return jnp.sum(x.reshape(bm // 8, 8, w), axis=0)


def _tri(n, upper=False):
    r = lax.broadcasted_iota(jnp.int32, (n, n), 0)
    c = lax.broadcasted_iota(jnp.int32, (n, n), 1)
    return (c >= r) if upper else (c <= r)


def _exact_tri_matmul(tri_bf16, x):
    x0 = x.astype(BF16)
    r1 = x - x0.astype(F32)
    x1 = r1.astype(BF16)
    x2 = (r1 - x1.astype(F32)).astype(BF16)
    w = x.shape[1]
    y = jnp.dot(tri_bf16, jnp.concatenate([x0, x1, x2], axis=1), preferred_element_type=F32)
    return y[:, :w] + y[:, w:2 * w] + y[:, 2 * w:]


def _dot_nt(a, b):
    return lax.dot_general(a, b, (((1,), (1,)), ((), ())), preferred_element_type=F32)


def _dot_tn(a, b):
    return lax.dot_general(a, b, (((0,), (0,)), ((), ())), preferred_element_type=F32)


def _dot(a, b):
    return jnp.dot(a, b, preferred_element_type=F32)


def _hg_gates(hq, hf, lb):
    sq = _sigmoid(hq)
    q = hq * sq
    sg = _sigmoid(hf)
    f = lb + (1.0 - lb) * sg
    return q, sq, sg, f


def _hg_intra(q, kk, g):
    c = q.shape[0]
    rows = lax.broadcasted_iota(jnp.int32, (c, 1), 0)
    a_rows, qts, kts, eqs, eks = [], [], [], [], []
    for i in range(c // HG_SUB):
        lo = i * HG_SUB
        ref = g[lo - 1:lo, :] if i else jnp.zeros_like(g[0:1, :])
        eq = jnp.exp(g[lo:lo + HG_SUB, :] - ref)
        ek = jnp.exp(jnp.where(rows < lo + HG_SUB, ref - g, 0.0))
        qt = q[lo:lo + HG_SUB, :] * eq
        kt = kk * ek
        a = _dot_nt(qt.astype(BF16), kt.astype(BF16))
        tpos = lo + lax.broadcasted_iota(jnp.int32, (HG_SUB, c), 0)
        spos = lax.broadcasted_iota(jnp.int32, (HG_SUB, c), 1)
        a_rows.append(jnp.where(spos <= tpos, a, 0.0))
        qts.append(qt), kts.append(kt), eqs.append(eq), eks.append(ek)
    return jnp.concatenate(a_rows, axis=0), qts, kts, eqs, eks


def _hgrn_fwd_serial(zh, lb3, *, tb=512):
    t = zh.shape[0]
    nh = lb3.shape[0]
    c = HG_CHUNK
    tb = min(tb, t)
    nchunk = tb // c
    hp = HG_HP if nh % HG_HP == 0 else 1

    def body(hq_ref, hf_ref, hi_ref, lb_ref, o_ref, st_ref, state):
        @pl.when(pl.program_id(1) == 0)
        def _():
            state[...] = jnp.zeros_like(state)

        tril = _tri(c).astype(BF16)

        def one_head(hh, ci, sl):
            ls = slice(hh * HG_DK, (hh + 1) * HG_DK)
            q, _, _, f = _hg_gates(hq_ref[sl, ls], hf_ref[sl, ls], lb_ref[hh])
            v = hi_ref[sl, ls]
            kk = 1.0 - f
            g = _exact_tri_matmul(tril, jnp.log(f))
            a, _, _, _, _ = _hg_intra(q, kk, g)
            st = state[hh]
            st_ref[hh, ci] = st
            vb = v.astype(BF16)
            o = _dot(a.astype(BF16), vb) + _dot_nt((q * jnp.exp(g)).astype(BF16), st.astype(BF16))
            o_ref[sl, ls] = o
            glast = g[c - 1:c, :]
            kg = kk * jnp.exp(glast - g)
            state[hh] = st * jnp.exp(glast) + _dot_tn(vb, kg.astype(BF16))

        def chunk(ci, carry):
            sl = pl.ds(pl.multiple_of(ci * c, c), c)
            for hh in range(hp):
                one_head(hh, ci, sl)
            return carry

        lax.fori_loop(0, nchunk, chunk, 0)

    def col(cb):
        return pl.BlockSpec((tb, hp * HG_DK), lambda h, i: (i, cb * (nh // hp) + h))

    return pl.pallas_call(
        body,
        out_shape=[jax.ShapeDtypeStruct((t, nh * HG_DK), F32), jax.ShapeDtypeStruct((nh, t // c, HG_DK, HG_DK), F32)],
        grid=(nh // hp, t // tb),
        in_specs=[col(0), col(1), col(2), pl.BlockSpec((hp, 1, HG_DK), lambda h, i: (h, 0, 0))],
        out_specs=[pl.BlockSpec((tb, hp * HG_DK), lambda h, i: (i, h)),
                   pl.BlockSpec((hp, nchunk, HG_DK, HG_DK), lambda h, i: (h, i, 0, 0))],
        scratch_shapes=[pltpu.VMEM((hp, HG_DK, HG_DK), F32)],
        compiler_params=_cparams(("parallel", "arbitrary")),
        name="hgrn_fwd",
    )(zh, zh, zh, lb3)


def _hgrn_bwd_serial(zh, lb3, states, d_o, *, tb=512):
    t = zh.shape[0]
    nh = lb3.shape[0]
    c = HG_CHUNK
    tb = min(tb, t)
    nchunk = tb // c
    nblk = t // tb
    hp = HG_HP if nh % HG_HP == 0 else 1

    def body(hq_ref, hf_ref, hi_ref, lb_ref, st_ref, do_ref, dq_ref, df_ref, dv_ref, dlb_ref, dstate):
        @pl.when(pl.program_id(1) == 0)
        def _():
            dstate[...] = jnp.zeros_like(dstate)
            dlb_ref[...] = jnp.zeros_like(dlb_ref)

        tril = _tri(c).astype(BF16)
        triu = _tri(c, upper=True).astype(BF16)
        last_row = lax.broadcasted_iota(jnp.int32, (c, 1), 0) == c - 1

        def one_head(hh, ci, sl):
            ls = slice(hh * HG_DK, (hh + 1) * HG_DK)
            lb = lb_ref[hh]
            hq, hf = hq_ref[sl, ls], hf_ref[sl, ls]
            q, sq, sg, f = _hg_gates(hq, hf, lb)
            v = hi_ref[sl, ls]
            kk = 1.0 - f
            g = _exact_tri_matmul(tril, jnp.log(f))
            a, qts, kts, eqs, eks = _hg_intra(q, kk, g)
            st = st_ref[hh, ci]
            dst = dstate[hh]
            do = do_ref[sl, ls]
            dob, vb = do.astype(BF16), v.astype(BF16)
            glast = g[c - 1:c, :]
            eg = jnp.exp(g)
            egl = jnp.exp(glast - g)
            qg = q * eg
            kg = kk * egl
            dv = _dot_tn(a.astype(BF16), dob) + _dot_nt(kg.astype(BF16), dst.astype(BF16))
            da = jnp.where(_tri(c), _dot_nt(dob, vb), 0.0).astype(BF16)
            dq_parts, dgq_parts = [], []
            dk = jnp.zeros_like(kk)
            dgk = jnp.zeros_like(kk)
            for i in range(c // HG_SUB):
                da_i = da[i * HG_SUB:(i + 1) * HG_SUB, :]
                ktb, qtb = kts[i].astype(BF16), qts[i].astype(BF16)
                xi = _dot(da_i, ktb)
                yi = _dot_tn(da_i, qtb)
                dq_parts.append(xi * eqs[i])
                dk = dk + yi * eks[i]
                dgq_parts.append(xi * qtb.astype(F32))
                dgk = dgk + yi * ktb.astype(F32)
            dq_inter = _dot(dob, st.astype(BF16)) * eg
            dq = jnp.concatenate(dq_parts, axis=0) + dq_inter
            dk_state = _dot(vb, dst.astype(BF16)) * egl
            dk = dk + dk_state
            dg = jnp.concatenate(dgq_parts, axis=0) - dgk + q * dq_inter - kk * dk_state
            dgl = jnp.sum(kk * dk_state, axis=0, keepdims=True) + jnp.exp(glast) * jnp.sum(st * dst, axis=0, keepdims=True)
            dg = dg + jnp.where(last_row, dgl, 0.0)
            dlogf = _exact_tri_matmul(triu, dg)
            dfv = dlogf / f - dk
            dq_ref[sl, ls] = (dq * (sq * (1.0 + hq * (1.0 - sq)))).astype(dq_ref.dtype)
            df_ref[sl, ls] = (dfv * (1.0 - lb) * sg * (1.0 - sg)).astype(df_ref.dtype)
            dv_ref[sl, ls] = dv.astype(dv_ref.dtype)
            dlb_ref[hh] += jnp.sum(dfv * (1.0 - sg), axis=0, keepdims=True)
            dstate[hh] = dst * jnp.exp(glast) + _dot_tn(dob, qg.astype(BF16))

        def chunk(j, carry):
            ci = nchunk - 1 - j
            sl = pl.ds(pl.multiple_of(ci * c, c), c)
            for hh in range(hp):
                one_head(hh, ci, sl)
            return carry

        lax.fori_loop(0, nchunk, chunk, 0)

    def col(cb):
        return pl.BlockSpec((tb, hp * HG_DK), lambda h, i: (nblk - 1 - i, cb * (nh // hp) + h))

    ocol = pl.BlockSpec((tb, hp * HG_DK), lambda h, i: (nblk - 1 - i, h))
    w = nh * HG_DK
    dq, df, dv, dlb = pl.pallas_call(
        body,
        out_shape=[jax.ShapeDtypeStruct((t, w), BF16)] * 3 + [jax.ShapeDtypeStruct((nh, 1, HG_DK), F32)],
        grid=(nh // hp, nblk),
        in_specs=[col(0), col(1), col(2), pl.BlockSpec((hp, 1, HG_DK), lambda h, i: (h, 0, 0)),
                  pl.BlockSpec((hp, nchunk, HG_DK, HG_DK), lambda h, i: (h, nblk - 1 - i, 0, 0)), ocol],
        out_specs=[ocol, ocol, ocol, pl.BlockSpec((hp, 1, HG_DK), lambda h, i: (h, 0, 0))],
        scratch_shapes=[pltpu.VMEM((hp, HG_DK, HG_DK), F32)],
        compiler_params=_cparams(("parallel", "arbitrary")),
        name="hgrn_bwd",
    )(zh, zh, zh, lb3, states, d_o)
    return dq, df, dv, dlb.reshape(w)


def _hg_heads(x, hp):
    return [x[:, h * HG_DK:(h + 1) * HG_DK] for h in range(hp)]


def _hg_intra_wide(q, kk, g, hp):
    c = q.shape[0]
    rows = lax.broadcasted_iota(jnp.int32, (c, 1), 0)
    a_rows = [[] for _ in range(hp)]
    qts, kts, eqs, eks = [], [], [], []
    for i in range(c // HG_SUB):
        lo = i * HG_SUB
        ref = g[lo - 1:lo, :] if i else jnp.zeros_like(g[0:1, :])
        eq = jnp.exp(g[lo:lo + HG_SUB, :] - ref)
        ek = jnp.exp(jnp.where(rows < lo + HG_SUB, ref - g, 0.0))
        qtb = (q[lo:lo + HG_SUB, :] * eq).astype(BF16)
        ktb = (kk * ek).astype(BF16)
        tpos = lo + lax.broadcasted_iota(jnp.int32, (HG_SUB, c), 0)
        spos = lax.broadcasted_iota(jnp.int32, (HG_SUB, c), 1)
        for h, (qh, kh) in enumerate(zip(_hg_heads(qtb, hp), _hg_heads(ktb, hp))):
            a_rows[h].append(jnp.where(spos <= tpos, _dot_nt(qh, kh), 0.0))
        qts.append(qtb), kts.append(ktb), eqs.append(eq), eks.append(ek)
    return [jnp.concatenate(r, axis=0) for r in a_rows], qts, kts, eqs, eks


def _hgrn_fwd(zh, lb3, *, tb=512):
    t = zh.shape[0]
    nh = lb3.shape[0]
    c = HG_CHUNK
    tb = min(tb, t)
    nchunk = tb // c
    hp = HG_HP if nh % HG_HP == 0 else 1
    wp = hp * HG_DK

    def body(hq_ref, hf_ref, hi_ref, lb_ref, o_ref, st_ref, state):
        @pl.when(pl.program_id(1) == 0)
        def _():
            state[...] = jnp.zeros_like(state)

        tril = _tri(c).astype(BF16)

        def chunk(ci, carry):
            sl = pl.ds(pl.multiple_of(ci * c, c), c)
            q, _, _, f = _hg_gates(hq_ref[sl, :], hf_ref[sl, :], lb_ref[...])
            kk = 1.0 - f
            g = _exact_tri_matmul(tril, jnp.log(f))
            a, _, _, _, _ = _hg_intra_wide(q, kk, g, hp)
            vb = hi_ref[sl, :].astype(BF16)
            glast = g[c - 1:c, :]
            qgb = (q * jnp.exp(g)).astype(BF16)
            kgb = (kk * jnp.exp(glast - g)).astype(BF16)
            dec = jnp.exp(glast)
            sts = [state[h] for h in range(hp)]
            for h in range(hp):
                st_ref[h, ci] = sts[h]
            vh, qgh, kgh, dech = _hg_heads(vb, hp), _hg_heads(qgb, hp), _hg_heads(kgb, hp), _hg_heads(dec, hp)
            o = [_dot(a[h].astype(BF16), vh[h]) + _dot_nt(qgh[h], sts[h].astype(BF16)) for h in range(hp)]
            new = [_dot_tn(vh[h], kgh[h]) for h in range(hp)]
            o_ref[sl, :] = jnp.concatenate(o, axis=1)
            for h in range(hp):
                state[h] = sts[h] * dech[h] + new[h]
            return carry

        lax.fori_loop(0, nchunk, chunk, 0)

    def col(cb):
        return pl.BlockSpec((tb, wp), lambda h, i: (i, cb * (nh // hp) + h))

    return pl.pallas_call(
        body,
        out_shape=[jax.ShapeDtypeStruct((t, nh * HG_DK), F32), jax.ShapeDtypeStruct((nh, t // c, HG_DK, HG_DK), F32)],
        grid=(nh // hp, t // tb),
        in_specs=[col(0), col(1), col(2), pl.BlockSpec((1, wp), lambda h, i: (0, h))],
        out_specs=[pl.BlockSpec((tb, wp), lambda h, i: (i, h)),
                   pl.BlockSpec((hp, nchunk, HG_DK, HG_DK), lambda h, i: (h, i, 0, 0))],
        scratch_shapes=[pltpu.VMEM((hp, HG_DK, HG_DK), F32)],
        compiler_params=_cparams(("parallel", "arbitrary")),
        name="hgrn_fwd",
    )(zh, zh, zh, lb3.reshape(1, -1))


def _hgrn_bwd(zh, lb3, states, d_o, *, tb=512):
    t = zh.shape[0]
    nh = lb3.shape[0]
    c = HG_CHUNK
    tb = min(tb, t)
    nchunk = tb // c
    nblk = t // tb
    hp = HG_HP if nh % HG_HP == 0 else 1
    wp = hp * HG_DK

    def body(hq_ref, hf_ref, hi_ref, lb_ref, st_ref, do_ref, dq_ref, df_ref, dv_ref, dlb_ref, dstate):
        @pl.when(pl.program_id(1) == 0)
        def _():
            dstate[...] = jnp.zeros_like(dstate)
            dlb_ref[...] = jnp.zeros_like(dlb_ref)

        tril = _tri(c).astype(BF16)
        triu = _tri(c, upper=True).astype(BF16)
        last_row = lax.broadcasted_iota(jnp.int32, (c, 1), 0) == c - 1
        heads = range(hp)

        def chunk(j, carry):
            ci = nchunk - 1 - j
            sl = pl.ds(pl.multiple_of(ci * c, c), c)
            lb = lb_ref[...]
            hq, hf = hq_ref[sl, :], hf_ref[sl, :]
            q, sq, sg, f = _hg_gates(hq, hf, lb)
            kk = 1.0 - f
            g = _exact_tri_matmul(tril, jnp.log(f))
            a, qts, kts, eqs, eks = _hg_intra_wide(q, kk, g, hp)
            glast = g[c - 1:c, :]
            eg, egl, dec = jnp.exp(g), jnp.exp(glast - g), jnp.exp(glast)
            vb, dob = hi_ref[sl, :].astype(BF16), do_ref[sl, :].astype(BF16)
            qgb, kgb = (q * eg).astype(BF16), (kk * egl).astype(BF16)
            sts = [st_ref[h, ci] for h in heads]
            dsts = [dstate[h] for h in heads]
            stb, dstb = [s.astype(BF16) for s in sts], [s.astype(BF16) for s in dsts]
            vh, doh, qgh, kgh = _hg_heads(vb, hp), _hg_heads(dob, hp), _hg_heads(qgb, hp), _hg_heads(kgb, hp)
            dv = [_dot_tn(a[h].astype(BF16), doh[h]) + _dot_nt(kgh[h], dstb[h]) for h in heads]
            da = [jnp.where(_tri(c), _dot_nt(doh[h], vh[h]), 0.0).astype(BF16) for h in heads]
            dq_inter = jnp.concatenate([_dot(doh[h], stb[h]) for h in heads], axis=1) * eg
            dk_state = jnp.concatenate([_dot(vh[h], dstb[h]) for h in heads], axis=1) * egl
            new_dst = [_dot_tn(doh[h], qgh[h]) for h in heads]
            xs, dk, dgk = [], dk_state, 0.0
            for i in range(c // HG_SUB):
                rs = slice(i * HG_SUB, (i + 1) * HG_SUB)
                kth, qth = _hg_heads(kts[i], hp), _hg_heads(qts[i], hp)
                xi = jnp.concatenate([_dot(da[h][rs, :], kth[h]) for h in heads], axis=1)
                yi = jnp.concatenate([_dot_tn(da[h][rs, :], qth[h]) for h in heads], axis=1)
                xs.append(xi)
                dk = dk + yi * eks[i]
                dgk = dgk + yi * kts[i].astype(F32)
            dq = jnp.concatenate([x * e for x, e in zip(xs, eqs)], axis=0) + dq_inter
            dgq = jnp.concatenate([x * qt.astype(F32) for x, qt in zip(xs, qts)], axis=0)
            dg = dgq - dgk + q * dq_inter - kk * dk_state
            sdot = jnp.concatenate([jnp.sum(sts[h] * dsts[h], axis=0, keepdims=True) for h in heads], axis=1)
            dgl = jnp.sum(kk * dk_state, axis=0, keepdims=True) + dec * sdot
            dg = dg + jnp.where(last_row, dgl, 0.0)
            dlogf = _exact_tri_matmul(triu, dg)
            dfv = dlogf / f - dk
            dq_ref[sl, :] = (dq * (sq * (1.0 + hq * (1.0 - sq)))).astype(dq_ref.dtype)
            df_ref[sl, :] = (dfv * (1.0 - lb) * sg * (1.0 - sg)).astype(df_ref.dtype)
            dv_ref[sl, :] = jnp.concatenate(dv, axis=1).astype(dv_ref.dtype)
            dlb_ref[...] += jnp.sum(dfv * (1.0 - sg), axis=0, keepdims=True)
            dech = _hg_heads(dec, hp)
            for h in heads:
                dstate[h] = dsts[h] * dech[h] + new_dst[h]
            return carry

        lax.fori_loop(0, nchunk, chunk, 0)

    def col(cb):
        return pl.BlockSpec((tb, wp), lambda h, i: (nblk - 1 - i, cb * (nh // hp) + h))

    ocol = pl.BlockSpec((tb, wp), lambda h, i: (nblk - 1 - i, h))
    lbspec = pl.BlockSpec((1, wp), lambda h, i: (0, h))
    w = nh * HG_DK
    dq, df, dv, dlb = pl.pallas_call(
        body,
        out_shape=[jax.ShapeDtypeStruct((t, w), BF16)] * 3 + [jax.ShapeDtypeStruct((1, w), F32)],
        grid=(nh // hp, nblk),
        in_specs=[col(0), col(1), col(2), lbspec,
                  pl.BlockSpec((hp, nchunk, HG_DK, HG_DK), lambda h, i: (h, nblk - 1 - i, 0, 0)), ocol],
        out_specs=[ocol, ocol, ocol, lbspec],
        scratch_shapes=[pltpu.VMEM((hp, HG_DK, HG_DK), F32)],
        compiler_params=_cparams(("parallel", "arbitrary")),
        name="hgrn_bwd",
    )(zh, zh, zh, lb3.reshape(1, -1), states, d_o)
    return dq, df, dv, dlb.reshape(w)


NEG = -1e30
ATT_GW = ATT_HEADS * ATT_DH


def _att_scores(q, kp, kc, has_prev):
    scale = ATT_DH ** -0.5
    i = lax.broadcasted_iota(jnp.int32, (ATT_BLK, ATT_BLK), 0)
    j = lax.broadcasted_iota(jnp.int32, (ATT_BLK, ATT_BLK), 1)
    s_p = jnp.where(jnp.logical_and(j >= i, has_prev), _dot_nt(q, kp) * scale, NEG)
    s_c = jnp.where(j <= i, _dot_nt(q, kc) * scale, NEG)
    return s_p, s_c


def _att_views(arrs, d):
    return [a.reshape(d, -1, ATT_GW) for a in arrs]


def _att_unview(a, d):
    return a.reshape(-1, ATT_GW) if d == 1 else a


ATT_QB = 4


def _attn_fwd(qb, kb, vb, g):
    d = ATT_PATTERNS[g][1]
    q2, k2, v2 = _att_views([qb, kb, vb], d)
    nblk = q2.shape[1] // ATT_BLK
    nq = ATT_QB if nblk % ATT_QB == 0 else 1
    rows = nq * ATT_BLK

    def body(q_ref, kc_ref, kp_ref, vc_ref, vp_ref, o_ref, l_ref):
        first = pl.program_id(1) == 0
        hss = [slice(h * ATT_DH, (h + 1) * ATT_DH) for h in range(ATT_HEADS)]
        for b in range(nq):
            rs = slice(b * ATT_BLK, (b + 1) * ATT_BLK)
            ps = slice((b - 1) * ATT_BLK, b * ATT_BLK)
            has_prev = jnp.logical_not(first) if b == 0 else True
            kv = [(kp_ref[:, hs], vp_ref[:, hs]) if b == 0 else (kc_ref[ps, hs], vc_ref[ps, hs]) for hs in hss]
            sc = [_att_scores(q_ref[rs, hs], kv[h][0], kc_ref[rs, hs], has_prev) for h, hs in enumerate(hss)]
            ms = [jnp.maximum(jnp.max(s_p, axis=1, keepdims=True), jnp.max(s_c, axis=1, keepdims=True)) for s_p, s_c in sc]
            ps_ = [(jnp.exp(s_p - m), jnp.exp(s_c - m)) for (s_p, s_c), m in zip(sc, ms)]
            ls = [jnp.sum(p_p, axis=1, keepdims=True) + jnp.sum(p_c, axis=1, keepdims=True) for p_p, p_c in ps_]
            os_ = [_dot(p_p.astype(BF16), kv[h][1]) + _dot(p_c.astype(BF16), vc_ref[rs, hss[h]]) for h, (p_p, p_c) in enumerate(ps_)]
            for h, hs in enumerate(hss):
                o_ref[rs, hs] = os_[h] / ls[h]
                l_ref[rs, hs] = jnp.broadcast_to(ms[h] + jnp.log(ls[h]), (ATT_BLK, ATT_DH))

    cur = pl.BlockSpec((None, rows, ATT_GW), lambda r, n: (r, n, 0))
    prev = pl.BlockSpec((None, ATT_BLK, ATT_GW), lambda r, n: (r, jnp.maximum(n * nq - 1, 0), 0))
    o, lse = pl.pallas_call(
        body,
        out_shape=[jax.ShapeDtypeStruct(q2.shape, F32)] * 2,
        grid=(d, nblk // nq),
        in_specs=[cur, cur, prev, cur, prev],
        out_specs=[cur, cur],
        compiler_params=_cparams(("parallel", "arbitrary")),
        name=f"attn_fwd_g{g}",
    )(q2, k2, k2, v2, v2)
    return _att_unview(o, d), _att_unview(lse, d)


def _attn_bwd(qb, kb, vb, o, lse, d_o, d_lse, g):
    d = ATT_PATTERNS[g][1]
    q2, k2, v2 = _att_views([qb, kb, vb], d)
    o2, l2, do2, dl2 = _att_views([o, lse, d_o, d_lse], d)
    nblk = q2.shape[1] // ATT_BLK
    nq = ATT_QB if nblk % ATT_QB == 0 else 1
    rows = nq * ATT_BLK
    ns = nblk // nq
    scale = ATT_DH ** -0.5

    def body(q_ref, kc_ref, kp_ref, vc_ref, vp_ref, o_ref, l_ref, do_ref, dl_ref, dq_ref, dk_ref, dv_ref, ck, cv):
        n = pl.program_id(1)

        @pl.when(n == 0)
        def _():
            ck[...] = jnp.zeros_like(ck)
            cv[...] = jnp.zeros_like(cv)

        first = n == ns - 1
        hss = [slice(h * ATT_DH, (h + 1) * ATT_DH) for h in range(ATT_HEADS)]
        heads = range(ATT_HEADS)
        pend_k, pend_v = [ck[:, hs] for hs in hss], [cv[:, hs] for hs in hss]
        for b in reversed(range(nq)):
            rs = slice(b * ATT_BLK, (b + 1) * ATT_BLK)
            ps = slice((b - 1) * ATT_BLK, b * ATT_BLK)
            has_prev = jnp.logical_not(first) if b == 0 else True
            q = [q_ref[rs, hs] for hs in hss]
            kc, vc = [kc_ref[rs, hs] for hs in hss], [vc_ref[rs, hs] for hs in hss]
            kp = [kp_ref[:, hs] if b == 0 else kc_ref[ps, hs] for hs in hss]
            vp = [vp_ref[:, hs] if b == 0 else vc_ref[ps, hs] for hs in hss]
            sc = [_att_scores(q[h], kp[h], kc[h], has_prev) for h in heads]
            dob = [do_ref[rs, hs].astype(BF16) for hs in hss]
            dp = [(_dot_nt(dob[h], vp[h]), _dot_nt(dob[h], vc[h])) for h in heads]
            delta = [jnp.sum(do_ref[rs, hs] * o_ref[rs, hs] - dl_ref[rs, hs], axis=1, keepdims=True) for hs in hss]
            pr = [(jnp.exp(sc[h][0] - l_ref[rs, hss[h]][:, 0:1]), jnp.exp(sc[h][1] - l_ref[rs, hss[h]][:, 0:1])) for h in heads]
            ds = [((pr[h][0] * (dp[h][0] - delta[h]) * scale).astype(BF16), (pr[h][1] * (dp[h][1] - delta[h]) * scale).astype(BF16))
                  for h in heads]
            pb = [(pr[h][0].astype(BF16), pr[h][1].astype(BF16)) for h in heads]
            dq = [_dot(ds[h][0], kp[h]) + _dot(ds[h][1], kc[h]) for h in heads]
            dk_c = [_dot_tn(ds[h][1], q[h]) for h in heads]
            dv_c = [_dot_tn(pb[h][1], dob[h]) for h in heads]
            dk_p = [_dot_tn(ds[h][0], q[h]) for h in heads]
            dv_p = [_dot_tn(pb[h][0], dob[h]) for h in heads]
            for h, hs in enumerate(hss):
                dq_ref[rs, hs] = dq[h]
                dk_ref[rs, hs] = pend_k[h] + dk_c[h]
                dv_ref[rs, hs] = pend_v[h] + dv_c[h]
            pend_k, pend_v = dk_p, dv_p
        for h, hs in enumerate(hss):
            ck[:, hs] = pend_k[h]
            cv[:, hs] = pend_v[h]

    cur = pl.BlockSpec((None, rows, ATT_GW), lambda r, n: (r, ns - 1 - n, 0))
    prev = pl.BlockSpec((None, ATT_BLK, ATT_GW), lambda r, n: (r, jnp.maximum((ns - 1 - n) * nq - 1, 0), 0))
    shp = jax.ShapeDtypeStruct(q2.shape, F32)
    dq, dk, dv = pl.pallas_call(
        body,
        out_shape=[shp, shp, shp],
        grid=(d, ns),
        in_specs=[cur, cur, prev, cur, prev, cur, cur, cur, cur],
        out_specs=[cur, cur, cur],
        scratch_shapes=[pltpu.VMEM((ATT_BLK, ATT_GW), F32), pltpu.VMEM((ATT_BLK, ATT_GW), F32)],
        compiler_params=_cparams(("parallel", "arbitrary")),
        name=f"attn_bwd_g{g}",
    )(q2, k2, k2, v2, v2, o2, l2, do2, dl2)
    return _att_unview(dq, d), _att_unview(dk, d), _att_unview(dv, d)


def _attn_fwd_1blk(qb, kb, vb, g):
    d = ATT_PATTERNS[g][1]
    q2, k2, v2 = _att_views([qb, kb, vb], d)
    nb = q2.shape[1] // ATT_BLK

    def body(q_ref, kc_ref, kp_ref, vc_ref, vp_ref, o_ref, l_ref):
        has_prev = pl.program_id(1) > 0
        for h in range(ATT_HEADS):
            hs = slice(h * ATT_DH, (h + 1) * ATT_DH)
            s_p, s_c = _att_scores(q_ref[:, hs], kp_ref[:, hs], kc_ref[:, hs], has_prev)
            m = jnp.maximum(jnp.max(s_p, axis=1, keepdims=True), jnp.max(s_c, axis=1, keepdims=True))
            p_p, p_c = jnp.exp(s_p - m), jnp.exp(s_c - m)
            l = jnp.sum(p_p, axis=1, keepdims=True) + jnp.sum(p_c, axis=1, keepdims=True)
            o = _dot(p_p.astype(BF16), vp_ref[:, hs]) + _dot(p_c.astype(BF16), vc_ref[:, hs])
            o_ref[:, hs] = o / l
            l_ref[:, hs] = jnp.broadcast_to(m + jnp.log(l), (ATT_BLK, ATT_DH))

    cur = pl.BlockSpec((None, ATT_BLK, ATT_GW), lambda r, n: (r, n, 0))
    prev = pl.BlockSpec((None, ATT_BLK, ATT_GW), lambda r, n: (r, jnp.maximum(n - 1, 0), 0))
    o, lse = pl.pallas_call(
        body,
        out_shape=[jax.ShapeDtypeStruct(q2.shape, F32)] * 2,
        grid=(d, nb),
        in_specs=[cur, cur, prev, cur, prev],
        out_specs=[cur, cur],
        compiler_params=_cparams(("parallel", "arbitrary")),
        name=f"attn_fwd_g{g}",
    )(q2, k2, k2, v2, v2)
    return _att_unview(o, d), _att_unview(lse, d)


def _attn_bwd_1blk(qb, kb, vb, o, lse, d_o, d_lse, g):
    d = ATT_PATTERNS[g][1]
    q2, k2, v2 = _att_views([qb, kb, vb], d)
    o2, l2, do2, dl2 = _att_views([o, lse, d_o, d_lse], d)
    nb = q2.shape[1] // ATT_BLK

    def body(q_ref, kc_ref, kp_ref, vc_ref, vp_ref, o_ref, l_ref, do_ref, dl_ref, dq_ref, dk_ref, dv_ref, ck, cv):
        n = pl.program_id(1)
        active = n < nb

        @pl.when(n == 0)
        def _():
            ck[...] = jnp.zeros_like(ck)
            cv[...] = jnp.zeros_like(cv)

        @pl.when(jnp.logical_not(active))
        def _():
            dk_ref[...] = ck[...]
            dv_ref[...] = cv[...]

        @pl.when(active)
        def _():
            has_prev = n > 0
            for h in range(ATT_HEADS):
                hs = slice(h * ATT_DH, (h + 1) * ATT_DH)
                q, kp, kc, vp, vc = q_ref[:, hs], kp_ref[:, hs], kc_ref[:, hs], vp_ref[:, hs], vc_ref[:, hs]
                s_p, s_c = _att_scores(q, kp, kc, has_prev)
                lse_h = l_ref[:, hs][:, 0:1]
                p_p, p_c = jnp.exp(s_p - lse_h), jnp.exp(s_c - lse_h)
                do = do_ref[:, hs]
                delta = jnp.sum(do * o_ref[:, hs] - dl_ref[:, hs], axis=1, keepdims=True)
                dob = do.astype(BF16)
                scale = ATT_DH ** -0.5
                ds_p = (p_p * (_dot_nt(dob, vp) - delta) * scale).astype(BF16)
                ds_c = (p_c * (_dot_nt(dob, vc) - delta) * scale).astype(BF16)
                dq_ref[:, hs] = _dot(ds_p, kp) + _dot(ds_c, kc)
                dk_ref[:, hs] = ck[:, hs] + _dot_tn(ds_p, q)
                dv_ref[:, hs] = cv[:, hs] + _dot_tn(p_p.astype(BF16), dob)
                ck[:, hs] = _dot_tn(ds_c, q)
                cv[:, hs] = _dot_tn(p_c.astype(BF16), dob)

    def qn(n):
        return jnp.minimum(n, nb - 1)

    cur = pl.BlockSpec((None, ATT_BLK, ATT_GW), lambda r, n: (r, qn(n), 0))
    prev = pl.BlockSpec((None, ATT_BLK, ATT_GW), lambda r, n: (r, jnp.maximum(qn(n) - 1, 0), 0))
    behind = pl.BlockSpec((None, ATT_BLK, ATT_GW), lambda r, n: (r, jnp.maximum(n - 1, 0), 0))
    shp = jax.ShapeDtypeStruct(q2.shape, F32)
    dq, dk, dv = pl.pallas_call(
        body,
        out_shape=[shp, shp, shp],
        grid=(d, nb + 1),
        in_specs=[cur, cur, prev, cur, prev, cur, cur, cur, cur],
        out_specs=[cur, behind, behind],
        scratch_shapes=[pltpu.VMEM((ATT_BLK, ATT_GW), F32), pltpu.VMEM((ATT_BLK, ATT_GW), F32)],
        compiler_params=_cparams(("parallel", "arbitrary")),
        name=f"attn_bwd_g{g}",
    )(q2, k2, k2, v2, v2, o2, l2, do2, dl2)
    return _att_unview(dq, d), _att_unview(dk, d), _att_unview(dv, d)


def _rms_parts(x, width):
    outs = []
    for lo in range(0, x.shape[1], width):
        xs = x[:, lo:lo + width].astype(F32)
        r = lax.rsqrt(jnp.mean(xs * xs, axis=1, keepdims=True) + EPS)
        outs.append((xs * r, r))
    return outs


def _rms_bwd_part(xh, r, dxh):
    return r * (dxh - xh * jnp.mean(dxh * xh, axis=1, keepdims=True))


def _norm_fwd(x, gain):
    d = x.shape[1]

    def fn(ins, consts):
        (xh, _), = _rms_parts(ins[0], d)
        return [xh * consts[0]], []

    (h,), _ = _rowwise(fn, [(x, d, 0)], [gain.reshape(1, d)], [(d, BF16)], [], bm=512, name="norm_fwd")
    return h


def _norm_bwd(x, gain, dh, dres):
    d = x.shape[1]

    def fn(ins, consts):
        (xh, r), = _rms_parts(ins[0], d)
        dx = ins[2] + _rms_bwd_part(xh, r, ins[1] * consts[0])
        return [dx], [_colsum8(ins[1] * xh)]

    (dx,), (dg,) = _rowwise(fn, [(x, d, 0), (dh, d, 0), (dres, d, 0)], [gain.reshape(1, d)], [(d, F32)], [d],
                            bm=512, name="norm_bwd")
    return dx, dg


def _rot_sign():
    lane = lax.broadcasted_iota(jnp.int32, (1, ATT_DH), 1)
    return jnp.where(lane < ATT_DH // 2, -1.0, 1.0).astype(F32)


def _rope(y, cos, sin):
    return y * cos + pltpu.roll(y, ATT_DH // 2, axis=1) * _rot_sign() * sin


def _rope_t(dy, cos, sin):
    return dy * cos - pltpu.roll(dy * sin, ATT_DH // 2, axis=1) * _rot_sign()


def _qk_prep(zq, zk, zv, qn, kn, cos, sin):
    w = zq.shape[1]

    def fn(ins, consts):
        cs, sn = ins[3], ins[4]
        outs = []
        for z, gain in ((ins[0], consts[0]), (ins[1], consts[1])):
            for i, (xh, _) in enumerate(_rms_parts(z, ATT_DH)):
                outs.append(_rope(xh * gain[:, i * ATT_DH:(i + 1) * ATT_DH], cs, sn))
        outs += [ins[2][:, i * ATT_DH:(i + 1) * ATT_DH] for i in range(w // ATT_DH)]
        groups = [jnp.concatenate(outs[i:i + ATT_HEADS], axis=1) for i in range(0, len(outs), ATT_HEADS)]
        return groups, []

    outs, _ = _rowwise(fn, [(zq, w, 0), (zk, w, 0), (zv, w, 0), (cos, ATT_DH, 0), (sin, ATT_DH, 0)], [qn, kn],
                       [(ATT_GW, BF16, ATT_PATTERNS[g][1]) for g in range(ATT_GROUPS)] * 3, [], bm=256, name="qk_prep")
    return outs[0:3], outs[3:6], outs[6:9]


def _qk_prep_bwd(zq, zk, dq_g, dk_g, dv_g, qn, kn, cos, sin):
    w = zq.shape[1]

    def fn(ins, consts):
        cs, sn = ins[2], ins[3]
        outs, sums = [], []
        for z, gain, dparts in ((ins[0], consts[0], ins[4:7]), (ins[1], consts[1], ins[7:10])):
            dout = jnp.concatenate(dparts, axis=1)
            dz, dgain = [], []
            for i, (xh, r) in enumerate(_rms_parts(z, ATT_DH)):
                hs = slice(i * ATT_DH, (i + 1) * ATT_DH)
                dy = _rope_t(dout[:, hs], cs, sn)
                dgain.append(_colsum8(dy * xh))
                dz.append(_rms_bwd_part(xh, r, dy * gain[:, hs]))
            outs.append(jnp.concatenate(dz, axis=1))
            sums.append(jnp.concatenate(dgain, axis=1))
        outs.append(jnp.concatenate(ins[10:13], axis=1))
        return outs, sums

    ins = [(zq, w, 0), (zk, w, 0), (cos, ATT_DH, 0), (sin, ATT_DH, 0)]
    for parts in (dq_g, dk_g, dv_g):
        ins += [(a, ATT_GW, 0, ATT_PATTERNS[g][1]) for g, a in enumerate(parts)]
    (dzq, dzk, dzv), (dqn, dkn) = _rowwise(fn, ins, [qn, kn], [(w, BF16)] * 3, [w, w], bm=256, name="qk_prep_bwd")
    return dzq, dzk, dzv, dqn, dkn


def _post_a(o_raw, zh, gout):
    w = o_raw.shape[1]

    def fn(ins, consts):
        oh = jnp.concatenate([xh for xh, _ in _rms_parts(ins[0], HG_DK)], axis=1)
        hg = ins[1]
        return [oh * consts[0] * (hg * _sigmoid(hg))], []

    (y,), _ = _rowwise(fn, [(o_raw, w, 0), (zh, w, 3)], [gout.reshape(1, w)], [(w, BF16)], [], bm=512, name="post_a")
    return y


def _post_a_bwd(o_raw, zh, gout, dy):
    w = o_raw.shape[1]

    def fn(ins, consts):
        parts = _rms_parts(ins[0], HG_DK)
        oh = jnp.concatenate([xh for xh, _ in parts], axis=1)
        hg, dyv, gain = ins[1], ins[2], consts[0]
        sg = _sigmoid(hg)
        s = hg * sg
        doh = dyv * gain * s
        do = jnp.concatenate([_rms_bwd_part(xh, r, doh[:, i * HG_DK:(i + 1) * HG_DK]) for i, (xh, r) in enumerate(parts)], axis=1)
        dhg = dyv * oh * gain * (sg * (1.0 + hg * (1.0 - sg)))
        return [do, dhg], [_colsum8(dyv * oh * s)]

    (do, dhg), (dgain,) = _rowwise(fn, [(o_raw, w, 0), (zh, w, 3), (dy, w, 0)], [gout.reshape(1, w)],
                                   [(w, F32), (w, BF16)], [w], bm=512, name="post_a_bwd")
    return do, dhg, dgain


def _merge_alpha(lses):
    m = jnp.maximum(jnp.maximum(lses[0], lses[1]), lses[2])
    e = [jnp.exp(l - m) for l in lses]
    inv = 1.0 / (e[0] + e[1] + e[2])
    return [x * inv for x in e]


def _group_ins(parts):
    return [(a, ATT_GW, 0, ATT_PATTERNS[g][1]) for g, a in enumerate(parts)]


def _merge_b(o_g, lse_g):
    def fn(ins, consts):
        al = _merge_alpha(ins[3:6])
        return [al[0] * ins[0] + al[1] * ins[1] + al[2] * ins[2]], []

    (y,), _ = _rowwise(fn, _group_ins(o_g) + _group_ins(lse_g), [], [(ATT_GW, BF16)], [], bm=512, name="merge_b")
    return y


def _merge_b_bwd(o_g, lse_g, dy):
    def fn(ins, consts):
        al = _merge_alpha(ins[3:6])
        dyv = ins[6]
        dal = [dyv * ins[i] for i in range(3)]
        tot = al[0] * dal[0] + al[1] * dal[1] + al[2] * dal[2]
        return [al[i] * dyv for i in range(3)] + [al[i] * (dal[i] - tot) for i in range(3)], []

    outs, _ = _rowwise(fn, _group_ins(o_g) + _group_ins(lse_g) + [(dy, ATT_GW, 0)], [],
                       [(ATT_GW, F32, ATT_PATTERNS[g][1]) for g in range(ATT_GROUPS)] * 2, [], bm=512, name="merge_b_bwd")
    return outs[:3], outs[3:]


def _loss_head(y, target):
    d = y.shape[1]

    def fn(ins, consts):
        e = ins[0] - ins[1]
        return [e * (1.0 / d)], [_colsum8(e * e)]

    (dy,), (sq,) = _rowwise(fn, [(y, d, 0), (target, d, 0)], [], [(d, F32)], [d], bm=512, name="loss_head")
    return 0.5 * jnp.sum(sq) / d, dy


def _silu_grad(a):
    s = _sigmoid(a)
    return s * (1.0 + a * (1.0 - s))


def _ffn_fwd(x, gain, wt, wo_fn, tag):
    t, d = x.shape
    f = wt.shape[0] // 2
    h = _norm_fwd(x, gain)

    def act(accs, ex):
        a, b = accs
        s = _sigmoid(a)
        sa = a * s
        return (sa * b, b, 0.5 * sa, 0.5 * (s + sa * (1.0 - s)))

    bn = FFN_BN if f % FFN_BN == 0 else 256
    u, b, sa, sp = _mm([h], [wt, wt], [(0, 0, 0), (0, 1, 1)], 2, act, [BF16] * 4, m=t, n=f, k=d, tb=True,
                       bm=512, bn=bn, bk=d, b_off=[(0, 0), (f // min(bn, f), 0)], n_outer=True, name=f"ffn_in_{tag}")
    wo = wo_fn(u)
    (y,) = _mm([u], [wo], [(0, 0, 0)], 1, lambda accs, ex: (ex[0] + 0.5 * accs[0],), [F32], m=t, n=d, k=f,
               bm=512, bn=d, bk=f, extras=[x], name=f"ffn_out_{tag}")
    return y, (x, h, u, b, sa, sp, wo)


def _ffn_bwd(dy, saved, gain, wt, tag, tok, emit):
    x, h, u, b, sa, sp, wo = saved
    t, d = x.shape
    f = wo.shape[0]
    dyb = (dy + tok).astype(BF16)

    def dact(accs, ex):
        bv, sav, spv = (e.astype(F32) for e in ex)
        return (accs[0] * bv * spv, accs[0] * sav)

    bn = FFN_BN if f % FFN_BN == 0 else 256
    da, db = _mm([dyb], [wo], [(0, 0, 0)], 1, dact, [BF16, BF16], m=t, n=f, k=d, tb=True, bm=512, bn=bn, bk=d,
                 extras=[b, sa, sp], n_outer=True, name=f"ffn_dact_{tag}")
    (dwo,) = _mm([u], [dyb], [(0, 0, 0)], 1, lambda accs, ex: (0.5 * accs[0],), [BF16], m=f, n=d, k=t, ta=True,
                 bm=1408, bn=d, bk=1024, name=f"ffn_dwo_{tag}")
    dwt = [_mm([g], [h], [(0, 0, 0)], 1, _first, [BF16], m=f, n=d, k=t, ta=True, bm=1408, bn=d, bk=1024,
               name=f"ffn_dwt{i}_{tag}")[0] for i, g in enumerate((da, db))]
    tok = emit(jnp.concatenate(dwt, axis=0), dwo)
    (dh,) = _mm([da, db], [wt, wt], [(0, 0, 0), (1, 1, 0)], 1, _first, [F32], m=t, n=d, k=f, bm=512, bn=d, bk=f,
                b_off=[(0, 0), (0, 1)], name=f"ffn_dh_{tag}")
    dx, dgain = _norm_bwd(x, gain + tok, dh, dy)
    return dx, dgain, tok


FFN_BN = 1408
Z_SPLITS = (("h", 4096), ("q", 1536), ("k", 1536), ("v", 1536), ("g", 2048))


def _mix_fwd(x, p, cos, sin):
    t, d = x.shape
    hm = _norm_fwd(x, p["gm"])
    z, off = {}, 0
    for nm, width in Z_SPLITS:
        bn = 1024 if off % 1024 == 0 and width % 1024 == 0 else 512
        (z[nm],) = _mm([hm], [p["wint"]], [(0, 0, 0)], 1, _first, [F32 if nm == "h" else BF16], m=t, n=width, k=d, tb=True, bm=1024, bn=bn, bk=d,
                       b_off=[(off // bn, 0)], name=f"mix_in_{nm}")
        off += width
    o_raw, states = _hgrn_fwd(z["h"], p["lb3"])
    qb, kb, vb = _qk_prep(z["q"], z["k"], z["v"], p["qn"], p["kn"], cos, sin)
    o_g, lse_g = zip(*[_attn_fwd(qb[g], kb[g], vb[g], g) for g in range(ATT_GROUPS)])
    oa = _post_a(o_raw, z["h"], p["gout"])
    ob = _merge_b(o_g, lse_g)
    late = p["late"](ob)
    p = dict(p, **late)
    (ya,) = _mm([oa], [p["wa"]], [(0, 0, 0)], 1, _first, [F32], m=t, n=d, k=oa.shape[1], bm=1024, bn=d, bk=oa.shape[1],
                name="branch_a")

    def gate(accs, ex):
        return (_sigmoid(ex[0].astype(F32)) * ex[2] + _sigmoid(ex[1].astype(F32)) * accs[0], accs[0])

    merged, yb = _mm([ob], [p["wbt"]], [(0, 0, 0)], 1, gate, [BF16, F32], m=t, n=d, k=ATT_GW, tb=True, bm=512, bn=d,
                     bk=ATT_GW, extras=[z["g"], z["g"], ya], e_off=[0, 1, 0], name="branch_b_gate")
    (y,) = _mm([merged], [p["wo"]], [(0, 0, 0)], 1, lambda accs, ex: (ex[0] + accs[0],), [F32], m=t, n=d, k=d,
               bm=1024, bn=d, bk=d, extras=[x], name="mix_out")
    return y, (x, hm, z, o_raw, states, qb, kb, vb, o_g, lse_g, oa, ob, ya, yb, merged, late)


def _mix_bwd(dy, saved, p, cos, sin, tok):
    x, hm, z, o_raw, states, qb, kb, vb, o_g, lse_g, oa, ob, ya, yb, merged, late = saved
    p = dict(p, **late)
    t, d = x.shape
    w = oa.shape[1]
    dyb = (dy + tok).astype(BF16)

    def dgate(accs, ex):
        dm = accs[0]
        sa, sb = _sigmoid(ex[0].astype(F32)), _sigmoid(ex[1].astype(F32))
        return (sa * dm, sb * dm, dm * ex[2] * sa * (1.0 - sa), dm * ex[3] * sb * (1.0 - sb))

    dya, dyb_, dga, dgb = _mm([dyb], [p["wo"]], [(0, 0, 0)], 1, dgate, [BF16] * 4, m=t, n=d, k=d, tb=True, bm=512, bn=d,
                              bk=d, extras=[z["g"], z["g"], ya, yb], e_off=[0, 1, 0, 0], name="mix_out_bwd")
    (dwo,) = _mm([merged], [dyb], [(0, 0, 0)], 1, _first, [BF16], m=d, n=d, k=t, ta=True, bm=d, bn=d, bk=1024, name="mix_dwo")
    (doa,) = _mm([dya], [p["wa"]], [(0, 0, 0)], 1, _first, [F32], m=t, n=w, k=d, tb=True, bm=1024, bn=w, bk=d, name="branch_a_bwd")
    (dwa,) = _mm([oa], [dya], [(0, 0, 0)], 1, _first, [BF16], m=w, n=d, k=t, ta=True, bm=w, bn=d, bk=1024, name="branch_a_dw")
    (dob,) = _mm([dyb_], [p["wbt"]], [(0, 0, 0)], 1, _first, [F32], m=t, n=ATT_GW, k=d, bm=1024, bn=ATT_GW, bk=d,
                 name="branch_b_bwd")
    (dwbt,) = _mm([dyb_], [ob], [(0, 0, 0)], 1, _first, [BF16], m=d, n=ATT_GW, k=t, ta=True, bm=d, bn=ATT_GW, bk=1024,
                  name="branch_b_dw")
    do_raw, dhg, dgout = _post_a_bwd(o_raw, z["h"], p["gout"], doa)
    do_g, dlse_g = _merge_b_bwd(o_g, lse_g, dob)
    dq_g, dk_g, dv_g = zip(*[_attn_bwd(qb[g], kb[g], vb[g], o_g[g], lse_g[g], do_g[g], dlse_g[g], g)
                             for g in range(ATT_GROUPS)])
    dzq, dzk, dzv, dqn, dkn = _qk_prep_bwd(z["q"], z["k"], dq_g, dk_g, dv_g, p["qn"], p["kn"], cos, sin)
    dhq, dhf, dhi, lbsum = _hgrn_bwd(z["h"], p["lb3"], states, do_raw)
    dz = jnp.concatenate([dhq, dhf, dhi, dhg, dzq, dzk, dzv, dga, dgb], axis=1)
    pw = dz.shape[1]
    (dhm,) = _mm([dz], [p["wint"]], [(0, 0, 0)], 1, _first, [F32], m=t, n=d, k=pw, bm=1024, bn=d, bk=1536, name="mix_in_bwd")
    (dwint,) = _mm([dz], [hm], [(0, 0, 0)], 1, _first, [BF16], m=pw, n=d, k=t, ta=True, bm=1536, bn=d, bk=1024, name="mix_in_dw")
    dx, dgm = _norm_bwd(x, p["gm"], dhm, dy)
    return dx, dict(gm=dgm, wint=dwint, lbsum=lbsum, gout=dgout, qn=dqn, kn=dkn, wa=dwa, wbt=dwbt, wo=dwo)


def _rope_tables(t):
    pos = jnp.arange(t, dtype=F32)
    inv = ROPE_THETA ** (-jnp.arange(0, ATT_DH, 2, dtype=F32) / ATT_DH)
    ang = pos[:, None] * inv[None, :]
    ang = jnp.concatenate([ang, ang], axis=-1)
    return jnp.cos(ang), jnp.sin(ang)


def _lower_bounds(logits):
    lb = jnp.cumsum(jax.nn.softmax(logits, axis=0), axis=0)
    return lb - lb[0:1]


def _head_gain(g):
    return jnp.tile(g[:, None, :], (1, ATT_HEADS, 1)).reshape(1, ATT_GROUPS * ATT_GW)


SMALL_GRADS = ("ffn1_norm", "mix_norm", "lbsum", "hgrn_out_norm", "attn_q_norm", "attn_k_norm", "ffn2_norm")


def _local_step(x, target, small, fetch, emit):
    t = x.shape[0]
    depth = small["ffn1_norm"].shape[0]
    cos, sin = _rope_tables(t)
    lb_all = _lower_bounds(small["hgrn_lb_logits"])
    saved = []
    for l in range(depth):
        w1t = fetch("w1t", l, x)["w1t"]
        x, s1 = _ffn_fwd(x, small["ffn1_norm"][l], w1t, lambda after, l=l: fetch("w1o", l, after)["w1o"], "1")
        p = dict(gm=small["mix_norm"][l], wint=fetch("wint", l, x)["wint"], lb3=lb_all[l].reshape(-1, 1, HG_DK),
                 gout=small["hgrn_out_norm"][l], qn=_head_gain(small["attn_q_norm"][l]),
                 kn=_head_gain(small["attn_k_norm"][l]), late=functools.partial(fetch, "mout", l))
        x, sm = _mix_fwd(x, p, cos, sin)
        w2t = fetch("w2t", l, x)["w2t"]
        x, s2 = _ffn_fwd(x, small["ffn2_norm"][l], w2t, lambda after, l=l: fetch("w2o", l, after)["w2o"], "2")
        saved.append((p, w1t, w2t, s1, sm, s2))
    loss, dx = _loss_head(x, target)
    gsmall = {k: [None] * depth for k in SMALL_GRADS}
    tok = jnp.zeros((), F32)
    for l in reversed(range(depth)):
        p, w1t, w2t, s1, sm, s2 = saved[l]
        dx, gsmall["ffn2_norm"][l], tok = _ffn_bwd(dx, s2, small["ffn2_norm"][l], w2t, "2", tok,
                                                   lambda dwt, dwo, l=l: emit("ffn2", l, dict(w2t=dwt, w2o=dwo), None))
        dx, gm = _mix_bwd(dx, sm, p, cos, sin, tok)
        tok = emit("mix", l, {k: gm[k] for k in ("wint", "wa", "wbt", "wo")}, None)
        gsmall["mix_norm"][l], gsmall["lbsum"][l], gsmall["hgrn_out_norm"][l] = gm["gm"], gm["lbsum"], gm["gout"]
        for k, src in (("attn_q_norm", "qn"), ("attn_k_norm", "kn")):
            gsmall[k][l] = jnp.sum(gm[src].reshape(ATT_GROUPS, ATT_HEADS, ATT_DH), axis=1)
        dx, gsmall["ffn1_norm"][l], tok = _ffn_bwd(dx, s1, small["ffn1_norm"][l], w1t, "1", tok,
                                                   lambda dwt, dwo, l=l: emit("ffn1", l, dict(w1t=dwt, w1o=dwo), None))
    emit("small", 0, {}, ({k: jnp.stack(v) for k, v in gsmall.items()}, loss))
    return dx


_HBM = pl.BlockSpec(memory_space=pltpu.HBM)
_SEM = pl.BlockSpec(memory_space=pltpu.SEMAPHORE)
_EFFECT = pltpu.SideEffectType.DATAFLOW_SIDE_EFFECTING


def _peer(p):
    x, y, c = lax.axis_index("x"), lax.axis_index("y"), lax.axis_index("c")
    me = 4 * x + 2 * y + c
    return (1 - x if p & 4 else x, 1 - y if p & 2 else y, 1 - c if p & 1 else c), jnp.bitwise_xor(me, p), me


def _xchg_copy(src, land, mode, send_sems, recv_sems, k, p, arriving):
    peer, peer_id, me = _peer(p)
    block = src if mode == "gather" else src.at[peer_id]
    return pltpu.make_async_remote_copy(
        src_ref=block, dst_ref=land.at[peer_id if arriving else me], send_sem=send_sems.at[k * (N_DEV - 1) + p - 1],
        recv_sem=recv_sems.at[k * (N_DEV - 1) + p - 1], device_id=peer, device_id_type=MESH)


def _xchg_start(srcs, modes, groups, name):
    n, ng = len(srcs), len(groups)

    def body(*refs):
        src = refs[:n]
        sems = refs[n:n + 2 * ng]
        land = refs[n + 2 * ng + n:n + 2 * ng + 2 * n]
        token = refs[n + 2 * ng + 2 * n]
        for gi, idx in enumerate(groups):
            for ki, k in enumerate(idx):
                for p in range(1, N_DEV):
                    _xchg_copy(src[k], land[k], modes[k], sems[2 * gi], sems[2 * gi + 1], ki, p, False).start()
        token[...] = jnp.zeros_like(token)

    sem_shapes = []
    for idx in groups:
        sem_shapes += [pltpu.SemaphoreType.DMA((len(idx) * (N_DEV - 1),))] * 2
    outs = pl.pallas_call(
        body,
        out_shape=sem_shapes + [pltpu.HBM(a.shape, a.dtype) for a in srcs]
        + [pltpu.HBM((N_DEV,) + a.shape[-2:], a.dtype) for a in srcs] + [jax.ShapeDtypeStruct((8, 128), F32)],
        in_specs=[_HBM] * n,
        out_specs=[_SEM] * (2 * ng) + [_HBM] * (2 * n) + [pl.BlockSpec(memory_space=pltpu.VMEM)],
        input_output_aliases={i: 2 * ng + i for i in range(n)},
        compiler_params=pltpu.CompilerParams(has_side_effects=_EFFECT),
        name=name,
    )(*[pltpu.with_memory_space_constraint(a, pltpu.HBM) for a in srcs])
    sems = [(outs[2 * gi], outs[2 * gi + 1]) for gi in range(ng)]
    return sems, outs[2 * ng:2 * ng + n], outs[2 * ng + n:2 * ng + 2 * n], outs[-1]


def _xchg_wait_call(srcs, lands, modes, sems, after, name):
    n = len(srcs)

    def body(*refs):
        src, land = refs[:n], refs[n:2 * n]
        send_sems, recv_sems = refs[2 * n], refs[2 * n + 1]
        for p in range(1, N_DEV):
            for k in range(n):
                cp = _xchg_copy(src[k], land[k], modes[k], send_sems, recv_sems, k, p, True)
                cp.wait_send()
                cp.wait_recv()

    outs = pl.pallas_call(
        body,
        out_shape=[pltpu.HBM(a.shape, a.dtype) for a in list(srcs) + list(lands)],
        in_specs=[_HBM] * (2 * n) + [_SEM, _SEM, pl.BlockSpec(memory_space=pl.ANY)],
        out_specs=[_HBM] * (2 * n),
        input_output_aliases={i: i for i in range(2 * n)},
        compiler_params=pltpu.CompilerParams(has_side_effects=_EFFECT),
        name=name,
    )(*srcs, *lands, sems[0], sems[1], after)
    return outs[:n], outs[n:]


def _xchg_wait(srcs, lands, modes, sems, after, name):
    srcs, lands = _xchg_wait_call(srcs, lands, modes, sems, after, name)
    me = 4 * lax.axis_index("x") + 2 * lax.axis_index("y") + lax.axis_index("c")
    done = []
    for a, land, mode in zip(srcs, lands, modes):
        own = a[None] if mode == "gather" else lax.dynamic_slice_in_dim(a, me, 1, axis=0)
        done.append(lax.dynamic_update_slice(land, own, (me, 0, 0)))
    return done


def _sum_slots(land):
    g, _, r, c = land.shape
    br = r // 2 if (r % 32 == 0 and r >= 256) else r

    def body(l_ref, o_ref):
        acc = l_ref[0, 0].astype(F32)
        for j in range(1, N_DEV):
            acc = acc + l_ref[0, j].astype(F32)
        o_ref[0] = acc

    return pl.pallas_call(
        body,
        out_shape=jax.ShapeDtypeStruct((g, r, c), F32),
        grid=(g, r // br),
        in_specs=[pl.BlockSpec((1, N_DEV, br, c), lambda i, j: (i, 0, j, 0))],
        out_specs=pl.BlockSpec((1, br, c), lambda i, j: (i, j, 0)),
        compiler_params=_cparams(("parallel", "parallel")),
        name="sum_slots",
    )(land)


def _adamw(w, g, m, v):
    shape = w.shape
    cols = shape[-1]
    rows = int(np.prod(shape[:-1]))
    bm = max(b for b in range(8, 257, 8) if rows % b == 0) if rows % 8 == 0 else rows
    c1 = 1.0 - ADAM_B1 ** ADAM_STEP
    c2 = 1.0 - ADAM_B2 ** ADAM_STEP

    def fn(ins, consts):
        wv, gv, mv, vv = ins
        m2 = ADAM_B1 * mv + (1.0 - ADAM_B1) * gv
        v2 = ADAM_B2 * vv + (1.0 - ADAM_B2) * (gv * gv)
        delta = -ADAM_LR * ((m2 / c1) / (jnp.sqrt(v2 / c2) + ADAM_EPS) + ADAM_WD * wv)
        return [delta, m2, v2], []

    outs, _ = _rowwise(fn, [(a.reshape(rows, cols), cols, 0) for a in (w, g, m, v)], [], [(cols, F32)] * 3, [],
                       bm=bm, name="adamw")
    return [o.reshape(shape) for o in outs]


BIG = ("w1t", "w1o", "wint", "wa", "wbt", "wo", "w2t", "w2o")
FETCH_GROUPS = dict(w1t=("w1t",), w1o=("w1o",), wint=("wint",), mout=("wa", "wbt", "wo"), w2t=("w2t",), w2o=("w2o",))
SMALL_ROWS = (("ffn1_norm", 0), ("mix_norm", 2), ("lbsum", 4), ("hgrn_out_norm", 6), ("ffn2_norm", 8),
              ("attn_q_norm", 10), ("attn_k_norm", 12))
SMALL_PACK_ROWS = 16


def kernel(x, ffn1_norm, ffn1_w_in, ffn1_w_out, mix_norm, w_in, hgrn_lb_logits, hgrn_out_norm, attn_q_norm, attn_k_norm, w_branch_a, w_branch_b, w_out, ffn2_norm, ffn2_w_in, ffn2_w_out, loss_target, m_ffn1_norm, m_ffn1_w_in, m_ffn1_w_out, m_mix_norm, m_w_in, m_hgrn_lb_logits, m_hgrn_out_norm, m_attn_q_norm, m_attn_k_norm, m_w_branch_a, m_w_branch_b, m_w_out, m_ffn2_norm, m_ffn2_w_in, m_ffn2_w_out, v_ffn1_norm, v_ffn1_w_in, v_ffn1_w_out, v_mix_norm, v_w_in, v_hgrn_lb_logits, v_hgrn_out_norm, v_attn_q_norm, v_attn_k_norm, v_w_branch_a, v_w_branch_b, v_w_out, v_ffn2_norm, v_ffn2_w_in, v_ffn2_w_out):
    names = ("ffn1_norm", "ffn1_w_in", "ffn1_w_out", "mix_norm", "w_in", "hgrn_lb_logits", "hgrn_out_norm", "attn_q_norm",
             "attn_k_norm", "w_branch_a", "w_branch_b", "w_out", "ffn2_norm", "ffn2_w_in", "ffn2_w_out")
    w = dict(zip(names, (ffn1_norm, ffn1_w_in, ffn1_w_out, mix_norm, w_in, hgrn_lb_logits, hgrn_out_norm, attn_q_norm,
                         attn_k_norm, w_branch_a, w_branch_b, w_out, ffn2_norm, ffn2_w_in, ffn2_w_out)))
    m = dict(zip(names, (m_ffn1_norm, m_ffn1_w_in, m_ffn1_w_out, m_mix_norm, m_w_in, m_hgrn_lb_logits, m_hgrn_out_norm,
                         m_attn_q_norm, m_attn_k_norm, m_w_branch_a, m_w_branch_b, m_w_out, m_ffn2_norm, m_ffn2_w_in, m_ffn2_w_out)))
    v = dict(zip(names, (v_ffn1_norm, v_ffn1_w_in, v_ffn1_w_out, v_mix_norm, v_w_in, v_hgrn_lb_logits, v_hgrn_out_norm,
                         v_attn_q_norm, v_attn_k_norm, v_w_branch_a, v_w_branch_b, v_w_out, v_ffn2_norm, v_ffn2_w_in, v_ffn2_w_out)))
    depth, d = ffn1_norm.shape

    def tr(a):
        return jnp.swapaxes(a, 1, 2)

    shard = dict(w1t=tr(ffn1_w_in), w1o=ffn1_w_out, wint=tr(w_in), wa=w_branch_a,
                 wbt=tr(w_branch_b).reshape(depth, -1, d), wo=w_out, w2t=tr(ffn2_w_in), w2o=ffn2_w_out)
    order = [(g, l) for l in range(depth) for g in FETCH_GROUPS]
    flat = [(g, l, k) for g, l in order for k in FETCH_GROUPS[g]]
    groups, pos = [], 0
    for g, l in order:
        groups.append(list(range(pos, pos + len(FETCH_GROUPS[g]))))
        pos += len(FETCH_GROUPS[g])
    g_sems, g_srcs, g_lands, _ = _xchg_start([shard[k][l].astype(BF16) for _, l, k in flat], ["gather"] * len(flat),
                                             groups, "gather_start")

    def fetch(group, l, after):
        gi = order.index((group, l))
        idx = groups[gi]
        lands = _xchg_wait([g_srcs[i] for i in idx], [g_lands[i] for i in idx], ["gather"] * len(idx), g_sems[gi], after,
                           f"gather_wait_{group}{l}")
        out = {}
        for k, land in zip(FETCH_GROUPS[group], lands):
            out[k] = land.reshape(d, -1) if k == "wbt" else land.reshape(-1, d)
        return out

    pending = []

    def emit(group, l, g, final):
        keys = list(g)
        srcs = [g[k].reshape(N_DEV, -1, d) for k in keys]
        modes = ["scatter"] * len(keys)
        if final is not None:
            gsmall, loss = final
            pack = jnp.zeros((SMALL_PACK_ROWS, d), F32)
            for k, r0 in SMALL_ROWS:
                rows = gsmall[k].reshape(depth, -1)
                pack = pack.at[r0:r0 + depth, :rows.shape[1]].set(rows)
            srcs.append(pack.at[14, :].set(loss))
            modes.append("gather")
            keys.append("small")
        sems, s_thru, l_thru, token = _xchg_start(srcs, modes, [list(range(len(srcs)))], f"grads_start_{group}{l}")
        pending.append((group, l, keys, modes, sems[0], s_thru, l_thru))
        return token[0, 0]

    small = {k: w[k] for k in ("ffn1_norm", "mix_norm", "hgrn_lb_logits", "hgrn_out_norm", "attn_q_norm", "attn_k_norm", "ffn2_norm")}
    dx = _local_step(x[0], loss_target[0], small, fetch, emit)

    summed = {}
    for group, l, keys, modes, sems, s_thru, l_thru in pending:
        lands = _xchg_wait(s_thru, l_thru, modes, sems, dx, f"grads_wait_{group}{l}")
        for k, land in zip(keys, lands):
            summed[k, l] = _sum_slots(land[None])[0]
    gsum = {k: jnp.stack([summed[k, l] for l in range(depth)]) for k in BIG}
    tot = summed["small", 0]

    grads = {}
    for k, r0 in SMALL_ROWS:
        shp = (depth,) + (w[k].shape[1:] if k != "lbsum" else (d,))
        grads[k] = tot[r0:r0 + depth, :int(np.prod(shp[1:]))].reshape(shp)
    _, lb_vjp = jax.vjp(_lower_bounds, hgrn_lb_logits)
    grads["hgrn_lb_logits"] = lb_vjp(grads.pop("lbsum"))[0]
    grads["ffn1_w_in"], grads["ffn1_w_out"] = tr(gsum["w1t"]), gsum["w1o"]
    grads["w_in"], grads["w_branch_a"] = tr(gsum["wint"]), gsum["wa"]
    grads["w_branch_b"] = tr(gsum["wbt"].reshape(depth, d // N_DEV, -1))
    grads["w_out"] = gsum["wo"]
    grads["ffn2_w_in"], grads["ffn2_w_out"] = tr(gsum["w2t"]), gsum["w2o"]

    upd = {k: _adamw(w[k], grads[k], m[k], v[k]) for k in names}
    return (tot[14, 0], dx[None], *[grads[k] for k in names], *[upd[k][0] for k in names],
            *[upd[k][1] for k in names], *[upd[k][2] for k in names])
```

```python
import functools
import math

import jax
import jax.numpy as jnp
import numpy as np
from jax import lax
from jax.experimental import pallas as pl
from jax.experimental.pallas import tpu as pltpu

F32 = jnp.float32
BF16 = jnp.bfloat16

N_DEV = 8
EPS = 1e-6
HG_DK = 128
HG_CHUNK = 64
HG_SUB = 16
HG_HP = 4
ATT_PATTERNS = ((128, 1), (512, 4), (2048, 16))
ATT_GROUPS = 3
ATT_HEADS = 4
ATT_DH = 128
ATT_BLK = 128
ROPE_THETA = 10000.0
ADAM_LR, ADAM_B1, ADAM_B2, ADAM_EPS, ADAM_WD, ADAM_STEP = 0.001, 0.9, 0.999, 1e-08, 0.01, 10
VMEM_LIMIT_BYTES = 56 * 1024 * 1024
MESH = pl.DeviceIdType.MESH


def _cparams(sem, **kw):
    return pltpu.CompilerParams(dimension_semantics=sem, vmem_limit_bytes=VMEM_LIMIT_BYTES, **kw)


def _sigmoid(x):
    return 1.0 / (1.0 + jnp.exp(-x))


def _mm(a_list, b_list, pairs, n_acc, fin, out_dtypes, *, m, n, k, ta=False, tb=False, bm, bn, bk,
        b_off=None, extras=(), e_off=None, n_outer=False, consts=(), a_pro=None, n_sums=0, name):
    bm, bn, bk = min(bm, m), min(bn, n), min(bk, k)
    assert m % bm == 0 and n % bn == 0 and k % bk == 0, (name, m, n, k, bm, bn, bk)
    nk = k // bk
    assert not (a_pro and (nk > 1 or ta or n_outer)) and not (n_sums and (bn != n or n_outer)), name
    b_off = b_off or [(0, 0)] * len(b_list)
    e_off = e_off or [0] * len(extras)
    na, nb, ne, nc, no = len(a_list), len(b_list), len(extras), len(consts), len(out_dtypes)
    nao = na if a_pro else 0
    dn = (((0,) if ta else (1,), (1,) if tb else (0,)), ((), ()))

    def body(*refs):
        refs = list(refs)
        a_refs, b_refs, e_refs, c_refs, o_refs, ao_refs, s_refs = (
            [refs.pop(0) for _ in range(cnt)] for cnt in (na, nb, ne, nc, no, nao, n_sums))
        acc_refs = refs
        kk = pl.program_id(2)
        first = pl.program_id(0) == 0
        cvals = [c[...] for c in c_refs]
        a_vals = [r[...] for r in a_refs]
        if a_pro:
            a_vals = a_pro(a_vals, cvals)
            for r, v in zip(ao_refs, a_vals):
                r[...] = v
        parts = [None] * n_acc
        for ai, bi, ci in pairs:
            p = lax.dot_general(a_vals[ai], b_refs[bi][...], dn, preferred_element_type=F32)
            parts[ci] = p if parts[ci] is None else parts[ci] + p

        def finish(accs):
            ex = [e[...] for e in e_refs]
            res = fin(accs, ex, cvals) if nc else fin(accs, ex)
            outs, sums = res if n_sums else (res, ())
            for o_ref, o in zip(o_refs, outs):
                o_ref[...] = o.astype(o_ref.dtype)
            if n_sums:
                @pl.when(first)
                def _():
                    for s_ref, s in zip(s_refs, sums):
                        s_ref[...] = s

                @pl.when(jnp.logical_not(first))
                def _():
                    for s_ref, s in zip(s_refs, sums):
                        s_ref[...] += s

        if nk == 1:
            finish(parts)
        else:
            @pl.when(kk == 0)
            def _():
                for c in range(n_acc):
                    acc_refs[c][...] = parts[c]

            @pl.when(kk > 0)
            def _():
                for c in range(n_acc):
                    acc_refs[c][...] += parts[c]

            @pl.when(kk == nk - 1)
            def _():
                finish([acc_refs[c][...] for c in range(n_acc)])

    def ij(f):
        return (lambda j, i, q: f(i, j, q)) if n_outer else f

    a_spec = pl.BlockSpec((bk, bm), ij(lambda i, j, q: (q, i))) if ta else pl.BlockSpec((bm, bk), ij(lambda i, j, q: (i, q)))

    def b_spec(off):
        on, ok = off
        if tb:
            return pl.BlockSpec((bn, bk), ij(lambda i, j, q: (j + on, q + ok)))
        return pl.BlockSpec((bk, bn), ij(lambda i, j, q: (q + ok, j + on)))

    mn_spec = pl.BlockSpec((bm, bn), ij(lambda i, j, q: (i, j)))
    outs = pl.pallas_call(
        body,
        out_shape=[jax.ShapeDtypeStruct((m, n), d) for d in out_dtypes] + [jax.ShapeDtypeStruct((m, k), BF16)] * nao
        + [jax.ShapeDtypeStruct((8, n), F32)] * n_sums,
        grid=(n // bn, m // bm, nk) if n_outer else (m // bm, n // bn, nk),
        in_specs=[a_spec] * na + [b_spec(o) for o in b_off]
        + [pl.BlockSpec((bm, bn), ij(lambda i, j, q, o=o: (i, j + o))) for o in e_off]
        + [pl.BlockSpec(c.shape, lambda *_, nd=c.ndim: (0,) * nd) for c in consts],
        out_specs=[mn_spec] * no + [a_spec] * nao + [pl.BlockSpec((8, n), lambda *_: (0, 0))] * n_sums,
        scratch_shapes=[pltpu.VMEM((bm, bn), F32) for _ in range(n_acc if nk > 1 else 0)],
        compiler_params=_cparams(("arbitrary" if n_sums else "parallel", "parallel", "arbitrary")),
        name=name,
    )(*a_list, *b_list, *extras, *consts)
    return outs


def _first(accs, ex):
    return (accs[0],)


def _rowwise(fn, ins, consts, out_defs, sum_widths, *, bm, name):
    ins = [tuple(e) + (1,) * (4 - len(e)) for e in ins]
    out_defs = [tuple(e) + (1,) * (3 - len(e)) for e in out_defs]
    t = ins[0][0].shape[-2] * ins[0][3]
    bm = min(bm, t)
    assert t % bm == 0, (name, t, bm)
    ni, nc, no, ns = len(ins), len(consts), len(out_defs), len(sum_widths)
    strided = [w for _, w, _, d in ins if d > 1] + [w for w, _, d in out_defs if d > 1]

    def body(*refs):
        i_refs, c_refs = refs[:ni], refs[ni:ni + nc]
        o_refs, s_refs = refs[ni + nc:ni + nc + no], refs[ni + nc + no:ni + nc + no + ns]
        scratch = list(refs[ni + nc + no + ns:])
        vals = []
        for ref, (_, w, _, d) in zip(i_refs, ins):
            if d == 1:
                vals.append(ref[...])
                continue
            s = scratch.pop(0)
            for r in range(d):
                for c in range(w // 128):
                    s.at[c][pl.ds(r, bm // d, stride=d), :] = ref[r, :, c * 128:(c + 1) * 128].astype(F32)
            vals.append(jnp.concatenate([s[c] for c in range(w // 128)], axis=1))
        outs, sums = fn(vals, [r[...] for r in c_refs])
        for o_ref, o, (w, _, d) in zip(o_refs, outs, out_defs):
            if d == 1:
                o_ref[...] = o.astype(o_ref.dtype)
                continue
            s = scratch.pop(0)
            for c in range(w // 128):
                s[c] = o[:, c * 128:(c + 1) * 128].astype(F32)
            for r in range(d):
                for c in range(w // 128):
                    o_ref[r, :, c * 128:(c + 1) * 128] = s.at[c][pl.ds(r, bm // d, stride=d), :].astype(o_ref.dtype)
        if ns:
            first = pl.program_id(0) == 0

            @pl.when(first)
            def _():
                for s_ref, s in zip(s_refs, sums):
                    s_ref[...] = s

            @pl.when(jnp.logical_not(first))
            def _():
                for s_ref, s in zip(s_refs, sums):
                    s_ref[...] += s

    def win(width, cb, d):
        if d > 1:
            return pl.BlockSpec((d, bm // d, width), lambda i: (0, i, 0))
        return pl.BlockSpec((bm, width), lambda i: (i, cb))

    res = pl.pallas_call(
        body,
        out_shape=[jax.ShapeDtypeStruct((t, w) if d == 1 else (d, t // d, w), dt) for w, dt, d in out_defs]
        + [jax.ShapeDtypeStruct((8, w), F32) for w in sum_widths],
        grid=(t // bm,),
        in_specs=[win(w, cb, d) for _, w, cb, d in ins] + [pl.BlockSpec(c.shape, lambda i, nd=c.ndim: (0,) * nd) for c in consts],
        out_specs=[win(w, 0, d) for w, _, d in out_defs] + [pl.BlockSpec((8, w), lambda i: (0, 0)) for w in sum_widths],
        scratch_shapes=[pltpu.VMEM((w // 128, bm, 128), F32) for w in strided],
        compiler_params=_cparams(("arbitrary",) if ns else ("parallel",)),
        name=name,
    )(*[e[0] for e in ins], *consts)
    return res[:no], [jnp.sum(s, axis=0) for s in res[no:]]


def _colsum8(x):
    bm, w = x.shape
    return jnp.sum(x.reshape(bm // 8, 8, w), axis=0)


def _tri(n, upper=False):
    r = lax.broadcasted_iota(jnp.int32, (n, n), 0)
    c = lax.broadcasted_iota(jnp.int32, (n, n), 1)
    return (c >= r) if upper else (c <= r)


def _exact_tri_matmul(tri_bf16, x):
    x0 = x.astype(BF16)
    r1 = x - x0.astype(F32)
    x1 = r1.astype(BF16)
    x2 = (r1 - x1.astype(F32)).astype(BF16)
    w = x.shape[1]
    y = jnp.dot(tri_bf16, jnp.concatenate([x0, x1, x2], axis=1), preferred_element_type=F32)
    return y[:, :w] + y[:, w:2 * w] + y[:, 2 * w:]


def _dot_nt(a, b):
    return lax.dot_general(a, b, (((1,), (1,)), ((), ())), preferred_element_type=F32)


def _dot_tn(a, b):
    return lax.dot_general(a, b, (((0,), (0,)), ((), ())), preferred_element_type=F32)


def _dot(a, b):
    return jnp.dot(a, b, preferred_element_type=F32)


def _hg_gates(hq, hf, lb):
    sq = _sigmoid(hq)
    q = hq * sq
    sg = _sigmoid(hf)
    f = lb + (1.0 - lb) * sg
    return q, sq, sg, f


def _hg_intra(q, kk, g):
    c = q.shape[0]
    rows = lax.broadcasted_iota(jnp.int32, (c, 1), 0)
    a_rows, qts, kts, eqs, eks = [], [], [], [], []
    for i in range(c // HG_SUB):
        lo = i * HG_SUB
        ref = g[lo - 1:lo, :] if i else jnp.zeros_like(g[0:1, :])
        eq = jnp.exp(g[lo:lo + HG_SUB, :] - ref)
        ek = jnp.exp(jnp.where(rows < lo + HG_SUB, ref - g, 0.0))
        qt = q[lo:lo + HG_SUB, :] * eq
        kt = kk * ek
        a = _dot_nt(qt.astype(BF16), kt.astype(BF16))
        tpos = lo + lax.broadcasted_iota(jnp.int32, (HG_SUB, c), 0)
        spos = lax.broadcasted_iota(jnp.int32, (HG_SUB, c), 1)
        a_rows.append(jnp.where(spos <= tpos, a, 0.0))
        qts.append(qt), kts.append(kt), eqs.append(eq), eks.append(ek)
    return jnp.concatenate(a_rows, axis=0), qts, kts, eqs, eks


def _hgrn_fwd_serial(zh, lb3, *, tb=512):
    t = zh.shape[0]
    nh = lb3.shape[0]
    c = HG_CHUNK
    tb = min(tb, t)
    nchunk = tb // c
    hp = HG_HP if nh % HG_HP == 0 else 1

    def body(hq_ref, hf_ref, hi_ref, lb_ref, o_ref, st_ref, state):
        @pl.when(pl.program_id(1) == 0)
        def _():
            state[...] = jnp.zeros_like(state)

        tril = _tri(c).astype(BF16)

        def one_head(hh, ci, sl):
            ls = slice(hh * HG_DK, (hh + 1) * HG_DK)
            q, _, _, f = _hg_gates(hq_ref[sl, ls], hf_ref[sl, ls], lb_ref[hh])
            v = hi_ref[sl, ls]
            kk = 1.0 - f
            g = _exact_tri_matmul(tril, jnp.log(f))
            a, _, _, _, _ = _hg_intra(q, kk, g)
            st = state[hh]
            st_ref[hh, ci] = st
            vb = v.astype(BF16)
            o = _dot(a.astype(BF16), vb) + _dot_nt((q * jnp.exp(g)).astype(BF16), st.astype(BF16))
            o_ref[sl, ls] = o
            glast = g[c - 1:c, :]
            kg = kk * jnp.exp(glast - g)
            state[hh] = st * jnp.exp(glast) + _dot_tn(vb, kg.astype(BF16))

        def chunk(ci, carry):
            sl = pl.ds(pl.multiple_of(ci * c, c), c)
            for hh in range(hp):
                one_head(hh, ci, sl)
            return carry

        lax.fori_loop(0, nchunk, chunk, 0)

    def col(cb):
        return pl.BlockSpec((tb, hp * HG_DK), lambda h, i: (i, cb * (nh // hp) + h))

    return pl.pallas_call(
        body,
        out_shape=[jax.ShapeDtypeStruct((t, nh * HG_DK), F32), jax.ShapeDtypeStruct((nh, t // c, HG_DK, HG_DK), F32)],
        grid=(nh // hp, t // tb),
        in_specs=[col(0), col(1), col(2), pl.BlockSpec((hp, 1, HG_DK), lambda h, i: (h, 0, 0))],
        out_specs=[pl.BlockSpec((tb, hp * HG_DK), lambda h, i: (i, h)),
                   pl.BlockSpec((hp, nchunk, HG_DK, HG_DK), lambda h, i: (h, i, 0, 0))],
        scratch_shapes=[pltpu.VMEM((hp, HG_DK, HG_DK), F32)],
        compiler_params=_cparams(("parallel", "arbitrary")),
        name="hgrn_fwd",
    )(zh, zh, zh, lb3)


def _hgrn_bwd_serial(zh, lb3, states, d_o, *, tb=512):
    t = zh.shape[0]
    nh = lb3.shape[0]
    c = HG_CHUNK
    tb = min(tb, t)
    nchunk = tb // c
    nblk = t // tb
    hp = HG_HP if nh % HG_HP == 0 else 1

    def body(hq_ref, hf_ref, hi_ref, lb_ref, st_ref, do_ref, dq_ref, df_ref, dv_ref, dlb_ref, dstate):
        @pl.when(pl.program_id(1) == 0)
        def _():
            dstate[...] = jnp.zeros_like(dstate)
            dlb_ref[...] = jnp.zeros_like(dlb_ref)

        tril = _tri(c).astype(BF16)
        triu = _tri(c, upper=True).astype(BF16)
        last_row = lax.broadcasted_iota(jnp.int32, (c, 1), 0) == c - 1

        def one_head(hh, ci, sl):
            ls = slice(hh * HG_DK, (hh + 1) * HG_DK)
            lb = lb_ref[hh]
            hq, hf = hq_ref[sl, ls], hf_ref[sl, ls]
            q, sq, sg, f = _hg_gates(hq, hf, lb)
            v = hi_ref[sl, ls]
            kk = 1.0 - f
            g = _exact_tri_matmul(tril, jnp.log(f))
            a, qts, kts, eqs, eks = _hg_intra(q, kk, g)
            st = st_ref[hh, ci]
            dst = dstate[hh]
            do = do_ref[sl, ls]
            dob, vb = do.astype(BF16), v.astype(BF16)
            glast = g[c - 1:c, :]
            eg = jnp.exp(g)
            egl = jnp.exp(glast - g)
            qg = q * eg
            kg = kk * egl
            dv = _dot_tn(a.astype(BF16), dob) + _dot_nt(kg.astype(BF16), dst.astype(BF16))
            da = jnp.where(_tri(c), _dot_nt(dob, vb), 0.0).astype(BF16)
            dq_parts, dgq_parts = [], []
            dk = jnp.zeros_like(kk)
            dgk = jnp.zeros_like(kk)
            for i in range(c // HG_SUB):
                da_i = da[i * HG_SUB:(i + 1) * HG_SUB, :]
                ktb, qtb = kts[i].astype(BF16), qts[i].astype(BF16)
                xi = _dot(da_i, ktb)
                yi = _dot_tn(da_i, qtb)
                dq_parts.append(xi * eqs[i])
                dk = dk + yi * eks[i]
                dgq_parts.append(xi * qtb.astype(F32))
                dgk = dgk + yi * ktb.astype(F32)
            dq_inter = _dot(dob, st.astype(BF16)) * eg
            dq = jnp.concatenate(dq_parts, axis=0) + dq_inter
            dk_state = _dot(vb, dst.astype(BF16)) * egl
            dk = dk + dk_state
            dg = jnp.concatenate(dgq_parts, axis=0) - dgk + q * dq_inter - kk * dk_state
            dgl = jnp.sum(kk * dk_state, axis=0, keepdims=True) + jnp.exp(glast) * jnp.sum(st * dst, axis=0, keepdims=True)
            dg = dg + jnp.where(last_row, dgl, 0.0)
            dlogf = _exact_tri_matmul(triu, dg)
            dfv = dlogf / f - dk
            dq_ref[sl, ls] = (dq * (sq * (1.0 + hq * (1.0 - sq)))).astype(dq_ref.dtype)
            df_ref[sl, ls] = (dfv * (1.0 - lb) * sg * (1.0 - sg)).astype(df_ref.dtype)
            dv_ref[sl, ls] = dv.astype(dv_ref.dtype)
            dlb_ref[hh] += jnp.sum(dfv * (1.0 - sg), axis=0, keepdims=True)
            dstate[hh] = dst * jnp.exp(glast) + _dot_tn(dob, qg.astype(BF16))

        def chunk(j, carry):
            ci = nchunk - 1 - j
            sl = pl.ds(pl.multiple_of(ci * c, c), c)
            for hh in range(hp):
                one_head(hh, ci, sl)
            return carry

        lax.fori_loop(0, nchunk, chunk, 0)

    def col(cb):
        return pl.BlockSpec((tb, hp * HG_DK), lambda h, i: (nblk - 1 - i, cb * (nh // hp) + h))

    ocol = pl.BlockSpec((tb, hp * HG_DK), lambda h, i: (nblk - 1 - i, h))
    w = nh * HG_DK
    dq, df, dv, dlb = pl.pallas_call(
        body,
        out_shape=[jax.ShapeDtypeStruct((t, w), BF16)] * 3 + [jax.ShapeDtypeStruct((nh, 1, HG_DK), F32)],
        grid=(nh // hp, nblk),
        in_specs=[col(0), col(1), col(2), pl.BlockSpec((hp, 1, HG_DK), lambda h, i: (h, 0, 0)),
                  pl.BlockSpec((hp, nchunk, HG_DK, HG_DK), lambda h, i: (h, nblk - 1 - i, 0, 0)), ocol],
        out_specs=[ocol, ocol, ocol, pl.BlockSpec((hp, 1, HG_DK), lambda h, i: (h, 0, 0))],
        scratch_shapes=[pltpu.VMEM((hp, HG_DK, HG_DK), F32)],
        compiler_params=_cparams(("parallel", "arbitrary")),
        name="hgrn_bwd",
    )(zh, zh, zh, lb3, states, d_o)
    return dq, df, dv, dlb.reshape(w)


def _hg_heads(x, hp):
    return [x[:, h * HG_DK:(h + 1) * HG_DK] for h in range(hp)]


def _hg_intra_wide(q, kk, g, hp):
    c = q.shape[0]
    rows = lax.broadcasted_iota(jnp.int32, (c, 1), 0)
    a_rows = [[] for _ in range(hp)]
    qts, kts, eqs, eks = [], [], [], []
    for i in range(c // HG_SUB):
        lo = i * HG_SUB
        ref = g[lo - 1:lo, :] if i else jnp.zeros_like(g[0:1, :])
        eq = jnp.exp(g[lo:lo + HG_SUB, :] - ref)
        ek = jnp.exp(jnp.where(rows < lo + HG_SUB, ref - g, 0.0))
        qtb = (q[lo:lo + HG_SUB, :] * eq).astype(BF16)
        ktb = (kk * ek).astype(BF16)
        tpos = lo + lax.broadcasted_iota(jnp.int32, (HG_SUB, c), 0)
        spos = lax.broadcasted_iota(jnp.int32, (HG_SUB, c), 1)
        for h, (qh, kh) in enumerate(zip(_hg_heads(qtb, hp), _hg_heads(ktb, hp))):
            a_rows[h].append(jnp.where(spos <= tpos, _dot_nt(qh, kh), 0.0))
        qts.append(qtb), kts.append(ktb), eqs.append(eq), eks.append(ek)
    return [jnp.concatenate(r, axis=0) for r in a_rows], qts, kts, eqs, eks


def _hgrn_fwd(zh, lb3, *, tb=512):
    t = zh.shape[0]
    nh = lb3.shape[0]
    c = HG_CHUNK
    tb = min(tb, t)
    nchunk = tb // c
    hp = HG_HP if nh % HG_HP == 0 else 1
    wp = hp * HG_DK

    def body(hq_ref, hf_ref, hi_ref, lb_ref, o_ref, st_ref, state):
        @pl.when(pl.program_id(1) == 0)
        def _():
            state[...] = jnp.zeros_like(state)

        tril = _tri(c).astype(BF16)

        def chunk(ci, carry):
            sl = pl.ds(pl.multiple_of(ci * c, c), c)
            q, _, _, f = _hg_gates(hq_ref[sl, :], hf_ref[sl, :], lb_ref[...])
            kk = 1.0 - f
            g = _exact_tri_matmul(tril, jnp.log(f))
            a, _, _, _, _ = _hg_intra_wide(q, kk, g, hp)
            vb = hi_ref[sl, :].astype(BF16)
            glast = g[c - 1:c, :]
            qgb = (q * jnp.exp(g)).astype(BF16)
            kgb = (kk * jnp.exp(glast - g)).astype(BF16)
            dec = jnp.exp(glast)
            sts = [state[h] for h in range(hp)]
            for h in range(hp):
                st_ref[h, ci] = sts[h]
            vh, qgh, kgh, dech = _hg_heads(vb, hp), _hg_heads(qgb, hp), _hg_heads(kgb, hp), _hg_heads(dec, hp)
            o = [_dot(a[h].astype(BF16), vh[h]) + _dot_nt(qgh[h], sts[h].astype(BF16)) for h in range(hp)]
            new = [_dot_tn(vh[h], kgh[h]) for h in range(hp)]
            o_ref[sl, :] = jnp.concatenate(o, axis=1)
            for h in range(hp):
                state[h] = sts[h] * dech[h] + new[h]
            return carry

        lax.fori_loop(0, nchunk, chunk, 0)

    def col(cb):
        return pl.BlockSpec((tb, wp), lambda h, i: (i, cb * (nh // hp) + h))

    return pl.pallas_call(
        body,
        out_shape=[jax.ShapeDtypeStruct((t, nh * HG_DK), F32), jax.ShapeDtypeStruct((nh, t // c, HG_DK, HG_DK), F32)],
        grid=(nh // hp, t // tb),
        in_specs=[col(0), col(1), col(2), pl.BlockSpec((1, wp), lambda h, i: (0, h))],
        out_specs=[pl.BlockSpec((tb, wp), lambda h, i: (i, h)),
                   pl.BlockSpec((hp, nchunk, HG_DK, HG_DK), lambda h, i: (h, i, 0, 0))],
        scratch_shapes=[pltpu.VMEM((hp, HG_DK, HG_DK), F32)],
        compiler_params=_cparams(("parallel", "arbitrary")),
        name="hgrn_fwd",
    )(zh, zh, zh, lb3.reshape(1, -1))


def _hgrn_bwd(zh, lb3, states, d_o, *, tb=512):
    t = zh.shape[0]
    nh = lb3.shape[0]
    c = HG_CHUNK
    tb = min(tb, t)
    nchunk = tb // c
    nblk = t // tb
    hp = HG_HP if nh % HG_HP == 0 else 1
    wp = hp * HG_DK

    def body(hq_ref, hf_ref, hi_ref, lb_ref, st_ref, do_ref, dq_ref, df_ref, dv_ref, dlb_ref, dstate):
        @pl.when(pl.program_id(1) == 0)
        def _():
            dstate[...] = jnp.zeros_like(dstate)
            dlb_ref[...] = jnp.zeros_like(dlb_ref)

        tril = _tri(c).astype(BF16)
        triu = _tri(c, upper=True).astype(BF16)
        last_row = lax.broadcasted_iota(jnp.int32, (c, 1), 0) == c - 1
        heads = range(hp)

        def chunk(j, carry):
            ci = nchunk - 1 - j
            sl = pl.ds(pl.multiple_of(ci * c, c), c)
            lb = lb_ref[...]
            hq, hf = hq_ref[sl, :], hf_ref[sl, :]
            q, sq, sg, f = _hg_gates(hq, hf, lb)
            kk = 1.0 - f
            g = _exact_tri_matmul(tril, jnp.log(f))
            a, qts, kts, eqs, eks = _hg_intra_wide(q, kk, g, hp)
            glast = g[c - 1:c, :]
            eg, egl, dec = jnp.exp(g), jnp.exp(glast - g), jnp.exp(glast)
            vb, dob = hi_ref[sl, :].astype(BF16), do_ref[sl, :].astype(BF16)
            qgb, kgb = (q * eg).astype(BF16), (kk * egl).astype(BF16)
            sts = [st_ref[h, ci] for h in heads]
            dsts = [dstate[h] for h in heads]
            stb, dstb = [s.astype(BF16) for s in sts], [s.astype(BF16) for s in dsts]
            vh, doh, qgh, kgh = _hg_heads(vb, hp), _hg_heads(dob, hp), _hg_heads(qgb, hp), _hg_heads(kgb, hp)
            dv = [_dot_tn(a[h].astype(BF16), doh[h]) + _dot_nt(kgh[h], dstb[h]) for h in heads]
            da = [jnp.where(_tri(c), _dot_nt(doh[h], vh[h]), 0.0).astype(BF16) for h in heads]
            dq_inter = jnp.concatenate([_dot(doh[h], stb[h]) for h in heads], axis=1) * eg
            dk_state = jnp.concatenate([_dot(vh[h], dstb[h]) for h in heads], axis=1) * egl
            new_dst = [_dot_tn(doh[h], qgh[h]) for h in heads]
            xs, dk, dgk = [], dk_state, 0.0
            for i in range(c // HG_SUB):
                rs = slice(i * HG_SUB, (i + 1) * HG_SUB)
                kth, qth = _hg_heads(kts[i], hp), _hg_heads(qts[i], hp)
                xi = jnp.concatenate([_dot(da[h][rs, :], kth[h]) for h in heads], axis=1)
                yi = jnp.concatenate([_dot_tn(da[h][rs, :], qth[h]) for h in heads], axis=1)
                xs.append(xi)
                dk = dk + yi * eks[i]
                dgk = dgk + yi * kts[i].astype(F32)
            dq = jnp.concatenate([x * e for x, e in zip(xs, eqs)], axis=0) + dq_inter
            dgq = jnp.concatenate([x * qt.astype(F32) for x, qt in zip(xs, qts)], axis=0)
            dg = dgq - dgk + q * dq_inter - kk * dk_state
            sdot = jnp.concatenate([jnp.sum(sts[h] * dsts[h], axis=0, keepdims=True) for h in heads], axis=1)
            dgl = jnp.sum(kk * dk_state, axis=0, keepdims=True) + dec * sdot
            dg = dg + jnp.where(last_row, dgl, 0.0)
            dlogf = _exact_tri_matmul(triu, dg)
            dfv = dlogf / f - dk
            dq_ref[sl, :] = (dq * (sq * (1.0 + hq * (1.0 - sq)))).astype(dq_ref.dtype)
            df_ref[sl, :] = (dfv * (1.0 - lb) * sg * (1.0 - sg)).astype(df_ref.dtype)
            dv_ref[sl, :] = jnp.concatenate(dv, axis=1).astype(dv_ref.dtype)
            dlb_ref[...] += jnp.sum(dfv * (1.0 - sg), axis=0, keepdims=True)
            dech = _hg_heads(dec, hp)
            for h in heads:
                dstate[h] = dsts[h] * dech[h] + new_dst[h]
            return carry

        lax.fori_loop(0, nchunk, chunk, 0)

    def col(cb):
        return pl.BlockSpec((tb, wp), lambda h, i: (nblk - 1 - i, cb * (nh // hp) + h))

    ocol = pl.BlockSpec((tb, wp), lambda h, i: (nblk - 1 - i, h))
    lbspec = pl.BlockSpec((1, wp), lambda h, i: (0, h))
    w = nh * HG_DK
    dq, df, dv, dlb = pl.pallas_call(
        body,
        out_shape=[jax.ShapeDtypeStruct((t, w), BF16)] * 3 + [jax.ShapeDtypeStruct((1, w), F32)],
        grid=(nh // hp, nblk),
        in_specs=[col(0), col(1), col(2), lbspec,
                  pl.BlockSpec((hp, nchunk, HG_DK, HG_DK), lambda h, i: (h, nblk - 1 - i, 0, 0)), ocol],
        out_specs=[ocol, ocol, ocol, lbspec],
        scratch_shapes=[pltpu.VMEM((hp, HG_DK, HG_DK), F32)],
        compiler_params=_cparams(("parallel", "arbitrary")),
        name="hgrn_bwd",
    )(zh, zh, zh, lb3.reshape(1, -1), states, d_o)
    return dq, df, dv, dlb.reshape(w)


NEG = -1e30
ATT_GW = ATT_HEADS * ATT_DH


def _att_scores(q, kp, kc, has_prev):
    scale = ATT_DH ** -0.5
    i = lax.broadcasted_iota(jnp.int32, (ATT_BLK, ATT_BLK), 0)
    j = lax.broadcasted_iota(jnp.int32, (ATT_BLK, ATT_BLK), 1)
    s_p = jnp.where(jnp.logical_and(j >= i, has_prev), _dot_nt(q, kp) * scale, NEG)
    s_c = jnp.where(j <= i, _dot_nt(q, kc) * scale, NEG)
    return s_p, s_c


def _att_views(arrs, d):
    return [a.reshape(d, -1, ATT_GW) for a in arrs]


def _att_unview(a, d):
    return a.reshape(-1, ATT_GW) if d == 1 else a


ATT_QB = 4


def _attn_fwd(qb, kb, vb, g):
    d = ATT_PATTERNS[g][1]
    q2, k2, v2 = _att_views([qb, kb, vb], d)
    nblk = q2.shape[1] // ATT_BLK
    nq = ATT_QB if nblk % ATT_QB == 0 else 1
    rows = nq * ATT_BLK

    def body(q_ref, kc_ref, kp_ref, vc_ref, vp_ref, o_ref, l_ref):
        first = pl.program_id(1) == 0
        hss = [slice(h * ATT_DH, (h + 1) * ATT_DH) for h in range(ATT_HEADS)]
        for b in range(nq):
            rs = slice(b * ATT_BLK, (b + 1) * ATT_BLK)
            ps = slice((b - 1) * ATT_BLK, b * ATT_BLK)
            has_prev = jnp.logical_not(first) if b == 0 else True
            kv = [(kp_ref[:, hs], vp_ref[:, hs]) if b == 0 else (kc_ref[ps, hs], vc_ref[ps, hs]) for hs in hss]
            sc = [_att_scores(q_ref[rs, hs], kv[h][0], kc_ref[rs, hs], has_prev) for h, hs in enumerate(hss)]
            ms = [jnp.maximum(jnp.max(s_p, axis=1, keepdims=True), jnp.max(s_c, axis=1, keepdims=True)) for s_p, s_c in sc]
            ps_ = [(jnp.exp(s_p - m), jnp.exp(s_c - m)) for (s_p, s_c), m in zip(sc, ms)]
            ls = [jnp.sum(p_p, axis=1, keepdims=True) + jnp.sum(p_c, axis=1, keepdims=True) for p_p, p_c in ps_]
            os_ = [_dot(p_p.astype(BF16), kv[h][1]) + _dot(p_c.astype(BF16), vc_ref[rs, hss[h]]) for h, (p_p, p_c) in enumerate(ps_)]
            for h, hs in enumerate(hss):
                o_ref[rs, hs] = os_[h] / ls[h]
                l_ref[rs, hs] = jnp.broadcast_to(ms[h] + jnp.log(ls[h]), (ATT_BLK, ATT_DH))

    cur = pl.BlockSpec((None, rows, ATT_GW), lambda r, n: (r, n, 0))
    prev = pl.BlockSpec((None, ATT_BLK, ATT_GW), lambda r, n: (r, jnp.maximum(n * nq - 1, 0), 0))
    o, lse = pl.pallas_call(
        body,
        out_shape=[jax.ShapeDtypeStruct(q2.shape, F32)] * 2,
        grid=(d, nblk // nq),
        in_specs=[cur, cur, prev, cur, prev],
        out_specs=[cur, cur],
        compiler_params=_cparams(("parallel", "arbitrary")),
        name=f"attn_fwd_g{g}",
    )(q2, k2, k2, v2, v2)
    return _att_unview(o, d), _att_unview(lse, d)


def _attn_bwd(qb, kb, vb, o, lse, d_o, d_lse, g):
    d = ATT_PATTERNS[g][1]
    q2, k2, v2 = _att_views([qb, kb, vb], d)
    o2, l2, do2, dl2 = _att_views([o, lse, d_o, d_lse], d)
    nblk = q2.shape[1] // ATT_BLK
    nq = ATT_QB if nblk % ATT_QB == 0 else 1
    rows = nq * ATT_BLK
    ns = nblk // nq
    scale = ATT_DH ** -0.5

    def body(q_ref, kc_ref, kp_ref, vc_ref, vp_ref, o_ref, l_ref, do_ref, dl_ref, dq_ref, dk_ref, dv_ref, ck, cv):
        n = pl.program_id(1)

        @pl.when(n == 0)
        def _():
            ck[...] = jnp.zeros_like(ck)
            cv[...] = jnp.zeros_like(cv)

        first = n == ns - 1
        hss = [slice(h * ATT_DH, (h + 1) * ATT_DH) for h in range(ATT_HEADS)]
        heads = range(ATT_HEADS)
        pend_k, pend_v = [ck[:, hs] for hs in hss], [cv[:, hs] for hs in hss]
        for b in reversed(range(nq)):
            rs = slice(b * ATT_BLK, (b + 1) * ATT_BLK)
            ps = slice((b - 1) * ATT_BLK, b * ATT_BLK)
            has_prev = jnp.logical_not(first) if b == 0 else True
            q = [q_ref[rs, hs] for hs in hss]
            kc, vc = [kc_ref[rs, hs] for hs in hss], [vc_ref[rs, hs] for hs in hss]
            kp = [kp_ref[:, hs] if b == 0 else kc_ref[ps, hs] for hs in hss]
            vp = [vp_ref[:, hs] if b == 0 else vc_ref[ps, hs] for hs in hss]
            sc = [_att_scores(q[h], kp[h], kc[h], has_prev) for h in heads]
            dob = [do_ref[rs, hs].astype(BF16) for hs in hss]
            dp = [(_dot_nt(dob[h], vp[h]), _dot_nt(dob[h], vc[h])) for h in heads]
            delta = [jnp.sum(do_ref[rs, hs] * o_ref[rs, hs] - dl_ref[rs, hs], axis=1, keepdims=True) for hs in hss]
            pr = [(jnp.exp(sc[h][0] - l_ref[rs, hss[h]][:, 0:1]), jnp.exp(sc[h][1] - l_ref[rs, hss[h]][:, 0:1])) for h in heads]
            ds = [((pr[h][0] * (dp[h][0] - delta[h]) * scale).astype(BF16), (pr[h][1] * (dp[h][1] - delta[h]) * scale).astype(BF16))
                  for h in heads]
            pb = [(pr[h][0].astype(BF16), pr[h][1].astype(BF16)) for h in heads]
            dq = [_dot(ds[h][0], kp[h]) + _dot(ds[h][1], kc[h]) for h in heads]
            dk_c = [_dot_tn(ds[h][1], q[h]) for h in heads]
            dv_c = [_dot_tn(pb[h][1], dob[h]) for h in heads]
            dk_p = [_dot_tn(ds[h][0], q[h]) for h in heads]
            dv_p = [_dot_tn(pb[h][0], dob[h]) for h in heads]
            for h, hs in enumerate(hss):
                dq_ref[rs, hs] = dq[h]
                dk_ref[rs, hs] = pend_k[h] + dk_c[h]
                dv_ref[rs, hs] = pend_v[h] + dv_c[h]
            pend_k, pend_v = dk_p, dv_p
        for h, hs in enumerate(hss):
            ck[:, hs] = pend_k[h]
            cv[:, hs] = pend_v[h]

    cur = pl.BlockSpec((None, rows, ATT_GW), lambda r, n: (r, ns - 1 - n, 0))
    prev = pl.BlockSpec((None, ATT_BLK, ATT_GW), lambda r, n: (r, jnp.maximum((ns - 1 - n) * nq - 1, 0), 0))
    shp = jax.ShapeDtypeStruct(q2.shape, F32)
    dq, dk, dv = pl.pallas_call(
        body,
        out_shape=[shp, shp, shp],
        grid=(d, ns),
        in_specs=[cur, cur, prev, cur, prev, cur, cur, cur, cur],
        out_specs=[cur, cur, cur],
        scratch_shapes=[pltpu.VMEM((ATT_BLK, ATT_GW), F32), pltpu.VMEM((ATT_BLK, ATT_GW), F32)],
        compiler_params=_cparams(("parallel", "arbitrary")),
        name=f"attn_bwd_g{g}",
    )(q2, k2, k2, v2, v2, o2, l2, do2, dl2)
    return _att_unview(dq, d), _att_unview(dk, d), _att_unview(dv, d)


def _attn_fwd_1blk(qb, kb, vb, g):
    d = ATT_PATTERNS[g][1]
    q2, k2, v2 = _att_views([qb, kb, vb], d)
    nb = q2.shape[1] // ATT_BLK

    def body(q_ref, kc_ref, kp_ref, vc_ref, vp_ref, o_ref, l_ref):
        has_prev = pl.program_id(1) > 0
        for h in range(ATT_HEADS):
            hs = slice(h * ATT_DH, (h + 1) * ATT_DH)
            s_p, s_c = _att_scores(q_ref[:, hs], kp_ref[:, hs], kc_ref[:, hs], has_prev)
            m = jnp.maximum(jnp.max(s_p, axis=1, keepdims=True), jnp.max(s_c, axis=1, keepdims=True))
            p_p, p_c = jnp.exp(s_p - m), jnp.exp(s_c - m)
            l = jnp.sum(p_p, axis=1, keepdims=True) + jnp.sum(p_c, axis=1, keepdims=True)
            o = _dot(p_p.astype(BF16), vp_ref[:, hs]) + _dot(p_c.astype(BF16), vc_ref[:, hs])
            o_ref[:, hs] = o / l
            l_ref[:, hs] = jnp.broadcast_to(m + jnp.log(l), (ATT_BLK, ATT_DH))

    cur = pl.BlockSpec((None, ATT_BLK, ATT_GW), lambda r, n: (r, n, 0))
    prev = pl.BlockSpec((None, ATT_BLK, ATT_GW), lambda r, n: (r, jnp.maximum(n - 1, 0), 0))
    o, lse = pl.pallas_call(
        body,
        out_shape=[jax.ShapeDtypeStruct(q2.shape, F32)] * 2,
        grid=(d, nb),
        in_specs=[cur, cur, prev, cur, prev],
        out_specs=[cur, cur],
        compiler_params=_cparams(("parallel", "arbitrary")),
        name=f"attn_fwd_g{g}",
    )(q2, k2, k2, v2, v2)
    return _att_unview(o, d), _att_unview(lse, d)


def _attn_bwd_1blk(qb, kb, vb, o, lse, d_o, d_lse, g):
    d = ATT_PATTERNS[g][1]
    q2, k2, v2 = _att_views([qb, kb, vb], d)
    o2, l2, do2, dl2 = _att_views([o, lse, d_o, d_lse], d)
    nb = q2.shape[1] // ATT_BLK

    def body(q_ref, kc_ref, kp_ref, vc_ref, vp_ref, o_ref, l_ref, do_ref, dl_ref, dq_ref, dk_ref, dv_ref, ck, cv):
        n = pl.program_id(1)
        active = n < nb

        @pl.when(n == 0)
        def _():
            ck[...] = jnp.zeros_like(ck)
            cv[...] = jnp.zeros_like(cv)

        @pl.when(jnp.logical_not(active))
        def _():
            dk_ref[...] = ck[...]
            dv_ref[...] = cv[...]

        @pl.when(active)
        def _():
            has_prev = n > 0
            for h in range(ATT_HEADS):
                hs = slice(h * ATT_DH, (h + 1) * ATT_DH)
                q, kp, kc, vp, vc = q_ref[:, hs], kp_ref[:, hs], kc_ref[:, hs], vp_ref[:, hs], vc_ref[:, hs]
                s_p, s_c = _att_scores(q, kp, kc, has_prev)
                lse_h = l_ref[:, hs][:, 0:1]
                p_p, p_c = jnp.exp(s_p - lse_h), jnp.exp(s_c - lse_h)
                do = do_ref[:, hs]
                delta = jnp.sum(do * o_ref[:, hs] - dl_ref[:, hs], axis=1, keepdims=True)
                dob = do.astype(BF16)
                scale = ATT_DH ** -0.5
                ds_p = (p_p * (_dot_nt(dob, vp) - delta) * scale).astype(BF16)
                ds_c = (p_c * (_dot_nt(dob, vc) - delta) * scale).astype(BF16)
                dq_ref[:, hs] = _dot(ds_p, kp) + _dot(ds_c, kc)
                dk_ref[:, hs] = ck[:, hs] + _dot_tn(ds_p, q)
                dv_ref[:, hs] = cv[:, hs] + _dot_tn(p_p.astype(BF16), dob)
                ck[:, hs] = _dot_tn(ds_c, q)
                cv[:, hs] = _dot_tn(p_c.astype(BF16), dob)

    def qn(n):
        return jnp.minimum(n, nb - 1)

    cur = pl.BlockSpec((None, ATT_BLK, ATT_GW), lambda r, n: (r, qn(n), 0))
    prev = pl.BlockSpec((None, ATT_BLK, ATT_GW), lambda r, n: (r, jnp.maximum(qn(n) - 1, 0), 0))
    behind = pl.BlockSpec((None, ATT_BLK, ATT_GW), lambda r, n: (r, jnp.maximum(n - 1, 0), 0))
    shp = jax.ShapeDtypeStruct(q2.shape, F32)
    dq, dk, dv = pl.pallas_call(
        body,
        out_shape=[shp, shp, shp],
        grid=(d, nb + 1),
        in_specs=[cur, cur, prev, cur, prev, cur, cur, cur, cur],
        out_specs=[cur, behind, behind],
        scratch_shapes=[pltpu.VMEM((ATT_BLK, ATT_GW), F32), pltpu.VMEM((ATT_BLK, ATT_GW), F32)],
        compiler_params=_cparams(("parallel", "arbitrary")),
        name=f"attn_bwd_g{g}",
    )(q2, k2, k2, v2, v2, o2, l2, do2, dl2)
    return _att_unview(dq, d), _att_unview(dk, d), _att_unview(dv, d)


def _rms_parts(x, width):
    outs = []
    for lo in range(0, x.shape[1], width):
        xs = x[:, lo:lo + width].astype(F32)
        r = lax.rsqrt(jnp.mean(xs * xs, axis=1, keepdims=True) + EPS)
        outs.append((xs * r, r))
    return outs


def _rms_bwd_part(xh, r, dxh):
    return r * (dxh - xh * jnp.mean(dxh * xh, axis=1, keepdims=True))


def _norm_pro(a, consts):
    (xh, _), = _rms_parts(a[0], a[0].shape[1])
    return [(xh * consts[0]).astype(BF16)]


def _norm_bwd_fin(accs, ex, consts):
    xv, dres = ex
    (xh, r), = _rms_parts(xv, xv.shape[1])
    return [dres + _rms_bwd_part(xh, r, accs[0] * consts[0])], [_colsum8(accs[0] * xh)]


def _norm_fwd(x, gain):
    d = x.shape[1]

    def fn(ins, consts):
        (xh, _), = _rms_parts(ins[0], d)
        return [xh * consts[0]], []

    (h,), _ = _rowwise(fn, [(x, d, 0)], [gain.reshape(1, d)], [(d, BF16)], [], bm=512, name="norm_fwd")
    return h


def _norm_bwd(x, gain, dh, dres):
    d = x.shape[1]

    def fn(ins, consts):
        (xh, r), = _rms_parts(ins[0], d)
        dx = ins[2] + _rms_bwd_part(xh, r, ins[1] * consts[0])
        return [dx], [_colsum8(ins[1] * xh)]

    (dx,), (dg,) = _rowwise(fn, [(x, d, 0), (dh, d, 0), (dres, d, 0)], [gain.reshape(1, d)], [(d, F32)], [d],
                            bm=512, name="norm_bwd")
    return dx, dg


def _rot_sign():
    lane = lax.broadcasted_iota(jnp.int32, (1, ATT_DH), 1)
    return jnp.where(lane < ATT_DH // 2, -1.0, 1.0).astype(F32)


def _rope(y, cos, sin):
    return y * cos + pltpu.roll(y, ATT_DH // 2, axis=1) * _rot_sign() * sin


def _rope_t(dy, cos, sin):
    return dy * cos - pltpu.roll(dy * sin, ATT_DH // 2, axis=1) * _rot_sign()


def _qk_prep(zq, zk, zv, qn, kn, cos, sin):
    w = zq.shape[1]

    def fn(ins, consts):
        cs, sn = ins[3], ins[4]
        outs = []
        for z, gain in ((ins[0], consts[0]), (ins[1], consts[1])):
            for i, (xh, _) in enumerate(_rms_parts(z, ATT_DH)):
                outs.append(_rope(xh * gain[:, i * ATT_DH:(i + 1) * ATT_DH], cs, sn))
        outs += [ins[2][:, i * ATT_DH:(i + 1) * ATT_DH] for i in range(w // ATT_DH)]
        groups = [jnp.concatenate(outs[i:i + ATT_HEADS], axis=1) for i in range(0, len(outs), ATT_HEADS)]
        return groups, []

    outs, _ = _rowwise(fn, [(zq, w, 0), (zk, w, 0), (zv, w, 0), (cos, ATT_DH, 0), (sin, ATT_DH, 0)], [qn, kn],
                       [(ATT_GW, BF16, ATT_PATTERNS[g][1]) for g in range(ATT_GROUPS)] * 3, [], bm=256, name="qk_prep")
    return outs[0:3], outs[3:6], outs[6:9]


def _qk_prep_bwd(zq, zk, dq_g, dk_g, dv_g, qn, kn, cos, sin):
    w = zq.shape[1]

    def fn(ins, consts):
        cs, sn = ins[2], ins[3]
        outs, sums = [], []
        for z, gain, dparts in ((ins[0], consts[0], ins[4:7]), (ins[1], consts[1], ins[7:10])):
            dout = jnp.concatenate(dparts, axis=1)
            dz, dgain = [], []
            for i, (xh, r) in enumerate(_rms_parts(z, ATT_DH)):
                hs = slice(i * ATT_DH, (i + 1) * ATT_DH)
                dy = _rope_t(dout[:, hs], cs, sn)
                dgain.append(_colsum8(dy * xh))
                dz.append(_rms_bwd_part(xh, r, dy * gain[:, hs]))
            outs.append(jnp.concatenate(dz, axis=1))
            sums.append(jnp.concatenate(dgain, axis=1))
        outs.append(jnp.concatenate(ins[10:13], axis=1))
        return outs, sums

    ins = [(zq, w, 0), (zk, w, 0), (cos, ATT_DH, 0), (sin, ATT_DH, 0)]
    for parts in (dq_g, dk_g, dv_g):
        ins += [(a, ATT_GW, 0, ATT_PATTERNS[g][1]) for g, a in enumerate(parts)]
    (dzq, dzk, dzv), (dqn, dkn) = _rowwise(fn, ins, [qn, kn], [(w, BF16)] * 3, [w, w], bm=256, name="qk_prep_bwd")
    return dzq, dzk, dzv, dqn, dkn


def _post_a(o_raw, zh, gout):
    w = o_raw.shape[1]

    def fn(ins, consts):
        oh = jnp.concatenate([xh for xh, _ in _rms_parts(ins[0], HG_DK)], axis=1)
        hg = ins[1]
        return [oh * consts[0] * (hg * _sigmoid(hg))], []

    (y,), _ = _rowwise(fn, [(o_raw, w, 0), (zh, w, 3)], [gout.reshape(1, w)], [(w, BF16)], [], bm=512, name="post_a")
    return y


def _post_a_bwd(o_raw, zh, gout, dy):
    w = o_raw.shape[1]

    def fn(ins, consts):
        parts = _rms_parts(ins[0], HG_DK)
        oh = jnp.concatenate([xh for xh, _ in parts], axis=1)
        hg, dyv, gain = ins[1], ins[2], consts[0]
        sg = _sigmoid(hg)
        s = hg * sg
        doh = dyv * gain * s
        do = jnp.concatenate([_rms_bwd_part(xh, r, doh[:, i * HG_DK:(i + 1) * HG_DK]) for i, (xh, r) in enumerate(parts)], axis=1)
        dhg = dyv * oh * gain * (sg * (1.0 + hg * (1.0 - sg)))
        return [do, dhg], [_colsum8(dyv * oh * s)]

    (do, dhg), (dgain,) = _rowwise(fn, [(o_raw, w, 0), (zh, w, 3), (dy, w, 0)], [gout.reshape(1, w)],
                                   [(w, F32), (w, BF16)], [w], bm=512, name="post_a_bwd")
    return do, dhg, dgain


def _merge_alpha(lses):
    m = jnp.maximum(jnp.maximum(lses[0], lses[1]), lses[2])
    e = [jnp.exp(l - m) for l in lses]
    inv = 1.0 / (e[0] + e[1] + e[2])
    return [x * inv for x in e]


def _group_ins(parts):
    return [(a, ATT_GW, 0, ATT_PATTERNS[g][1]) for g, a in enumerate(parts)]


def _merge_b(o_g, lse_g):
    def fn(ins, consts):
        al = _merge_alpha(ins[3:6])
        return [al[0] * ins[0] + al[1] * ins[1] + al[2] * ins[2]], []

    (y,), _ = _rowwise(fn, _group_ins(o_g) + _group_ins(lse_g), [], [(ATT_GW, BF16)], [], bm=512, name="merge_b")
    return y


def _merge_b_bwd(o_g, lse_g, dy):
    def fn(ins, consts):
        al = _merge_alpha(ins[3:6])
        dyv = ins[6]
        dal = [dyv * ins[i] for i in range(3)]
        tot = al[0] * dal[0] + al[1] * dal[1] + al[2] * dal[2]
        return [al[i] * dyv for i in range(3)] + [al[i] * (dal[i] - tot) for i in range(3)], []

    outs, _ = _rowwise(fn, _group_ins(o_g) + _group_ins(lse_g) + [(dy, ATT_GW, 0)], [],
                       [(ATT_GW, F32, ATT_PATTERNS[g][1]) for g in range(ATT_GROUPS)] * 2, [], bm=512, name="merge_b_bwd")
    return outs[:3], outs[3:]


def _loss_head(y, target):
    d = y.shape[1]

    def fn(ins, consts):
        e = ins[0] - ins[1]
        return [e * (1.0 / d)], [_colsum8(e * e)]

    (dy,), (sq,) = _rowwise(fn, [(y, d, 0), (target, d, 0)], [], [(d, F32)], [d], bm=512, name="loss_head")
    return 0.5 * jnp.sum(sq) / d, dy


def _silu_grad(a):
    s = _sigmoid(a)
    return s * (1.0 + a * (1.0 - s))


def _ffn_fwd(x, gain, wt, wo_fn, tag):
    t, d = x.shape
    f = wt.shape[0] // 2

    def act(accs, ex, consts):
        a, b = accs
        s = _sigmoid(a)
        sa = a * s
        return (sa * b, b, 0.5 * sa, 0.5 * (s + sa * (1.0 - s)))

    bn = FFN_BN if f % FFN_BN == 0 else 256
    u, b, sa, sp, h = _mm([x], [wt, wt], [(0, 0, 0), (0, 1, 1)], 2, act, [BF16] * 4, m=t, n=f, k=d, tb=True,
                          bm=512, bn=bn, bk=d, b_off=[(0, 0), (f // min(bn, f), 0)],
                          consts=[gain.reshape(1, d)], a_pro=_norm_pro, name=f"ffn_in_{tag}")
    wo = wo_fn(u)
    (y,) = _mm([u], [wo], [(0, 0, 0)], 1, lambda accs, ex: (ex[0] + 0.5 * accs[0],), [F32], m=t, n=d, k=f,
               bm=512, bn=d, bk=f, extras=[x], name=f"ffn_out_{tag}")
    return y, (x, h, u, b, sa, sp, wo)


def _ffn_bwd(dy, saved, gain, wt, tag, tok, emit):
    x, h, u, b, sa, sp, wo = saved
    t, d = x.shape
    f = wo.shape[0]
    dyb = (dy + tok).astype(BF16)

    def dact(accs, ex):
        bv, sav, spv = (e.astype(F32) for e in ex)
        return (accs[0] * bv * spv, accs[0] * sav)

    bn = FFN_BN if f % FFN_BN == 0 else 256
    da, db = _mm([dyb], [wo], [(0, 0, 0)], 1, dact, [BF16, BF16], m=t, n=f, k=d, tb=True, bm=512, bn=bn, bk=d,
                 extras=[b, sa, sp], n_outer=True, name=f"ffn_dact_{tag}")
    (dwo,) = _mm([u], [dyb], [(0, 0, 0)], 1, lambda accs, ex: (0.5 * accs[0],), [BF16], m=f, n=d, k=t, ta=True,
                 bm=1408, bn=d, bk=1024, name=f"ffn_dwo_{tag}")
    dwt = [_mm([g], [h], [(0, 0, 0)], 1, _first, [BF16], m=f, n=d, k=t, ta=True, bm=1408, bn=d, bk=1024,
               name=f"ffn_dwt{i}_{tag}")[0] for i, g in enumerate((da, db))]
    tok = emit(jnp.concatenate(dwt, axis=0), dwo)
    bk = min(FFN_BN, f)
    dx, dgain = _mm([da, db], [wt, wt], [(0, 0, 0), (1, 1, 0)], 1, _norm_bwd_fin, [F32], m=t, n=d, k=f, bm=512, bn=d,
                    bk=bk, b_off=[(0, 0), (0, f // bk)], extras=[x, dy], consts=[(gain + tok).reshape(1, d)], n_sums=1,
                    name=f"ffn_dh_{tag}")
    return dx, jnp.sum(dgain, axis=0), tok


FFN_BN = 1408
Z_SPLITS = (("h", 4096), ("q", 1536), ("k", 1536), ("v", 1536), ("g", 2048))


def _mix_fwd(x, p, cos, sin):
    t, d = x.shape
    z, off, hm = {}, 0, None
    for nm, width in Z_SPLITS:
        bn = 1024 if off % 1024 == 0 and width % 1024 == 0 else 512
        first = hm is None
        res = _mm([x if first else hm], [p["wint"]], [(0, 0, 0)], 1, (lambda accs, ex, consts: (accs[0],)) if first else _first,
                  [F32 if nm == "h" else BF16], m=t, n=width, k=d, tb=True, bm=1024, bn=bn, bk=d, b_off=[(off // bn, 0)],
                  consts=[p["gm"].reshape(1, d)] if first else (), a_pro=_norm_pro if first else None, name=f"mix_in_{nm}")
        z[nm] = res[0]
        hm = res[1] if first else hm
        off += width
    o_raw, states = _hgrn_fwd(z["h"], p["lb3"])
    qb, kb, vb = _qk_prep(z["q"], z["k"], z["v"], p["qn"], p["kn"], cos, sin)
    o_g, lse_g = zip(*[_attn_fwd(qb[g], kb[g], vb[g], g) for g in range(ATT_GROUPS)])
    oa = _post_a(o_raw, z["h"], p["gout"])
    ob = _merge_b(o_g, lse_g)
    late = p["late"](ob)
    p = dict(p, **late)
    (ya,) = _mm([oa], [p["wa"]], [(0, 0, 0)], 1, _first, [F32], m=t, n=d, k=oa.shape[1], bm=1024, bn=d, bk=oa.shape[1],
                name="branch_a")

    def gate(accs, ex):
        return (_sigmoid(ex[0].astype(F32)) * ex[2] + _sigmoid(ex[1].astype(F32)) * accs[0], accs[0])

    merged, yb = _mm([ob], [p["wbt"]], [(0, 0, 0)], 1, gate, [BF16, F32], m=t, n=d, k=ATT_GW, tb=True, bm=512, bn=d,
                     bk=ATT_GW, extras=[z["g"], z["g"], ya], e_off=[0, 1, 0], name="branch_b_gate")
    (y,) = _mm([merged], [p["wo"]], [(0, 0, 0)], 1, lambda accs, ex: (ex[0] + accs[0],), [F32], m=t, n=d, k=d,
               bm=1024, bn=d, bk=d, extras=[x], name="mix_out")
    return y, (x, hm, z, o_raw, states, qb, kb, vb, o_g, lse_g, oa, ob, ya, yb, merged, late)


def _mix_bwd(dy, saved, p, cos, sin, tok):
    x, hm, z, o_raw, states, qb, kb, vb, o_g, lse_g, oa, ob, ya, yb, merged, late = saved
    p = dict(p, **late)
    t, d = x.shape
    w = oa.shape[1]
    dyb = (dy + tok).astype(BF16)

    def dgate(accs, ex):
        dm = accs[0]
        sa, sb = _sigmoid(ex[0].astype(F32)), _sigmoid(ex[1].astype(F32))
        return (sa * dm, sb * dm, dm * ex[2] * sa * (1.0 - sa), dm * ex[3] * sb * (1.0 - sb))

    dya, dyb_, dga, dgb = _mm([dyb], [p["wo"]], [(0, 0, 0)], 1, dgate, [BF16] * 4, m=t, n=d, k=d, tb=True, bm=512, bn=d,
                              bk=d, extras=[z["g"], z["g"], ya, yb], e_off=[0, 1, 0, 0], name="mix_out_bwd")
    (dwo,) = _mm([merged], [dyb], [(0, 0, 0)], 1, _first, [BF16], m=d, n=d, k=t, ta=True, bm=d, bn=d, bk=1024, name="mix_dwo")
    (doa,) = _mm([dya], [p["wa"]], [(0, 0, 0)], 1, _first, [F32], m=t, n=w, k=d, tb=True, bm=1024, bn=w, bk=d, name="branch_a_bwd")
    (dwa,) = _mm([oa], [dya], [(0, 0, 0)], 1, _first, [BF16], m=w, n=d, k=t, ta=True, bm=w, bn=d, bk=1024, name="branch_a_dw")
    (dob,) = _mm([dyb_], [p["wbt"]], [(0, 0, 0)], 1, _first, [F32], m=t, n=ATT_GW, k=d, bm=1024, bn=ATT_GW, bk=d,
                 name="branch_b_bwd")
    (dwbt,) = _mm([dyb_], [ob], [(0, 0, 0)], 1, _first, [BF16], m=d, n=ATT_GW, k=t, ta=True, bm=d, bn=ATT_GW, bk=1024,
                  name="branch_b_dw")
    do_raw, dhg, dgout = _post_a_bwd(o_raw, z["h"], p["gout"], doa)
    do_g, dlse_g = _merge_b_bwd(o_g, lse_g, dob)
    dq_g, dk_g, dv_g = zip(*[_attn_bwd(qb[g], kb[g], vb[g], o_g[g], lse_g[g], do_g[g], dlse_g[g], g)
                             for g in range(ATT_GROUPS)])
    dzq, dzk, dzv, dqn, dkn = _qk_prep_bwd(z["q"], z["k"], dq_g, dk_g, dv_g, p["qn"], p["kn"], cos, sin)
    dhq, dhf, dhi, lbsum = _hgrn_bwd(z["h"], p["lb3"], states, do_raw)
    dz = jnp.concatenate([dhq, dhf, dhi, dhg, dzq, dzk, dzv, dga, dgb], axis=1)
    pw = dz.shape[1]
    (dwint,) = _mm([dz], [hm], [(0, 0, 0)], 1, _first, [BF16], m=pw, n=d, k=t, ta=True, bm=1536, bn=d, bk=1024, name="mix_in_dw")
    dx, dgm = _mm([dz], [p["wint"]], [(0, 0, 0)], 1, _norm_bwd_fin, [F32], m=t, n=d, k=pw, bm=512, bn=d, bk=1536,
                  extras=[x, dy], consts=[p["gm"].reshape(1, d)], n_sums=1, name="mix_in_bwd")
    return dx, dict(gm=jnp.sum(dgm, axis=0), wint=dwint, lbsum=lbsum, gout=dgout, qn=dqn, kn=dkn, wa=dwa, wbt=dwbt, wo=dwo)


def _rope_tables(t):
    pos = jnp.arange(t, dtype=F32)
    inv = ROPE_THETA ** (-jnp.arange(0, ATT_DH, 2, dtype=F32) / ATT_DH)
    ang = pos[:, None] * inv[None, :]
    ang = jnp.concatenate([ang, ang], axis=-1)
    return jnp.cos(ang), jnp.sin(ang)


def _lower_bounds(logits):
    lb = jnp.cumsum(jax.nn.softmax(logits, axis=0), axis=0)
    return lb - lb[0:1]


def _head_gain(g):
    return jnp.tile(g[:, None, :], (1, ATT_HEADS, 1)).reshape(1, ATT_GROUPS * ATT_GW)


SMALL_GRADS = ("ffn1_norm", "mix_norm", "lbsum", "hgrn_out_norm", "attn_q_norm", "attn_k_norm", "ffn2_norm")


def _local_step(x, target, small, fetch, emit):
    t = x.shape[0]
    depth = small["ffn1_norm"].shape[0]
    cos, sin = _rope_tables(t)
    lb_all = _lower_bounds(small["hgrn_lb_logits"])
    saved = []
    for l in range(depth):
        w1t = fetch("w1t", l, x)["w1t"]
        x, s1 = _ffn_fwd(x, small["ffn1_norm"][l], w1t, lambda after, l=l: fetch("w1o", l, after)["w1o"], "1")
        p = dict(gm=small["mix_norm"][l], wint=fetch("wint", l, x)["wint"], lb3=lb_all[l].reshape(-1, 1, HG_DK),
                 gout=small["hgrn_out_norm"][l], qn=_head_gain(small["attn_q_norm"][l]),
                 kn=_head_gain(small["attn_k_norm"][l]), late=functools.partial(fetch, "mout", l))
        x, sm = _mix_fwd(x, p, cos, sin)
        w2t = fetch("w2t", l, x)["w2t"]
        x, s2 = _ffn_fwd(x, small["ffn2_norm"][l], w2t, lambda after, l=l: fetch("w2o", l, after)["w2o"], "2")
        saved.append((p, w1t, w2t, s1, sm, s2))
    loss, dx = _loss_head(x, target)
    gsmall = {k: [None] * depth for k in SMALL_GRADS}
    tok = jnp.zeros((), F32)
    for l in reversed(range(depth)):
        p, w1t, w2t, s1, sm, s2 = saved[l]
        dx, gsmall["ffn2_norm"][l], tok = _ffn_bwd(dx, s2, small["ffn2_norm"][l], w2t, "2", tok,
                                                   lambda dwt, dwo, l=l: emit("ffn2", l, dict(w2t=dwt, w2o=dwo), None))
        dx, gm = _mix_bwd(dx, sm, p, cos, sin, tok)
        tok = emit("mix", l, {k: gm[k] for k in ("wint", "wa", "wbt", "wo")}, None)
        gsmall["mix_norm"][l], gsmall["lbsum"][l], gsmall["hgrn_out_norm"][l] = gm["gm"], gm["lbsum"], gm["gout"]
        for k, src in (("attn_q_norm", "qn"), ("attn_k_norm", "kn")):
            gsmall[k][l] = jnp.sum(gm[src].reshape(ATT_GROUPS, ATT_HEADS, ATT_DH), axis=1)
        dx, gsmall["ffn1_norm"][l], tok = _ffn_bwd(dx, s1, small["ffn1_norm"][l], w1t, "1", tok,
                                                   lambda dwt, dwo, l=l: emit("ffn1", l, dict(w1t=dwt, w1o=dwo), None))
    emit("small", 0, {}, ({k: jnp.stack(v) for k, v in gsmall.items()}, loss))
    return dx


_HBM = pl.BlockSpec(memory_space=pltpu.HBM)
_SEM = pl.BlockSpec(memory_space=pltpu.SEMAPHORE)
_EFFECT = pltpu.SideEffectType.DATAFLOW_SIDE_EFFECTING


def _peer(p):
    x, y, c = lax.axis_index("x"), lax.axis_index("y"), lax.axis_index("c")
    me = 4 * x + 2 * y + c
    return (1 - x if p & 4 else x, 1 - y if p & 2 else y, 1 - c if p & 1 else c), jnp.bitwise_xor(me, p), me


def _xchg_copy(src, land, mode, send_sems, recv_sems, k, p, arriving):
    peer, peer_id, me = _peer(p)
    block = src if mode == "gather" else src.at[peer_id]
    return pltpu.make_async_remote_copy(
        src_ref=block, dst_ref=land.at[peer_id if arriving else me], send_sem=send_sems.at[k * (N_DEV - 1) + p - 1],
        recv_sem=recv_sems.at[k * (N_DEV - 1) + p - 1], device_id=peer, device_id_type=MESH)


def _xchg_start(srcs, modes, groups, name):
    n, ng = len(srcs), len(groups)

    def body(*refs):
        src = refs[:n]
        sems = refs[n:n + 2 * ng]
        land = refs[n + 2 * ng + n:n + 2 * ng + 2 * n]
        token = refs[n + 2 * ng + 2 * n]
        for gi, idx in enumerate(groups):
            for ki, k in enumerate(idx):
                for p in range(1, N_DEV):
                    _xchg_copy(src[k], land[k], modes[k], sems[2 * gi], sems[2 * gi + 1], ki, p, False).start()
        token[...] = jnp.zeros_like(token)

    sem_shapes = []
    for idx in groups:
        sem_shapes += [pltpu.SemaphoreType.DMA((len(idx) * (N_DEV - 1),))] * 2
    outs = pl.pallas_call(
        body,
        out_shape=sem_shapes + [pltpu.HBM(a.shape, a.dtype) for a in srcs]
        + [pltpu.HBM((N_DEV,) + a.shape[-2:], a.dtype) for a in srcs] + [jax.ShapeDtypeStruct((8, 128), F32)],
        in_specs=[_HBM] * n,
        out_specs=[_SEM] * (2 * ng) + [_HBM] * (2 * n) + [pl.BlockSpec(memory_space=pltpu.VMEM)],
        input_output_aliases={i: 2 * ng + i for i in range(n)},
        compiler_params=pltpu.CompilerParams(has_side_effects=_EFFECT),
        name=name,
    )(*[pltpu.with_memory_space_constraint(a, pltpu.HBM) for a in srcs])
    sems = [(outs[2 * gi], outs[2 * gi + 1]) for gi in range(ng)]
    return sems, outs[2 * ng:2 * ng + n], outs[2 * ng + n:2 * ng + 2 * n], outs[-1]


def _xchg_wait_call(srcs, lands, modes, sems, after, name):
    n = len(srcs)

    def body(*refs):
        src, land = refs[:n], refs[n:2 * n]
        send_sems, recv_sems = refs[2 * n], refs[2 * n + 1]
        for p in range(1, N_DEV):
            for k in range(n):
                cp = _xchg_copy(src[k], land[k], modes[k], send_sems, recv_sems, k, p, True)
                cp.wait_send()
                cp.wait_recv()

    outs = pl.pallas_call(
        body,
        out_shape=[pltpu.HBM(a.shape, a.dtype) for a in list(srcs) + list(lands)],
        in_specs=[_HBM] * (2 * n) + [_SEM, _SEM, pl.BlockSpec(memory_space=pl.ANY)],
        out_specs=[_HBM] * (2 * n),
        input_output_aliases={i: i for i in range(2 * n)},
        compiler_params=pltpu.CompilerParams(has_side_effects=_EFFECT),
        name=name,
    )(*srcs, *lands, sems[0], sems[1], after)
    return outs[:n], outs[n:]


def _xchg_wait(srcs, lands, modes, sems, after, name):
    srcs, lands = _xchg_wait_call(srcs, lands, modes, sems, after, name)
    me = 4 * lax.axis_index("x") + 2 * lax.axis_index("y") + lax.axis_index("c")
    done = []
    for a, land, mode in zip(srcs, lands, modes):
        own = a[None] if mode == "gather" else lax.dynamic_slice_in_dim(a, me, 1, axis=0)
        done.append(lax.dynamic_update_slice(land, own, (me, 0, 0)))
    return done


def _sum_slots(land):
    g, _, r, c = land.shape
    br = r // 2 if (r % 32 == 0 and r >= 256) else r

    def body(l_ref, o_ref):
        acc = l_ref[0, 0].astype(F32)
        for j in range(1, N_DEV):
            acc = acc + l_ref[0, j].astype(F32)
        o_ref[0] = acc

    return pl.pallas_call(
        body,
        out_shape=jax.ShapeDtypeStruct((g, r, c), F32),
        grid=(g, r // br),
        in_specs=[pl.BlockSpec((1, N_DEV, br, c), lambda i, j: (i, 0, j, 0))],
        out_specs=pl.BlockSpec((1, br, c), lambda i, j: (i, j, 0)),
        compiler_params=_cparams(("parallel", "parallel")),
        name="sum_slots",
    )(land)


def _adamw(w, g, m, v):
    shape = w.shape
    cols = shape[-1]
    rows = int(np.prod(shape[:-1]))
    bm = max(b for b in range(8, 257, 8) if rows % b == 0) if rows % 8 == 0 else rows
    c1 = 1.0 - ADAM_B1 ** ADAM_STEP
    c2 = 1.0 - ADAM_B2 ** ADAM_STEP

    def fn(ins, consts):
        wv, gv, mv, vv = ins
        m2 = ADAM_B1 * mv + (1.0 - ADAM_B1) * gv
        v2 = ADAM_B2 * vv + (1.0 - ADAM_B2) * (gv * gv)
        delta = -ADAM_LR * ((m2 / c1) / (jnp.sqrt(v2 / c2) + ADAM_EPS) + ADAM_WD * wv)
        return [delta, m2, v2], []

    outs, _ = _rowwise(fn, [(a.reshape(rows, cols), cols, 0) for a in (w, g, m, v)], [], [(cols, F32)] * 3, [],
                       bm=bm, name="adamw")
    return [o.reshape(shape) for o in outs]


BIG = ("w1t", "w1o", "wint", "wa", "wbt", "wo", "w2t", "w2o")
FETCH_GROUPS = dict(w1t=("w1t",), w1o=("w1o",), wint=("wint",), mout=("wa", "wbt", "wo"), w2t=("w2t",), w2o=("w2o",))
SMALL_ROWS = (("ffn1_norm", 0), ("mix_norm", 2), ("lbsum", 4), ("hgrn_out_norm", 6), ("ffn2_norm", 8),
              ("attn_q_norm", 10), ("attn_k_norm", 12))
SMALL_PACK_ROWS = 16


def kernel(x, ffn1_norm, ffn1_w_in, ffn1_w_out, mix_norm, w_in, hgrn_lb_logits, hgrn_out_norm, attn_q_norm, attn_k_norm, w_branch_a, w_branch_b, w_out, ffn2_norm, ffn2_w_in, ffn2_w_out, loss_target, m_ffn1_norm, m_ffn1_w_in, m_ffn1_w_out, m_mix_norm, m_w_in, m_hgrn_lb_logits, m_hgrn_out_norm, m_attn_q_norm, m_attn_k_norm, m_w_branch_a, m_w_branch_b, m_w_out, m_ffn2_norm, m_ffn2_w_in, m_ffn2_w_out, v_ffn1_norm, v_ffn1_w_in, v_ffn1_w_out, v_mix_norm, v_w_in, v_hgrn_lb_logits, v_hgrn_out_norm, v_attn_q_norm, v_attn_k_norm, v_w_branch_a, v_w_branch_b, v_w_out, v_ffn2_norm, v_ffn2_w_in, v_ffn2_w_out):
    names = ("ffn1_norm", "ffn1_w_in", "ffn1_w_out", "mix_norm", "w_in", "hgrn_lb_logits", "hgrn_out_norm", "attn_q_norm",
             "attn_k_norm", "w_branch_a", "w_branch_b", "w_out", "ffn2_norm", "ffn2_w_in", "ffn2_w_out")
    w = dict(zip(names, (ffn1_norm, ffn1_w_in, ffn1_w_out, mix_norm, w_in, hgrn_lb_logits, hgrn_out_norm, attn_q_norm,
                         attn_k_norm, w_branch_a, w_branch_b, w_out, ffn2_norm, ffn2_w_in, ffn2_w_out)))
    m = dict(zip(names, (m_ffn1_norm, m_ffn1_w_in, m_ffn1_w_out, m_mix_norm, m_w_in, m_hgrn_lb_logits, m_hgrn_out_norm,
                         m_attn_q_norm, m_attn_k_norm, m_w_branch_a, m_w_branch_b, m_w_out, m_ffn2_norm, m_ffn2_w_in, m_ffn2_w_out)))
    v = dict(zip(names, (v_ffn1_norm, v_ffn1_w_in, v_ffn1_w_out, v_mix_norm, v_w_in, v_hgrn_lb_logits, v_hgrn_out_norm,
                         v_attn_q_norm, v_attn_k_norm, v_w_branch_a, v_w_branch_b, v_w_out, v_ffn2_norm, v_ffn2_w_in, v_ffn2_w_out)))
    depth, d = ffn1_norm.shape

    def tr(a):
        return jnp.swapaxes(a, 1, 2)

    shard = dict(w1t=tr(ffn1_w_in), w1o=ffn1_w_out, wint=tr(w_in), wa=w_branch_a,
                 wbt=tr(w_branch_b).reshape(depth, -1, d), wo=w_out, w2t=tr(ffn2_w_in), w2o=ffn2_w_out)
    order = [(g, l) for l in range(depth) for g in FETCH_GROUPS]
    flat = [(g, l, k) for g, l in order for k in FETCH_GROUPS[g]]
    groups, pos = [], 0
    for g, l in order:
        groups.append(list(range(pos, pos + len(FETCH_GROUPS[g]))))
        pos += len(FETCH_GROUPS[g])
    g_sems, g_srcs, g_lands, _ = _xchg_start([shard[k][l].astype(BF16) for _, l, k in flat], ["gather"] * len(flat),
                                             groups, "gather_start")

    def fetch(group, l, after):
        gi = order.index((group, l))
        idx = groups[gi]
        lands = _xchg_wait([g_srcs[i] for i in idx], [g_lands[i] for i in idx], ["gather"] * len(idx), g_sems[gi], after,
                           f"gather_wait_{group}{l}")
        out = {}
        for k, land in zip(FETCH_GROUPS[group], lands):
            out[k] = land.reshape(d, -1) if k == "wbt" else land.reshape(-1, d)
        return out

    pending = []

    def emit(group, l, g, final):
        keys = list(g)
        srcs = [g[k].reshape(N_DEV, -1, d) for k in keys]
        modes = ["scatter"] * len(keys)
        if final is not None:
            gsmall, loss = final
            pack = jnp.zeros((SMALL_PACK_ROWS, d), F32)
            for k, r0 in SMALL_ROWS:
                rows = gsmall[k].reshape(depth, -1)
                pack = pack.at[r0:r0 + depth, :rows.shape[1]].set(rows)
            srcs.append(pack.at[14, :].set(loss))
            modes.append("gather")
            keys.append("small")
        sems, s_thru, l_thru, token = _xchg_start(srcs, modes, [list(range(len(srcs)))], f"grads_start_{group}{l}")
        pending.append((group, l, keys, modes, sems[0], s_thru, l_thru))
        return token[0, 0]

    small = {k: w[k] for k in ("ffn1_norm", "mix_norm", "hgrn_lb_logits", "hgrn_out_norm", "attn_q_norm", "attn_k_norm", "ffn2_norm")}
    dx = _local_step(x[0], loss_target[0], small, fetch, emit)

    summed = {}
    for group, l, keys, modes, sems, s_thru, l_thru in pending:
        lands = _xchg_wait(s_thru, l_thru, modes, sems, dx, f"grads_wait_{group}{l}")
        for k, land in zip(keys, lands):
            summed[k, l] = _sum_slots(land[None])[0]
    gsum = {k: jnp.stack([summed[k, l] for l in range(depth)]) for k in BIG}
    tot = summed["small", 0]

    grads = {}
    for k, r0 in SMALL_ROWS:
        shp = (depth,) + (w[k].shape[1:] if k != "lbsum" else (d,))
        grads[k] = tot[r0:r0 + depth, :int(np.prod(shp[1:]))].reshape(shp)
    _, lb_vjp = jax.vjp(_lower_bounds, hgrn_lb_logits)
    grads["hgrn_lb_logits"] = lb_vjp(grads.pop("lbsum"))[0]
    grads["ffn1_w_in"], grads["ffn1_w_out"] = tr(gsum["w1t"]), gsum["w1o"]
    grads["w_in"], grads["w_branch_a"] = tr(gsum["wint"]), gsum["wa"]
    grads["w_branch_b"] = tr(gsum["wbt"].reshape(depth, d // N_DEV, -1))
    grads["w_out"] = gsum["wo"]
    grads["ffn2_w_in"], grads["ffn2_w_out"] = tr(gsum["w2t"]), gsum["w2o"]

    upd = {k: _adamw(w[k], grads[k], m[k], v[k]) for k in names}
    return (tot[14, 0], dx[None], *[grads[k] for k in names], *[upd[k][0] for k in names],
            *[upd[k][1] for k in names], *[upd[k][2] for k in names])
```

```python
import functools
import math

import jax
import jax.numpy as jnp
import numpy as np
from jax import lax
from jax.experimental import pallas as pl
from jax.experimental.pallas import tpu as pltpu

F32 = jnp.float32
BF16 = jnp.bfloat16

N_DEV = 8
EPS = 1e-6
HG_DK = 128
HG_CHUNK = 64
HG_SUB = 16
HG_HP = 4
ATT_PATTERNS = ((128, 1), (512, 4), (2048, 16))
ATT_GROUPS = 3
ATT_HEADS = 4
ATT_DH = 128
ATT_BLK = 128
ROPE_THETA = 10000.0
ADAM_LR, ADAM_B1, ADAM_B2, ADAM_EPS, ADAM_WD, ADAM_STEP = 0.001, 0.9, 0.999, 1e-08, 0.01, 10
VMEM_LIMIT_BYTES = 56 * 1024 * 1024
MXU_COLS = 256
MESH = pl.DeviceIdType.MESH


def _cparams(sem, **kw):
    return pltpu.CompilerParams(dimension_semantics=sem, vmem_limit_bytes=VMEM_LIMIT_BYTES, **kw)


def _sigmoid(x):
    return 1.0 / (1.0 + jnp.exp(-x))


def _mm(a_list, b_list, pairs, n_acc, fin, out_dtypes, *, m, n, k, ta=False, tb=False, bm, bn, bk,
        b_off=None, extras=(), e_off=None, n_outer=False, consts=(), a_pro=None, n_sums=0, chunk=0, name):
    bm, bn, bk = min(bm, m), min(bn, n), min(bk, k)
    assert m % bm == 0 and n % bn == 0 and k % bk == 0, (name, m, n, k, bm, bn, bk)
    nk = k // bk
    assert not (a_pro and (nk > 1 or ta or n_outer)) and not (n_sums and (bn != n or n_outer)), name
    assert not (chunk and (nk > 1 or n_sums or chunk % 128)), name
    b_off = b_off or [(0, 0)] * len(b_list)
    e_off = e_off or [0] * len(extras)
    na, nb, ne, nc, no = len(a_list), len(b_list), len(extras), len(consts), len(out_dtypes)
    nao = na if a_pro else 0
    dn = (((0,) if ta else (1,), (1,) if tb else (0,)), ((), ()))

    def body(*refs):
        refs = list(refs)
        a_refs, b_refs, e_refs, c_refs, o_refs, ao_refs, s_refs = (
            [refs.pop(0) for _ in range(cnt)] for cnt in (na, nb, ne, nc, no, nao, n_sums))
        acc_refs = refs
        kk = pl.program_id(2)
        first = pl.program_id(0) == 0
        cvals = [c[...] for c in c_refs]
        a_vals = [r[...] for r in a_refs]
        if a_pro:
            a_vals = a_pro(a_vals, cvals)
            for r, v in zip(ao_refs, a_vals):
                r[...] = v
        if chunk:
            spans = [slice(lo, min(lo + chunk, bn)) for lo in range(0, bn, chunk)]
            chunks = []
            for cs in spans:
                parts = [None] * n_acc
                for ai, bi, ci in pairs:
                    p = lax.dot_general(a_vals[ai], b_refs[bi][cs, :] if tb else b_refs[bi][:, cs], dn,
                                        preferred_element_type=F32)
                    parts[ci] = p if parts[ci] is None else parts[ci] + p
                chunks.append(parts)
            for cs, parts in zip(spans, chunks):
                ex = [e[:, cs] for e in e_refs]
                outs = fin(parts, ex, cvals) if nc else fin(parts, ex)
                for o_ref, o in zip(o_refs, outs):
                    o_ref[:, cs] = o.astype(o_ref.dtype)
            return

        parts = [None] * n_acc
        for ai, bi, ci in pairs:
            p = lax.dot_general(a_vals[ai], b_refs[bi][...], dn, preferred_element_type=F32)
            parts[ci] = p if parts[ci] is None else parts[ci] + p

        def finish(accs):
            ex = [e[...] for e in e_refs]
            res = fin(accs, ex, cvals) if nc else fin(accs, ex)
            outs, sums = res if n_sums else (res, ())
            for o_ref, o in zip(o_refs, outs):
                o_ref[...] = o.astype(o_ref.dtype)
            if n_sums:
                @pl.when(first)
                def _():
                    for s_ref, s in zip(s_refs, sums):
                        s_ref[...] = s

                @pl.when(jnp.logical_not(first))
                def _():
                    for s_ref, s in zip(s_refs, sums):
                        s_ref[...] += s

        if nk == 1:
            finish(parts)
        else:
            @pl.when(kk == 0)
            def _():
                for c in range(n_acc):
                    acc_refs[c][...] = parts[c]

            @pl.when(kk > 0)
            def _():
                for c in range(n_acc):
                    acc_refs[c][...] += parts[c]

            @pl.when(kk == nk - 1)
            def _():
                finish([acc_refs[c][...] for c in range(n_acc)])

    def ij(f):
        return (lambda j, i, q: f(i, j, q)) if n_outer else f

    a_spec = pl.BlockSpec((bk, bm), ij(lambda i, j, q: (q, i))) if ta else pl.BlockSpec((bm, bk), ij(lambda i, j, q: (i, q)))

    def b_spec(off):
        on, ok = off
        if tb:
            return pl.BlockSpec((bn, bk), ij(lambda i, j, q: (j + on, q + ok)))
        return pl.BlockSpec((bk, bn), ij(lambda i, j, q: (q + ok, j + on)))

    mn_spec = pl.BlockSpec((bm, bn), ij(lambda i, j, q: (i, j)))
    outs = pl.pallas_call(
        body,
        out_shape=[jax.ShapeDtypeStruct((m, n), d) for d in out_dtypes] + [jax.ShapeDtypeStruct((m, k), BF16)] * nao
        + [jax.ShapeDtypeStruct((8, n), F32)] * n_sums,
        grid=(n // bn, m // bm, nk) if n_outer else (m // bm, n // bn, nk),
        in_specs=[a_spec] * na + [b_spec(o) for o in b_off]
        + [pl.BlockSpec((bm, bn), ij(lambda i, j, q, o=o: (i, j + o))) for o in e_off]
        + [pl.BlockSpec(c.shape, lambda *_, nd=c.ndim: (0,) * nd) for c in consts],
        out_specs=[mn_spec] * no + [a_spec] * nao + [pl.BlockSpec((8, n), lambda *_: (0, 0))] * n_sums,
        scratch_shapes=[pltpu.VMEM((bm, bn), F32) for _ in range(n_acc if nk > 1 else 0)],
        compiler_params=_cparams(("arbitrary" if n_sums else "parallel", "parallel", "arbitrary")),
        name=name,
    )(*a_list, *b_list, *extras, *consts)
    return outs


def _first(accs, ex):
    return (accs[0],)


def _rowwise(fn, ins, consts, out_defs, sum_widths, *, bm, name):
    ins = [tuple(e) + (1,) * (4 - len(e)) for e in ins]
    out_defs = [tuple(e) + (1,) * (3 - len(e)) for e in out_defs]
    t = ins[0][0].shape[-2] * ins[0][3]
    bm = min(bm, t)
    assert t % bm == 0, (name, t, bm)
    ni, nc, no, ns = len(ins), len(consts), len(out_defs), len(sum_widths)
    strided = [w for _, w, _, d in ins if d > 1] + [w for w, _, d in out_defs if d > 1]

    def body(*refs):
        i_refs, c_refs = refs[:ni], refs[ni:ni + nc]
        o_refs, s_refs = refs[ni + nc:ni + nc + no], refs[ni + nc + no:ni + nc + no + ns]
        scratch = list(refs[ni + nc + no + ns:])
        vals = []
        for ref, (_, w, _, d) in zip(i_refs, ins):
            if d == 1:
                vals.append(ref[...])
                continue
            s = scratch.pop(0)
            for r in range(d):
                for c in range(w // 128):
                    s.at[c][pl.ds(r, bm // d, stride=d), :] = ref[r, :, c * 128:(c + 1) * 128].astype(F32)
            vals.append(jnp.concatenate([s[c] for c in range(w // 128)], axis=1))
        outs, sums = fn(vals, [r[...] for r in c_refs])
        for o_ref, o, (w, _, d) in zip(o_refs, outs, out_defs):
            if d == 1:
                o_ref[...] = o.astype(o_ref.dtype)
                continue
            s = scratch.pop(0)
            for c in range(w // 128):
                s[c] = o[:, c * 128:(c + 1) * 128].astype(F32)
            for r in range(d):
                for c in range(w // 128):
                    o_ref[r, :, c * 128:(c + 1) * 128] = s.at[c][pl.ds(r, bm // d, stride=d), :].astype(o_ref.dtype)
        if ns:
            first = pl.program_id(0) == 0

            @pl.when(first)
            def _():
                for s_ref, s in zip(s_refs, sums):
                    s_ref[...] = s

            @pl.when(jnp.logical_not(first))
            def _():
                for s_ref, s in zip(s_refs, sums):
                    s_ref[...] += s

    def win(width, cb, d):
        if d > 1:
            return pl.BlockSpec((d, bm // d, width), lambda i: (0, i, 0))
        return pl.BlockSpec((bm, width), lambda i: (i, cb))

    res = pl.pallas_call(
        body,
        out_shape=[jax.ShapeDtypeStruct((t, w) if d == 1 else (d, t // d, w), dt) for w, dt, d in out_defs]
        + [jax.ShapeDtypeStruct((8, w), F32) for w in sum_widths],
        grid=(t // bm,),
        in_specs=[win(w, cb, d) for _, w, cb, d in ins] + [pl.BlockSpec(c.shape, lambda i, nd=c.ndim: (0,) * nd) for c in consts],
        out_specs=[win(w, 0, d) for w, _, d in out_defs] + [pl.BlockSpec((8, w), lambda i: (0, 0)) for w in sum_widths],
        scratch_shapes=[pltpu.VMEM((w // 128, bm, 128), F32) for w in strided],
        compiler_params=_cparams(("arbitrary",) if ns else ("parallel",)),
        name=name,
    )(*[e[0] for e in ins], *consts)
    return res[:no], [jnp.sum(s, axis=0) for s in res[no:]]


def _colsum8(x):
    bm, w = x.shape
    return jnp.sum(x.reshape(bm // 8, 8, w), axis=0)


def _tri(n, upper=False):
    r = lax.broadcasted_iota(jnp.int32, (n, n), 0)
    c = lax.broadcasted_iota(jnp.int32, (n, n), 1)
    return (c >= r) if upper else (c <= r)


def _exact_tri_matmul(tri_bf16, x):
    x0 = x.astype(BF16)
    r1 = x - x0.astype(F32)
    x1 = r1.astype(BF16)
    x2 = (r1 - x1.astype(F32)).astype(BF16)
    w = x.shape[1]
    y = jnp.dot(tri_bf16, jnp.concatenate([x0, x1, x2], axis=1), preferred_element_type=F32)
    return y[:, :w] + y[:, w:2 * w] + y[:, 2 * w:]


def _dot_nt(a, b):
    return lax.dot_general(a, b, (((1,), (1,)), ((), ())), preferred_element_type=F32)


def _dot_tn(a, b):
    return lax.dot_general(a, b, (((0,), (0,)), ((), ())), preferred_element_type=F32)


def _dot(a, b):
    return jnp.dot(a, b, preferred_element_type=F32)


def _hg_gates(hq, hf, lb):
    sq = _sigmoid(hq)
    q = hq * sq
    sg = _sigmoid(hf)
    f = lb + (1.0 - lb) * sg
    return q, sq, sg, f


def _hg_intra(q, kk, g):
    c = q.shape[0]
    rows = lax.broadcasted_iota(jnp.int32, (c, 1), 0)
    a_rows, qts, kts, eqs, eks = [], [], [], [], []
    for i in range(c // HG_SUB):
        lo = i * HG_SUB
        ref = g[lo - 1:lo, :] if i else jnp.zeros_like(g[0:1, :])
        eq = jnp.exp(g[lo:lo + HG_SUB, :] - ref)
        ek = jnp.exp(jnp.where(rows < lo + HG_SUB, ref - g, 0.0))
        qt = q[lo:lo + HG_SUB, :] * eq
        kt = kk * ek
        a = _dot_nt(qt.astype(BF16), kt.astype(BF16))
        tpos = lo + lax.broadcasted_iota(jnp.int32, (HG_SUB, c), 0)
        spos = lax.broadcasted_iota(jnp.int32, (HG_SUB, c), 1)
        a_rows.append(jnp.where(spos <= tpos, a, 0.0))
        qts.append(qt), kts.append(kt), eqs.append(eq), eks.append(ek)
    return jnp.concatenate(a_rows, axis=0), qts, kts, eqs, eks


def _hgrn_fwd_serial(zh, lb3, *, tb=512):
    t = zh.shape[0]
    nh = lb3.shape[0]
    c = HG_CHUNK
    tb = min(tb, t)
    nchunk = tb // c
    hp = HG_HP if nh % HG_HP == 0 else 1

    def body(hq_ref, hf_ref, hi_ref, lb_ref, o_ref, st_ref, state):
        @pl.when(pl.program_id(1) == 0)
        def _():
            state[...] = jnp.zeros_like(state)

        tril = _tri(c).astype(BF16)

        def one_head(hh, ci, sl):
            ls = slice(hh * HG_DK, (hh + 1) * HG_DK)
            q, _, _, f = _hg_gates(hq_ref[sl, ls], hf_ref[sl, ls], lb_ref[hh])
            v = hi_ref[sl, ls]
            kk = 1.0 - f
            g = _exact_tri_matmul(tril, jnp.log(f))
            a, _, _, _, _ = _hg_intra(q, kk, g)
            st = state[hh]
            st_ref[hh, ci] = st
            vb = v.astype(BF16)
            o = _dot(a.astype(BF16), vb) + _dot_nt((q * jnp.exp(g)).astype(BF16), st.astype(BF16))
            o_ref[sl, ls] = o
            glast = g[c - 1:c, :]
            kg = kk * jnp.exp(glast - g)
            state[hh] = st * jnp.exp(glast) + _dot_tn(vb, kg.astype(BF16))

        def chunk(ci, carry):
            sl = pl.ds(pl.multiple_of(ci * c, c), c)
            for hh in range(hp):
                one_head(hh, ci, sl)
            return carry

        lax.fori_loop(0, nchunk, chunk, 0)

    def col(cb):
        return pl.BlockSpec((tb, hp * HG_DK), lambda h, i: (i, cb * (nh // hp) + h))

    return pl.pallas_call(
        body,
        out_shape=[jax.ShapeDtypeStruct((t, nh * HG_DK), F32), jax.ShapeDtypeStruct((nh, t // c, HG_DK, HG_DK), F32)],
        grid=(nh // hp, t // tb),
        in_specs=[col(0), col(1), col(2), pl.BlockSpec((hp, 1, HG_DK), lambda h, i: (h, 0, 0))],
        out_specs=[pl.BlockSpec((tb, hp * HG_DK), lambda h, i: (i, h)),
                   pl.BlockSpec((hp, nchunk, HG_DK, HG_DK), lambda h, i: (h, i, 0, 0))],
        scratch_shapes=[pltpu.VMEM((hp, HG_DK, HG_DK), F32)],
        compiler_params=_cparams(("parallel", "arbitrary")),
        name="hgrn_fwd",
    )(zh, zh, zh, lb3)


def _hgrn_bwd_serial(zh, lb3, states, d_o, *, tb=512):
    t = zh.shape[0]
    nh = lb3.shape[0]
    c = HG_CHUNK
    tb = min(tb, t)
    nchunk = tb // c
    nblk = t // tb
    hp = HG_HP if nh % HG_HP == 0 else 1

    def body(hq_ref, hf_ref, hi_ref, lb_ref, st_ref, do_ref, dq_ref, df_ref, dv_ref, dlb_ref, dstate):
        @pl.when(pl.program_id(1) == 0)
        def _():
            dstate[...] = jnp.zeros_like(dstate)
            dlb_ref[...] = jnp.zeros_like(dlb_ref)

        tril = _tri(c).astype(BF16)
        triu = _tri(c, upper=True).astype(BF16)
        last_row = lax.broadcasted_iota(jnp.int32, (c, 1), 0) == c - 1

        def one_head(hh, ci, sl):
            ls = slice(hh * HG_DK, (hh + 1) * HG_DK)
            lb = lb_ref[hh]
            hq, hf = hq_ref[sl, ls], hf_ref[sl, ls]
            q, sq, sg, f = _hg_gates(hq, hf, lb)
            v = hi_ref[sl, ls]
            kk = 1.0 - f
            g = _exact_tri_matmul(tril, jnp.log(f))
            a, qts, kts, eqs, eks = _hg_intra(q, kk, g)
            st = st_ref[hh, ci]
            dst = dstate[hh]
            do = do_ref[sl, ls]
            dob, vb = do.astype(BF16), v.astype(BF16)
            glast = g[c - 1:c, :]
            eg = jnp.exp(g)
            egl = jnp.exp(glast - g)
            qg = q * eg
            kg = kk * egl
            dv = _dot_tn(a.astype(BF16), dob) + _dot_nt(kg.astype(BF16), dst.astype(BF16))
            da = jnp.where(_tri(c), _dot_nt(dob, vb), 0.0).astype(BF16)
            dq_parts, dgq_parts = [], []
            dk = jnp.zeros_like(kk)
            dgk = jnp.zeros_like(kk)
            for i in range(c // HG_SUB):
                da_i = da[i * HG_SUB:(i + 1) * HG_SUB, :]
                ktb, qtb = kts[i].astype(BF16), qts[i].astype(BF16)
                xi = _dot(da_i, ktb)
                yi = _dot_tn(da_i, qtb)
                dq_parts.append(xi * eqs[i])
                dk = dk + yi * eks[i]
                dgq_parts.append(xi * qtb.astype(F32))
                dgk = dgk + yi * ktb.astype(F32)
            dq_inter = _dot(dob, st.astype(BF16)) * eg
            dq = jnp.concatenate(dq_parts, axis=0) + dq_inter
            dk_state = _dot(vb, dst.astype(BF16)) * egl
            dk = dk + dk_state
            dg = jnp.concatenate(dgq_parts, axis=0) - dgk + q * dq_inter - kk * dk_state
            dgl = jnp.sum(kk * dk_state, axis=0, keepdims=True) + jnp.exp(glast) * jnp.sum(st * dst, axis=0, keepdims=True)
            dg = dg + jnp.where(last_row, dgl, 0.0)
            dlogf = _exact_tri_matmul(triu, dg)
            dfv = dlogf / f - dk
            dq_ref[sl, ls] = (dq * (sq * (1.0 + hq * (1.0 - sq)))).astype(dq_ref.dtype)
            df_ref[sl, ls] = (dfv * (1.0 - lb) * sg * (1.0 - sg)).astype(df_ref.dtype)
            dv_ref[sl, ls] = dv.astype(dv_ref.dtype)
            dlb_ref[hh] += jnp.sum(dfv * (1.0 - sg), axis=0, keepdims=True)
            dstate[hh] = dst * jnp.exp(glast) + _dot_tn(dob, qg.astype(BF16))

        def chunk(j, carry):
            ci = nchunk - 1 - j
            sl = pl.ds(pl.multiple_of(ci * c, c), c)
            for hh in range(hp):
                one_head(hh, ci, sl)
            return carry

        lax.fori_loop(0, nchunk, chunk, 0)

    def col(cb):
        return pl.BlockSpec((tb, hp * HG_DK), lambda h, i: (nblk - 1 - i, cb * (nh // hp) + h))

    ocol = pl.BlockSpec((tb, hp * HG_DK), lambda h, i: (nblk - 1 - i, h))
    w = nh * HG_DK
    dq, df, dv, dlb = pl.pallas_call(
        body,
        out_shape=[jax.ShapeDtypeStruct((t, w), BF16)] * 3 + [jax.ShapeDtypeStruct((nh, 1, HG_DK), F32)],
        grid=(nh // hp, nblk),
        in_specs=[col(0), col(1), col(2), pl.BlockSpec((hp, 1, HG_DK), lambda h, i: (h, 0, 0)),
                  pl.BlockSpec((hp, nchunk, HG_DK, HG_DK), lambda h, i: (h, nblk - 1 - i, 0, 0)), ocol],
        out_specs=[ocol, ocol, ocol, pl.BlockSpec((hp, 1, HG_DK), lambda h, i: (h, 0, 0))],
        scratch_shapes=[pltpu.VMEM((hp, HG_DK, HG_DK), F32)],
        compiler_params=_cparams(("parallel", "arbitrary")),
        name="hgrn_bwd",
    )(zh, zh, zh, lb3, states, d_o)
    return dq, df, dv, dlb.reshape(w)


def _hg_heads(x, hp):
    return [x[:, h * HG_DK:(h + 1) * HG_DK] for h in range(hp)]


def _hg_intra_wide(q, kk, g, hp):
    c = q.shape[0]
    rows = lax.broadcasted_iota(jnp.int32, (c, 1), 0)
    a_rows = [[] for _ in range(hp)]
    qts, kts, eqs, eks = [], [], [], []
    for i in range(c // HG_SUB):
        lo = i * HG_SUB
        ref = g[lo - 1:lo, :] if i else jnp.zeros_like(g[0:1, :])
        eq = jnp.exp(g[lo:lo + HG_SUB, :] - ref)
        ek = jnp.exp(jnp.where(rows < lo + HG_SUB, ref - g, 0.0))
        qtb = (q[lo:lo + HG_SUB, :] * eq).astype(BF16)
        ktb = (kk * ek).astype(BF16)
        tpos = lo + lax.broadcasted_iota(jnp.int32, (HG_SUB, c), 0)
        spos = lax.broadcasted_iota(jnp.int32, (HG_SUB, c), 1)
        for h, (qh, kh) in enumerate(zip(_hg_heads(qtb, hp), _hg_heads(ktb, hp))):
            a_rows[h].append(jnp.where(spos <= tpos, _dot_nt(qh, kh), 0.0))
        qts.append(qtb), kts.append(ktb), eqs.append(eq), eks.append(ek)
    return [jnp.concatenate(r, axis=0) for r in a_rows], qts, kts, eqs, eks


def _hgrn_fwd(zh, lb3, *, tb=512):
    t = zh.shape[0]
    nh = lb3.shape[0]
    c = HG_CHUNK
    tb = min(tb, t)
    nchunk = tb // c
    hp = HG_HP if nh % HG_HP == 0 else 1
    wp = hp * HG_DK

    def body(hq_ref, hf_ref, hi_ref, lb_ref, o_ref, st_ref, state):
        @pl.when(pl.program_id(1) == 0)
        def _():
            state[...] = jnp.zeros_like(state)

        tril = _tri(c).astype(BF16)

        def chunk(ci, carry):
            sl = pl.ds(pl.multiple_of(ci * c, c), c)
            q, _, _, f = _hg_gates(hq_ref[sl, :], hf_ref[sl, :], lb_ref[...])
            kk = 1.0 - f
            g = _exact_tri_matmul(tril, jnp.log(f))
            a, _, _, _, _ = _hg_intra_wide(q, kk, g, hp)
            vb = hi_ref[sl, :].astype(BF16)
            glast = g[c - 1:c, :]
            qgb = (q * jnp.exp(g)).astype(BF16)
            kgb = (kk * jnp.exp(glast - g)).astype(BF16)
            dec = jnp.exp(glast)
            sts = [state[h] for h in range(hp)]
            for h in range(hp):
                st_ref[h, ci] = sts[h]
            vh, qgh, kgh, dech = _hg_heads(vb, hp), _hg_heads(qgb, hp), _hg_heads(kgb, hp), _hg_heads(dec, hp)
            o = [_dot(a[h].astype(BF16), vh[h]) + _dot_nt(qgh[h], sts[h].astype(BF16)) for h in range(hp)]
            new = [_dot_tn(vh[h], kgh[h]) for h in range(hp)]
            o_ref[sl, :] = jnp.concatenate(o, axis=1)
            for h in range(hp):
                state[h] = sts[h] * dech[h] + new[h]
            return carry

        lax.fori_loop(0, nchunk, chunk, 0)

    def col(cb):
        return pl.BlockSpec((tb, wp), lambda h, i: (i, cb * (nh // hp) + h))

    return pl.pallas_call(
        body,
        out_shape=[jax.ShapeDtypeStruct((t, nh * HG_DK), F32), jax.ShapeDtypeStruct((nh, t // c, HG_DK, HG_DK), F32)],
        grid=(nh // hp, t // tb),
        in_specs=[col(0), col(1), col(2), pl.BlockSpec((1, wp), lambda h, i: (0, h))],
        out_specs=[pl.BlockSpec((tb, wp), lambda h, i: (i, h)),
                   pl.BlockSpec((hp, nchunk, HG_DK, HG_DK), lambda h, i: (h, i, 0, 0))],
        scratch_shapes=[pltpu.VMEM((hp, HG_DK, HG_DK), F32)],
        compiler_params=_cparams(("parallel", "arbitrary")),
        name="hgrn_fwd",
    )(zh, zh, zh, lb3.reshape(1, -1))


def _hgrn_bwd(zh, lb3, states, d_o, *, tb=512):
    t = zh.shape[0]
    nh = lb3.shape[0]
    c = HG_CHUNK
    tb = min(tb, t)
    nchunk = tb // c
    nblk = t // tb
    hp = HG_HP if nh % HG_HP == 0 else 1
    wp = hp * HG_DK

    def body(hq_ref, hf_ref, hi_ref, lb_ref, st_ref, do_ref, dq_ref, df_ref, dv_ref, dlb_ref, dstate):
        @pl.when(pl.program_id(1) == 0)
        def _():
            dstate[...] = jnp.zeros_like(dstate)
            dlb_ref[...] = jnp.zeros_like(dlb_ref)

        tril = _tri(c).astype(BF16)
        triu = _tri(c, upper=True).astype(BF16)
        last_row = lax.broadcasted_iota(jnp.int32, (c, 1), 0) == c - 1
        heads = range(hp)

        def chunk(j, carry):
            ci = nchunk - 1 - j
            sl = pl.ds(pl.multiple_of(ci * c, c), c)
            lb = lb_ref[...]
            hq, hf = hq_ref[sl, :], hf_ref[sl, :]
            q, sq, sg, f = _hg_gates(hq, hf, lb)
            kk = 1.0 - f
            g = _exact_tri_matmul(tril, jnp.log(f))
            a, qts, kts, eqs, eks = _hg_intra_wide(q, kk, g, hp)
            glast = g[c - 1:c, :]
            eg, egl, dec = jnp.exp(g), jnp.exp(glast - g), jnp.exp(glast)
            vb, dob = hi_ref[sl, :].astype(BF16), do_ref[sl, :].astype(BF16)
            qgb, kgb = (q * eg).astype(BF16), (kk * egl).astype(BF16)
            sts = [st_ref[h, ci] for h in heads]
            dsts = [dstate[h] for h in heads]
            stb, dstb = [s.astype(BF16) for s in sts], [s.astype(BF16) for s in dsts]
            vh, doh, qgh, kgh = _hg_heads(vb, hp), _hg_heads(dob, hp), _hg_heads(qgb, hp), _hg_heads(kgb, hp)
            dv = [_dot_tn(a[h].astype(BF16), doh[h]) + _dot_nt(kgh[h], dstb[h]) for h in heads]
            da = [jnp.where(_tri(c), _dot_nt(doh[h], vh[h]), 0.0).astype(BF16) for h in heads]
            dq_inter = jnp.concatenate([_dot(doh[h], stb[h]) for h in heads], axis=1) * eg
            dk_state = jnp.concatenate([_dot(vh[h], dstb[h]) for h in heads], axis=1) * egl
            new_dst = [_dot_tn(doh[h], qgh[h]) for h in heads]
            xs, dk, dgk = [], dk_state, 0.0
            for i in range(c // HG_SUB):
                rs = slice(i * HG_SUB, (i + 1) * HG_SUB)
                kth, qth = _hg_heads(kts[i], hp), _hg_heads(qts[i], hp)
                xi = jnp.concatenate([_dot(da[h][rs, :], kth[h]) for h in heads], axis=1)
                yi = jnp.concatenate([_dot_tn(da[h][rs, :], qth[h]) for h in heads], axis=1)
                xs.append(xi)
                dk = dk + yi * eks[i]
                dgk = dgk + yi * kts[i].astype(F32)
            dq = jnp.concatenate([x * e for x, e in zip(xs, eqs)], axis=0) + dq_inter
            dgq = jnp.concatenate([x * qt.astype(F32) for x, qt in zip(xs, qts)], axis=0)
            dg = dgq - dgk + q * dq_inter - kk * dk_state
            sdot = jnp.concatenate([jnp.sum(sts[h] * dsts[h], axis=0, keepdims=True) for h in heads], axis=1)
            dgl = jnp.sum(kk * dk_state, axis=0, keepdims=True) + dec * sdot
            dg = dg + jnp.where(last_row, dgl, 0.0)
            dlogf = _exact_tri_matmul(triu, dg)
            dfv = dlogf / f - dk
            dq_ref[sl, :] = (dq * (sq * (1.0 + hq * (1.0 - sq)))).astype(dq_ref.dtype)
            df_ref[sl, :] = (dfv * (1.0 - lb) * sg * (1.0 - sg)).astype(df_ref.dtype)
            dv_ref[sl, :] = jnp.concatenate(dv, axis=1).astype(dv_ref.dtype)
            dlb_ref[...] += jnp.sum(dfv * (1.0 - sg), axis=0, keepdims=True)
            dech = _hg_heads(dec, hp)
            for h in heads:
                dstate[h] = dsts[h] * dech[h] + new_dst[h]
            return carry

        lax.fori_loop(0, nchunk, chunk, 0)

    def col(cb):
        return pl.BlockSpec((tb, wp), lambda h, i: (nblk - 1 - i, cb * (nh // hp) + h))

    ocol = pl.BlockSpec((tb, wp), lambda h, i: (nblk - 1 - i, h))
    lbspec = pl.BlockSpec((1, wp), lambda h, i: (0, h))
    w = nh * HG_DK
    dq, df, dv, dlb = pl.pallas_call(
        body,
        out_shape=[jax.ShapeDtypeStruct((t, w), BF16)] * 3 + [jax.ShapeDtypeStruct((1, w), F32)],
        grid=(nh // hp, nblk),
        in_specs=[col(0), col(1), col(2), lbspec,
                  pl.BlockSpec((hp, nchunk, HG_DK, HG_DK), lambda h, i: (h, nblk - 1 - i, 0, 0)), ocol],
        out_specs=[ocol, ocol, ocol, lbspec],
        scratch_shapes=[pltpu.VMEM((hp, HG_DK, HG_DK), F32)],
        compiler_params=_cparams(("parallel", "arbitrary")),
        name="hgrn_bwd",
    )(zh, zh, zh, lb3.reshape(1, -1), states, d_o)
    return dq, df, dv, dlb.reshape(w)


NEG = -1e30
ATT_GW = ATT_HEADS * ATT_DH


def _att_scores(q, kp, kc, has_prev):
    scale = ATT_DH ** -0.5
    i = lax.broadcasted_iota(jnp.int32, (ATT_BLK, ATT_BLK), 0)
    j = lax.broadcasted_iota(jnp.int32, (ATT_BLK, ATT_BLK), 1)
    s_p = jnp.where(jnp.logical_and(j >= i, has_prev), _dot_nt(q, kp) * scale, NEG)
    s_c = jnp.where(j <= i, _dot_nt(q, kc) * scale, NEG)
    return s_p, s_c


def _att_views(arrs, d):
    return [a.reshape(d, -1, ATT_GW) for a in arrs]


def _att_unview(a, d):
    return a.reshape(-1, ATT_GW) if d == 1 else a


ATT_QB = 4


def _attn_fwd(qb, kb, vb, g):
    d = ATT_PATTERNS[g][1]
    q2, k2, v2 = _att_views([qb, kb, vb], d)
    nblk = q2.shape[1] // ATT_BLK
    nq = ATT_QB if nblk % ATT_QB == 0 else 1
    rows = nq * ATT_BLK

    def body(q_ref, kc_ref, kp_ref, vc_ref, vp_ref, o_ref, l_ref):
        first = pl.program_id(1) == 0
        hss = [slice(h * ATT_DH, (h + 1) * ATT_DH) for h in range(ATT_HEADS)]
        for b in range(nq):
            rs = slice(b * ATT_BLK, (b + 1) * ATT_BLK)
            ps = slice((b - 1) * ATT_BLK, b * ATT_BLK)
            has_prev = jnp.logical_not(first) if b == 0 else True
            kv = [(kp_ref[:, hs], vp_ref[:, hs]) if b == 0 else (kc_ref[ps, hs], vc_ref[ps, hs]) for hs in hss]
            sc = [_att_scores(q_ref[rs, hs], kv[h][0], kc_ref[rs, hs], has_prev) for h, hs in enumerate(hss)]
            ms = [jnp.maximum(jnp.max(s_p, axis=1, keepdims=True), jnp.max(s_c, axis=1, keepdims=True)) for s_p, s_c in sc]
            ps_ = [(jnp.exp(s_p - m), jnp.exp(s_c - m)) for (s_p, s_c), m in zip(sc, ms)]
            ls = [jnp.sum(p_p, axis=1, keepdims=True) + jnp.sum(p_c, axis=1, keepdims=True) for p_p, p_c in ps_]
            os_ = [_dot(p_p.astype(BF16), kv[h][1]) + _dot(p_c.astype(BF16), vc_ref[rs, hss[h]]) for h, (p_p, p_c) in enumerate(ps_)]
            for h, hs in enumerate(hss):
                o_ref[rs, hs] = os_[h] / ls[h]
                l_ref[rs, hs] = jnp.broadcast_to(ms[h] + jnp.log(ls[h]), (ATT_BLK, ATT_DH))

    cur = pl.BlockSpec((None, rows, ATT_GW), lambda r, n: (r, n, 0))
    prev = pl.BlockSpec((None, ATT_BLK, ATT_GW), lambda r, n: (r, jnp.maximum(n * nq - 1, 0), 0))
    o, lse = pl.pallas_call(
        body,
        out_shape=[jax.ShapeDtypeStruct(q2.shape, F32)] * 2,
        grid=(d, nblk // nq),
        in_specs=[cur, cur, prev, cur, prev],
        out_specs=[cur, cur],
        compiler_params=_cparams(("parallel", "arbitrary")),
        name=f"attn_fwd_g{g}",
    )(q2, k2, k2, v2, v2)
    return _att_unview(o, d), _att_unview(lse, d)


def _attn_bwd(qb, kb, vb, o, lse, d_o, d_lse, g):
    d = ATT_PATTERNS[g][1]
    q2, k2, v2 = _att_views([qb, kb, vb], d)
    o2, l2, do2, dl2 = _att_views([o, lse, d_o, d_lse], d)
    nblk = q2.shape[1] // ATT_BLK
    nq = ATT_QB if nblk % ATT_QB == 0 else 1
    rows = nq * ATT_BLK
    ns = nblk // nq
    scale = ATT_DH ** -0.5

    def body(q_ref, kc_ref, kp_ref, vc_ref, vp_ref, o_ref, l_ref, do_ref, dl_ref, dq_ref, dk_ref, dv_ref, ck, cv):
        n = pl.program_id(1)

        @pl.when(n == 0)
        def _():
            ck[...] = jnp.zeros_like(ck)
            cv[...] = jnp.zeros_like(cv)

        first = n == ns - 1
        hss = [slice(h * ATT_DH, (h + 1) * ATT_DH) for h in range(ATT_HEADS)]
        heads = range(ATT_HEADS)
        pend_k, pend_v = [ck[:, hs] for hs in hss], [cv[:, hs] for hs in hss]
        for b in reversed(range(nq)):
            rs = slice(b * ATT_BLK, (b + 1) * ATT_BLK)
            ps = slice((b - 1) * ATT_BLK, b * ATT_BLK)
            has_prev = jnp.logical_not(first) if b == 0 else True
            q = [q_ref[rs, hs] for hs in hss]
            kc, vc = [kc_ref[rs, hs] for hs in hss], [vc_ref[rs, hs] for hs in hss]
            kp = [kp_ref[:, hs] if b == 0 else kc_ref[ps, hs] for hs in hss]
            vp = [vp_ref[:, hs] if b == 0 else vc_ref[ps, hs] for hs in hss]
            sc = [_att_scores(q[h], kp[h], kc[h], has_prev) for h in heads]
            dob = [do_ref[rs, hs].astype(BF16) for hs in hss]
            dp = [(_dot_nt(dob[h], vp[h]), _dot_nt(dob[h], vc[h])) for h in heads]
            delta = [jnp.sum(do_ref[rs, hs] * o_ref[rs, hs] - dl_ref[rs, hs], axis=1, keepdims=True) for hs in hss]
            pr = [(jnp.exp(sc[h][0] - l_ref[rs, hss[h]][:, 0:1]), jnp.exp(sc[h][1] - l_ref[rs, hss[h]][:, 0:1])) for h in heads]
            ds = [((pr[h][0] * (dp[h][0] - delta[h]) * scale).astype(BF16), (pr[h][1] * (dp[h][1] - delta[h]) * scale).astype(BF16))
                  for h in heads]
            pb = [(pr[h][0].astype(BF16), pr[h][1].astype(BF16)) for h in heads]
            dq = [_dot(ds[h][0], kp[h]) + _dot(ds[h][1], kc[h]) for h in heads]
            dk_c = [_dot_tn(ds[h][1], q[h]) for h in heads]
            dv_c = [_dot_tn(pb[h][1], dob[h]) for h in heads]
            dk_p = [_dot_tn(ds[h][0], q[h]) for h in heads]
            dv_p = [_dot_tn(pb[h][0], dob[h]) for h in heads]
            for h, hs in enumerate(hss):
                dq_ref[rs, hs] = dq[h]
                dk_ref[rs, hs] = pend_k[h] + dk_c[h]
                dv_ref[rs, hs] = pend_v[h] + dv_c[h]
            pend_k, pend_v = dk_p, dv_p
        for h, hs in enumerate(hss):
            ck[:, hs] = pend_k[h]
            cv[:, hs] = pend_v[h]

    cur = pl.BlockSpec((None, rows, ATT_GW), lambda r, n: (r, ns - 1 - n, 0))
    prev = pl.BlockSpec((None, ATT_BLK, ATT_GW), lambda r, n: (r, jnp.maximum((ns - 1 - n) * nq - 1, 0), 0))
    shp = jax.ShapeDtypeStruct(q2.shape, F32)
    dq, dk, dv = pl.pallas_call(
        body,
        out_shape=[shp, shp, shp],
        grid=(d, ns),
        in_specs=[cur, cur, prev, cur, prev, cur, cur, cur, cur],
        out_specs=[cur, cur, cur],
        scratch_shapes=[pltpu.VMEM((ATT_BLK, ATT_GW), F32), pltpu.VMEM((ATT_BLK, ATT_GW), F32)],
        compiler_params=_cparams(("parallel", "arbitrary")),
        name=f"attn_bwd_g{g}",
    )(q2, k2, k2, v2, v2, o2, l2, do2, dl2)
    return _att_unview(dq, d), _att_unview(dk, d), _att_unview(dv, d)


def _attn_fwd_1blk(qb, kb, vb, g):
    d = ATT_PATTERNS[g][1]
    q2, k2, v2 = _att_views([qb, kb, vb], d)
    nb = q2.shape[1] // ATT_BLK

    def body(q_ref, kc_ref, kp_ref, vc_ref, vp_ref, o_ref, l_ref):
        has_prev = pl.program_id(1) > 0
        for h in range(ATT_HEADS):
            hs = slice(h * ATT_DH, (h + 1) * ATT_DH)
            s_p, s_c = _att_scores(q_ref[:, hs], kp_ref[:, hs], kc_ref[:, hs], has_prev)
            m = jnp.maximum(jnp.max(s_p, axis=1, keepdims=True), jnp.max(s_c, axis=1, keepdims=True))
            p_p, p_c = jnp.exp(s_p - m), jnp.exp(s_c - m)
            l = jnp.sum(p_p, axis=1, keepdims=True) + jnp.sum(p_c, axis=1, keepdims=True)
            o = _dot(p_p.astype(BF16), vp_ref[:, hs]) + _dot(p_c.astype(BF16), vc_ref[:, hs])
            o_ref[:, hs] = o / l
            l_ref[:, hs] = jnp.broadcast_to(m + jnp.log(l), (ATT_BLK, ATT_DH))

    cur = pl.BlockSpec((None, ATT_BLK, ATT_GW), lambda r, n: (r, n, 0))
    prev = pl.BlockSpec((None, ATT_BLK, ATT_GW), lambda r, n: (r, jnp.maximum(n - 1, 0), 0))
    o, lse = pl.pallas_call(
        body,
        out_shape=[jax.ShapeDtypeStruct(q2.shape, F32)] * 2,
        grid=(d, nb),
        in_specs=[cur, cur, prev, cur, prev],
        out_specs=[cur, cur],
        compiler_params=_cparams(("parallel", "arbitrary")),
        name=f"attn_fwd_g{g}",
    )(q2, k2, k2, v2, v2)
    return _att_unview(o, d), _att_unview(lse, d)


def _attn_bwd_1blk(qb, kb, vb, o, lse, d_o, d_lse, g):
    d = ATT_PATTERNS[g][1]
    q2, k2, v2 = _att_views([qb, kb, vb], d)
    o2, l2, do2, dl2 = _att_views([o, lse, d_o, d_lse], d)
    nb = q2.shape[1] // ATT_BLK

    def body(q_ref, kc_ref, kp_ref, vc_ref, vp_ref, o_ref, l_ref, do_ref, dl_ref, dq_ref, dk_ref, dv_ref, ck, cv):
        n = pl.program_id(1)
        active = n < nb

        @pl.when(n == 0)
        def _():
            ck[...] = jnp.zeros_like(ck)
            cv[...] = jnp.zeros_like(cv)

        @pl.when(jnp.logical_not(active))
        def _():
            dk_ref[...] = ck[...]
            dv_ref[...] = cv[...]

        @pl.when(active)
        def _():
            has_prev = n > 0
            for h in range(ATT_HEADS):
                hs = slice(h * ATT_DH, (h + 1) * ATT_DH)
                q, kp, kc, vp, vc = q_ref[:, hs], kp_ref[:, hs], kc_ref[:, hs], vp_ref[:, hs], vc_ref[:, hs]
                s_p, s_c = _att_scores(q, kp, kc, has_prev)
                lse_h = l_ref[:, hs][:, 0:1]
                p_p, p_c = jnp.exp(s_p - lse_h), jnp.exp(s_c - lse_h)
                do = do_ref[:, hs]
                delta = jnp.sum(do * o_ref[:, hs] - dl_ref[:, hs], axis=1, keepdims=True)
                dob = do.astype(BF16)
                scale = ATT_DH ** -0.5
                ds_p = (p_p * (_dot_nt(dob, vp) - delta) * scale).astype(BF16)
                ds_c = (p_c * (_dot_nt(dob, vc) - delta) * scale).astype(BF16)
                dq_ref[:, hs] = _dot(ds_p, kp) + _dot(ds_c, kc)
                dk_ref[:, hs] = ck[:, hs] + _dot_tn(ds_p, q)
                dv_ref[:, hs] = cv[:, hs] + _dot_tn(p_p.astype(BF16), dob)
                ck[:, hs] = _dot_tn(ds_c, q)
                cv[:, hs] = _dot_tn(p_c.astype(BF16), dob)

    def qn(n):
        return jnp.minimum(n, nb - 1)

    cur = pl.BlockSpec((None, ATT_BLK, ATT_GW), lambda r, n: (r, qn(n), 0))
    prev = pl.BlockSpec((None, ATT_BLK, ATT_GW), lambda r, n: (r, jnp.maximum(qn(n) - 1, 0), 0))
    behind = pl.BlockSpec((None, ATT_BLK, ATT_GW), lambda r, n: (r, jnp.maximum(n - 1, 0), 0))
    shp = jax.ShapeDtypeStruct(q2.shape, F32)
    dq, dk, dv = pl.pallas_call(
        body,
        out_shape=[shp, shp, shp],
        grid=(d, nb + 1),
        in_specs=[cur, cur, prev, cur, prev, cur, cur, cur, cur],
        out_specs=[cur, behind, behind],
        scratch_shapes=[pltpu.VMEM((ATT_BLK, ATT_GW), F32), pltpu.VMEM((ATT_BLK, ATT_GW), F32)],
        compiler_params=_cparams(("parallel", "arbitrary")),
        name=f"attn_bwd_g{g}",
    )(q2, k2, k2, v2, v2, o2, l2, do2, dl2)
    return _att_unview(dq, d), _att_unview(dk, d), _att_unview(dv, d)


def _rms_parts(x, width):
    outs = []
    for lo in range(0, x.shape[1], width):
        xs = x[:, lo:lo + width].astype(F32)
        r = lax.rsqrt(jnp.mean(xs * xs, axis=1, keepdims=True) + EPS)
        outs.append((xs * r, r))
    return outs


def _rms_bwd_part(xh, r, dxh):
    return r * (dxh - xh * jnp.mean(dxh * xh, axis=1, keepdims=True))


def _norm_pro(a, consts):
    (xh, _), = _rms_parts(a[0], a[0].shape[1])
    return [(xh * consts[0]).astype(BF16)]


def _norm_bwd_fin(accs, ex, consts):
    xv, dres = ex
    (xh, r), = _rms_parts(xv, xv.shape[1])
    return [dres + _rms_bwd_part(xh, r, accs[0] * consts[0])], [_colsum8(accs[0] * xh)]


def _norm_fwd(x, gain):
    d = x.shape[1]

    def fn(ins, consts):
        (xh, _), = _rms_parts(ins[0], d)
        return [xh * consts[0]], []

    (h,), _ = _rowwise(fn, [(x, d, 0)], [gain.reshape(1, d)], [(d, BF16)], [], bm=512, name="norm_fwd")
    return h


def _norm_bwd(x, gain, dh, dres):
    d = x.shape[1]

    def fn(ins, consts):
        (xh, r), = _rms_parts(ins[0], d)
        dx = ins[2] + _rms_bwd_part(xh, r, ins[1] * consts[0])
        return [dx], [_colsum8(ins[1] * xh)]

    (dx,), (dg,) = _rowwise(fn, [(x, d, 0), (dh, d, 0), (dres, d, 0)], [gain.reshape(1, d)], [(d, F32)], [d],
                            bm=512, name="norm_bwd")
    return dx, dg


def _rot_sign():
    lane = lax.broadcasted_iota(jnp.int32, (1, ATT_DH), 1)
    return jnp.where(lane < ATT_DH // 2, -1.0, 1.0).astype(F32)


def _rope(y, cos, sin):
    return y * cos + pltpu.roll(y, ATT_DH // 2, axis=1) * _rot_sign() * sin


def _rope_t(dy, cos, sin):
    return dy * cos - pltpu.roll(dy * sin, ATT_DH // 2, axis=1) * _rot_sign()


def _qk_prep(zq, zk, zv, qn, kn, cos, sin):
    w = zq.shape[1]

    def fn(ins, consts):
        cs, sn = ins[3], ins[4]
        outs = []
        for z, gain in ((ins[0], consts[0]), (ins[1], consts[1])):
            for i, (xh, _) in enumerate(_rms_parts(z, ATT_DH)):
                outs.append(_rope(xh * gain[:, i * ATT_DH:(i + 1) * ATT_DH], cs, sn))
        outs += [ins[2][:, i * ATT_DH:(i + 1) * ATT_DH] for i in range(w // ATT_DH)]
        groups = [jnp.concatenate(outs[i:i + ATT_HEADS], axis=1) for i in range(0, len(outs), ATT_HEADS)]
        return groups, []

    outs, _ = _rowwise(fn, [(zq, w, 0), (zk, w, 0), (zv, w, 0), (cos, ATT_DH, 0), (sin, ATT_DH, 0)], [qn, kn],
                       [(ATT_GW, BF16, ATT_PATTERNS[g][1]) for g in range(ATT_GROUPS)] * 3, [], bm=256, name="qk_prep")
    return outs[0:3], outs[3:6], outs[6:9]


def _qk_prep_bwd(zq, zk, dq_g, dk_g, dv_g, qn, kn, cos, sin):
    w = zq.shape[1]

    def fn(ins, consts):
        cs, sn = ins[2], ins[3]
        outs, sums = [], []
        for z, gain, dparts in ((ins[0], consts[0], ins[4:7]), (ins[1], consts[1], ins[7:10])):
            dout = jnp.concatenate(dparts, axis=1)
            dz, dgain = [], []
            for i, (xh, r) in enumerate(_rms_parts(z, ATT_DH)):
                hs = slice(i * ATT_DH, (i + 1) * ATT_DH)
                dy = _rope_t(dout[:, hs], cs, sn)
                dgain.append(_colsum8(dy * xh))
                dz.append(_rms_bwd_part(xh, r, dy * gain[:, hs]))
            outs.append(jnp.concatenate(dz, axis=1))
            sums.append(jnp.concatenate(dgain, axis=1))
        outs.append(jnp.concatenate(ins[10:13], axis=1))
        return outs, sums

    ins = [(zq, w, 0), (zk, w, 0), (cos, ATT_DH, 0), (sin, ATT_DH, 0)]
    for parts in (dq_g, dk_g, dv_g):
        ins += [(a, ATT_GW, 0, ATT_PATTERNS[g][1]) for g, a in enumerate(parts)]
    (dzq, dzk, dzv), (dqn, dkn) = _rowwise(fn, ins, [qn, kn], [(w, BF16)] * 3, [w, w], bm=256, name="qk_prep_bwd")
    return dzq, dzk, dzv, dqn, dkn


def _post_a(o_raw, zh, gout):
    w = o_raw.shape[1]

    def fn(ins, consts):
        oh = jnp.concatenate([xh for xh, _ in _rms_parts(ins[0], HG_DK)], axis=1)
        hg = ins[1]
        return [oh * consts[0] * (hg * _sigmoid(hg))], []

    (y,), _ = _rowwise(fn, [(o_raw, w, 0), (zh, w, 3)], [gout.reshape(1, w)], [(w, BF16)], [], bm=512, name="post_a")
    return y


def _post_a_bwd(o_raw, zh, gout, dy):
    w = o_raw.shape[1]

    def fn(ins, consts):
        parts = _rms_parts(ins[0], HG_DK)
        oh = jnp.concatenate([xh for xh, _ in parts], axis=1)
        hg, dyv, gain = ins[1], ins[2], consts[0]
        sg = _sigmoid(hg)
        s = hg * sg
        doh = dyv * gain * s
        do = jnp.concatenate([_rms_bwd_part(xh, r, doh[:, i * HG_DK:(i + 1) * HG_DK]) for i, (xh, r) in enumerate(parts)], axis=1)
        dhg = dyv * oh * gain * (sg * (1.0 + hg * (1.0 - sg)))
        return [do, dhg], [_colsum8(dyv * oh * s)]

    (do, dhg), (dgain,) = _rowwise(fn, [(o_raw, w, 0), (zh, w, 3), (dy, w, 0)], [gout.reshape(1, w)],
                                   [(w, F32), (w, BF16)], [w], bm=512, name="post_a_bwd")
    return do, dhg, dgain


def _merge_alpha(lses):
    m = jnp.maximum(jnp.maximum(lses[0], lses[1]), lses[2])
    e = [jnp.exp(l - m) for l in lses]
    inv = 1.0 / (e[0] + e[1] + e[2])
    return [x * inv for x in e]


def _group_ins(parts):
    return [(a, ATT_GW, 0, ATT_PATTERNS[g][1]) for g, a in enumerate(parts)]


def _merge_b(o_g, lse_g):
    def fn(ins, consts):
        al = _merge_alpha(ins[3:6])
        return [al[0] * ins[0] + al[1] * ins[1] + al[2] * ins[2]], []

    (y,), _ = _rowwise(fn, _group_ins(o_g) + _group_ins(lse_g), [], [(ATT_GW, BF16)], [], bm=512, name="merge_b")
    return y


def _merge_b_bwd(o_g, lse_g, dy):
    def fn(ins, consts):
        al = _merge_alpha(ins[3:6])
        dyv = ins[6]
        dal = [dyv * ins[i] for i in range(3)]
        tot = al[0] * dal[0] + al[1] * dal[1] + al[2] * dal[2]
        return [al[i] * dyv for i in range(3)] + [al[i] * (dal[i] - tot) for i in range(3)], []

    outs, _ = _rowwise(fn, _group_ins(o_g) + _group_ins(lse_g) + [(dy, ATT_GW, 0)], [],
                       [(ATT_GW, F32, ATT_PATTERNS[g][1]) for g in range(ATT_GROUPS)] * 2, [], bm=512, name="merge_b_bwd")
    return outs[:3], outs[3:]


def _loss_head(y, target):
    d = y.shape[1]

    def fn(ins, consts):
        e = ins[0] - ins[1]
        return [e * (1.0 / d)], [_colsum8(e * e)]

    (dy,), (sq,) = _rowwise(fn, [(y, d, 0), (target, d, 0)], [], [(d, F32)], [d], bm=512, name="loss_head")
    return 0.5 * jnp.sum(sq) / d, dy


def _silu_grad(a):
    s = _sigmoid(a)
    return s * (1.0 + a * (1.0 - s))


def _ffn_fwd(x, gain, wt, wo_fn, tag):
    t, d = x.shape
    f = wt.shape[0] // 2

    def act(accs, ex, consts):
        a, b = accs
        s = _sigmoid(a)
        sa = a * s
        return (sa * b, b, 0.5 * sa, 0.5 * (s + sa * (1.0 - s)))

    bn = FFN_BN if f % FFN_BN == 0 else 256
    u, b, sa, sp, h = _mm([x], [wt, wt], [(0, 0, 0), (0, 1, 1)], 2, act, [BF16] * 4, m=t, n=f, k=d, tb=True,
                          bm=512, bn=bn, bk=d, b_off=[(0, 0), (f // min(bn, f), 0)],
                          consts=[gain.reshape(1, d)], a_pro=_norm_pro, chunk=MXU_COLS, name=f"ffn_in_{tag}")
    wo = wo_fn(u)
    (y,) = _mm([u], [wo], [(0, 0, 0)], 1, lambda accs, ex: (ex[0] + 0.5 * accs[0],), [F32], m=t, n=d, k=f,
               bm=512, bn=d, bk=f, extras=[x], name=f"ffn_out_{tag}")
    return y, (x, h, u, b, sa, sp, wo)


def _ffn_bwd(dy, saved, gain, wt, tag, tok, emit):
    x, h, u, b, sa, sp, wo = saved
    t, d = x.shape
    f = wo.shape[0]
    dyb = (dy + tok).astype(BF16)

    def dact(accs, ex):
        bv, sav, spv = (e.astype(F32) for e in ex)
        return (accs[0] * bv * spv, accs[0] * sav)

    bn = FFN_BN if f % FFN_BN == 0 else 256
    da, db = _mm([dyb], [wo], [(0, 0, 0)], 1, dact, [BF16, BF16], m=t, n=f, k=d, tb=True, bm=512, bn=bn, bk=d,
                 extras=[b, sa, sp], n_outer=True, chunk=MXU_COLS, name=f"ffn_dact_{tag}")
    (dwo,) = _mm([u], [dyb], [(0, 0, 0)], 1, lambda accs, ex: (0.5 * accs[0],), [BF16], m=f, n=d, k=t, ta=True,
                 bm=1408, bn=d, bk=1024, name=f"ffn_dwo_{tag}")
    dwt = [_mm([g], [h], [(0, 0, 0)], 1, _first, [BF16], m=f, n=d, k=t, ta=True, bm=1408, bn=d, bk=1024,
               name=f"ffn_dwt{i}_{tag}")[0] for i, g in enumerate((da, db))]
    tok = emit(jnp.concatenate(dwt, axis=0), dwo)
    bk = min(FFN_BN, f)
    dx, dgain = _mm([da, db], [wt, wt], [(0, 0, 0), (1, 1, 0)], 1, _norm_bwd_fin, [F32], m=t, n=d, k=f, bm=512, bn=d,
                    bk=bk, b_off=[(0, 0), (0, f // bk)], extras=[x, dy], consts=[(gain + tok).reshape(1, d)], n_sums=1,
                    name=f"ffn_dh_{tag}")
    return dx, jnp.sum(dgain, axis=0), tok


FFN_BN = 1408
Z_SPLITS = (("h", 4096), ("q", 1536), ("k", 1536), ("v", 1536), ("g", 2048))


def _mix_fwd(x, p, cos, sin):
    t, d = x.shape
    z, off, hm = {}, 0, None
    for nm, width in Z_SPLITS:
        bn = 1024 if off % 1024 == 0 and width % 1024 == 0 else 512
        first = hm is None
        res = _mm([x if first else hm], [p["wint"]], [(0, 0, 0)], 1, (lambda accs, ex, consts: (accs[0],)) if first else _first,
                  [F32 if nm == "h" else BF16], m=t, n=width, k=d, tb=True, bm=1024, bn=bn, bk=d, b_off=[(off // bn, 0)],
                  consts=[p["gm"].reshape(1, d)] if first else (), a_pro=_norm_pro if first else None, name=f"mix_in_{nm}")
        z[nm] = res[0]
        hm = res[1] if first else hm
        off += width
    o_raw, states = _hgrn_fwd(z["h"], p["lb3"])
    qb, kb, vb = _qk_prep(z["q"], z["k"], z["v"], p["qn"], p["kn"], cos, sin)
    o_g, lse_g = zip(*[_attn_fwd(qb[g], kb[g], vb[g], g) for g in range(ATT_GROUPS)])
    oa = _post_a(o_raw, z["h"], p["gout"])
    ob = _merge_b(o_g, lse_g)
    late = p["late"](ob)
    p = dict(p, **late)
    (ya,) = _mm([oa], [p["wa"]], [(0, 0, 0)], 1, _first, [F32], m=t, n=d, k=oa.shape[1], bm=1024, bn=d, bk=oa.shape[1],
                name="branch_a")

    def gate(accs, ex):
        return (_sigmoid(ex[0].astype(F32)) * ex[2] + _sigmoid(ex[1].astype(F32)) * accs[0], accs[0])

    merged, yb = _mm([ob], [p["wbt"]], [(0, 0, 0)], 1, gate, [BF16, F32], m=t, n=d, k=ATT_GW, tb=True, bm=512, bn=d,
                     bk=ATT_GW, extras=[z["g"], z["g"], ya], e_off=[0, 1, 0], chunk=MXU_COLS, name="branch_b_gate")
    (y,) = _mm([merged], [p["wo"]], [(0, 0, 0)], 1, lambda accs, ex: (ex[0] + accs[0],), [F32], m=t, n=d, k=d,
               bm=1024, bn=d, bk=d, extras=[x], name="mix_out")
    return y, (x, hm, z, o_raw, states, qb, kb, vb, o_g, lse_g, oa, ob, ya, yb, merged, late)


def _mix_bwd(dy, saved, p, cos, sin, tok):
    x, hm, z, o_raw, states, qb, kb, vb, o_g, lse_g, oa, ob, ya, yb, merged, late = saved
    p = dict(p, **late)
    t, d = x.shape
    w = oa.shape[1]
    dyb = (dy + tok).astype(BF16)

    def dgate(accs, ex):
        dm = accs[0]
        sa, sb = _sigmoid(ex[0].astype(F32)), _sigmoid(ex[1].astype(F32))
        return (sa * dm, sb * dm, dm * ex[2] * sa * (1.0 - sa), dm * ex[3] * sb * (1.0 - sb))

    dya, dyb_, dga, dgb = _mm([dyb], [p["wo"]], [(0, 0, 0)], 1, dgate, [BF16] * 4, m=t, n=d, k=d, tb=True, bm=512, bn=d,
                              bk=d, extras=[z["g"], z["g"], ya, yb], e_off=[0, 1, 0, 0], chunk=MXU_COLS, name="mix_out_bwd")
    (dwo,) = _mm([merged], [dyb], [(0, 0, 0)], 1, _first, [BF16], m=d, n=d, k=t, ta=True, bm=d, bn=d, bk=1024, name="mix_dwo")
    (doa,) = _mm([dya], [p["wa"]], [(0, 0, 0)], 1, _first, [F32], m=t, n=w, k=d, tb=True, bm=1024, bn=w, bk=d, name="branch_a_bwd")
    (dwa,) = _mm([oa], [dya], [(0, 0, 0)], 1, _first, [BF16], m=w, n=d, k=t, ta=True, bm=w, bn=d, bk=1024, name="branch_a_dw")
    (dob,) = _mm([dyb_], [p["wbt"]], [(0, 0, 0)], 1, _first, [F32], m=t, n=ATT_GW, k=d, bm=1024, bn=ATT_GW, bk=d,
                 name="branch_b_bwd")
    (dwbt,) = _mm([dyb_], [ob], [(0, 0, 0)], 1, _first, [BF16], m=d, n=ATT_GW, k=t, ta=True, bm=d, bn=ATT_GW, bk=1024,
                  name="branch_b_dw")
    do_raw, dhg, dgout = _post_a_bwd(o_raw, z["h"], p["gout"], doa)
    do_g, dlse_g = _merge_b_bwd(o_g, lse_g, dob)
    dq_g, dk_g, dv_g = zip(*[_attn_bwd(qb[g], kb[g], vb[g], o_g[g], lse_g[g], do_g[g], dlse_g[g], g)
                             for g in range(ATT_GROUPS)])
    dzq, dzk, dzv, dqn, dkn = _qk_prep_bwd(z["q"], z["k"], dq_g, dk_g, dv_g, p["qn"], p["kn"], cos, sin)
    dhq, dhf, dhi, lbsum = _hgrn_bwd(z["h"], p["lb3"], states, do_raw)
    dz = jnp.concatenate([dhq, dhf, dhi, dhg, dzq, dzk, dzv, dga, dgb], axis=1)
    pw = dz.shape[1]
    (dwint,) = _mm([dz], [hm], [(0, 0, 0)], 1, _first, [BF16], m=pw, n=d, k=t, ta=True, bm=1536, bn=d, bk=1024, name="mix_in_dw")
    dx, dgm = _mm([dz], [p["wint"]], [(0, 0, 0)], 1, _norm_bwd_fin, [F32], m=t, n=d, k=pw, bm=512, bn=d, bk=1536,
                  extras=[x, dy], consts=[p["gm"].reshape(1, d)], n_sums=1, name="mix_in_bwd")
    return dx, dict(gm=jnp.sum(dgm, axis=0), wint=dwint, lbsum=lbsum, gout=dgout, qn=dqn, kn=dkn, wa=dwa, wbt=dwbt, wo=dwo)


def _rope_tables(t):
    pos = jnp.arange(t, dtype=F32)
    inv = ROPE_THETA ** (-jnp.arange(0, ATT_DH, 2, dtype=F32) / ATT_DH)
    ang = pos[:, None] * inv[None, :]
    ang = jnp.concatenate([ang, ang], axis=-1)
    return jnp.cos(ang), jnp.sin(ang)


def _lower_bounds(logits):
    lb = jnp.cumsum(jax.nn.softmax(logits, axis=0), axis=0)
    return lb - lb[0:1]


def _head_gain(g):
    return jnp.tile(g[:, None, :], (1, ATT_HEADS, 1)).reshape(1, ATT_GROUPS * ATT_GW)


SMALL_GRADS = ("ffn1_norm", "mix_norm", "lbsum", "hgrn_out_norm", "attn_q_norm", "attn_k_norm", "ffn2_norm")


def _local_step(x, target, small, fetch, emit):
    t = x.shape[0]
    depth = small["ffn1_norm"].shape[0]
    cos, sin = _rope_tables(t)
    lb_all = _lower_bounds(small["hgrn_lb_logits"])
    saved = []
    for l in range(depth):
        w1t = fetch("w1t", l, x)["w1t"]
        x, s1 = _ffn_fwd(x, small["ffn1_norm"][l], w1t, lambda after, l=l: fetch("w1o", l, after)["w1o"], "1")
        p = dict(gm=small["mix_norm"][l], wint=fetch("wint", l, x)["wint"], lb3=lb_all[l].reshape(-1, 1, HG_DK),
                 gout=small["hgrn_out_norm"][l], qn=_head_gain(small["attn_q_norm"][l]),
                 kn=_head_gain(small["attn_k_norm"][l]), late=functools.partial(fetch, "mout", l))
        x, sm = _mix_fwd(x, p, cos, sin)
        w2t = fetch("w2t", l, x)["w2t"]
        x, s2 = _ffn_fwd(x, small["ffn2_norm"][l], w2t, lambda after, l=l: fetch("w2o", l, after)["w2o"], "2")
        saved.append((p, w1t, w2t, s1, sm, s2))
    loss, dx = _loss_head(x, target)
    gsmall = {k: [None] * depth for k in SMALL_GRADS}
    tok = jnp.zeros((), F32)
    for l in reversed(range(depth)):
        p, w1t, w2t, s1, sm, s2 = saved[l]
        dx, gsmall["ffn2_norm"][l], tok = _ffn_bwd(dx, s2, small["ffn2_norm"][l], w2t, "2", tok,
                                                   lambda dwt, dwo, l=l: emit("ffn2", l, dict(w2t=dwt, w2o=dwo), None))
        dx, gm = _mix_bwd(dx, sm, p, cos, sin, tok)
        tok = emit("mix", l, {k: gm[k] for k in ("wint", "wa", "wbt", "wo")}, None)
        gsmall["mix_norm"][l], gsmall["lbsum"][l], gsmall["hgrn_out_norm"][l] = gm["gm"], gm["lbsum"], gm["gout"]
        for k, src in (("attn_q_norm", "qn"), ("attn_k_norm", "kn")):
            gsmall[k][l] = jnp.sum(gm[src].reshape(ATT_GROUPS, ATT_HEADS, ATT_DH), axis=1)
        dx, gsmall["ffn1_norm"][l], tok = _ffn_bwd(dx, s1, small["ffn1_norm"][l], w1t, "1", tok,
                                                   lambda dwt, dwo, l=l: emit("ffn1", l, dict(w1t=dwt, w1o=dwo), None))
    emit("small", 0, {}, ({k: jnp.stack(v) for k, v in gsmall.items()}, loss))
    return dx


_HBM = pl.BlockSpec(memory_space=pltpu.HBM)
_SEM = pl.BlockSpec(memory_space=pltpu.SEMAPHORE)
_EFFECT = pltpu.SideEffectType.DATAFLOW_SIDE_EFFECTING


def _peer(p):
    x, y, c = lax.axis_index("x"), lax.axis_index("y"), lax.axis_index("c")
    me = 4 * x + 2 * y + c
    return (1 - x if p & 4 else x, 1 - y if p & 2 else y, 1 - c if p & 1 else c), jnp.bitwise_xor(me, p), me


def _xchg_copy(src, land, mode, send_sems, recv_sems, k, p, arriving):
    peer, peer_id, me = _peer(p)
    block = src if mode == "gather" else src.at[peer_id]
    return pltpu.make_async_remote_copy(
        src_ref=block, dst_ref=land.at[peer_id if arriving else me], send_sem=send_sems.at[k * (N_DEV - 1) + p - 1],
        recv_sem=recv_sems.at[k * (N_DEV - 1) + p - 1], device_id=peer, device_id_type=MESH)


def _xchg_start(srcs, modes, groups, name):
    n, ng = len(srcs), len(groups)

    def body(*refs):
        src = refs[:n]
        sems = refs[n:n + 2 * ng]
        land = refs[n + 2 * ng + n:n + 2 * ng + 2 * n]
        token = refs[n + 2 * ng + 2 * n]
        for gi, idx in enumerate(groups):
            for ki, k in enumerate(idx):
                for p in range(1, N_DEV):
                    _xchg_copy(src[k], land[k], modes[k], sems[2 * gi], sems[2 * gi + 1], ki, p, False).start()
        token[...] = jnp.zeros_like(token)

    sem_shapes = []
    for idx in groups:
        sem_shapes += [pltpu.SemaphoreType.DMA((len(idx) * (N_DEV - 1),))] * 2
    outs = pl.pallas_call(
        body,
        out_shape=sem_shapes + [pltpu.HBM(a.shape, a.dtype) for a in srcs]
        + [pltpu.HBM((N_DEV,) + a.shape[-2:], a.dtype) for a in srcs] + [jax.ShapeDtypeStruct((8, 128), F32)],
        in_specs=[_HBM] * n,
        out_specs=[_SEM] * (2 * ng) + [_HBM] * (2 * n) + [pl.BlockSpec(memory_space=pltpu.VMEM)],
        input_output_aliases={i: 2 * ng + i for i in range(n)},
        compiler_params=pltpu.CompilerParams(has_side_effects=_EFFECT),
        name=name,
    )(*[pltpu.with_memory_space_constraint(a, pltpu.HBM) for a in srcs])
    sems = [(outs[2 * gi], outs[2 * gi + 1]) for gi in range(ng)]
    return sems, outs[2 * ng:2 * ng + n], outs[2 * ng + n:2 * ng + 2 * n], outs[-1]


def _xchg_wait_call(srcs, lands, modes, sems, after, name):
    n = len(srcs)

    def body(*refs):
        src, land = refs[:n], refs[n:2 * n]
        send_sems, recv_sems = refs[2 * n], refs[2 * n + 1]
        for p in range(1, N_DEV):
            for k in range(n):
                cp = _xchg_copy(src[k], land[k], modes[k], send_sems, recv_sems, k, p, True)
                cp.wait_send()
                cp.wait_recv()

    outs = pl.pallas_call(
        body,
        out_shape=[pltpu.HBM(a.shape, a.dtype) for a in list(srcs) + list(lands)],
        in_specs=[_HBM] * (2 * n) + [_SEM, _SEM, pl.BlockSpec(memory_space=pl.ANY)],
        out_specs=[_HBM] * (2 * n),
        input_output_aliases={i: i for i in range(2 * n)},
        compiler_params=pltpu.CompilerParams(has_side_effects=_EFFECT),
        name=name,
    )(*srcs, *lands, sems[0], sems[1], after)
    return outs[:n], outs[n:]


def _xchg_wait(srcs, lands, modes, sems, after, name):
    srcs, lands = _xchg_wait_call(srcs, lands, modes, sems, after, name)
    me = 4 * lax.axis_index("x") + 2 * lax.axis_index("y") + lax.axis_index("c")
    done = []
    for a, land, mode in zip(srcs, lands, modes):
        own = a[None] if mode == "gather" else lax.dynamic_slice_in_dim(a, me, 1, axis=0)
        done.append(lax.dynamic_update_slice(land, own, (me, 0, 0)))
    return done


def _sum_slots(land):
    g, _, r, c = land.shape
    br = r // 2 if (r % 32 == 0 and r >= 256) else r

    def body(l_ref, o_ref):
        acc = l_ref[0, 0].astype(F32)
        for j in range(1, N_DEV):
            acc = acc + l_ref[0, j].astype(F32)
        o_ref[0] = acc

    return pl.pallas_call(
        body,
        out_shape=jax.ShapeDtypeStruct((g, r, c), F32),
        grid=(g, r // br),
        in_specs=[pl.BlockSpec((1, N_DEV, br, c), lambda i, j: (i, 0, j, 0))],
        out_specs=pl.BlockSpec((1, br, c), lambda i, j: (i, j, 0)),
        compiler_params=_cparams(("parallel", "parallel")),
        name="sum_slots",
    )(land)


def _adamw(w, g, m, v):
    shape = w.shape
    cols = shape[-1]
    rows = int(np.prod(shape[:-1]))
    bm = max(b for b in range(8, 257, 8) if rows % b == 0) if rows % 8 == 0 else rows
    c1 = 1.0 - ADAM_B1 ** ADAM_STEP
    c2 = 1.0 - ADAM_B2 ** ADAM_STEP

    def fn(ins, consts):
        wv, gv, mv, vv = ins
        m2 = ADAM_B1 * mv + (1.0 - ADAM_B1) * gv
        v2 = ADAM_B2 * vv + (1.0 - ADAM_B2) * (gv * gv)
        delta = -ADAM_LR * ((m2 / c1) / (jnp.sqrt(v2 / c2) + ADAM_EPS) + ADAM_WD * wv)
        return [delta, m2, v2], []

    outs, _ = _rowwise(fn, [(a.reshape(rows, cols), cols, 0) for a in (w, g, m, v)], [], [(cols, F32)] * 3, [],
                       bm=bm, name="adamw")
    return [o.reshape(shape) for o in outs]


BIG = ("w1t", "w1o", "wint", "wa", "wbt", "wo", "w2t", "w2o")
FETCH_GROUPS = dict(w1t=("w1t",), w1o=("w1o",), wint=("wint",), mout=("wa", "wbt", "wo"), w2t=("w2t",), w2o=("w2o",))
SMALL_ROWS = (("ffn1_norm", 0), ("mix_norm", 2), ("lbsum", 4), ("hgrn_out_norm", 6), ("ffn2_norm", 8),
              ("attn_q_norm", 10), ("attn_k_norm", 12))
SMALL_PACK_ROWS = 16


def kernel(x, ffn1_norm, ffn1_w_in, ffn1_w_out, mix_norm, w_in, hgrn_lb_logits, hgrn_out_norm, attn_q_norm, attn_k_norm, w_branch_a, w_branch_b, w_out, ffn2_norm, ffn2_w_in, ffn2_w_out, loss_target, m_ffn1_norm, m_ffn1_w_in, m_ffn1_w_out, m_mix_norm, m_w_in, m_hgrn_lb_logits, m_hgrn_out_norm, m_attn_q_norm, m_attn_k_norm, m_w_branch_a, m_w_branch_b, m_w_out, m_ffn2_norm, m_ffn2_w_in, m_ffn2_w_out, v_ffn1_norm, v_ffn1_w_in, v_ffn1_w_out, v_mix_norm, v_w_in, v_hgrn_lb_logits, v_hgrn_out_norm, v_attn_q_norm, v_attn_k_norm, v_w_branch_a, v_w_branch_b, v_w_out, v_ffn2_norm, v_ffn2_w_in, v_ffn2_w_out):
    names = ("ffn1_norm", "ffn1_w_in", "ffn1_w_out", "mix_norm", "w_in", "hgrn_lb_logits", "hgrn_out_norm", "attn_q_norm",
             "attn_k_norm", "w_branch_a", "w_branch_b", "w_out", "ffn2_norm", "ffn2_w_in", "ffn2_w_out")
    w = dict(zip(names, (ffn1_norm, ffn1_w_in, ffn1_w_out, mix_norm, w_in, hgrn_lb_logits, hgrn_out_norm, attn_q_norm,
                         attn_k_norm, w_branch_a, w_branch_b, w_out, ffn2_norm, ffn2_w_in, ffn2_w_out)))
    m = dict(zip(names, (m_ffn1_norm, m_ffn1_w_in, m_ffn1_w_out, m_mix_norm, m_w_in, m_hgrn_lb_logits, m_hgrn_out_norm,
                         m_attn_q_norm, m_attn_k_norm, m_w_branch_a, m_w_branch_b, m_w_out, m_ffn2_norm, m_ffn2_w_in, m_ffn2_w_out)))
    v = dict(zip(names, (v_ffn1_norm, v_ffn1_w_in, v_ffn1_w_out, v_mix_norm, v_w_in, v_hgrn_lb_logits, v_hgrn_out_norm,
                         v_attn_q_norm, v_attn_k_norm, v_w_branch_a, v_w_branch_b, v_w_out, v_ffn2_norm, v_ffn2_w_in, v_ffn2_w_out)))
    depth, d = ffn1_norm.shape

    def tr(a):
        return jnp.swapaxes(a, 1, 2)

    shard = dict(w1t=tr(ffn1_w_in), w1o=ffn1_w_out, wint=tr(w_in), wa=w_branch_a,
                 wbt=tr(w_branch_b).reshape(depth, -1, d), wo=w_out, w2t=tr(ffn2_w_in), w2o=ffn2_w_out)
    order = [(g, l) for l in range(depth) for g in FETCH_GROUPS]
    flat = [(g, l, k) for g, l in order for k in FETCH_GROUPS[g]]
    groups, pos = [], 0
    for g, l in order:
        groups.append(list(range(pos, pos + len(FETCH_GROUPS[g]))))
        pos += len(FETCH_GROUPS[g])
    g_sems, g_srcs, g_lands, _ = _xchg_start([shard[k][l].astype(BF16) for _, l, k in flat], ["gather"] * len(flat),
                                             groups, "gather_start")

    def fetch(group, l, after):
        gi = order.index((group, l))
        idx = groups[gi]
        lands = _xchg_wait([g_srcs[i] for i in idx], [g_lands[i] for i in idx], ["gather"] * len(idx), g_sems[gi], after,
                           f"gather_wait_{group}{l}")
        out = {}
        for k, land in zip(FETCH_GROUPS[group], lands):
            out[k] = land.reshape(d, -1) if k == "wbt" else land.reshape(-1, d)
        return out

    pending = []

    def emit(group, l, g, final):
        keys = list(g)
        srcs = [g[k].reshape(N_DEV, -1, d) for k in keys]
        modes = ["scatter"] * len(keys)
        if final is not None:
            gsmall, loss = final
            pack = jnp.zeros((SMALL_PACK_ROWS, d), F32)
            for k, r0 in SMALL_ROWS:
                rows = gsmall[k].reshape(depth, -1)
                pack = pack.at[r0:r0 + depth, :rows.shape[1]].set(rows)
            srcs.append(pack.at[14, :].set(loss))
            modes.append("gather")
            keys.append("small")
        sems, s_thru, l_thru, token = _xchg_start(srcs, modes, [list(range(len(srcs)))], f"grads_start_{group}{l}")
        pending.append((group, l, keys, modes, sems[0], s_thru, l_thru))
        return token[0, 0]

    small = {k: w[k] for k in ("ffn1_norm", "mix_norm", "hgrn_lb_logits", "hgrn_out_norm", "attn_q_norm", "attn_k_norm", "ffn2_norm")}
    dx = _local_step(x[0], loss_target[0], small, fetch, emit)

    summed = {}
    for group, l, keys, modes, sems, s_thru, l_thru in pending:
        lands = _xchg_wait(s_thru, l_thru, modes, sems, dx, f"grads_wait_{group}{l}")
        for k, land in zip(keys, lands):
            summed[k, l] = _sum_slots(land[None])[0]
    gsum = {k: jnp.stack([summed[k, l] for l in range(depth)]) for k in BIG}
    tot = summed["small", 0]

    grads = {}
    for k, r0 in SMALL_ROWS:
        shp = (depth,) + (w[k].shape[1:] if k != "lbsum" else (d,))
        grads[k] = tot[r0:r0 + depth, :int(np.prod(shp[1:]))].reshape(shp)
    _, lb_vjp = jax.vjp(_lower_bounds, hgrn_lb_logits)
    grads["hgrn_lb_logits"] = lb_vjp(grads.pop("lbsum"))[0]
    grads["ffn1_w_in"], grads["ffn1_w_out"] = tr(gsum["w1t"]), gsum["w1o"]
    grads["w_in"], grads["w_branch_a"] = tr(gsum["wint"]), gsum["wa"]
    grads["w_branch_b"] = tr(gsum["wbt"].reshape(depth, d // N_DEV, -1))
    grads["w_out"] = gsum["wo"]
    grads["ffn2_w_in"], grads["ffn2_w_out"] = tr(gsum["w2t"]), gsum["w2o"]

    upd = {k: _adamw(w[k], grads[k], m[k], v[k]) for k in names}
    return (tot[14, 0], dx[None], *[grads[k] for k in names], *[upd[k][0] for k in names],
            *[upd[k][1] for k in names], *[upd[k][2] for k in names])
```

```python
import functools
import math

import jax
import jax.numpy as jnp
import numpy as np
from jax import lax
from jax.experimental import pallas as pl
from jax.experimental.pallas import tpu as pltpu

F32 = jnp.float32
BF16 = jnp.bfloat16

N_DEV = 8
EPS = 1e-6
HG_DK = 128
HG_CHUNK = 64
HG_SUB = 16
HG_HP = 4
ATT_PATTERNS = ((128, 1), (512, 4), (2048, 16))
ATT_GROUPS = 3
ATT_HEADS = 4
ATT_DH = 128
ATT_BLK = 128
ROPE_THETA = 10000.0
ADAM_LR, ADAM_B1, ADAM_B2, ADAM_EPS, ADAM_WD, ADAM_STEP = 0.001, 0.9, 0.999, 1e-08, 0.01, 10
VMEM_LIMIT_BYTES = 56 * 1024 * 1024
MXU_COLS = 256
MESH = pl.DeviceIdType.MESH


def _cparams(sem, **kw):
    return pltpu.CompilerParams(dimension_semantics=sem, vmem_limit_bytes=VMEM_LIMIT_BYTES, **kw)


def _sigmoid(x):
    return 1.0 / (1.0 + jnp.exp(-x))


def _mm(a_list, b_list, pairs, n_acc, fin, out_dtypes, *, m, n, k, ta=False, tb=False, bm, bn, bk,
        b_off=None, extras=(), e_off=None, n_outer=False, consts=(), a_pro=None, n_sums=0, chunk=0, name):
    bm, bn, bk = min(bm, m), min(bn, n), min(bk, k)
    assert m % bm == 0 and n % bn == 0 and k % bk == 0, (name, m, n, k, bm, bn, bk)
    nk = k // bk
    assert not (a_pro and (nk > 1 or ta or n_outer)) and not (n_sums and (bn != n or n_outer)), name
    assert not (chunk and (nk > 1 or n_sums or chunk % 128)), name
    b_off = b_off or [(0, 0)] * len(b_list)
    e_off = e_off or [0] * len(extras)
    na, nb, ne, nc, no = len(a_list), len(b_list), len(extras), len(consts), len(out_dtypes)
    nao = na if a_pro else 0
    dn = (((0,) if ta else (1,), (1,) if tb else (0,)), ((), ()))

    def body(*refs):
        refs = list(refs)
        a_refs, b_refs, e_refs, c_refs, o_refs, ao_refs, s_refs = (
            [refs.pop(0) for _ in range(cnt)] for cnt in (na, nb, ne, nc, no, nao, n_sums))
        acc_refs = refs
        kk = pl.program_id(2)
        first = pl.program_id(0) == 0
        cvals = [c[...] for c in c_refs]
        a_vals = [r[...] for r in a_refs]
        if a_pro:
            a_vals = a_pro(a_vals, cvals)
            for r, v in zip(ao_refs, a_vals):
                r[...] = v
        if chunk:
            spans = [slice(lo, min(lo + chunk, bn)) for lo in range(0, bn, chunk)]
            chunks = []
            for cs in spans:
                parts = [None] * n_acc
                for ai, bi, ci in pairs:
                    p = lax.dot_general(a_vals[ai], b_refs[bi][cs, :] if tb else b_refs[bi][:, cs], dn,
                                        preferred_element_type=F32)
                    parts[ci] = p if parts[ci] is None else parts[ci] + p
                chunks.append(parts)
            for cs, parts in zip(spans, chunks):
                ex = [e[:, cs] for e in e_refs]
                outs = fin(parts, ex, cvals) if nc else fin(parts, ex)
                for o_ref, o in zip(o_refs, outs):
                    o_ref[:, cs] = o.astype(o_ref.dtype)
            return

        parts = [None] * n_acc
        for ai, bi, ci in pairs:
            p = lax.dot_general(a_vals[ai], b_refs[bi][...], dn, preferred_element_type=F32)
            parts[ci] = p if parts[ci] is None else parts[ci] + p

        def finish(accs):
            ex = [e[...] for e in e_refs]
            res = fin(accs, ex, cvals) if nc else fin(accs, ex)
            outs, sums = res if n_sums else (res, ())
            for o_ref, o in zip(o_refs, outs):
                o_ref[...] = o.astype(o_ref.dtype)
            if n_sums:
                @pl.when(first)
                def _():
                    for s_ref, s in zip(s_refs, sums):
                        s_ref[...] = s

                @pl.when(jnp.logical_not(first))
                def _():
                    for s_ref, s in zip(s_refs, sums):
                        s_ref[...] += s

        if nk == 1:
            finish(parts)
        else:
            @pl.when(kk == 0)
            def _():
                for c in range(n_acc):
                    acc_refs[c][...] = parts[c]

            @pl.when(kk > 0)
            def _():
                for c in range(n_acc):
                    acc_refs[c][...] += parts[c]

            @pl.when(kk == nk - 1)
            def _():
                finish([acc_refs[c][...] for c in range(n_acc)])

    def ij(f):
        return (lambda j, i, q: f(i, j, q)) if n_outer else f

    a_spec = pl.BlockSpec((bk, bm), ij(lambda i, j, q: (q, i))) if ta else pl.BlockSpec((bm, bk), ij(lambda i, j, q: (i, q)))

    b_mode = dict(pipeline_mode=pl.Buffered(1)) if (bn == n and nk == 1) else {}

    def b_spec(off):
        on, ok = off
        if tb:
            return pl.BlockSpec((bn, bk), ij(lambda i, j, q: (j + on, q + ok)), **b_mode)
        return pl.BlockSpec((bk, bn), ij(lambda i, j, q: (q + ok, j + on)), **b_mode)

    mn_spec = pl.BlockSpec((bm, bn), ij(lambda i, j, q: (i, j)))
    outs = pl.pallas_call(
        body,
        out_shape=[jax.ShapeDtypeStruct((m, n), d) for d in out_dtypes] + [jax.ShapeDtypeStruct((m, k), BF16)] * nao
        + [jax.ShapeDtypeStruct((8, n), F32)] * n_sums,
        grid=(n // bn, m // bm, nk) if n_outer else (m // bm, n // bn, nk),
        in_specs=[a_spec] * na + [b_spec(o) for o in b_off]
        + [pl.BlockSpec((bm, bn), ij(lambda i, j, q, o=o: (i, j + o))) for o in e_off]
        + [pl.BlockSpec(c.shape, lambda *_, nd=c.ndim: (0,) * nd) for c in consts],
        out_specs=[mn_spec] * no + [a_spec] * nao + [pl.BlockSpec((8, n), lambda *_: (0, 0))] * n_sums,
        scratch_shapes=[pltpu.VMEM((bm, bn), F32) for _ in range(n_acc if nk > 1 else 0)],
        compiler_params=_cparams(("arbitrary" if n_sums else "parallel", "parallel", "arbitrary")),
        name=name,
    )(*a_list, *b_list, *extras, *consts)
    return outs


def _first(accs, ex):
    return (accs[0],)


def _rowwise(fn, ins, consts, out_defs, sum_widths, *, bm, name):
    ins = [tuple(e) + (1,) * (4 - len(e)) for e in ins]
    out_defs = [tuple(e) + (1,) * (3 - len(e)) for e in out_defs]
    t = ins[0][0].shape[-2] * ins[0][3]
    bm = min(bm, t)
    assert t % bm == 0, (name, t, bm)
    ni, nc, no, ns = len(ins), len(consts), len(out_defs), len(sum_widths)
    strided = [w for _, w, _, d in ins if d > 1] + [w for w, _, d in out_defs if d > 1]

    def body(*refs):
        i_refs, c_refs = refs[:ni], refs[ni:ni + nc]
        o_refs, s_refs = refs[ni + nc:ni + nc + no], refs[ni + nc + no:ni + nc + no + ns]
        scratch = list(refs[ni + nc + no + ns:])
        vals = []
        for ref, (_, w, _, d) in zip(i_refs, ins):
            if d == 1:
                vals.append(ref[...])
                continue
            s = scratch.pop(0)
            for r in range(d):
                for c in range(w // 128):
                    s.at[c][pl.ds(r, bm // d, stride=d), :] = ref[r, :, c * 128:(c + 1) * 128].astype(F32)
            vals.append(jnp.concatenate([s[c] for c in range(w // 128)], axis=1))
        outs, sums = fn(vals, [r[...] for r in c_refs])
        for o_ref, o, (w, _, d) in zip(o_refs, outs, out_defs):
            if d == 1:
                o_ref[...] = o.astype(o_ref.dtype)
                continue
            s = scratch.pop(0)
            for c in range(w // 128):
                s[c] = o[:, c * 128:(c + 1) * 128].astype(F32)
            for r in range(d):
                for c in range(w // 128):
                    o_ref[r, :, c * 128:(c + 1) * 128] = s.at[c][pl.ds(r, bm // d, stride=d), :].astype(o_ref.dtype)
        if ns:
            first = pl.program_id(0) == 0

            @pl.when(first)
            def _():
                for s_ref, s in zip(s_refs, sums):
                    s_ref[...] = s

            @pl.when(jnp.logical_not(first))
            def _():
                for s_ref, s in zip(s_refs, sums):
                    s_ref[...] += s

    def win(width, cb, d):
        if d > 1:
            return pl.BlockSpec((d, bm // d, width), lambda i: (0, i, 0))
        return pl.BlockSpec((bm, width), lambda i: (i, cb))

    res = pl.pallas_call(
        body,
        out_shape=[jax.ShapeDtypeStruct((t, w) if d == 1 else (d, t // d, w), dt) for w, dt, d in out_defs]
        + [jax.ShapeDtypeStruct((8, w), F32) for w in sum_widths],
        grid=(t // bm,),
        in_specs=[win(w, cb, d) for _, w, cb, d in ins] + [pl.BlockSpec(c.shape, lambda i, nd=c.ndim: (0,) * nd) for c in consts],
        out_specs=[win(w, 0, d) for w, _, d in out_defs] + [pl.BlockSpec((8, w), lambda i: (0, 0)) for w in sum_widths],
        scratch_shapes=[pltpu.VMEM((w // 128, bm, 128), F32) for w in strided],
        compiler_params=_cparams(("arbitrary",) if ns else ("parallel",)),
        name=name,
    )(*[e[0] for e in ins], *consts)
    return res[:no], [jnp.sum(s, axis=0) for s in res[no:]]


def _colsum8(x):
    bm, w = x.shape
    return jnp.sum(x.reshape(bm // 8, 8, w), axis=0)


def _tri(n, upper=False):
    r = lax.broadcasted_iota(jnp.int32, (n, n), 0)
    c = lax.broadcasted_iota(jnp.int32, (n, n), 1)
    return (c >= r) if upper else (c <= r)


def _exact_tri_matmul(tri_bf16, x):
    x0 = x.astype(BF16)
    r1 = x - x0.astype(F32)
    x1 = r1.astype(BF16)
    x2 = (r1 - x1.astype(F32)).astype(BF16)
    w = x.shape[1]
    y = jnp.dot(tri_bf16, jnp.concatenate([x0, x1, x2], axis=1), preferred_element_type=F32)
    return y[:, :w] + y[:, w:2 * w] + y[:, 2 * w:]


def _dot_nt(a, b):
    return lax.dot_general(a, b, (((1,), (1,)), ((), ())), preferred_element_type=F32)


def _dot_tn(a, b):
    return lax.dot_general(a, b, (((0,), (0,)), ((), ())), preferred_element_type=F32)


def _dot(a, b):
    return jnp.dot(a, b, preferred_element_type=F32)


def _hg_gates(hq, hf, lb):
    sq = _sigmoid(hq)
    q = hq * sq
    sg = _sigmoid(hf)
    f = lb + (1.0 - lb) * sg
    return q, sq, sg, f


def _hg_intra(q, kk, g):
    c = q.shape[0]
    rows = lax.broadcasted_iota(jnp.int32, (c, 1), 0)
    a_rows, qts, kts, eqs, eks = [], [], [], [], []
    for i in range(c // HG_SUB):
        lo = i * HG_SUB
        ref = g[lo - 1:lo, :] if i else jnp.zeros_like(g[0:1, :])
        eq = jnp.exp(g[lo:lo + HG_SUB, :] - ref)
        ek = jnp.exp(jnp.where(rows < lo + HG_SUB, ref - g, 0.0))
        qt = q[lo:lo + HG_SUB, :] * eq
        kt = kk * ek
        a = _dot_nt(qt.astype(BF16), kt.astype(BF16))
        tpos = lo + lax.broadcasted_iota(jnp.int32, (HG_SUB, c), 0)
        spos = lax.broadcasted_iota(jnp.int32, (HG_SUB, c), 1)
        a_rows.append(jnp.where(spos <= tpos, a, 0.0))
        qts.append(qt), kts.append(kt), eqs.append(eq), eks.append(ek)
    return jnp.concatenate(a_rows, axis=0), qts, kts, eqs, eks


def _hgrn_fwd_serial(zh, lb3, *, tb=512):
    t = zh.shape[0]
    nh = lb3.shape[0]
    c = HG_CHUNK
    tb = min(tb, t)
    nchunk = tb // c
    hp = HG_HP if nh % HG_HP == 0 else 1

    def body(hq_ref, hf_ref, hi_ref, lb_ref, o_ref, st_ref, state):
        @pl.when(pl.program_id(1) == 0)
        def _():
            state[...] = jnp.zeros_like(state)

        tril = _tri(c).astype(BF16)

        def one_head(hh, ci, sl):
            ls = slice(hh * HG_DK, (hh + 1) * HG_DK)
            q, _, _, f = _hg_gates(hq_ref[sl, ls], hf_ref[sl, ls], lb_ref[hh])
            v = hi_ref[sl, ls]
            kk = 1.0 - f
            g = _exact_tri_matmul(tril, jnp.log(f))
            a, _, _, _, _ = _hg_intra(q, kk, g)
            st = state[hh]
            st_ref[hh, ci] = st
            vb = v.astype(BF16)
            o = _dot(a.astype(BF16), vb) + _dot_nt((q * jnp.exp(g)).astype(BF16), st.astype(BF16))
            o_ref[sl, ls] = o
            glast = g[c - 1:c, :]
            kg = kk * jnp.exp(glast - g)
            state[hh] = st * jnp.exp(glast) + _dot_tn(vb, kg.astype(BF16))

        def chunk(ci, carry):
            sl = pl.ds(pl.multiple_of(ci * c, c), c)
            for hh in range(hp):
                one_head(hh, ci, sl)
            return carry

        lax.fori_loop(0, nchunk, chunk, 0)

    def col(cb):
        return pl.BlockSpec((tb, hp * HG_DK), lambda h, i: (i, cb * (nh // hp) + h))

    return pl.pallas_call(
        body,
        out_shape=[jax.ShapeDtypeStruct((t, nh * HG_DK), F32), jax.ShapeDtypeStruct((nh, t // c, HG_DK, HG_DK), F32)],
        grid=(nh // hp, t // tb),
        in_specs=[col(0), col(1), col(2), pl.BlockSpec((hp, 1, HG_DK), lambda h, i: (h, 0, 0))],
        out_specs=[pl.BlockSpec((tb, hp * HG_DK), lambda h, i: (i, h)),
                   pl.BlockSpec((hp, nchunk, HG_DK, HG_DK), lambda h, i: (h, i, 0, 0))],
        scratch_shapes=[pltpu.VMEM((hp, HG_DK, HG_DK), F32)],
        compiler_params=_cparams(("parallel", "arbitrary")),
        name="hgrn_fwd",
    )(zh, zh, zh, lb3)


def _hgrn_bwd_serial(zh, lb3, states, d_o, *, tb=512):
    t = zh.shape[0]
    nh = lb3.shape[0]
    c = HG_CHUNK
    tb = min(tb, t)
    nchunk = tb // c
    nblk = t // tb
    hp = HG_HP if nh % HG_HP == 0 else 1

    def body(hq_ref, hf_ref, hi_ref, lb_ref, st_ref, do_ref, dq_ref, df_ref, dv_ref, dlb_ref, dstate):
        @pl.when(pl.program_id(1) == 0)
        def _():
            dstate[...] = jnp.zeros_like(dstate)
            dlb_ref[...] = jnp.zeros_like(dlb_ref)

        tril = _tri(c).astype(BF16)
        triu = _tri(c, upper=True).astype(BF16)
        last_row = lax.broadcasted_iota(jnp.int32, (c, 1), 0) == c - 1

        def one_head(hh, ci, sl):
            ls = slice(hh * HG_DK, (hh + 1) * HG_DK)
            lb = lb_ref[hh]
            hq, hf = hq_ref[sl, ls], hf_ref[sl, ls]
            q, sq, sg, f = _hg_gates(hq, hf, lb)
            v = hi_ref[sl, ls]
            kk = 1.0 - f
            g = _exact_tri_matmul(tril, jnp.log(f))
            a, qts, kts, eqs, eks = _hg_intra(q, kk, g)
            st = st_ref[hh, ci]
            dst = dstate[hh]
            do = do_ref[sl, ls]
            dob, vb = do.astype(BF16), v.astype(BF16)
            glast = g[c - 1:c, :]
            eg = jnp.exp(g)
            egl = jnp.exp(glast - g)
            qg = q * eg
            kg = kk * egl
            dv = _dot_tn(a.astype(BF16), dob) + _dot_nt(kg.astype(BF16), dst.astype(BF16))
            da = jnp.where(_tri(c), _dot_nt(dob, vb), 0.0).astype(BF16)
            dq_parts, dgq_parts = [], []
            dk = jnp.zeros_like(kk)
            dgk = jnp.zeros_like(kk)
            for i in range(c // HG_SUB):
                da_i = da[i * HG_SUB:(i + 1) * HG_SUB, :]
                ktb, qtb = kts[i].astype(BF16), qts[i].astype(BF16)
                xi = _dot(da_i, ktb)
                yi = _dot_tn(da_i, qtb)
                dq_parts.append(xi * eqs[i])
                dk = dk + yi * eks[i]
                dgq_parts.append(xi * qtb.astype(F32))
                dgk = dgk + yi * ktb.astype(F32)
            dq_inter = _dot(dob, st.astype(BF16)) * eg
            dq = jnp.concatenate(dq_parts, axis=0) + dq_inter
            dk_state = _dot(vb, dst.astype(BF16)) * egl
            dk = dk + dk_state
            dg = jnp.concatenate(dgq_parts, axis=0) - dgk + q * dq_inter - kk * dk_state
            dgl = jnp.sum(kk * dk_state, axis=0, keepdims=True) + jnp.exp(glast) * jnp.sum(st * dst, axis=0, keepdims=True)
            dg = dg + jnp.where(last_row, dgl, 0.0)
            dlogf = _exact_tri_matmul(triu, dg)
            dfv = dlogf / f - dk
            dq_ref[sl, ls] = (dq * (sq * (1.0 + hq * (1.0 - sq)))).astype(dq_ref.dtype)
            df_ref[sl, ls] = (dfv * (1.0 - lb) * sg * (1.0 - sg)).astype(df_ref.dtype)
            dv_ref[sl, ls] = dv.astype(dv_ref.dtype)
            dlb_ref[hh] += jnp.sum(dfv * (1.0 - sg), axis=0, keepdims=True)
            dstate[hh] = dst * jnp.exp(glast) + _dot_tn(dob, qg.astype(BF16))

        def chunk(j, carry):
            ci = nchunk - 1 - j
            sl = pl.ds(pl.multiple_of(ci * c, c), c)
            for hh in range(hp):
                one_head(hh, ci, sl)
            return carry

        lax.fori_loop(0, nchunk, chunk, 0)

    def col(cb):
        return pl.BlockSpec((tb, hp * HG_DK), lambda h, i: (nblk - 1 - i, cb * (nh // hp) + h))

    ocol = pl.BlockSpec((tb, hp * HG_DK), lambda h, i: (nblk - 1 - i, h))
    w = nh * HG_DK
    dq, df, dv, dlb = pl.pallas_call(
        body,
        out_shape=[jax.ShapeDtypeStruct((t, w), BF16)] * 3 + [jax.ShapeDtypeStruct((nh, 1, HG_DK), F32)],
        grid=(nh // hp, nblk),
        in_specs=[col(0), col(1), col(2), pl.BlockSpec((hp, 1, HG_DK), lambda h, i: (h, 0, 0)),
                  pl.BlockSpec((hp, nchunk, HG_DK, HG_DK), lambda h, i: (h, nblk - 1 - i, 0, 0)), ocol],
        out_specs=[ocol, ocol, ocol, pl.BlockSpec((hp, 1, HG_DK), lambda h, i: (h, 0, 0))],
        scratch_shapes=[pltpu.VMEM((hp, HG_DK, HG_DK), F32)],
        compiler_params=_cparams(("parallel", "arbitrary")),
        name="hgrn_bwd",
    )(zh, zh, zh, lb3, states, d_o)
    return dq, df, dv, dlb.reshape(w)


def _hg_heads(x, hp):
    return [x[:, h * HG_DK:(h + 1) * HG_DK] for h in range(hp)]


def _hg_intra_wide(q, kk, g, hp):
    c = q.shape[0]
    rows = lax.broadcasted_iota(jnp.int32, (c, 1), 0)
    a_rows = [[] for _ in range(hp)]
    qts, kts, eqs, eks = [], [], [], []
    for i in range(c // HG_SUB):
        lo = i * HG_SUB
        ref = g[lo - 1:lo, :] if i else jnp.zeros_like(g[0:1, :])
        eq = jnp.exp(g[lo:lo + HG_SUB, :] - ref)
        ek = jnp.exp(jnp.where(rows < lo + HG_SUB, ref - g, 0.0))
        qtb = (q[lo:lo + HG_SUB, :] * eq).astype(BF16)
        ktb = (kk * ek).astype(BF16)
        tpos = lo + lax.broadcasted_iota(jnp.int32, (HG_SUB, c), 0)
        spos = lax.broadcasted_iota(jnp.int32, (HG_SUB, c), 1)
        for h, (qh, kh) in enumerate(zip(_hg_heads(qtb, hp), _hg_heads(ktb, hp))):
            a_rows[h].append(jnp.where(spos <= tpos, _dot_nt(qh, kh), 0.0))
        qts.append(qtb), kts.append(ktb), eqs.append(eq), eks.append(ek)
    return [jnp.concatenate(r, axis=0) for r in a_rows], qts, kts, eqs, eks


def _hgrn_fwd(zh, lb3, *, tb=512):
    t = zh.shape[0]
    nh = lb3.shape[0]
    c = HG_CHUNK
    tb = min(tb, t)
    nchunk = tb // c
    hp = HG_HP if nh % HG_HP == 0 else 1
    wp = hp * HG_DK

    def body(hq_ref, hf_ref, hi_ref, lb_ref, o_ref, st_ref, state):
        @pl.when(pl.program_id(1) == 0)
        def _():
            state[...] = jnp.zeros_like(state)

        tril = _tri(c).astype(BF16)

        def chunk(ci, carry):
            sl = pl.ds(pl.multiple_of(ci * c, c), c)
            q, _, _, f = _hg_gates(hq_ref[sl, :], hf_ref[sl, :], lb_ref[...])
            kk = 1.0 - f
            g = _exact_tri_matmul(tril, jnp.log(f))
            a, _, _, _, _ = _hg_intra_wide(q, kk, g, hp)
            vb = hi_ref[sl, :].astype(BF16)
            glast = g[c - 1:c, :]
            qgb = (q * jnp.exp(g)).astype(BF16)
            kgb = (kk * jnp.exp(glast - g)).astype(BF16)
            dec = jnp.exp(glast)
            sts = [state[h] for h in range(hp)]
            for h in range(hp):
                st_ref[h, ci] = sts[h]
            vh, qgh, kgh, dech = _hg_heads(vb, hp), _hg_heads(qgb, hp), _hg_heads(kgb, hp), _hg_heads(dec, hp)
            o = [_dot(a[h].astype(BF16), vh[h]) + _dot_nt(qgh[h], sts[h].astype(BF16)) for h in range(hp)]
            new = [_dot_tn(vh[h], kgh[h]) for h in range(hp)]
            o_ref[sl, :] = jnp.concatenate(o, axis=1)
            for h in range(hp):
                state[h] = sts[h] * dech[h] + new[h]
            return carry

        lax.fori_loop(0, nchunk, chunk, 0)

    def col(cb):
        return pl.BlockSpec((tb, wp), lambda h, i: (i, cb * (nh // hp) + h))

    return pl.pallas_call(
        body,
        out_shape=[jax.ShapeDtypeStruct((t, nh * HG_DK), F32), jax.ShapeDtypeStruct((nh, t // c, HG_DK, HG_DK), F32)],
        grid=(nh // hp, t // tb),
        in_specs=[col(0), col(1), col(2), pl.BlockSpec((1, wp), lambda h, i: (0, h))],
        out_specs=[pl.BlockSpec((tb, wp), lambda h, i: (i, h)),
                   pl.BlockSpec((hp, nchunk, HG_DK, HG_DK), lambda h, i: (h, i, 0, 0))],
        scratch_shapes=[pltpu.VMEM((hp, HG_DK, HG_DK), F32)],
        compiler_params=_cparams(("parallel", "arbitrary")),
        name="hgrn_fwd",
    )(zh, zh, zh, lb3.reshape(1, -1))


def _hgrn_bwd(zh, lb3, states, d_o, *, tb=512):
    t = zh.shape[0]
    nh = lb3.shape[0]
    c = HG_CHUNK
    tb = min(tb, t)
    nchunk = tb // c
    nblk = t // tb
    hp = HG_HP if nh % HG_HP == 0 else 1
    wp = hp * HG_DK

    def body(hq_ref, hf_ref, hi_ref, lb_ref, st_ref, do_ref, dq_ref, df_ref, dv_ref, dlb_ref, dstate):
        @pl.when(pl.program_id(1) == 0)
        def _():
            dstate[...] = jnp.zeros_like(dstate)
            dlb_ref[...] = jnp.zeros_like(dlb_ref)

        tril = _tri(c).astype(BF16)
        triu = _tri(c, upper=True).astype(BF16)
        last_row = lax.broadcasted_iota(jnp.int32, (c, 1), 0) == c - 1
        heads = range(hp)

        def chunk(j, carry):
            ci = nchunk - 1 - j
            sl = pl.ds(pl.multiple_of(ci * c, c), c)
            lb = lb_ref[...]
            hq, hf = hq_ref[sl, :], hf_ref[sl, :]
            q, sq, sg, f = _hg_gates(hq, hf, lb)
            kk = 1.0 - f
            g = _exact_tri_matmul(tril, jnp.log(f))
            a, qts, kts, eqs, eks = _hg_intra_wide(q, kk, g, hp)
            glast = g[c - 1:c, :]
            eg, egl, dec = jnp.exp(g), jnp.exp(glast - g), jnp.exp(glast)
            vb, dob = hi_ref[sl, :].astype(BF16), do_ref[sl, :].astype(BF16)
            qgb, kgb = (q * eg).astype(BF16), (kk * egl).astype(BF16)
            sts = [st_ref[h, ci] for h in heads]
            dsts = [dstate[h] for h in heads]
            stb, dstb = [s.astype(BF16) for s in sts], [s.astype(BF16) for s in dsts]
            vh, doh, qgh, kgh = _hg_heads(vb, hp), _hg_heads(dob, hp), _hg_heads(qgb, hp), _hg_heads(kgb, hp)
            dv = [_dot_tn(a[h].astype(BF16), doh[h]) + _dot_nt(kgh[h], dstb[h]) for h in heads]
            da = [jnp.where(_tri(c), _dot_nt(doh[h], vh[h]), 0.0).astype(BF16) for h in heads]
            dq_inter = jnp.concatenate([_dot(doh[h], stb[h]) for h in heads], axis=1) * eg
            dk_state = jnp.concatenate([_dot(vh[h], dstb[h]) for h in heads], axis=1) * egl
            new_dst = [_dot_tn(doh[h], qgh[h]) for h in heads]
            xs, dk, dgk = [], dk_state, 0.0
            for i in range(c // HG_SUB):
                rs = slice(i * HG_SUB, (i + 1) * HG_SUB)
                kth, qth = _hg_heads(kts[i], hp), _hg_heads(qts[i], hp)
                xi = jnp.concatenate([_dot(da[h][rs, :], kth[h]) for h in heads], axis=1)
                yi = jnp.concatenate([_dot_tn(da[h][rs, :], qth[h]) for h in heads], axis=1)
                xs.append(xi)
                dk = dk + yi * eks[i]
                dgk = dgk + yi * kts[i].astype(F32)
            dq = jnp.concatenate([x * e for x, e in zip(xs, eqs)], axis=0) + dq_inter
            dgq = jnp.concatenate([x * qt.astype(F32) for x, qt in zip(xs, qts)], axis=0)
            dg = dgq - dgk + q * dq_inter - kk * dk_state
            sdot = jnp.concatenate([jnp.sum(sts[h] * dsts[h], axis=0, keepdims=True) for h in heads], axis=1)
            dgl = jnp.sum(kk * dk_state, axis=0, keepdims=True) + dec * sdot
            dg = dg + jnp.where(last_row, dgl, 0.0)
            dlogf = _exact_tri_matmul(triu, dg)
            dfv = dlogf / f - dk
            dq_ref[sl, :] = (dq * (sq * (1.0 + hq * (1.0 - sq)))).astype(dq_ref.dtype)
            df_ref[sl, :] = (dfv * (1.0 - lb) * sg * (1.0 - sg)).astype(df_ref.dtype)
            dv_ref[sl, :] = jnp.concatenate(dv, axis=1).astype(dv_ref.dtype)
            dlb_ref[...] += jnp.sum(dfv * (1.0 - sg), axis=0, keepdims=True)
            dech = _hg_heads(dec, hp)
            for h in heads:
                dstate[h] = dsts[h] * dech[h] + new_dst[h]
            return carry

        lax.fori_loop(0, nchunk, chunk, 0)

    def col(cb):
        return pl.BlockSpec((tb, wp), lambda h, i: (nblk - 1 - i, cb * (nh // hp) + h))

    ocol = pl.BlockSpec((tb, wp), lambda h, i: (nblk - 1 - i, h))
    lbspec = pl.BlockSpec((1, wp), lambda h, i: (0, h))
    w = nh * HG_DK
    dq, df, dv, dlb = pl.pallas_call(
        body,
        out_shape=[jax.ShapeDtypeStruct((t, w), BF16)] * 3 + [jax.ShapeDtypeStruct((1, w), F32)],
        grid=(nh // hp, nblk),
        in_specs=[col(0), col(1), col(2), lbspec,
                  pl.BlockSpec((hp, nchunk, HG_DK, HG_DK), lambda h, i: (h, nblk - 1 - i, 0, 0)), ocol],
        out_specs=[ocol, ocol, ocol, lbspec],
        scratch_shapes=[pltpu.VMEM((hp, HG_DK, HG_DK), F32)],
        compiler_params=_cparams(("parallel", "arbitrary")),
        name="hgrn_bwd",
    )(zh, zh, zh, lb3.reshape(1, -1), states, d_o)
    return dq, df, dv, dlb.reshape(w)


NEG = -1e30
ATT_GW = ATT_HEADS * ATT_DH


def _att_scores(q, kp, kc, has_prev):
    scale = ATT_DH ** -0.5
    i = lax.broadcasted_iota(jnp.int32, (ATT_BLK, ATT_BLK), 0)
    j = lax.broadcasted_iota(jnp.int32, (ATT_BLK, ATT_BLK), 1)
    s_p = jnp.where(jnp.logical_and(j >= i, has_prev), _dot_nt(q, kp) * scale, NEG)
    s_c = jnp.where(j <= i, _dot_nt(q, kc) * scale, NEG)
    return s_p, s_c


def _att_views(arrs, d):
    return [a.reshape(d, -1, ATT_GW) for a in arrs]


def _att_unview(a, d):
    return a.reshape(-1, ATT_GW) if d == 1 else a


ATT_QB = 4


def _attn_fwd(qb, kb, vb, g):
    d = ATT_PATTERNS[g][1]
    q2, k2, v2 = _att_views([qb, kb, vb], d)
    nblk = q2.shape[1] // ATT_BLK
    nq = ATT_QB if nblk % ATT_QB == 0 else 1
    rows = nq * ATT_BLK

    def body(q_ref, kc_ref, kp_ref, vc_ref, vp_ref, o_ref, l_ref):
        first = pl.program_id(1) == 0
        hss = [slice(h * ATT_DH, (h + 1) * ATT_DH) for h in range(ATT_HEADS)]
        for b in range(nq):
            rs = slice(b * ATT_BLK, (b + 1) * ATT_BLK)
            ps = slice((b - 1) * ATT_BLK, b * ATT_BLK)
            has_prev = jnp.logical_not(first) if b == 0 else True
            kv = [(kp_ref[:, hs], vp_ref[:, hs]) if b == 0 else (kc_ref[ps, hs], vc_ref[ps, hs]) for hs in hss]
            sc = [_att_scores(q_ref[rs, hs], kv[h][0], kc_ref[rs, hs], has_prev) for h, hs in enumerate(hss)]
            ms = [jnp.maximum(jnp.max(s_p, axis=1, keepdims=True), jnp.max(s_c, axis=1, keepdims=True)) for s_p, s_c in sc]
            ps_ = [(jnp.exp(s_p - m), jnp.exp(s_c - m)) for (s_p, s_c), m in zip(sc, ms)]
            ls = [jnp.sum(p_p, axis=1, keepdims=True) + jnp.sum(p_c, axis=1, keepdims=True) for p_p, p_c in ps_]
            os_ = [_dot(p_p.astype(BF16), kv[h][1]) + _dot(p_c.astype(BF16), vc_ref[rs, hss[h]]) for h, (p_p, p_c) in enumerate(ps_)]
            for h, hs in enumerate(hss):
                o_ref[rs, hs] = os_[h] / ls[h]
                l_ref[rs, hs] = jnp.broadcast_to(ms[h] + jnp.log(ls[h]), (ATT_BLK, ATT_DH))

    cur = pl.BlockSpec((None, rows, ATT_GW), lambda r, n: (r, n, 0))
    prev = pl.BlockSpec((None, ATT_BLK, ATT_GW), lambda r, n: (r, jnp.maximum(n * nq - 1, 0), 0))
    o, lse = pl.pallas_call(
        body,
        out_shape=[jax.ShapeDtypeStruct(q2.shape, F32)] * 2,
        grid=(d, nblk // nq),
        in_specs=[cur, cur, prev, cur, prev],
        out_specs=[cur, cur],
        compiler_params=_cparams(("parallel", "arbitrary")),
        name=f"attn_fwd_g{g}",
    )(q2, k2, k2, v2, v2)
    return _att_unview(o, d), _att_unview(lse, d)


def _attn_bwd(qb, kb, vb, o, lse, d_o, d_lse, g):
    d = ATT_PATTERNS[g][1]
    q2, k2, v2 = _att_views([qb, kb, vb], d)
    o2, l2, do2, dl2 = _att_views([o, lse, d_o, d_lse], d)
    nblk = q2.shape[1] // ATT_BLK
    nq = ATT_QB if nblk % ATT_QB == 0 else 1
    rows = nq * ATT_BLK
    ns = nblk // nq
    scale = ATT_DH ** -0.5

    def body(q_ref, kc_ref, kp_ref, vc_ref, vp_ref, o_ref, l_ref, do_ref, dl_ref, dq_ref, dk_ref, dv_ref, ck, cv):
        n = pl.program_id(1)

        @pl.when(n == 0)
        def _():
            ck[...] = jnp.zeros_like(ck)
            cv[...] = jnp.zeros_like(cv)

        first = n == ns - 1
        hss = [slice(h * ATT_DH, (h + 1) * ATT_DH) for h in range(ATT_HEADS)]
        heads = range(ATT_HEADS)
        pend_k, pend_v = [ck[:, hs] for hs in hss], [cv[:, hs] for hs in hss]
        for b in reversed(range(nq)):
            rs = slice(b * ATT_BLK, (b + 1) * ATT_BLK)
            ps = slice((b - 1) * ATT_BLK, b * ATT_BLK)
            has_prev = jnp.logical_not(first) if b == 0 else True
            q = [q_ref[rs, hs] for hs in hss]
            kc, vc = [kc_ref[rs, hs] for hs in hss], [vc_ref[rs, hs] for hs in hss]
            kp = [kp_ref[:, hs] if b == 0 else kc_ref[ps, hs] for hs in hss]
            vp = [vp_ref[:, hs] if b == 0 else vc_ref[ps, hs] for hs in hss]
            sc = [_att_scores(q[h], kp[h], kc[h], has_prev) for h in heads]
            dob = [do_ref[rs, hs].astype(BF16) for hs in hss]
            dp = [(_dot_nt(dob[h], vp[h]), _dot_nt(dob[h], vc[h])) for h in heads]
            delta = [jnp.sum(do_ref[rs, hs] * o_ref[rs, hs] - dl_ref[rs, hs], axis=1, keepdims=True) for hs in hss]
            pr = [(jnp.exp(sc[h][0] - l_ref[rs, hss[h]][:, 0:1]), jnp.exp(sc[h][1] - l_ref[rs, hss[h]][:, 0:1])) for h in heads]
            ds = [((pr[h][0] * (dp[h][0] - delta[h]) * scale).astype(BF16), (pr[h][1] * (dp[h][1] - delta[h]) * scale).astype(BF16))
                  for h in heads]
            pb = [(pr[h][0].astype(BF16), pr[h][1].astype(BF16)) for h in heads]
            dq = [_dot(ds[h][0], kp[h]) + _dot(ds[h][1], kc[h]) for h in heads]
            dk_c = [_dot_tn(ds[h][1], q[h]) for h in heads]
            dv_c = [_dot_tn(pb[h][1], dob[h]) for h in heads]
            dk_p = [_dot_tn(ds[h][0], q[h]) for h in heads]
            dv_p = [_dot_tn(pb[h][0], dob[h]) for h in heads]
            for h, hs in enumerate(hss):
                dq_ref[rs, hs] = dq[h]
                dk_ref[rs, hs] = pend_k[h] + dk_c[h]
                dv_ref[rs, hs] = pend_v[h] + dv_c[h]
            pend_k, pend_v = dk_p, dv_p
        for h, hs in enumerate(hss):
            ck[:, hs] = pend_k[h]
            cv[:, hs] = pend_v[h]

    cur = pl.BlockSpec((None, rows, ATT_GW), lambda r, n: (r, ns - 1 - n, 0))
    prev = pl.BlockSpec((None, ATT_BLK, ATT_GW), lambda r, n: (r, jnp.maximum((ns - 1 - n) * nq - 1, 0), 0))
    shp = jax.ShapeDtypeStruct(q2.shape, F32)
    dq, dk, dv = pl.pallas_call(
        body,
        out_shape=[shp, shp, shp],
        grid=(d, ns),
        in_specs=[cur, cur, prev, cur, prev, cur, cur, cur, cur],
        out_specs=[cur, cur, cur],
        scratch_shapes=[pltpu.VMEM((ATT_BLK, ATT_GW), F32), pltpu.VMEM((ATT_BLK, ATT_GW), F32)],
        compiler_params=_cparams(("parallel", "arbitrary")),
        name=f"attn_bwd_g{g}",
    )(q2, k2, k2, v2, v2, o2, l2, do2, dl2)
    return _att_unview(dq, d), _att_unview(dk, d), _att_unview(dv, d)


def _attn_fwd_1blk(qb, kb, vb, g):
    d = ATT_PATTERNS[g][1]
    q2, k2, v2 = _att_views([qb, kb, vb], d)
    nb = q2.shape[1] // ATT_BLK

    def body(q_ref, kc_ref, kp_ref, vc_ref, vp_ref, o_ref, l_ref):
        has_prev = pl.program_id(1) > 0
        for h in range(ATT_HEADS):
            hs = slice(h * ATT_DH, (h + 1) * ATT_DH)
            s_p, s_c = _att_scores(q_ref[:, hs], kp_ref[:, hs], kc_ref[:, hs], has_prev)
            m = jnp.maximum(jnp.max(s_p, axis=1, keepdims=True), jnp.max(s_c, axis=1, keepdims=True))
            p_p, p_c = jnp.exp(s_p - m), jnp.exp(s_c - m)
            l = jnp.sum(p_p, axis=1, keepdims=True) + jnp.sum(p_c, axis=1, keepdims=True)
            o = _dot(p_p.astype(BF16), vp_ref[:, hs]) + _dot(p_c.astype(BF16), vc_ref[:, hs])
            o_ref[:, hs] = o / l
            l_ref[:, hs] = jnp.broadcast_to(m + jnp.log(l), (ATT_BLK, ATT_DH))

    cur = pl.BlockSpec((None, ATT_BLK, ATT_GW), lambda r, n: (r, n, 0))
    prev = pl.BlockSpec((None, ATT_BLK, ATT_GW), lambda r, n: (r, jnp.maximum(n - 1, 0), 0))
    o, lse = pl.pallas_call(
        body,
        out_shape=[jax.ShapeDtypeStruct(q2.shape, F32)] * 2,
        grid=(d, nb),
        in_specs=[cur, cur, prev, cur, prev],
        out_specs=[cur, cur],
        compiler_params=_cparams(("parallel", "arbitrary")),
        name=f"attn_fwd_g{g}",
    )(q2, k2, k2, v2, v2)
    return _att_unview(o, d), _att_unview(lse, d)


def _attn_bwd_1blk(qb, kb, vb, o, lse, d_o, d_lse, g):
    d = ATT_PATTERNS[g][1]
    q2, k2, v2 = _att_views([qb, kb, vb], d)
    o2, l2, do2, dl2 = _att_views([o, lse, d_o, d_lse], d)
    nb = q2.shape[1] // ATT_BLK

    def body(q_ref, kc_ref, kp_ref, vc_ref, vp_ref, o_ref, l_ref, do_ref, dl_ref, dq_ref, dk_ref, dv_ref, ck, cv):
        n = pl.program_id(1)
        active = n < nb

        @pl.when(n == 0)
        def _():
            ck[...] = jnp.zeros_like(ck)
            cv[...] = jnp.zeros_like(cv)

        @pl.when(jnp.logical_not(active))
        def _():
            dk_ref[...] = ck[...]
            dv_ref[...] = cv[...]

        @pl.when(active)
        def _():
            has_prev = n > 0
            for h in range(ATT_HEADS):
                hs = slice(h * ATT_DH, (h + 1) * ATT_DH)
                q, kp, kc, vp, vc = q_ref[:, hs], kp_ref[:, hs], kc_ref[:, hs], vp_ref[:, hs], vc_ref[:, hs]
                s_p, s_c = _att_scores(q, kp, kc, has_prev)
                lse_h = l_ref[:, hs][:, 0:1]
                p_p, p_c = jnp.exp(s_p - lse_h), jnp.exp(s_c - lse_h)
                do = do_ref[:, hs]
                delta = jnp.sum(do * o_ref[:, hs] - dl_ref[:, hs], axis=1, keepdims=True)
                dob = do.astype(BF16)
                scale = ATT_DH ** -0.5
                ds_p = (p_p * (_dot_nt(dob, vp) - delta) * scale).astype(BF16)
                ds_c = (p_c * (_dot_nt(dob, vc) - delta) * scale).astype(BF16)
                dq_ref[:, hs] = _dot(ds_p, kp) + _dot(ds_c, kc)
                dk_ref[:, hs] = ck[:, hs] + _dot_tn(ds_p, q)
                dv_ref[:, hs] = cv[:, hs] + _dot_tn(p_p.astype(BF16), dob)
                ck[:, hs] = _dot_tn(ds_c, q)
                cv[:, hs] = _dot_tn(p_c.astype(BF16), dob)

    def qn(n):
        return jnp.minimum(n, nb - 1)

    cur = pl.BlockSpec((None, ATT_BLK, ATT_GW), lambda r, n: (r, qn(n), 0))
    prev = pl.BlockSpec((None, ATT_BLK, ATT_GW), lambda r, n: (r, jnp.maximum(qn(n) - 1, 0), 0))
    behind = pl.BlockSpec((None, ATT_BLK, ATT_GW), lambda r, n: (r, jnp.maximum(n - 1, 0), 0))
    shp = jax.ShapeDtypeStruct(q2.shape, F32)
    dq, dk, dv = pl.pallas_call(
        body,
        out_shape=[shp, shp, shp],
        grid=(d, nb + 1),
        in_specs=[cur, cur, prev, cur, prev, cur, cur, cur, cur],
        out_specs=[cur, behind, behind],
        scratch_shapes=[pltpu.VMEM((ATT_BLK, ATT_GW), F32), pltpu.VMEM((ATT_BLK, ATT_GW), F32)],
        compiler_params=_cparams(("parallel", "arbitrary")),
        name=f"attn_bwd_g{g}",
    )(q2, k2, k2, v2, v2, o2, l2, do2, dl2)
    return _att_unview(dq, d), _att_unview(dk, d), _att_unview(dv, d)


def _rms_parts(x, width):
    outs = []
    for lo in range(0, x.shape[1], width):
        xs = x[:, lo:lo + width].astype(F32)
        r = lax.rsqrt(jnp.mean(xs * xs, axis=1, keepdims=True) + EPS)
        outs.append((xs * r, r))
    return outs


def _rms_bwd_part(xh, r, dxh):
    return r * (dxh - xh * jnp.mean(dxh * xh, axis=1, keepdims=True))


def _norm_pro(a, consts):
    (xh, _), = _rms_parts(a[0], a[0].shape[1])
    return [(xh * consts[0]).astype(BF16)]


def _norm_bwd_fin(accs, ex, consts):
    xv, dres = ex
    (xh, r), = _rms_parts(xv, xv.shape[1])
    return [dres + _rms_bwd_part(xh, r, accs[0] * consts[0])], [_colsum8(accs[0] * xh)]


def _norm_fwd(x, gain):
    d = x.shape[1]

    def fn(ins, consts):
        (xh, _), = _rms_parts(ins[0], d)
        return [xh * consts[0]], []

    (h,), _ = _rowwise(fn, [(x, d, 0)], [gain.reshape(1, d)], [(d, BF16)], [], bm=512, name="norm_fwd")
    return h


def _norm_bwd(x, gain, dh, dres):
    d = x.shape[1]

    def fn(ins, consts):
        (xh, r), = _rms_parts(ins[0], d)
        dx = ins[2] + _rms_bwd_part(xh, r, ins[1] * consts[0])
        return [dx], [_colsum8(ins[1] * xh)]

    (dx,), (dg,) = _rowwise(fn, [(x, d, 0), (dh, d, 0), (dres, d, 0)], [gain.reshape(1, d)], [(d, F32)], [d],
                            bm=512, name="norm_bwd")
    return dx, dg


def _rot_sign():
    lane = lax.broadcasted_iota(jnp.int32, (1, ATT_DH), 1)
    return jnp.where(lane < ATT_DH // 2, -1.0, 1.0).astype(F32)


def _rope(y, cos, sin):
    return y * cos + pltpu.roll(y, ATT_DH // 2, axis=1) * _rot_sign() * sin


def _rope_t(dy, cos, sin):
    return dy * cos - pltpu.roll(dy * sin, ATT_DH // 2, axis=1) * _rot_sign()


def _qk_prep(zq, zk, zv, qn, kn, cos, sin):
    w = zq.shape[1]

    def fn(ins, consts):
        cs, sn = ins[3], ins[4]
        outs = []
        for z, gain in ((ins[0], consts[0]), (ins[1], consts[1])):
            for i, (xh, _) in enumerate(_rms_parts(z, ATT_DH)):
                outs.append(_rope(xh * gain[:, i * ATT_DH:(i + 1) * ATT_DH], cs, sn))
        outs += [ins[2][:, i * ATT_DH:(i + 1) * ATT_DH] for i in range(w // ATT_DH)]
        groups = [jnp.concatenate(outs[i:i + ATT_HEADS], axis=1) for i in range(0, len(outs), ATT_HEADS)]
        return groups, []

    outs, _ = _rowwise(fn, [(zq, w, 0), (zk, w, 0), (zv, w, 0), (cos, ATT_DH, 0), (sin, ATT_DH, 0)], [qn, kn],
                       [(ATT_GW, BF16, ATT_PATTERNS[g][1]) for g in range(ATT_GROUPS)] * 3, [], bm=256, name="qk_prep")
    return outs[0:3], outs[3:6], outs[6:9]


def _qk_prep_bwd(zq, zk, dq_g, dk_g, dv_g, qn, kn, cos, sin):
    w = zq.shape[1]

    def fn(ins, consts):
        cs, sn = ins[2], ins[3]
        outs, sums = [], []
        for z, gain, dparts in ((ins[0], consts[0], ins[4:7]), (ins[1], consts[1], ins[7:10])):
            dout = jnp.concatenate(dparts, axis=1)
            dz, dgain = [], []
            for i, (xh, r) in enumerate(_rms_parts(z, ATT_DH)):
                hs = slice(i * ATT_DH, (i + 1) * ATT_DH)
                dy = _rope_t(dout[:, hs], cs, sn)
                dgain.append(_colsum8(dy * xh))
                dz.append(_rms_bwd_part(xh, r, dy * gain[:, hs]))
            outs.append(jnp.concatenate(dz, axis=1))
            sums.append(jnp.concatenate(dgain, axis=1))
        outs.append(jnp.concatenate(ins[10:13], axis=1))
        return outs, sums

    ins = [(zq, w, 0), (zk, w, 0), (cos, ATT_DH, 0), (sin, ATT_DH, 0)]
    for parts in (dq_g, dk_g, dv_g):
        ins += [(a, ATT_GW, 0, ATT_PATTERNS[g][1]) for g, a in enumerate(parts)]
    (dzq, dzk, dzv), (dqn, dkn) = _rowwise(fn, ins, [qn, kn], [(w, BF16)] * 3, [w, w], bm=256, name="qk_prep_bwd")
    return dzq, dzk, dzv, dqn, dkn


def _post_a(o_raw, zh, gout):
    w = o_raw.shape[1]

    def fn(ins, consts):
        oh = jnp.concatenate([xh for xh, _ in _rms_parts(ins[0], HG_DK)], axis=1)
        hg = ins[1]
        return [oh * consts[0] * (hg * _sigmoid(hg))], []

    (y,), _ = _rowwise(fn, [(o_raw, w, 0), (zh, w, 3)], [gout.reshape(1, w)], [(w, BF16)], [], bm=512, name="post_a")
    return y


def _post_a_bwd(o_raw, zh, gout, dy):
    w = o_raw.shape[1]

    def fn(ins, consts):
        parts = _rms_parts(ins[0], HG_DK)
        oh = jnp.concatenate([xh for xh, _ in parts], axis=1)
        hg, dyv, gain = ins[1], ins[2], consts[0]
        sg = _sigmoid(hg)
        s = hg * sg
        doh = dyv * gain * s
        do = jnp.concatenate([_rms_bwd_part(xh, r, doh[:, i * HG_DK:(i + 1) * HG_DK]) for i, (xh, r) in enumerate(parts)], axis=1)
        dhg = dyv * oh * gain * (sg * (1.0 + hg * (1.0 - sg)))
        return [do, dhg], [_colsum8(dyv * oh * s)]

    (do, dhg), (dgain,) = _rowwise(fn, [(o_raw, w, 0), (zh, w, 3), (dy, w, 0)], [gout.reshape(1, w)],
                                   [(w, F32), (w, BF16)], [w], bm=512, name="post_a_bwd")
    return do, dhg, dgain


def _merge_alpha(lses):
    m = jnp.maximum(jnp.maximum(lses[0], lses[1]), lses[2])
    e = [jnp.exp(l - m) for l in lses]
    inv = 1.0 / (e[0] + e[1] + e[2])
    return [x * inv for x in e]


def _group_ins(parts):
    return [(a, ATT_GW, 0, ATT_PATTERNS[g][1]) for g, a in enumerate(parts)]


def _merge_b(o_g, lse_g):
    def fn(ins, consts):
        al = _merge_alpha(ins[3:6])
        return [al[0] * ins[0] + al[1] * ins[1] + al[2] * ins[2]], []

    (y,), _ = _rowwise(fn, _group_ins(o_g) + _group_ins(lse_g), [], [(ATT_GW, BF16)], [], bm=512, name="merge_b")
    return y


def _merge_b_bwd(o_g, lse_g, dy):
    def fn(ins, consts):
        al = _merge_alpha(ins[3:6])
        dyv = ins[6]
        dal = [dyv * ins[i] for i in range(3)]
        tot = al[0] * dal[0] + al[1] * dal[1] + al[2] * dal[2]
        return [al[i] * dyv for i in range(3)] + [al[i] * (dal[i] - tot) for i in range(3)], []

    outs, _ = _rowwise(fn, _group_ins(o_g) + _group_ins(lse_g) + [(dy, ATT_GW, 0)], [],
                       [(ATT_GW, F32, ATT_PATTERNS[g][1]) for g in range(ATT_GROUPS)] * 2, [], bm=512, name="merge_b_bwd")
    return outs[:3], outs[3:]


def _loss_head(y, target):
    d = y.shape[1]

    def fn(ins, consts):
        e = ins[0] - ins[1]
        return [e * (1.0 / d)], [_colsum8(e * e)]

    (dy,), (sq,) = _rowwise(fn, [(y, d, 0), (target, d, 0)], [], [(d, F32)], [d], bm=512, name="loss_head")
    return 0.5 * jnp.sum(sq) / d, dy


def _silu_grad(a):
    s = _sigmoid(a)
    return s * (1.0 + a * (1.0 - s))


def _ffn_fwd(x, gain, wt, wo_fn, tag):
    t, d = x.shape
    f = wt.shape[0] // 2

    def act(accs, ex, consts):
        a, b = accs
        s = _sigmoid(a)
        sa = a * s
        return (sa * b, b, 0.5 * sa, 0.5 * (s + sa * (1.0 - s)))

    bn = FFN_BN if f % FFN_BN == 0 else 256
    u, b, sa, sp, h = _mm([x], [wt, wt], [(0, 0, 0), (0, 1, 1)], 2, act, [BF16] * 4, m=t, n=f, k=d, tb=True,
                          bm=512, bn=bn, bk=d, b_off=[(0, 0), (f // min(bn, f), 0)],
                          consts=[gain.reshape(1, d)], a_pro=_norm_pro, chunk=MXU_COLS, name=f"ffn_in_{tag}")
    wo = wo_fn(u)
    (y,) = _mm([u], [wo], [(0, 0, 0)], 1, lambda accs, ex: (ex[0] + 0.5 * accs[0],), [F32], m=t, n=d, k=f,
               bm=512, bn=d, bk=f, extras=[x], name=f"ffn_out_{tag}")
    return y, (x, h, u, b, sa, sp, wo)


def _ffn_bwd(dy, saved, gain, wt, tag, tok, emit):
    x, h, u, b, sa, sp, wo = saved
    t, d = x.shape
    f = wo.shape[0]
    dyb = (dy + tok).astype(BF16)

    def dact(accs, ex):
        bv, sav, spv = (e.astype(F32) for e in ex)
        return (accs[0] * bv * spv, accs[0] * sav)

    bn = FFN_BN if f % FFN_BN == 0 else 256
    da, db = _mm([dyb], [wo], [(0, 0, 0)], 1, dact, [BF16, BF16], m=t, n=f, k=d, tb=True, bm=512, bn=bn, bk=d,
                 extras=[b, sa, sp], n_outer=True, chunk=MXU_COLS, name=f"ffn_dact_{tag}")
    (dwo,) = _mm([u], [dyb], [(0, 0, 0)], 1, lambda accs, ex: (0.5 * accs[0],), [BF16], m=f, n=d, k=t, ta=True,
                 bm=1408, bn=d, bk=1024, name=f"ffn_dwo_{tag}")
    dwt = [_mm([g], [h], [(0, 0, 0)], 1, _first, [BF16], m=f, n=d, k=t, ta=True, bm=1408, bn=d, bk=1024,
               name=f"ffn_dwt{i}_{tag}")[0] for i, g in enumerate((da, db))]
    tok = emit(jnp.concatenate(dwt, axis=0), dwo)
    bk = min(FFN_BN, f)
    dx, dgain = _mm([da, db], [wt, wt], [(0, 0, 0), (1, 1, 0)], 1, _norm_bwd_fin, [F32], m=t, n=d, k=f, bm=512, bn=d,
                    bk=bk, b_off=[(0, 0), (0, f // bk)], extras=[x, dy], consts=[(gain + tok).reshape(1, d)], n_sums=1,
                    name=f"ffn_dh_{tag}")
    return dx, jnp.sum(dgain, axis=0), tok


FFN_BN = 2816
Z_SPLITS = (("h", 4096), ("q", 1536), ("k", 1536), ("v", 1536), ("g", 2048))


def _mix_fwd(x, p, cos, sin):
    t, d = x.shape
    z, off, hm = {}, 0, None
    for nm, width in Z_SPLITS:
        bn = 1024 if off % 1024 == 0 and width % 1024 == 0 else 512
        first = hm is None
        res = _mm([x if first else hm], [p["wint"]], [(0, 0, 0)], 1, (lambda accs, ex, consts: (accs[0],)) if first else _first,
                  [F32 if nm == "h" else BF16], m=t, n=width, k=d, tb=True, bm=1024, bn=bn, bk=d, b_off=[(off // bn, 0)],
                  consts=[p["gm"].reshape(1, d)] if first else (), a_pro=_norm_pro if first else None, name=f"mix_in_{nm}")
        z[nm] = res[0]
        hm = res[1] if first else hm
        off += width
    o_raw, states = _hgrn_fwd(z["h"], p["lb3"])
    qb, kb, vb = _qk_prep(z["q"], z["k"], z["v"], p["qn"], p["kn"], cos, sin)
    o_g, lse_g = zip(*[_attn_fwd(qb[g], kb[g], vb[g], g) for g in range(ATT_GROUPS)])
    oa = _post_a(o_raw, z["h"], p["gout"])
    ob = _merge_b(o_g, lse_g)
    late = p["late"](ob)
    p = dict(p, **late)
    (ya,) = _mm([oa], [p["wa"]], [(0, 0, 0)], 1, _first, [F32], m=t, n=d, k=oa.shape[1], bm=1024, bn=d, bk=oa.shape[1],
                name="branch_a")

    def gate(accs, ex):
        return (_sigmoid(ex[0].astype(F32)) * ex[2] + _sigmoid(ex[1].astype(F32)) * accs[0], accs[0])

    merged, yb = _mm([ob], [p["wbt"]], [(0, 0, 0)], 1, gate, [BF16, F32], m=t, n=d, k=ATT_GW, tb=True, bm=512, bn=d,
                     bk=ATT_GW, extras=[z["g"], z["g"], ya], e_off=[0, 1, 0], chunk=MXU_COLS, name="branch_b_gate")
    (y,) = _mm([merged], [p["wo"]], [(0, 0, 0)], 1, lambda accs, ex: (ex[0] + accs[0],), [F32], m=t, n=d, k=d,
               bm=1024, bn=d, bk=d, extras=[x], name="mix_out")
    return y, (x, hm, z, o_raw, states, qb, kb, vb, o_g, lse_g, oa, ob, ya, yb, merged, late)


def _mix_bwd(dy, saved, p, cos, sin, tok):
    x, hm, z, o_raw, states, qb, kb, vb, o_g, lse_g, oa, ob, ya, yb, merged, late = saved
    p = dict(p, **late)
    t, d = x.shape
    w = oa.shape[1]
    dyb = (dy + tok).astype(BF16)

    def dgate(accs, ex):
        dm = accs[0]
        sa, sb = _sigmoid(ex[0].astype(F32)), _sigmoid(ex[1].astype(F32))
        return (sa * dm, sb * dm, dm * ex[2] * sa * (1.0 - sa), dm * ex[3] * sb * (1.0 - sb))

    dya, dyb_, dga, dgb = _mm([dyb], [p["wo"]], [(0, 0, 0)], 1, dgate, [BF16] * 4, m=t, n=d, k=d, tb=True, bm=512, bn=d,
                              bk=d, extras=[z["g"], z["g"], ya, yb], e_off=[0, 1, 0, 0], chunk=MXU_COLS, name="mix_out_bwd")
    (dwo,) = _mm([merged], [dyb], [(0, 0, 0)], 1, _first, [BF16], m=d, n=d, k=t, ta=True, bm=d, bn=d, bk=1024, name="mix_dwo")
    (doa,) = _mm([dya], [p["wa"]], [(0, 0, 0)], 1, _first, [F32], m=t, n=w, k=d, tb=True, bm=1024, bn=w, bk=d, name="branch_a_bwd")
    (dwa,) = _mm([oa], [dya], [(0, 0, 0)], 1, _first, [BF16], m=w, n=d, k=t, ta=True, bm=w, bn=d, bk=1024, name="branch_a_dw")
    (dob,) = _mm([dyb_], [p["wbt"]], [(0, 0, 0)], 1, _first, [F32], m=t, n=ATT_GW, k=d, bm=1024, bn=ATT_GW, bk=d,
                 name="branch_b_bwd")
    (dwbt,) = _mm([dyb_], [ob], [(0, 0, 0)], 1, _first, [BF16], m=d, n=ATT_GW, k=t, ta=True, bm=d, bn=ATT_GW, bk=1024,
                  name="branch_b_dw")
    do_raw, dhg, dgout = _post_a_bwd(o_raw, z["h"], p["gout"], doa)
    do_g, dlse_g = _merge_b_bwd(o_g, lse_g, dob)
    dq_g, dk_g, dv_g = zip(*[_attn_bwd(qb[g], kb[g], vb[g], o_g[g], lse_g[g], do_g[g], dlse_g[g], g)
                             for g in range(ATT_GROUPS)])
    dzq, dzk, dzv, dqn, dkn = _qk_prep_bwd(z["q"], z["k"], dq_g, dk_g, dv_g, p["qn"], p["kn"], cos, sin)
    dhq, dhf, dhi, lbsum = _hgrn_bwd(z["h"], p["lb3"], states, do_raw)
    dz = jnp.concatenate([dhq, dhf, dhi, dhg, dzq, dzk, dzv, dga, dgb], axis=1)
    pw = dz.shape[1]
    (dwint,) = _mm([dz], [hm], [(0, 0, 0)], 1, _first, [BF16], m=pw, n=d, k=t, ta=True, bm=1536, bn=d, bk=1024, name="mix_in_dw")
    dx, dgm = _mm([dz], [p["wint"]], [(0, 0, 0)], 1, _norm_bwd_fin, [F32], m=t, n=d, k=pw, bm=512, bn=d, bk=1536,
                  extras=[x, dy], consts=[p["gm"].reshape(1, d)], n_sums=1, name="mix_in_bwd")
    return dx, dict(gm=jnp.sum(dgm, axis=0), wint=dwint, lbsum=lbsum, gout=dgout, qn=dqn, kn=dkn, wa=dwa, wbt=dwbt, wo=dwo)


def _rope_tables(t):
    pos = jnp.arange(t, dtype=F32)
    inv = ROPE_THETA ** (-jnp.arange(0, ATT_DH, 2, dtype=F32) / ATT_DH)
    ang = pos[:, None] * inv[None, :]
    ang = jnp.concatenate([ang, ang], axis=-1)
    return jnp.cos(ang), jnp.sin(ang)


def _lower_bounds(logits):
    lb = jnp.cumsum(jax.nn.softmax(logits, axis=0), axis=0)
    return lb - lb[0:1]


def _head_gain(g):
    return jnp.tile(g[:, None, :], (1, ATT_HEADS, 1)).reshape(1, ATT_GROUPS * ATT_GW)


SMALL_GRADS = ("ffn1_norm", "mix_norm", "lbsum", "hgrn_out_norm", "attn_q_norm", "attn_k_norm", "ffn2_norm")


def _local_step(x, target, small, fetch, emit):
    t = x.shape[0]
    depth = small["ffn1_norm"].shape[0]
    cos, sin = _rope_tables(t)
    lb_all = _lower_bounds(small["hgrn_lb_logits"])
    saved = []
    for l in range(depth):
        w1t = fetch("w1t", l, x)["w1t"]
        x, s1 = _ffn_fwd(x, small["ffn1_norm"][l], w1t, lambda after, l=l: fetch("w1o", l, after)["w1o"], "1")
        p = dict(gm=small["mix_norm"][l], wint=fetch("wint", l, x)["wint"], lb3=lb_all[l].reshape(-1, 1, HG_DK),
                 gout=small["hgrn_out_norm"][l], qn=_head_gain(small["attn_q_norm"][l]),
                 kn=_head_gain(small["attn_k_norm"][l]), late=functools.partial(fetch, "mout", l))
        x, sm = _mix_fwd(x, p, cos, sin)
        w2t = fetch("w2t", l, x)["w2t"]
        x, s2 = _ffn_fwd(x, small["ffn2_norm"][l], w2t, lambda after, l=l: fetch("w2o", l, after)["w2o"], "2")
        saved.append((p, w1t, w2t, s1, sm, s2))
    loss, dx = _loss_head(x, target)
    gsmall = {k: [None] * depth for k in SMALL_GRADS}
    tok = jnp.zeros((), F32)
    for l in reversed(range(depth)):
        p, w1t, w2t, s1, sm, s2 = saved[l]
        dx, gsmall["ffn2_norm"][l], tok = _ffn_bwd(dx, s2, small["ffn2_norm"][l], w2t, "2", tok,
                                                   lambda dwt, dwo, l=l: emit("ffn2", l, dict(w2t=dwt, w2o=dwo), None))
        dx, gm = _mix_bwd(dx, sm, p, cos, sin, tok)
        tok = emit("mix", l, {k: gm[k] for k in ("wint", "wa", "wbt", "wo")}, None)
        gsmall["mix_norm"][l], gsmall["lbsum"][l], gsmall["hgrn_out_norm"][l] = gm["gm"], gm["lbsum"], gm["gout"]
        for k, src in (("attn_q_norm", "qn"), ("attn_k_norm", "kn")):
            gsmall[k][l] = jnp.sum(gm[src].reshape(ATT_GROUPS, ATT_HEADS, ATT_DH), axis=1)
        dx, gsmall["ffn1_norm"][l], tok = _ffn_bwd(dx, s1, small["ffn1_norm"][l], w1t, "1", tok,
                                                   lambda dwt, dwo, l=l: emit("ffn1", l, dict(w1t=dwt, w1o=dwo), None))
    emit("small", 0, {}, ({k: jnp.stack(v) for k, v in gsmall.items()}, loss))
    return dx


_HBM = pl.BlockSpec(memory_space=pltpu.HBM)
_SEM = pl.BlockSpec(memory_space=pltpu.SEMAPHORE)
_EFFECT = pltpu.SideEffectType.DATAFLOW_SIDE_EFFECTING


def _peer(p):
    x, y, c = lax.axis_index("x"), lax.axis_index("y"), lax.axis_index("c")
    me = 4 * x + 2 * y + c
    return (1 - x if p & 4 else x, 1 - y if p & 2 else y, 1 - c if p & 1 else c), jnp.bitwise_xor(me, p), me


def _xchg_copy(src, land, mode, send_sems, recv_sems, k, p, arriving):
    peer, peer_id, me = _peer(p)
    block = src if mode == "gather" else src.at[peer_id]
    return pltpu.make_async_remote_copy(
        src_ref=block, dst_ref=land.at[peer_id if arriving else me], send_sem=send_sems.at[k * (N_DEV - 1) + p - 1],
        recv_sem=recv_sems.at[k * (N_DEV - 1) + p - 1], device_id=peer, device_id_type=MESH)


def _xchg_start(srcs, modes, groups, name):
    n, ng = len(srcs), len(groups)

    def body(*refs):
        src = refs[:n]
        sems = refs[n:n + 2 * ng]
        land = refs[n + 2 * ng + n:n + 2 * ng + 2 * n]
        token = refs[n + 2 * ng + 2 * n]
        for gi, idx in enumerate(groups):
            for ki, k in enumerate(idx):
                for p in range(1, N_DEV):
                    _xchg_copy(src[k], land[k], modes[k], sems[2 * gi], sems[2 * gi + 1], ki, p, False).start()
        token[...] = jnp.zeros_like(token)

    sem_shapes = []
    for idx in groups:
        sem_shapes += [pltpu.SemaphoreType.DMA((len(idx) * (N_DEV - 1),))] * 2
    outs = pl.pallas_call(
        body,
        out_shape=sem_shapes + [pltpu.HBM(a.shape, a.dtype) for a in srcs]
        + [pltpu.HBM((N_DEV,) + a.shape[-2:], a.dtype) for a in srcs] + [jax.ShapeDtypeStruct((8, 128), F32)],
        in_specs=[_HBM] * n,
        out_specs=[_SEM] * (2 * ng) + [_HBM] * (2 * n) + [pl.BlockSpec(memory_space=pltpu.VMEM)],
        input_output_aliases={i: 2 * ng + i for i in range(n)},
        compiler_params=pltpu.CompilerParams(has_side_effects=_EFFECT),
        name=name,
    )(*[pltpu.with_memory_space_constraint(a, pltpu.HBM) for a in srcs])
    sems = [(outs[2 * gi], outs[2 * gi + 1]) for gi in range(ng)]
    return sems, outs[2 * ng:2 * ng + n], outs[2 * ng + n:2 * ng + 2 * n], outs[-1]


def _xchg_wait_call(srcs, lands, modes, sems, after, name):
    n = len(srcs)

    def body(*refs):
        src, land = refs[:n], refs[n:2 * n]
        send_sems, recv_sems = refs[2 * n], refs[2 * n + 1]
        for p in range(1, N_DEV):
            for k in range(n):
                cp = _xchg_copy(src[k], land[k], modes[k], send_sems, recv_sems, k, p, True)
                cp.wait_send()
                cp.wait_recv()

    outs = pl.pallas_call(
        body,
        out_shape=[pltpu.HBM(a.shape, a.dtype) for a in list(srcs) + list(lands)],
        in_specs=[_HBM] * (2 * n) + [_SEM, _SEM, pl.BlockSpec(memory_space=pl.ANY)],
        out_specs=[_HBM] * (2 * n),
        input_output_aliases={i: i for i in range(2 * n)},
        compiler_params=pltpu.CompilerParams(has_side_effects=_EFFECT),
        name=name,
    )(*srcs, *lands, sems[0], sems[1], after)
    return outs[:n], outs[n:]


def _xchg_wait(srcs, lands, modes, sems, after, name):
    srcs, lands = _xchg_wait_call(srcs, lands, modes, sems, after, name)
    me = 4 * lax.axis_index("x") + 2 * lax.axis_index("y") + lax.axis_index("c")
    done = []
    for a, land, mode in zip(srcs, lands, modes):
        own = a[None] if mode == "gather" else lax.dynamic_slice_in_dim(a, me, 1, axis=0)
        done.append(lax.dynamic_update_slice(land, own, (me, 0, 0)))
    return done


def _sum_slots(land):
    g, _, r, c = land.shape
    br = r // 2 if (r % 32 == 0 and r >= 256) else r

    def body(l_ref, o_ref):
        acc = l_ref[0, 0].astype(F32)
        for j in range(1, N_DEV):
            acc = acc + l_ref[0, j].astype(F32)
        o_ref[0] = acc

    return pl.pallas_call(
        body,
        out_shape=jax.ShapeDtypeStruct((g, r, c), F32),
        grid=(g, r // br),
        in_specs=[pl.BlockSpec((1, N_DEV, br, c), lambda i, j: (i, 0, j, 0))],
        out_specs=pl.BlockSpec((1, br, c), lambda i, j: (i, j, 0)),
        compiler_params=_cparams(("parallel", "parallel")),
        name="sum_slots",
    )(land)


def _adamw(w, g, m, v):
    shape = w.shape
    cols = shape[-1]
    rows = int(np.prod(shape[:-1]))
    bm = max(b for b in range(8, 257, 8) if rows % b == 0) if rows % 8 == 0 else rows
    c1 = 1.0 - ADAM_B1 ** ADAM_STEP
    c2 = 1.0 - ADAM_B2 ** ADAM_STEP

    def fn(ins, consts):
        wv, gv, mv, vv = ins
        m2 = ADAM_B1 * mv + (1.0 - ADAM_B1) * gv
        v2 = ADAM_B2 * vv + (1.0 - ADAM_B2) * (gv * gv)
        delta = -ADAM_LR * ((m2 / c1) / (jnp.sqrt(v2 / c2) + ADAM_EPS) + ADAM_WD * wv)
        return [delta, m2, v2], []

    outs, _ = _rowwise(fn, [(a.reshape(rows, cols), cols, 0) for a in (w, g, m, v)], [], [(cols, F32)] * 3, [],
                       bm=bm, name="adamw")
    return [o.reshape(shape) for o in outs]


BIG = ("w1t", "w1o", "wint", "wa", "wbt", "wo", "w2t", "w2o")
FETCH_GROUPS = dict(w1t=("w1t",), w1o=("w1o",), wint=("wint",), mout=("wa", "wbt", "wo"), w2t=("w2t",), w2o=("w2o",))
SMALL_ROWS = (("ffn1_norm", 0), ("mix_norm", 2), ("lbsum", 4), ("hgrn_out_norm", 6), ("ffn2_norm", 8),
              ("attn_q_norm", 10), ("attn_k_norm", 12))
SMALL_PACK_ROWS = 16


def kernel(x, ffn1_norm, ffn1_w_in, ffn1_w_out, mix_norm, w_in, hgrn_lb_logits, hgrn_out_norm, attn_q_norm, attn_k_norm, w_branch_a, w_branch_b, w_out, ffn2_norm, ffn2_w_in, ffn2_w_out, loss_target, m_ffn1_norm, m_ffn1_w_in, m_ffn1_w_out, m_mix_norm, m_w_in, m_hgrn_lb_logits, m_hgrn_out_norm, m_attn_q_norm, m_attn_k_norm, m_w_branch_a, m_w_branch_b, m_w_out, m_ffn2_norm, m_ffn2_w_in, m_ffn2_w_out, v_ffn1_norm, v_ffn1_w_in, v_ffn1_w_out, v_mix_norm, v_w_in, v_hgrn_lb_logits, v_hgrn_out_norm, v_attn_q_norm, v_attn_k_norm, v_w_branch_a, v_w_branch_b, v_w_out, v_ffn2_norm, v_ffn2_w_in, v_ffn2_w_out):
    names = ("ffn1_norm", "ffn1_w_in", "ffn1_w_out", "mix_norm", "w_in", "hgrn_lb_logits", "hgrn_out_norm", "attn_q_norm",
             "attn_k_norm", "w_branch_a", "w_branch_b", "w_out", "ffn2_norm", "ffn2_w_in", "ffn2_w_out")
    w = dict(zip(names, (ffn1_norm, ffn1_w_in, ffn1_w_out, mix_norm, w_in, hgrn_lb_logits, hgrn_out_norm, attn_q_norm,
                         attn_k_norm, w_branch_a, w_branch_b, w_out, ffn2_norm, ffn2_w_in, ffn2_w_out)))
    m = dict(zip(names, (m_ffn1_norm, m_ffn1_w_in, m_ffn1_w_out, m_mix_norm, m_w_in, m_hgrn_lb_logits, m_hgrn_out_norm,
                         m_attn_q_norm, m_attn_k_norm, m_w_branch_a, m_w_branch_b, m_w_out, m_ffn2_norm, m_ffn2_w_in, m_ffn2_w_out)))
    v = dict(zip(names, (v_ffn1_norm, v_ffn1_w_in, v_ffn1_w_out, v_mix_norm, v_w_in, v_hgrn_lb_logits, v_hgrn_out_norm,
                         v_attn_q_norm, v_attn_k_norm, v_w_branch_a, v_w_branch_b, v_w_out, v_ffn2_norm, v_ffn2_w_in, v_ffn2_w_out)))
    depth, d = ffn1_norm.shape

    def tr(a):
        return jnp.swapaxes(a, 1, 2)

    shard = dict(w1t=tr(ffn1_w_in), w1o=ffn1_w_out, wint=tr(w_in), wa=w_branch_a,
                 wbt=tr(w_branch_b).reshape(depth, -1, d), wo=w_out, w2t=tr(ffn2_w_in), w2o=ffn2_w_out)
    order = [(g, l) for l in range(depth) for g in FETCH_GROUPS]
    flat = [(g, l, k) for g, l in order for k in FETCH_GROUPS[g]]
    groups, pos = [], 0
    for g, l in order:
        groups.append(list(range(pos, pos + len(FETCH_GROUPS[g]))))
        pos += len(FETCH_GROUPS[g])
    g_sems, g_srcs, g_lands, _ = _xchg_start([shard[k][l].astype(BF16) for _, l, k in flat], ["gather"] * len(flat),
                                             groups, "gather_start")

    def fetch(group, l, after):
        gi = order.index((group, l))
        idx = groups[gi]
        lands = _xchg_wait([g_srcs[i] for i in idx], [g_lands[i] for i in idx], ["gather"] * len(idx), g_sems[gi], after,
                           f"gather_wait_{group}{l}")
        out = {}
        for k, land in zip(FETCH_GROUPS[group], lands):
            out[k] = land.reshape(d, -1) if k == "wbt" else land.reshape(-1, d)
        return out

    pending = []

    def emit(group, l, g, final):
        keys = list(g)
        srcs = [g[k].reshape(N_DEV, -1, d) for k in keys]
        modes = ["scatter"] * len(keys)
        if final is not None:
            gsmall, loss = final
            pack = jnp.zeros((SMALL_PACK_ROWS, d), F32)
            for k, r0 in SMALL_ROWS:
                rows = gsmall[k].reshape(depth, -1)
                pack = pack.at[r0:r0 + depth, :rows.shape[1]].set(rows)
            srcs.append(pack.at[14, :].set(loss))
            modes.append("gather")
            keys.append("small")
        sems, s_thru, l_thru, token = _xchg_start(srcs, modes, [list(range(len(srcs)))], f"grads_start_{group}{l}")
        pending.append((group, l, keys, modes, sems[0], s_thru, l_thru))
        return token[0, 0]

    small = {k: w[k] for k in ("ffn1_norm", "mix_norm", "hgrn_lb_logits", "hgrn_out_norm", "attn_q_norm", "attn_k_norm", "ffn2_norm")}
    dx = _local_step(x[0], loss_target[0], small, fetch, emit)

    summed = {}
    for group, l, keys, modes, sems, s_thru, l_thru in pending:
        lands = _xchg_wait(s_thru, l_thru, modes, sems, dx, f"grads_wait_{group}{l}")
        for k, land in zip(keys, lands):
            summed[k, l] = _sum_slots(land[None])[0]
    gsum = {k: jnp.stack([summed[k, l] for l in range(depth)]) for k in BIG}
    tot = summed["small", 0]

    grads = {}
    for k, r0 in SMALL_ROWS:
        shp = (depth,) + (w[k].shape[1:] if k != "lbsum" else (d,))
        grads[k] = tot[r0:r0 + depth, :int(np.prod(shp[1:]))].reshape(shp)
    _, lb_vjp = jax.vjp(_lower_bounds, hgrn_lb_logits)
    grads["hgrn_lb_logits"] = lb_vjp(grads.pop("lbsum"))[0]
    grads["ffn1_w_in"], grads["ffn1_w_out"] = tr(gsum["w1t"]), gsum["w1o"]
    grads["w_in"], grads["w_branch_a"] = tr(gsum["wint"]), gsum["wa"]
    grads["w_branch_b"] = tr(gsum["wbt"].reshape(depth, d // N_DEV, -1))
    grads["w_out"] = gsum["wo"]
    grads["ffn2_w_in"], grads["ffn2_w_out"] = tr(gsum["w2t"]), gsum["w2o"]

    upd = {k: _adamw(w[k], grads[k], m[k], v[k]) for k in names}
    return (tot[14, 0], dx[None], *[grads[k] for k in names], *[upd[k][0] for k in names],
            *[upd[k][1] for k in names], *[upd[k][2] for k in names])
```

```python
import functools
import math

import jax
import jax.numpy as jnp
import numpy as np
from jax import lax
from jax.experimental import pallas as pl
from jax.experimental.pallas import tpu as pltpu

F32 = jnp.float32
BF16 = jnp.bfloat16

N_DEV = 8
EPS = 1e-6
HG_DK = 128
HG_CHUNK = 64
HG_SUB = 16
HG_HP = 8
ATT_PATTERNS = ((128, 1), (512, 4), (2048, 16))
ATT_GROUPS = 3
ATT_HEADS = 4
ATT_DH = 128
ATT_BLK = 128
ROPE_THETA = 10000.0
ADAM_LR, ADAM_B1, ADAM_B2, ADAM_EPS, ADAM_WD, ADAM_STEP = 0.001, 0.9, 0.999, 1e-08, 0.01, 10
VMEM_LIMIT_BYTES = 56 * 1024 * 1024
MXU_COLS = 256
MESH = pl.DeviceIdType.MESH


def _cparams(sem, **kw):
    return pltpu.CompilerParams(dimension_semantics=sem, vmem_limit_bytes=VMEM_LIMIT_BYTES, **kw)


def _sigmoid(x):
    return 1.0 / (1.0 + jnp.exp(-x))


def _mm(a_list, b_list, pairs, n_acc, fin, out_dtypes, *, m, n, k, ta=False, tb=False, bm, bn, bk,
        b_off=None, extras=(), e_off=None, n_outer=False, consts=(), a_pro=None, n_sums=0, chunk=0, name):
    bm, bn, bk = min(bm, m), min(bn, n), min(bk, k)
    assert m % bm == 0 and n % bn == 0 and k % bk == 0, (name, m, n, k, bm, bn, bk)
    nk = k // bk
    assert not (a_pro and (nk > 1 or ta or n_outer)) and not (n_sums and (bn != n or n_outer)), name
    assert not (chunk and (nk > 1 or n_sums or chunk % 128)), name
    b_off = b_off or [(0, 0)] * len(b_list)
    e_off = e_off or [0] * len(extras)
    na, nb, ne, nc, no = len(a_list), len(b_list), len(extras), len(consts), len(out_dtypes)
    nao = na if a_pro else 0
    dn = (((0,) if ta else (1,), (1,) if tb else (0,)), ((), ()))

    def body(*refs):
        refs = list(refs)
        a_refs, b_refs, e_refs, c_refs, o_refs, ao_refs, s_refs = (
            [refs.pop(0) for _ in range(cnt)] for cnt in (na, nb, ne, nc, no, nao, n_sums))
        acc_refs = refs
        kk = pl.program_id(2)
        first = pl.program_id(0) == 0
        cvals = [c[...] for c in c_refs]
        a_vals = [r[...] for r in a_refs]
        if a_pro:
            a_vals = a_pro(a_vals, cvals)
            for r, v in zip(ao_refs, a_vals):
                r[...] = v
        if chunk:
            spans = [slice(lo, min(lo + chunk, bn)) for lo in range(0, bn, chunk)]
            chunks = []
            for cs in spans:
                parts = [None] * n_acc
                for ai, bi, ci in pairs:
                    p = lax.dot_general(a_vals[ai], b_refs[bi][cs, :] if tb else b_refs[bi][:, cs], dn,
                                        preferred_element_type=F32)
                    parts[ci] = p if parts[ci] is None else parts[ci] + p
                chunks.append(parts)
            for cs, parts in zip(spans, chunks):
                ex = [e[:, cs] for e in e_refs]
                outs = fin(parts, ex, cvals) if nc else fin(parts, ex)
                for o_ref, o in zip(o_refs, outs):
                    o_ref[:, cs] = o.astype(o_ref.dtype)
            return

        parts = [None] * n_acc
        for ai, bi, ci in pairs:
            p = lax.dot_general(a_vals[ai], b_refs[bi][...], dn, preferred_element_type=F32)
            parts[ci] = p if parts[ci] is None else parts[ci] + p

        def finish(accs):
            ex = [e[...] for e in e_refs]
            res = fin(accs, ex, cvals) if nc else fin(accs, ex)
            outs, sums = res if n_sums else (res, ())
            for o_ref, o in zip(o_refs, outs):
                o_ref[...] = o.astype(o_ref.dtype)
            if n_sums:
                @pl.when(first)
                def _():
                    for s_ref, s in zip(s_refs, sums):
                        s_ref[...] = s

                @pl.when(jnp.logical_not(first))
                def _():
                    for s_ref, s in zip(s_refs, sums):
                        s_ref[...] += s

        if nk == 1:
            finish(parts)
        else:
            @pl.when(kk == 0)
            def _():
                for c in range(n_acc):
                    acc_refs[c][...] = parts[c]

            @pl.when(kk > 0)
            def _():
                for c in range(n_acc):
                    acc_refs[c][...] += parts[c]

            @pl.when(kk == nk - 1)
            def _():
                finish([acc_refs[c][...] for c in range(n_acc)])

    def ij(f):
        return (lambda j, i, q: f(i, j, q)) if n_outer else f

    a_spec = pl.BlockSpec((bk, bm), ij(lambda i, j, q: (q, i))) if ta else pl.BlockSpec((bm, bk), ij(lambda i, j, q: (i, q)))

    b_mode = dict(pipeline_mode=pl.Buffered(1)) if (bn == n and nk == 1) else {}

    def b_spec(off):
        on, ok = off
        if tb:
            return pl.BlockSpec((bn, bk), ij(lambda i, j, q: (j + on, q + ok)), **b_mode)
        return pl.BlockSpec((bk, bn), ij(lambda i, j, q: (q + ok, j + on)), **b_mode)

    mn_spec = pl.BlockSpec((bm, bn), ij(lambda i, j, q: (i, j)))
    outs = pl.pallas_call(
        body,
        out_shape=[jax.ShapeDtypeStruct((m, n), d) for d in out_dtypes] + [jax.ShapeDtypeStruct((m, k), BF16)] * nao
        + [jax.ShapeDtypeStruct((8, n), F32)] * n_sums,
        grid=(n // bn, m // bm, nk) if n_outer else (m // bm, n // bn, nk),
        in_specs=[a_spec] * na + [b_spec(o) for o in b_off]
        + [pl.BlockSpec((bm, bn), ij(lambda i, j, q, o=o: (i, j + o))) for o in e_off]
        + [pl.BlockSpec(c.shape, lambda *_, nd=c.ndim: (0,) * nd) for c in consts],
        out_specs=[mn_spec] * no + [a_spec] * nao + [pl.BlockSpec((8, n), lambda *_: (0, 0))] * n_sums,
        scratch_shapes=[pltpu.VMEM((bm, bn), F32) for _ in range(n_acc if nk > 1 else 0)],
        compiler_params=_cparams(("arbitrary" if n_sums else "parallel", "parallel", "arbitrary")),
        name=name,
    )(*a_list, *b_list, *extras, *consts)
    return outs


def _first(accs, ex):
    return (accs[0],)


def _rowwise(fn, ins, consts, out_defs, sum_widths, *, bm, name):
    ins = [tuple(e) + (1,) * (4 - len(e)) for e in ins]
    out_defs = [tuple(e) + (1,) * (3 - len(e)) for e in out_defs]
    t = ins[0][0].shape[-2] * ins[0][3]
    bm = min(bm, t)
    assert t % bm == 0, (name, t, bm)
    ni, nc, no, ns = len(ins), len(consts), len(out_defs), len(sum_widths)
    strided = [w for _, w, _, d in ins if d > 1] + [w for w, _, d in out_defs if d > 1]

    def body(*refs):
        i_refs, c_refs = refs[:ni], refs[ni:ni + nc]
        o_refs, s_refs = refs[ni + nc:ni + nc + no], refs[ni + nc + no:ni + nc + no + ns]
        scratch = list(refs[ni + nc + no + ns:])
        vals = []
        for ref, (_, w, _, d) in zip(i_refs, ins):
            if d == 1:
                vals.append(ref[...])
                continue
            s = scratch.pop(0)
            for r in range(d):
                for c in range(w // 128):
                    s.at[c][pl.ds(r, bm // d, stride=d), :] = ref[r, :, c * 128:(c + 1) * 128].astype(F32)
            vals.append(jnp.concatenate([s[c] for c in range(w // 128)], axis=1))
        outs, sums = fn(vals, [r[...] for r in c_refs])
        for o_ref, o, (w, _, d) in zip(o_refs, outs, out_defs):
            if d == 1:
                o_ref[...] = o.astype(o_ref.dtype)
                continue
            s = scratch.pop(0)
            for c in range(w // 128):
                s[c] = o[:, c * 128:(c + 1) * 128].astype(F32)
            for r in range(d):
                for c in range(w // 128):
                    o_ref[r, :, c * 128:(c + 1) * 128] = s.at[c][pl.ds(r, bm // d, stride=d), :].astype(o_ref.dtype)
        if ns:
            first = pl.program_id(0) == 0

            @pl.when(first)
            def _():
                for s_ref, s in zip(s_refs, sums):
                    s_ref[...] = s

            @pl.when(jnp.logical_not(first))
            def _():
                for s_ref, s in zip(s_refs, sums):
                    s_ref[...] += s

    def win(width, cb, d):
        if d > 1:
            return pl.BlockSpec((d, bm // d, width), lambda i: (0, i, 0))
        return pl.BlockSpec((bm, width), lambda i: (i, cb))

    res = pl.pallas_call(
        body,
        out_shape=[jax.ShapeDtypeStruct((t, w) if d == 1 else (d, t // d, w), dt) for w, dt, d in out_defs]
        + [jax.ShapeDtypeStruct((8, w), F32) for w in sum_widths],
        grid=(t // bm,),
        in_specs=[win(w, cb, d) for _, w, cb, d in ins] + [pl.BlockSpec(c.shape, lambda i, nd=c.ndim: (0,) * nd) for c in consts],
        out_specs=[win(w, 0, d) for w, _, d in out_defs] + [pl.BlockSpec((8, w), lambda i: (0, 0)) for w in sum_widths],
        scratch_shapes=[pltpu.VMEM((w // 128, bm, 128), F32) for w in strided],
        compiler_params=_cparams(("arbitrary",) if ns else ("parallel",)),
        name=name,
    )(*[e[0] for e in ins], *consts)
    return res[:no], [jnp.sum(s, axis=0) for s in res[no:]]


def _colsum8(x):
    bm, w = x.shape
    return jnp.sum(x.reshape(bm // 8, 8, w), axis=0)


def _tri(n, upper=False):
    r = lax.broadcasted_iota(jnp.int32, (n, n), 0)
    c = lax.broadcasted_iota(jnp.int32, (n, n), 1)
    return (c >= r) if upper else (c <= r)


def _exact_tri_matmul(tri_bf16, x):
    x0 = x.astype(BF16)
    r1 = x - x0.astype(F32)
    x1 = r1.astype(BF16)
    x2 = (r1 - x1.astype(F32)).astype(BF16)
    w = x.shape[1]
    y = jnp.dot(tri_bf16, jnp.concatenate([x0, x1, x2], axis=1), preferred_element_type=F32)
    return y[:, :w] + y[:, w:2 * w] + y[:, 2 * w:]


def _dot_nt(a, b):
    return lax.dot_general(a, b, (((1,), (1,)), ((), ())), preferred_element_type=F32)


def _dot_tn(a, b):
    return lax.dot_general(a, b, (((0,), (0,)), ((), ())), preferred_element_type=F32)


def _dot(a, b):
    return jnp.dot(a, b, preferred_element_type=F32)


def _hg_gates(hq, hf, lb):
    sq = _sigmoid(hq)
    q = hq * sq
    sg = _sigmoid(hf)
    f = lb + (1.0 - lb) * sg
    return q, sq, sg, f


def _hg_intra(q, kk, g):
    c = q.shape[0]
    rows = lax.broadcasted_iota(jnp.int32, (c, 1), 0)
    a_rows, qts, kts, eqs, eks = [], [], [], [], []
    for i in range(c // HG_SUB):
        lo = i * HG_SUB
        ref = g[lo - 1:lo, :] if i else jnp.zeros_like(g[0:1, :])
        eq = jnp.exp(g[lo:lo + HG_SUB, :] - ref)
        ek = jnp.exp(jnp.where(rows < lo + HG_SUB, ref - g, 0.0))
        qt = q[lo:lo + HG_SUB, :] * eq
        kt = kk * ek
        a = _dot_nt(qt.astype(BF16), kt.astype(BF16))
        tpos = lo + lax.broadcasted_iota(jnp.int32, (HG_SUB, c), 0)
        spos = lax.broadcasted_iota(jnp.int32, (HG_SUB, c), 1)
        a_rows.append(jnp.where(spos <= tpos, a, 0.0))
        qts.append(qt), kts.append(kt), eqs.append(eq), eks.append(ek)
    return jnp.concatenate(a_rows, axis=0), qts, kts, eqs, eks


def _hgrn_fwd_serial(zh, lb3, *, tb=512):
    t = zh.shape[0]
    nh = lb3.shape[0]
    c = HG_CHUNK
    tb = min(tb, t)
    nchunk = tb // c
    hp = HG_HP if nh % HG_HP == 0 else 1

    def body(hq_ref, hf_ref, hi_ref, lb_ref, o_ref, st_ref, state):
        @pl.when(pl.program_id(1) == 0)
        def _():
            state[...] = jnp.zeros_like(state)

        tril = _tri(c).astype(BF16)

        def one_head(hh, ci, sl):
            ls = slice(hh * HG_DK, (hh + 1) * HG_DK)
            q, _, _, f = _hg_gates(hq_ref[sl, ls], hf_ref[sl, ls], lb_ref[hh])
            v = hi_ref[sl, ls]
            kk = 1.0 - f
            g = _exact_tri_matmul(tril, jnp.log(f))
            a, _, _, _, _ = _hg_intra(q, kk, g)
            st = state[hh]
            st_ref[hh, ci] = st
            vb = v.astype(BF16)
            o = _dot(a.astype(BF16), vb) + _dot_nt((q * jnp.exp(g)).astype(BF16), st.astype(BF16))
            o_ref[sl, ls] = o
            glast = g[c - 1:c, :]
            kg = kk * jnp.exp(glast - g)
            state[hh] = st * jnp.exp(glast) + _dot_tn(vb, kg.astype(BF16))

        def chunk(ci, carry):
            sl = pl.ds(pl.multiple_of(ci * c, c), c)
            for hh in range(hp):
                one_head(hh, ci, sl)
            return carry

        lax.fori_loop(0, nchunk, chunk, 0)

    def col(cb):
        return pl.BlockSpec((tb, hp * HG_DK), lambda h, i: (i, cb * (nh // hp) + h))

    return pl.pallas_call(
        body,
        out_shape=[jax.ShapeDtypeStruct((t, nh * HG_DK), F32), jax.ShapeDtypeStruct((nh, t // c, HG_DK, HG_DK), F32)],
        grid=(nh // hp, t // tb),
        in_specs=[col(0), col(1), col(2), pl.BlockSpec((hp, 1, HG_DK), lambda h, i: (h, 0, 0))],
        out_specs=[pl.BlockSpec((tb, hp * HG_DK), lambda h, i: (i, h)),
                   pl.BlockSpec((hp, nchunk, HG_DK, HG_DK), lambda h, i: (h, i, 0, 0))],
        scratch_shapes=[pltpu.VMEM((hp, HG_DK, HG_DK), F32)],
        compiler_params=_cparams(("parallel", "arbitrary")),
        name="hgrn_fwd",
    )(zh, zh, zh, lb3)


def _hgrn_bwd_serial(zh, lb3, states, d_o, *, tb=512):
    t = zh.shape[0]
    nh = lb3.shape[0]
    c = HG_CHUNK
    tb = min(tb, t)
    nchunk = tb // c
    nblk = t // tb
    hp = HG_HP if nh % HG_HP == 0 else 1

    def body(hq_ref, hf_ref, hi_ref, lb_ref, st_ref, do_ref, dq_ref, df_ref, dv_ref, dlb_ref, dstate):
        @pl.when(pl.program_id(1) == 0)
        def _():
            dstate[...] = jnp.zeros_like(dstate)
            dlb_ref[...] = jnp.zeros_like(dlb_ref)

        tril = _tri(c).astype(BF16)
        triu = _tri(c, upper=True).astype(BF16)
        last_row = lax.broadcasted_iota(jnp.int32, (c, 1), 0) == c - 1

        def one_head(hh, ci, sl):
            ls = slice(hh * HG_DK, (hh + 1) * HG_DK)
            lb = lb_ref[hh]
            hq, hf = hq_ref[sl, ls], hf_ref[sl, ls]
            q, sq, sg, f = _hg_gates(hq, hf, lb)
            v = hi_ref[sl, ls]
            kk = 1.0 - f
            g = _exact_tri_matmul(tril, jnp.log(f))
            a, qts, kts, eqs, eks = _hg_intra(q, kk, g)
            st = st_ref[hh, ci]
            dst = dstate[hh]
            do = do_ref[sl, ls]
            dob, vb = do.astype(BF16), v.astype(BF16)
            glast = g[c - 1:c, :]
            eg = jnp.exp(g)
            egl = jnp.exp(glast - g)
            qg = q * eg
            kg = kk * egl
            dv = _dot_tn(a.astype(BF16), dob) + _dot_nt(kg.astype(BF16), dst.astype(BF16))
            da = jnp.where(_tri(c), _dot_nt(dob, vb), 0.0).astype(BF16)
            dq_parts, dgq_parts = [], []
            dk = jnp.zeros_like(kk)
            dgk = jnp.zeros_like(kk)
            for i in range(c // HG_SUB):
                da_i = da[i * HG_SUB:(i + 1) * HG_SUB, :]
                ktb, qtb = kts[i].astype(BF16), qts[i].astype(BF16)
                xi = _dot(da_i, ktb)
                yi = _dot_tn(da_i, qtb)
                dq_parts.append(xi * eqs[i])
                dk = dk + yi * eks[i]
                dgq_parts.append(xi * qtb.astype(F32))
                dgk = dgk + yi * ktb.astype(F32)
            dq_inter = _dot(dob, st.astype(BF16)) * eg
            dq = jnp.concatenate(dq_parts, axis=0) + dq_inter
            dk_state = _dot(vb, dst.astype(BF16)) * egl
            dk = dk + dk_state
            dg = jnp.concatenate(dgq_parts, axis=0) - dgk + q * dq_inter - kk * dk_state
            dgl = jnp.sum(kk * dk_state, axis=0, keepdims=True) + jnp.exp(glast) * jnp.sum(st * dst, axis=0, keepdims=True)
            dg = dg + jnp.where(last_row, dgl, 0.0)
            dlogf = _exact_tri_matmul(triu, dg)
            dfv = dlogf / f - dk
            dq_ref[sl, ls] = (dq * (sq * (1.0 + hq * (1.0 - sq)))).astype(dq_ref.dtype)
            df_ref[sl, ls] = (dfv * (1.0 - lb) * sg * (1.0 - sg)).astype(df_ref.dtype)
            dv_ref[sl, ls] = dv.astype(dv_ref.dtype)
            dlb_ref[hh] += jnp.sum(dfv * (1.0 - sg), axis=0, keepdims=True)
            dstate[hh] = dst * jnp.exp(glast) + _dot_tn(dob, qg.astype(BF16))

        def chunk(j, carry):
            ci = nchunk - 1 - j
            sl = pl.ds(pl.multiple_of(ci * c, c), c)
            for hh in range(hp):
                one_head(hh, ci, sl)
            return carry

        lax.fori_loop(0, nchunk, chunk, 0)

    def col(cb):
        return pl.BlockSpec((tb, hp * HG_DK), lambda h, i: (nblk - 1 - i, cb * (nh // hp) + h))

    ocol = pl.BlockSpec((tb, hp * HG_DK), lambda h, i: (nblk - 1 - i, h))
    w = nh * HG_DK
    dq, df, dv, dlb = pl.pallas_call(
        body,
        out_shape=[jax.ShapeDtypeStruct((t, w), BF16)] * 3 + [jax.ShapeDtypeStruct((nh, 1, HG_DK), F32)],
        grid=(nh // hp, nblk),
        in_specs=[col(0), col(1), col(2), pl.BlockSpec((hp, 1, HG_DK), lambda h, i: (h, 0, 0)),
                  pl.BlockSpec((hp, nchunk, HG_DK, HG_DK), lambda h, i: (h, nblk - 1 - i, 0, 0)), ocol],
        out_specs=[ocol, ocol, ocol, pl.BlockSpec((hp, 1, HG_DK), lambda h, i: (h, 0, 0))],
        scratch_shapes=[pltpu.VMEM((hp, HG_DK, HG_DK), F32)],
        compiler_params=_cparams(("parallel", "arbitrary")),
        name="hgrn_bwd",
    )(zh, zh, zh, lb3, states, d_o)
    return dq, df, dv, dlb.reshape(w)


def _hg_heads(x, hp):
    return [x[:, h * HG_DK:(h + 1) * HG_DK] for h in range(hp)]


def _hg_intra_wide(q, kk, g, hp):
    c = q.shape[0]
    rows = lax.broadcasted_iota(jnp.int32, (c, 1), 0)
    a_rows = [[] for _ in range(hp)]
    qts, kts, eqs, eks = [], [], [], []
    for i in range(c // HG_SUB):
        lo = i * HG_SUB
        ref = g[lo - 1:lo, :] if i else jnp.zeros_like(g[0:1, :])
        eq = jnp.exp(g[lo:lo + HG_SUB, :] - ref)
        ek = jnp.exp(jnp.where(rows < lo + HG_SUB, ref - g, 0.0))
        qtb = (q[lo:lo + HG_SUB, :] * eq).astype(BF16)
        ktb = (kk * ek).astype(BF16)
        tpos = lo + lax.broadcasted_iota(jnp.int32, (HG_SUB, c), 0)
        spos = lax.broadcasted_iota(jnp.int32, (HG_SUB, c), 1)
        for h, (qh, kh) in enumerate(zip(_hg_heads(qtb, hp), _hg_heads(ktb, hp))):
            a_rows[h].append(jnp.where(spos <= tpos, _dot_nt(qh, kh), 0.0))
        qts.append(qtb), kts.append(ktb), eqs.append(eq), eks.append(ek)
    return [jnp.concatenate(r, axis=0) for r in a_rows], qts, kts, eqs, eks


def _hgrn_fwd(zh, lb3, *, tb=512):
    t = zh.shape[0]
    nh = lb3.shape[0]
    c = HG_CHUNK
    tb = min(tb, t)
    nchunk = tb // c
    hp = HG_HP if nh % HG_HP == 0 else 1
    wp = hp * HG_DK

    def body(hq_ref, hf_ref, hi_ref, lb_ref, o_ref, st_ref, state):
        @pl.when(pl.program_id(1) == 0)
        def _():
            state[...] = jnp.zeros_like(state)

        tril = _tri(c).astype(BF16)

        def chunk(ci, carry):
            sl = pl.ds(pl.multiple_of(ci * c, c), c)
            q, _, _, f = _hg_gates(hq_ref[sl, :], hf_ref[sl, :], lb_ref[...])
            kk = 1.0 - f
            g = _exact_tri_matmul(tril, jnp.log(f))
            a, _, _, _, _ = _hg_intra_wide(q, kk, g, hp)
            vb = hi_ref[sl, :].astype(BF16)
            glast = g[c - 1:c, :]
            qgb = (q * jnp.exp(g)).astype(BF16)
            kgb = (kk * jnp.exp(glast - g)).astype(BF16)
            dec = jnp.exp(glast)
            sts = [state[h] for h in range(hp)]
            for h in range(hp):
                st_ref[h, ci] = sts[h]
            vh, qgh, kgh, dech = _hg_heads(vb, hp), _hg_heads(qgb, hp), _hg_heads(kgb, hp), _hg_heads(dec, hp)
            o = [_dot(a[h].astype(BF16), vh[h]) + _dot_nt(qgh[h], sts[h].astype(BF16)) for h in range(hp)]
            new = [_dot_tn(vh[h], kgh[h]) for h in range(hp)]
            o_ref[sl, :] = jnp.concatenate(o, axis=1)
            for h in range(hp):
                state[h] = sts[h] * dech[h] + new[h]
            return carry

        lax.fori_loop(0, nchunk, chunk, 0)

    def col(cb):
        return pl.BlockSpec((tb, wp), lambda h, i: (i, cb * (nh // hp) + h))

    return pl.pallas_call(
        body,
        out_shape=[jax.ShapeDtypeStruct((t, nh * HG_DK), F32), jax.ShapeDtypeStruct((nh, t // c, HG_DK, HG_DK), F32)],
        grid=(nh // hp, t // tb),
        in_specs=[col(0), col(1), col(2), pl.BlockSpec((1, wp), lambda h, i: (0, h))],
        out_specs=[pl.BlockSpec((tb, wp), lambda h, i: (i, h)),
                   pl.BlockSpec((hp, nchunk, HG_DK, HG_DK), lambda h, i: (h, i, 0, 0))],
        scratch_shapes=[pltpu.VMEM((hp, HG_DK, HG_DK), F32)],
        compiler_params=_cparams(("parallel", "arbitrary")),
        name="hgrn_fwd",
    )(zh, zh, zh, lb3.reshape(1, -1))


def _hgrn_bwd(zh, lb3, states, d_o, *, tb=512):
    t = zh.shape[0]
    nh = lb3.shape[0]
    c = HG_CHUNK
    tb = min(tb, t)
    nchunk = tb // c
    nblk = t // tb
    hp = HG_HP if nh % HG_HP == 0 else 1
    wp = hp * HG_DK

    def body(hq_ref, hf_ref, hi_ref, lb_ref, st_ref, do_ref, dq_ref, df_ref, dv_ref, dlb_ref, dstate):
        @pl.when(pl.program_id(1) == 0)
        def _():
            dstate[...] = jnp.zeros_like(dstate)
            dlb_ref[...] = jnp.zeros_like(dlb_ref)

        tril = _tri(c).astype(BF16)
        triu = _tri(c, upper=True).astype(BF16)
        last_row = lax.broadcasted_iota(jnp.int32, (c, 1), 0) == c - 1
        heads = range(hp)

        def chunk(j, carry):
            ci = nchunk - 1 - j
            sl = pl.ds(pl.multiple_of(ci * c, c), c)
            lb = lb_ref[...]
            hq, hf = hq_ref[sl, :], hf_ref[sl, :]
            q, sq, sg, f = _hg_gates(hq, hf, lb)
            kk = 1.0 - f
            g = _exact_tri_matmul(tril, jnp.log(f))
            a, qts, kts, eqs, eks = _hg_intra_wide(q, kk, g, hp)
            glast = g[c - 1:c, :]
            eg, egl, dec = jnp.exp(g), jnp.exp(glast - g), jnp.exp(glast)
            vb, dob = hi_ref[sl, :].astype(BF16), do_ref[sl, :].astype(BF16)
            qgb, kgb = (q * eg).astype(BF16), (kk * egl).astype(BF16)
            sts = [st_ref[h, ci] for h in heads]
            dsts = [dstate[h] for h in heads]
            stb, dstb = [s.astype(BF16) for s in sts], [s.astype(BF16) for s in dsts]
            vh, doh, qgh, kgh = _hg_heads(vb, hp), _hg_heads(dob, hp), _hg_heads(qgb, hp), _hg_heads(kgb, hp)
            dv = [_dot_tn(a[h].astype(BF16), doh[h]) + _dot_nt(kgh[h], dstb[h]) for h in heads]
            da = [jnp.where(_tri(c), _dot_nt(doh[h], vh[h]), 0.0).astype(BF16) for h in heads]
            dq_inter = jnp.concatenate([_dot(doh[h], stb[h]) for h in heads], axis=1) * eg
            dk_state = jnp.concatenate([_dot(vh[h], dstb[h]) for h in heads], axis=1) * egl
            new_dst = [_dot_tn(doh[h], qgh[h]) for h in heads]
            xs, dk, dgk = [], dk_state, 0.0
            for i in range(c // HG_SUB):
                rs = slice(i * HG_SUB, (i + 1) * HG_SUB)
                kth, qth = _hg_heads(kts[i], hp), _hg_heads(qts[i], hp)
                xi = jnp.concatenate([_dot(da[h][rs, :], kth[h]) for h in heads], axis=1)
                yi = jnp.concatenate([_dot_tn(da[h][rs, :], qth[h]) for h in heads], axis=1)
                xs.append(xi)
                dk = dk + yi * eks[i]
                dgk = dgk + yi * kts[i].astype(F32)
            dq = jnp.concatenate([x * e for x, e in zip(xs, eqs)], axis=0) + dq_inter
            dgq = jnp.concatenate([x * qt.astype(F32) for x, qt in zip(xs, qts)], axis=0)
            dg = dgq - dgk + q * dq_inter - kk * dk_state
            sdot = jnp.concatenate([jnp.sum(sts[h] * dsts[h], axis=0, keepdims=True) for h in heads], axis=1)
            dgl = jnp.sum(kk * dk_state, axis=0, keepdims=True) + dec * sdot
            dg = dg + jnp.where(last_row, dgl, 0.0)
            dlogf = _exact_tri_matmul(triu, dg)
            dfv = dlogf / f - dk
            dq_ref[sl, :] = (dq * (sq * (1.0 + hq * (1.0 - sq)))).astype(dq_ref.dtype)
            df_ref[sl, :] = (dfv * (1.0 - lb) * sg * (1.0 - sg)).astype(df_ref.dtype)
            dv_ref[sl, :] = jnp.concatenate(dv, axis=1).astype(dv_ref.dtype)
            dlb_ref[...] += jnp.sum(dfv * (1.0 - sg), axis=0, keepdims=True)
            dech = _hg_heads(dec, hp)
            for h in heads:
                dstate[h] = dsts[h] * dech[h] + new_dst[h]
            return carry

        lax.fori_loop(0, nchunk, chunk, 0)

    def col(cb):
        return pl.BlockSpec((tb, wp), lambda h, i: (nblk - 1 - i, cb * (nh // hp) + h))

    ocol = pl.BlockSpec((tb, wp), lambda h, i: (nblk - 1 - i, h))
    lbspec = pl.BlockSpec((1, wp), lambda h, i: (0, h))
    w = nh * HG_DK
    dq, df, dv, dlb = pl.pallas_call(
        body,
        out_shape=[jax.ShapeDtypeStruct((t, w), BF16)] * 3 + [jax.ShapeDtypeStruct((1, w), F32)],
        grid=(nh // hp, nblk),
        in_specs=[col(0), col(1), col(2), lbspec,
                  pl.BlockSpec((hp, nchunk, HG_DK, HG_DK), lambda h, i: (h, nblk - 1 - i, 0, 0)), ocol],
        out_specs=[ocol, ocol, ocol, lbspec],
        scratch_shapes=[pltpu.VMEM((hp, HG_DK, HG_DK), F32)],
        compiler_params=_cparams(("parallel", "arbitrary")),
        name="hgrn_bwd",
    )(zh, zh, zh, lb3.reshape(1, -1), states, d_o)
    return dq, df, dv, dlb.reshape(w)


NEG = -1e30
ATT_GW = ATT_HEADS * ATT_DH


def _att_scores(q, kp, kc, has_prev):
    scale = ATT_DH ** -0.5
    i = lax.broadcasted_iota(jnp.int32, (ATT_BLK, ATT_BLK), 0)
    j = lax.broadcasted_iota(jnp.int32, (ATT_BLK, ATT_BLK), 1)
    s_p = jnp.where(jnp.logical_and(j >= i, has_prev), _dot_nt(q, kp) * scale, NEG)
    s_c = jnp.where(j <= i, _dot_nt(q, kc) * scale, NEG)
    return s_p, s_c


def _att_views(arrs, d):
    return [a.reshape(d, -1, ATT_GW) for a in arrs]


def _att_unview(a, d):
    return a.reshape(-1, ATT_GW) if d == 1 else a


ATT_QB = 4


def _attn_fwd(qb, kb, vb, g):
    d = ATT_PATTERNS[g][1]
    q2, k2, v2 = _att_views([qb, kb, vb], d)
    nblk = q2.shape[1] // ATT_BLK
    nq = ATT_QB if nblk % ATT_QB == 0 else 1
    rows = nq * ATT_BLK

    def body(q_ref, kc_ref, kp_ref, vc_ref, vp_ref, o_ref, l_ref):
        first = pl.program_id(1) == 0
        hss = [slice(h * ATT_DH, (h + 1) * ATT_DH) for h in range(ATT_HEADS)]
        for b in range(nq):
            rs = slice(b * ATT_BLK, (b + 1) * ATT_BLK)
            ps = slice((b - 1) * ATT_BLK, b * ATT_BLK)
            has_prev = jnp.logical_not(first) if b == 0 else True
            kv = [(kp_ref[:, hs], vp_ref[:, hs]) if b == 0 else (kc_ref[ps, hs], vc_ref[ps, hs]) for hs in hss]
            sc = [_att_scores(q_ref[rs, hs], kv[h][0], kc_ref[rs, hs], has_prev) for h, hs in enumerate(hss)]
            ms = [jnp.maximum(jnp.max(s_p, axis=1, keepdims=True), jnp.max(s_c, axis=1, keepdims=True)) for s_p, s_c in sc]
            ps_ = [(jnp.exp(s_p - m), jnp.exp(s_c - m)) for (s_p, s_c), m in zip(sc, ms)]
            ls = [jnp.sum(p_p, axis=1, keepdims=True) + jnp.sum(p_c, axis=1, keepdims=True) for p_p, p_c in ps_]
            os_ = [_dot(p_p.astype(BF16), kv[h][1]) + _dot(p_c.astype(BF16), vc_ref[rs, hss[h]]) for h, (p_p, p_c) in enumerate(ps_)]
            for h, hs in enumerate(hss):
                o_ref[rs, hs] = os_[h] / ls[h]
                l_ref[rs, hs] = jnp.broadcast_to(ms[h] + jnp.log(ls[h]), (ATT_BLK, ATT_DH))

    cur = pl.BlockSpec((None, rows, ATT_GW), lambda r, n: (r, n, 0))
    prev = pl.BlockSpec((None, ATT_BLK, ATT_GW), lambda r, n: (r, jnp.maximum(n * nq - 1, 0), 0))
    o, lse = pl.pallas_call(
        body,
        out_shape=[jax.ShapeDtypeStruct(q2.shape, F32)] * 2,
        grid=(d, nblk // nq),
        in_specs=[cur, cur, prev, cur, prev],
        out_specs=[cur, cur],
        compiler_params=_cparams(("parallel", "arbitrary")),
        name=f"attn_fwd_g{g}",
    )(q2, k2, k2, v2, v2)
    return _att_unview(o, d), _att_unview(lse, d)


def _attn_bwd(qb, kb, vb, o, lse, d_o, d_lse, g):
    d = ATT_PATTERNS[g][1]
    q2, k2, v2 = _att_views([qb, kb, vb], d)
    o2, l2, do2, dl2 = _att_views([o, lse, d_o, d_lse], d)
    nblk = q2.shape[1] // ATT_BLK
    nq = ATT_QB if nblk % ATT_QB == 0 else 1
    rows = nq * ATT_BLK
    ns = nblk // nq
    scale = ATT_DH ** -0.5

    def body(q_ref, kc_ref, kp_ref, vc_ref, vp_ref, o_ref, l_ref, do_ref, dl_ref, dq_ref, dk_ref, dv_ref, ck, cv):
        n = pl.program_id(1)

        @pl.when(n == 0)
        def _():
            ck[...] = jnp.zeros_like(ck)
            cv[...] = jnp.zeros_like(cv)

        first = n == ns - 1
        hss = [slice(h * ATT_DH, (h + 1) * ATT_DH) for h in range(ATT_HEADS)]
        heads = range(ATT_HEADS)
        pend_k, pend_v = [ck[:, hs] for hs in hss], [cv[:, hs] for hs in hss]
        for b in reversed(range(nq)):
            rs = slice(b * ATT_BLK, (b + 1) * ATT_BLK)
            ps = slice((b - 1) * ATT_BLK, b * ATT_BLK)
            has_prev = jnp.logical_not(first) if b == 0 else True
            q = [q_ref[rs, hs] for hs in hss]
            kc, vc = [kc_ref[rs, hs] for hs in hss], [vc_ref[rs, hs] for hs in hss]
            kp = [kp_ref[:, hs] if b == 0 else kc_ref[ps, hs] for hs in hss]
            vp = [vp_ref[:, hs] if b == 0 else vc_ref[ps, hs] for hs in hss]
            sc = [_att_scores(q[h], kp[h], kc[h], has_prev) for h in heads]
            dob = [do_ref[rs, hs].astype(BF16) for hs in hss]
            dp = [(_dot_nt(dob[h], vp[h]), _dot_nt(dob[h], vc[h])) for h in heads]
            delta = [jnp.sum(do_ref[rs, hs] * o_ref[rs, hs] - dl_ref[rs, hs], axis=1, keepdims=True) for hs in hss]
            pr = [(jnp.exp(sc[h][0] - l_ref[rs, hss[h]][:, 0:1]), jnp.exp(sc[h][1] - l_ref[rs, hss[h]][:, 0:1])) for h in heads]
            ds = [((pr[h][0] * (dp[h][0] - delta[h]) * scale).astype(BF16), (pr[h][1] * (dp[h][1] - delta[h]) * scale).astype(BF16))
                  for h in heads]
            pb = [(pr[h][0].astype(BF16), pr[h][1].astype(BF16)) for h in heads]
            dq = [_dot(ds[h][0], kp[h]) + _dot(ds[h][1], kc[h]) for h in heads]
            dk_c = [_dot_tn(ds[h][1], q[h]) for h in heads]
            dv_c = [_dot_tn(pb[h][1], dob[h]) for h in heads]
            dk_p = [_dot_tn(ds[h][0], q[h]) for h in heads]
            dv_p = [_dot_tn(pb[h][0], dob[h]) for h in heads]
            for h, hs in enumerate(hss):
                dq_ref[rs, hs] = dq[h]
                dk_ref[rs, hs] = pend_k[h] + dk_c[h]
                dv_ref[rs, hs] = pend_v[h] + dv_c[h]
            pend_k, pend_v = dk_p, dv_p
        for h, hs in enumerate(hss):
            ck[:, hs] = pend_k[h]
            cv[:, hs] = pend_v[h]

    cur = pl.BlockSpec((None, rows, ATT_GW), lambda r, n: (r, ns - 1 - n, 0))
    prev = pl.BlockSpec((None, ATT_BLK, ATT_GW), lambda r, n: (r, jnp.maximum((ns - 1 - n) * nq - 1, 0), 0))
    shp = jax.ShapeDtypeStruct(q2.shape, F32)
    dq, dk, dv = pl.pallas_call(
        body,
        out_shape=[shp, shp, shp],
        grid=(d, ns),
        in_specs=[cur, cur, prev, cur, prev, cur, cur, cur, cur],
        out_specs=[cur, cur, cur],
        scratch_shapes=[pltpu.VMEM((ATT_BLK, ATT_GW), F32), pltpu.VMEM((ATT_BLK, ATT_GW), F32)],
        compiler_params=_cparams(("parallel", "arbitrary")),
        name=f"attn_bwd_g{g}",
    )(q2, k2, k2, v2, v2, o2, l2, do2, dl2)
    return _att_unview(dq, d), _att_unview(dk, d), _att_unview(dv, d)


def _attn_fwd_1blk(qb, kb, vb, g):
    d = ATT_PATTERNS[g][1]
    q2, k2, v2 = _att_views([qb, kb, vb], d)
    nb = q2.shape[1] // ATT_BLK

    def body(q_ref, kc_ref, kp_ref, vc_ref, vp_ref, o_ref, l_ref):
        has_prev = pl.program_id(1) > 0
        for h in range(ATT_HEADS):
            hs = slice(h * ATT_DH, (h + 1) * ATT_DH)
            s_p, s_c = _att_scores(q_ref[:, hs], kp_ref[:, hs], kc_ref[:, hs], has_prev)
            m = jnp.maximum(jnp.max(s_p, axis=1, keepdims=True), jnp.max(s_c, axis=1, keepdims=True))
            p_p, p_c = jnp.exp(s_p - m), jnp.exp(s_c - m)
            l = jnp.sum(p_p, axis=1, keepdims=True) + jnp.sum(p_c, axis=1, keepdims=True)
            o = _dot(p_p.astype(BF16), vp_ref[:, hs]) + _dot(p_c.astype(BF16), vc_ref[:, hs])
            o_ref[:, hs] = o / l
            l_ref[:, hs] = jnp.broadcast_to(m + jnp.log(l), (ATT_BLK, ATT_DH))

    cur = pl.BlockSpec((None, ATT_BLK, ATT_GW), lambda r, n: (r, n, 0))
    prev = pl.BlockSpec((None, ATT_BLK, ATT_GW), lambda r, n: (r, jnp.maximum(n - 1, 0), 0))
    o, lse = pl.pallas_call(
        body,
        out_shape=[jax.ShapeDtypeStruct(q2.shape, F32)] * 2,
        grid=(d, nb),
        in_specs=[cur, cur, prev, cur, prev],
        out_specs=[cur, cur],
        compiler_params=_cparams(("parallel", "arbitrary")),
        name=f"attn_fwd_g{g}",
    )(q2, k2, k2, v2, v2)
    return _att_unview(o, d), _att_unview(lse, d)


def _attn_bwd_1blk(qb, kb, vb, o, lse, d_o, d_lse, g):
    d = ATT_PATTERNS[g][1]
    q2, k2, v2 = _att_views([qb, kb, vb], d)
    o2, l2, do2, dl2 = _att_views([o, lse, d_o, d_lse], d)
    nb = q2.shape[1] // ATT_BLK

    def body(q_ref, kc_ref, kp_ref, vc_ref, vp_ref, o_ref, l_ref, do_ref, dl_ref, dq_ref, dk_ref, dv_ref, ck, cv):
        n = pl.program_id(1)
        active = n < nb

        @pl.when(n == 0)
        def _():
            ck[...] = jnp.zeros_like(ck)
            cv[...] = jnp.zeros_like(cv)

        @pl.when(jnp.logical_not(active))
        def _():
            dk_ref[...] = ck[...]
            dv_ref[...] = cv[...]

        @pl.when(active)
        def _():
            has_prev = n > 0
            for h in range(ATT_HEADS):
                hs = slice(h * ATT_DH, (h + 1) * ATT_DH)
                q, kp, kc, vp, vc = q_ref[:, hs], kp_ref[:, hs], kc_ref[:, hs], vp_ref[:, hs], vc_ref[:, hs]
                s_p, s_c = _att_scores(q, kp, kc, has_prev)
                lse_h = l_ref[:, hs][:, 0:1]
                p_p, p_c = jnp.exp(s_p - lse_h), jnp.exp(s_c - lse_h)
                do = do_ref[:, hs]
                delta = jnp.sum(do * o_ref[:, hs] - dl_ref[:, hs], axis=1, keepdims=True)
                dob = do.astype(BF16)
                scale = ATT_DH ** -0.5
                ds_p = (p_p * (_dot_nt(dob, vp) - delta) * scale).astype(BF16)
                ds_c = (p_c * (_dot_nt(dob, vc) - delta) * scale).astype(BF16)
                dq_ref[:, hs] = _dot(ds_p, kp) + _dot(ds_c, kc)
                dk_ref[:, hs] = ck[:, hs] + _dot_tn(ds_p, q)
                dv_ref[:, hs] = cv[:, hs] + _dot_tn(p_p.astype(BF16), dob)
                ck[:, hs] = _dot_tn(ds_c, q)
                cv[:, hs] = _dot_tn(p_c.astype(BF16), dob)

    def qn(n):
        return jnp.minimum(n, nb - 1)

    cur = pl.BlockSpec((None, ATT_BLK, ATT_GW), lambda r, n: (r, qn(n), 0))
    prev = pl.BlockSpec((None, ATT_BLK, ATT_GW), lambda r, n: (r, jnp.maximum(qn(n) - 1, 0), 0))
    behind = pl.BlockSpec((None, ATT_BLK, ATT_GW), lambda r, n: (r, jnp.maximum(n - 1, 0), 0))
    shp = jax.ShapeDtypeStruct(q2.shape, F32)
    dq, dk, dv = pl.pallas_call(
        body,
        out_shape=[shp, shp, shp],
        grid=(d, nb + 1),
        in_specs=[cur, cur, prev, cur, prev, cur, cur, cur, cur],
        out_specs=[cur, behind, behind],
        scratch_shapes=[pltpu.VMEM((ATT_BLK, ATT_GW), F32), pltpu.VMEM((ATT_BLK, ATT_GW), F32)],
        compiler_params=_cparams(("parallel", "arbitrary")),
        name=f"attn_bwd_g{g}",
    )(q2, k2, k2, v2, v2, o2, l2, do2, dl2)
    return _att_unview(dq, d), _att_unview(dk, d), _att_unview(dv, d)


def _rms_parts(x, width):
    outs = []
    for lo in range(0, x.shape[1], width):
        xs = x[:, lo:lo + width].astype(F32)
        r = lax.rsqrt(jnp.mean(xs * xs, axis=1, keepdims=True) + EPS)
        outs.append((xs * r, r))
    return outs


def _rms_bwd_part(xh, r, dxh):
    return r * (dxh - xh * jnp.mean(dxh * xh, axis=1, keepdims=True))


def _norm_pro(a, consts):
    (xh, _), = _rms_parts(a[0], a[0].shape[1])
    return [(xh * consts[0]).astype(BF16)]


def _norm_bwd_fin(accs, ex, consts):
    xv, dres = ex
    (xh, r), = _rms_parts(xv, xv.shape[1])
    dx = dres + _rms_bwd_part(xh, r, accs[0] * consts[0])
    return [dx, dx], [_colsum8(accs[0] * xh)]


def _norm_fwd(x, gain):
    d = x.shape[1]

    def fn(ins, consts):
        (xh, _), = _rms_parts(ins[0], d)
        return [xh * consts[0]], []

    (h,), _ = _rowwise(fn, [(x, d, 0)], [gain.reshape(1, d)], [(d, BF16)], [], bm=512, name="norm_fwd")
    return h


def _norm_bwd(x, gain, dh, dres):
    d = x.shape[1]

    def fn(ins, consts):
        (xh, r), = _rms_parts(ins[0], d)
        dx = ins[2] + _rms_bwd_part(xh, r, ins[1] * consts[0])
        return [dx], [_colsum8(ins[1] * xh)]

    (dx,), (dg,) = _rowwise(fn, [(x, d, 0), (dh, d, 0), (dres, d, 0)], [gain.reshape(1, d)], [(d, F32)], [d],
                            bm=512, name="norm_bwd")
    return dx, dg


def _rot_sign():
    lane = lax.broadcasted_iota(jnp.int32, (1, ATT_DH), 1)
    return jnp.where(lane < ATT_DH // 2, -1.0, 1.0).astype(F32)


def _rope(y, cos, sin):
    return y * cos + pltpu.roll(y, ATT_DH // 2, axis=1) * _rot_sign() * sin


def _rope_t(dy, cos, sin):
    return dy * cos - pltpu.roll(dy * sin, ATT_DH // 2, axis=1) * _rot_sign()


def _qk_prep(zq, zk, zv, qn, kn, cos, sin):
    w = zq.shape[1]

    def fn(ins, consts):
        cs, sn = ins[3], ins[4]
        outs = []
        for z, gain in ((ins[0], consts[0]), (ins[1], consts[1])):
            for i, (xh, _) in enumerate(_rms_parts(z, ATT_DH)):
                outs.append(_rope(xh * gain[:, i * ATT_DH:(i + 1) * ATT_DH], cs, sn))
        outs += [ins[2][:, i * ATT_DH:(i + 1) * ATT_DH] for i in range(w // ATT_DH)]
        groups = [jnp.concatenate(outs[i:i + ATT_HEADS], axis=1) for i in range(0, len(outs), ATT_HEADS)]
        return groups, []

    outs, _ = _rowwise(fn, [(zq, w, 0), (zk, w, 0), (zv, w, 0), (cos, ATT_DH, 0), (sin, ATT_DH, 0)], [qn, kn],
                       [(ATT_GW, BF16, ATT_PATTERNS[g][1]) for g in range(ATT_GROUPS)] * 3, [], bm=256, name="qk_prep")
    return outs[0:3], outs[3:6], outs[6:9]


def _qk_prep_bwd(zq, zk, dq_g, dk_g, dv_g, qn, kn, cos, sin):
    w = zq.shape[1]

    def fn(ins, consts):
        cs, sn = ins[2], ins[3]
        outs, sums = [], []
        for z, gain, dparts in ((ins[0], consts[0], ins[4:7]), (ins[1], consts[1], ins[7:10])):
            dout = jnp.concatenate(dparts, axis=1)
            dz, dgain = [], []
            for i, (xh, r) in enumerate(_rms_parts(z, ATT_DH)):
                hs = slice(i * ATT_DH, (i + 1) * ATT_DH)
                dy = _rope_t(dout[:, hs], cs, sn)
                dgain.append(_colsum8(dy * xh))
                dz.append(_rms_bwd_part(xh, r, dy * gain[:, hs]))
            outs.append(jnp.concatenate(dz, axis=1))
            sums.append(jnp.concatenate(dgain, axis=1))
        outs.append(jnp.concatenate(ins[10:13], axis=1))
        return outs, sums

    ins = [(zq, w, 0), (zk, w, 0), (cos, ATT_DH, 0), (sin, ATT_DH, 0)]
    for parts in (dq_g, dk_g, dv_g):
        ins += [(a, ATT_GW, 0, ATT_PATTERNS[g][1]) for g, a in enumerate(parts)]
    (dzq, dzk, dzv), (dqn, dkn) = _rowwise(fn, ins, [qn, kn], [(w, BF16)] * 3, [w, w], bm=256, name="qk_prep_bwd")
    return dzq, dzk, dzv, dqn, dkn


def _post_a(o_raw, zh, gout):
    w = o_raw.shape[1]

    def fn(ins, consts):
        oh = jnp.concatenate([xh for xh, _ in _rms_parts(ins[0], HG_DK)], axis=1)
        hg = ins[1]
        return [oh * consts[0] * (hg * _sigmoid(hg))], []

    (y,), _ = _rowwise(fn, [(o_raw, w, 0), (zh, w, 3)], [gout.reshape(1, w)], [(w, BF16)], [], bm=512, name="post_a")
    return y


def _post_a_bwd(o_raw, zh, gout, dy):
    w = o_raw.shape[1]

    def fn(ins, consts):
        parts = _rms_parts(ins[0], HG_DK)
        oh = jnp.concatenate([xh for xh, _ in parts], axis=1)
        hg, dyv, gain = ins[1], ins[2], consts[0]
        sg = _sigmoid(hg)
        s = hg * sg
        doh = dyv * gain * s
        do = jnp.concatenate([_rms_bwd_part(xh, r, doh[:, i * HG_DK:(i + 1) * HG_DK]) for i, (xh, r) in enumerate(parts)], axis=1)
        dhg = dyv * oh * gain * (sg * (1.0 + hg * (1.0 - sg)))
        return [do, dhg], [_colsum8(dyv * oh * s)]

    (do, dhg), (dgain,) = _rowwise(fn, [(o_raw, w, 0), (zh, w, 3), (dy, w, 0)], [gout.reshape(1, w)],
                                   [(w, F32), (w, BF16)], [w], bm=512, name="post_a_bwd")
    return do, dhg, dgain


def _merge_alpha(lses):
    m = jnp.maximum(jnp.maximum(lses[0], lses[1]), lses[2])
    e = [jnp.exp(l - m) for l in lses]
    inv = 1.0 / (e[0] + e[1] + e[2])
    return [x * inv for x in e]


def _group_ins(parts):
    return [(a, ATT_GW, 0, ATT_PATTERNS[g][1]) for g, a in enumerate(parts)]


def _merge_b(o_g, lse_g):
    def fn(ins, consts):
        al = _merge_alpha(ins[3:6])
        return [al[0] * ins[0] + al[1] * ins[1] + al[2] * ins[2]], []

    (y,), _ = _rowwise(fn, _group_ins(o_g) + _group_ins(lse_g), [], [(ATT_GW, BF16)], [], bm=512, name="merge_b")
    return y


def _merge_b_bwd(o_g, lse_g, dy):
    def fn(ins, consts):
        al = _merge_alpha(ins[3:6])
        dyv = ins[6]
        dal = [dyv * ins[i] for i in range(3)]
        tot = al[0] * dal[0] + al[1] * dal[1] + al[2] * dal[2]
        return [al[i] * dyv for i in range(3)] + [al[i] * (dal[i] - tot) for i in range(3)], []

    outs, _ = _rowwise(fn, _group_ins(o_g) + _group_ins(lse_g) + [(dy, ATT_GW, 0)], [],
                       [(ATT_GW, F32, ATT_PATTERNS[g][1]) for g in range(ATT_GROUPS)] * 2, [], bm=512, name="merge_b_bwd")
    return outs[:3], outs[3:]


def _loss_head(y, target):
    d = y.shape[1]

    def fn(ins, consts):
        e = ins[0] - ins[1]
        return [e * (1.0 / d)] * 2, [_colsum8(e * e)]

    (dy, dyb), (sq,) = _rowwise(fn, [(y, d, 0), (target, d, 0)], [], [(d, F32), (d, BF16)], [d], bm=512, name="loss_head")
    return 0.5 * jnp.sum(sq) / d, dy, dyb


def _silu_grad(a):
    s = _sigmoid(a)
    return s * (1.0 + a * (1.0 - s))


def _ffn_fwd(x, gain, wt, wo_fn, tag):
    t, d = x.shape
    f = wt.shape[0] // 2

    def act(accs, ex, consts):
        a, b = accs
        s = _sigmoid(a)
        sa = a * s
        return (sa * b, b, 0.5 * sa, 0.5 * (s + sa * (1.0 - s)))

    bn = FFN_BN if f % FFN_BN == 0 else 256
    u, b, sa, sp, h = _mm([x], [wt, wt], [(0, 0, 0), (0, 1, 1)], 2, act, [BF16] * 4, m=t, n=f, k=d, tb=True,
                          bm=512, bn=bn, bk=d, b_off=[(0, 0), (f // min(bn, f), 0)],
                          consts=[gain.reshape(1, d)], a_pro=_norm_pro, chunk=MXU_COLS, name=f"ffn_in_{tag}")
    wo = wo_fn(u)
    (y,) = _mm([u], [wo], [(0, 0, 0)], 1, lambda accs, ex: (ex[0] + 0.5 * accs[0],), [F32], m=t, n=d, k=f,
               bm=512, bn=d, bk=f, extras=[x], name=f"ffn_out_{tag}")
    return y, (x, h, u, b, sa, sp, wo)


def _ffn_bwd(dy, dyb, saved, gain, wt, tag, tok, emit):
    x, h, u, b, sa, sp, wo = saved
    t, d = x.shape
    f = wo.shape[0]

    def dact(accs, ex, consts):
        bv, sav, spv = (e.astype(F32) for e in ex)
        return (accs[0] * bv * spv, accs[0] * sav)

    bn = FFN_BN if f % FFN_BN == 0 else 256
    da, db = _mm([dyb], [wo], [(0, 0, 0)], 1, dact, [BF16, BF16], m=t, n=f, k=d, tb=True, bm=512, bn=bn, bk=d,
                 extras=[b, sa, sp], n_outer=True, chunk=MXU_COLS, consts=[tok], name=f"ffn_dact_{tag}")
    (dwo,) = _mm([u], [dyb], [(0, 0, 0)], 1, lambda accs, ex: (0.5 * accs[0],), [BF16], m=f, n=d, k=t, ta=True,
                 bm=1408, bn=d, bk=1024, name=f"ffn_dwo_{tag}")
    dwt = [_mm([g], [h], [(0, 0, 0)], 1, _first, [BF16], m=f, n=d, k=t, ta=True, bm=1408, bn=d, bk=1024,
               name=f"ffn_dwt{i}_{tag}")[0] for i, g in enumerate((da, db))]
    tok = emit(jnp.concatenate(dwt, axis=0), dwo)
    bk = min(FFN_BN, f)
    dx, dxb, dgain = _mm([da, db], [wt, wt], [(0, 0, 0), (1, 1, 0)], 1, _norm_bwd_fin, [F32, BF16], m=t, n=d, k=f,
                         bm=512, bn=d, bk=bk, b_off=[(0, 0), (0, f // bk)], extras=[x, dy],
                         consts=[gain.reshape(1, d), tok], n_sums=1, name=f"ffn_dh_{tag}")
    return dx, dxb, jnp.sum(dgain, axis=0), tok


FFN_BN = 2816
Z_SPLITS = (("h", 4096), ("q", 1536), ("k", 1536), ("v", 1536), ("g", 2048))


def _mix_fwd(x, p, cos, sin):
    t, d = x.shape
    z, off, hm = {}, 0, None
    for nm, width in Z_SPLITS:
        bn = 1024 if off % 1024 == 0 and width % 1024 == 0 else 512
        first = hm is None
        res = _mm([x if first else hm], [p["wint"]], [(0, 0, 0)], 1, (lambda accs, ex, consts: (accs[0],)) if first else _first,
                  [F32 if nm == "h" else BF16], m=t, n=width, k=d, tb=True, bm=1024, bn=bn, bk=d, b_off=[(off // bn, 0)],
                  consts=[p["gm"].reshape(1, d)] if first else (), a_pro=_norm_pro if first else None, name=f"mix_in_{nm}")
        z[nm] = res[0]
        hm = res[1] if first else hm
        off += width
    o_raw, states = _hgrn_fwd(z["h"], p["lb3"])
    qb, kb, vb = _qk_prep(z["q"], z["k"], z["v"], p["qn"], p["kn"], cos, sin)
    o_g, lse_g = zip(*[_attn_fwd(qb[g], kb[g], vb[g], g) for g in range(ATT_GROUPS)])
    oa = _post_a(o_raw, z["h"], p["gout"])
    ob = _merge_b(o_g, lse_g)
    late = p["late"](ob)
    p = dict(p, **late)
    (ya,) = _mm([oa], [p["wa"]], [(0, 0, 0)], 1, _first, [F32], m=t, n=d, k=oa.shape[1], bm=1024, bn=d, bk=oa.shape[1],
                name="branch_a")

    def gate(accs, ex):
        return (_sigmoid(ex[0].astype(F32)) * ex[2] + _sigmoid(ex[1].astype(F32)) * accs[0], accs[0])

    merged, yb = _mm([ob], [p["wbt"]], [(0, 0, 0)], 1, gate, [BF16, F32], m=t, n=d, k=ATT_GW, tb=True, bm=512, bn=d,
                     bk=ATT_GW, extras=[z["g"], z["g"], ya], e_off=[0, 1, 0], chunk=MXU_COLS, name="branch_b_gate")
    (y,) = _mm([merged], [p["wo"]], [(0, 0, 0)], 1, lambda accs, ex: (ex[0] + accs[0],), [F32], m=t, n=d, k=d,
               bm=1024, bn=d, bk=d, extras=[x], name="mix_out")
    return y, (x, hm, z, o_raw, states, qb, kb, vb, o_g, lse_g, oa, ob, ya, yb, merged, late)


def _mix_bwd(dy, dyb, saved, p, cos, sin, tok):
    x, hm, z, o_raw, states, qb, kb, vb, o_g, lse_g, oa, ob, ya, yb, merged, late = saved
    p = dict(p, **late)
    t, d = x.shape
    w = oa.shape[1]

    def dgate(accs, ex, consts):
        dm = accs[0]
        sa, sb = _sigmoid(ex[0].astype(F32)), _sigmoid(ex[1].astype(F32))
        return (sa * dm, sb * dm, dm * ex[2] * sa * (1.0 - sa), dm * ex[3] * sb * (1.0 - sb))

    dya, dyb_, dga, dgb = _mm([dyb], [p["wo"]], [(0, 0, 0)], 1, dgate, [BF16] * 4, m=t, n=d, k=d, tb=True, bm=512, bn=d,
                              bk=d, extras=[z["g"], z["g"], ya, yb], e_off=[0, 1, 0, 0], chunk=MXU_COLS, consts=[tok], name="mix_out_bwd")
    (dwo,) = _mm([merged], [dyb], [(0, 0, 0)], 1, _first, [BF16], m=d, n=d, k=t, ta=True, bm=d, bn=d, bk=1024, name="mix_dwo")
    (doa,) = _mm([dya], [p["wa"]], [(0, 0, 0)], 1, _first, [F32], m=t, n=w, k=d, tb=True, bm=1024, bn=w, bk=d, name="branch_a_bwd")
    (dwa,) = _mm([oa], [dya], [(0, 0, 0)], 1, _first, [BF16], m=w, n=d, k=t, ta=True, bm=w, bn=d, bk=1024, name="branch_a_dw")
    (dob,) = _mm([dyb_], [p["wbt"]], [(0, 0, 0)], 1, _first, [F32], m=t, n=ATT_GW, k=d, bm=1024, bn=ATT_GW, bk=d,
                 name="branch_b_bwd")
    (dwbt,) = _mm([dyb_], [ob], [(0, 0, 0)], 1, _first, [BF16], m=d, n=ATT_GW, k=t, ta=True, bm=d, bn=ATT_GW, bk=1024,
                  name="branch_b_dw")
    do_raw, dhg, dgout = _post_a_bwd(o_raw, z["h"], p["gout"], doa)
    do_g, dlse_g = _merge_b_bwd(o_g, lse_g, dob)
    dq_g, dk_g, dv_g = zip(*[_attn_bwd(qb[g], kb[g], vb[g], o_g[g], lse_g[g], do_g[g], dlse_g[g], g)
                             for g in range(ATT_GROUPS)])
    dzq, dzk, dzv, dqn, dkn = _qk_prep_bwd(z["q"], z["k"], dq_g, dk_g, dv_g, p["qn"], p["kn"], cos, sin)
    dhq, dhf, dhi, lbsum = _hgrn_bwd(z["h"], p["lb3"], states, do_raw)
    dz = jnp.concatenate([dhq, dhf, dhi, dhg, dzq, dzk, dzv, dga, dgb], axis=1)
    pw = dz.shape[1]
    (dwint,) = _mm([dz], [hm], [(0, 0, 0)], 1, _first, [BF16], m=pw, n=d, k=t, ta=True, bm=1536, bn=d, bk=1024, name="mix_in_dw")
    dx, dxb, dgm = _mm([dz], [p["wint"]], [(0, 0, 0)], 1, _norm_bwd_fin, [F32, BF16], m=t, n=d, k=pw, bm=512, bn=d, bk=1536,
                       extras=[x, dy], consts=[p["gm"].reshape(1, d)], n_sums=1, name="mix_in_bwd")
    return dx, dxb, dict(gm=jnp.sum(dgm, axis=0), wint=dwint, lbsum=lbsum, gout=dgout, qn=dqn, kn=dkn, wa=dwa, wbt=dwbt, wo=dwo)


def _rope_tables(t):
    pos = jnp.arange(t, dtype=F32)
    inv = ROPE_THETA ** (-jnp.arange(0, ATT_DH, 2, dtype=F32) / ATT_DH)
    ang = pos[:, None] * inv[None, :]
    ang = jnp.concatenate([ang, ang], axis=-1)
    return jnp.cos(ang), jnp.sin(ang)


def _lower_bounds(logits):
    lb = jnp.cumsum(jax.nn.softmax(logits, axis=0), axis=0)
    return lb - lb[0:1]


def _head_gain(g):
    return jnp.tile(g[:, None, :], (1, ATT_HEADS, 1)).reshape(1, ATT_GROUPS * ATT_GW)


SMALL_GRADS = ("ffn1_norm", "mix_norm", "lbsum", "hgrn_out_norm", "attn_q_norm", "attn_k_norm", "ffn2_norm")


def _local_step(x, target, small, fetch, emit):
    t = x.shape[0]
    depth = small["ffn1_norm"].shape[0]
    cos, sin = _rope_tables(t)
    lb_all = _lower_bounds(small["hgrn_lb_logits"])
    saved = []
    for l in range(depth):
        w1t = fetch("w1t", l, x)["w1t"]
        x, s1 = _ffn_fwd(x, small["ffn1_norm"][l], w1t, lambda after, l=l: fetch("w1o", l, after)["w1o"], "1")
        p = dict(gm=small["mix_norm"][l], wint=fetch("wint", l, x)["wint"], lb3=lb_all[l].reshape(-1, 1, HG_DK),
                 gout=small["hgrn_out_norm"][l], qn=_head_gain(small["attn_q_norm"][l]),
                 kn=_head_gain(small["attn_k_norm"][l]), late=functools.partial(fetch, "mout", l))
        x, sm = _mix_fwd(x, p, cos, sin)
        w2t = fetch("w2t", l, x)["w2t"]
        x, s2 = _ffn_fwd(x, small["ffn2_norm"][l], w2t, lambda after, l=l: fetch("w2o", l, after)["w2o"], "2")
        saved.append((p, w1t, w2t, s1, sm, s2))
    loss, dx, dxb = _loss_head(x, target)
    gsmall = {k: [None] * depth for k in SMALL_GRADS}
    tok = jnp.zeros((8, 128), F32)
    for l in reversed(range(depth)):
        p, w1t, w2t, s1, sm, s2 = saved[l]
        dx, dxb, gsmall["ffn2_norm"][l], tok = _ffn_bwd(
            dx, dxb, s2, small["ffn2_norm"][l], w2t, "2", tok, lambda dwt, dwo, l=l: emit("ffn2", l, dict(w2t=dwt, w2o=dwo), None))
        dx, dxb, gm = _mix_bwd(dx, dxb, sm, p, cos, sin, tok)
        tok = emit("mix", l, {k: gm[k] for k in ("wint", "wa", "wbt", "wo")}, None)
        gsmall["mix_norm"][l], gsmall["lbsum"][l], gsmall["hgrn_out_norm"][l] = gm["gm"], gm["lbsum"], gm["gout"]
        for k, src in (("attn_q_norm", "qn"), ("attn_k_norm", "kn")):
            gsmall[k][l] = jnp.sum(gm[src].reshape(ATT_GROUPS, ATT_HEADS, ATT_DH), axis=1)
        dx, dxb, gsmall["ffn1_norm"][l], tok = _ffn_bwd(
            dx, dxb, s1, small["ffn1_norm"][l], w1t, "1", tok, lambda dwt, dwo, l=l: emit("ffn1", l, dict(w1t=dwt, w1o=dwo), None))
    emit("small", 0, {}, ({k: jnp.stack(v) for k, v in gsmall.items()}, loss))
    return dx


_HBM = pl.BlockSpec(memory_space=pltpu.HBM)
_SEM = pl.BlockSpec(memory_space=pltpu.SEMAPHORE)
_EFFECT = pltpu.SideEffectType.DATAFLOW_SIDE_EFFECTING


def _peer(p):
    x, y, c = lax.axis_index("x"), lax.axis_index("y"), lax.axis_index("c")
    me = 4 * x + 2 * y + c
    return (1 - x if p & 4 else x, 1 - y if p & 2 else y, 1 - c if p & 1 else c), jnp.bitwise_xor(me, p), me


def _xchg_copy(src, land, mode, send_sems, recv_sems, k, p, arriving):
    peer, peer_id, me = _peer(p)
    block = src if mode == "gather" else src.at[peer_id]
    return pltpu.make_async_remote_copy(
        src_ref=block, dst_ref=land.at[peer_id if arriving else me], send_sem=send_sems.at[k * (N_DEV - 1) + p - 1],
        recv_sem=recv_sems.at[k * (N_DEV - 1) + p - 1], device_id=peer, device_id_type=MESH)


def _xchg_start(srcs, modes, groups, name):
    n, ng = len(srcs), len(groups)

    def body(*refs):
        src = refs[:n]
        sems = refs[n:n + 2 * ng]
        land = refs[n + 2 * ng + n:n + 2 * ng + 2 * n]
        token = refs[n + 2 * ng + 2 * n]
        for gi, idx in enumerate(groups):
            for ki, k in enumerate(idx):
                for p in range(1, N_DEV):
                    _xchg_copy(src[k], land[k], modes[k], sems[2 * gi], sems[2 * gi + 1], ki, p, False).start()
        token[...] = jnp.zeros_like(token)

    sem_shapes = []
    for idx in groups:
        sem_shapes += [pltpu.SemaphoreType.DMA((len(idx) * (N_DEV - 1),))] * 2
    outs = pl.pallas_call(
        body,
        out_shape=sem_shapes + [pltpu.HBM(a.shape, a.dtype) for a in srcs]
        + [pltpu.HBM((N_DEV,) + a.shape[-2:], a.dtype) for a in srcs] + [jax.ShapeDtypeStruct((8, 128), F32)],
        in_specs=[_HBM] * n,
        out_specs=[_SEM] * (2 * ng) + [_HBM] * (2 * n) + [pl.BlockSpec(memory_space=pltpu.VMEM)],
        input_output_aliases={i: 2 * ng + i for i in range(n)},
        compiler_params=pltpu.CompilerParams(has_side_effects=_EFFECT),
        name=name,
    )(*[pltpu.with_memory_space_constraint(a, pltpu.HBM) for a in srcs])
    sems = [(outs[2 * gi], outs[2 * gi + 1]) for gi in range(ng)]
    return sems, outs[2 * ng:2 * ng + n], outs[2 * ng + n:2 * ng + 2 * n], outs[-1]


def _xchg_wait_call(srcs, lands, modes, sems, after, name):
    n = len(srcs)

    def body(*refs):
        src, land = refs[:n], refs[n:2 * n]
        send_sems, recv_sems = refs[2 * n], refs[2 * n + 1]
        for p in range(1, N_DEV):
            for k in range(n):
                cp = _xchg_copy(src[k], land[k], modes[k], send_sems, recv_sems, k, p, True)
                cp.wait_send()
                cp.wait_recv()

    outs = pl.pallas_call(
        body,
        out_shape=[pltpu.HBM(a.shape, a.dtype) for a in list(srcs) + list(lands)],
        in_specs=[_HBM] * (2 * n) + [_SEM, _SEM, pl.BlockSpec(memory_space=pl.ANY)],
        out_specs=[_HBM] * (2 * n),
        input_output_aliases={i: i for i in range(2 * n)},
        compiler_params=pltpu.CompilerParams(has_side_effects=_EFFECT),
        name=name,
    )(*srcs, *lands, sems[0], sems[1], after)
    return outs[:n], outs[n:]


def _xchg_wait(srcs, lands, modes, sems, after, name):
    srcs, lands = _xchg_wait_call(srcs, lands, modes, sems, after, name)
    me = 4 * lax.axis_index("x") + 2 * lax.axis_index("y") + lax.axis_index("c")
    done = []
    for a, land, mode in zip(srcs, lands, modes):
        own = a[None] if mode == "gather" else lax.dynamic_slice_in_dim(a, me, 1, axis=0)
        done.append(lax.dynamic_update_slice(land, own, (me, 0, 0)))
    return done


def _sum_slots(land):
    g, _, r, c = land.shape
    br = r // 2 if (r % 32 == 0 and r >= 256) else r

    def body(l_ref, o_ref):
        acc = l_ref[0, 0].astype(F32)
        for j in range(1, N_DEV):
            acc = acc + l_ref[0, j].astype(F32)
        o_ref[0] = acc

    return pl.pallas_call(
        body,
        out_shape=jax.ShapeDtypeStruct((g, r, c), F32),
        grid=(g, r // br),
        in_specs=[pl.BlockSpec((1, N_DEV, br, c), lambda i, j: (i, 0, j, 0))],
        out_specs=pl.BlockSpec((1, br, c), lambda i, j: (i, j, 0)),
        compiler_params=_cparams(("parallel", "parallel")),
        name="sum_slots",
    )(land)


def _adamw(w, g, m, v):
    shape = w.shape
    cols = shape[-1]
    rows = int(np.prod(shape[:-1]))
    bm = max(b for b in range(8, 257, 8) if rows % b == 0) if rows % 8 == 0 else rows
    c1 = 1.0 - ADAM_B1 ** ADAM_STEP
    c2 = 1.0 - ADAM_B2 ** ADAM_STEP

    def fn(ins, consts):
        wv, gv, mv, vv = ins
        m2 = ADAM_B1 * mv + (1.0 - ADAM_B1) * gv
        v2 = ADAM_B2 * vv + (1.0 - ADAM_B2) * (gv * gv)
        delta = -ADAM_LR * ((m2 / c1) / (jnp.sqrt(v2 / c2) + ADAM_EPS) + ADAM_WD * wv)
        return [delta, m2, v2], []

    outs, _ = _rowwise(fn, [(a.reshape(rows, cols), cols, 0) for a in (w, g, m, v)], [], [(cols, F32)] * 3, [],
                       bm=bm, name="adamw")
    return [o.reshape(shape) for o in outs]


BIG = ("w1t", "w1o", "wint", "wa", "wbt", "wo", "w2t", "w2o")
FETCH_GROUPS = dict(w1t=("w1t",), w1o=("w1o",), wint=("wint",), mout=("wa", "wbt", "wo"), w2t=("w2t",), w2o=("w2o",))
SMALL_ROWS = (("ffn1_norm", 0), ("mix_norm", 2), ("lbsum", 4), ("hgrn_out_norm", 6), ("ffn2_norm", 8),
              ("attn_q_norm", 10), ("attn_k_norm", 12))
SMALL_PACK_ROWS = 16


def kernel(x, ffn1_norm, ffn1_w_in, ffn1_w_out, mix_norm, w_in, hgrn_lb_logits, hgrn_out_norm, attn_q_norm, attn_k_norm, w_branch_a, w_branch_b, w_out, ffn2_norm, ffn2_w_in, ffn2_w_out, loss_target, m_ffn1_norm, m_ffn1_w_in, m_ffn1_w_out, m_mix_norm, m_w_in, m_hgrn_lb_logits, m_hgrn_out_norm, m_attn_q_norm, m_attn_k_norm, m_w_branch_a, m_w_branch_b, m_w_out, m_ffn2_norm, m_ffn2_w_in, m_ffn2_w_out, v_ffn1_norm, v_ffn1_w_in, v_ffn1_w_out, v_mix_norm, v_w_in, v_hgrn_lb_logits, v_hgrn_out_norm, v_attn_q_norm, v_attn_k_norm, v_w_branch_a, v_w_branch_b, v_w_out, v_ffn2_norm, v_ffn2_w_in, v_ffn2_w_out):
    names = ("ffn1_norm", "ffn1_w_in", "ffn1_w_out", "mix_norm", "w_in", "hgrn_lb_logits", "hgrn_out_norm", "attn_q_norm",
             "attn_k_norm", "w_branch_a", "w_branch_b", "w_out", "ffn2_norm", "ffn2_w_in", "ffn2_w_out")
    w = dict(zip(names, (ffn1_norm, ffn1_w_in, ffn1_w_out, mix_norm, w_in, hgrn_lb_logits, hgrn_out_norm, attn_q_norm,
                         attn_k_norm, w_branch_a, w_branch_b, w_out, ffn2_norm, ffn2_w_in, ffn2_w_out)))
    m = dict(zip(names, (m_ffn1_norm, m_ffn1_w_in, m_ffn1_w_out, m_mix_norm, m_w_in, m_hgrn_lb_logits, m_hgrn_out_norm,
                         m_attn_q_norm, m_attn_k_norm, m_w_branch_a, m_w_branch_b, m_w_out, m_ffn2_norm, m_ffn2_w_in, m_ffn2_w_out)))
    v = dict(zip(names, (v_ffn1_norm, v_ffn1_w_in, v_ffn1_w_out, v_mix_norm, v_w_in, v_hgrn_lb_logits, v_hgrn_out_norm,
                         v_attn_q_norm, v_attn_k_norm, v_w_branch_a, v_w_branch_b, v_w_out, v_ffn2_norm, v_ffn2_w_in, v_ffn2_w_out)))
    depth, d = ffn1_norm.shape

    def tr(a):
        return jnp.swapaxes(a, 1, 2)

    shard = dict(w1t=tr(ffn1_w_in), w1o=ffn1_w_out, wint=tr(w_in), wa=w_branch_a,
                 wbt=tr(w_branch_b).reshape(depth, -1, d), wo=w_out, w2t=tr(ffn2_w_in), w2o=ffn2_w_out)
    order = [(g, l) for l in range(depth) for g in FETCH_GROUPS]
    flat = [(g, l, k) for g, l in order for k in FETCH_GROUPS[g]]
    groups, pos = [], 0
    for g, l in order:
        groups.append(list(range(pos, pos + len(FETCH_GROUPS[g]))))
        pos += len(FETCH_GROUPS[g])
    g_sems, g_srcs, g_lands, _ = _xchg_start([shard[k][l].astype(BF16) for _, l, k in flat], ["gather"] * len(flat),
                                             groups, "gather_start")

    def fetch(group, l, after):
        gi = order.index((group, l))
        idx = groups[gi]
        lands = _xchg_wait([g_srcs[i] for i in idx], [g_lands[i] for i in idx], ["gather"] * len(idx), g_sems[gi], after,
                           f"gather_wait_{group}{l}")
        out = {}
        for k, land in zip(FETCH_GROUPS[group], lands):
            out[k] = land.reshape(d, -1) if k == "wbt" else land.reshape(-1, d)
        return out

    pending = []

    def emit(group, l, g, final):
        keys = list(g)
        srcs = [g[k].reshape(N_DEV, -1, d) for k in keys]
        modes = ["scatter"] * len(keys)
        if final is not None:
            gsmall, loss = final
            pack = jnp.zeros((SMALL_PACK_ROWS, d), F32)
            for k, r0 in SMALL_ROWS:
                rows = gsmall[k].reshape(depth, -1)
                pack = pack.at[r0:r0 + depth, :rows.shape[1]].set(rows)
            srcs.append(pack.at[14, :].set(loss))
            modes.append("gather")
            keys.append("small")
        sems, s_thru, l_thru, token = _xchg_start(srcs, modes, [list(range(len(srcs)))], f"grads_start_{group}{l}")
        pending.append((group, l, keys, modes, sems[0], s_thru, l_thru))
        return token

    small = {k: w[k] for k in ("ffn1_norm", "mix_norm", "hgrn_lb_logits", "hgrn_out_norm", "attn_q_norm", "attn_k_norm", "ffn2_norm")}
    dx = _local_step(x[0], loss_target[0], small, fetch, emit)

    summed = {}
    for group, l, keys, modes, sems, s_thru, l_thru in pending:
        lands = _xchg_wait(s_thru, l_thru, modes, sems, dx, f"grads_wait_{group}{l}")
        for k, land in zip(keys, lands):
            summed[k, l] = _sum_slots(land[None])[0]
    gsum = {k: jnp.stack([summed[k, l] for l in range(depth)]) for k in BIG}
    tot = summed["small", 0]

    grads = {}
    for k, r0 in SMALL_ROWS:
        shp = (depth,) + (w[k].shape[1:] if k != "lbsum" else (d,))
        grads[k] = tot[r0:r0 + depth, :int(np.prod(shp[1:]))].reshape(shp)
    _, lb_vjp = jax.vjp(_lower_bounds, hgrn_lb_logits)
    grads["hgrn_lb_logits"] = lb_vjp(grads.pop("lbsum"))[0]
    grads["ffn1_w_in"], grads["ffn1_w_out"] = tr(gsum["w1t"]), gsum["w1o"]
    grads["w_in"], grads["w_branch_a"] = tr(gsum["wint"]), gsum["wa"]
    grads["w_branch_b"] = tr(gsum["wbt"].reshape(depth, d // N_DEV, -1))
    grads["w_out"] = gsum["wo"]
    grads["ffn2_w_in"], grads["ffn2_w_out"] = tr(gsum["w2t"]), gsum["w2o"]

    upd = {k: _adamw(w[k], grads[k], m[k], v[k]) for k in names}
    return (tot[14, 0], dx[None], *[grads[k] for k in names], *[upd[k][0] for k in names],
            *[upd[k][1] for k in names], *[upd[k][2] for k in names])
```

```python
import functools
import math

import jax
import jax.numpy as jnp
import numpy as np
from jax import lax
from jax.experimental import pallas as pl
from jax.experimental.pallas import tpu as pltpu

F32 = jnp.float32
BF16 = jnp.bfloat16

N_DEV = 8
EPS = 1e-6
HG_DK = 128
HG_CHUNK = 64
HG_SUB = 16
HG_HP = 8
ATT_PATTERNS = ((128, 1), (512, 4), (2048, 16))
ATT_GROUPS = 3
ATT_HEADS = 4
ATT_DH = 128
ATT_BLK = 128
ROPE_THETA = 10000.0
ADAM_LR, ADAM_B1, ADAM_B2, ADAM_EPS, ADAM_WD, ADAM_STEP = 0.001, 0.9, 0.999, 1e-08, 0.01, 10
VMEM_LIMIT_BYTES = 56 * 1024 * 1024
MXU_COLS = 256
MESH = pl.DeviceIdType.MESH


def _cparams(sem, **kw):
    return pltpu.CompilerParams(dimension_semantics=sem, vmem_limit_bytes=VMEM_LIMIT_BYTES, **kw)


def _sigmoid(x):
    return 1.0 / (1.0 + jnp.exp(-x))


def _mm(a_list, b_list, pairs, n_acc, fin, out_dtypes, *, m, n, k, ta=False, tb=False, bm, bn, bk,
        b_off=None, extras=(), e_off=None, n_outer=False, consts=(), a_pro=None, n_sums=0, chunk=0, a_cat=False, name):
    bm, bn, bk = min(bm, m), min(bn, n), min(bk, k)
    assert m % bm == 0 and n % bn == 0 and k % bk == 0, (name, m, n, k, bm, bn, bk)
    nk = k // bk
    assert not (a_pro and (nk > 1 or ta or n_outer)) and not (n_sums and (bn != n or n_outer)), name
    assert not (chunk and (nk > 1 or n_sums or chunk % 128)), name
    if a_cat:
        unit = bm if ta else bk
        widths = [a.shape[1] for a in a_list]
        assert all(w % unit == 0 for w in widths) and sum(widths) == (m if ta else k) and not a_pro, name
        cat_counts = [w // unit for w in widths]
        cat_starts = [sum(cat_counts[:i]) for i in range(len(widths))]
    b_off = b_off or [(0, 0)] * len(b_list)
    e_off = e_off or [0] * len(extras)
    na, nb, ne, nc, no = len(a_list), len(b_list), len(extras), len(consts), len(out_dtypes)
    nao = na if a_pro else 0
    dn = (((0,) if ta else (1,), (1,) if tb else (0,)), ((), ()))

    def body(*refs):
        refs = list(refs)
        a_refs, b_refs, e_refs, c_refs, o_refs, ao_refs, s_refs = (
            [refs.pop(0) for _ in range(cnt)] for cnt in (na, nb, ne, nc, no, nao, n_sums))
        acc_refs = refs
        kk = pl.program_id(2)
        first = pl.program_id(0) == 0
        cvals = [c[...] for c in c_refs]
        a_vals = [r[...] for r in a_refs]
        if a_cat:
            col = pl.program_id(1 if n_outer else 0) if ta else kk
            sel = a_vals[0]
            for start, v in zip(cat_starts[1:], a_vals[1:]):
                sel = jnp.where(col >= start, v, sel)
            a_vals = [sel]
        if a_pro:
            a_vals = a_pro(a_vals, cvals)
            for r, v in zip(ao_refs, a_vals):
                r[...] = v
        if chunk:
            spans = [slice(lo, min(lo + chunk, bn)) for lo in range(0, bn, chunk)]
            chunks = []
            for cs in spans:
                parts = [None] * n_acc
                for ai, bi, ci in pairs:
                    p = lax.dot_general(a_vals[ai], b_refs[bi][cs, :] if tb else b_refs[bi][:, cs], dn,
                                        preferred_element_type=F32)
                    parts[ci] = p if parts[ci] is None else parts[ci] + p
                chunks.append(parts)
            for cs, parts in zip(spans, chunks):
                ex = [e[:, cs] for e in e_refs]
                outs = fin(parts, ex, cvals) if nc else fin(parts, ex)
                for o_ref, o in zip(o_refs, outs):
                    o_ref[:, cs] = o.astype(o_ref.dtype)
            return

        parts = [None] * n_acc
        for ai, bi, ci in pairs:
            p = lax.dot_general(a_vals[ai], b_refs[bi][...], dn, preferred_element_type=F32)
            parts[ci] = p if parts[ci] is None else parts[ci] + p

        def finish(accs):
            ex = [e[...] for e in e_refs]
            res = fin(accs, ex, cvals) if nc else fin(accs, ex)
            outs, sums = res if n_sums else (res, ())
            for o_ref, o in zip(o_refs, outs):
                o_ref[...] = o.astype(o_ref.dtype)
            if n_sums:
                @pl.when(first)
                def _():
                    for s_ref, s in zip(s_refs, sums):
                        s_ref[...] = s

                @pl.when(jnp.logical_not(first))
                def _():
                    for s_ref, s in zip(s_refs, sums):
                        s_ref[...] += s

        if nk == 1:
            finish(parts)
        else:
            @pl.when(kk == 0)
            def _():
                for c in range(n_acc):
                    acc_refs[c][...] = parts[c]

            @pl.when(kk > 0)
            def _():
                for c in range(n_acc):
                    acc_refs[c][...] += parts[c]

            @pl.when(kk == nk - 1)
            def _():
                finish([acc_refs[c][...] for c in range(n_acc)])

    def ij(f):
        return (lambda j, i, q: f(i, j, q)) if n_outer else f

    a_spec = pl.BlockSpec((bk, bm), ij(lambda i, j, q: (q, i))) if ta else pl.BlockSpec((bm, bk), ij(lambda i, j, q: (i, q)))
    a_specs = [a_spec] * na
    if a_cat:
        def part_spec(start, count):
            def col(c):
                return jnp.clip(c - start, 0, count - 1)
            if ta:
                return pl.BlockSpec((bk, bm), ij(lambda i, j, q: (q, col(i))))
            return pl.BlockSpec((bm, bk), ij(lambda i, j, q: (i, col(q))))
        a_specs = [part_spec(s, c) for s, c in zip(cat_starts, cat_counts)]

    b_mode = dict(pipeline_mode=pl.Buffered(1)) if (bn == n and nk == 1) else {}

    def b_spec(off):
        on, ok = off
        if tb:
            return pl.BlockSpec((bn, bk), ij(lambda i, j, q: (j + on, q + ok)), **b_mode)
        return pl.BlockSpec((bk, bn), ij(lambda i, j, q: (q + ok, j + on)), **b_mode)

    mn_spec = pl.BlockSpec((bm, bn), ij(lambda i, j, q: (i, j)))
    outs = pl.pallas_call(
        body,
        out_shape=[jax.ShapeDtypeStruct((m, n), d) for d in out_dtypes] + [jax.ShapeDtypeStruct((m, k), BF16)] * nao
        + [jax.ShapeDtypeStruct((8, n), F32)] * n_sums,
        grid=(n // bn, m // bm, nk) if n_outer else (m // bm, n // bn, nk),
        in_specs=a_specs + [b_spec(o) for o in b_off]
        + [pl.BlockSpec((bm, bn), ij(lambda i, j, q, o=o: (i, j + o))) for o in e_off]
        + [pl.BlockSpec(c.shape, lambda *_, nd=c.ndim: (0,) * nd) for c in consts],
        out_specs=[mn_spec] * no + [a_spec] * nao + [pl.BlockSpec((8, n), lambda *_: (0, 0))] * n_sums,
        scratch_shapes=[pltpu.VMEM((bm, bn), F32) for _ in range(n_acc if nk > 1 else 0)],
        compiler_params=_cparams(("arbitrary" if n_sums else "parallel", "parallel", "arbitrary")),
        name=name,
    )(*a_list, *b_list, *extras, *consts)
    return outs


def _first(accs, ex):
    return (accs[0],)


def _rowwise(fn, ins, consts, out_defs, sum_widths, *, bm, name):
    ins = [tuple(e) + (1,) * (4 - len(e)) for e in ins]
    out_defs = [tuple(e) + (1,) * (3 - len(e)) for e in out_defs]
    t = ins[0][0].shape[-2] * ins[0][3]
    bm = min(bm, t)
    assert t % bm == 0, (name, t, bm)
    ni, nc, no, ns = len(ins), len(consts), len(out_defs), len(sum_widths)
    strided = [w for _, w, _, d in ins if d > 1] + [w for w, _, d in out_defs if d > 1]

    def body(*refs):
        i_refs, c_refs = refs[:ni], refs[ni:ni + nc]
        o_refs, s_refs = refs[ni + nc:ni + nc + no], refs[ni + nc + no:ni + nc + no + ns]
        scratch = list(refs[ni + nc + no + ns:])
        vals = []
        for ref, (_, w, _, d) in zip(i_refs, ins):
            if d == 1:
                vals.append(ref[...])
                continue
            s = scratch.pop(0)
            for r in range(d):
                for c in range(w // 128):
                    s.at[c][pl.ds(r, bm // d, stride=d), :] = ref[r, :, c * 128:(c + 1) * 128].astype(F32)
            vals.append(jnp.concatenate([s[c] for c in range(w // 128)], axis=1))
        outs, sums = fn(vals, [r[...] for r in c_refs])
        for o_ref, o, (w, _, d) in zip(o_refs, outs, out_defs):
            if d == 1:
                o_ref[...] = o.astype(o_ref.dtype)
                continue
            s = scratch.pop(0)
            for c in range(w // 128):
                s[c] = o[:, c * 128:(c + 1) * 128].astype(F32)
            for r in range(d):
                for c in range(w // 128):
                    o_ref[r, :, c * 128:(c + 1) * 128] = s.at[c][pl.ds(r, bm // d, stride=d), :].astype(o_ref.dtype)
        if ns:
            first = pl.program_id(0) == 0

            @pl.when(first)
            def _():
                for s_ref, s in zip(s_refs, sums):
                    s_ref[...] = s

            @pl.when(jnp.logical_not(first))
            def _():
                for s_ref, s in zip(s_refs, sums):
                    s_ref[...] += s

    def win(width, cb, d):
        if d > 1:
            return pl.BlockSpec((d, bm // d, width), lambda i: (0, i, 0))
        return pl.BlockSpec((bm, width), lambda i: (i, cb))

    res = pl.pallas_call(
        body,
        out_shape=[jax.ShapeDtypeStruct((t, w) if d == 1 else (d, t // d, w), dt) for w, dt, d in out_defs]
        + [jax.ShapeDtypeStruct((8, w), F32) for w in sum_widths],
        grid=(t // bm,),
        in_specs=[win(w, cb, d) for _, w, cb, d in ins] + [pl.BlockSpec(c.shape, lambda i, nd=c.ndim: (0,) * nd) for c in consts],
        out_specs=[win(w, 0, d) for w, _, d in out_defs] + [pl.BlockSpec((8, w), lambda i: (0, 0)) for w in sum_widths],
        scratch_shapes=[pltpu.VMEM((w // 128, bm, 128), F32) for w in strided],
        compiler_params=_cparams(("arbitrary",) if ns else ("parallel",)),
        name=name,
    )(*[e[0] for e in ins], *consts)
    return res[:no], [jnp.sum(s, axis=0) for s in res[no:]]


def _colsum8(x):
    bm, w = x.shape
    return jnp.sum(x.reshape(bm // 8, 8, w), axis=0)


def _tri(n, upper=False):
    r = lax.broadcasted_iota(jnp.int32, (n, n), 0)
    c = lax.broadcasted_iota(jnp.int32, (n, n), 1)
    return (c >= r) if upper else (c <= r)


def _exact_tri_matmul(tri_bf16, x):
    x0 = x.astype(BF16)
    r1 = x - x0.astype(F32)
    x1 = r1.astype(BF16)
    x2 = (r1 - x1.astype(F32)).astype(BF16)
    w = x.shape[1]
    y = jnp.dot(tri_bf16, jnp.concatenate([x0, x1, x2], axis=1), preferred_element_type=F32)
    return y[:, :w] + y[:, w:2 * w] + y[:, 2 * w:]


def _dot_nt(a, b):
    return lax.dot_general(a, b, (((1,), (1,)), ((), ())), preferred_element_type=F32)


def _dot_tn(a, b):
    return lax.dot_general(a, b, (((0,), (0,)), ((), ())), preferred_element_type=F32)


def _dot(a, b):
    return jnp.dot(a, b, preferred_element_type=F32)


def _hg_gates(hq, hf, lb):
    sq = _sigmoid(hq)
    q = hq * sq
    sg = _sigmoid(hf)
    f = lb + (1.0 - lb) * sg
    return q, sq, sg, f


def _hg_intra(q, kk, g):
    c = q.shape[0]
    rows = lax.broadcasted_iota(jnp.int32, (c, 1), 0)
    a_rows, qts, kts, eqs, eks = [], [], [], [], []
    for i in range(c // HG_SUB):
        lo = i * HG_SUB
        ref = g[lo - 1:lo, :] if i else jnp.zeros_like(g[0:1, :])
        eq = jnp.exp(g[lo:lo + HG_SUB, :] - ref)
        ek = jnp.exp(jnp.where(rows < lo + HG_SUB, ref - g, 0.0))
        qt = q[lo:lo + HG_SUB, :] * eq
        kt = kk * ek
        a = _dot_nt(qt.astype(BF16), kt.astype(BF16))
        tpos = lo + lax.broadcasted_iota(jnp.int32, (HG_SUB, c), 0)
        spos = lax.broadcasted_iota(jnp.int32, (HG_SUB, c), 1)
        a_rows.append(jnp.where(spos <= tpos, a, 0.0))
        qts.append(qt), kts.append(kt), eqs.append(eq), eks.append(ek)
    return jnp.concatenate(a_rows, axis=0), qts, kts, eqs, eks


def _hgrn_fwd_serial(zh, lb3, *, tb=512):
    t = zh.shape[0]
    nh = lb3.shape[0]
    c = HG_CHUNK
    tb = min(tb, t)
    nchunk = tb // c
    hp = HG_HP if nh % HG_HP == 0 else 1

    def body(hq_ref, hf_ref, hi_ref, lb_ref, o_ref, st_ref, state):
        @pl.when(pl.program_id(1) == 0)
        def _():
            state[...] = jnp.zeros_like(state)

        tril = _tri(c).astype(BF16)

        def one_head(hh, ci, sl):
            ls = slice(hh * HG_DK, (hh + 1) * HG_DK)
            q, _, _, f = _hg_gates(hq_ref[sl, ls], hf_ref[sl, ls], lb_ref[hh])
            v = hi_ref[sl, ls]
            kk = 1.0 - f
            g = _exact_tri_matmul(tril, jnp.log(f))
            a, _, _, _, _ = _hg_intra(q, kk, g)
            st = state[hh]
            st_ref[hh, ci] = st
            vb = v.astype(BF16)
            o = _dot(a.astype(BF16), vb) + _dot_nt((q * jnp.exp(g)).astype(BF16), st.astype(BF16))
            o_ref[sl, ls] = o
            glast = g[c - 1:c, :]
            kg = kk * jnp.exp(glast - g)
            state[hh] = st * jnp.exp(glast) + _dot_tn(vb, kg.astype(BF16))

        def chunk(ci, carry):
            sl = pl.ds(pl.multiple_of(ci * c, c), c)
            for hh in range(hp):
                one_head(hh, ci, sl)
            return carry

        lax.fori_loop(0, nchunk, chunk, 0)

    def col(cb):
        return pl.BlockSpec((tb, hp * HG_DK), lambda h, i: (i, cb * (nh // hp) + h))

    return pl.pallas_call(
        body,
        out_shape=[jax.ShapeDtypeStruct((t, nh * HG_DK), F32), jax.ShapeDtypeStruct((nh, t // c, HG_DK, HG_DK), F32)],
        grid=(nh // hp, t // tb),
        in_specs=[col(0), col(1), col(2), pl.BlockSpec((hp, 1, HG_DK), lambda h, i: (h, 0, 0))],
        out_specs=[pl.BlockSpec((tb, hp * HG_DK), lambda h, i: (i, h)),
                   pl.BlockSpec((hp, nchunk, HG_DK, HG_DK), lambda h, i: (h, i, 0, 0))],
        scratch_shapes=[pltpu.VMEM((hp, HG_DK, HG_DK), F32)],
        compiler_params=_cparams(("parallel", "arbitrary")),
        name="hgrn_fwd",
    )(zh, zh, zh, lb3)


def _hgrn_bwd_serial(zh, lb3, states, d_o, *, tb=512):
    t = zh.shape[0]
    nh = lb3.shape[0]
    c = HG_CHUNK
    tb = min(tb, t)
    nchunk = tb // c
    nblk = t // tb
    hp = HG_HP if nh % HG_HP == 0 else 1

    def body(hq_ref, hf_ref, hi_ref, lb_ref, st_ref, do_ref, dq_ref, df_ref, dv_ref, dlb_ref, dstate):
        @pl.when(pl.program_id(1) == 0)
        def _():
            dstate[...] = jnp.zeros_like(dstate)
            dlb_ref[...] = jnp.zeros_like(dlb_ref)

        tril = _tri(c).astype(BF16)
        triu = _tri(c, upper=True).astype(BF16)
        last_row = lax.broadcasted_iota(jnp.int32, (c, 1), 0) == c - 1

        def one_head(hh, ci, sl):
            ls = slice(hh * HG_DK, (hh + 1) * HG_DK)
            lb = lb_ref[hh]
            hq, hf = hq_ref[sl, ls], hf_ref[sl, ls]
            q, sq, sg, f = _hg_gates(hq, hf, lb)
            v = hi_ref[sl, ls]
            kk = 1.0 - f
            g = _exact_tri_matmul(tril, jnp.log(f))
            a, qts, kts, eqs, eks = _hg_intra(q, kk, g)
            st = st_ref[hh, ci]
            dst = dstate[hh]
            do = do_ref[sl, ls]
            dob, vb = do.astype(BF16), v.astype(BF16)
            glast = g[c - 1:c, :]
            eg = jnp.exp(g)
            egl = jnp.exp(glast - g)
            qg = q * eg
            kg = kk * egl
            dv = _dot_tn(a.astype(BF16), dob) + _dot_nt(kg.astype(BF16), dst.astype(BF16))
            da = jnp.where(_tri(c), _dot_nt(dob, vb), 0.0).astype(BF16)
            dq_parts, dgq_parts = [], []
            dk = jnp.zeros_like(kk)
            dgk = jnp.zeros_like(kk)
            for i in range(c // HG_SUB):
                da_i = da[i * HG_SUB:(i + 1) * HG_SUB, :]
                ktb, qtb = kts[i].astype(BF16), qts[i].astype(BF16)
                xi = _dot(da_i, ktb)
                yi = _dot_tn(da_i, qtb)
                dq_parts.append(xi * eqs[i])
                dk = dk + yi * eks[i]
                dgq_parts.append(xi * qtb.astype(F32))
                dgk = dgk + yi * ktb.astype(F32)
            dq_inter = _dot(dob, st.astype(BF16)) * eg
            dq = jnp.concatenate(dq_parts, axis=0) + dq_inter
            dk_state = _dot(vb, dst.astype(BF16)) * egl
            dk = dk + dk_state
            dg = jnp.concatenate(dgq_parts, axis=0) - dgk + q * dq_inter - kk * dk_state
            dgl = jnp.sum(kk * dk_state, axis=0, keepdims=True) + jnp.exp(glast) * jnp.sum(st * dst, axis=0, keepdims=True)
            dg = dg + jnp.where(last_row, dgl, 0.0)
            dlogf = _exact_tri_matmul(triu, dg)
            dfv = dlogf / f - dk
            dq_ref[sl, ls] = (dq * (sq * (1.0 + hq * (1.0 - sq)))).astype(dq_ref.dtype)
            df_ref[sl, ls] = (dfv * (1.0 - lb) * sg * (1.0 - sg)).astype(df_ref.dtype)
            dv_ref[sl, ls] = dv.astype(dv_ref.dtype)
            dlb_ref[hh] += jnp.sum(dfv * (1.0 - sg), axis=0, keepdims=True)
            dstate[hh] = dst * jnp.exp(glast) + _dot_tn(dob, qg.astype(BF16))

        def chunk(j, carry):
            ci = nchunk - 1 - j
            sl = pl.ds(pl.multiple_of(ci * c, c), c)
            for hh in range(hp):
                one_head(hh, ci, sl)
            return carry

        lax.fori_loop(0, nchunk, chunk, 0)

    def col(cb):
        return pl.BlockSpec((tb, hp * HG_DK), lambda h, i: (nblk - 1 - i, cb * (nh // hp) + h))

    ocol = pl.BlockSpec((tb, hp * HG_DK), lambda h, i: (nblk - 1 - i, h))
    w = nh * HG_DK
    dq, df, dv, dlb = pl.pallas_call(
        body,
        out_shape=[jax.ShapeDtypeStruct((t, w), BF16)] * 3 + [jax.ShapeDtypeStruct((nh, 1, HG_DK), F32)],
        grid=(nh // hp, nblk),
        in_specs=[col(0), col(1), col(2), pl.BlockSpec((hp, 1, HG_DK), lambda h, i: (h, 0, 0)),
                  pl.BlockSpec((hp, nchunk, HG_DK, HG_DK), lambda h, i: (h, nblk - 1 - i, 0, 0)), ocol],
        out_specs=[ocol, ocol, ocol, pl.BlockSpec((hp, 1, HG_DK), lambda h, i: (h, 0, 0))],
        scratch_shapes=[pltpu.VMEM((hp, HG_DK, HG_DK), F32)],
        compiler_params=_cparams(("parallel", "arbitrary")),
        name="hgrn_bwd",
    )(zh, zh, zh, lb3, states, d_o)
    return dq, df, dv, dlb.reshape(w)


def _hg_heads(x, hp):
    return [x[:, h * HG_DK:(h + 1) * HG_DK] for h in range(hp)]


def _hg_intra_wide(q, kk, g, hp):
    c = q.shape[0]
    rows = lax.broadcasted_iota(jnp.int32, (c, 1), 0)
    a_rows = [[] for _ in range(hp)]
    qts, kts, eqs, eks = [], [], [], []
    for i in range(c // HG_SUB):
        lo = i * HG_SUB
        ref = g[lo - 1:lo, :] if i else jnp.zeros_like(g[0:1, :])
        eq = jnp.exp(g[lo:lo + HG_SUB, :] - ref)
        ek = jnp.exp(jnp.where(rows < lo + HG_SUB, ref - g, 0.0))
        qtb = (q[lo:lo + HG_SUB, :] * eq).astype(BF16)
        ktb = (kk * ek).astype(BF16)
        tpos = lo + lax.broadcasted_iota(jnp.int32, (HG_SUB, c), 0)
        spos = lax.broadcasted_iota(jnp.int32, (HG_SUB, c), 1)
        for h, (qh, kh) in enumerate(zip(_hg_heads(qtb, hp), _hg_heads(ktb, hp))):
            a_rows[h].append(jnp.where(spos <= tpos, _dot_nt(qh, kh), 0.0))
        qts.append(qtb), kts.append(ktb), eqs.append(eq), eks.append(ek)
    return [jnp.concatenate(r, axis=0) for r in a_rows], qts, kts, eqs, eks


def _hgrn_fwd(zh, lb3, *, tb=512):
    t = zh.shape[0]
    nh = lb3.shape[0]
    c = HG_CHUNK
    tb = min(tb, t)
    nchunk = tb // c
    hp = HG_HP if nh % HG_HP == 0 else 1
    wp = hp * HG_DK

    def body(hq_ref, hf_ref, hi_ref, lb_ref, o_ref, st_ref, state):
        @pl.when(pl.program_id(1) == 0)
        def _():
            state[...] = jnp.zeros_like(state)

        tril = _tri(c).astype(BF16)

        def chunk(ci, carry):
            sl = pl.ds(pl.multiple_of(ci * c, c), c)
            q, _, _, f = _hg_gates(hq_ref[sl, :], hf_ref[sl, :], lb_ref[...])
            kk = 1.0 - f
            g = _exact_tri_matmul(tril, jnp.log(f))
            a, _, _, _, _ = _hg_intra_wide(q, kk, g, hp)
            vb = hi_ref[sl, :].astype(BF16)
            glast = g[c - 1:c, :]
            qgb = (q * jnp.exp(g)).astype(BF16)
            kgb = (kk * jnp.exp(glast - g)).astype(BF16)
            dec = jnp.exp(glast)
            sts = [state[h] for h in range(hp)]
            for h in range(hp):
                st_ref[h, ci] = sts[h]
            vh, qgh, kgh, dech = _hg_heads(vb, hp), _hg_heads(qgb, hp), _hg_heads(kgb, hp), _hg_heads(dec, hp)
            o = [_dot(a[h].astype(BF16), vh[h]) + _dot_nt(qgh[h], sts[h].astype(BF16)) for h in range(hp)]
            new = [_dot_tn(vh[h], kgh[h]) for h in range(hp)]
            o_ref[sl, :] = jnp.concatenate(o, axis=1)
            for h in range(hp):
                state[h] = sts[h] * dech[h] + new[h]
            return carry

        lax.fori_loop(0, nchunk, chunk, 0)

    def col(cb):
        return pl.BlockSpec((tb, wp), lambda h, i: (i, cb * (nh // hp) + h))

    return pl.pallas_call(
        body,
        out_shape=[jax.ShapeDtypeStruct((t, nh * HG_DK), F32), jax.ShapeDtypeStruct((nh, t // c, HG_DK, HG_DK), F32)],
        grid=(nh // hp, t // tb),
        in_specs=[col(0), col(1), col(2), pl.BlockSpec((1, wp), lambda h, i: (0, h))],
        out_specs=[pl.BlockSpec((tb, wp), lambda h, i: (i, h)),
                   pl.BlockSpec((hp, nchunk, HG_DK, HG_DK), lambda h, i: (h, i, 0, 0))],
        scratch_shapes=[pltpu.VMEM((hp, HG_DK, HG_DK), F32)],
        compiler_params=_cparams(("parallel", "arbitrary")),
        name="hgrn_fwd",
    )(zh, zh, zh, lb3.reshape(1, -1))


def _hgrn_bwd(zh, lb3, states, d_o, *, tb=512):
    t = zh.shape[0]
    nh = lb3.shape[0]
    c = HG_CHUNK
    tb = min(tb, t)
    nchunk = tb // c
    nblk = t // tb
    hp = HG_HP if nh % HG_HP == 0 else 1
    wp = hp * HG_DK

    def body(hq_ref, hf_ref, hi_ref, lb_ref, st_ref, do_ref, dq_ref, df_ref, dv_ref, dlb_ref, dstate):
        @pl.when(pl.program_id(1) == 0)
        def _():
            dstate[...] = jnp.zeros_like(dstate)
            dlb_ref[...] = jnp.zeros_like(dlb_ref)

        tril = _tri(c).astype(BF16)
        triu = _tri(c, upper=True).astype(BF16)
        last_row = lax.broadcasted_iota(jnp.int32, (c, 1), 0) == c - 1
        heads = range(hp)

        def chunk(j, carry):
            ci = nchunk - 1 - j
            sl = pl.ds(pl.multiple_of(ci * c, c), c)
            lb = lb_ref[...]
            hq, hf = hq_ref[sl, :], hf_ref[sl, :]
            q, sq, sg, f = _hg_gates(hq, hf, lb)
            kk = 1.0 - f
            g = _exact_tri_matmul(tril, jnp.log(f))
            a, qts, kts, eqs, eks = _hg_intra_wide(q, kk, g, hp)
            glast = g[c - 1:c, :]
            eg, egl, dec = jnp.exp(g), jnp.exp(glast - g), jnp.exp(glast)
            vb, dob = hi_ref[sl, :].astype(BF16), do_ref[sl, :].astype(BF16)
            qgb, kgb = (q * eg).astype(BF16), (kk * egl).astype(BF16)
            sts = [st_ref[h, ci] for h in heads]
            dsts = [dstate[h] for h in heads]
            stb, dstb = [s.astype(BF16) for s in sts], [s.astype(BF16) for s in dsts]
            vh, doh, qgh, kgh = _hg_heads(vb, hp), _hg_heads(dob, hp), _hg_heads(qgb, hp), _hg_heads(kgb, hp)
            dv = [_dot_tn(a[h].astype(BF16), doh[h]) + _dot_nt(kgh[h], dstb[h]) for h in heads]
            da = [jnp.where(_tri(c), _dot_nt(doh[h], vh[h]), 0.0).astype(BF16) for h in heads]
            dq_inter = jnp.concatenate([_dot(doh[h], stb[h]) for h in heads], axis=1) * eg
            dk_state = jnp.concatenate([_dot(vh[h], dstb[h]) for h in heads], axis=1) * egl
            new_dst = [_dot_tn(doh[h], qgh[h]) for h in heads]
            xs, dk, dgk = [], dk_state, 0.0
            for i in range(c // HG_SUB):
                rs = slice(i * HG_SUB, (i + 1) * HG_SUB)
                kth, qth = _hg_heads(kts[i], hp), _hg_heads(qts[i], hp)
                xi = jnp.concatenate([_dot(da[h][rs, :], kth[h]) for h in heads], axis=1)
                yi = jnp.concatenate([_dot_tn(da[h][rs, :], qth[h]) for h in heads], axis=1)
                xs.append(xi)
                dk = dk + yi * eks[i]
                dgk = dgk + yi * kts[i].astype(F32)
            dq = jnp.concatenate([x * e for x, e in zip(xs, eqs)], axis=0) + dq_inter
            dgq = jnp.concatenate([x * qt.astype(F32) for x, qt in zip(xs, qts)], axis=0)
            dg = dgq - dgk + q * dq_inter - kk * dk_state
            sdot = jnp.concatenate([jnp.sum(sts[h] * dsts[h], axis=0, keepdims=True) for h in heads], axis=1)
            dgl = jnp.sum(kk * dk_state, axis=0, keepdims=True) + dec * sdot
            dg = dg + jnp.where(last_row, dgl, 0.0)
            dlogf = _exact_tri_matmul(triu, dg)
            dfv = dlogf / f - dk
            dq_ref[sl, :] = (dq * (sq * (1.0 + hq * (1.0 - sq)))).astype(dq_ref.dtype)
            df_ref[sl, :] = (dfv * (1.0 - lb) * sg * (1.0 - sg)).astype(df_ref.dtype)
            dv_ref[sl, :] = jnp.concatenate(dv, axis=1).astype(dv_ref.dtype)
            dlb_ref[...] += jnp.sum(dfv * (1.0 - sg), axis=0, keepdims=True)
            dech = _hg_heads(dec, hp)
            for h in heads:
                dstate[h] = dsts[h] * dech[h] + new_dst[h]
            return carry

        lax.fori_loop(0, nchunk, chunk, 0)

    def col(cb):
        return pl.BlockSpec((tb, wp), lambda h, i: (nblk - 1 - i, cb * (nh // hp) + h))

    ocol = pl.BlockSpec((tb, wp), lambda h, i: (nblk - 1 - i, h))
    lbspec = pl.BlockSpec((1, wp), lambda h, i: (0, h))
    w = nh * HG_DK
    dq, df, dv, dlb = pl.pallas_call(
        body,
        out_shape=[jax.ShapeDtypeStruct((t, w), BF16)] * 3 + [jax.ShapeDtypeStruct((1, w), F32)],
        grid=(nh // hp, nblk),
        in_specs=[col(0), col(1), col(2), lbspec,
                  pl.BlockSpec((hp, nchunk, HG_DK, HG_DK), lambda h, i: (h, nblk - 1 - i, 0, 0)), ocol],
        out_specs=[ocol, ocol, ocol, lbspec],
        scratch_shapes=[pltpu.VMEM((hp, HG_DK, HG_DK), F32)],
        compiler_params=_cparams(("parallel", "arbitrary")),
        name="hgrn_bwd",
    )(zh, zh, zh, lb3.reshape(1, -1), states, d_o)
    return dq, df, dv, dlb.reshape(w)


NEG = -1e30
ATT_GW = ATT_HEADS * ATT_DH


def _att_scores(q, kp, kc, has_prev):
    scale = ATT_DH ** -0.5
    i = lax.broadcasted_iota(jnp.int32, (ATT_BLK, ATT_BLK), 0)
    j = lax.broadcasted_iota(jnp.int32, (ATT_BLK, ATT_BLK), 1)
    s_p = jnp.where(jnp.logical_and(j >= i, has_prev), _dot_nt(q, kp) * scale, NEG)
    s_c = jnp.where(j <= i, _dot_nt(q, kc) * scale, NEG)
    return s_p, s_c


def _att_views(arrs, d):
    return [a.reshape(d, -1, ATT_GW) for a in arrs]


def _att_unview(a, d):
    return a.reshape(-1, ATT_GW) if d == 1 else a


ATT_QB = 4


def _attn_fwd(qb, kb, vb, g):
    d = ATT_PATTERNS[g][1]
    q2, k2, v2 = _att_views([qb, kb, vb], d)
    nblk = q2.shape[1] // ATT_BLK
    nq = ATT_QB if nblk % ATT_QB == 0 else 1
    rows = nq * ATT_BLK

    def body(q_ref, kc_ref, kp_ref, vc_ref, vp_ref, o_ref, l_ref):
        first = pl.program_id(1) == 0
        hss = [slice(h * ATT_DH, (h + 1) * ATT_DH) for h in range(ATT_HEADS)]
        for b in range(nq):
            rs = slice(b * ATT_BLK, (b + 1) * ATT_BLK)
            ps = slice((b - 1) * ATT_BLK, b * ATT_BLK)
            has_prev = jnp.logical_not(first) if b == 0 else True
            kv = [(kp_ref[:, hs], vp_ref[:, hs]) if b == 0 else (kc_ref[ps, hs], vc_ref[ps, hs]) for hs in hss]
            sc = [_att_scores(q_ref[rs, hs], kv[h][0], kc_ref[rs, hs], has_prev) for h, hs in enumerate(hss)]
            ms = [jnp.maximum(jnp.max(s_p, axis=1, keepdims=True), jnp.max(s_c, axis=1, keepdims=True)) for s_p, s_c in sc]
            ps_ = [(jnp.exp(s_p - m), jnp.exp(s_c - m)) for (s_p, s_c), m in zip(sc, ms)]
            ls = [jnp.sum(p_p, axis=1, keepdims=True) + jnp.sum(p_c, axis=1, keepdims=True) for p_p, p_c in ps_]
            os_ = [_dot(p_p.astype(BF16), kv[h][1]) + _dot(p_c.astype(BF16), vc_ref[rs, hss[h]]) for h, (p_p, p_c) in enumerate(ps_)]
            for h, hs in enumerate(hss):
                o_ref[rs, hs] = os_[h] / ls[h]
                l_ref[rs, hs] = jnp.broadcast_to(ms[h] + jnp.log(ls[h]), (ATT_BLK, ATT_DH))

    cur = pl.BlockSpec((None, rows, ATT_GW), lambda r, n: (r, n, 0))
    prev = pl.BlockSpec((None, ATT_BLK, ATT_GW), lambda r, n: (r, jnp.maximum(n * nq - 1, 0), 0))
    o, lse = pl.pallas_call(
        body,
        out_shape=[jax.ShapeDtypeStruct(q2.shape, F32)] * 2,
        grid=(d, nblk // nq),
        in_specs=[cur, cur, prev, cur, prev],
        out_specs=[cur, cur],
        compiler_params=_cparams(("parallel", "arbitrary")),
        name=f"attn_fwd_g{g}",
    )(q2, k2, k2, v2, v2)
    return _att_unview(o, d), _att_unview(lse, d)


def _attn_bwd(qb, kb, vb, o, lse, d_o, d_lse, g):
    d = ATT_PATTERNS[g][1]
    q2, k2, v2 = _att_views([qb, kb, vb], d)
    o2, l2, do2, dl2 = _att_views([o, lse, d_o, d_lse], d)
    nblk = q2.shape[1] // ATT_BLK
    nq = ATT_QB if nblk % ATT_QB == 0 else 1
    rows = nq * ATT_BLK
    ns = nblk // nq
    scale = ATT_DH ** -0.5

    def body(q_ref, kc_ref, kp_ref, vc_ref, vp_ref, o_ref, l_ref, do_ref, dl_ref, dq_ref, dk_ref, dv_ref, ck, cv):
        n = pl.program_id(1)

        @pl.when(n == 0)
        def _():
            ck[...] = jnp.zeros_like(ck)
            cv[...] = jnp.zeros_like(cv)

        first = n == ns - 1
        hss = [slice(h * ATT_DH, (h + 1) * ATT_DH) for h in range(ATT_HEADS)]
        heads = range(ATT_HEADS)
        pend_k, pend_v = [ck[:, hs] for hs in hss], [cv[:, hs] for hs in hss]
        for b in reversed(range(nq)):
            rs = slice(b * ATT_BLK, (b + 1) * ATT_BLK)
            ps = slice((b - 1) * ATT_BLK, b * ATT_BLK)
            has_prev = jnp.logical_not(first) if b == 0 else True
            q = [q_ref[rs, hs] for hs in hss]
            kc, vc = [kc_ref[rs, hs] for hs in hss], [vc_ref[rs, hs] for hs in hss]
            kp = [kp_ref[:, hs] if b == 0 else kc_ref[ps, hs] for hs in hss]
            vp = [vp_ref[:, hs] if b == 0 else vc_ref[ps, hs] for hs in hss]
            sc = [_att_scores(q[h], kp[h], kc[h], has_prev) for h in heads]
            dob = [do_ref[rs, hs].astype(BF16) for hs in hss]
            dp = [(_dot_nt(dob[h], vp[h]), _dot_nt(dob[h], vc[h])) for h in heads]
            delta = [jnp.sum(do_ref[rs, hs] * o_ref[rs, hs] - dl_ref[rs, hs], axis=1, keepdims=True) for hs in hss]
            pr = [(jnp.exp(sc[h][0] - l_ref[rs, hss[h]][:, 0:1]), jnp.exp(sc[h][1] - l_ref[rs, hss[h]][:, 0:1])) for h in heads]
            ds = [((pr[h][0] * (dp[h][0] - delta[h]) * scale).astype(BF16), (pr[h][1] * (dp[h][1] - delta[h]) * scale).astype(BF16))
                  for h in heads]
            pb = [(pr[h][0].astype(BF16), pr[h][1].astype(BF16)) for h in heads]
            dq = [_dot(ds[h][0], kp[h]) + _dot(ds[h][1], kc[h]) for h in heads]
            dk_c = [_dot_tn(ds[h][1], q[h]) for h in heads]
            dv_c = [_dot_tn(pb[h][1], dob[h]) for h in heads]
            dk_p = [_dot_tn(ds[h][0], q[h]) for h in heads]
            dv_p = [_dot_tn(pb[h][0], dob[h]) for h in heads]
            for h, hs in enumerate(hss):
                dq_ref[rs, hs] = dq[h]
                dk_ref[rs, hs] = pend_k[h] + dk_c[h]
                dv_ref[rs, hs] = pend_v[h] + dv_c[h]
            pend_k, pend_v = dk_p, dv_p
        for h, hs in enumerate(hss):
            ck[:, hs] = pend_k[h]
            cv[:, hs] = pend_v[h]

    cur = pl.BlockSpec((None, rows, ATT_GW), lambda r, n: (r, ns - 1 - n, 0))
    prev = pl.BlockSpec((None, ATT_BLK, ATT_GW), lambda r, n: (r, jnp.maximum((ns - 1 - n) * nq - 1, 0), 0))
    shp = jax.ShapeDtypeStruct(q2.shape, F32)
    dq, dk, dv = pl.pallas_call(
        body,
        out_shape=[shp, shp, shp],
        grid=(d, ns),
        in_specs=[cur, cur, prev, cur, prev, cur, cur, cur, cur],
        out_specs=[cur, cur, cur],
        scratch_shapes=[pltpu.VMEM((ATT_BLK, ATT_GW), F32), pltpu.VMEM((ATT_BLK, ATT_GW), F32)],
        compiler_params=_cparams(("parallel", "arbitrary")),
        name=f"attn_bwd_g{g}",
    )(q2, k2, k2, v2, v2, o2, l2, do2, dl2)
    return _att_unview(dq, d), _att_unview(dk, d), _att_unview(dv, d)


def _attn_fwd_1blk(qb, kb, vb, g):
    d = ATT_PATTERNS[g][1]
    q2, k2, v2 = _att_views([qb, kb, vb], d)
    nb = q2.shape[1] // ATT_BLK

    def body(q_ref, kc_ref, kp_ref, vc_ref, vp_ref, o_ref, l_ref):
        has_prev = pl.program_id(1) > 0
        for h in range(ATT_HEADS):
            hs = slice(h * ATT_DH, (h + 1) * ATT_DH)
            s_p, s_c = _att_scores(q_ref[:, hs], kp_ref[:, hs], kc_ref[:, hs], has_prev)
            m = jnp.maximum(jnp.max(s_p, axis=1, keepdims=True), jnp.max(s_c, axis=1, keepdims=True))
            p_p, p_c = jnp.exp(s_p - m), jnp.exp(s_c - m)
            l = jnp.sum(p_p, axis=1, keepdims=True) + jnp.sum(p_c, axis=1, keepdims=True)
            o = _dot(p_p.astype(BF16), vp_ref[:, hs]) + _dot(p_c.astype(BF16), vc_ref[:, hs])
            o_ref[:, hs] = o / l
            l_ref[:, hs] = jnp.broadcast_to(m + jnp.log(l), (ATT_BLK, ATT_DH))

    cur = pl.BlockSpec((None, ATT_BLK, ATT_GW), lambda r, n: (r, n, 0))
    prev = pl.BlockSpec((None, ATT_BLK, ATT_GW), lambda r, n: (r, jnp.maximum(n - 1, 0), 0))
    o, lse = pl.pallas_call(
        body,
        out_shape=[jax.ShapeDtypeStruct(q2.shape, F32)] * 2,
        grid=(d, nb),
        in_specs=[cur, cur, prev, cur, prev],
        out_specs=[cur, cur],
        compiler_params=_cparams(("parallel", "arbitrary")),
        name=f"attn_fwd_g{g}",
    )(q2, k2, k2, v2, v2)
    return _att_unview(o, d), _att_unview(lse, d)


def _attn_bwd_1blk(qb, kb, vb, o, lse, d_o, d_lse, g):
    d = ATT_PATTERNS[g][1]
    q2, k2, v2 = _att_views([qb, kb, vb], d)
    o2, l2, do2, dl2 = _att_views([o, lse, d_o, d_lse], d)
    nb = q2.shape[1] // ATT_BLK

    def body(q_ref, kc_ref, kp_ref, vc_ref, vp_ref, o_ref, l_ref, do_ref, dl_ref, dq_ref, dk_ref, dv_ref, ck, cv):
        n = pl.program_id(1)
        active = n < nb

        @pl.when(n == 0)
        def _():
            ck[...] = jnp.zeros_like(ck)
            cv[...] = jnp.zeros_like(cv)

        @pl.when(jnp.logical_not(active))
        def _():
            dk_ref[...] = ck[...]
            dv_ref[...] = cv[...]

        @pl.when(active)
        def _():
            has_prev = n > 0
            for h in range(ATT_HEADS):
                hs = slice(h * ATT_DH, (h + 1) * ATT_DH)
                q, kp, kc, vp, vc = q_ref[:, hs], kp_ref[:, hs], kc_ref[:, hs], vp_ref[:, hs], vc_ref[:, hs]
                s_p, s_c = _att_scores(q, kp, kc, has_prev)
                lse_h = l_ref[:, hs][:, 0:1]
                p_p, p_c = jnp.exp(s_p - lse_h), jnp.exp(s_c - lse_h)
                do = do_ref[:, hs]
                delta = jnp.sum(do * o_ref[:, hs] - dl_ref[:, hs], axis=1, keepdims=True)
                dob = do.astype(BF16)
                scale = ATT_DH ** -0.5
                ds_p = (p_p * (_dot_nt(dob, vp) - delta) * scale).astype(BF16)
                ds_c = (p_c * (_dot_nt(dob, vc) - delta) * scale).astype(BF16)
                dq_ref[:, hs] = _dot(ds_p, kp) + _dot(ds_c, kc)
                dk_ref[:, hs] = ck[:, hs] + _dot_tn(ds_p, q)
                dv_ref[:, hs] = cv[:, hs] + _dot_tn(p_p.astype(BF16), dob)
                ck[:, hs] = _dot_tn(ds_c, q)
                cv[:, hs] = _dot_tn(p_c.astype(BF16), dob)

    def qn(n):
        return jnp.minimum(n, nb - 1)

    cur = pl.BlockSpec((None, ATT_BLK, ATT_GW), lambda r, n: (r, qn(n), 0))
    prev = pl.BlockSpec((None, ATT_BLK, ATT_GW), lambda r, n: (r, jnp.maximum(qn(n) - 1, 0), 0))
    behind = pl.BlockSpec((None, ATT_BLK, ATT_GW), lambda r, n: (r, jnp.maximum(n - 1, 0), 0))
    shp = jax.ShapeDtypeStruct(q2.shape, F32)
    dq, dk, dv = pl.pallas_call(
        body,
        out_shape=[shp, shp, shp],
        grid=(d, nb + 1),
        in_specs=[cur, cur, prev, cur, prev, cur, cur, cur, cur],
        out_specs=[cur, behind, behind],
        scratch_shapes=[pltpu.VMEM((ATT_BLK, ATT_GW), F32), pltpu.VMEM((ATT_BLK, ATT_GW), F32)],
        compiler_params=_cparams(("parallel", "arbitrary")),
        name=f"attn_bwd_g{g}",
    )(q2, k2, k2, v2, v2, o2, l2, do2, dl2)
    return _att_unview(dq, d), _att_unview(dk, d), _att_unview(dv, d)


def _rms_parts(x, width):
    outs = []
    for lo in range(0, x.shape[1], width):
        xs = x[:, lo:lo + width].astype(F32)
        r = lax.rsqrt(jnp.mean(xs * xs, axis=1, keepdims=True) + EPS)
        outs.append((xs * r, r))
    return outs


def _rms_bwd_part(xh, r, dxh):
    return r * (dxh - xh * jnp.mean(dxh * xh, axis=1, keepdims=True))


def _norm_pro(a, consts):
    (xh, _), = _rms_parts(a[0], a[0].shape[1])
    return [(xh * consts[0]).astype(BF16)]


def _norm_bwd_fin(accs, ex, consts):
    xv, dres = ex
    (xh, r), = _rms_parts(xv, xv.shape[1])
    dx = dres + _rms_bwd_part(xh, r, accs[0] * consts[0])
    return [dx, dx], [_colsum8(accs[0] * xh)]


def _norm_fwd(x, gain):
    d = x.shape[1]

    def fn(ins, consts):
        (xh, _), = _rms_parts(ins[0], d)
        return [xh * consts[0]], []

    (h,), _ = _rowwise(fn, [(x, d, 0)], [gain.reshape(1, d)], [(d, BF16)], [], bm=512, name="norm_fwd")
    return h


def _norm_bwd(x, gain, dh, dres):
    d = x.shape[1]

    def fn(ins, consts):
        (xh, r), = _rms_parts(ins[0], d)
        dx = ins[2] + _rms_bwd_part(xh, r, ins[1] * consts[0])
        return [dx], [_colsum8(ins[1] * xh)]

    (dx,), (dg,) = _rowwise(fn, [(x, d, 0), (dh, d, 0), (dres, d, 0)], [gain.reshape(1, d)], [(d, F32)], [d],
                            bm=512, name="norm_bwd")
    return dx, dg


def _rot_sign():
    lane = lax.broadcasted_iota(jnp.int32, (1, ATT_DH), 1)
    return jnp.where(lane < ATT_DH // 2, -1.0, 1.0).astype(F32)


def _rope(y, cos, sin):
    return y * cos + pltpu.roll(y, ATT_DH // 2, axis=1) * _rot_sign() * sin


def _rope_t(dy, cos, sin):
    return dy * cos - pltpu.roll(dy * sin, ATT_DH // 2, axis=1) * _rot_sign()


def _qk_prep(zq, zk, zv, qn, kn, cos, sin):
    w = zq.shape[1]

    def fn(ins, consts):
        cs, sn = ins[3], ins[4]
        outs = []
        for z, gain in ((ins[0], consts[0]), (ins[1], consts[1])):
            for i, (xh, _) in enumerate(_rms_parts(z, ATT_DH)):
                outs.append(_rope(xh * gain[:, i * ATT_DH:(i + 1) * ATT_DH], cs, sn))
        outs += [ins[2][:, i * ATT_DH:(i + 1) * ATT_DH] for i in range(w // ATT_DH)]
        groups = [jnp.concatenate(outs[i:i + ATT_HEADS], axis=1) for i in range(0, len(outs), ATT_HEADS)]
        return groups, []

    outs, _ = _rowwise(fn, [(zq, w, 0), (zk, w, 0), (zv, w, 0), (cos, ATT_DH, 0), (sin, ATT_DH, 0)], [qn, kn],
                       [(ATT_GW, BF16, ATT_PATTERNS[g][1]) for g in range(ATT_GROUPS)] * 3, [], bm=256, name="qk_prep")
    return outs[0:3], outs[3:6], outs[6:9]


def _qk_prep_bwd(zq, zk, dq_g, dk_g, dv_g, qn, kn, cos, sin):
    w = zq.shape[1]

    def fn(ins, consts):
        cs, sn = ins[2], ins[3]
        outs, sums = [], []
        for z, gain, dparts in ((ins[0], consts[0], ins[4:7]), (ins[1], consts[1], ins[7:10])):
            dout = jnp.concatenate(dparts, axis=1)
            dz, dgain = [], []
            for i, (xh, r) in enumerate(_rms_parts(z, ATT_DH)):
                hs = slice(i * ATT_DH, (i + 1) * ATT_DH)
                dy = _rope_t(dout[:, hs], cs, sn)
                dgain.append(_colsum8(dy * xh))
                dz.append(_rms_bwd_part(xh, r, dy * gain[:, hs]))
            outs.append(jnp.concatenate(dz, axis=1))
            sums.append(jnp.concatenate(dgain, axis=1))
        outs.append(jnp.concatenate(ins[10:13], axis=1))
        return outs, sums

    ins = [(zq, w, 0), (zk, w, 0), (cos, ATT_DH, 0), (sin, ATT_DH, 0)]
    for parts in (dq_g, dk_g, dv_g):
        ins += [(a, ATT_GW, 0, ATT_PATTERNS[g][1]) for g, a in enumerate(parts)]
    (dzq, dzk, dzv), (dqn, dkn) = _rowwise(fn, ins, [qn, kn], [(w, BF16)] * 3, [w, w], bm=256, name="qk_prep_bwd")
    return dzq, dzk, dzv, dqn, dkn


def _post_a(o_raw, zh, gout):
    w = o_raw.shape[1]

    def fn(ins, consts):
        oh = jnp.concatenate([xh for xh, _ in _rms_parts(ins[0], HG_DK)], axis=1)
        hg = ins[1]
        return [oh * consts[0] * (hg * _sigmoid(hg))], []

    (y,), _ = _rowwise(fn, [(o_raw, w, 0), (zh, w, 3)], [gout.reshape(1, w)], [(w, BF16)], [], bm=512, name="post_a")
    return y


def _post_a_bwd(o_raw, zh, gout, dy):
    w = o_raw.shape[1]

    def fn(ins, consts):
        parts = _rms_parts(ins[0], HG_DK)
        oh = jnp.concatenate([xh for xh, _ in parts], axis=1)
        hg, dyv, gain = ins[1], ins[2], consts[0]
        sg = _sigmoid(hg)
        s = hg * sg
        doh = dyv * gain * s
        do = jnp.concatenate([_rms_bwd_part(xh, r, doh[:, i * HG_DK:(i + 1) * HG_DK]) for i, (xh, r) in enumerate(parts)], axis=1)
        dhg = dyv * oh * gain * (sg * (1.0 + hg * (1.0 - sg)))
        return [do, dhg], [_colsum8(dyv * oh * s)]

    (do, dhg), (dgain,) = _rowwise(fn, [(o_raw, w, 0), (zh, w, 3), (dy, w, 0)], [gout.reshape(1, w)],
                                   [(w, F32), (w, BF16)], [w], bm=512, name="post_a_bwd")
    return do, dhg, dgain


def _merge_alpha(lses):
    m = jnp.maximum(jnp.maximum(lses[0], lses[1]), lses[2])
    e = [jnp.exp(l - m) for l in lses]
    inv = 1.0 / (e[0] + e[1] + e[2])
    return [x * inv for x in e]


def _group_ins(parts):
    return [(a, ATT_GW, 0, ATT_PATTERNS[g][1]) for g, a in enumerate(parts)]


def _merge_b(o_g, lse_g):
    def fn(ins, consts):
        al = _merge_alpha(ins[3:6])
        return [al[0] * ins[0] + al[1] * ins[1] + al[2] * ins[2]], []

    (y,), _ = _rowwise(fn, _group_ins(o_g) + _group_ins(lse_g), [], [(ATT_GW, BF16)], [], bm=512, name="merge_b")
    return y


def _merge_b_bwd(o_g, lse_g, dy):
    def fn(ins, consts):
        al = _merge_alpha(ins[3:6])
        dyv = ins[6]
        dal = [dyv * ins[i] for i in range(3)]
        tot = al[0] * dal[0] + al[1] * dal[1] + al[2] * dal[2]
        return [al[i] * dyv for i in range(3)] + [al[i] * (dal[i] - tot) for i in range(3)], []

    outs, _ = _rowwise(fn, _group_ins(o_g) + _group_ins(lse_g) + [(dy, ATT_GW, 0)], [],
                       [(ATT_GW, F32, ATT_PATTERNS[g][1]) for g in range(ATT_GROUPS)] * 2, [], bm=512, name="merge_b_bwd")
    return outs[:3], outs[3:]


def _loss_head(y, target):
    d = y.shape[1]

    def fn(ins, consts):
        e = ins[0] - ins[1]
        return [e * (1.0 / d)] * 2, [_colsum8(e * e)]

    (dy, dyb), (sq,) = _rowwise(fn, [(y, d, 0), (target, d, 0)], [], [(d, F32), (d, BF16)], [d], bm=512, name="loss_head")
    return 0.5 * jnp.sum(sq) / d, dy, dyb


def _silu_grad(a):
    s = _sigmoid(a)
    return s * (1.0 + a * (1.0 - s))


def _ffn_fwd(x, gain, wt, wo_fn, tag):
    t, d = x.shape
    f = wt.shape[0] // 2

    def act(accs, ex, consts):
        a, b = accs
        s = _sigmoid(a)
        sa = a * s
        return (sa * b, b, 0.5 * sa, 0.5 * (s + sa * (1.0 - s)))

    bn = FFN_BN if f % FFN_BN == 0 else 256
    u, b, sa, sp, h = _mm([x], [wt, wt], [(0, 0, 0), (0, 1, 1)], 2, act, [BF16] * 4, m=t, n=f, k=d, tb=True,
                          bm=512, bn=bn, bk=d, b_off=[(0, 0), (f // min(bn, f), 0)],
                          consts=[gain.reshape(1, d)], a_pro=_norm_pro, chunk=MXU_COLS, name=f"ffn_in_{tag}")
    wo = wo_fn(u)
    (y,) = _mm([u], [wo], [(0, 0, 0)], 1, lambda accs, ex: (ex[0] + 0.5 * accs[0],), [F32], m=t, n=d, k=f,
               bm=512, bn=d, bk=f, extras=[x], name=f"ffn_out_{tag}")
    return y, (x, h, u, b, sa, sp, wo)


def _ffn_bwd(dy, dyb, saved, gain, wt, tag, tok, emit):
    x, h, u, b, sa, sp, wo = saved
    t, d = x.shape
    f = wo.shape[0]

    def dact(accs, ex, consts):
        bv, sav, spv = (e.astype(F32) for e in ex)
        return (accs[0] * bv * spv, accs[0] * sav)

    bn = FFN_BN if f % FFN_BN == 0 else 256
    da, db = _mm([dyb], [wo], [(0, 0, 0)], 1, dact, [BF16, BF16], m=t, n=f, k=d, tb=True, bm=512, bn=bn, bk=d,
                 extras=[b, sa, sp], n_outer=True, chunk=MXU_COLS, consts=[tok], name=f"ffn_dact_{tag}")
    (dwo,) = _mm([u], [dyb], [(0, 0, 0)], 1, lambda accs, ex: (0.5 * accs[0],), [BF16], m=f, n=d, k=t, ta=True,
                 bm=1408, bn=d, bk=1024, name=f"ffn_dwo_{tag}")
    (dwt,) = _mm([da, db], [h], [(0, 0, 0)], 1, _first, [BF16], m=2 * f, n=d, k=t, ta=True, bm=min(1408, f), bn=d,
                 bk=1024, a_cat=True, name=f"ffn_dwt_{tag}")
    tok = emit(dwt, dwo)
    bk = min(FFN_BN, f)
    dx, dxb, dgain = _mm([da, db], [wt, wt], [(0, 0, 0), (1, 1, 0)], 1, _norm_bwd_fin, [F32, BF16], m=t, n=d, k=f,
                         bm=512, bn=d, bk=bk, b_off=[(0, 0), (0, f // bk)], extras=[x, dy],
                         consts=[gain.reshape(1, d), tok], n_sums=1, name=f"ffn_dh_{tag}")
    return dx, dxb, jnp.sum(dgain, axis=0), tok


FFN_BN = 2816
Z_SPLITS = (("h", 4096), ("q", 1536), ("k", 1536), ("v", 1536), ("g", 2048))


def _mix_fwd(x, p, cos, sin):
    t, d = x.shape
    z, off, hm = {}, 0, None
    for nm, width in Z_SPLITS:
        bn = 1024 if off % 1024 == 0 and width % 1024 == 0 else 512
        first = hm is None
        res = _mm([x if first else hm], [p["wint"]], [(0, 0, 0)], 1, (lambda accs, ex, consts: (accs[0],)) if first else _first,
                  [F32 if nm == "h" else BF16], m=t, n=width, k=d, tb=True, bm=1024, bn=bn, bk=d, b_off=[(off // bn, 0)],
                  consts=[p["gm"].reshape(1, d)] if first else (), a_pro=_norm_pro if first else None, name=f"mix_in_{nm}")
        z[nm] = res[0]
        hm = res[1] if first else hm
        off += width
    o_raw, states = _hgrn_fwd(z["h"], p["lb3"])
    qb, kb, vb = _qk_prep(z["q"], z["k"], z["v"], p["qn"], p["kn"], cos, sin)
    o_g, lse_g = zip(*[_attn_fwd(qb[g], kb[g], vb[g], g) for g in range(ATT_GROUPS)])
    oa = _post_a(o_raw, z["h"], p["gout"])
    ob = _merge_b(o_g, lse_g)
    late = p["late"](ob)
    p = dict(p, **late)
    (ya,) = _mm([oa], [p["wa"]], [(0, 0, 0)], 1, _first, [F32], m=t, n=d, k=oa.shape[1], bm=1024, bn=d, bk=oa.shape[1],
                name="branch_a")

    def gate(accs, ex):
        return (_sigmoid(ex[0].astype(F32)) * ex[2] + _sigmoid(ex[1].astype(F32)) * accs[0], accs[0])

    merged, yb = _mm([ob], [p["wbt"]], [(0, 0, 0)], 1, gate, [BF16, F32], m=t, n=d, k=ATT_GW, tb=True, bm=512, bn=d,
                     bk=ATT_GW, extras=[z["g"], z["g"], ya], e_off=[0, 1, 0], chunk=MXU_COLS, name="branch_b_gate")
    (y,) = _mm([merged], [p["wo"]], [(0, 0, 0)], 1, lambda accs, ex: (ex[0] + accs[0],), [F32], m=t, n=d, k=d,
               bm=1024, bn=d, bk=d, extras=[x], name="mix_out")
    return y, (x, hm, z, o_raw, states, qb, kb, vb, o_g, lse_g, oa, ob, ya, yb, merged, late)


def _mix_bwd(dy, dyb, saved, p, cos, sin, tok):
    x, hm, z, o_raw, states, qb, kb, vb, o_g, lse_g, oa, ob, ya, yb, merged, late = saved
    p = dict(p, **late)
    t, d = x.shape
    w = oa.shape[1]

    def dgate(accs, ex, consts):
        dm = accs[0]
        sa, sb = _sigmoid(ex[0].astype(F32)), _sigmoid(ex[1].astype(F32))
        return (sa * dm, sb * dm, dm * ex[2] * sa * (1.0 - sa), dm * ex[3] * sb * (1.0 - sb))

    dya, dyb_, dga, dgb = _mm([dyb], [p["wo"]], [(0, 0, 0)], 1, dgate, [BF16] * 4, m=t, n=d, k=d, tb=True, bm=512, bn=d,
                              bk=d, extras=[z["g"], z["g"], ya, yb], e_off=[0, 1, 0, 0], chunk=MXU_COLS, consts=[tok], name="mix_out_bwd")
    (dwo,) = _mm([merged], [dyb], [(0, 0, 0)], 1, _first, [BF16], m=d, n=d, k=t, ta=True, bm=d, bn=d, bk=1024, name="mix_dwo")
    (doa,) = _mm([dya], [p["wa"]], [(0, 0, 0)], 1, _first, [F32], m=t, n=w, k=d, tb=True, bm=1024, bn=w, bk=d, name="branch_a_bwd")
    (dwa,) = _mm([oa], [dya], [(0, 0, 0)], 1, _first, [BF16], m=w, n=d, k=t, ta=True, bm=w, bn=d, bk=1024, name="branch_a_dw")
    (dob,) = _mm([dyb_], [p["wbt"]], [(0, 0, 0)], 1, _first, [F32], m=t, n=ATT_GW, k=d, bm=1024, bn=ATT_GW, bk=d,
                 name="branch_b_bwd")
    (dwbt,) = _mm([dyb_], [ob], [(0, 0, 0)], 1, _first, [BF16], m=d, n=ATT_GW, k=t, ta=True, bm=d, bn=ATT_GW, bk=1024,
                  name="branch_b_dw")
    do_raw, dhg, dgout = _post_a_bwd(o_raw, z["h"], p["gout"], doa)
    do_g, dlse_g = _merge_b_bwd(o_g, lse_g, dob)
    dq_g, dk_g, dv_g = zip(*[_attn_bwd(qb[g], kb[g], vb[g], o_g[g], lse_g[g], do_g[g], dlse_g[g], g)
                             for g in range(ATT_GROUPS)])
    dzq, dzk, dzv, dqn, dkn = _qk_prep_bwd(z["q"], z["k"], dq_g, dk_g, dv_g, p["qn"], p["kn"], cos, sin)
    dhq, dhf, dhi, lbsum = _hgrn_bwd(z["h"], p["lb3"], states, do_raw)
    dz = [dhq, dhf, dhi, dhg, dzq, dzk, dzv, dga, dgb]
    pw = sum(a.shape[1] for a in dz)
    (dwint,) = _mm(dz, [hm], [(0, 0, 0)], 1, _first, [BF16], m=pw, n=d, k=t, ta=True, bm=512, bn=d, bk=1024, a_cat=True,
                   name="mix_in_dw")
    dx, dxb, dgm = _mm(dz, [p["wint"]], [(0, 0, 0)], 1, _norm_bwd_fin, [F32, BF16], m=t, n=d, k=pw, bm=512, bn=d, bk=512,
                       extras=[x, dy], consts=[p["gm"].reshape(1, d)], n_sums=1, a_cat=True, name="mix_in_bwd")
    return dx, dxb, dict(gm=jnp.sum(dgm, axis=0), wint=dwint, lbsum=lbsum, gout=dgout, qn=dqn, kn=dkn, wa=dwa, wbt=dwbt, wo=dwo)


def _rope_tables(t):
    pos = jnp.arange(t, dtype=F32)
    inv = ROPE_THETA ** (-jnp.arange(0, ATT_DH, 2, dtype=F32) / ATT_DH)
    ang = pos[:, None] * inv[None, :]
    ang = jnp.concatenate([ang, ang], axis=-1)
    return jnp.cos(ang), jnp.sin(ang)


def _lower_bounds(logits):
    lb = jnp.cumsum(jax.nn.softmax(logits, axis=0), axis=0)
    return lb - lb[0:1]


def _head_gain(g):
    return jnp.tile(g[:, None, :], (1, ATT_HEADS, 1)).reshape(1, ATT_GROUPS * ATT_GW)


SMALL_GRADS = ("ffn1_norm", "mix_norm", "lbsum", "hgrn_out_norm", "attn_q_norm", "attn_k_norm", "ffn2_norm")


def _local_step(x, target, small, fetch, emit):
    t = x.shape[0]
    depth = small["ffn1_norm"].shape[0]
    cos, sin = _rope_tables(t)
    lb_all = _lower_bounds(small["hgrn_lb_logits"])
    saved = []
    for l in range(depth):
        w1t = fetch("w1t", l, x)["w1t"]
        x, s1 = _ffn_fwd(x, small["ffn1_norm"][l], w1t, lambda after, l=l: fetch("w1o", l, after)["w1o"], "1")
        p = dict(gm=small["mix_norm"][l], wint=fetch("wint", l, x)["wint"], lb3=lb_all[l].reshape(-1, 1, HG_DK),
                 gout=small["hgrn_out_norm"][l], qn=_head_gain(small["attn_q_norm"][l]),
                 kn=_head_gain(small["attn_k_norm"][l]), late=functools.partial(fetch, "mout", l))
        x, sm = _mix_fwd(x, p, cos, sin)
        w2t = fetch("w2t", l, x)["w2t"]
        x, s2 = _ffn_fwd(x, small["ffn2_norm"][l], w2t, lambda after, l=l: fetch("w2o", l, after)["w2o"], "2")
        saved.append((p, w1t, w2t, s1, sm, s2))
    loss, dx, dxb = _loss_head(x, target)
    gsmall = {k: [None] * depth for k in SMALL_GRADS}
    tok = jnp.zeros((8, 128), F32)
    for l in reversed(range(depth)):
        p, w1t, w2t, s1, sm, s2 = saved[l]
        dx, dxb, gsmall["ffn2_norm"][l], tok = _ffn_bwd(
            dx, dxb, s2, small["ffn2_norm"][l], w2t, "2", tok, lambda dwt, dwo, l=l: emit("ffn2", l, dict(w2t=dwt, w2o=dwo), None))
        dx, dxb, gm = _mix_bwd(dx, dxb, sm, p, cos, sin, tok)
        tok = emit("mix", l, {k: gm[k] for k in ("wint", "wa", "wbt", "wo")}, None)
        gsmall["mix_norm"][l], gsmall["lbsum"][l], gsmall["hgrn_out_norm"][l] = gm["gm"], gm["lbsum"], gm["gout"]
        for k, src in (("attn_q_norm", "qn"), ("attn_k_norm", "kn")):
            gsmall[k][l] = jnp.sum(gm[src].reshape(ATT_GROUPS, ATT_HEADS, ATT_DH), axis=1)
        dx, dxb, gsmall["ffn1_norm"][l], tok = _ffn_bwd(
            dx, dxb, s1, small["ffn1_norm"][l], w1t, "1", tok, lambda dwt, dwo, l=l: emit("ffn1", l, dict(w1t=dwt, w1o=dwo), None))
    emit("small", 0, {}, ({k: jnp.stack(v) for k, v in gsmall.items()}, loss))
    return dx


_HBM = pl.BlockSpec(memory_space=pltpu.HBM)
_SEM = pl.BlockSpec(memory_space=pltpu.SEMAPHORE)
_EFFECT = pltpu.SideEffectType.DATAFLOW_SIDE_EFFECTING


def _peer(p):
    x, y, c = lax.axis_index("x"), lax.axis_index("y"), lax.axis_index("c")
    me = 4 * x + 2 * y + c
    return (1 - x if p & 4 else x, 1 - y if p & 2 else y, 1 - c if p & 1 else c), jnp.bitwise_xor(me, p), me


def _xchg_copy(src, land, mode, send_sems, recv_sems, k, p, arriving):
    peer, peer_id, me = _peer(p)
    block = src if mode == "gather" else src.at[peer_id]
    return pltpu.make_async_remote_copy(
        src_ref=block, dst_ref=land.at[peer_id if arriving else me], send_sem=send_sems.at[k * (N_DEV - 1) + p - 1],
        recv_sem=recv_sems.at[k * (N_DEV - 1) + p - 1], device_id=peer, device_id_type=MESH)


def _xchg_start(srcs, modes, groups, name):
    n, ng = len(srcs), len(groups)

    def body(*refs):
        src = refs[:n]
        sems = refs[n:n + 2 * ng]
        land = refs[n + 2 * ng + n:n + 2 * ng + 2 * n]
        token = refs[n + 2 * ng + 2 * n]
        for gi, idx in enumerate(groups):
            for ki, k in enumerate(idx):
                for p in range(1, N_DEV):
                    _xchg_copy(src[k], land[k], modes[k], sems[2 * gi], sems[2 * gi + 1], ki, p, False).start()
        token[...] = jnp.zeros_like(token)

    sem_shapes = []
    for idx in groups:
        sem_shapes += [pltpu.SemaphoreType.DMA((len(idx) * (N_DEV - 1),))] * 2
    outs = pl.pallas_call(
        body,
        out_shape=sem_shapes + [pltpu.HBM(a.shape, a.dtype) for a in srcs]
        + [pltpu.HBM((N_DEV,) + a.shape[-2:], a.dtype) for a in srcs] + [jax.ShapeDtypeStruct((8, 128), F32)],
        in_specs=[_HBM] * n,
        out_specs=[_SEM] * (2 * ng) + [_HBM] * (2 * n) + [pl.BlockSpec(memory_space=pltpu.VMEM)],
        input_output_aliases={i: 2 * ng + i for i in range(n)},
        compiler_params=pltpu.CompilerParams(has_side_effects=_EFFECT),
        name=name,
    )(*[pltpu.with_memory_space_constraint(a, pltpu.HBM) for a in srcs])
    sems = [(outs[2 * gi], outs[2 * gi + 1]) for gi in range(ng)]
    return sems, outs[2 * ng:2 * ng + n], outs[2 * ng + n:2 * ng + 2 * n], outs[-1]


def _xchg_wait_call(srcs, lands, modes, sems, after, name):
    n = len(srcs)

    def body(*refs):
        src, land = refs[:n], refs[n:2 * n]
        send_sems, recv_sems = refs[2 * n], refs[2 * n + 1]
        for p in range(1, N_DEV):
            for k in range(n):
                cp = _xchg_copy(src[k], land[k], modes[k], send_sems, recv_sems, k, p, True)
                cp.wait_send()
                cp.wait_recv()

    outs = pl.pallas_call(
        body,
        out_shape=[pltpu.HBM(a.shape, a.dtype) for a in list(srcs) + list(lands)],
        in_specs=[_HBM] * (2 * n) + [_SEM, _SEM, pl.BlockSpec(memory_space=pl.ANY)],
        out_specs=[_HBM] * (2 * n),
        input_output_aliases={i: i for i in range(2 * n)},
        compiler_params=pltpu.CompilerParams(has_side_effects=_EFFECT),
        name=name,
    )(*srcs, *lands, sems[0], sems[1], after)
    return outs[:n], outs[n:]


def _xchg_wait(srcs, lands, modes, sems, after, name):
    srcs, lands = _xchg_wait_call(srcs, lands, modes, sems, after, name)
    me = 4 * lax.axis_index("x") + 2 * lax.axis_index("y") + lax.axis_index("c")
    done = []
    for a, land, mode in zip(srcs, lands, modes):
        own = a[None] if mode == "gather" else lax.dynamic_slice_in_dim(a, me, 1, axis=0)
        done.append(lax.dynamic_update_slice(land, own, (me, 0, 0)))
    return done


def _sum_slots(land):
    g, _, r, c = land.shape
    br = r // 2 if (r % 32 == 0 and r >= 256) else r

    def body(l_ref, o_ref):
        acc = l_ref[0, 0].astype(F32)
        for j in range(1, N_DEV):
            acc = acc + l_ref[0, j].astype(F32)
        o_ref[0] = acc

    return pl.pallas_call(
        body,
        out_shape=jax.ShapeDtypeStruct((g, r, c), F32),
        grid=(g, r // br),
        in_specs=[pl.BlockSpec((1, N_DEV, br, c), lambda i, j: (i, 0, j, 0))],
        out_specs=pl.BlockSpec((1, br, c), lambda i, j: (i, j, 0)),
        compiler_params=_cparams(("parallel", "parallel")),
        name="sum_slots",
    )(land)


def _adamw(w, g, m, v):
    shape = w.shape
    cols = shape[-1]
    rows = int(np.prod(shape[:-1]))
    bm = max(b for b in range(8, 257, 8) if rows % b == 0) if rows % 8 == 0 else rows
    c1 = 1.0 - ADAM_B1 ** ADAM_STEP
    c2 = 1.0 - ADAM_B2 ** ADAM_STEP

    def fn(ins, consts):
        wv, gv, mv, vv = ins
        m2 = ADAM_B1 * mv + (1.0 - ADAM_B1) * gv
        v2 = ADAM_B2 * vv + (1.0 - ADAM_B2) * (gv * gv)
        delta = -ADAM_LR * ((m2 / c1) / (jnp.sqrt(v2 / c2) + ADAM_EPS) + ADAM_WD * wv)
        return [delta, m2, v2], []

    outs, _ = _rowwise(fn, [(a.reshape(rows, cols), cols, 0) for a in (w, g, m, v)], [], [(cols, F32)] * 3, [],
                       bm=bm, name="adamw")
    return [o.reshape(shape) for o in outs]


BIG = ("w1t", "w1o", "wint", "wa", "wbt", "wo", "w2t", "w2o")
FETCH_GROUPS = dict(w1t=("w1t",), w1o=("w1o",), wint=("wint",), mout=("wa", "wbt", "wo"), w2t=("w2t",), w2o=("w2o",))
SMALL_ROWS = (("ffn1_norm", 0), ("mix_norm", 2), ("lbsum", 4), ("hgrn_out_norm", 6), ("ffn2_norm", 8),
              ("attn_q_norm", 10), ("attn_k_norm", 12))
SMALL_PACK_ROWS = 16


def kernel(x, ffn1_norm, ffn1_w_in, ffn1_w_out, mix_norm, w_in, hgrn_lb_logits, hgrn_out_norm, attn_q_norm, attn_k_norm, w_branch_a, w_branch_b, w_out, ffn2_norm, ffn2_w_in, ffn2_w_out, loss_target, m_ffn1_norm, m_ffn1_w_in, m_ffn1_w_out, m_mix_norm, m_w_in, m_hgrn_lb_logits, m_hgrn_out_norm, m_attn_q_norm, m_attn_k_norm, m_w_branch_a, m_w_branch_b, m_w_out, m_ffn2_norm, m_ffn2_w_in, m_ffn2_w_out, v_ffn1_norm, v_ffn1_w_in, v_ffn1_w_out, v_mix_norm, v_w_in, v_hgrn_lb_logits, v_hgrn_out_norm, v_attn_q_norm, v_attn_k_norm, v_w_branch_a, v_w_branch_b, v_w_out, v_ffn2_norm, v_ffn2_w_in, v_ffn2_w_out):
    names = ("ffn1_norm", "ffn1_w_in", "ffn1_w_out", "mix_norm", "w_in", "hgrn_lb_logits", "hgrn_out_norm", "attn_q_norm",
             "attn_k_norm", "w_branch_a", "w_branch_b", "w_out", "ffn2_norm", "ffn2_w_in", "ffn2_w_out")
    w = dict(zip(names, (ffn1_norm, ffn1_w_in, ffn1_w_out, mix_norm, w_in, hgrn_lb_logits, hgrn_out_norm, attn_q_norm,
                         attn_k_norm, w_branch_a, w_branch_b, w_out, ffn2_norm, ffn2_w_in, ffn2_w_out)))
    m = dict(zip(names, (m_ffn1_norm, m_ffn1_w_in, m_ffn1_w_out, m_mix_norm, m_w_in, m_hgrn_lb_logits, m_hgrn_out_norm,
                         m_attn_q_norm, m_attn_k_norm, m_w_branch_a, m_w_branch_b, m_w_out, m_ffn2_norm, m_ffn2_w_in, m_ffn2_w_out)))
    v = dict(zip(names, (v_ffn1_norm, v_ffn1_w_in, v_ffn1_w_out, v_mix_norm, v_w_in, v_hgrn_lb_logits, v_hgrn_out_norm,
                         v_attn_q_norm, v_attn_k_norm, v_w_branch_a, v_w_branch_b, v_w_out, v_ffn2_norm, v_ffn2_w_in, v_ffn2_w_out)))
    depth, d = ffn1_norm.shape

    def tr(a):
        return jnp.swapaxes(a, 1, 2)

    shard = dict(w1t=tr(ffn1_w_in), w1o=ffn1_w_out, wint=tr(w_in), wa=w_branch_a,
                 wbt=tr(w_branch_b).reshape(depth, -1, d), wo=w_out, w2t=tr(ffn2_w_in), w2o=ffn2_w_out)
    order = [(g, l) for l in range(depth) for g in FETCH_GROUPS]
    started = {}
    for name, part in (("gather_start_first", order[:2]), ("gather_start_rest", order[2:])):
        flat = [(l, k) for g, l in part for k in FETCH_GROUPS[g]]
        groups, pos = [], 0
        for g, l in part:
            groups.append(list(range(pos, pos + len(FETCH_GROUPS[g]))))
            pos += len(FETCH_GROUPS[g])
        sems, srcs, lands, _ = _xchg_start([shard[k][l].astype(BF16) for l, k in flat], ["gather"] * len(flat), groups, name)
        for gi, key in enumerate(part):
            started[key] = ([srcs[i] for i in groups[gi]], [lands[i] for i in groups[gi]], sems[gi])

    def fetch(group, l, after):
        srcs, lands, sems = started[group, l]
        lands = _xchg_wait(srcs, lands, ["gather"] * len(srcs), sems, after, f"gather_wait_{group}{l}")
        out = {}
        for k, land in zip(FETCH_GROUPS[group], lands):
            out[k] = land.reshape(d, -1) if k == "wbt" else land.reshape(-1, d)
        return out

    pending = []

    def emit(group, l, g, final):
        keys = list(g)
        srcs = [g[k].reshape(N_DEV, -1, d) for k in keys]
        modes = ["scatter"] * len(keys)
        if final is not None:
            gsmall, loss = final
            pack = jnp.zeros((SMALL_PACK_ROWS, d), F32)
            for k, r0 in SMALL_ROWS:
                rows = gsmall[k].reshape(depth, -1)
                pack = pack.at[r0:r0 + depth, :rows.shape[1]].set(rows)
            srcs.append(pack.at[14, :].set(loss))
            modes.append("gather")
            keys.append("small")
        sems, s_thru, l_thru, token = _xchg_start(srcs, modes, [list(range(len(srcs)))], f"grads_start_{group}{l}")
        pending.append((group, l, keys, modes, sems[0], s_thru, l_thru))
        return token

    small = {k: w[k] for k in ("ffn1_norm", "mix_norm", "hgrn_lb_logits", "hgrn_out_norm", "attn_q_norm", "attn_k_norm", "ffn2_norm")}
    dx = _local_step(x[0], loss_target[0], small, fetch, emit)

    summed = {}
    for group, l, keys, modes, sems, s_thru, l_thru in pending:
        lands = _xchg_wait(s_thru, l_thru, modes, sems, dx, f"grads_wait_{group}{l}")
        for k, land in zip(keys, lands):
            summed[k, l] = _sum_slots(land[None])[0]
    gsum = {k: jnp.stack([summed[k, l] for l in range(depth)]) for k in BIG}
    tot = summed["small", 0]

    grads = {}
    for k, r0 in SMALL_ROWS:
        shp = (depth,) + (w[k].shape[1:] if k != "lbsum" else (d,))
        grads[k] = tot[r0:r0 + depth, :int(np.prod(shp[1:]))].reshape(shp)
    _, lb_vjp = jax.vjp(_lower_bounds, hgrn_lb_logits)
    grads["hgrn_lb_logits"] = lb_vjp(grads.pop("lbsum"))[0]
    grads["ffn1_w_in"], grads["ffn1_w_out"] = tr(gsum["w1t"]), gsum["w1o"]
    grads["w_in"], grads["w_branch_a"] = tr(gsum["wint"]), gsum["wa"]
    grads["w_branch_b"] = tr(gsum["wbt"].reshape(depth, d // N_DEV, -1))
    grads["w_out"] = gsum["wo"]
    grads["ffn2_w_in"], grads["ffn2_w_out"] = tr(gsum["w2t"]), gsum["w2o"]

    upd = {k: _adamw(w[k], grads[k], m[k], v[k]) for k in names}
    return (tot[14, 0], dx[None], *[grads[k] for k in names], *[upd[k][0] for k in names],
            *[upd[k][1] for k in names], *[upd[k][2] for k in names])
```

```python
import functools
import math

import jax
import jax.numpy as jnp
import numpy as np
from jax import lax
from jax.experimental import pallas as pl
from jax.experimental.pallas import tpu as pltpu

F32 = jnp.float32
BF16 = jnp.bfloat16

N_DEV = 8
EPS = 1e-6
HG_DK = 128
HG_CHUNK = 64
HG_SUB = 16
HG_HP = 8
ATT_PATTERNS = ((128, 1), (512, 4), (2048, 16))
ATT_GROUPS = 3
ATT_HEADS = 4
ATT_DH = 128
ATT_BLK = 128
ROPE_THETA = 10000.0
ADAM_LR, ADAM_B1, ADAM_B2, ADAM_EPS, ADAM_WD, ADAM_STEP = 0.001, 0.9, 0.999, 1e-08, 0.01, 10
VMEM_LIMIT_BYTES = 56 * 1024 * 1024
MXU_COLS = 256
MESH = pl.DeviceIdType.MESH


def _cparams(sem, **kw):
    return pltpu.CompilerParams(dimension_semantics=sem, vmem_limit_bytes=VMEM_LIMIT_BYTES, **kw)


def _sigmoid(x):
    return 1.0 / (1.0 + jnp.exp(-x))


def _mm(a_list, b_list, pairs, n_acc, fin, out_dtypes, *, m, n, k, ta=False, tb=False, bm, bn, bk,
        b_off=None, extras=(), e_off=None, n_outer=False, consts=(), a_pro=None, n_sums=0, chunk=0, a_cat=False, name):
    bm, bn, bk = min(bm, m), min(bn, n), min(bk, k)
    assert m % bm == 0 and n % bn == 0 and k % bk == 0, (name, m, n, k, bm, bn, bk)
    nk = k // bk
    assert not (a_pro and (nk > 1 or ta or n_outer)) and not (n_sums and (bn != n or n_outer)), name
    assert not (chunk and (nk > 1 or n_sums or chunk % 128)), name
    if a_cat:
        unit = bm if ta else bk
        widths = [a.shape[1] for a in a_list]
        assert all(w % unit == 0 for w in widths) and sum(widths) == (m if ta else k) and not a_pro, name
        cat_counts = [w // unit for w in widths]
        cat_starts = [sum(cat_counts[:i]) for i in range(len(widths))]
    b_off = b_off or [(0, 0)] * len(b_list)
    e_off = e_off or [0] * len(extras)
    na, nb, ne, nc, no = len(a_list), len(b_list), len(extras), len(consts), len(out_dtypes)
    nao = na if a_pro else 0
    dn = (((0,) if ta else (1,), (1,) if tb else (0,)), ((), ()))

    def body(*refs):
        refs = list(refs)
        a_refs, b_refs, e_refs, c_refs, o_refs, ao_refs, s_refs = (
            [refs.pop(0) for _ in range(cnt)] for cnt in (na, nb, ne, nc, no, nao, n_sums))
        acc_refs = refs
        kk = pl.program_id(2)
        first = pl.program_id(0) == 0
        cvals = [c[...] for c in c_refs]
        a_vals = [r[...] for r in a_refs]
        if a_cat:
            col = pl.program_id(1 if n_outer else 0) if ta else kk
            sel = a_vals[0]
            for start, v in zip(cat_starts[1:], a_vals[1:]):
                sel = jnp.where(col >= start, v, sel)
            a_vals = [sel]
        if a_pro:
            a_vals = a_pro(a_vals, cvals)
            for r, v in zip(ao_refs, a_vals):
                r[...] = v
        if chunk:
            spans = [slice(lo, min(lo + chunk, bn)) for lo in range(0, bn, chunk)]
            chunks = []
            for cs in spans:
                parts = [None] * n_acc
                for ai, bi, ci in pairs:
                    p = lax.dot_general(a_vals[ai], b_refs[bi][cs, :] if tb else b_refs[bi][:, cs], dn,
                                        preferred_element_type=F32)
                    parts[ci] = p if parts[ci] is None else parts[ci] + p
                chunks.append(parts)
            for cs, parts in zip(spans, chunks):
                ex = [e[:, cs] for e in e_refs]
                outs = fin(parts, ex, cvals) if nc else fin(parts, ex)
                for o_ref, o in zip(o_refs, outs):
                    o_ref[:, cs] = o.astype(o_ref.dtype)
            return

        parts = [None] * n_acc
        for ai, bi, ci in pairs:
            p = lax.dot_general(a_vals[ai], b_refs[bi][...], dn, preferred_element_type=F32)
            parts[ci] = p if parts[ci] is None else parts[ci] + p

        def finish(accs):
            ex = [e[...] for e in e_refs]
            res = fin(accs, ex, cvals) if nc else fin(accs, ex)
            outs, sums = res if n_sums else (res, ())
            for o_ref, o in zip(o_refs, outs):
                o_ref[...] = o.astype(o_ref.dtype)
            if n_sums:
                @pl.when(first)
                def _():
                    for s_ref, s in zip(s_refs, sums):
                        s_ref[...] = s

                @pl.when(jnp.logical_not(first))
                def _():
                    for s_ref, s in zip(s_refs, sums):
                        s_ref[...] += s

        if nk == 1:
            finish(parts)
        else:
            @pl.when(kk == 0)
            def _():
                for c in range(n_acc):
                    acc_refs[c][...] = parts[c]

            @pl.when(kk > 0)
            def _():
                for c in range(n_acc):
                    acc_refs[c][...] += parts[c]

            @pl.when(kk == nk - 1)
            def _():
                finish([acc_refs[c][...] for c in range(n_acc)])

    def ij(f):
        return (lambda j, i, q: f(i, j, q)) if n_outer else f

    a_spec = pl.BlockSpec((bk, bm), ij(lambda i, j, q: (q, i))) if ta else pl.BlockSpec((bm, bk), ij(lambda i, j, q: (i, q)))
    a_specs = [a_spec] * na
    if a_cat:
        def part_spec(start, count):
            def col(c):
                return jnp.clip(c - start, 0, count - 1)
            if ta:
                return pl.BlockSpec((bk, bm), ij(lambda i, j, q: (q, col(i))))
            return pl.BlockSpec((bm, bk), ij(lambda i, j, q: (i, col(q))))
        a_specs = [part_spec(s, c) for s, c in zip(cat_starts, cat_counts)]

    b_mode = dict(pipeline_mode=pl.Buffered(1)) if (bn == n and nk == 1) else {}

    def b_spec(off):
        on, ok = off
        if tb:
            return pl.BlockSpec((bn, bk), ij(lambda i, j, q: (j + on, q + ok)), **b_mode)
        return pl.BlockSpec((bk, bn), ij(lambda i, j, q: (q + ok, j + on)), **b_mode)

    mn_spec = pl.BlockSpec((bm, bn), ij(lambda i, j, q: (i, j)))
    outs = pl.pallas_call(
        body,
        out_shape=[jax.ShapeDtypeStruct((m, n), d) for d in out_dtypes] + [jax.ShapeDtypeStruct((m, k), BF16)] * nao
        + [jax.ShapeDtypeStruct((8, n), F32)] * n_sums,
        grid=(n // bn, m // bm, nk) if n_outer else (m // bm, n // bn, nk),
        in_specs=a_specs + [b_spec(o) for o in b_off]
        + [pl.BlockSpec((bm, bn), ij(lambda i, j, q, o=o: (i, j + o))) for o in e_off]
        + [pl.BlockSpec(c.shape, lambda *_, nd=c.ndim: (0,) * nd) for c in consts],
        out_specs=[mn_spec] * no + [a_spec] * nao + [pl.BlockSpec((8, n), lambda *_: (0, 0))] * n_sums,
        scratch_shapes=[pltpu.VMEM((bm, bn), F32) for _ in range(n_acc if nk > 1 else 0)],
        compiler_params=_cparams(("arbitrary" if n_sums else "parallel", "parallel", "arbitrary")),
        name=name,
    )(*a_list, *b_list, *extras, *consts)
    return outs


def _first(accs, ex):
    return (accs[0],)


def _rowwise(fn, ins, consts, out_defs, sum_widths, *, bm, name):
    ins = [tuple(e) + (1,) * (4 - len(e)) for e in ins]
    out_defs = [tuple(e) + (1,) * (3 - len(e)) for e in out_defs]
    t = ins[0][0].shape[-2] * ins[0][3]
    bm = min(bm, t)
    assert t % bm == 0, (name, t, bm)
    ni, nc, no, ns = len(ins), len(consts), len(out_defs), len(sum_widths)
    strided = [w for _, w, _, d in ins if d > 1] + [w for w, _, d in out_defs if d > 1]

    def body(*refs):
        i_refs, c_refs = refs[:ni], refs[ni:ni + nc]
        o_refs, s_refs = refs[ni + nc:ni + nc + no], refs[ni + nc + no:ni + nc + no + ns]
        scratch = list(refs[ni + nc + no + ns:])
        vals = []
        for ref, (_, w, _, d) in zip(i_refs, ins):
            if d == 1:
                vals.append(ref[...])
                continue
            s = scratch.pop(0)
            for r in range(d):
                for c in range(w // 128):
                    s.at[c][pl.ds(r, bm // d, stride=d), :] = ref[r, :, c * 128:(c + 1) * 128].astype(F32)
            vals.append(jnp.concatenate([s[c] for c in range(w // 128)], axis=1))
        outs, sums = fn(vals, [r[...] for r in c_refs])
        for o_ref, o, (w, _, d) in zip(o_refs, outs, out_defs):
            if d == 1:
                o_ref[...] = o.astype(o_ref.dtype)
                continue
            s = scratch.pop(0)
            for c in range(w // 128):
                s[c] = o[:, c * 128:(c + 1) * 128].astype(F32)
            for r in range(d):
                for c in range(w // 128):
                    o_ref[r, :, c * 128:(c + 1) * 128] = s.at[c][pl.ds(r, bm // d, stride=d), :].astype(o_ref.dtype)
        if ns:
            first = pl.program_id(0) == 0

            @pl.when(first)
            def _():
                for s_ref, s in zip(s_refs, sums):
                    s_ref[...] = s

            @pl.when(jnp.logical_not(first))
            def _():
                for s_ref, s in zip(s_refs, sums):
                    s_ref[...] += s

    def win(width, cb, d):
        if d > 1:
            return pl.BlockSpec((d, bm // d, width), lambda i: (0, i, 0))
        return pl.BlockSpec((bm, width), lambda i: (i, cb))

    res = pl.pallas_call(
        body,
        out_shape=[jax.ShapeDtypeStruct((t, w) if d == 1 else (d, t // d, w), dt) for w, dt, d in out_defs]
        + [jax.ShapeDtypeStruct((8, w), F32) for w in sum_widths],
        grid=(t // bm,),
        in_specs=[win(w, cb, d) for _, w, cb, d in ins] + [pl.BlockSpec(c.shape, lambda i, nd=c.ndim: (0,) * nd) for c in consts],
        out_specs=[win(w, 0, d) for w, _, d in out_defs] + [pl.BlockSpec((8, w), lambda i: (0, 0)) for w in sum_widths],
        scratch_shapes=[pltpu.VMEM((w // 128, bm, 128), F32) for w in strided],
        compiler_params=_cparams(("arbitrary",) if ns else ("parallel",)),
        name=name,
    )(*[e[0] for e in ins], *consts)
    return res[:no], [jnp.sum(s, axis=0) for s in res[no:]]


def _colsum8(x):
    bm, w = x.shape
    return jnp.sum(x.reshape(bm // 8, 8, w), axis=0)


def _tri(n, upper=False):
    r = lax.broadcasted_iota(jnp.int32, (n, n), 0)
    c = lax.broadcasted_iota(jnp.int32, (n, n), 1)
    return (c >= r) if upper else (c <= r)


def _exact_tri_matmul(tri_bf16, x):
    x0 = x.astype(BF16)
    r1 = x - x0.astype(F32)
    x1 = r1.astype(BF16)
    x2 = (r1 - x1.astype(F32)).astype(BF16)
    w = x.shape[1]
    y = jnp.dot(tri_bf16, jnp.concatenate([x0, x1, x2], axis=1), preferred_element_type=F32)
    return y[:, :w] + y[:, w:2 * w] + y[:, 2 * w:]


def _dot_nt(a, b):
    return lax.dot_general(a, b, (((1,), (1,)), ((), ())), preferred_element_type=F32)


def _dot_tn(a, b):
    return lax.dot_general(a, b, (((0,), (0,)), ((), ())), preferred_element_type=F32)


def _dot(a, b):
    return jnp.dot(a, b, preferred_element_type=F32)


def _hg_gates(hq, hf, lb):
    sq = _sigmoid(hq)
    q = hq * sq
    sg = _sigmoid(hf)
    f = lb + (1.0 - lb) * sg
    return q, sq, sg, f


def _hg_intra(q, kk, g):
    c = q.shape[0]
    rows = lax.broadcasted_iota(jnp.int32, (c, 1), 0)
    a_rows, qts, kts, eqs, eks = [], [], [], [], []
    for i in range(c // HG_SUB):
        lo = i * HG_SUB
        ref = g[lo - 1:lo, :] if i else jnp.zeros_like(g[0:1, :])
        eq = jnp.exp(g[lo:lo + HG_SUB, :] - ref)
        ek = jnp.exp(jnp.where(rows < lo + HG_SUB, ref - g, 0.0))
        qt = q[lo:lo + HG_SUB, :] * eq
        kt = kk * ek
        a = _dot_nt(qt.astype(BF16), kt.astype(BF16))
        tpos = lo + lax.broadcasted_iota(jnp.int32, (HG_SUB, c), 0)
        spos = lax.broadcasted_iota(jnp.int32, (HG_SUB, c), 1)
        a_rows.append(jnp.where(spos <= tpos, a, 0.0))
        qts.append(qt), kts.append(kt), eqs.append(eq), eks.append(ek)
    return jnp.concatenate(a_rows, axis=0), qts, kts, eqs, eks


def _hgrn_fwd_serial(zh, lb3, *, tb=512):
    t = zh.shape[0]
    nh = lb3.shape[0]
    c = HG_CHUNK
    tb = min(tb, t)
    nchunk = tb // c
    hp = HG_HP if nh % HG_HP == 0 else 1

    def body(hq_ref, hf_ref, hi_ref, lb_ref, o_ref, st_ref, state):
        @pl.when(pl.program_id(1) == 0)
        def _():
            state[...] = jnp.zeros_like(state)

        tril = _tri(c).astype(BF16)

        def one_head(hh, ci, sl):
            ls = slice(hh * HG_DK, (hh + 1) * HG_DK)
            q, _, _, f = _hg_gates(hq_ref[sl, ls], hf_ref[sl, ls], lb_ref[hh])
            v = hi_ref[sl, ls]
            kk = 1.0 - f
            g = _exact_tri_matmul(tril, jnp.log(f))
            a, _, _, _, _ = _hg_intra(q, kk, g)
            st = state[hh]
            st_ref[hh, ci] = st
            vb = v.astype(BF16)
            o = _dot(a.astype(BF16), vb) + _dot_nt((q * jnp.exp(g)).astype(BF16), st.astype(BF16))
            o_ref[sl, ls] = o
            glast = g[c - 1:c, :]
            kg = kk * jnp.exp(glast - g)
            state[hh] = st * jnp.exp(glast) + _dot_tn(vb, kg.astype(BF16))

        def chunk(ci, carry):
            sl = pl.ds(pl.multiple_of(ci * c, c), c)
            for hh in range(hp):
                one_head(hh, ci, sl)
            return carry

        lax.fori_loop(0, nchunk, chunk, 0)

    def col(cb):
        return pl.BlockSpec((tb, hp * HG_DK), lambda h, i: (i, cb * (nh // hp) + h))

    return pl.pallas_call(
        body,
        out_shape=[jax.ShapeDtypeStruct((t, nh * HG_DK), F32), jax.ShapeDtypeStruct((nh, t // c, HG_DK, HG_DK), F32)],
        grid=(nh // hp, t // tb),
        in_specs=[col(0), col(1), col(2), pl.BlockSpec((hp, 1, HG_DK), lambda h, i: (h, 0, 0))],
        out_specs=[pl.BlockSpec((tb, hp * HG_DK), lambda h, i: (i, h)),
                   pl.BlockSpec((hp, nchunk, HG_DK, HG_DK), lambda h, i: (h, i, 0, 0))],
        scratch_shapes=[pltpu.VMEM((hp, HG_DK, HG_DK), F32)],
        compiler_params=_cparams(("parallel", "arbitrary")),
        name="hgrn_fwd",
    )(zh, zh, zh, lb3)


def _hgrn_bwd_serial(zh, lb3, states, d_o, *, tb=512):
    t = zh.shape[0]
    nh = lb3.shape[0]
    c = HG_CHUNK
    tb = min(tb, t)
    nchunk = tb // c
    nblk = t // tb
    hp = HG_HP if nh % HG_HP == 0 else 1

    def body(hq_ref, hf_ref, hi_ref, lb_ref, st_ref, do_ref, dq_ref, df_ref, dv_ref, dlb_ref, dstate):
        @pl.when(pl.program_id(1) == 0)
        def _():
            dstate[...] = jnp.zeros_like(dstate)
            dlb_ref[...] = jnp.zeros_like(dlb_ref)

        tril = _tri(c).astype(BF16)
        triu = _tri(c, upper=True).astype(BF16)
        last_row = lax.broadcasted_iota(jnp.int32, (c, 1), 0) == c - 1

        def one_head(hh, ci, sl):
            ls = slice(hh * HG_DK, (hh + 1) * HG_DK)
            lb = lb_ref[hh]
            hq, hf = hq_ref[sl, ls], hf_ref[sl, ls]
            q, sq, sg, f = _hg_gates(hq, hf, lb)
            v = hi_ref[sl, ls]
            kk = 1.0 - f
            g = _exact_tri_matmul(tril, jnp.log(f))
            a, qts, kts, eqs, eks = _hg_intra(q, kk, g)
            st = st_ref[hh, ci]
            dst = dstate[hh]
            do = do_ref[sl, ls]
            dob, vb = do.astype(BF16), v.astype(BF16)
            glast = g[c - 1:c, :]
            eg = jnp.exp(g)
            egl = jnp.exp(glast - g)
            qg = q * eg
            kg = kk * egl
            dv = _dot_tn(a.astype(BF16), dob) + _dot_nt(kg.astype(BF16), dst.astype(BF16))
            da = jnp.where(_tri(c), _dot_nt(dob, vb), 0.0).astype(BF16)
            dq_parts, dgq_parts = [], []
            dk = jnp.zeros_like(kk)
            dgk = jnp.zeros_like(kk)
            for i in range(c // HG_SUB):
                da_i = da[i * HG_SUB:(i + 1) * HG_SUB, :]
                ktb, qtb = kts[i].astype(BF16), qts[i].astype(BF16)
                xi = _dot(da_i, ktb)
                yi = _dot_tn(da_i, qtb)
                dq_parts.append(xi * eqs[i])
                dk = dk + yi * eks[i]
                dgq_parts.append(xi * qtb.astype(F32))
                dgk = dgk + yi * ktb.astype(F32)
            dq_inter = _dot(dob, st.astype(BF16)) * eg
            dq = jnp.concatenate(dq_parts, axis=0) + dq_inter
            dk_state = _dot(vb, dst.astype(BF16)) * egl
            dk = dk + dk_state
            dg = jnp.concatenate(dgq_parts, axis=0) - dgk + q * dq_inter - kk * dk_state
            dgl = jnp.sum(kk * dk_state, axis=0, keepdims=True) + jnp.exp(glast) * jnp.sum(st * dst, axis=0, keepdims=True)
            dg = dg + jnp.where(last_row, dgl, 0.0)
            dlogf = _exact_tri_matmul(triu, dg)
            dfv = dlogf / f - dk
            dq_ref[sl, ls] = (dq * (sq * (1.0 + hq * (1.0 - sq)))).astype(dq_ref.dtype)
            df_ref[sl, ls] = (dfv * (1.0 - lb) * sg * (1.0 - sg)).astype(df_ref.dtype)
            dv_ref[sl, ls] = dv.astype(dv_ref.dtype)
            dlb_ref[hh] += jnp.sum(dfv * (1.0 - sg), axis=0, keepdims=True)
            dstate[hh] = dst * jnp.exp(glast) + _dot_tn(dob, qg.astype(BF16))

        def chunk(j, carry):
            ci = nchunk - 1 - j
            sl = pl.ds(pl.multiple_of(ci * c, c), c)
            for hh in range(hp):
                one_head(hh, ci, sl)
            return carry

        lax.fori_loop(0, nchunk, chunk, 0)

    def col(cb):
        return pl.BlockSpec((tb, hp * HG_DK), lambda h, i: (nblk - 1 - i, cb * (nh // hp) + h))

    ocol = pl.BlockSpec((tb, hp * HG_DK), lambda h, i: (nblk - 1 - i, h))
    w = nh * HG_DK
    dq, df, dv, dlb = pl.pallas_call(
        body,
        out_shape=[jax.ShapeDtypeStruct((t, w), BF16)] * 3 + [jax.ShapeDtypeStruct((nh, 1, HG_DK), F32)],
        grid=(nh // hp, nblk),
        in_specs=[col(0), col(1), col(2), pl.BlockSpec((hp, 1, HG_DK), lambda h, i: (h, 0, 0)),
                  pl.BlockSpec((hp, nchunk, HG_DK, HG_DK), lambda h, i: (h, nblk - 1 - i, 0, 0)), ocol],
        out_specs=[ocol, ocol, ocol, pl.BlockSpec((hp, 1, HG_DK), lambda h, i: (h, 0, 0))],
        scratch_shapes=[pltpu.VMEM((hp, HG_DK, HG_DK), F32)],
        compiler_params=_cparams(("parallel", "arbitrary")),
        name="hgrn_bwd",
    )(zh, zh, zh, lb3, states, d_o)
    return dq, df, dv, dlb.reshape(w)


def _hg_heads(x, hp):
    return [x[:, h * HG_DK:(h + 1) * HG_DK] for h in range(hp)]


def _hg_intra_wide(q, kk, g, hp):
    c = q.shape[0]
    rows = lax.broadcasted_iota(jnp.int32, (c, 1), 0)
    a_rows = [[] for _ in range(hp)]
    qts, kts, eqs, eks = [], [], [], []
    for i in range(c // HG_SUB):
        lo = i * HG_SUB
        ref = g[lo - 1:lo, :] if i else jnp.zeros_like(g[0:1, :])
        eq = jnp.exp(g[lo:lo + HG_SUB, :] - ref)
        ek = jnp.exp(jnp.where(rows < lo + HG_SUB, ref - g, 0.0))
        qtb = (q[lo:lo + HG_SUB, :] * eq).astype(BF16)
        ktb = (kk * ek).astype(BF16)
        tpos = lo + lax.broadcasted_iota(jnp.int32, (HG_SUB, c), 0)
        spos = lax.broadcasted_iota(jnp.int32, (HG_SUB, c), 1)
        for h, (qh, kh) in enumerate(zip(_hg_heads(qtb, hp), _hg_heads(ktb, hp))):
            a_rows[h].append(jnp.where(spos <= tpos, _dot_nt(qh, kh), 0.0))
        qts.append(qtb), kts.append(ktb), eqs.append(eq), eks.append(ek)
    return [jnp.concatenate(r, axis=0) for r in a_rows], qts, kts, eqs, eks


def _hgrn_fwd(zh, lb3, *, tb=512):
    t = zh.shape[0]
    nh = lb3.shape[0]
    c = HG_CHUNK
    tb = min(tb, t)
    nchunk = tb // c
    hp = HG_HP if nh % HG_HP == 0 else 1
    wp = hp * HG_DK

    def body(hq_ref, hf_ref, hi_ref, lb_ref, o_ref, st_ref, state):
        @pl.when(pl.program_id(1) == 0)
        def _():
            state[...] = jnp.zeros_like(state)

        tril = _tri(c).astype(BF16)

        def chunk(ci, carry):
            sl = pl.ds(pl.multiple_of(ci * c, c), c)
            q, _, _, f = _hg_gates(hq_ref[sl, :], hf_ref[sl, :], lb_ref[...])
            kk = 1.0 - f
            g = _exact_tri_matmul(tril, jnp.log(f))
            a, _, _, _, _ = _hg_intra_wide(q, kk, g, hp)
            vb = hi_ref[sl, :].astype(BF16)
            glast = g[c - 1:c, :]
            qgb = (q * jnp.exp(g)).astype(BF16)
            kgb = (kk * jnp.exp(glast - g)).astype(BF16)
            dec = jnp.exp(glast)
            sts = [state[h] for h in range(hp)]
            for h in range(hp):
                st_ref[h, ci] = sts[h]
            vh, qgh, kgh, dech = _hg_heads(vb, hp), _hg_heads(qgb, hp), _hg_heads(kgb, hp), _hg_heads(dec, hp)
            o = [_dot(a[h].astype(BF16), vh[h]) + _dot_nt(qgh[h], sts[h].astype(BF16)) for h in range(hp)]
            new = [_dot_tn(vh[h], kgh[h]) for h in range(hp)]
            o_ref[sl, :] = jnp.concatenate(o, axis=1)
            for h in range(hp):
                state[h] = sts[h] * dech[h] + new[h]
            return carry

        lax.fori_loop(0, nchunk, chunk, 0)

    def col(cb):
        return pl.BlockSpec((tb, wp), lambda h, i: (i, cb * (nh // hp) + h))

    return pl.pallas_call(
        body,
        out_shape=[jax.ShapeDtypeStruct((t, nh * HG_DK), F32), jax.ShapeDtypeStruct((nh, t // c, HG_DK, HG_DK), F32)],
        grid=(nh // hp, t // tb),
        in_specs=[col(0), col(1), col(2), pl.BlockSpec((1, wp), lambda h, i: (0, h))],
        out_specs=[pl.BlockSpec((tb, wp), lambda h, i: (i, h)),
                   pl.BlockSpec((hp, nchunk, HG_DK, HG_DK), lambda h, i: (h, i, 0, 0))],
        scratch_shapes=[pltpu.VMEM((hp, HG_DK, HG_DK), F32)],
        compiler_params=_cparams(("parallel", "arbitrary")),
        name="hgrn_fwd",
    )(zh, zh, zh, lb3.reshape(1, -1))


def _hgrn_bwd(zh, lb3, states, d_o, *, tb=512):
    t = zh.shape[0]
    nh = lb3.shape[0]
    c = HG_CHUNK
    tb = min(tb, t)
    nchunk = tb // c
    nblk = t // tb
    hp = HG_HP if nh % HG_HP == 0 else 1
    wp = hp * HG_DK

    def body(hq_ref, hf_ref, hi_ref, lb_ref, st_ref, do_ref, dq_ref, df_ref, dv_ref, dlb_ref, dstate):
        @pl.when(pl.program_id(1) == 0)
        def _():
            dstate[...] = jnp.zeros_like(dstate)
            dlb_ref[...] = jnp.zeros_like(dlb_ref)

        tril = _tri(c).astype(BF16)
        triu = _tri(c, upper=True).astype(BF16)
        last_row = lax.broadcasted_iota(jnp.int32, (c, 1), 0) == c - 1
        heads = range(hp)

        def chunk(j, carry):
            ci = nchunk - 1 - j
            sl = pl.ds(pl.multiple_of(ci * c, c), c)
            lb = lb_ref[...]
            hq, hf = hq_ref[sl, :], hf_ref[sl, :]
            q, sq, sg, f = _hg_gates(hq, hf, lb)
            kk = 1.0 - f
            g = _exact_tri_matmul(tril, jnp.log(f))
            a, qts, kts, eqs, eks = _hg_intra_wide(q, kk, g, hp)
            glast = g[c - 1:c, :]
            eg, egl, dec = jnp.exp(g), jnp.exp(glast - g), jnp.exp(glast)
            vb, dob = hi_ref[sl, :].astype(BF16), do_ref[sl, :].astype(BF16)
            qgb, kgb = (q * eg).astype(BF16), (kk * egl).astype(BF16)
            sts = [st_ref[h, ci] for h in heads]
            dsts = [dstate[h] for h in heads]
            stb, dstb = [s.astype(BF16) for s in sts], [s.astype(BF16) for s in dsts]
            vh, doh, qgh, kgh = _hg_heads(vb, hp), _hg_heads(dob, hp), _hg_heads(qgb, hp), _hg_heads(kgb, hp)
            dv = [_dot_tn(a[h].astype(BF16), doh[h]) + _dot_nt(kgh[h], dstb[h]) for h in heads]
            da = [jnp.where(_tri(c), _dot_nt(doh[h], vh[h]), 0.0).astype(BF16) for h in heads]
            dq_inter = jnp.concatenate([_dot(doh[h], stb[h]) for h in heads], axis=1) * eg
            dk_state = jnp.concatenate([_dot(vh[h], dstb[h]) for h in heads], axis=1) * egl
            new_dst = [_dot_tn(doh[h], qgh[h]) for h in heads]
            xs, dk, dgk = [], dk_state, 0.0
            for i in range(c // HG_SUB):
                rs = slice(i * HG_SUB, (i + 1) * HG_SUB)
                kth, qth = _hg_heads(kts[i], hp), _hg_heads(qts[i], hp)
                xi = jnp.concatenate([_dot(da[h][rs, :], kth[h]) for h in heads], axis=1)
                yi = jnp.concatenate([_dot_tn(da[h][rs, :], qth[h]) for h in heads], axis=1)
                xs.append(xi)
                dk = dk + yi * eks[i]
                dgk = dgk + yi * kts[i].astype(F32)
            dq = jnp.concatenate([x * e for x, e in zip(xs, eqs)], axis=0) + dq_inter
            dgq = jnp.concatenate([x * qt.astype(F32) for x, qt in zip(xs, qts)], axis=0)
            dg = dgq - dgk + q * dq_inter - kk * dk_state
            sdot = jnp.concatenate([jnp.sum(sts[h] * dsts[h], axis=0, keepdims=True) for h in heads], axis=1)
            dgl = jnp.sum(kk * dk_state, axis=0, keepdims=True) + dec * sdot
            dg = dg + jnp.where(last_row, dgl, 0.0)
            dlogf = _exact_tri_matmul(triu, dg)
            dfv = dlogf / f - dk
            dq_ref[sl, :] = (dq * (sq * (1.0 + hq * (1.0 - sq)))).astype(dq_ref.dtype)
            df_ref[sl, :] = (dfv * (1.0 - lb) * sg * (1.0 - sg)).astype(df_ref.dtype)
            dv_ref[sl, :] = jnp.concatenate(dv, axis=1).astype(dv_ref.dtype)
            dlb_ref[...] += jnp.sum(dfv * (1.0 - sg), axis=0, keepdims=True)
            dech = _hg_heads(dec, hp)
            for h in heads:
                dstate[h] = dsts[h] * dech[h] + new_dst[h]
            return carry

        lax.fori_loop(0, nchunk, chunk, 0)

    def col(cb):
        return pl.BlockSpec((tb, wp), lambda h, i: (nblk - 1 - i, cb * (nh // hp) + h))

    ocol = pl.BlockSpec((tb, wp), lambda h, i: (nblk - 1 - i, h))
    lbspec = pl.BlockSpec((1, wp), lambda h, i: (0, h))
    w = nh * HG_DK
    dq, df, dv, dlb = pl.pallas_call(
        body,
        out_shape=[jax.ShapeDtypeStruct((t, w), BF16)] * 3 + [jax.ShapeDtypeStruct((1, w), F32)],
        grid=(nh // hp, nblk),
        in_specs=[col(0), col(1), col(2), lbspec,
                  pl.BlockSpec((hp, nchunk, HG_DK, HG_DK), lambda h, i: (h, nblk - 1 - i, 0, 0)), ocol],
        out_specs=[ocol, ocol, ocol, lbspec],
        scratch_shapes=[pltpu.VMEM((hp, HG_DK, HG_DK), F32)],
        compiler_params=_cparams(("parallel", "arbitrary")),
        name="hgrn_bwd",
    )(zh, zh, zh, lb3.reshape(1, -1), states, d_o)
    return dq, df, dv, dlb.reshape(w)


NEG = -1e30
ATT_GW = ATT_HEADS * ATT_DH


def _att_scores(q, kp, kc, has_prev):
    scale = ATT_DH ** -0.5
    i = lax.broadcasted_iota(jnp.int32, (ATT_BLK, ATT_BLK), 0)
    j = lax.broadcasted_iota(jnp.int32, (ATT_BLK, ATT_BLK), 1)
    s_p = jnp.where(jnp.logical_and(j >= i, has_prev), _dot_nt(q, kp) * scale, NEG)
    s_c = jnp.where(j <= i, _dot_nt(q, kc) * scale, NEG)
    return s_p, s_c


def _att_views(arrs, d):
    return [a.reshape(d, -1, ATT_GW) for a in arrs]


def _att_unview(a, d):
    return a.reshape(-1, ATT_GW) if d == 1 else a


ATT_QB = 4


def _attn_fwd(qb, kb, vb, g):
    d = ATT_PATTERNS[g][1]
    q2, k2, v2 = _att_views([qb, kb, vb], d)
    nblk = q2.shape[1] // ATT_BLK
    nq = ATT_QB if nblk % ATT_QB == 0 else 1
    rows = nq * ATT_BLK

    def body(q_ref, kc_ref, kp_ref, vc_ref, vp_ref, o_ref, l_ref):
        first = pl.program_id(1) == 0
        hss = [slice(h * ATT_DH, (h + 1) * ATT_DH) for h in range(ATT_HEADS)]
        for b in range(nq):
            rs = slice(b * ATT_BLK, (b + 1) * ATT_BLK)
            ps = slice((b - 1) * ATT_BLK, b * ATT_BLK)
            has_prev = jnp.logical_not(first) if b == 0 else True
            kv = [(kp_ref[:, hs], vp_ref[:, hs]) if b == 0 else (kc_ref[ps, hs], vc_ref[ps, hs]) for hs in hss]
            sc = [_att_scores(q_ref[rs, hs], kv[h][0], kc_ref[rs, hs], has_prev) for h, hs in enumerate(hss)]
            ms = [jnp.maximum(jnp.max(s_p, axis=1, keepdims=True), jnp.max(s_c, axis=1, keepdims=True)) for s_p, s_c in sc]
            ps_ = [(jnp.exp(s_p - m), jnp.exp(s_c - m)) for (s_p, s_c), m in zip(sc, ms)]
            ls = [jnp.sum(p_p, axis=1, keepdims=True) + jnp.sum(p_c, axis=1, keepdims=True) for p_p, p_c in ps_]
            os_ = [_dot(p_p.astype(BF16), kv[h][1]) + _dot(p_c.astype(BF16), vc_ref[rs, hss[h]]) for h, (p_p, p_c) in enumerate(ps_)]
            for h, hs in enumerate(hss):
                o_ref[rs, hs] = os_[h] / ls[h]
                l_ref[rs, hs] = jnp.broadcast_to(ms[h] + jnp.log(ls[h]), (ATT_BLK, ATT_DH))

    cur = pl.BlockSpec((None, rows, ATT_GW), lambda r, n: (r, n, 0))
    prev = pl.BlockSpec((None, ATT_BLK, ATT_GW), lambda r, n: (r, jnp.maximum(n * nq - 1, 0), 0))
    o, lse = pl.pallas_call(
        body,
        out_shape=[jax.ShapeDtypeStruct(q2.shape, F32)] * 2,
        grid=(d, nblk // nq),
        in_specs=[cur, cur, prev, cur, prev],
        out_specs=[cur, cur],
        compiler_params=_cparams(("parallel", "arbitrary")),
        name=f"attn_fwd_g{g}",
    )(q2, k2, k2, v2, v2)
    return _att_unview(o, d), _att_unview(lse, d)


def _attn_bwd(qb, kb, vb, o, lse, d_o, d_lse, g):
    d = ATT_PATTERNS[g][1]
    q2, k2, v2 = _att_views([qb, kb, vb], d)
    o2, l2, do2, dl2 = _att_views([o, lse, d_o, d_lse], d)
    nblk = q2.shape[1] // ATT_BLK
    nq = ATT_QB if nblk % ATT_QB == 0 else 1
    rows = nq * ATT_BLK
    ns = nblk // nq
    scale = ATT_DH ** -0.5

    def body(q_ref, kc_ref, kp_ref, vc_ref, vp_ref, o_ref, l_ref, do_ref, dl_ref, dq_ref, dk_ref, dv_ref, ck, cv):
        n = pl.program_id(1)

        @pl.when(n == 0)
        def _():
            ck[...] = jnp.zeros_like(ck)
            cv[...] = jnp.zeros_like(cv)

        first = n == ns - 1
        hss = [slice(h * ATT_DH, (h + 1) * ATT_DH) for h in range(ATT_HEADS)]
        heads = range(ATT_HEADS)
        pend_k, pend_v = [ck[:, hs] for hs in hss], [cv[:, hs] for hs in hss]
        for b in reversed(range(nq)):
            rs = slice(b * ATT_BLK, (b + 1) * ATT_BLK)
            ps = slice((b - 1) * ATT_BLK, b * ATT_BLK)
            has_prev = jnp.logical_not(first) if b == 0 else True
            q = [q_ref[rs, hs] for hs in hss]
            kc, vc = [kc_ref[rs, hs] for hs in hss], [vc_ref[rs, hs] for hs in hss]
            kp = [kp_ref[:, hs] if b == 0 else kc_ref[ps, hs] for hs in hss]
            vp = [vp_ref[:, hs] if b == 0 else vc_ref[ps, hs] for hs in hss]
            sc = [_att_scores(q[h], kp[h], kc[h], has_prev) for h in heads]
            dob = [do_ref[rs, hs].astype(BF16) for hs in hss]
            dp = [(_dot_nt(dob[h], vp[h]), _dot_nt(dob[h], vc[h])) for h in heads]
            delta = [jnp.sum(do_ref[rs, hs] * o_ref[rs, hs] - dl_ref[rs, hs], axis=1, keepdims=True) for hs in hss]
            pr = [(jnp.exp(sc[h][0] - l_ref[rs, hss[h]][:, 0:1]), jnp.exp(sc[h][1] - l_ref[rs, hss[h]][:, 0:1])) for h in heads]
            ds = [((pr[h][0] * (dp[h][0] - delta[h]) * scale).astype(BF16), (pr[h][1] * (dp[h][1] - delta[h]) * scale).astype(BF16))
                  for h in heads]
            pb = [(pr[h][0].astype(BF16), pr[h][1].astype(BF16)) for h in heads]
            dq = [_dot(ds[h][0], kp[h]) + _dot(ds[h][1], kc[h]) for h in heads]
            dk_c = [_dot_tn(ds[h][1], q[h]) for h in heads]
            dv_c = [_dot_tn(pb[h][1], dob[h]) for h in heads]
            dk_p = [_dot_tn(ds[h][0], q[h]) for h in heads]
            dv_p = [_dot_tn(pb[h][0], dob[h]) for h in heads]
            for h, hs in enumerate(hss):
                dq_ref[rs, hs] = dq[h]
                dk_ref[rs, hs] = pend_k[h] + dk_c[h]
                dv_ref[rs, hs] = pend_v[h] + dv_c[h]
            pend_k, pend_v = dk_p, dv_p
        for h, hs in enumerate(hss):
            ck[:, hs] = pend_k[h]
            cv[:, hs] = pend_v[h]

    cur = pl.BlockSpec((None, rows, ATT_GW), lambda r, n: (r, ns - 1 - n, 0))
    prev = pl.BlockSpec((None, ATT_BLK, ATT_GW), lambda r, n: (r, jnp.maximum((ns - 1 - n) * nq - 1, 0), 0))
    shp = jax.ShapeDtypeStruct(q2.shape, F32)
    dq, dk, dv = pl.pallas_call(
        body,
        out_shape=[shp, shp, shp],
        grid=(d, ns),
        in_specs=[cur, cur, prev, cur, prev, cur, cur, cur, cur],
        out_specs=[cur, cur, cur],
        scratch_shapes=[pltpu.VMEM((ATT_BLK, ATT_GW), F32), pltpu.VMEM((ATT_BLK, ATT_GW), F32)],
        compiler_params=_cparams(("parallel", "arbitrary")),
        name=f"attn_bwd_g{g}",
    )(q2, k2, k2, v2, v2, o2, l2, do2, dl2)
    return _att_unview(dq, d), _att_unview(dk, d), _att_unview(dv, d)


def _attn_fwd_1blk(qb, kb, vb, g):
    d = ATT_PATTERNS[g][1]
    q2, k2, v2 = _att_views([qb, kb, vb], d)
    nb = q2.shape[1] // ATT_BLK

    def body(q_ref, kc_ref, kp_ref, vc_ref, vp_ref, o_ref, l_ref):
        has_prev = pl.program_id(1) > 0
        for h in range(ATT_HEADS):
            hs = slice(h * ATT_DH, (h + 1) * ATT_DH)
            s_p, s_c = _att_scores(q_ref[:, hs], kp_ref[:, hs], kc_ref[:, hs], has_prev)
            m = jnp.maximum(jnp.max(s_p, axis=1, keepdims=True), jnp.max(s_c, axis=1, keepdims=True))
            p_p, p_c = jnp.exp(s_p - m), jnp.exp(s_c - m)
            l = jnp.sum(p_p, axis=1, keepdims=True) + jnp.sum(p_c, axis=1, keepdims=True)
            o = _dot(p_p.astype(BF16), vp_ref[:, hs]) + _dot(p_c.astype(BF16), vc_ref[:, hs])
            o_ref[:, hs] = o / l
            l_ref[:, hs] = jnp.broadcast_to(m + jnp.log(l), (ATT_BLK, ATT_DH))

    cur = pl.BlockSpec((None, ATT_BLK, ATT_GW), lambda r, n: (r, n, 0))
    prev = pl.BlockSpec((None, ATT_BLK, ATT_GW), lambda r, n: (r, jnp.maximum(n - 1, 0), 0))
    o, lse = pl.pallas_call(
        body,
        out_shape=[jax.ShapeDtypeStruct(q2.shape, F32)] * 2,
        grid=(d, nb),
        in_specs=[cur, cur, prev, cur, prev],
        out_specs=[cur, cur],
        compiler_params=_cparams(("parallel", "arbitrary")),
        name=f"attn_fwd_g{g}",
    )(q2, k2, k2, v2, v2)
    return _att_unview(o, d), _att_unview(lse, d)


def _attn_bwd_1blk(qb, kb, vb, o, lse, d_o, d_lse, g):
    d = ATT_PATTERNS[g][1]
    q2, k2, v2 = _att_views([qb, kb, vb], d)
    o2, l2, do2, dl2 = _att_views([o, lse, d_o, d_lse], d)
    nb = q2.shape[1] // ATT_BLK

    def body(q_ref, kc_ref, kp_ref, vc_ref, vp_ref, o_ref, l_ref, do_ref, dl_ref, dq_ref, dk_ref, dv_ref, ck, cv):
        n = pl.program_id(1)
        active = n < nb

        @pl.when(n == 0)
        def _():
            ck[...] = jnp.zeros_like(ck)
            cv[...] = jnp.zeros_like(cv)

        @pl.when(jnp.logical_not(active))
        def _():
            dk_ref[...] = ck[...]
            dv_ref[...] = cv[...]

        @pl.when(active)
        def _():
            has_prev = n > 0
            for h in range(ATT_HEADS):
                hs = slice(h * ATT_DH, (h + 1) * ATT_DH)
                q, kp, kc, vp, vc = q_ref[:, hs], kp_ref[:, hs], kc_ref[:, hs], vp_ref[:, hs], vc_ref[:, hs]
                s_p, s_c = _att_scores(q, kp, kc, has_prev)
                lse_h = l_ref[:, hs][:, 0:1]
                p_p, p_c = jnp.exp(s_p - lse_h), jnp.exp(s_c - lse_h)
                do = do_ref[:, hs]
                delta = jnp.sum(do * o_ref[:, hs] - dl_ref[:, hs], axis=1, keepdims=True)
                dob = do.astype(BF16)
                scale = ATT_DH ** -0.5
                ds_p = (p_p * (_dot_nt(dob, vp) - delta) * scale).astype(BF16)
                ds_c = (p_c * (_dot_nt(dob, vc) - delta) * scale).astype(BF16)
                dq_ref[:, hs] = _dot(ds_p, kp) + _dot(ds_c, kc)
                dk_ref[:, hs] = ck[:, hs] + _dot_tn(ds_p, q)
                dv_ref[:, hs] = cv[:, hs] + _dot_tn(p_p.astype(BF16), dob)
                ck[:, hs] = _dot_tn(ds_c, q)
                cv[:, hs] = _dot_tn(p_c.astype(BF16), dob)

    def qn(n):
        return jnp.minimum(n, nb - 1)

    cur = pl.BlockSpec((None, ATT_BLK, ATT_GW), lambda r, n: (r, qn(n), 0))
    prev = pl.BlockSpec((None, ATT_BLK, ATT_GW), lambda r, n: (r, jnp.maximum(qn(n) - 1, 0), 0))
    behind = pl.BlockSpec((None, ATT_BLK, ATT_GW), lambda r, n: (r, jnp.maximum(n - 1, 0), 0))
    shp = jax.ShapeDtypeStruct(q2.shape, F32)
    dq, dk, dv = pl.pallas_call(
        body,
        out_shape=[shp, shp, shp],
        grid=(d, nb + 1),
        in_specs=[cur, cur, prev, cur, prev, cur, cur, cur, cur],
        out_specs=[cur, behind, behind],
        scratch_shapes=[pltpu.VMEM((ATT_BLK, ATT_GW), F32), pltpu.VMEM((ATT_BLK, ATT_GW), F32)],
        compiler_params=_cparams(("parallel", "arbitrary")),
        name=f"attn_bwd_g{g}",
    )(q2, k2, k2, v2, v2, o2, l2, do2, dl2)
    return _att_unview(dq, d), _att_unview(dk, d), _att_unview(dv, d)


def _rms_parts(x, width):
    outs = []
    for lo in range(0, x.shape[1], width):
        xs = x[:, lo:lo + width].astype(F32)
        r = lax.rsqrt(jnp.mean(xs * xs, axis=1, keepdims=True) + EPS)
        outs.append((xs * r, r))
    return outs


def _rms_bwd_part(xh, r, dxh):
    return r * (dxh - xh * jnp.mean(dxh * xh, axis=1, keepdims=True))


def _norm_pro(a, consts):
    (xh, _), = _rms_parts(a[0], a[0].shape[1])
    return [(xh * consts[0]).astype(BF16)]


def _norm_bwd_fin(accs, ex, consts):
    xv, dres = ex
    (xh, r), = _rms_parts(xv, xv.shape[1])
    dx = dres + _rms_bwd_part(xh, r, accs[0] * consts[0])
    return [dx, dx], [_colsum8(accs[0] * xh)]


def _norm_fwd(x, gain):
    d = x.shape[1]

    def fn(ins, consts):
        (xh, _), = _rms_parts(ins[0], d)
        return [xh * consts[0]], []

    (h,), _ = _rowwise(fn, [(x, d, 0)], [gain.reshape(1, d)], [(d, BF16)], [], bm=512, name="norm_fwd")
    return h


def _norm_bwd(x, gain, dh, dres):
    d = x.shape[1]

    def fn(ins, consts):
        (xh, r), = _rms_parts(ins[0], d)
        dx = ins[2] + _rms_bwd_part(xh, r, ins[1] * consts[0])
        return [dx], [_colsum8(ins[1] * xh)]

    (dx,), (dg,) = _rowwise(fn, [(x, d, 0), (dh, d, 0), (dres, d, 0)], [gain.reshape(1, d)], [(d, F32)], [d],
                            bm=512, name="norm_bwd")
    return dx, dg


def _rot_sign():
    lane = lax.broadcasted_iota(jnp.int32, (1, ATT_DH), 1)
    return jnp.where(lane < ATT_DH // 2, -1.0, 1.0).astype(F32)


def _rope(y, cos, sin):
    return y * cos + pltpu.roll(y, ATT_DH // 2, axis=1) * _rot_sign() * sin


def _rope_t(dy, cos, sin):
    return dy * cos - pltpu.roll(dy * sin, ATT_DH // 2, axis=1) * _rot_sign()


def _qk_prep(zq, zk, zv, qn, kn, cos, sin):
    w = zq.shape[1]

    def fn(ins, consts):
        cs, sn = ins[3], ins[4]
        outs = []
        for z, gain in ((ins[0], consts[0]), (ins[1], consts[1])):
            for i, (xh, _) in enumerate(_rms_parts(z, ATT_DH)):
                outs.append(_rope(xh * gain[:, i * ATT_DH:(i + 1) * ATT_DH], cs, sn))
        outs += [ins[2][:, i * ATT_DH:(i + 1) * ATT_DH] for i in range(w // ATT_DH)]
        groups = [jnp.concatenate(outs[i:i + ATT_HEADS], axis=1) for i in range(0, len(outs), ATT_HEADS)]
        return groups, []

    outs, _ = _rowwise(fn, [(zq, w, 0), (zk, w, 0), (zv, w, 0), (cos, ATT_DH, 0), (sin, ATT_DH, 0)], [qn, kn],
                       [(ATT_GW, BF16, ATT_PATTERNS[g][1]) for g in range(ATT_GROUPS)] * 3, [], bm=256, name="qk_prep")
    return outs[0:3], outs[3:6], outs[6:9]


def _qk_prep_bwd(zq, zk, dq_g, dk_g, dv_g, qn, kn, cos, sin):
    w = zq.shape[1]

    def fn(ins, consts):
        cs, sn = ins[2], ins[3]
        outs, sums = [], []
        for z, gain, dparts in ((ins[0], consts[0], ins[4:7]), (ins[1], consts[1], ins[7:10])):
            dout = jnp.concatenate(dparts, axis=1)
            dz, dgain = [], []
            for i, (xh, r) in enumerate(_rms_parts(z, ATT_DH)):
                hs = slice(i * ATT_DH, (i + 1) * ATT_DH)
                dy = _rope_t(dout[:, hs], cs, sn)
                dgain.append(_colsum8(dy * xh))
                dz.append(_rms_bwd_part(xh, r, dy * gain[:, hs]))
            outs.append(jnp.concatenate(dz, axis=1))
            sums.append(jnp.concatenate(dgain, axis=1))
        outs.append(jnp.concatenate(ins[10:13], axis=1))
        return outs, sums

    ins = [(zq, w, 0), (zk, w, 0), (cos, ATT_DH, 0), (sin, ATT_DH, 0)]
    for parts in (dq_g, dk_g, dv_g):
        ins += [(a, ATT_GW, 0, ATT_PATTERNS[g][1]) for g, a in enumerate(parts)]
    (dzq, dzk, dzv), (dqn, dkn) = _rowwise(fn, ins, [qn, kn], [(w, BF16)] * 3, [w, w], bm=256, name="qk_prep_bwd")
    return dzq, dzk, dzv, dqn, dkn


def _post_a(o_raw, zh, gout):
    w = o_raw.shape[1]

    def fn(ins, consts):
        oh = jnp.concatenate([xh for xh, _ in _rms_parts(ins[0], HG_DK)], axis=1)
        hg = ins[1]
        return [oh * consts[0] * (hg * _sigmoid(hg))], []

    (y,), _ = _rowwise(fn, [(o_raw, w, 0), (zh, w, 3)], [gout.reshape(1, w)], [(w, BF16)], [], bm=512, name="post_a")
    return y


def _post_a_bwd(o_raw, zh, gout, dy):
    w = o_raw.shape[1]

    def fn(ins, consts):
        parts = _rms_parts(ins[0], HG_DK)
        oh = jnp.concatenate([xh for xh, _ in parts], axis=1)
        hg, dyv, gain = ins[1], ins[2], consts[0]
        sg = _sigmoid(hg)
        s = hg * sg
        doh = dyv * gain * s
        do = jnp.concatenate([_rms_bwd_part(xh, r, doh[:, i * HG_DK:(i + 1) * HG_DK]) for i, (xh, r) in enumerate(parts)], axis=1)
        dhg = dyv * oh * gain * (sg * (1.0 + hg * (1.0 - sg)))
        return [do, dhg], [_colsum8(dyv * oh * s)]

    (do, dhg), (dgain,) = _rowwise(fn, [(o_raw, w, 0), (zh, w, 3), (dy, w, 0)], [gout.reshape(1, w)],
                                   [(w, F32), (w, BF16)], [w], bm=512, name="post_a_bwd")
    return do, dhg, dgain


def _merge_alpha(lses):
    m = jnp.maximum(jnp.maximum(lses[0], lses[1]), lses[2])
    e = [jnp.exp(l - m) for l in lses]
    inv = 1.0 / (e[0] + e[1] + e[2])
    return [x * inv for x in e]


def _group_ins(parts):
    return [(a, ATT_GW, 0, ATT_PATTERNS[g][1]) for g, a in enumerate(parts)]


def _merge_b(o_g, lse_g):
    def fn(ins, consts):
        al = _merge_alpha(ins[3:6])
        return [al[0] * ins[0] + al[1] * ins[1] + al[2] * ins[2]], []

    (y,), _ = _rowwise(fn, _group_ins(o_g) + _group_ins(lse_g), [], [(ATT_GW, BF16)], [], bm=512, name="merge_b")
    return y


def _merge_b_bwd(o_g, lse_g, dy):
    def fn(ins, consts):
        al = _merge_alpha(ins[3:6])
        dyv = ins[6]
        dal = [dyv * ins[i] for i in range(3)]
        tot = al[0] * dal[0] + al[1] * dal[1] + al[2] * dal[2]
        return [al[i] * dyv for i in range(3)] + [al[i] * (dal[i] - tot) for i in range(3)], []

    outs, _ = _rowwise(fn, _group_ins(o_g) + _group_ins(lse_g) + [(dy, ATT_GW, 0)], [],
                       [(ATT_GW, F32, ATT_PATTERNS[g][1]) for g in range(ATT_GROUPS)] * 2, [], bm=512, name="merge_b_bwd")
    return outs[:3], outs[3:]


def _loss_head(y, target):
    d = y.shape[1]

    def fn(ins, consts):
        e = ins[0] - ins[1]
        return [e * (1.0 / d)] * 2, [_colsum8(e * e)]

    (dy, dyb), (sq,) = _rowwise(fn, [(y, d, 0), (target, d, 0)], [], [(d, F32), (d, BF16)], [d], bm=512, name="loss_head")
    return 0.5 * jnp.sum(sq) / d, dy, dyb


def _silu_grad(a):
    s = _sigmoid(a)
    return s * (1.0 + a * (1.0 - s))


def _ffn_fwd(x, gain, wt, wo_fn, tag):
    t, d = x.shape
    f = wt.shape[0] // 2

    def act(accs, ex, consts):
        a, b = accs
        s = _sigmoid(a)
        sa = a * s
        return (sa * b, b, 0.5 * sa, 0.5 * (s + sa * (1.0 - s)))

    bn = FFN_BN if f % FFN_BN == 0 else 256
    u, b, sa, sp, h = _mm([x], [wt, wt], [(0, 0, 0), (0, 1, 1)], 2, act, [BF16] * 4, m=t, n=f, k=d, tb=True,
                          bm=512, bn=bn, bk=d, b_off=[(0, 0), (f // min(bn, f), 0)],
                          consts=[gain.reshape(1, d)], a_pro=_norm_pro, chunk=MXU_COLS, name=f"ffn_in_{tag}")
    wo = wo_fn(u)
    (y,) = _mm([u], [wo], [(0, 0, 0)], 1, lambda accs, ex: (ex[0] + 0.5 * accs[0],), [F32], m=t, n=d, k=f,
               bm=512, bn=d, bk=f, extras=[x], name=f"ffn_out_{tag}")
    return y, (x, h, u, b, sa, sp, wo)


def _ffn_bwd(dy, dyb, saved, gain, wt, tag, tok, emit):
    x, h, u, b, sa, sp, wo = saved
    t, d = x.shape
    f = wo.shape[0]

    def dact(accs, ex, consts):
        bv, sav, spv = (e.astype(F32) for e in ex)
        return (accs[0] * bv * spv, accs[0] * sav)

    bn = FFN_BN if f % FFN_BN == 0 else 256
    da, db = _mm([dyb], [wo], [(0, 0, 0)], 1, dact, [BF16, BF16], m=t, n=f, k=d, tb=True, bm=512, bn=bn, bk=d,
                 extras=[b, sa, sp], n_outer=True, chunk=MXU_COLS, consts=[tok], name=f"ffn_dact_{tag}")
    (dwo,) = _mm([u], [dyb], [(0, 0, 0)], 1, lambda accs, ex: (0.5 * accs[0],), [BF16], m=f, n=d, k=t, ta=True,
                 bm=1408, bn=d, bk=1024, name=f"ffn_dwo_{tag}")
    (dwt,) = _mm([da, db], [h], [(0, 0, 0)], 1, _first, [BF16], m=2 * f, n=d, k=t, ta=True, bm=min(1408, f), bn=d,
                 bk=1024, a_cat=True, name=f"ffn_dwt_{tag}")
    tok = emit(dwt, dwo)
    bk = min(FFN_BN, f)
    dx, dxb, dgain = _mm([da, db], [wt, wt], [(0, 0, 0), (1, 1, 0)], 1, _norm_bwd_fin, [F32, BF16], m=t, n=d, k=f,
                         bm=512, bn=d, bk=bk, b_off=[(0, 0), (0, f // bk)], extras=[x, dy],
                         consts=[gain.reshape(1, d), tok], n_sums=1, name=f"ffn_dh_{tag}")
    return dx, dxb, jnp.sum(dgain, axis=0), tok


FFN_BN = 2816
Z_SPLITS = (("h", 4096), ("q", 1536), ("k", 1536), ("v", 1536), ("g", 2048))


def _mix_fwd(x, p, cos, sin):
    t, d = x.shape
    z, off, hm = {}, 0, None
    for nm, width in Z_SPLITS:
        bn = 1024 if off % 1024 == 0 and width % 1024 == 0 else 512
        first = hm is None
        res = _mm([x if first else hm], [p["wint"]], [(0, 0, 0)], 1, (lambda accs, ex, consts: (accs[0],)) if first else _first,
                  [F32 if nm == "h" else BF16], m=t, n=width, k=d, tb=True, bm=1024, bn=bn, bk=d, b_off=[(off // bn, 0)],
                  consts=[p["gm"].reshape(1, d)] if first else (), a_pro=_norm_pro if first else None, name=f"mix_in_{nm}")
        z[nm] = res[0]
        hm = res[1] if first else hm
        off += width
    o_raw, states = _hgrn_fwd(z["h"], p["lb3"])
    qb, kb, vb = _qk_prep(z["q"], z["k"], z["v"], p["qn"], p["kn"], cos, sin)
    o_g, lse_g = zip(*[_attn_fwd(qb[g], kb[g], vb[g], g) for g in range(ATT_GROUPS)])
    oa = _post_a(o_raw, z["h"], p["gout"])
    ob = _merge_b(o_g, lse_g)
    late = p["late"](ob)
    p = dict(p, **late)
    (ya,) = _mm([oa], [p["wa"]], [(0, 0, 0)], 1, _first, [F32], m=t, n=d, k=oa.shape[1], bm=1024, bn=d, bk=oa.shape[1],
                name="branch_a")

    def gate(accs, ex):
        return (_sigmoid(ex[0].astype(F32)) * ex[2] + _sigmoid(ex[1].astype(F32)) * accs[0], accs[0])

    merged, yb = _mm([ob], [p["wbt"]], [(0, 0, 0)], 1, gate, [BF16, F32], m=t, n=d, k=ATT_GW, tb=True, bm=512, bn=d,
                     bk=ATT_GW, extras=[z["g"], z["g"], ya], e_off=[0, 1, 0], chunk=MXU_COLS, name="branch_b_gate")
    (y,) = _mm([merged], [p["wo"]], [(0, 0, 0)], 1, lambda accs, ex: (ex[0] + accs[0],), [F32], m=t, n=d, k=d,
               bm=1024, bn=d, bk=d, extras=[x], name="mix_out")
    return y, (x, hm, z, o_raw, states, qb, kb, vb, o_g, lse_g, oa, ob, ya, yb, merged, late)


def _mix_bwd(dy, dyb, saved, p, cos, sin, tok):
    x, hm, z, o_raw, states, qb, kb, vb, o_g, lse_g, oa, ob, ya, yb, merged, late = saved
    p = dict(p, **late)
    t, d = x.shape
    w = oa.shape[1]

    def dgate(accs, ex, consts):
        dm = accs[0]
        sa, sb = _sigmoid(ex[0].astype(F32)), _sigmoid(ex[1].astype(F32))
        return (sa * dm, sb * dm, dm * ex[2] * sa * (1.0 - sa), dm * ex[3] * sb * (1.0 - sb))

    dya, dyb_, dga, dgb = _mm([dyb], [p["wo"]], [(0, 0, 0)], 1, dgate, [BF16] * 4, m=t, n=d, k=d, tb=True, bm=512, bn=d,
                              bk=d, extras=[z["g"], z["g"], ya, yb], e_off=[0, 1, 0, 0], chunk=MXU_COLS, consts=[tok], name="mix_out_bwd")
    (dwo,) = _mm([merged], [dyb], [(0, 0, 0)], 1, _first, [BF16], m=d, n=d, k=t, ta=True, bm=d, bn=d, bk=1024, name="mix_dwo")
    (doa,) = _mm([dya], [p["wa"]], [(0, 0, 0)], 1, _first, [F32], m=t, n=w, k=d, tb=True, bm=1024, bn=w, bk=d, name="branch_a_bwd")
    (dwa,) = _mm([oa], [dya], [(0, 0, 0)], 1, _first, [BF16], m=w, n=d, k=t, ta=True, bm=w, bn=d, bk=1024, name="branch_a_dw")
    (dob,) = _mm([dyb_], [p["wbt"]], [(0, 0, 0)], 1, _first, [F32], m=t, n=ATT_GW, k=d, bm=1024, bn=ATT_GW, bk=d,
                 name="branch_b_bwd")
    (dwbt,) = _mm([dyb_], [ob], [(0, 0, 0)], 1, _first, [BF16], m=d, n=ATT_GW, k=t, ta=True, bm=d, bn=ATT_GW, bk=1024,
                  name="branch_b_dw")
    do_raw, dhg, dgout = _post_a_bwd(o_raw, z["h"], p["gout"], doa)
    do_g, dlse_g = _merge_b_bwd(o_g, lse_g, dob)
    dq_g, dk_g, dv_g = zip(*[_attn_bwd(qb[g], kb[g], vb[g], o_g[g], lse_g[g], do_g[g], dlse_g[g], g)
                             for g in range(ATT_GROUPS)])
    dzq, dzk, dzv, dqn, dkn = _qk_prep_bwd(z["q"], z["k"], dq_g, dk_g, dv_g, p["qn"], p["kn"], cos, sin)
    dhq, dhf, dhi, lbsum = _hgrn_bwd(z["h"], p["lb3"], states, do_raw)
    dz = jnp.concatenate([dhq, dhf, dhi, dhg, dzq, dzk, dzv, dga, dgb], axis=1)
    pw = dz.shape[1]
    (dwint,) = _mm([dz], [hm], [(0, 0, 0)], 1, _first, [BF16], m=pw, n=d, k=t, ta=True, bm=1536, bn=d, bk=1024, name="mix_in_dw")
    dx, dxb, dgm = _mm([dz], [p["wint"]], [(0, 0, 0)], 1, _norm_bwd_fin, [F32, BF16], m=t, n=d, k=pw, bm=512, bn=d, bk=1536,
                       extras=[x, dy], consts=[p["gm"].reshape(1, d)], n_sums=1, name="mix_in_bwd")
    return dx, dxb, dict(gm=jnp.sum(dgm, axis=0), wint=dwint, lbsum=lbsum, gout=dgout, qn=dqn, kn=dkn, wa=dwa, wbt=dwbt, wo=dwo)


def _rope_tables(t):
    pos = jnp.arange(t, dtype=F32)
    inv = ROPE_THETA ** (-jnp.arange(0, ATT_DH, 2, dtype=F32) / ATT_DH)
    ang = pos[:, None] * inv[None, :]
    ang = jnp.concatenate([ang, ang], axis=-1)
    return jnp.cos(ang), jnp.sin(ang)


def _lower_bounds(logits):
    lb = jnp.cumsum(jax.nn.softmax(logits, axis=0), axis=0)
    return lb - lb[0:1]


def _head_gain(g):
    return jnp.tile(g[:, None, :], (1, ATT_HEADS, 1)).reshape(1, ATT_GROUPS * ATT_GW)


SMALL_GRADS = ("ffn1_norm", "mix_norm", "lbsum", "hgrn_out_norm", "attn_q_norm", "attn_k_norm", "ffn2_norm")


def _local_step(x, target, small, fetch, emit):
    t = x.shape[0]
    depth = small["ffn1_norm"].shape[0]
    cos, sin = _rope_tables(t)
    lb_all = _lower_bounds(small["hgrn_lb_logits"])
    saved = []
    for l in range(depth):
        w1t = fetch("w1t", l, x)["w1t"]
        x, s1 = _ffn_fwd(x, small["ffn1_norm"][l], w1t, lambda after, l=l: fetch("w1o", l, after)["w1o"], "1")
        p = dict(gm=small["mix_norm"][l], wint=fetch("wint", l, x)["wint"], lb3=lb_all[l].reshape(-1, 1, HG_DK),
                 gout=small["hgrn_out_norm"][l], qn=_head_gain(small["attn_q_norm"][l]),
                 kn=_head_gain(small["attn_k_norm"][l]), late=functools.partial(fetch, "mout", l))
        x, sm = _mix_fwd(x, p, cos, sin)
        w2t = fetch("w2t", l, x)["w2t"]
        x, s2 = _ffn_fwd(x, small["ffn2_norm"][l], w2t, lambda after, l=l: fetch("w2o", l, after)["w2o"], "2")
        saved.append((p, w1t, w2t, s1, sm, s2))
    loss, dx, dxb = _loss_head(x, target)
    gsmall = {k: [None] * depth for k in SMALL_GRADS}
    tok = jnp.zeros((8, 128), F32)
    for l in reversed(range(depth)):
        p, w1t, w2t, s1, sm, s2 = saved[l]
        dx, dxb, gsmall["ffn2_norm"][l], tok = _ffn_bwd(
            dx, dxb, s2, small["ffn2_norm"][l], w2t, "2", tok, lambda dwt, dwo, l=l: emit("ffn2", l, dict(w2t=dwt, w2o=dwo), None))
        dx, dxb, gm = _mix_bwd(dx, dxb, sm, p, cos, sin, tok)
        tok = emit("mix", l, {k: gm[k] for k in ("wint", "wa", "wbt", "wo")}, None)
        gsmall["mix_norm"][l], gsmall["lbsum"][l], gsmall["hgrn_out_norm"][l] = gm["gm"], gm["lbsum"], gm["gout"]
        for k, src in (("attn_q_norm", "qn"), ("attn_k_norm", "kn")):
            gsmall[k][l] = jnp.sum(gm[src].reshape(ATT_GROUPS, ATT_HEADS, ATT_DH), axis=1)
        dx, dxb, gsmall["ffn1_norm"][l], tok = _ffn_bwd(
            dx, dxb, s1, small["ffn1_norm"][l], w1t, "1", tok, lambda dwt, dwo, l=l: emit("ffn1", l, dict(w1t=dwt, w1o=dwo), None))
    emit("small", 0, {}, ({k: jnp.stack(v) for k, v in gsmall.items()}, loss))
    return dx


_HBM = pl.BlockSpec(memory_space=pltpu.HBM)
_SEM = pl.BlockSpec(memory_space=pltpu.SEMAPHORE)
_EFFECT = pltpu.SideEffectType.DATAFLOW_SIDE_EFFECTING


def _peer(p):
    x, y, c = lax.axis_index("x"), lax.axis_index("y"), lax.axis_index("c")
    me = 4 * x + 2 * y + c
    return (1 - x if p & 4 else x, 1 - y if p & 2 else y, 1 - c if p & 1 else c), jnp.bitwise_xor(me, p), me


def _xchg_copy(src, land, mode, send_sems, recv_sems, k, p, arriving):
    peer, peer_id, me = _peer(p)
    block = src if mode == "gather" else src.at[peer_id]
    return pltpu.make_async_remote_copy(
        src_ref=block, dst_ref=land.at[peer_id if arriving else me], send_sem=send_sems.at[k * (N_DEV - 1) + p - 1],
        recv_sem=recv_sems.at[k * (N_DEV - 1) + p - 1], device_id=peer, device_id_type=MESH)


def _xchg_start(srcs, modes, groups, name):
    n, ng = len(srcs), len(groups)

    def body(*refs):
        src = refs[:n]
        sems = refs[n:n + 2 * ng]
        land = refs[n + 2 * ng + n:n + 2 * ng + 2 * n]
        token = refs[n + 2 * ng + 2 * n]
        for gi, idx in enumerate(groups):
            for ki, k in enumerate(idx):
                for p in range(1, N_DEV):
                    _xchg_copy(src[k], land[k], modes[k], sems[2 * gi], sems[2 * gi + 1], ki, p, False).start()
        token[...] = jnp.zeros_like(token)

    sem_shapes = []
    for idx in groups:
        sem_shapes += [pltpu.SemaphoreType.DMA((len(idx) * (N_DEV - 1),))] * 2
    outs = pl.pallas_call(
        body,
        out_shape=sem_shapes + [pltpu.HBM(a.shape, a.dtype) for a in srcs]
        + [pltpu.HBM((N_DEV,) + a.shape[-2:], a.dtype) for a in srcs] + [jax.ShapeDtypeStruct((8, 128), F32)],
        in_specs=[_HBM] * n,
        out_specs=[_SEM] * (2 * ng) + [_HBM] * (2 * n) + [pl.BlockSpec(memory_space=pltpu.VMEM)],
        input_output_aliases={i: 2 * ng + i for i in range(n)},
        compiler_params=pltpu.CompilerParams(has_side_effects=_EFFECT),
        name=name,
    )(*[pltpu.with_memory_space_constraint(a, pltpu.HBM) for a in srcs])
    sems = [(outs[2 * gi], outs[2 * gi + 1]) for gi in range(ng)]
    return sems, outs[2 * ng:2 * ng + n], outs[2 * ng + n:2 * ng + 2 * n], outs[-1]


def _xchg_wait_call(srcs, lands, modes, sems, after, name):
    n = len(srcs)

    def body(*refs):
        src, land = refs[:n], refs[n:2 * n]
        send_sems, recv_sems = refs[2 * n], refs[2 * n + 1]
        for p in range(1, N_DEV):
            for k in range(n):
                cp = _xchg_copy(src[k], land[k], modes[k], send_sems, recv_sems, k, p, True)
                cp.wait_send()
                cp.wait_recv()

    outs = pl.pallas_call(
        body,
        out_shape=[pltpu.HBM(a.shape, a.dtype) for a in list(srcs) + list(lands)],
        in_specs=[_HBM] * (2 * n) + [_SEM, _SEM, pl.BlockSpec(memory_space=pl.ANY)],
        out_specs=[_HBM] * (2 * n),
        input_output_aliases={i: i for i in range(2 * n)},
        compiler_params=pltpu.CompilerParams(has_side_effects=_EFFECT),
        name=name,
    )(*srcs, *lands, sems[0], sems[1], after)
    return outs[:n], outs[n:]


def _xchg_wait(srcs, lands, modes, sems, after, name):
    srcs, lands = _xchg_wait_call(srcs, lands, modes, sems, after, name)
    me = 4 * lax.axis_index("x") + 2 * lax.axis_index("y") + lax.axis_index("c")
    done = []
    for a, land, mode in zip(srcs, lands, modes):
        own = a[None] if mode == "gather" else lax.dynamic_slice_in_dim(a, me, 1, axis=0)
        done.append(lax.dynamic_update_slice(land, own, (me, 0, 0)))
    return done


def _sum_slots(land):
    g, _, r, c = land.shape
    br = r // 2 if (r % 32 == 0 and r >= 256) else r

    def body(l_ref, o_ref):
        acc = l_ref[0, 0].astype(F32)
        for j in range(1, N_DEV):
            acc = acc + l_ref[0, j].astype(F32)
        o_ref[0] = acc

    return pl.pallas_call(
        body,
        out_shape=jax.ShapeDtypeStruct((g, r, c), F32),
        grid=(g, r // br),
        in_specs=[pl.BlockSpec((1, N_DEV, br, c), lambda i, j: (i, 0, j, 0))],
        out_specs=pl.BlockSpec((1, br, c), lambda i, j: (i, j, 0)),
        compiler_params=_cparams(("parallel", "parallel")),
        name="sum_slots",
    )(land)


def _adamw(w, g, m, v):
    shape = w.shape
    cols = shape[-1]
    rows = int(np.prod(shape[:-1]))
    bm = max(b for b in range(8, 513, 8) if rows % b == 0) if rows % 8 == 0 else rows
    c1 = 1.0 - ADAM_B1 ** ADAM_STEP
    c2 = 1.0 - ADAM_B2 ** ADAM_STEP

    def fn(ins, consts):
        wv, gv, mv, vv = ins
        m2 = ADAM_B1 * mv + (1.0 - ADAM_B1) * gv
        v2 = ADAM_B2 * vv + (1.0 - ADAM_B2) * (gv * gv)
        delta = -ADAM_LR * ((m2 / c1) / (jnp.sqrt(v2 / c2) + ADAM_EPS) + ADAM_WD * wv)
        return [delta, m2, v2], []

    outs, _ = _rowwise(fn, [(a.reshape(rows, cols), cols, 0) for a in (w, g, m, v)], [], [(cols, F32)] * 3, [],
                       bm=bm, name="adamw")
    return [o.reshape(shape) for o in outs]


BIG = ("w1t", "w1o", "wint", "wa", "wbt", "wo", "w2t", "w2o")
FETCH_GROUPS = dict(w1t=("w1t",), w1o=("w1o",), wint=("wint",), mout=("wa", "wbt", "wo"), w2t=("w2t",), w2o=("w2o",))
SMALL_ROWS = (("ffn1_norm", 0), ("mix_norm", 2), ("lbsum", 4), ("hgrn_out_norm", 6), ("ffn2_norm", 8),
              ("attn_q_norm", 10), ("attn_k_norm", 12))
SMALL_PACK_ROWS = 16


def kernel(x, ffn1_norm, ffn1_w_in, ffn1_w_out, mix_norm, w_in, hgrn_lb_logits, hgrn_out_norm, attn_q_norm, attn_k_norm, w_branch_a, w_branch_b, w_out, ffn2_norm, ffn2_w_in, ffn2_w_out, loss_target, m_ffn1_norm, m_ffn1_w_in, m_ffn1_w_out, m_mix_norm, m_w_in, m_hgrn_lb_logits, m_hgrn_out_norm, m_attn_q_norm, m_attn_k_norm, m_w_branch_a, m_w_branch_b, m_w_out, m_ffn2_norm, m_ffn2_w_in, m_ffn2_w_out, v_ffn1_norm, v_ffn1_w_in, v_ffn1_w_out, v_mix_norm, v_w_in, v_hgrn_lb_logits, v_hgrn_out_norm, v_attn_q_norm, v_attn_k_norm, v_w_branch_a, v_w_branch_b, v_w_out, v_ffn2_norm, v_ffn2_w_in, v_ffn2_w_out):
    names = ("ffn1_norm", "ffn1_w_in", "ffn1_w_out", "mix_norm", "w_in", "hgrn_lb_logits", "hgrn_out_norm", "attn_q_norm",
             "attn_k_norm", "w_branch_a", "w_branch_b", "w_out", "ffn2_norm", "ffn2_w_in", "ffn2_w_out")
    w = dict(zip(names, (ffn1_norm, ffn1_w_in, ffn1_w_out, mix_norm, w_in, hgrn_lb_logits, hgrn_out_norm, attn_q_norm,
                         attn_k_norm, w_branch_a, w_branch_b, w_out, ffn2_norm, ffn2_w_in, ffn2_w_out)))
    m = dict(zip(names, (m_ffn1_norm, m_ffn1_w_in, m_ffn1_w_out, m_mix_norm, m_w_in, m_hgrn_lb_logits, m_hgrn_out_norm,
                         m_attn_q_norm, m_attn_k_norm, m_w_branch_a, m_w_branch_b, m_w_out, m_ffn2_norm, m_ffn2_w_in, m_ffn2_w_out)))
    v = dict(zip(names, (v_ffn1_norm, v_ffn1_w_in, v_ffn1_w_out, v_mix_norm, v_w_in, v_hgrn_lb_logits, v_hgrn_out_norm,
                         v_attn_q_norm, v_attn_k_norm, v_w_branch_a, v_w_branch_b, v_w_out, v_ffn2_norm, v_ffn2_w_in, v_ffn2_w_out)))
    depth, d = ffn1_norm.shape

    def tr(a):
        return jnp.swapaxes(a, 1, 2)

    shard = dict(w1t=tr(ffn1_w_in), w1o=ffn1_w_out, wint=tr(w_in), wa=w_branch_a,
                 wbt=tr(w_branch_b).reshape(depth, -1, d), wo=w_out, w2t=tr(ffn2_w_in), w2o=ffn2_w_out)
    order = [(g, l) for l in range(depth) for g in FETCH_GROUPS]
    started = {}
    for name, part in (("gather_start", order),):
        flat = [(l, k) for g, l in part for k in FETCH_GROUPS[g]]
        groups, pos = [], 0
        for g, l in part:
            groups.append(list(range(pos, pos + len(FETCH_GROUPS[g]))))
            pos += len(FETCH_GROUPS[g])
        sems, srcs, lands, _ = _xchg_start([shard[k][l].astype(BF16) for l, k in flat], ["gather"] * len(flat), groups, name)
        for gi, key in enumerate(part):
            started[key] = ([srcs[i] for i in groups[gi]], [lands[i] for i in groups[gi]], sems[gi])

    def fetch(group, l, after):
        srcs, lands, sems = started[group, l]
        lands = _xchg_wait(srcs, lands, ["gather"] * len(srcs), sems, after, f"gather_wait_{group}{l}")
        out = {}
        for k, land in zip(FETCH_GROUPS[group], lands):
            out[k] = land.reshape(d, -1) if k == "wbt" else land.reshape(-1, d)
        return out

    pending = []

    def emit(group, l, g, final):
        keys = list(g)
        srcs = [g[k].reshape(N_DEV, -1, d) for k in keys]
        modes = ["scatter"] * len(keys)
        if final is not None:
            gsmall, loss = final
            pack = jnp.zeros((SMALL_PACK_ROWS, d), F32)
            for k, r0 in SMALL_ROWS:
                rows = gsmall[k].reshape(depth, -1)
                pack = pack.at[r0:r0 + depth, :rows.shape[1]].set(rows)
            srcs.append(pack.at[14, :].set(loss))
            modes.append("gather")
            keys.append("small")
        sems, s_thru, l_thru, token = _xchg_start(srcs, modes, [list(range(len(srcs)))], f"grads_start_{group}{l}")
        pending.append((group, l, keys, modes, sems[0], s_thru, l_thru))
        return token

    small = {k: w[k] for k in ("ffn1_norm", "mix_norm", "hgrn_lb_logits", "hgrn_out_norm", "attn_q_norm", "attn_k_norm", "ffn2_norm")}
    dx = _local_step(x[0], loss_target[0], small, fetch, emit)

    summed = {}
    for group, l, keys, modes, sems, s_thru, l_thru in pending:
        lands = _xchg_wait(s_thru, l_thru, modes, sems, dx, f"grads_wait_{group}{l}")
        for k, land in zip(keys, lands):
            summed[k, l] = _sum_slots(land[None])[0]
    gsum = {k: jnp.stack([summed[k, l] for l in range(depth)]) for k in BIG}
    tot = summed["small", 0]

    grads = {}
    for k, r0 in SMALL_ROWS:
        shp = (depth,) + (w[k].shape[1:] if k != "lbsum" else (d,))
        grads[k] = tot[r0:r0 + depth, :int(np.prod(shp[1:]))].reshape(shp)
    _, lb_vjp = jax.vjp(_lower_bounds, hgrn_lb_logits)
    grads["hgrn_lb_logits"] = lb_vjp(grads.pop("lbsum"))[0]
    grads["ffn1_w_in"], grads["ffn1_w_out"] = tr(gsum["w1t"]), gsum["w1o"]
    grads["w_in"], grads["w_branch_a"] = tr(gsum["wint"]), gsum["wa"]
    grads["w_branch_b"] = tr(gsum["wbt"].reshape(depth, d // N_DEV, -1))
    grads["w_out"] = gsum["wo"]
    grads["ffn2_w_in"], grads["ffn2_w_out"] = tr(gsum["w2t"]), gsum["w2o"]

    upd = {k: _adamw(w[k], grads[k], m[k], v[k]) for k in names}
    return (tot[14, 0], dx[None], *[grads[k] for k in names], *[upd[k][0] for k in names],
            *[upd[k][1] for k in names], *[upd[k][2] for k in names])
```

```python
import functools
import math

import jax
import jax.numpy as jnp
import numpy as np
from jax import lax
from jax.experimental import pallas as pl
from jax.experimental.pallas import tpu as pltpu

F32 = jnp.float32
BF16 = jnp.bfloat16

N_DEV = 8
EPS = 1e-6
HG_DK = 128
HG_CHUNK = 64
HG_SUB = 16
HG_HP = 8
ATT_PATTERNS = ((128, 1), (512, 4), (2048, 16))
ATT_GROUPS = 3
ATT_HEADS = 4
ATT_DH = 128
ATT_BLK = 128
ROPE_THETA = 10000.0
ADAM_LR, ADAM_B1, ADAM_B2, ADAM_EPS, ADAM_WD, ADAM_STEP = 0.001, 0.9, 0.999, 1e-08, 0.01, 10
VMEM_LIMIT_BYTES = 56 * 1024 * 1024
MXU_COLS = 256
MESH = pl.DeviceIdType.MESH


def _cparams(sem, **kw):
    return pltpu.CompilerParams(dimension_semantics=sem, vmem_limit_bytes=VMEM_LIMIT_BYTES, **kw)


def _sigmoid(x):
    return 1.0 / (1.0 + jnp.exp(-x))


def _mm(a_list, b_list, pairs, n_acc, fin, out_dtypes, *, m, n, k, ta=False, tb=False, bm, bn, bk,
        b_off=None, extras=(), e_off=None, n_outer=False, consts=(), a_pro=None, n_sums=0, chunk=0, a_cat=False, name):
    bm, bn, bk = min(bm, m), min(bn, n), min(bk, k)
    assert m % bm == 0 and n % bn == 0 and k % bk == 0, (name, m, n, k, bm, bn, bk)
    nk = k // bk
    assert not (a_pro and (nk > 1 or ta or n_outer)) and not (n_sums and (bn != n or n_outer)), name
    assert not (chunk and (nk > 1 or n_sums or chunk % 128)), name
    if a_cat:
        unit = bm if ta else bk
        widths = [a.shape[1] for a in a_list]
        assert all(w % unit == 0 for w in widths) and sum(widths) == (m if ta else k) and not a_pro, name
        cat_counts = [w // unit for w in widths]
        cat_starts = [sum(cat_counts[:i]) for i in range(len(widths))]
    b_off = b_off or [(0, 0)] * len(b_list)
    e_off = e_off or [0] * len(extras)
    na, nb, ne, nc, no = len(a_list), len(b_list), len(extras), len(consts), len(out_dtypes)
    nao = na if a_pro else 0
    dn = (((0,) if ta else (1,), (1,) if tb else (0,)), ((), ()))

    def body(*refs):
        refs = list(refs)
        a_refs, b_refs, e_refs, c_refs, o_refs, ao_refs, s_refs = (
            [refs.pop(0) for _ in range(cnt)] for cnt in (na, nb, ne, nc, no, nao, n_sums))
        acc_refs = refs
        kk = pl.program_id(2)
        first = pl.program_id(0) == 0
        cvals = [c[...] for c in c_refs]
        a_vals = [r[...] for r in a_refs]
        if a_cat:
            col = pl.program_id(1 if n_outer else 0) if ta else kk
            sel = a_vals[0]
            for start, v in zip(cat_starts[1:], a_vals[1:]):
                sel = jnp.where(col >= start, v, sel)
            a_vals = [sel]
        if a_pro:
            a_vals = a_pro(a_vals, cvals)
            for r, v in zip(ao_refs, a_vals):
                r[...] = v
        if chunk:
            spans = [slice(lo, min(lo + chunk, bn)) for lo in range(0, bn, chunk)]
            chunks = []
            for cs in spans:
                parts = [None] * n_acc
                for ai, bi, ci in pairs:
                    p = lax.dot_general(a_vals[ai], b_refs[bi][cs, :] if tb else b_refs[bi][:, cs], dn,
                                        preferred_element_type=F32)
                    parts[ci] = p if parts[ci] is None else parts[ci] + p
                chunks.append(parts)
            for cs, parts in zip(spans, chunks):
                ex = [e[:, cs] for e in e_refs]
                outs = fin(parts, ex, cvals) if nc else fin(parts, ex)
                for o_ref, o in zip(o_refs, outs):
                    o_ref[:, cs] = o.astype(o_ref.dtype)
            return

        parts = [None] * n_acc
        for ai, bi, ci in pairs:
            p = lax.dot_general(a_vals[ai], b_refs[bi][...], dn, preferred_element_type=F32)
            parts[ci] = p if parts[ci] is None else parts[ci] + p

        def finish(accs):
            ex = [e[...] for e in e_refs]
            res = fin(accs, ex, cvals) if nc else fin(accs, ex)
            outs, sums = res if n_sums else (res, ())
            for o_ref, o in zip(o_refs, outs):
                o_ref[...] = o.astype(o_ref.dtype)
            if n_sums:
                @pl.when(first)
                def _():
                    for s_ref, s in zip(s_refs, sums):
                        s_ref[...] = s

                @pl.when(jnp.logical_not(first))
                def _():
                    for s_ref, s in zip(s_refs, sums):
                        s_ref[...] += s

        if nk == 1:
            finish(parts)
        else:
            @pl.when(kk == 0)
            def _():
                for c in range(n_acc):
                    acc_refs[c][...] = parts[c]

            @pl.when(kk > 0)
            def _():
                for c in range(n_acc):
                    acc_refs[c][...] += parts[c]

            @pl.when(kk == nk - 1)
            def _():
                finish([acc_refs[c][...] for c in range(n_acc)])

    def ij(f):
        return (lambda j, i, q: f(i, j, q)) if n_outer else f

    a_spec = pl.BlockSpec((bk, bm), ij(lambda i, j, q: (q, i))) if ta else pl.BlockSpec((bm, bk), ij(lambda i, j, q: (i, q)))
    a_specs = [a_spec] * na
    if a_cat:
        def part_spec(start, count):
            def col(c):
                return jnp.clip(c - start, 0, count - 1)
            if ta:
                return pl.BlockSpec((bk, bm), ij(lambda i, j, q: (q, col(i))))
            return pl.BlockSpec((bm, bk), ij(lambda i, j, q: (i, col(q))))
        a_specs = [part_spec(s, c) for s, c in zip(cat_starts, cat_counts)]

    b_mode = dict(pipeline_mode=pl.Buffered(1)) if (bn == n and nk == 1) else {}

    def b_spec(off):
        on, ok = off
        if tb:
            return pl.BlockSpec((bn, bk), ij(lambda i, j, q: (j + on, q + ok)), **b_mode)
        return pl.BlockSpec((bk, bn), ij(lambda i, j, q: (q + ok, j + on)), **b_mode)

    mn_spec = pl.BlockSpec((bm, bn), ij(lambda i, j, q: (i, j)))
    outs = pl.pallas_call(
        body,
        out_shape=[jax.ShapeDtypeStruct((m, n), d) for d in out_dtypes] + [jax.ShapeDtypeStruct((m, k), BF16)] * nao
        + [jax.ShapeDtypeStruct((8, n), F32)] * n_sums,
        grid=(n // bn, m // bm, nk) if n_outer else (m // bm, n // bn, nk),
        in_specs=a_specs + [b_spec(o) for o in b_off]
        + [pl.BlockSpec((bm, bn), ij(lambda i, j, q, o=o: (i, j + o))) for o in e_off]
        + [pl.BlockSpec(c.shape, lambda *_, nd=c.ndim: (0,) * nd) for c in consts],
        out_specs=[mn_spec] * no + [a_spec] * nao + [pl.BlockSpec((8, n), lambda *_: (0, 0))] * n_sums,
        scratch_shapes=[pltpu.VMEM((bm, bn), F32) for _ in range(n_acc if nk > 1 else 0)],
        compiler_params=_cparams(("arbitrary" if n_sums else "parallel", "parallel", "arbitrary")),
        name=name,
    )(*a_list, *b_list, *extras, *consts)
    return outs


def _first(accs, ex):
    return (accs[0],)


def _rowwise(fn, ins, consts, out_defs, sum_widths, *, bm, name):
    ins = [tuple(e) + (1,) * (4 - len(e)) for e in ins]
    out_defs = [tuple(e) + (1,) * (3 - len(e)) for e in out_defs]
    t = ins[0][0].shape[-2] * ins[0][3]
    bm = min(bm, t)
    assert t % bm == 0, (name, t, bm)
    ni, nc, no, ns = len(ins), len(consts), len(out_defs), len(sum_widths)
    strided = [w for _, w, _, d in ins if d > 1] + [w for w, _, d in out_defs if d > 1]

    def body(*refs):
        i_refs, c_refs = refs[:ni], refs[ni:ni + nc]
        o_refs, s_refs = refs[ni + nc:ni + nc + no], refs[ni + nc + no:ni + nc + no + ns]
        scratch = list(refs[ni + nc + no + ns:])
        vals = []
        for ref, (_, w, _, d) in zip(i_refs, ins):
            if d == 1:
                vals.append(ref[...])
                continue
            s = scratch.pop(0)
            for r in range(d):
                for c in range(w // 128):
                    s.at[c][pl.ds(r, bm // d, stride=d), :] = ref[r, :, c * 128:(c + 1) * 128].astype(F32)
            vals.append(jnp.concatenate([s[c] for c in range(w // 128)], axis=1))
        outs, sums = fn(vals, [r[...] for r in c_refs])
        for o_ref, o, (w, _, d) in zip(o_refs, outs, out_defs):
            if d == 1:
                o_ref[...] = o.astype(o_ref.dtype)
                continue
            s = scratch.pop(0)
            for c in range(w // 128):
                s[c] = o[:, c * 128:(c + 1) * 128].astype(F32)
            for r in range(d):
                for c in range(w // 128):
                    o_ref[r, :, c * 128:(c + 1) * 128] = s.at[c][pl.ds(r, bm // d, stride=d), :].astype(o_ref.dtype)
        if ns:
            first = pl.program_id(0) == 0

            @pl.when(first)
            def _():
                for s_ref, s in zip(s_refs, sums):
                    s_ref[...] = s

            @pl.when(jnp.logical_not(first))
            def _():
                for s_ref, s in zip(s_refs, sums):
                    s_ref[...] += s

    def win(width, cb, d):
        if d > 1:
            return pl.BlockSpec((d, bm // d, width), lambda i: (0, i, 0))
        return pl.BlockSpec((bm, width), lambda i: (i, cb))

    res = pl.pallas_call(
        body,
        out_shape=[jax.ShapeDtypeStruct((t, w) if d == 1 else (d, t // d, w), dt) for w, dt, d in out_defs]
        + [jax.ShapeDtypeStruct((8, w), F32) for w in sum_widths],
        grid=(t // bm,),
        in_specs=[win(w, cb, d) for _, w, cb, d in ins] + [pl.BlockSpec(c.shape, lambda i, nd=c.ndim: (0,) * nd) for c in consts],
        out_specs=[win(w, 0, d) for w, _, d in out_defs] + [pl.BlockSpec((8, w), lambda i: (0, 0)) for w in sum_widths],
        scratch_shapes=[pltpu.VMEM((w // 128, bm, 128), F32) for w in strided],
        compiler_params=_cparams(("arbitrary",) if ns else ("parallel",)),
        name=name,
    )(*[e[0] for e in ins], *consts)
    return res[:no], [jnp.sum(s, axis=0) for s in res[no:]]


def _colsum8(x):
    bm, w = x.shape
    return jnp.sum(x.reshape(bm // 8, 8, w), axis=0)


def _tri(n, upper=False):
    r = lax.broadcasted_iota(jnp.int32, (n, n), 0)
    c = lax.broadcasted_iota(jnp.int32, (n, n), 1)
    return (c >= r) if upper else (c <= r)


def _exact_tri_matmul(tri_bf16, x):
    x0 = x.astype(BF16)
    r1 = x - x0.astype(F32)
    x1 = r1.astype(BF16)
    x2 = (r1 - x1.astype(F32)).astype(BF16)
    w = x.shape[1]
    y = jnp.dot(tri_bf16, jnp.concatenate([x0, x1, x2], axis=1), preferred_element_type=F32)
    return y[:, :w] + y[:, w:2 * w] + y[:, 2 * w:]


def _dot_nt(a, b):
    return lax.dot_general(a, b, (((1,), (1,)), ((), ())), preferred_element_type=F32)


def _dot_tn(a, b):
    return lax.dot_general(a, b, (((0,), (0,)), ((), ())), preferred_element_type=F32)


def _dot(a, b):
    return jnp.dot(a, b, preferred_element_type=F32)


def _hg_gates(hq, hf, lb):
    sq = _sigmoid(hq)
    q = hq * sq
    sg = _sigmoid(hf)
    f = lb + (1.0 - lb) * sg
    return q, sq, sg, f


def _hg_intra(q, kk, g):
    c = q.shape[0]
    rows = lax.broadcasted_iota(jnp.int32, (c, 1), 0)
    a_rows, qts, kts, eqs, eks = [], [], [], [], []
    for i in range(c // HG_SUB):
        lo = i * HG_SUB
        ref = g[lo - 1:lo, :] if i else jnp.zeros_like(g[0:1, :])
        eq = jnp.exp(g[lo:lo + HG_SUB, :] - ref)
        ek = jnp.exp(jnp.where(rows < lo + HG_SUB, ref - g, 0.0))
        qt = q[lo:lo + HG_SUB, :] * eq
        kt = kk * ek
        a = _dot_nt(qt.astype(BF16), kt.astype(BF16))
        tpos = lo + lax.broadcasted_iota(jnp.int32, (HG_SUB, c), 0)
        spos = lax.broadcasted_iota(jnp.int32, (HG_SUB, c), 1)
        a_rows.append(jnp.where(spos <= tpos, a, 0.0))
        qts.append(qt), kts.append(kt), eqs.append(eq), eks.append(ek)
    return jnp.concatenate(a_rows, axis=0), qts, kts, eqs, eks


def _hgrn_fwd_serial(zh, lb3, *, tb=512):
    t = zh.shape[0]
    nh = lb3.shape[0]
    c = HG_CHUNK
    tb = min(tb, t)
    nchunk = tb // c
    hp = HG_HP if nh % HG_HP == 0 else 1

    def body(hq_ref, hf_ref, hi_ref, lb_ref, o_ref, st_ref, state):
        @pl.when(pl.program_id(1) == 0)
        def _():
            state[...] = jnp.zeros_like(state)

        tril = _tri(c).astype(BF16)

        def one_head(hh, ci, sl):
            ls = slice(hh * HG_DK, (hh + 1) * HG_DK)
            q, _, _, f = _hg_gates(hq_ref[sl, ls], hf_ref[sl, ls], lb_ref[hh])
            v = hi_ref[sl, ls]
            kk = 1.0 - f
            g = _exact_tri_matmul(tril, jnp.log(f))
            a, _, _, _, _ = _hg_intra(q, kk, g)
            st = state[hh]
            st_ref[hh, ci] = st
            vb = v.astype(BF16)
            o = _dot(a.astype(BF16), vb) + _dot_nt((q * jnp.exp(g)).astype(BF16), st.astype(BF16))
            o_ref[sl, ls] = o
            glast = g[c - 1:c, :]
            kg = kk * jnp.exp(glast - g)
            state[hh] = st * jnp.exp(glast) + _dot_tn(vb, kg.astype(BF16))

        def chunk(ci, carry):
            sl = pl.ds(pl.multiple_of(ci * c, c), c)
            for hh in range(hp):
                one_head(hh, ci, sl)
            return carry

        lax.fori_loop(0, nchunk, chunk, 0)

    def col(cb):
        return pl.BlockSpec((tb, hp * HG_DK), lambda h, i: (i, cb * (nh // hp) + h))

    return pl.pallas_call(
        body,
        out_shape=[jax.ShapeDtypeStruct((t, nh * HG_DK), F32), jax.ShapeDtypeStruct((nh, t // c, HG_DK, HG_DK), F32)],
        grid=(nh // hp, t // tb),
        in_specs=[col(0), col(1), col(2), pl.BlockSpec((hp, 1, HG_DK), lambda h, i: (h, 0, 0))],
        out_specs=[pl.BlockSpec((tb, hp * HG_DK), lambda h, i: (i, h)),
                   pl.BlockSpec((hp, nchunk, HG_DK, HG_DK), lambda h, i: (h, i, 0, 0))],
        scratch_shapes=[pltpu.VMEM((hp, HG_DK, HG_DK), F32)],
        compiler_params=_cparams(("parallel", "arbitrary")),
        name="hgrn_fwd",
    )(zh, zh, zh, lb3)


def _hgrn_bwd_serial(zh, lb3, states, d_o, *, tb=512):
    t = zh.shape[0]
    nh = lb3.shape[0]
    c = HG_CHUNK
    tb = min(tb, t)
    nchunk = tb // c
    nblk = t // tb
    hp = HG_HP if nh % HG_HP == 0 else 1

    def body(hq_ref, hf_ref, hi_ref, lb_ref, st_ref, do_ref, dq_ref, df_ref, dv_ref, dlb_ref, dstate):
        @pl.when(pl.program_id(1) == 0)
        def _():
            dstate[...] = jnp.zeros_like(dstate)
            dlb_ref[...] = jnp.zeros_like(dlb_ref)

        tril = _tri(c).astype(BF16)
        triu = _tri(c, upper=True).astype(BF16)
        last_row = lax.broadcasted_iota(jnp.int32, (c, 1), 0) == c - 1

        def one_head(hh, ci, sl):
            ls = slice(hh * HG_DK, (hh + 1) * HG_DK)
            lb = lb_ref[hh]
            hq, hf = hq_ref[sl, ls], hf_ref[sl, ls]
            q, sq, sg, f = _hg_gates(hq, hf, lb)
            v = hi_ref[sl, ls]
            kk = 1.0 - f
            g = _exact_tri_matmul(tril, jnp.log(f))
            a, qts, kts, eqs, eks = _hg_intra(q, kk, g)
            st = st_ref[hh, ci]
            dst = dstate[hh]
            do = do_ref[sl, ls]
            dob, vb = do.astype(BF16), v.astype(BF16)
            glast = g[c - 1:c, :]
            eg = jnp.exp(g)
            egl = jnp.exp(glast - g)
            qg = q * eg
            kg = kk * egl
            dv = _dot_tn(a.astype(BF16), dob) + _dot_nt(kg.astype(BF16), dst.astype(BF16))
            da = jnp.where(_tri(c), _dot_nt(dob, vb), 0.0).astype(BF16)
            dq_parts, dgq_parts = [], []
            dk = jnp.zeros_like(kk)
            dgk = jnp.zeros_like(kk)
            for i in range(c // HG_SUB):
                da_i = da[i * HG_SUB:(i + 1) * HG_SUB, :]
                ktb, qtb = kts[i].astype(BF16), qts[i].astype(BF16)
                xi = _dot(da_i, ktb)
                yi = _dot_tn(da_i, qtb)
                dq_parts.append(xi * eqs[i])
                dk = dk + yi * eks[i]
                dgq_parts.append(xi * qtb.astype(F32))
                dgk = dgk + yi * ktb.astype(F32)
            dq_inter = _dot(dob, st.astype(BF16)) * eg
            dq = jnp.concatenate(dq_parts, axis=0) + dq_inter
            dk_state = _dot(vb, dst.astype(BF16)) * egl
            dk = dk + dk_state
            dg = jnp.concatenate(dgq_parts, axis=0) - dgk + q * dq_inter - kk * dk_state
            dgl = jnp.sum(kk * dk_state, axis=0, keepdims=True) + jnp.exp(glast) * jnp.sum(st * dst, axis=0, keepdims=True)
            dg = dg + jnp.where(last_row, dgl, 0.0)
            dlogf = _exact_tri_matmul(triu, dg)
            dfv = dlogf / f - dk
            dq_ref[sl, ls] = (dq * (sq * (1.0 + hq * (1.0 - sq)))).astype(dq_ref.dtype)
            df_ref[sl, ls] = (dfv * (1.0 - lb) * sg * (1.0 - sg)).astype(df_ref.dtype)
            dv_ref[sl, ls] = dv.astype(dv_ref.dtype)
            dlb_ref[hh] += jnp.sum(dfv * (1.0 - sg), axis=0, keepdims=True)
            dstate[hh] = dst * jnp.exp(glast) + _dot_tn(dob, qg.astype(BF16))

        def chunk(j, carry):
            ci = nchunk - 1 - j
            sl = pl.ds(pl.multiple_of(ci * c, c), c)
            for hh in range(hp):
                one_head(hh, ci, sl)
            return carry

        lax.fori_loop(0, nchunk, chunk, 0)

    def col(cb):
        return pl.BlockSpec((tb, hp * HG_DK), lambda h, i: (nblk - 1 - i, cb * (nh // hp) + h))

    ocol = pl.BlockSpec((tb, hp * HG_DK), lambda h, i: (nblk - 1 - i, h))
    w = nh * HG_DK
    dq, df, dv, dlb = pl.pallas_call(
        body,
        out_shape=[jax.ShapeDtypeStruct((t, w), BF16)] * 3 + [jax.ShapeDtypeStruct((nh, 1, HG_DK), F32)],
        grid=(nh // hp, nblk),
        in_specs=[col(0), col(1), col(2), pl.BlockSpec((hp, 1, HG_DK), lambda h, i: (h, 0, 0)),
                  pl.BlockSpec((hp, nchunk, HG_DK, HG_DK), lambda h, i: (h, nblk - 1 - i, 0, 0)), ocol],
        out_specs=[ocol, ocol, ocol, pl.BlockSpec((hp, 1, HG_DK), lambda h, i: (h, 0, 0))],
        scratch_shapes=[pltpu.VMEM((hp, HG_DK, HG_DK), F32)],
        compiler_params=_cparams(("parallel", "arbitrary")),
        name="hgrn_bwd",
    )(zh, zh, zh, lb3, states, d_o)
    return dq, df, dv, dlb.reshape(w)


def _hg_heads(x, hp):
    return [x[:, h * HG_DK:(h + 1) * HG_DK] for h in range(hp)]


def _hg_intra_wide(q, kk, g, hp):
    c = q.shape[0]
    rows = lax.broadcasted_iota(jnp.int32, (c, 1), 0)
    a_rows = [[] for _ in range(hp)]
    qts, kts, eqs, eks = [], [], [], []
    for i in range(c // HG_SUB):
        lo = i * HG_SUB
        ref = g[lo - 1:lo, :] if i else jnp.zeros_like(g[0:1, :])
        eq = jnp.exp(g[lo:lo + HG_SUB, :] - ref)
        ek = jnp.exp(jnp.where(rows < lo + HG_SUB, ref - g, 0.0))
        qtb = (q[lo:lo + HG_SUB, :] * eq).astype(BF16)
        ktb = (kk * ek).astype(BF16)
        tpos = lo + lax.broadcasted_iota(jnp.int32, (HG_SUB, c), 0)
        spos = lax.broadcasted_iota(jnp.int32, (HG_SUB, c), 1)
        for h, (qh, kh) in enumerate(zip(_hg_heads(qtb, hp), _hg_heads(ktb, hp))):
            a_rows[h].append(jnp.where(spos <= tpos, _dot_nt(qh, kh), 0.0))
        qts.append(qtb), kts.append(ktb), eqs.append(eq), eks.append(ek)
    return [jnp.concatenate(r, axis=0) for r in a_rows], qts, kts, eqs, eks


def _hgrn_fwd(zh, lb3, *, tb=512):
    t = zh.shape[0]
    nh = lb3.shape[0]
    c = HG_CHUNK
    tb = min(tb, t)
    nchunk = tb // c
    hp = HG_HP if nh % HG_HP == 0 else 1
    wp = hp * HG_DK

    def body(hq_ref, hf_ref, hi_ref, lb_ref, o_ref, st_ref, state):
        @pl.when(pl.program_id(1) == 0)
        def _():
            state[...] = jnp.zeros_like(state)

        tril = _tri(c).astype(BF16)

        def chunk(ci, carry):
            sl = pl.ds(pl.multiple_of(ci * c, c), c)
            q, _, _, f = _hg_gates(hq_ref[sl, :], hf_ref[sl, :], lb_ref[...])
            kk = 1.0 - f
            g = _exact_tri_matmul(tril, jnp.log(f))
            a, _, _, _, _ = _hg_intra_wide(q, kk, g, hp)
            vb = hi_ref[sl, :].astype(BF16)
            glast = g[c - 1:c, :]
            qgb = (q * jnp.exp(g)).astype(BF16)
            kgb = (kk * jnp.exp(glast - g)).astype(BF16)
            dec = jnp.exp(glast)
            sts = [state[h] for h in range(hp)]
            for h in range(hp):
                st_ref[h, ci] = sts[h]
            vh, qgh, kgh, dech = _hg_heads(vb, hp), _hg_heads(qgb, hp), _hg_heads(kgb, hp), _hg_heads(dec, hp)
            o = [_dot(a[h].astype(BF16), vh[h]) + _dot_nt(qgh[h], sts[h].astype(BF16)) for h in range(hp)]
            new = [_dot_tn(vh[h], kgh[h]) for h in range(hp)]
            o_ref[sl, :] = jnp.concatenate(o, axis=1)
            for h in range(hp):
                state[h] = sts[h] * dech[h] + new[h]
            return carry

        lax.fori_loop(0, nchunk, chunk, 0)

    def col(cb):
        return pl.BlockSpec((tb, wp), lambda h, i: (i, cb * (nh // hp) + h))

    return pl.pallas_call(
        body,
        out_shape=[jax.ShapeDtypeStruct((t, nh * HG_DK), F32), jax.ShapeDtypeStruct((nh, t // c, HG_DK, HG_DK), F32)],
        grid=(nh // hp, t // tb),
        in_specs=[col(0), col(1), col(2), pl.BlockSpec((1, wp), lambda h, i: (0, h))],
        out_specs=[pl.BlockSpec((tb, wp), lambda h, i: (i, h)),
                   pl.BlockSpec((hp, nchunk, HG_DK, HG_DK), lambda h, i: (h, i, 0, 0))],
        scratch_shapes=[pltpu.VMEM((hp, HG_DK, HG_DK), F32)],
        compiler_params=_cparams(("parallel", "arbitrary")),
        name="hgrn_fwd",
    )(zh, zh, zh, lb3.reshape(1, -1))


def _hgrn_bwd(zh, lb3, states, d_o, *, tb=512):
    t = zh.shape[0]
    nh = lb3.shape[0]
    c = HG_CHUNK
    tb = min(tb, t)
    nchunk = tb // c
    nblk = t // tb
    hp = HG_HP if nh % HG_HP == 0 else 1
    wp = hp * HG_DK

    def body(hq_ref, hf_ref, hi_ref, lb_ref, st_ref, do_ref, dq_ref, df_ref, dv_ref, dlb_ref, dstate):
        @pl.when(pl.program_id(1) == 0)
        def _():
            dstate[...] = jnp.zeros_like(dstate)
            dlb_ref[...] = jnp.zeros_like(dlb_ref)

        tril = _tri(c).astype(BF16)
        triu = _tri(c, upper=True).astype(BF16)
        last_row = lax.broadcasted_iota(jnp.int32, (c, 1), 0) == c - 1
        heads = range(hp)

        def chunk(j, carry):
            ci = nchunk - 1 - j
            sl = pl.ds(pl.multiple_of(ci * c, c), c)
            lb = lb_ref[...]
            hq, hf = hq_ref[sl, :], hf_ref[sl, :]
            q, sq, sg, f = _hg_gates(hq, hf, lb)
            kk = 1.0 - f
            g = _exact_tri_matmul(tril, jnp.log(f))
            a, qts, kts, eqs, eks = _hg_intra_wide(q, kk, g, hp)
            glast = g[c - 1:c, :]
            eg, egl, dec = jnp.exp(g), jnp.exp(glast - g), jnp.exp(glast)
            vb, dob = hi_ref[sl, :].astype(BF16), do_ref[sl, :].astype(BF16)
            qgb, kgb = (q * eg).astype(BF16), (kk * egl).astype(BF16)
            sts = [st_ref[h, ci] for h in heads]
            dsts = [dstate[h] for h in heads]
            stb, dstb = [s.astype(BF16) for s in sts], [s.astype(BF16) for s in dsts]
            vh, doh, qgh, kgh = _hg_heads(vb, hp), _hg_heads(dob, hp), _hg_heads(qgb, hp), _hg_heads(kgb, hp)
            dv = [_dot_tn(a[h].astype(BF16), doh[h]) + _dot_nt(kgh[h], dstb[h]) for h in heads]
            da = [jnp.where(_tri(c), _dot_nt(doh[h], vh[h]), 0.0).astype(BF16) for h in heads]
            dq_inter = jnp.concatenate([_dot(doh[h], stb[h]) for h in heads], axis=1) * eg
            dk_state = jnp.concatenate([_dot(vh[h], dstb[h]) for h in heads], axis=1) * egl
            new_dst = [_dot_tn(doh[h], qgh[h]) for h in heads]
            xs, dk, dgk = [], dk_state, 0.0
            for i in range(c // HG_SUB):
                rs = slice(i * HG_SUB, (i + 1) * HG_SUB)
                kth, qth = _hg_heads(kts[i], hp), _hg_heads(qts[i], hp)
                xi = jnp.concatenate([_dot(da[h][rs, :], kth[h]) for h in heads], axis=1)
                yi = jnp.concatenate([_dot_tn(da[h][rs, :], qth[h]) for h in heads], axis=1)
                xs.append(xi)
                dk = dk + yi * eks[i]
                dgk = dgk + yi * kts[i].astype(F32)
            dq = jnp.concatenate([x * e for x, e in zip(xs, eqs)], axis=0) + dq_inter
            dgq = jnp.concatenate([x * qt.astype(F32) for x, qt in zip(xs, qts)], axis=0)
            dg = dgq - dgk + q * dq_inter - kk * dk_state
            sdot = jnp.concatenate([jnp.sum(sts[h] * dsts[h], axis=0, keepdims=True) for h in heads], axis=1)
            dgl = jnp.sum(kk * dk_state, axis=0, keepdims=True) + dec * sdot
            dg = dg + jnp.where(last_row, dgl, 0.0)
            dlogf = _exact_tri_matmul(triu, dg)
            dfv = dlogf / f - dk
            dq_ref[sl, :] = (dq * (sq * (1.0 + hq * (1.0 - sq)))).astype(dq_ref.dtype)
            df_ref[sl, :] = (dfv * (1.0 - lb) * sg * (1.0 - sg)).astype(df_ref.dtype)
            dv_ref[sl, :] = jnp.concatenate(dv, axis=1).astype(dv_ref.dtype)
            dlb_ref[...] += jnp.sum(dfv * (1.0 - sg), axis=0, keepdims=True)
            dech = _hg_heads(dec, hp)
            for h in heads:
                dstate[h] = dsts[h] * dech[h] + new_dst[h]
            return carry

        lax.fori_loop(0, nchunk, chunk, 0)

    def col(cb):
        return pl.BlockSpec((tb, wp), lambda h, i: (nblk - 1 - i, cb * (nh // hp) + h))

    ocol = pl.BlockSpec((tb, wp), lambda h, i: (nblk - 1 - i, h))
    lbspec = pl.BlockSpec((1, wp), lambda h, i: (0, h))
    w = nh * HG_DK
    dq, df, dv, dlb = pl.pallas_call(
        body,
        out_shape=[jax.ShapeDtypeStruct((t, w), BF16)] * 3 + [jax.ShapeDtypeStruct((1, w), F32)],
        grid=(nh // hp, nblk),
        in_specs=[col(0), col(1), col(2), lbspec,
                  pl.BlockSpec((hp, nchunk, HG_DK, HG_DK), lambda h, i: (h, nblk - 1 - i, 0, 0)), ocol],
        out_specs=[ocol, ocol, ocol, lbspec],
        scratch_shapes=[pltpu.VMEM((hp, HG_DK, HG_DK), F32)],
        compiler_params=_cparams(("parallel", "arbitrary")),
        name="hgrn_bwd",
    )(zh, zh, zh, lb3.reshape(1, -1), states, d_o)
    return dq, df, dv, dlb.reshape(w)


NEG = -1e30
ATT_GW = ATT_HEADS * ATT_DH


def _att_scores(q, kp, kc, has_prev):
    scale = ATT_DH ** -0.5
    i = lax.broadcasted_iota(jnp.int32, (ATT_BLK, ATT_BLK), 0)
    j = lax.broadcasted_iota(jnp.int32, (ATT_BLK, ATT_BLK), 1)
    s_p = jnp.where(jnp.logical_and(j >= i, has_prev), _dot_nt(q, kp) * scale, NEG)
    s_c = jnp.where(j <= i, _dot_nt(q, kc) * scale, NEG)
    return s_p, s_c


def _att_views(arrs, d):
    return [a.reshape(d, -1, ATT_GW) for a in arrs]


def _att_unview(a, d):
    return a.reshape(-1, ATT_GW) if d == 1 else a


ATT_QB = 4


def _attn_fwd(qb, kb, vb, g):
    d = ATT_PATTERNS[g][1]
    q2, k2, v2 = _att_views([qb, kb, vb], d)
    nblk = q2.shape[1] // ATT_BLK
    nq = ATT_QB if nblk % ATT_QB == 0 else 1
    rows = nq * ATT_BLK

    def body(q_ref, kc_ref, kp_ref, vc_ref, vp_ref, o_ref, l_ref):
        first = pl.program_id(1) == 0
        hss = [slice(h * ATT_DH, (h + 1) * ATT_DH) for h in range(ATT_HEADS)]
        for b in range(nq):
            rs = slice(b * ATT_BLK, (b + 1) * ATT_BLK)
            ps = slice((b - 1) * ATT_BLK, b * ATT_BLK)
            has_prev = jnp.logical_not(first) if b == 0 else True
            kv = [(kp_ref[:, hs], vp_ref[:, hs]) if b == 0 else (kc_ref[ps, hs], vc_ref[ps, hs]) for hs in hss]
            sc = [_att_scores(q_ref[rs, hs], kv[h][0], kc_ref[rs, hs], has_prev) for h, hs in enumerate(hss)]
            ms = [jnp.maximum(jnp.max(s_p, axis=1, keepdims=True), jnp.max(s_c, axis=1, keepdims=True)) for s_p, s_c in sc]
            ps_ = [(jnp.exp(s_p - m), jnp.exp(s_c - m)) for (s_p, s_c), m in zip(sc, ms)]
            ls = [jnp.sum(p_p, axis=1, keepdims=True) + jnp.sum(p_c, axis=1, keepdims=True) for p_p, p_c in ps_]
            os_ = [_dot(p_p.astype(BF16), kv[h][1]) + _dot(p_c.astype(BF16), vc_ref[rs, hss[h]]) for h, (p_p, p_c) in enumerate(ps_)]
            for h, hs in enumerate(hss):
                o_ref[rs, hs] = os_[h] / ls[h]
                l_ref[rs, hs] = jnp.broadcast_to(ms[h] + jnp.log(ls[h]), (ATT_BLK, ATT_DH))

    cur = pl.BlockSpec((None, rows, ATT_GW), lambda r, n: (r, n, 0))
    prev = pl.BlockSpec((None, ATT_BLK, ATT_GW), lambda r, n: (r, jnp.maximum(n * nq - 1, 0), 0))
    o, lse = pl.pallas_call(
        body,
        out_shape=[jax.ShapeDtypeStruct(q2.shape, F32)] * 2,
        grid=(d, nblk // nq),
        in_specs=[cur, cur, prev, cur, prev],
        out_specs=[cur, cur],
        compiler_params=_cparams(("parallel", "arbitrary")),
        name=f"attn_fwd_g{g}",
    )(q2, k2, k2, v2, v2)
    return _att_unview(o, d), _att_unview(lse, d)


def _attn_bwd(qb, kb, vb, o, lse, d_o, d_lse, g):
    d = ATT_PATTERNS[g][1]
    q2, k2, v2 = _att_views([qb, kb, vb], d)
    o2, l2, do2, dl2 = _att_views([o, lse, d_o, d_lse], d)
    nblk = q2.shape[1] // ATT_BLK
    nq = ATT_QB if nblk % ATT_QB == 0 else 1
    rows = nq * ATT_BLK
    ns = nblk // nq
    scale = ATT_DH ** -0.5

    def body(q_ref, kc_ref, kp_ref, vc_ref, vp_ref, o_ref, l_ref, do_ref, dl_ref, dq_ref, dk_ref, dv_ref, ck, cv):
        n = pl.program_id(1)

        @pl.when(n == 0)
        def _():
            ck[...] = jnp.zeros_like(ck)
            cv[...] = jnp.zeros_like(cv)

        first = n == ns - 1
        hss = [slice(h * ATT_DH, (h + 1) * ATT_DH) for h in range(ATT_HEADS)]
        heads = range(ATT_HEADS)
        pend_k, pend_v = [ck[:, hs] for hs in hss], [cv[:, hs] for hs in hss]
        for b in reversed(range(nq)):
            rs = slice(b * ATT_BLK, (b + 1) * ATT_BLK)
            ps = slice((b - 1) * ATT_BLK, b * ATT_BLK)
            has_prev = jnp.logical_not(first) if b == 0 else True
            q = [q_ref[rs, hs] for hs in hss]
            kc, vc = [kc_ref[rs, hs] for hs in hss], [vc_ref[rs, hs] for hs in hss]
            kp = [kp_ref[:, hs] if b == 0 else kc_ref[ps, hs] for hs in hss]
            vp = [vp_ref[:, hs] if b == 0 else vc_ref[ps, hs] for hs in hss]
            sc = [_att_scores(q[h], kp[h], kc[h], has_prev) for h in heads]
            dob = [do_ref[rs, hs].astype(BF16) for hs in hss]
            dp = [(_dot_nt(dob[h], vp[h]), _dot_nt(dob[h], vc[h])) for h in heads]
            delta = [jnp.sum(do_ref[rs, hs] * o_ref[rs, hs] - dl_ref[rs, hs], axis=1, keepdims=True) for hs in hss]
            pr = [(jnp.exp(sc[h][0] - l_ref[rs, hss[h]][:, 0:1]), jnp.exp(sc[h][1] - l_ref[rs, hss[h]][:, 0:1])) for h in heads]
            ds = [((pr[h][0] * (dp[h][0] - delta[h]) * scale).astype(BF16), (pr[h][1] * (dp[h][1] - delta[h]) * scale).astype(BF16))
                  for h in heads]
            pb = [(pr[h][0].astype(BF16), pr[h][1].astype(BF16)) for h in heads]
            dq = [_dot(ds[h][0], kp[h]) + _dot(ds[h][1], kc[h]) for h in heads]
            dk_c = [_dot_tn(ds[h][1], q[h]) for h in heads]
            dv_c = [_dot_tn(pb[h][1], dob[h]) for h in heads]
            dk_p = [_dot_tn(ds[h][0], q[h]) for h in heads]
            dv_p = [_dot_tn(pb[h][0], dob[h]) for h in heads]
            for h, hs in enumerate(hss):
                dq_ref[rs, hs] = dq[h]
                dk_ref[rs, hs] = pend_k[h] + dk_c[h]
                dv_ref[rs, hs] = pend_v[h] + dv_c[h]
            pend_k, pend_v = dk_p, dv_p
        for h, hs in enumerate(hss):
            ck[:, hs] = pend_k[h]
            cv[:, hs] = pend_v[h]

    cur = pl.BlockSpec((None, rows, ATT_GW), lambda r, n: (r, ns - 1 - n, 0))
    prev = pl.BlockSpec((None, ATT_BLK, ATT_GW), lambda r, n: (r, jnp.maximum((ns - 1 - n) * nq - 1, 0), 0))
    shp = jax.ShapeDtypeStruct(q2.shape, F32)
    dq, dk, dv = pl.pallas_call(
        body,
        out_shape=[shp, shp, shp],
        grid=(d, ns),
        in_specs=[cur, cur, prev, cur, prev, cur, cur, cur, cur],
        out_specs=[cur, cur, cur],
        scratch_shapes=[pltpu.VMEM((ATT_BLK, ATT_GW), F32), pltpu.VMEM((ATT_BLK, ATT_GW), F32)],
        compiler_params=_cparams(("parallel", "arbitrary")),
        name=f"attn_bwd_g{g}",
    )(q2, k2, k2, v2, v2, o2, l2, do2, dl2)
    return _att_unview(dq, d), _att_unview(dk, d), _att_unview(dv, d)


def _attn_fwd_1blk(qb, kb, vb, g):
    d = ATT_PATTERNS[g][1]
    q2, k2, v2 = _att_views([qb, kb, vb], d)
    nb = q2.shape[1] // ATT_BLK

    def body(q_ref, kc_ref, kp_ref, vc_ref, vp_ref, o_ref, l_ref):
        has_prev = pl.program_id(1) > 0
        for h in range(ATT_HEADS):
            hs = slice(h * ATT_DH, (h + 1) * ATT_DH)
            s_p, s_c = _att_scores(q_ref[:, hs], kp_ref[:, hs], kc_ref[:, hs], has_prev)
            m = jnp.maximum(jnp.max(s_p, axis=1, keepdims=True), jnp.max(s_c, axis=1, keepdims=True))
            p_p, p_c = jnp.exp(s_p - m), jnp.exp(s_c - m)
            l = jnp.sum(p_p, axis=1, keepdims=True) + jnp.sum(p_c, axis=1, keepdims=True)
            o = _dot(p_p.astype(BF16), vp_ref[:, hs]) + _dot(p_c.astype(BF16), vc_ref[:, hs])
            o_ref[:, hs] = o / l
            l_ref[:, hs] = jnp.broadcast_to(m + jnp.log(l), (ATT_BLK, ATT_DH))

    cur = pl.BlockSpec((None, ATT_BLK, ATT_GW), lambda r, n: (r, n, 0))
    prev = pl.BlockSpec((None, ATT_BLK, ATT_GW), lambda r, n: (r, jnp.maximum(n - 1, 0), 0))
    o, lse = pl.pallas_call(
        body,
        out_shape=[jax.ShapeDtypeStruct(q2.shape, F32)] * 2,
        grid=(d, nb),
        in_specs=[cur, cur, prev, cur, prev],
        out_specs=[cur, cur],
        compiler_params=_cparams(("parallel", "arbitrary")),
        name=f"attn_fwd_g{g}",
    )(q2, k2, k2, v2, v2)
    return _att_unview(o, d), _att_unview(lse, d)


def _attn_bwd_1blk(qb, kb, vb, o, lse, d_o, d_lse, g):
    d = ATT_PATTERNS[g][1]
    q2, k2, v2 = _att_views([qb, kb, vb], d)
    o2, l2, do2, dl2 = _att_views([o, lse, d_o, d_lse], d)
    nb = q2.shape[1] // ATT_BLK

    def body(q_ref, kc_ref, kp_ref, vc_ref, vp_ref, o_ref, l_ref, do_ref, dl_ref, dq_ref, dk_ref, dv_ref, ck, cv):
        n = pl.program_id(1)
        active = n < nb

        @pl.when(n == 0)
        def _():
            ck[...] = jnp.zeros_like(ck)
            cv[...] = jnp.zeros_like(cv)

        @pl.when(jnp.logical_not(active))
        def _():
            dk_ref[...] = ck[...]
            dv_ref[...] = cv[...]

        @pl.when(active)
        def _():
            has_prev = n > 0
            for h in range(ATT_HEADS):
                hs = slice(h * ATT_DH, (h + 1) * ATT_DH)
                q, kp, kc, vp, vc = q_ref[:, hs], kp_ref[:, hs], kc_ref[:, hs], vp_ref[:, hs], vc_ref[:, hs]
                s_p, s_c = _att_scores(q, kp, kc, has_prev)
                lse_h = l_ref[:, hs][:, 0:1]
                p_p, p_c = jnp.exp(s_p - lse_h), jnp.exp(s_c - lse_h)
                do = do_ref[:, hs]
                delta = jnp.sum(do * o_ref[:, hs] - dl_ref[:, hs], axis=1, keepdims=True)
                dob = do.astype(BF16)
                scale = ATT_DH ** -0.5
                ds_p = (p_p * (_dot_nt(dob, vp) - delta) * scale).astype(BF16)
                ds_c = (p_c * (_dot_nt(dob, vc) - delta) * scale).astype(BF16)
                dq_ref[:, hs] = _dot(ds_p, kp) + _dot(ds_c, kc)
                dk_ref[:, hs] = ck[:, hs] + _dot_tn(ds_p, q)
                dv_ref[:, hs] = cv[:, hs] + _dot_tn(p_p.astype(BF16), dob)
                ck[:, hs] = _dot_tn(ds_c, q)
                cv[:, hs] = _dot_tn(p_c.astype(BF16), dob)

    def qn(n):
        return jnp.minimum(n, nb - 1)

    cur = pl.BlockSpec((None, ATT_BLK, ATT_GW), lambda r, n: (r, qn(n), 0))
    prev = pl.BlockSpec((None, ATT_BLK, ATT_GW), lambda r, n: (r, jnp.maximum(qn(n) - 1, 0), 0))
    behind = pl.BlockSpec((None, ATT_BLK, ATT_GW), lambda r, n: (r, jnp.maximum(n - 1, 0), 0))
    shp = jax.ShapeDtypeStruct(q2.shape, F32)
    dq, dk, dv = pl.pallas_call(
        body,
        out_shape=[shp, shp, shp],
        grid=(d, nb + 1),
        in_specs=[cur, cur, prev, cur, prev, cur, cur, cur, cur],
        out_specs=[cur, behind, behind],
        scratch_shapes=[pltpu.VMEM((ATT_BLK, ATT_GW), F32), pltpu.VMEM((ATT_BLK, ATT_GW), F32)],
        compiler_params=_cparams(("parallel", "arbitrary")),
        name=f"attn_bwd_g{g}",
    )(q2, k2, k2, v2, v2, o2, l2, do2, dl2)
    return _att_unview(dq, d), _att_unview(dk, d), _att_unview(dv, d)


def _rms_parts(x, width):
    outs = []
    for lo in range(0, x.shape[1], width):
        xs = x[:, lo:lo + width].astype(F32)
        r = lax.rsqrt(jnp.mean(xs * xs, axis=1, keepdims=True) + EPS)
        outs.append((xs * r, r))
    return outs


def _rms_bwd_part(xh, r, dxh):
    return r * (dxh - xh * jnp.mean(dxh * xh, axis=1, keepdims=True))


def _norm_pro(a, consts):
    (xh, _), = _rms_parts(a[0], a[0].shape[1])
    return [(xh * consts[0]).astype(BF16)]


def _norm_bwd_fin(accs, ex, consts):
    xv, dres = ex
    (xh, r), = _rms_parts(xv, xv.shape[1])
    dx = dres + _rms_bwd_part(xh, r, accs[0] * consts[0])
    return [dx, dx], [_colsum8(accs[0] * xh)]


def _norm_fwd(x, gain):
    d = x.shape[1]

    def fn(ins, consts):
        (xh, _), = _rms_parts(ins[0], d)
        return [xh * consts[0]], []

    (h,), _ = _rowwise(fn, [(x, d, 0)], [gain.reshape(1, d)], [(d, BF16)], [], bm=512, name="norm_fwd")
    return h


def _norm_bwd(x, gain, dh, dres):
    d = x.shape[1]

    def fn(ins, consts):
        (xh, r), = _rms_parts(ins[0], d)
        dx = ins[2] + _rms_bwd_part(xh, r, ins[1] * consts[0])
        return [dx], [_colsum8(ins[1] * xh)]

    (dx,), (dg,) = _rowwise(fn, [(x, d, 0), (dh, d, 0), (dres, d, 0)], [gain.reshape(1, d)], [(d, F32)], [d],
                            bm=512, name="norm_bwd")
    return dx, dg


def _rot_sign():
    lane = lax.broadcasted_iota(jnp.int32, (1, ATT_DH), 1)
    return jnp.where(lane < ATT_DH // 2, -1.0, 1.0).astype(F32)


def _rope(y, cos, sin):
    return y * cos + pltpu.roll(y, ATT_DH // 2, axis=1) * _rot_sign() * sin


def _rope_t(dy, cos, sin):
    return dy * cos - pltpu.roll(dy * sin, ATT_DH // 2, axis=1) * _rot_sign()


def _qk_prep(zq, zk, zv, qn, kn, cos, sin):
    w = zq.shape[1]

    def fn(ins, consts):
        cs, sn = ins[3], ins[4]
        outs = []
        for z, gain in ((ins[0], consts[0]), (ins[1], consts[1])):
            for i, (xh, _) in enumerate(_rms_parts(z, ATT_DH)):
                outs.append(_rope(xh * gain[:, i * ATT_DH:(i + 1) * ATT_DH], cs, sn))
        outs += [ins[2][:, i * ATT_DH:(i + 1) * ATT_DH] for i in range(w // ATT_DH)]
        groups = [jnp.concatenate(outs[i:i + ATT_HEADS], axis=1) for i in range(0, len(outs), ATT_HEADS)]
        return groups, []

    outs, _ = _rowwise(fn, [(zq, w, 0), (zk, w, 0), (zv, w, 0), (cos, ATT_DH, 0), (sin, ATT_DH, 0)], [qn, kn],
                       [(ATT_GW, BF16, ATT_PATTERNS[g][1]) for g in range(ATT_GROUPS)] * 3, [], bm=256, name="qk_prep")
    return outs[0:3], outs[3:6], outs[6:9]


def _qk_prep_bwd(zq, zk, dq_g, dk_g, dv_g, qn, kn, cos, sin):
    w = zq.shape[1]

    def fn(ins, consts):
        cs, sn = ins[2], ins[3]
        outs, sums = [], []
        for z, gain, dparts in ((ins[0], consts[0], ins[4:7]), (ins[1], consts[1], ins[7:10])):
            dout = jnp.concatenate(dparts, axis=1)
            dz, dgain = [], []
            for i, (xh, r) in enumerate(_rms_parts(z, ATT_DH)):
                hs = slice(i * ATT_DH, (i + 1) * ATT_DH)
                dy = _rope_t(dout[:, hs], cs, sn)
                dgain.append(_colsum8(dy * xh))
                dz.append(_rms_bwd_part(xh, r, dy * gain[:, hs]))
            outs.append(jnp.concatenate(dz, axis=1))
            sums.append(jnp.concatenate(dgain, axis=1))
        outs.append(jnp.concatenate(ins[10:13], axis=1))
        return outs, sums

    ins = [(zq, w, 0), (zk, w, 0), (cos, ATT_DH, 0), (sin, ATT_DH, 0)]
    for parts in (dq_g, dk_g, dv_g):
        ins += [(a, ATT_GW, 0, ATT_PATTERNS[g][1]) for g, a in enumerate(parts)]
    (dzq, dzk, dzv), (dqn, dkn) = _rowwise(fn, ins, [qn, kn], [(w, BF16)] * 3, [w, w], bm=256, name="qk_prep_bwd")
    return dzq, dzk, dzv, dqn, dkn


def _post_a(o_raw, zh, gout):
    w = o_raw.shape[1]

    def fn(ins, consts):
        oh = jnp.concatenate([xh for xh, _ in _rms_parts(ins[0], HG_DK)], axis=1)
        hg = ins[1]
        return [oh * consts[0] * (hg * _sigmoid(hg))], []

    (y,), _ = _rowwise(fn, [(o_raw, w, 0), (zh, w, 3)], [gout.reshape(1, w)], [(w, BF16)], [], bm=512, name="post_a")
    return y


def _post_a_bwd(o_raw, zh, gout, dy):
    w = o_raw.shape[1]

    def fn(ins, consts):
        parts = _rms_parts(ins[0], HG_DK)
        oh = jnp.concatenate([xh for xh, _ in parts], axis=1)
        hg, dyv, gain = ins[1], ins[2], consts[0]
        sg = _sigmoid(hg)
        s = hg * sg
        doh = dyv * gain * s
        do = jnp.concatenate([_rms_bwd_part(xh, r, doh[:, i * HG_DK:(i + 1) * HG_DK]) for i, (xh, r) in enumerate(parts)], axis=1)
        dhg = dyv * oh * gain * (sg * (1.0 + hg * (1.0 - sg)))
        return [do, dhg], [_colsum8(dyv * oh * s)]

    (do, dhg), (dgain,) = _rowwise(fn, [(o_raw, w, 0), (zh, w, 3), (dy, w, 0)], [gout.reshape(1, w)],
                                   [(w, F32), (w, BF16)], [w], bm=512, name="post_a_bwd")
    return do, dhg, dgain


def _merge_alpha(lses):
    m = jnp.maximum(jnp.maximum(lses[0], lses[1]), lses[2])
    e = [jnp.exp(l - m) for l in lses]
    inv = 1.0 / (e[0] + e[1] + e[2])
    return [x * inv for x in e]


def _group_ins(parts):
    return [(a, ATT_GW, 0, ATT_PATTERNS[g][1]) for g, a in enumerate(parts)]


def _merge_b(o_g, lse_g):
    def fn(ins, consts):
        al = _merge_alpha(ins[3:6])
        return [al[0] * ins[0] + al[1] * ins[1] + al[2] * ins[2]], []

    (y,), _ = _rowwise(fn, _group_ins(o_g) + _group_ins(lse_g), [], [(ATT_GW, BF16)], [], bm=512, name="merge_b")
    return y


def _merge_b_bwd(o_g, lse_g, dy):
    def fn(ins, consts):
        al = _merge_alpha(ins[3:6])
        dyv = ins[6]
        dal = [dyv * ins[i] for i in range(3)]
        tot = al[0] * dal[0] + al[1] * dal[1] + al[2] * dal[2]
        return [al[i] * dyv for i in range(3)] + [al[i] * (dal[i] - tot) for i in range(3)], []

    outs, _ = _rowwise(fn, _group_ins(o_g) + _group_ins(lse_g) + [(dy, ATT_GW, 0)], [],
                       [(ATT_GW, F32, ATT_PATTERNS[g][1]) for g in range(ATT_GROUPS)] * 2, [], bm=512, name="merge_b_bwd")
    return outs[:3], outs[3:]


def _loss_head(y, target):
    d = y.shape[1]

    def fn(ins, consts):
        e = ins[0] - ins[1]
        return [e * (1.0 / d)] * 2, [_colsum8(e * e)]

    (dy, dyb), (sq,) = _rowwise(fn, [(y, d, 0), (target, d, 0)], [], [(d, F32), (d, BF16)], [d], bm=512, name="loss_head")
    return 0.5 * jnp.sum(sq) / d, dy, dyb


def _silu_grad(a):
    s = _sigmoid(a)
    return s * (1.0 + a * (1.0 - s))


def _ffn_fwd(x, gain, wt, wo_fn, tag):
    t, d = x.shape
    f = wt.shape[0] // 2

    def act(accs, ex, consts):
        a, b = accs
        s = _sigmoid(a)
        sa = a * s
        return (sa * b, b, 0.5 * sa, 0.5 * (s + sa * (1.0 - s)))

    bn = FFN_BN if f % FFN_BN == 0 else 256
    u, b, sa, sp, h = _mm([x], [wt, wt], [(0, 0, 0), (0, 1, 1)], 2, act, [BF16] * 4, m=t, n=f, k=d, tb=True,
                          bm=512, bn=bn, bk=d, b_off=[(0, 0), (f // min(bn, f), 0)],
                          consts=[gain.reshape(1, d)], a_pro=_norm_pro, chunk=MXU_COLS, name=f"ffn_in_{tag}")
    wo = wo_fn(u)
    (y,) = _mm([u], [wo], [(0, 0, 0)], 1, lambda accs, ex: (ex[0] + 0.5 * accs[0],), [F32], m=t, n=d, k=f,
               bm=512, bn=d, bk=f, extras=[x], name=f"ffn_out_{tag}")
    return y, (x, h, u, b, sa, sp, wo)


def _ffn_bwd(dy, dyb, saved, gain, wt, tag, tok, emit):
    x, h, u, b, sa, sp, wo = saved
    t, d = x.shape
    f = wo.shape[0]

    def dact(accs, ex, consts):
        bv, sav, spv = (e.astype(F32) for e in ex)
        return (accs[0] * bv * spv, accs[0] * sav)

    bn = FFN_BN if f % FFN_BN == 0 else 256
    da, db = _mm([dyb], [wo], [(0, 0, 0)], 1, dact, [BF16, BF16], m=t, n=f, k=d, tb=True, bm=512, bn=bn, bk=d,
                 extras=[b, sa, sp], n_outer=True, chunk=MXU_COLS, consts=[tok], name=f"ffn_dact_{tag}")
    (dwo,) = _mm([u], [dyb], [(0, 0, 0)], 1, lambda accs, ex: (0.5 * accs[0],), [BF16], m=f, n=d, k=t, ta=True,
                 bm=1408, bn=d, bk=1024, name=f"ffn_dwo_{tag}")
    (dwt,) = _mm([da, db], [h], [(0, 0, 0)], 1, _first, [BF16], m=2 * f, n=d, k=t, ta=True, bm=min(1408, f), bn=d,
                 bk=1024, a_cat=True, name=f"ffn_dwt_{tag}")
    tok = emit(dwt, dwo)
    bk = min(FFN_BN, f)
    dx, dxb, dgain = _mm([da, db], [wt, wt], [(0, 0, 0), (1, 1, 0)], 1, _norm_bwd_fin, [F32, BF16], m=t, n=d, k=f,
                         bm=512, bn=d, bk=bk, b_off=[(0, 0), (0, f // bk)], extras=[x, dy],
                         consts=[gain.reshape(1, d), tok], n_sums=1, name=f"ffn_dh_{tag}")
    return dx, dxb, jnp.sum(dgain, axis=0), tok


FFN_BN = 2816
Z_SPLITS = (("h", 4096), ("q", 1536), ("k", 1536), ("v", 1536), ("g", 2048))


def _mix_fwd(x, p, cos, sin):
    t, d = x.shape
    z, off, hm = {}, 0, None
    for nm, width in Z_SPLITS:
        bn = 1024 if off % 1024 == 0 and width % 1024 == 0 else 512
        first = hm is None
        res = _mm([x if first else hm], [p["wint"]], [(0, 0, 0)], 1, (lambda accs, ex, consts: (accs[0],)) if first else _first,
                  [F32 if nm == "h" else BF16], m=t, n=width, k=d, tb=True, bm=1024, bn=bn, bk=d, b_off=[(off // bn, 0)],
                  consts=[p["gm"].reshape(1, d)] if first else (), a_pro=_norm_pro if first else None, name=f"mix_in_{nm}")
        z[nm] = res[0]
        hm = res[1] if first else hm
        off += width
    o_raw, states = _hgrn_fwd(z["h"], p["lb3"])
    qb, kb, vb = _qk_prep(z["q"], z["k"], z["v"], p["qn"], p["kn"], cos, sin)
    o_g, lse_g = zip(*[_attn_fwd(qb[g], kb[g], vb[g], g) for g in range(ATT_GROUPS)])
    oa = _post_a(o_raw, z["h"], p["gout"])
    ob = _merge_b(o_g, lse_g)
    late = p["late"](ob)
    p = dict(p, **late)
    (ya,) = _mm([oa], [p["wa"]], [(0, 0, 0)], 1, _first, [F32], m=t, n=d, k=oa.shape[1], bm=1024, bn=d, bk=oa.shape[1],
                name="branch_a")

    def gate(accs, ex):
        return (_sigmoid(ex[0].astype(F32)) * ex[2] + _sigmoid(ex[1].astype(F32)) * accs[0], accs[0])

    merged, yb = _mm([ob], [p["wbt"]], [(0, 0, 0)], 1, gate, [BF16, F32], m=t, n=d, k=ATT_GW, tb=True, bm=512, bn=d,
                     bk=ATT_GW, extras=[z["g"], z["g"], ya], e_off=[0, 1, 0], chunk=MXU_COLS, name="branch_b_gate")
    (y,) = _mm([merged], [p["wo"]], [(0, 0, 0)], 1, lambda accs, ex: (ex[0] + accs[0],), [F32], m=t, n=d, k=d,
               bm=1024, bn=d, bk=d, extras=[x], name="mix_out")
    return y, (x, hm, z, o_raw, states, qb, kb, vb, o_g, lse_g, oa, ob, ya, yb, merged, late)


def _mix_bwd(dy, dyb, saved, p, cos, sin, tok):
    x, hm, z, o_raw, states, qb, kb, vb, o_g, lse_g, oa, ob, ya, yb, merged, late = saved
    p = dict(p, **late)
    t, d = x.shape
    w = oa.shape[1]

    def dgate(accs, ex, consts):
        dm = accs[0]
        sa, sb = _sigmoid(ex[0].astype(F32)), _sigmoid(ex[1].astype(F32))
        return (sa * dm, sb * dm, dm * ex[2] * sa * (1.0 - sa), dm * ex[3] * sb * (1.0 - sb))

    dya, dyb_, dga, dgb = _mm([dyb], [p["wo"]], [(0, 0, 0)], 1, dgate, [BF16] * 4, m=t, n=d, k=d, tb=True, bm=512, bn=d,
                              bk=d, extras=[z["g"], z["g"], ya, yb], e_off=[0, 1, 0, 0], chunk=MXU_COLS, consts=[tok], name="mix_out_bwd")
    (dwo,) = _mm([merged], [dyb], [(0, 0, 0)], 1, _first, [BF16], m=d, n=d, k=t, ta=True, bm=d, bn=d, bk=1024, name="mix_dwo")
    (doa,) = _mm([dya], [p["wa"]], [(0, 0, 0)], 1, _first, [F32], m=t, n=w, k=d, tb=True, bm=1024, bn=w, bk=d, name="branch_a_bwd")
    (dwa,) = _mm([oa], [dya], [(0, 0, 0)], 1, _first, [BF16], m=w, n=d, k=t, ta=True, bm=w, bn=d, bk=1024, name="branch_a_dw")
    (dob,) = _mm([dyb_], [p["wbt"]], [(0, 0, 0)], 1, _first, [F32], m=t, n=ATT_GW, k=d, bm=1024, bn=ATT_GW, bk=d,
                 name="branch_b_bwd")
    (dwbt,) = _mm([dyb_], [ob], [(0, 0, 0)], 1, _first, [BF16], m=d, n=ATT_GW, k=t, ta=True, bm=d, bn=ATT_GW, bk=1024,
                  name="branch_b_dw")
    do_raw, dhg, dgout = _post_a_bwd(o_raw, z["h"], p["gout"], doa)
    do_g, dlse_g = _merge_b_bwd(o_g, lse_g, dob)
    dq_g, dk_g, dv_g = zip(*[_attn_bwd(qb[g], kb[g], vb[g], o_g[g], lse_g[g], do_g[g], dlse_g[g], g)
                             for g in range(ATT_GROUPS)])
    dzq, dzk, dzv, dqn, dkn = _qk_prep_bwd(z["q"], z["k"], dq_g, dk_g, dv_g, p["qn"], p["kn"], cos, sin)
    dhq, dhf, dhi, lbsum = _hgrn_bwd(z["h"], p["lb3"], states, do_raw)
    dz = jnp.concatenate([dhq, dhf, dhi, dhg, dzq, dzk, dzv, dga, dgb], axis=1)
    pw = dz.shape[1]
    (dwint,) = _mm([dz], [hm], [(0, 0, 0)], 1, _first, [BF16], m=pw, n=d, k=t, ta=True, bm=1536, bn=d, bk=1024, name="mix_in_dw")
    dx, dxb, dgm = _mm([dz], [p["wint"]], [(0, 0, 0)], 1, _norm_bwd_fin, [F32, BF16], m=t, n=d, k=pw, bm=512, bn=d, bk=1536,
                       extras=[x, dy], consts=[p["gm"].reshape(1, d)], n_sums=1, name="mix_in_bwd")
    return dx, dxb, dict(gm=jnp.sum(dgm, axis=0), wint=dwint, lbsum=lbsum, gout=dgout, qn=dqn, kn=dkn, wa=dwa, wbt=dwbt, wo=dwo)


def _rope_tables(t):
    pos = jnp.arange(t, dtype=F32)
    inv = ROPE_THETA ** (-jnp.arange(0, ATT_DH, 2, dtype=F32) / ATT_DH)
    ang = pos[:, None] * inv[None, :]
    ang = jnp.concatenate([ang, ang], axis=-1)
    return jnp.cos(ang), jnp.sin(ang)


def _lower_bounds(logits):
    lb = jnp.cumsum(jax.nn.softmax(logits, axis=0), axis=0)
    return lb - lb[0:1]


def _head_gain(g):
    return jnp.tile(g[:, None, :], (1, ATT_HEADS, 1)).reshape(1, ATT_GROUPS * ATT_GW)


SMALL_GRADS = ("ffn1_norm", "mix_norm", "lbsum", "hgrn_out_norm", "attn_q_norm", "attn_k_norm", "ffn2_norm")


def _local_step(x, target, small, fetch, emit):
    t = x.shape[0]
    depth = small["ffn1_norm"].shape[0]
    cos, sin = _rope_tables(t)
    lb_all = _lower_bounds(small["hgrn_lb_logits"])
    saved = []
    for l in range(depth):
        w1t = fetch("w1t", l, x)["w1t"]
        x, s1 = _ffn_fwd(x, small["ffn1_norm"][l], w1t, lambda after, l=l: fetch("w1o", l, after)["w1o"], "1")
        p = dict(gm=small["mix_norm"][l], wint=fetch("wint", l, x)["wint"], lb3=lb_all[l].reshape(-1, 1, HG_DK),
                 gout=small["hgrn_out_norm"][l], qn=_head_gain(small["attn_q_norm"][l]),
                 kn=_head_gain(small["attn_k_norm"][l]), late=functools.partial(fetch, "mout", l))
        x, sm = _mix_fwd(x, p, cos, sin)
        w2t = fetch("w2t", l, x)["w2t"]
        x, s2 = _ffn_fwd(x, small["ffn2_norm"][l], w2t, lambda after, l=l: fetch("w2o", l, after)["w2o"], "2")
        saved.append((p, w1t, w2t, s1, sm, s2))
    loss, dx, dxb = _loss_head(x, target)
    gsmall = {k: [None] * depth for k in SMALL_GRADS}
    tok = jnp.zeros((8, 128), F32)
    for l in reversed(range(depth)):
        p, w1t, w2t, s1, sm, s2 = saved[l]
        dx, dxb, gsmall["ffn2_norm"][l], tok = _ffn_bwd(
            dx, dxb, s2, small["ffn2_norm"][l], w2t, "2", tok, lambda dwt, dwo, l=l: emit("ffn2", l, dict(w2t=dwt, w2o=dwo), None))
        dx, dxb, gm = _mix_bwd(dx, dxb, sm, p, cos, sin, tok)
        tok = emit("mix", l, {k: gm[k] for k in ("wint", "wa", "wbt", "wo")}, None)
        gsmall["mix_norm"][l], gsmall["lbsum"][l], gsmall["hgrn_out_norm"][l] = gm["gm"], gm["lbsum"], gm["gout"]
        for k, src in (("attn_q_norm", "qn"), ("attn_k_norm", "kn")):
            gsmall[k][l] = jnp.sum(gm[src].reshape(ATT_GROUPS, ATT_HEADS, ATT_DH), axis=1)
        dx, dxb, gsmall["ffn1_norm"][l], tok = _ffn_bwd(
            dx, dxb, s1, small["ffn1_norm"][l], w1t, "1", tok, lambda dwt, dwo, l=l: emit("ffn1", l, dict(w1t=dwt, w1o=dwo), None))
    emit("small", 0, {}, ({k: jnp.stack(v) for k, v in gsmall.items()}, loss))
    return dx


_HBM = pl.BlockSpec(memory_space=pltpu.HBM)
_SEM = pl.BlockSpec(memory_space=pltpu.SEMAPHORE)
_EFFECT = pltpu.SideEffectType.DATAFLOW_SIDE_EFFECTING


def _peer(p):
    x, y, c = lax.axis_index("x"), lax.axis_index("y"), lax.axis_index("c")
    me = 4 * x + 2 * y + c
    return (1 - x if p & 4 else x, 1 - y if p & 2 else y, 1 - c if p & 1 else c), jnp.bitwise_xor(me, p), me


def _xchg_copy(src, land, mode, send_sems, recv_sems, k, p, arriving):
    peer, peer_id, me = _peer(p)
    block = src if mode == "gather" else src.at[peer_id]
    return pltpu.make_async_remote_copy(
        src_ref=block, dst_ref=land.at[peer_id if arriving else me], send_sem=send_sems.at[k * (N_DEV - 1) + p - 1],
        recv_sem=recv_sems.at[k * (N_DEV - 1) + p - 1], device_id=peer, device_id_type=MESH)


def _xchg_start(srcs, modes, groups, name):
    n, ng = len(srcs), len(groups)

    def body(*refs):
        src = refs[:n]
        sems = refs[n:n + 2 * ng]
        land = refs[n + 2 * ng + n:n + 2 * ng + 2 * n]
        token = refs[n + 2 * ng + 2 * n]
        for gi, idx in enumerate(groups):
            for ki, k in enumerate(idx):
                for p in range(1, N_DEV):
                    _xchg_copy(src[k], land[k], modes[k], sems[2 * gi], sems[2 * gi + 1], ki, p, False).start()
        token[...] = jnp.zeros_like(token)

    sem_shapes = []
    for idx in groups:
        sem_shapes += [pltpu.SemaphoreType.DMA((len(idx) * (N_DEV - 1),))] * 2
    outs = pl.pallas_call(
        body,
        out_shape=sem_shapes + [pltpu.HBM(a.shape, a.dtype) for a in srcs]
        + [pltpu.HBM((N_DEV,) + a.shape[-2:], a.dtype) for a in srcs] + [jax.ShapeDtypeStruct((8, 128), F32)],
        in_specs=[_HBM] * n,
        out_specs=[_SEM] * (2 * ng) + [_HBM] * (2 * n) + [pl.BlockSpec(memory_space=pltpu.VMEM)],
        input_output_aliases={i: 2 * ng + i for i in range(n)},
        compiler_params=pltpu.CompilerParams(has_side_effects=_EFFECT),
        name=name,
    )(*[pltpu.with_memory_space_constraint(a, pltpu.HBM) for a in srcs])
    sems = [(outs[2 * gi], outs[2 * gi + 1]) for gi in range(ng)]
    return sems, outs[2 * ng:2 * ng + n], outs[2 * ng + n:2 * ng + 2 * n], outs[-1]


def _xchg_wait_call(srcs, lands, modes, sems, after, name):
    n = len(srcs)

    def body(*refs):
        src, land = refs[:n], refs[n:2 * n]
        send_sems, recv_sems = refs[2 * n], refs[2 * n + 1]
        for p in range(1, N_DEV):
            for k in range(n):
                cp = _xchg_copy(src[k], land[k], modes[k], send_sems, recv_sems, k, p, True)
                cp.wait_send()
                cp.wait_recv()

    outs = pl.pallas_call(
        body,
        out_shape=[pltpu.HBM(a.shape, a.dtype) for a in list(srcs) + list(lands)],
        in_specs=[_HBM] * (2 * n) + [_SEM, _SEM, pl.BlockSpec(memory_space=pl.ANY)],
        out_specs=[_HBM] * (2 * n),
        input_output_aliases={i: i for i in range(2 * n)},
        compiler_params=pltpu.CompilerParams(has_side_effects=_EFFECT),
        name=name,
    )(*srcs, *lands, sems[0], sems[1], after)
    return outs[:n], outs[n:]


def _xchg_wait(srcs, lands, modes, sems, after, name):
    srcs, lands = _xchg_wait_call(srcs, lands, modes, sems, after, name)
    me = 4 * lax.axis_index("x") + 2 * lax.axis_index("y") + lax.axis_index("c")
    done = []
    for a, land, mode in zip(srcs, lands, modes):
        own = a[None] if mode == "gather" else lax.dynamic_slice_in_dim(a, me, 1, axis=0)
        done.append(lax.dynamic_update_slice(land, own, (me, 0, 0)))
    return done


def _sum_slots(land):
    g, _, r, c = land.shape
    br = r // 2 if (r % 32 == 0 and r >= 256) else r

    def body(l_ref, o_ref):
        acc = l_ref[0, 0].astype(F32)
        for j in range(1, N_DEV):
            acc = acc + l_ref[0, j].astype(F32)
        o_ref[0] = acc

    return pl.pallas_call(
        body,
        out_shape=jax.ShapeDtypeStruct((g, r, c), F32),
        grid=(g, r // br),
        in_specs=[pl.BlockSpec((1, N_DEV, br, c), lambda i, j: (i, 0, j, 0))],
        out_specs=pl.BlockSpec((1, br, c), lambda i, j: (i, j, 0)),
        compiler_params=_cparams(("parallel", "parallel")),
        name="sum_slots",
    )(land)


def _adamw(w, g, m, v):
    shape = w.shape
    cols = shape[-1]
    rows = int(np.prod(shape[:-1]))
    bm = max(b for b in range(8, 513, 8) if rows % b == 0) if rows % 8 == 0 else rows
    c1 = 1.0 - ADAM_B1 ** ADAM_STEP
    c2 = 1.0 - ADAM_B2 ** ADAM_STEP

    def fn(ins, consts):
        wv, gv, mv, vv = ins
        m2 = ADAM_B1 * mv + (1.0 - ADAM_B1) * gv
        v2 = ADAM_B2 * vv + (1.0 - ADAM_B2) * (gv * gv)
        delta = -ADAM_LR * ((m2 / c1) / (jnp.sqrt(v2 / c2) + ADAM_EPS) + ADAM_WD * wv)
        return [delta, m2, v2], []

    outs, _ = _rowwise(fn, [(a.reshape(rows, cols), cols, 0) for a in (w, g, m, v)], [], [(cols, F32)] * 3, [],
                       bm=bm, name="adamw")
    return [o.reshape(shape) for o in outs]


BIG = ("w1t", "w1o", "wint", "wa", "wbt", "wo", "w2t", "w2o")
FETCH_GROUPS = dict(w1t=("w1t",), w1o=("w1o",), wint=("wint",), mout=("wa", "wbt", "wo"), w2t=("w2t",), w2o=("w2o",))
SMALL_ROWS = (("ffn1_norm", 0), ("mix_norm", 2), ("lbsum", 4), ("hgrn_out_norm", 6), ("ffn2_norm", 8),
              ("attn_q_norm", 10), ("attn_k_norm", 12))
SMALL_PACK_ROWS = 16


def kernel(x, ffn1_norm, ffn1_w_in, ffn1_w_out, mix_norm, w_in, hgrn_lb_logits, hgrn_out_norm, attn_q_norm, attn_k_norm, w_branch_a, w_branch_b, w_out, ffn2_norm, ffn2_w_in, ffn2_w_out, loss_target, m_ffn1_norm, m_ffn1_w_in, m_ffn1_w_out, m_mix_norm, m_w_in, m_hgrn_lb_logits, m_hgrn_out_norm, m_attn_q_norm, m_attn_k_norm, m_w_branch_a, m_w_branch_b, m_w_out, m_ffn2_norm, m_ffn2_w_in, m_ffn2_w_out, v_ffn1_norm, v_ffn1_w_in, v_ffn1_w_out, v_mix_norm, v_w_in, v_hgrn_lb_logits, v_hgrn_out_norm, v_attn_q_norm, v_attn_k_norm, v_w_branch_a, v_w_branch_b, v_w_out, v_ffn2_norm, v_ffn2_w_in, v_ffn2_w_out):
    names = ("ffn1_norm", "ffn1_w_in", "ffn1_w_out", "mix_norm", "w_in", "hgrn_lb_logits", "hgrn_out_norm", "attn_q_norm",
             "attn_k_norm", "w_branch_a", "w_branch_b", "w_out", "ffn2_norm", "ffn2_w_in", "ffn2_w_out")
    w = dict(zip(names, (ffn1_norm, ffn1_w_in, ffn1_w_out, mix_norm, w_in, hgrn_lb_logits, hgrn_out_norm, attn_q_norm,
                         attn_k_norm, w_branch_a, w_branch_b, w_out, ffn2_norm, ffn2_w_in, ffn2_w_out)))
    m = dict(zip(names, (m_ffn1_norm, m_ffn1_w_in, m_ffn1_w_out, m_mix_norm, m_w_in, m_hgrn_lb_logits, m_hgrn_out_norm,
                         m_attn_q_norm, m_attn_k_norm, m_w_branch_a, m_w_branch_b, m_w_out, m_ffn2_norm, m_ffn2_w_in, m_ffn2_w_out)))
    v = dict(zip(names, (v_ffn1_norm, v_ffn1_w_in, v_ffn1_w_out, v_mix_norm, v_w_in, v_hgrn_lb_logits, v_hgrn_out_norm,
                         v_attn_q_norm, v_attn_k_norm, v_w_branch_a, v_w_branch_b, v_w_out, v_ffn2_norm, v_ffn2_w_in, v_ffn2_w_out)))
    depth, d = ffn1_norm.shape

    def tr(a):
        return jnp.swapaxes(a, 1, 2)

    shard = dict(w1t=tr(ffn1_w_in), w1o=ffn1_w_out, wint=tr(w_in), wa=w_branch_a,
                 wbt=tr(w_branch_b).reshape(depth, -1, d), wo=w_out, w2t=tr(ffn2_w_in), w2o=ffn2_w_out)
    order = [(g, l) for l in range(depth) for g in FETCH_GROUPS]
    started = {}
    for name, part in (("gather_start", order),):
        flat = [(l, k) for g, l in part for k in FETCH_GROUPS[g]]
        groups, pos = [], 0
        for g, l in part:
            groups.append(list(range(pos, pos + len(FETCH_GROUPS[g]))))
            pos += len(FETCH_GROUPS[g])
        sems, srcs, lands, _ = _xchg_start([shard[k][l].astype(BF16) for l, k in flat], ["gather"] * len(flat), groups, name)
        for gi, key in enumerate(part):
            started[key] = ([srcs[i] for i in groups[gi]], [lands[i] for i in groups[gi]], sems[gi])

    def fetch(group, l, after):
        srcs, lands, sems = started[group, l]
        lands = _xchg_wait(srcs, lands, ["gather"] * len(srcs), sems, after, f"gather_wait_{group}{l}")
        out = {}
        for k, land in zip(FETCH_GROUPS[group], lands):
            out[k] = land.reshape(d, -1) if k == "wbt" else land.reshape(-1, d)
        return out

    pending = []

    def emit(group, l, g, final):
        keys = list(g)
        srcs = [g[k].reshape(N_DEV, -1, d) for k in keys]
        modes = ["scatter"] * len(keys)
        if final is not None:
            gsmall, loss = final
            pack = jnp.zeros((SMALL_PACK_ROWS, d), F32)
            for k, r0 in SMALL_ROWS:
                rows = gsmall[k].reshape(depth, -1)
                pack = pack.at[r0:r0 + depth, :rows.shape[1]].set(rows)
            srcs.append(pack.at[14, :].set(loss))
            modes.append("gather")
            keys.append("small")
        sems, s_thru, l_thru, token = _xchg_start(srcs, modes, [list(range(len(srcs)))], f"grads_start_{group}{l}")
        pending.append((group, l, keys, modes, sems[0], s_thru, l_thru))
        return token

    small = {k: w[k] for k in ("ffn1_norm", "mix_norm", "hgrn_lb_logits", "hgrn_out_norm", "attn_q_norm", "attn_k_norm", "ffn2_norm")}
    dx = _local_step(x[0], loss_target[0], small, fetch, emit)

    summed, after = {}, dx
    for group, l, keys, modes, sems, s_thru, l_thru in pending:
        lands = _xchg_wait(s_thru, l_thru, modes, sems, after, f"grads_wait_{group}{l}")
        for k, land in zip(keys, lands):
            summed[k, l] = _sum_slots(land[None])[0]
        after = summed[keys[-1], l]
    gsum = {k: jnp.stack([summed[k, l] for l in range(depth)]) for k in BIG}
    tot = summed["small", 0]

    grads = {}
    for k, r0 in SMALL_ROWS:
        shp = (depth,) + (w[k].shape[1:] if k != "lbsum" else (d,))
        grads[k] = tot[r0:r0 + depth, :int(np.prod(shp[1:]))].reshape(shp)
    _, lb_vjp = jax.vjp(_lower_bounds, hgrn_lb_logits)
    grads["hgrn_lb_logits"] = lb_vjp(grads.pop("lbsum"))[0]
    grads["ffn1_w_in"], grads["ffn1_w_out"] = tr(gsum["w1t"]), gsum["w1o"]
    grads["w_in"], grads["w_branch_a"] = tr(gsum["wint"]), gsum["wa"]
    grads["w_branch_b"] = tr(gsum["wbt"].reshape(depth, d // N_DEV, -1))
    grads["w_out"] = gsum["wo"]
    grads["ffn2_w_in"], grads["ffn2_w_out"] = tr(gsum["w2t"]), gsum["w2o"]

    upd = {k: _adamw(w[k], grads[k], m[k], v[k]) for k in names}
    return (tot[14, 0], dx[None], *[grads[k] for k in names], *[upd[k][0] for k in names],
            *[upd[k][1] for k in names], *[upd[k][2] for k in names])
```

```python
import functools

import jax
import jax.numpy as jnp
import numpy as np
from jax import lax
from jax.experimental import pallas as pl
from jax.experimental.pallas import tpu as pltpu

F32 = jnp.float32
BF16 = jnp.bfloat16

N_DEV = 8
EPS = 1e-6
HG_DK = 128
HG_CHUNK = 64
HG_SUB = 16
HG_HP = 8
ATT_PATTERNS = ((128, 1), (512, 4), (2048, 16))
ATT_GROUPS = 3
ATT_HEADS = 4
ATT_DH = 128
ATT_BLK = 128
ROPE_THETA = 10000.0
ADAM_LR, ADAM_B1, ADAM_B2, ADAM_EPS, ADAM_WD, ADAM_STEP = 0.001, 0.9, 0.999, 1e-08, 0.01, 10
VMEM_LIMIT_BYTES = 56 * 1024 * 1024
MXU_COLS = 256
MESH = pl.DeviceIdType.MESH


def _cparams(sem, **kw):
    return pltpu.CompilerParams(dimension_semantics=sem, vmem_limit_bytes=VMEM_LIMIT_BYTES, **kw)


def _sigmoid(x):
    return 1.0 / (1.0 + jnp.exp(-x))


def _mm(a_list, b_list, pairs, n_acc, fin, out_dtypes, *, m, n, k, ta=False, tb=False, bm, bn, bk,
        b_off=None, extras=(), e_off=None, n_outer=False, consts=(), a_pro=None, n_sums=0, chunk=0, a_cat=False, name):
    bm, bn, bk = min(bm, m), min(bn, n), min(bk, k)
    assert m % bm == 0 and n % bn == 0 and k % bk == 0, (name, m, n, k, bm, bn, bk)
    nk = k // bk
    assert not (a_pro and (nk > 1 or ta or n_outer)) and not (n_sums and (bn != n or n_outer)), name
    assert not (chunk and (nk > 1 or n_sums or chunk % 128)), name
    if a_cat:
        unit = bm if ta else bk
        widths = [a.shape[1] for a in a_list]
        assert all(w % unit == 0 for w in widths) and sum(widths) == (m if ta else k) and not a_pro, name
        assert ta or (nk > 1 and n_acc == 1 and len(b_list) == 1 and not chunk), name
        cat_counts = [w // unit for w in widths]
        cat_starts = [sum(cat_counts[:i]) for i in range(len(widths))]
    b_off = b_off or [(0, 0)] * len(b_list)
    e_off = e_off or [0] * len(extras)
    na, nb, ne, nc, no = len(a_list), len(b_list), len(extras), len(consts), len(out_dtypes)
    nao = na if a_pro else 0
    dn = (((0,) if ta else (1,), (1,) if tb else (0,)), ((), ()))

    def body(*refs):
        refs = list(refs)
        a_refs, b_refs, e_refs, c_refs, o_refs, ao_refs, s_refs = (
            [refs.pop(0) for _ in range(cnt)] for cnt in (na, nb, ne, nc, no, nao, n_sums))
        acc_refs = refs
        kk = pl.program_id(2)
        first = pl.program_id(0) == 0
        cvals = [c[...] for c in c_refs]
        a_vals = [r[...] for r in a_refs]
        if a_cat and ta:
            col = pl.program_id(1 if n_outer else 0)
            sel = a_vals[0]
            for start, v in zip(cat_starts[1:], a_vals[1:]):
                sel = jnp.where(col >= start, v, sel)
            a_vals = [sel]
        if a_pro:
            a_vals = a_pro(a_vals, cvals)
            for r, v in zip(ao_refs, a_vals):
                r[...] = v
        if chunk:
            spans = [slice(lo, min(lo + chunk, bn)) for lo in range(0, bn, chunk)]
            chunks = []
            for cs in spans:
                parts = [None] * n_acc
                for ai, bi, ci in pairs:
                    p = lax.dot_general(a_vals[ai], b_refs[bi][cs, :] if tb else b_refs[bi][:, cs], dn,
                                        preferred_element_type=F32)
                    parts[ci] = p if parts[ci] is None else parts[ci] + p
                chunks.append(parts)
            for cs, parts in zip(spans, chunks):
                ex = [e[:, cs] for e in e_refs]
                outs = fin(parts, ex, cvals) if nc else fin(parts, ex)
                for o_ref, o in zip(o_refs, outs):
                    o_ref[:, cs] = o.astype(o_ref.dtype)
            return

        parts = [None] * n_acc
        for ai, bi, ci in ([] if (a_cat and not ta) else pairs):
            p = lax.dot_general(a_vals[ai], b_refs[bi][...], dn, preferred_element_type=F32)
            parts[ci] = p if parts[ci] is None else parts[ci] + p

        def finish(accs):
            ex = [e[...] for e in e_refs]
            res = fin(accs, ex, cvals) if nc else fin(accs, ex)
            outs, sums = res if n_sums else (res, ())
            for o_ref, o in zip(o_refs, outs):
                o_ref[...] = o.astype(o_ref.dtype)
            if n_sums:
                @pl.when(first)
                def _():
                    for s_ref, s in zip(s_refs, sums):
                        s_ref[...] = s

                @pl.when(jnp.logical_not(first))
                def _():
                    for s_ref, s in zip(s_refs, sums):
                        s_ref[...] += s

        if a_cat and not ta:
            @pl.when(kk == 0)
            def _():
                acc_refs[0][...] = jnp.zeros_like(acc_refs[0])

            for start, count, a_ref in zip(cat_starts, cat_counts, a_refs):
                @pl.when(jnp.logical_and(kk >= start, kk < start + count))
                def _(a_ref=a_ref):
                    acc_refs[0][...] += lax.dot_general(a_ref[...], b_refs[0][...], dn, preferred_element_type=F32)

            @pl.when(kk == nk - 1)
            def _():
                finish([acc_refs[0][...]])
        elif nk == 1:
            finish(parts)
        else:
            @pl.when(kk == 0)
            def _():
                for c in range(n_acc):
                    acc_refs[c][...] = parts[c]

            @pl.when(kk > 0)
            def _():
                for c in range(n_acc):
                    acc_refs[c][...] += parts[c]

            @pl.when(kk == nk - 1)
            def _():
                finish([acc_refs[c][...] for c in range(n_acc)])

    def ij(f):
        return (lambda j, i, q: f(i, j, q)) if n_outer else f

    a_spec = pl.BlockSpec((bk, bm), ij(lambda i, j, q: (q, i))) if ta else pl.BlockSpec((bm, bk), ij(lambda i, j, q: (i, q)))
    a_specs = [a_spec] * na
    if a_cat:
        def part_spec(start, count):
            def col(c):
                return jnp.clip(c - start, 0, count - 1)
            if ta:
                return pl.BlockSpec((bk, bm), ij(lambda i, j, q: (q, col(i))))
            return pl.BlockSpec((bm, bk), ij(lambda i, j, q: (i, col(q))))
        a_specs = [part_spec(s, c) for s, c in zip(cat_starts, cat_counts)]

    b_mode = dict(pipeline_mode=pl.Buffered(1)) if (bn == n and nk == 1) else {}

    def b_spec(off):
        on, ok = off
        if tb:
            return pl.BlockSpec((bn, bk), ij(lambda i, j, q: (j + on, q + ok)), **b_mode)
        return pl.BlockSpec((bk, bn), ij(lambda i, j, q: (q + ok, j + on)), **b_mode)

    mn_spec = pl.BlockSpec((bm, bn), ij(lambda i, j, q: (i, j)))
    outs = pl.pallas_call(
        body,
        out_shape=[jax.ShapeDtypeStruct((m, n), d) for d in out_dtypes] + [jax.ShapeDtypeStruct((m, k), BF16)] * nao
        + [jax.ShapeDtypeStruct((8, n), F32)] * n_sums,
        grid=(n // bn, m // bm, nk) if n_outer else (m // bm, n // bn, nk),
        in_specs=a_specs + [b_spec(o) for o in b_off]
        + [pl.BlockSpec((bm, bn), ij(lambda i, j, q, o=o: (i, j + o))) for o in e_off]
        + [pl.BlockSpec(c.shape, lambda *_, nd=c.ndim: (0,) * nd) for c in consts],
        out_specs=[mn_spec] * no + [a_spec] * nao + [pl.BlockSpec((8, n), lambda *_: (0, 0))] * n_sums,
        scratch_shapes=[pltpu.VMEM((bm, bn), F32) for _ in range(n_acc if nk > 1 else 0)],
        compiler_params=_cparams(("arbitrary" if n_sums else "parallel", "parallel", "arbitrary")),
        name=name,
    )(*a_list, *b_list, *extras, *consts)
    return outs


def _first(accs, ex):
    return (accs[0],)


def _rowwise(fn, ins, consts, out_defs, sum_widths, *, bm, name):
    ins = [tuple(e) + (1,) * (4 - len(e)) for e in ins]
    out_defs = [tuple(e) + (1,) * (3 - len(e)) for e in out_defs]
    t = ins[0][0].shape[-2] * ins[0][3]
    bm = min(bm, t)
    assert t % bm == 0, (name, t, bm)
    ni, nc, no, ns = len(ins), len(consts), len(out_defs), len(sum_widths)
    strided = [w for _, w, _, d in ins if d > 1] + [w for w, _, d in out_defs if d > 1]

    def body(*refs):
        i_refs, c_refs = refs[:ni], refs[ni:ni + nc]
        o_refs, s_refs = refs[ni + nc:ni + nc + no], refs[ni + nc + no:ni + nc + no + ns]
        scratch = list(refs[ni + nc + no + ns:])
        vals = []
        for ref, (_, w, _, d) in zip(i_refs, ins):
            if d == 1:
                vals.append(ref[...])
                continue
            s = scratch.pop(0)
            for r in range(d):
                for c in range(w // 128):
                    s.at[c][pl.ds(r, bm // d, stride=d), :] = ref[r, :, c * 128:(c + 1) * 128].astype(F32)
            vals.append(jnp.concatenate([s[c] for c in range(w // 128)], axis=1))
        outs, sums = fn(vals, [r[...] for r in c_refs])
        for o_ref, o, (w, _, d) in zip(o_refs, outs, out_defs):
            if d == 1:
                o_ref[...] = o.astype(o_ref.dtype)
                continue
            s = scratch.pop(0)
            for c in range(w // 128):
                s[c] = o[:, c * 128:(c + 1) * 128].astype(F32)
            for r in range(d):
                for c in range(w // 128):
                    o_ref[r, :, c * 128:(c + 1) * 128] = s.at[c][pl.ds(r, bm // d, stride=d), :].astype(o_ref.dtype)
        if ns:
            first = pl.program_id(0) == 0

            @pl.when(first)
            def _():
                for s_ref, s in zip(s_refs, sums):
                    s_ref[...] = s

            @pl.when(jnp.logical_not(first))
            def _():
                for s_ref, s in zip(s_refs, sums):
                    s_ref[...] += s

    def win(width, cb, d):
        if d > 1:
            return pl.BlockSpec((d, bm // d, width), lambda i: (0, i, 0))
        return pl.BlockSpec((bm, width), lambda i: (i, cb))

    res = pl.pallas_call(
        body,
        out_shape=[jax.ShapeDtypeStruct((t, w) if d == 1 else (d, t // d, w), dt) for w, dt, d in out_defs]
        + [jax.ShapeDtypeStruct((8, w), F32) for w in sum_widths],
        grid=(t // bm,),
        in_specs=[win(w, cb, d) for _, w, cb, d in ins] + [pl.BlockSpec(c.shape, lambda i, nd=c.ndim: (0,) * nd) for c in consts],
        out_specs=[win(w, 0, d) for w, _, d in out_defs] + [pl.BlockSpec((8, w), lambda i: (0, 0)) for w in sum_widths],
        scratch_shapes=[pltpu.VMEM((w // 128, bm, 128), F32) for w in strided],
        compiler_params=_cparams(("arbitrary",) if ns else ("parallel",)),
        name=name,
    )(*[e[0] for e in ins], *consts)
    return res[:no], [jnp.sum(s, axis=0) for s in res[no:]]


def _colsum8(x):
    bm, w = x.shape
    return jnp.sum(x.reshape(bm // 8, 8, w), axis=0)


def _tri(n, upper=False):
    r = lax.broadcasted_iota(jnp.int32, (n, n), 0)
    c = lax.broadcasted_iota(jnp.int32, (n, n), 1)
    return (c >= r) if upper else (c <= r)


def _exact_tri_matmul(tri_bf16, x):
    x0 = x.astype(BF16)
    r1 = x - x0.astype(F32)
    x1 = r1.astype(BF16)
    x2 = (r1 - x1.astype(F32)).astype(BF16)
    w = x.shape[1]
    y = jnp.dot(tri_bf16, jnp.concatenate([x0, x1, x2], axis=1), preferred_element_type=F32)
    return y[:, :w] + y[:, w:2 * w] + y[:, 2 * w:]


def _dot_nt(a, b):
    return lax.dot_general(a, b, (((1,), (1,)), ((), ())), preferred_element_type=F32)


def _dot_tn(a, b):
    return lax.dot_general(a, b, (((0,), (0,)), ((), ())), preferred_element_type=F32)


def _dot(a, b):
    return jnp.dot(a, b, preferred_element_type=F32)


def _hg_gates(hq, hf, lb):
    sq = _sigmoid(hq)
    q = hq * sq
    sg = _sigmoid(hf)
    f = lb + (1.0 - lb) * sg
    return q, sq, sg, f


def _hg_heads(x, hp):
    return [x[:, h * HG_DK:(h + 1) * HG_DK] for h in range(hp)]


def _hg_intra_wide(q, kk, g, hp):
    c = q.shape[0]
    rows = lax.broadcasted_iota(jnp.int32, (c, 1), 0)
    a_rows = [[] for _ in range(hp)]
    qts, kts, eqs, eks = [], [], [], []
    for i in range(c // HG_SUB):
        lo = i * HG_SUB
        ref = g[lo - 1:lo, :] if i else jnp.zeros_like(g[0:1, :])
        eq = jnp.exp(g[lo:lo + HG_SUB, :] - ref)
        ek = jnp.exp(jnp.where(rows < lo + HG_SUB, ref - g, 0.0))
        qtb = (q[lo:lo + HG_SUB, :] * eq).astype(BF16)
        ktb = (kk * ek).astype(BF16)
        tpos = lo + lax.broadcasted_iota(jnp.int32, (HG_SUB, c), 0)
        spos = lax.broadcasted_iota(jnp.int32, (HG_SUB, c), 1)
        for h, (qh, kh) in enumerate(zip(_hg_heads(qtb, hp), _hg_heads(ktb, hp))):
            a_rows[h].append(jnp.where(spos <= tpos, _dot_nt(qh, kh), 0.0))
        qts.append(qtb), kts.append(ktb), eqs.append(eq), eks.append(ek)
    return [jnp.concatenate(r, axis=0) for r in a_rows], qts, kts, eqs, eks


def _hgrn_fwd(zh, lb3, *, tb=512):
    t = zh.shape[0]
    nh = lb3.shape[0]
    c = HG_CHUNK
    tb = min(tb, t)
    nchunk = tb // c
    hp = HG_HP if nh % HG_HP == 0 else 1
    wp = hp * HG_DK

    def body(hq_ref, hf_ref, hi_ref, lb_ref, o_ref, st_ref, state):
        @pl.when(pl.program_id(1) == 0)
        def _():
            state[...] = jnp.zeros_like(state)

        tril = _tri(c).astype(BF16)

        def chunk(ci, carry):
            sl = pl.ds(pl.multiple_of(ci * c, c), c)
            q, _, _, f = _hg_gates(hq_ref[sl, :], hf_ref[sl, :], lb_ref[...])
            kk = 1.0 - f
            g = _exact_tri_matmul(tril, jnp.log(f))
            a, _, _, _, _ = _hg_intra_wide(q, kk, g, hp)
            vb = hi_ref[sl, :].astype(BF16)
            glast = g[c - 1:c, :]
            qgb = (q * jnp.exp(g)).astype(BF16)
            kgb = (kk * jnp.exp(glast - g)).astype(BF16)
            dec = jnp.exp(glast)
            sts = [state[h] for h in range(hp)]
            for h in range(hp):
                st_ref[h, ci] = sts[h]
            vh, qgh, kgh, dech = _hg_heads(vb, hp), _hg_heads(qgb, hp), _hg_heads(kgb, hp), _hg_heads(dec, hp)
            o = [_dot(a[h].astype(BF16), vh[h]) + _dot_nt(qgh[h], sts[h].astype(BF16)) for h in range(hp)]
            new = [_dot_tn(vh[h], kgh[h]) for h in range(hp)]
            o_ref[sl, :] = jnp.concatenate(o, axis=1)
            for h in range(hp):
                state[h] = sts[h] * dech[h] + new[h]
            return carry

        lax.fori_loop(0, nchunk, chunk, 0)

    def col(cb):
        return pl.BlockSpec((tb, wp), lambda h, i: (i, cb * (nh // hp) + h))

    return pl.pallas_call(
        body,
        out_shape=[jax.ShapeDtypeStruct((t, nh * HG_DK), F32), jax.ShapeDtypeStruct((nh, t // c, HG_DK, HG_DK), F32)],
        grid=(nh // hp, t // tb),
        in_specs=[col(0), col(1), col(2), pl.BlockSpec((1, wp), lambda h, i: (0, h))],
        out_specs=[pl.BlockSpec((tb, wp), lambda h, i: (i, h)),
                   pl.BlockSpec((hp, nchunk, HG_DK, HG_DK), lambda h, i: (h, i, 0, 0))],
        scratch_shapes=[pltpu.VMEM((hp, HG_DK, HG_DK), F32)],
        compiler_params=_cparams(("parallel", "arbitrary")),
        name="hgrn_fwd",
    )(zh, zh, zh, lb3.reshape(1, -1))


def _hgrn_bwd(zh, lb3, states, d_o, *, tb=512):
    t = zh.shape[0]
    nh = lb3.shape[0]
    c = HG_CHUNK
    tb = min(tb, t)
    nchunk = tb // c
    nblk = t // tb
    hp = HG_HP if nh % HG_HP == 0 else 1
    wp = hp * HG_DK

    def body(hq_ref, hf_ref, hi_ref, lb_ref, st_ref, do_ref, dq_ref, df_ref, dv_ref, dlb_ref, dstate):
        @pl.when(pl.program_id(1) == 0)
        def _():
            dstate[...] = jnp.zeros_like(dstate)
            dlb_ref[...] = jnp.zeros_like(dlb_ref)

        tril = _tri(c).astype(BF16)
        triu = _tri(c, upper=True).astype(BF16)
        last_row = lax.broadcasted_iota(jnp.int32, (c, 1), 0) == c - 1
        heads = range(hp)

        def chunk(j, carry):
            ci = nchunk - 1 - j
            sl = pl.ds(pl.multiple_of(ci * c, c), c)
            lb = lb_ref[...]
            hq, hf = hq_ref[sl, :], hf_ref[sl, :]
            q, sq, sg, f = _hg_gates(hq, hf, lb)
            kk = 1.0 - f
            g = _exact_tri_matmul(tril, jnp.log(f))
            a, qts, kts, eqs, eks = _hg_intra_wide(q, kk, g, hp)
            glast = g[c - 1:c, :]
            eg, egl, dec = jnp.exp(g), jnp.exp(glast - g), jnp.exp(glast)
            vb, dob = hi_ref[sl, :].astype(BF16), do_ref[sl, :].astype(BF16)
            qgb, kgb = (q * eg).astype(BF16), (kk * egl).astype(BF16)
            sts = [st_ref[h, ci] for h in heads]
            dsts = [dstate[h] for h in heads]
            stb, dstb = [s.astype(BF16) for s in sts], [s.astype(BF16) for s in dsts]
            vh, doh, qgh, kgh = _hg_heads(vb, hp), _hg_heads(dob, hp), _hg_heads(qgb, hp), _hg_heads(kgb, hp)
            dv = [_dot_tn(a[h].astype(BF16), doh[h]) + _dot_nt(kgh[h], dstb[h]) for h in heads]
            da = [jnp.where(_tri(c), _dot_nt(doh[h], vh[h]), 0.0).astype(BF16) for h in heads]
            dq_inter = jnp.concatenate([_dot(doh[h], stb[h]) for h in heads], axis=1) * eg
            dk_state = jnp.concatenate([_dot(vh[h], dstb[h]) for h in heads], axis=1) * egl
            new_dst = [_dot_tn(doh[h], qgh[h]) for h in heads]
            xs, dk, dgk = [], dk_state, 0.0
            for i in range(c // HG_SUB):
                rs = slice(i * HG_SUB, (i + 1) * HG_SUB)
                kth, qth = _hg_heads(kts[i], hp), _hg_heads(qts[i], hp)
                xi = jnp.concatenate([_dot(da[h][rs, :], kth[h]) for h in heads], axis=1)
                yi = jnp.concatenate([_dot_tn(da[h][rs, :], qth[h]) for h in heads], axis=1)
                xs.append(xi)
                dk = dk + yi * eks[i]
                dgk = dgk + yi * kts[i].astype(F32)
            dq = jnp.concatenate([x * e for x, e in zip(xs, eqs)], axis=0) + dq_inter
            dgq = jnp.concatenate([x * qt.astype(F32) for x, qt in zip(xs, qts)], axis=0)
            dg = dgq - dgk + q * dq_inter - kk * dk_state
            sdot = jnp.concatenate([jnp.sum(sts[h] * dsts[h], axis=0, keepdims=True) for h in heads], axis=1)
            dgl = jnp.sum(kk * dk_state, axis=0, keepdims=True) + dec * sdot
            dg = dg + jnp.where(last_row, dgl, 0.0)
            dlogf = _exact_tri_matmul(triu, dg)
            dfv = dlogf / f - dk
            dq_ref[sl, :] = (dq * (sq * (1.0 + hq * (1.0 - sq)))).astype(dq_ref.dtype)
            df_ref[sl, :] = (dfv * (1.0 - lb) * sg * (1.0 - sg)).astype(df_ref.dtype)
            dv_ref[sl, :] = jnp.concatenate(dv, axis=1).astype(dv_ref.dtype)
            dlb_ref[...] += jnp.sum(dfv * (1.0 - sg), axis=0, keepdims=True)
            dech = _hg_heads(dec, hp)
            for h in heads:
                dstate[h] = dsts[h] * dech[h] + new_dst[h]
            return carry

        lax.fori_loop(0, nchunk, chunk, 0)

    def col(cb):
        return pl.BlockSpec((tb, wp), lambda h, i: (nblk - 1 - i, cb * (nh // hp) + h))

    ocol = pl.BlockSpec((tb, wp), lambda h, i: (nblk - 1 - i, h))
    lbspec = pl.BlockSpec((1, wp), lambda h, i: (0, h))
    w = nh * HG_DK
    dq, df, dv, dlb = pl.pallas_call(
        body,
        out_shape=[jax.ShapeDtypeStruct((t, w), BF16)] * 3 + [jax.ShapeDtypeStruct((1, w), F32)],
        grid=(nh // hp, nblk),
        in_specs=[col(0), col(1), col(2), lbspec,
                  pl.BlockSpec((hp, nchunk, HG_DK, HG_DK), lambda h, i: (h, nblk - 1 - i, 0, 0)), ocol],
        out_specs=[ocol, ocol, ocol, lbspec],
        scratch_shapes=[pltpu.VMEM((hp, HG_DK, HG_DK), F32)],
        compiler_params=_cparams(("parallel", "arbitrary")),
        name="hgrn_bwd",
    )(zh, zh, zh, lb3.reshape(1, -1), states, d_o)
    return dq, df, dv, dlb.reshape(w)


NEG = -1e30
ATT_GW = ATT_HEADS * ATT_DH


def _att_scores(q, kp, kc, has_prev):
    scale = ATT_DH ** -0.5
    i = lax.broadcasted_iota(jnp.int32, (ATT_BLK, ATT_BLK), 0)
    j = lax.broadcasted_iota(jnp.int32, (ATT_BLK, ATT_BLK), 1)
    s_p = jnp.where(jnp.logical_and(j >= i, has_prev), _dot_nt(q, kp) * scale, NEG)
    s_c = jnp.where(j <= i, _dot_nt(q, kc) * scale, NEG)
    return s_p, s_c


def _att_views(arrs, d):
    return [a.reshape(d, -1, ATT_GW) for a in arrs]


def _att_unview(a, d):
    return a.reshape(-1, ATT_GW) if d == 1 else a


ATT_QB = 4


def _attn_fwd(qb, kb, vb, g):
    d = ATT_PATTERNS[g][1]
    q2, k2, v2 = _att_views([qb, kb, vb], d)
    nblk = q2.shape[1] // ATT_BLK
    nq = ATT_QB if nblk % ATT_QB == 0 else 1
    rows = nq * ATT_BLK

    def body(q_ref, kc_ref, kp_ref, vc_ref, vp_ref, o_ref, l_ref):
        first = pl.program_id(1) == 0
        hss = [slice(h * ATT_DH, (h + 1) * ATT_DH) for h in range(ATT_HEADS)]
        for b in range(nq):
            rs = slice(b * ATT_BLK, (b + 1) * ATT_BLK)
            ps = slice((b - 1) * ATT_BLK, b * ATT_BLK)
            has_prev = jnp.logical_not(first) if b == 0 else True
            kv = [(kp_ref[:, hs], vp_ref[:, hs]) if b == 0 else (kc_ref[ps, hs], vc_ref[ps, hs]) for hs in hss]
            sc = [_att_scores(q_ref[rs, hs], kv[h][0], kc_ref[rs, hs], has_prev) for h, hs in enumerate(hss)]
            ms = [jnp.maximum(jnp.max(s_p, axis=1, keepdims=True), jnp.max(s_c, axis=1, keepdims=True)) for s_p, s_c in sc]
            ps_ = [(jnp.exp(s_p - m), jnp.exp(s_c - m)) for (s_p, s_c), m in zip(sc, ms)]
            ls = [jnp.sum(p_p, axis=1, keepdims=True) + jnp.sum(p_c, axis=1, keepdims=True) for p_p, p_c in ps_]
            os_ = [_dot(p_p.astype(BF16), kv[h][1]) + _dot(p_c.astype(BF16), vc_ref[rs, hss[h]]) for h, (p_p, p_c) in enumerate(ps_)]
            for h, hs in enumerate(hss):
                o_ref[rs, hs] = os_[h] / ls[h]
                l_ref[rs, hs] = jnp.broadcast_to(ms[h] + jnp.log(ls[h]), (ATT_BLK, ATT_DH))

    cur = pl.BlockSpec((None, rows, ATT_GW), lambda r, n: (r, n, 0))
    prev = pl.BlockSpec((None, ATT_BLK, ATT_GW), lambda r, n: (r, jnp.maximum(n * nq - 1, 0), 0))
    o, lse = pl.pallas_call(
        body,
        out_shape=[jax.ShapeDtypeStruct(q2.shape, F32)] * 2,
        grid=(d, nblk // nq),
        in_specs=[cur, cur, prev, cur, prev],
        out_specs=[cur, cur],
        compiler_params=_cparams(("parallel", "arbitrary")),
        name=f"attn_fwd_g{g}",
    )(q2, k2, k2, v2, v2)
    return _att_unview(o, d), _att_unview(lse, d)


def _attn_bwd(qb, kb, vb, o, lse, d_o, d_lse, g):
    d = ATT_PATTERNS[g][1]
    q2, k2, v2 = _att_views([qb, kb, vb], d)
    o2, l2, do2, dl2 = _att_views([o, lse, d_o, d_lse], d)
    nblk = q2.shape[1] // ATT_BLK
    nq = ATT_QB if nblk % ATT_QB == 0 else 1
    rows = nq * ATT_BLK
    ns = nblk // nq
    scale = ATT_DH ** -0.5

    def body(q_ref, kc_ref, kp_ref, vc_ref, vp_ref, o_ref, l_ref, do_ref, dl_ref, dq_ref, dk_ref, dv_ref, ck, cv):
        n = pl.program_id(1)

        @pl.when(n == 0)
        def _():
            ck[...] = jnp.zeros_like(ck)
            cv[...] = jnp.zeros_like(cv)

        first = n == ns - 1
        hss = [slice(h * ATT_DH, (h + 1) * ATT_DH) for h in range(ATT_HEADS)]
        heads = range(ATT_HEADS)
        pend_k, pend_v = [ck[:, hs] for hs in hss], [cv[:, hs] for hs in hss]
        for b in reversed(range(nq)):
            rs = slice(b * ATT_BLK, (b + 1) * ATT_BLK)
            ps = slice((b - 1) * ATT_BLK, b * ATT_BLK)
            has_prev = jnp.logical_not(first) if b == 0 else True
            q = [q_ref[rs, hs] for hs in hss]
            kc, vc = [kc_ref[rs, hs] for hs in hss], [vc_ref[rs, hs] for hs in hss]
            kp = [kp_ref[:, hs] if b == 0 else kc_ref[ps, hs] for hs in hss]
            vp = [vp_ref[:, hs] if b == 0 else vc_ref[ps, hs] for hs in hss]
            sc = [_att_scores(q[h], kp[h], kc[h], has_prev) for h in heads]
            dob = [do_ref[rs, hs].astype(BF16) for hs in hss]
            dp = [(_dot_nt(dob[h], vp[h]), _dot_nt(dob[h], vc[h])) for h in heads]
            delta = [jnp.sum(do_ref[rs, hs] * o_ref[rs, hs] - dl_ref[rs, hs], axis=1, keepdims=True) for hs in hss]
            pr = [(jnp.exp(sc[h][0] - l_ref[rs, hss[h]][:, 0:1]), jnp.exp(sc[h][1] - l_ref[rs, hss[h]][:, 0:1])) for h in heads]
            ds = [((pr[h][0] * (dp[h][0] - delta[h]) * scale).astype(BF16), (pr[h][1] * (dp[h][1] - delta[h]) * scale).astype(BF16))
                  for h in heads]
            pb = [(pr[h][0].astype(BF16), pr[h][1].astype(BF16)) for h in heads]
            dq = [_dot(ds[h][0], kp[h]) + _dot(ds[h][1], kc[h]) for h in heads]
            dk_c = [_dot_tn(ds[h][1], q[h]) for h in heads]
            dv_c = [_dot_tn(pb[h][1], dob[h]) for h in heads]
            dk_p = [_dot_tn(ds[h][0], q[h]) for h in heads]
            dv_p = [_dot_tn(pb[h][0], dob[h]) for h in heads]
            for h, hs in enumerate(hss):
                dq_ref[rs, hs] = dq[h]
                dk_ref[rs, hs] = pend_k[h] + dk_c[h]
                dv_ref[rs, hs] = pend_v[h] + dv_c[h]
            pend_k, pend_v = dk_p, dv_p
        for h, hs in enumerate(hss):
            ck[:, hs] = pend_k[h]
            cv[:, hs] = pend_v[h]

    cur = pl.BlockSpec((None, rows, ATT_GW), lambda r, n: (r, ns - 1 - n, 0))
    prev = pl.BlockSpec((None, ATT_BLK, ATT_GW), lambda r, n: (r, jnp.maximum((ns - 1 - n) * nq - 1, 0), 0))
    shp = jax.ShapeDtypeStruct(q2.shape, F32)
    dq, dk, dv = pl.pallas_call(
        body,
        out_shape=[shp, shp, shp],
        grid=(d, ns),
        in_specs=[cur, cur, prev, cur, prev, cur, cur, cur, cur],
        out_specs=[cur, cur, cur],
        scratch_shapes=[pltpu.VMEM((ATT_BLK, ATT_GW), F32), pltpu.VMEM((ATT_BLK, ATT_GW), F32)],
        compiler_params=_cparams(("parallel", "arbitrary")),
        name=f"attn_bwd_g{g}",
    )(q2, k2, k2, v2, v2, o2, l2, do2, dl2)
    return _att_unview(dq, d), _att_unview(dk, d), _att_unview(dv, d)


def _rms_parts(x, width):
    outs = []
    for lo in range(0, x.shape[1], width):
        xs = x[:, lo:lo + width].astype(F32)
        r = lax.rsqrt(jnp.mean(xs * xs, axis=1, keepdims=True) + EPS)
        outs.append((xs * r, r))
    return outs


def _rms_bwd_part(xh, r, dxh):
    return r * (dxh - xh * jnp.mean(dxh * xh, axis=1, keepdims=True))


def _norm_pro(a, consts):
    (xh, _), = _rms_parts(a[0], a[0].shape[1])
    return [(xh * consts[0]).astype(BF16)]


def _norm_bwd_fin(accs, ex, consts):
    xv, dres = ex
    (xh, r), = _rms_parts(xv, xv.shape[1])
    dx = dres + _rms_bwd_part(xh, r, accs[0] * consts[0])
    return [dx, dx], [_colsum8(accs[0] * xh)]


def _rot_sign():
    lane = lax.broadcasted_iota(jnp.int32, (1, ATT_DH), 1)
    return jnp.where(lane < ATT_DH // 2, -1.0, 1.0).astype(F32)


def _rope(y, cos, sin):
    return y * cos + pltpu.roll(y, ATT_DH // 2, axis=1) * _rot_sign() * sin


def _rope_t(dy, cos, sin):
    return dy * cos - pltpu.roll(dy * sin, ATT_DH // 2, axis=1) * _rot_sign()


def _qk_prep(zq, zk, zv, qn, kn, cos, sin):
    w = zq.shape[1]

    def fn(ins, consts):
        cs, sn = ins[3], ins[4]
        outs = []
        for z, gain in ((ins[0], consts[0]), (ins[1], consts[1])):
            for i, (xh, _) in enumerate(_rms_parts(z, ATT_DH)):
                outs.append(_rope(xh * gain[:, i * ATT_DH:(i + 1) * ATT_DH], cs, sn))
        outs += [ins[2][:, i * ATT_DH:(i + 1) * ATT_DH] for i in range(w // ATT_DH)]
        groups = [jnp.concatenate(outs[i:i + ATT_HEADS], axis=1) for i in range(0, len(outs), ATT_HEADS)]
        return groups, []

    outs, _ = _rowwise(fn, [(zq, w, 0), (zk, w, 0), (zv, w, 0), (cos, ATT_DH, 0), (sin, ATT_DH, 0)], [qn, kn],
                       [(ATT_GW, BF16, ATT_PATTERNS[g][1]) for g in range(ATT_GROUPS)] * 3, [], bm=256, name="qk_prep")
    return outs[0:3], outs[3:6], outs[6:9]


def _qk_prep_bwd(zq, zk, dq_g, dk_g, dv_g, qn, kn, cos, sin):
    w = zq.shape[1]

    def fn(ins, consts):
        cs, sn = ins[2], ins[3]
        outs, sums = [], []
        for z, gain, dparts in ((ins[0], consts[0], ins[4:7]), (ins[1], consts[1], ins[7:10])):
            dout = jnp.concatenate(dparts, axis=1)
            dz, dgain = [], []
            for i, (xh, r) in enumerate(_rms_parts(z, ATT_DH)):
                hs = slice(i * ATT_DH, (i + 1) * ATT_DH)
                dy = _rope_t(dout[:, hs], cs, sn)
                dgain.append(_colsum8(dy * xh))
                dz.append(_rms_bwd_part(xh, r, dy * gain[:, hs]))
            outs.append(jnp.concatenate(dz, axis=1))
            sums.append(jnp.concatenate(dgain, axis=1))
        outs.append(jnp.concatenate(ins[10:13], axis=1))
        return outs, sums

    ins = [(zq, w, 0), (zk, w, 0), (cos, ATT_DH, 0), (sin, ATT_DH, 0)]
    for parts in (dq_g, dk_g, dv_g):
        ins += [(a, ATT_GW, 0, ATT_PATTERNS[g][1]) for g, a in enumerate(parts)]
    (dzq, dzk, dzv), (dqn, dkn) = _rowwise(fn, ins, [qn, kn], [(w, BF16)] * 3, [w, w], bm=256, name="qk_prep_bwd")
    return dzq, dzk, dzv, dqn, dkn


def _post_a(o_raw, zh, gout):
    w = o_raw.shape[1]

    def fn(ins, consts):
        oh = jnp.concatenate([xh for xh, _ in _rms_parts(ins[0], HG_DK)], axis=1)
        hg = ins[1]
        return [oh * consts[0] * (hg * _sigmoid(hg))], []

    (y,), _ = _rowwise(fn, [(o_raw, w, 0), (zh, w, 3)], [gout.reshape(1, w)], [(w, BF16)], [], bm=512, name="post_a")
    return y


def _post_a_bwd(o_raw, zh, gout, dy):
    w = o_raw.shape[1]

    def fn(ins, consts):
        parts = _rms_parts(ins[0], HG_DK)
        oh = jnp.concatenate([xh for xh, _ in parts], axis=1)
        hg, dyv, gain = ins[1], ins[2], consts[0]
        sg = _sigmoid(hg)
        s = hg * sg
        doh = dyv * gain * s
        do = jnp.concatenate([_rms_bwd_part(xh, r, doh[:, i * HG_DK:(i + 1) * HG_DK]) for i, (xh, r) in enumerate(parts)], axis=1)
        dhg = dyv * oh * gain * (sg * (1.0 + hg * (1.0 - sg)))
        return [do, dhg], [_colsum8(dyv * oh * s)]

    (do, dhg), (dgain,) = _rowwise(fn, [(o_raw, w, 0), (zh, w, 3), (dy, w, 0)], [gout.reshape(1, w)],
                                   [(w, F32), (w, BF16)], [w], bm=512, name="post_a_bwd")
    return do, dhg, dgain


def _merge_alpha(lses):
    m = jnp.maximum(jnp.maximum(lses[0], lses[1]), lses[2])
    e = [jnp.exp(l - m) for l in lses]
    inv = 1.0 / (e[0] + e[1] + e[2])
    return [x * inv for x in e]


def _group_ins(parts):
    return [(a, ATT_GW, 0, ATT_PATTERNS[g][1]) for g, a in enumerate(parts)]


def _merge_b(o_g, lse_g):
    def fn(ins, consts):
        al = _merge_alpha(ins[3:6])
        return [al[0] * ins[0] + al[1] * ins[1] + al[2] * ins[2]], []

    (y,), _ = _rowwise(fn, _group_ins(o_g) + _group_ins(lse_g), [], [(ATT_GW, BF16)], [], bm=512, name="merge_b")
    return y


def _merge_b_bwd(o_g, lse_g, dy):
    def fn(ins, consts):
        al = _merge_alpha(ins[3:6])
        dyv = ins[6]
        dal = [dyv * ins[i] for i in range(3)]
        tot = al[0] * dal[0] + al[1] * dal[1] + al[2] * dal[2]
        return [al[i] * dyv for i in range(3)] + [al[i] * (dal[i] - tot) for i in range(3)], []

    outs, _ = _rowwise(fn, _group_ins(o_g) + _group_ins(lse_g) + [(dy, ATT_GW, 0)], [],
                       [(ATT_GW, F32, ATT_PATTERNS[g][1]) for g in range(ATT_GROUPS)] * 2, [], bm=512, name="merge_b_bwd")
    return outs[:3], outs[3:]


def _loss_head(y, target):
    d = y.shape[1]

    def fn(ins, consts):
        e = ins[0] - ins[1]
        return [e * (1.0 / d)] * 2, [_colsum8(e * e)]

    (dy, dyb), (sq,) = _rowwise(fn, [(y, d, 0), (target, d, 0)], [], [(d, F32), (d, BF16)], [d], bm=512, name="loss_head")
    return 0.5 * jnp.sum(sq) / d, dy, dyb


def _ffn_fwd(x, gain, wt, wo_fn, tag):
    t, d = x.shape
    f = wt.shape[0] // 2

    def act(accs, ex, consts):
        a, b = accs
        s = _sigmoid(a)
        sa = a * s
        return (sa * b, b, 0.5 * sa, 0.5 * (s + sa * (1.0 - s)))

    bn = FFN_BN if f % FFN_BN == 0 else 256
    u, b, sa, sp, h = _mm([x], [wt, wt], [(0, 0, 0), (0, 1, 1)], 2, act, [BF16] * 4, m=t, n=f, k=d, tb=True,
                          bm=512, bn=bn, bk=d, b_off=[(0, 0), (f // min(bn, f), 0)],
                          consts=[gain.reshape(1, d)], a_pro=_norm_pro, chunk=MXU_COLS, name=f"ffn_in_{tag}")
    wo = wo_fn(u)
    (y,) = _mm([u], [wo], [(0, 0, 0)], 1, lambda accs, ex: (ex[0] + 0.5 * accs[0],), [F32], m=t, n=d, k=f,
               bm=512, bn=d, bk=f, extras=[x], name=f"ffn_out_{tag}")
    return y, (x, h, u, b, sa, sp, wo)


def _ffn_bwd(dy, dyb, saved, gain, wt, tag, tok, emit):
    x, h, u, b, sa, sp, wo = saved
    t, d = x.shape
    f = wo.shape[0]

    def dact(accs, ex, consts):
        bv, sav, spv = (e.astype(F32) for e in ex)
        return (accs[0] * bv * spv, accs[0] * sav)

    bn = FFN_BN if f % FFN_BN == 0 else 256
    da, db = _mm([dyb], [wo], [(0, 0, 0)], 1, dact, [BF16, BF16], m=t, n=f, k=d, tb=True, bm=512, bn=bn, bk=d,
                 extras=[b, sa, sp], n_outer=True, chunk=MXU_COLS, consts=[tok], name=f"ffn_dact_{tag}")
    (dwo,) = _mm([u], [dyb], [(0, 0, 0)], 1, lambda accs, ex: (0.5 * accs[0],), [BF16], m=f, n=d, k=t, ta=True,
                 bm=1408, bn=d, bk=1024, name=f"ffn_dwo_{tag}")
    (dwt,) = _mm([da, db], [h], [(0, 0, 0)], 1, _first, [BF16], m=2 * f, n=d, k=t, ta=True, bm=min(1408, f), bn=d,
                 bk=1024, a_cat=True, name=f"ffn_dwt_{tag}")
    tok = emit(dwt, dwo)
    bk = min(FFN_BN, f)
    dx, dxb, dgain = _mm([da, db], [wt, wt], [(0, 0, 0), (1, 1, 0)], 1, _norm_bwd_fin, [F32, BF16], m=t, n=d, k=f,
                         bm=512, bn=d, bk=bk, b_off=[(0, 0), (0, f // bk)], extras=[x, dy],
                         consts=[gain.reshape(1, d), tok], n_sums=1, name=f"ffn_dh_{tag}")
    return dx, dxb, jnp.sum(dgain, axis=0), tok


FFN_BN = 2816
Z_SPLITS = (("h", 4096), ("q", 1536), ("k", 1536), ("v", 1536), ("g", 2048))


def _mix_fwd(x, p, cos, sin):
    t, d = x.shape
    z, off, hm = {}, 0, None
    for nm, width in Z_SPLITS:
        bn = 1024 if off % 1024 == 0 and width % 1024 == 0 else 512
        first = hm is None
        res = _mm([x if first else hm], [p["wint"]], [(0, 0, 0)], 1, (lambda accs, ex, consts: (accs[0],)) if first else _first,
                  [F32 if nm == "h" else BF16], m=t, n=width, k=d, tb=True, bm=1024, bn=bn, bk=d, b_off=[(off // bn, 0)],
                  consts=[p["gm"].reshape(1, d)] if first else (), a_pro=_norm_pro if first else None, name=f"mix_in_{nm}")
        z[nm] = res[0]
        hm = res[1] if first else hm
        off += width
    o_raw, states = _hgrn_fwd(z["h"], p["lb3"])
    qb, kb, vb = _qk_prep(z["q"], z["k"], z["v"], p["qn"], p["kn"], cos, sin)
    o_g, lse_g = zip(*[_attn_fwd(qb[g], kb[g], vb[g], g) for g in range(ATT_GROUPS)])
    oa = _post_a(o_raw, z["h"], p["gout"])
    ob = _merge_b(o_g, lse_g)
    late = p["late"](ob)
    p = dict(p, **late)
    (ya,) = _mm([oa], [p["wa"]], [(0, 0, 0)], 1, _first, [F32], m=t, n=d, k=oa.shape[1], bm=1024, bn=d, bk=oa.shape[1],
                name="branch_a")

    def gate(accs, ex):
        return (_sigmoid(ex[0].astype(F32)) * ex[2] + _sigmoid(ex[1].astype(F32)) * accs[0], accs[0])

    merged, yb = _mm([ob], [p["wbt"]], [(0, 0, 0)], 1, gate, [BF16, F32], m=t, n=d, k=ATT_GW, tb=True, bm=512, bn=d,
                     bk=ATT_GW, extras=[z["g"], z["g"], ya], e_off=[0, 1, 0], chunk=MXU_COLS, name="branch_b_gate")
    (y,) = _mm([merged], [p["wo"]], [(0, 0, 0)], 1, lambda accs, ex: (ex[0] + accs[0],), [F32], m=t, n=d, k=d,
               bm=1024, bn=d, bk=d, extras=[x], name="mix_out")
    return y, (x, hm, z, o_raw, states, qb, kb, vb, o_g, lse_g, oa, ob, ya, yb, merged, late)


def _mix_bwd(dy, dyb, saved, p, cos, sin, tok):
    x, hm, z, o_raw, states, qb, kb, vb, o_g, lse_g, oa, ob, ya, yb, merged, late = saved
    p = dict(p, **late)
    t, d = x.shape
    w = oa.shape[1]

    def dgate(accs, ex, consts):
        dm = accs[0]
        sa, sb = _sigmoid(ex[0].astype(F32)), _sigmoid(ex[1].astype(F32))
        return (sa * dm, sb * dm, dm * ex[2] * sa * (1.0 - sa), dm * ex[3] * sb * (1.0 - sb))

    dya, dyb_, dga, dgb = _mm([dyb], [p["wo"]], [(0, 0, 0)], 1, dgate, [BF16] * 4, m=t, n=d, k=d, tb=True, bm=512, bn=d,
                              bk=d, extras=[z["g"], z["g"], ya, yb], e_off=[0, 1, 0, 0], chunk=MXU_COLS, consts=[tok], name="mix_out_bwd")
    (dwo,) = _mm([merged], [dyb], [(0, 0, 0)], 1, _first, [BF16], m=d, n=d, k=t, ta=True, bm=d, bn=d, bk=1024, name="mix_dwo")
    (doa,) = _mm([dya], [p["wa"]], [(0, 0, 0)], 1, _first, [F32], m=t, n=w, k=d, tb=True, bm=1024, bn=w, bk=d, name="branch_a_bwd")
    (dwa,) = _mm([oa], [dya], [(0, 0, 0)], 1, _first, [BF16], m=w, n=d, k=t, ta=True, bm=w, bn=d, bk=1024, name="branch_a_dw")
    (dob,) = _mm([dyb_], [p["wbt"]], [(0, 0, 0)], 1, _first, [F32], m=t, n=ATT_GW, k=d, bm=1024, bn=ATT_GW, bk=d,
                 name="branch_b_bwd")
    (dwbt,) = _mm([dyb_], [ob], [(0, 0, 0)], 1, _first, [BF16], m=d, n=ATT_GW, k=t, ta=True, bm=d, bn=ATT_GW, bk=1024,
                  name="branch_b_dw")
    do_raw, dhg, dgout = _post_a_bwd(o_raw, z["h"], p["gout"], doa)
    do_g, dlse_g = _merge_b_bwd(o_g, lse_g, dob)
    dq_g, dk_g, dv_g = zip(*[_attn_bwd(qb[g], kb[g], vb[g], o_g[g], lse_g[g], do_g[g], dlse_g[g], g)
                             for g in range(ATT_GROUPS)])
    dzq, dzk, dzv, dqn, dkn = _qk_prep_bwd(z["q"], z["k"], dq_g, dk_g, dv_g, p["qn"], p["kn"], cos, sin)
    dhq, dhf, dhi, lbsum = _hgrn_bwd(z["h"], p["lb3"], states, do_raw)
    dz = [dhq, dhf, dhi, dhg, dzq, dzk, dzv, dga, dgb]
    pw = sum(a.shape[1] for a in dz)
    dwint = jnp.concatenate(
        [_mm([a], [hm], [(0, 0, 0)], 1, _first, [BF16], m=a.shape[1], n=d, k=t, ta=True, bm=a.shape[1], bn=d, bk=1024,
             name=f"mix_in_dw{i}")[0] for i, a in enumerate(dz)], axis=0)
    dx, dxb, dgm = _mm(dz, [p["wint"]], [(0, 0, 0)], 1, _norm_bwd_fin, [F32, BF16], m=t, n=d, k=pw, bm=512, bn=d, bk=512,
                       extras=[x, dy], consts=[p["gm"].reshape(1, d)], n_sums=1, a_cat=True, name="mix_in_bwd")
    return dx, dxb, dict(gm=jnp.sum(dgm, axis=0), wint=dwint, lbsum=lbsum, gout=dgout, qn=dqn, kn=dkn, wa=dwa, wbt=dwbt, wo=dwo)


def _rope_tables(t):
    pos = jnp.arange(t, dtype=F32)
    inv = ROPE_THETA ** (-jnp.arange(0, ATT_DH, 2, dtype=F32) / ATT_DH)
    ang = pos[:, None] * inv[None, :]
    ang = jnp.concatenate([ang, ang], axis=-1)
    return jnp.cos(ang), jnp.sin(ang)


def _lower_bounds(logits):
    lb = jnp.cumsum(jax.nn.softmax(logits, axis=0), axis=0)
    return lb - lb[0:1]


def _head_gain(g):
    return jnp.tile(g[:, None, :], (1, ATT_HEADS, 1)).reshape(1, ATT_GROUPS * ATT_GW)


SMALL_GRADS = ("ffn1_norm", "mix_norm", "lbsum", "hgrn_out_norm", "attn_q_norm", "attn_k_norm", "ffn2_norm")


def _local_step(x, target, small, fetch, emit):
    t = x.shape[0]
    depth = small["ffn1_norm"].shape[0]
    cos, sin = _rope_tables(t)
    lb_all = _lower_bounds(small["hgrn_lb_logits"])
    saved = []
    for l in range(depth):
        w1t = fetch("w1t", l, x)["w1t"]
        x, s1 = _ffn_fwd(x, small["ffn1_norm"][l], w1t, lambda after, l=l: fetch("w1o", l, after)["w1o"], "1")
        p = dict(gm=small["mix_norm"][l], wint=fetch("wint", l, x)["wint"], lb3=lb_all[l].reshape(-1, 1, HG_DK),
                 gout=small["hgrn_out_norm"][l], qn=_head_gain(small["attn_q_norm"][l]),
                 kn=_head_gain(small["attn_k_norm"][l]), late=functools.partial(fetch, "mout", l))
        x, sm = _mix_fwd(x, p, cos, sin)
        w2t = fetch("w2t", l, x)["w2t"]
        x, s2 = _ffn_fwd(x, small["ffn2_norm"][l], w2t, lambda after, l=l: fetch("w2o", l, after)["w2o"], "2")
        saved.append((p, w1t, w2t, s1, sm, s2))
    loss, dx, dxb = _loss_head(x, target)
    gsmall = {k: [None] * depth for k in SMALL_GRADS}
    tok = jnp.zeros((8, 128), F32)
    for l in reversed(range(depth)):
        p, w1t, w2t, s1, sm, s2 = saved[l]
        dx, dxb, gsmall["ffn2_norm"][l], tok = _ffn_bwd(
            dx, dxb, s2, small["ffn2_norm"][l], w2t, "2", tok, lambda dwt, dwo, l=l: emit("ffn2", l, dict(w2t=dwt, w2o=dwo), None))
        dx, dxb, gm = _mix_bwd(dx, dxb, sm, p, cos, sin, tok)
        tok = emit("mix", l, {k: gm[k] for k in ("wint", "wa", "wbt", "wo")}, None)
        gsmall["mix_norm"][l], gsmall["lbsum"][l], gsmall["hgrn_out_norm"][l] = gm["gm"], gm["lbsum"], gm["gout"]
        for k, src in (("attn_q_norm", "qn"), ("attn_k_norm", "kn")):
            gsmall[k][l] = jnp.sum(gm[src].reshape(ATT_GROUPS, ATT_HEADS, ATT_DH), axis=1)
        dx, dxb, gsmall["ffn1_norm"][l], tok = _ffn_bwd(
            dx, dxb, s1, small["ffn1_norm"][l], w1t, "1", tok, lambda dwt, dwo, l=l: emit("ffn1", l, dict(w1t=dwt, w1o=dwo), None))
    emit("small", 0, {}, ({k: jnp.stack(v) for k, v in gsmall.items()}, loss))
    return dx


_HBM = pl.BlockSpec(memory_space=pltpu.HBM)
_SEM = pl.BlockSpec(memory_space=pltpu.SEMAPHORE)
_EFFECT = pltpu.SideEffectType.DATAFLOW_SIDE_EFFECTING


def _peer(p):
    x, y, c = lax.axis_index("x"), lax.axis_index("y"), lax.axis_index("c")
    me = 4 * x + 2 * y + c
    return (1 - x if p & 4 else x, 1 - y if p & 2 else y, 1 - c if p & 1 else c), jnp.bitwise_xor(me, p), me


def _xchg_copy(src, land, mode, send_sems, recv_sems, k, p, arriving):
    peer, peer_id, me = _peer(p)
    block = src if mode == "gather" else src.at[peer_id]
    return pltpu.make_async_remote_copy(
        src_ref=block, dst_ref=land.at[peer_id if arriving else me], send_sem=send_sems.at[k * (N_DEV - 1) + p - 1],
        recv_sem=recv_sems.at[k * (N_DEV - 1) + p - 1], device_id=peer, device_id_type=MESH)


def _xchg_start(srcs, modes, groups, name):
    n, ng = len(srcs), len(groups)

    def body(*refs):
        src = refs[:n]
        sems = refs[n:n + 2 * ng]
        land = refs[n + 2 * ng + n:n + 2 * ng + 2 * n]
        token = refs[n + 2 * ng + 2 * n]
        for gi, idx in enumerate(groups):
            for ki, k in enumerate(idx):
                for p in range(1, N_DEV):
                    _xchg_copy(src[k], land[k], modes[k], sems[2 * gi], sems[2 * gi + 1], ki, p, False).start()
        token[...] = jnp.zeros_like(token)

    sem_shapes = []
    for idx in groups:
        sem_shapes += [pltpu.SemaphoreType.DMA((len(idx) * (N_DEV - 1),))] * 2
    outs = pl.pallas_call(
        body,
        out_shape=sem_shapes + [pltpu.HBM(a.shape, a.dtype) for a in srcs]
        + [pltpu.HBM((N_DEV,) + a.shape[-2:], a.dtype) for a in srcs] + [jax.ShapeDtypeStruct((8, 128), F32)],
        in_specs=[_HBM] * n,
        out_specs=[_SEM] * (2 * ng) + [_HBM] * (2 * n) + [pl.BlockSpec(memory_space=pltpu.VMEM)],
        input_output_aliases={i: 2 * ng + i for i in range(n)},
        compiler_params=pltpu.CompilerParams(has_side_effects=_EFFECT),
        name=name,
    )(*[pltpu.with_memory_space_constraint(a, pltpu.HBM) for a in srcs])
    sems = [(outs[2 * gi], outs[2 * gi + 1]) for gi in range(ng)]
    return sems, outs[2 * ng:2 * ng + n], outs[2 * ng + n:2 * ng + 2 * n], outs[-1]


def _xchg_wait_call(srcs, lands, modes, sems, after, name):
    n = len(srcs)

    def body(*refs):
        src, land = refs[:n], refs[n:2 * n]
        send_sems, recv_sems = refs[2 * n], refs[2 * n + 1]
        for p in range(1, N_DEV):
            for k in range(n):
                cp = _xchg_copy(src[k], land[k], modes[k], send_sems, recv_sems, k, p, True)
                cp.wait_send()
                cp.wait_recv()

    outs = pl.pallas_call(
        body,
        out_shape=[pltpu.HBM(a.shape, a.dtype) for a in list(srcs) + list(lands)],
        in_specs=[_HBM] * (2 * n) + [_SEM, _SEM, pl.BlockSpec(memory_space=pl.ANY)],
        out_specs=[_HBM] * (2 * n),
        input_output_aliases={i: i for i in range(2 * n)},
        compiler_params=pltpu.CompilerParams(has_side_effects=_EFFECT),
        name=name,
    )(*srcs, *lands, sems[0], sems[1], after)
    return outs[:n], outs[n:]


def _xchg_wait(srcs, lands, modes, sems, after, name):
    srcs, lands = _xchg_wait_call(srcs, lands, modes, sems, after, name)
    me = 4 * lax.axis_index("x") + 2 * lax.axis_index("y") + lax.axis_index("c")
    done = []
    for a, land, mode in zip(srcs, lands, modes):
        own = a[None] if mode == "gather" else lax.dynamic_slice_in_dim(a, me, 1, axis=0)
        done.append(lax.dynamic_update_slice(land, own, (me, 0, 0)))
    return done


def _sum_slots(land):
    g, _, r, c = land.shape
    br = r // 2 if (r % 32 == 0 and r >= 256) else r

    def body(l_ref, o_ref):
        acc = l_ref[0, 0].astype(F32)
        for j in range(1, N_DEV):
            acc = acc + l_ref[0, j].astype(F32)
        o_ref[0] = acc

    return pl.pallas_call(
        body,
        out_shape=jax.ShapeDtypeStruct((g, r, c), F32),
        grid=(g, r // br),
        in_specs=[pl.BlockSpec((1, N_DEV, br, c), lambda i, j: (i, 0, j, 0))],
        out_specs=pl.BlockSpec((1, br, c), lambda i, j: (i, j, 0)),
        compiler_params=_cparams(("parallel", "parallel")),
        name="sum_slots",
    )(land)


def _adamw(w, g, m, v):
    shape = w.shape
    cols = shape[-1]
    rows = int(np.prod(shape[:-1]))
    bm = max(b for b in range(8, 513, 8) if rows % b == 0) if rows % 8 == 0 else rows
    c1 = 1.0 - ADAM_B1 ** ADAM_STEP
    c2 = 1.0 - ADAM_B2 ** ADAM_STEP

    def fn(ins, consts):
        wv, gv, mv, vv = ins
        m2 = ADAM_B1 * mv + (1.0 - ADAM_B1) * gv
        v2 = ADAM_B2 * vv + (1.0 - ADAM_B2) * (gv * gv)
        delta = -ADAM_LR * ((m2 / c1) / (jnp.sqrt(v2 / c2) + ADAM_EPS) + ADAM_WD * wv)
        return [delta, m2, v2], []

    outs, _ = _rowwise(fn, [(a.reshape(rows, cols), cols, 0) for a in (w, g, m, v)], [], [(cols, F32)] * 3, [],
                       bm=bm, name="adamw")
    return [o.reshape(shape) for o in outs]


BIG = ("w1t", "w1o", "wint", "wa", "wbt", "wo", "w2t", "w2o")
FETCH_GROUPS = dict(w1t=("w1t",), w1o=("w1o",), wint=("wint",), mout=("wa", "wbt", "wo"), w2t=("w2t",), w2o=("w2o",))
SMALL_ROWS = (("ffn1_norm", 0), ("mix_norm", 2), ("lbsum", 4), ("hgrn_out_norm", 6), ("ffn2_norm", 8),
              ("attn_q_norm", 10), ("attn_k_norm", 12))
SMALL_PACK_ROWS = 16


def kernel(x, ffn1_norm, ffn1_w_in, ffn1_w_out, mix_norm, w_in, hgrn_lb_logits, hgrn_out_norm, attn_q_norm, attn_k_norm, w_branch_a, w_branch_b, w_out, ffn2_norm, ffn2_w_in, ffn2_w_out, loss_target, m_ffn1_norm, m_ffn1_w_in, m_ffn1_w_out, m_mix_norm, m_w_in, m_hgrn_lb_logits, m_hgrn_out_norm, m_attn_q_norm, m_attn_k_norm, m_w_branch_a, m_w_branch_b, m_w_out, m_ffn2_norm, m_ffn2_w_in, m_ffn2_w_out, v_ffn1_norm, v_ffn1_w_in, v_ffn1_w_out, v_mix_norm, v_w_in, v_hgrn_lb_logits, v_hgrn_out_norm, v_attn_q_norm, v_attn_k_norm, v_w_branch_a, v_w_branch_b, v_w_out, v_ffn2_norm, v_ffn2_w_in, v_ffn2_w_out):
    names = ("ffn1_norm", "ffn1_w_in", "ffn1_w_out", "mix_norm", "w_in", "hgrn_lb_logits", "hgrn_out_norm", "attn_q_norm",
             "attn_k_norm", "w_branch_a", "w_branch_b", "w_out", "ffn2_norm", "ffn2_w_in", "ffn2_w_out")
    w = dict(zip(names, (ffn1_norm, ffn1_w_in, ffn1_w_out, mix_norm, w_in, hgrn_lb_logits, hgrn_out_norm, attn_q_norm,
                         attn_k_norm, w_branch_a, w_branch_b, w_out, ffn2_norm, ffn2_w_in, ffn2_w_out)))
    m = dict(zip(names, (m_ffn1_norm, m_ffn1_w_in, m_ffn1_w_out, m_mix_norm, m_w_in, m_hgrn_lb_logits, m_hgrn_out_norm,
                         m_attn_q_norm, m_attn_k_norm, m_w_branch_a, m_w_branch_b, m_w_out, m_ffn2_norm, m_ffn2_w_in, m_ffn2_w_out)))
    v = dict(zip(names, (v_ffn1_norm, v_ffn1_w_in, v_ffn1_w_out, v_mix_norm, v_w_in, v_hgrn_lb_logits, v_hgrn_out_norm,
                         v_attn_q_norm, v_attn_k_norm, v_w_branch_a, v_w_branch_b, v_w_out, v_ffn2_norm, v_ffn2_w_in, v_ffn2_w_out)))
    depth, d = ffn1_norm.shape

    def tr(a):
        return jnp.swapaxes(a, 1, 2)

    shard = dict(w1t=tr(ffn1_w_in), w1o=ffn1_w_out, wint=tr(w_in), wa=w_branch_a,
                 wbt=tr(w_branch_b).reshape(depth, -1, d), wo=w_out, w2t=tr(ffn2_w_in), w2o=ffn2_w_out)
    order = [(g, l) for l in range(depth) for g in FETCH_GROUPS]
    started = {}
    for name, part in (("gather_start", order),):
        flat = [(l, k) for g, l in part for k in FETCH_GROUPS[g]]
        groups, pos = [], 0
        for g, l in part:
            groups.append(list(range(pos, pos + len(FETCH_GROUPS[g]))))
            pos += len(FETCH_GROUPS[g])
        sems, srcs, lands, _ = _xchg_start([shard[k][l].astype(BF16) for l, k in flat], ["gather"] * len(flat), groups, name)
        for gi, key in enumerate(part):
            started[key] = ([srcs[i] for i in groups[gi]], [lands[i] for i in groups[gi]], sems[gi])

    def fetch(group, l, after):
        srcs, lands, sems = started[group, l]
        lands = _xchg_wait(srcs, lands, ["gather"] * len(srcs), sems, after, f"gather_wait_{group}{l}")
        out = {}
        for k, land in zip(FETCH_GROUPS[group], lands):
            out[k] = land.reshape(d, -1) if k == "wbt" else land.reshape(-1, d)
        return out

    pending = []

    def emit(group, l, g, final):
        keys = list(g)
        srcs = [g[k].reshape(N_DEV, -1, d) for k in keys]
        modes = ["scatter"] * len(keys)
        if final is not None:
            gsmall, loss = final
            pack = jnp.zeros((SMALL_PACK_ROWS, d), F32)
            for k, r0 in SMALL_ROWS:
                rows = gsmall[k].reshape(depth, -1)
                pack = pack.at[r0:r0 + depth, :rows.shape[1]].set(rows)
            srcs.append(pack.at[14, :].set(loss))
            modes.append("gather")
            keys.append("small")
        sems, s_thru, l_thru, token = _xchg_start(srcs, modes, [list(range(len(srcs)))], f"grads_start_{group}{l}")
        pending.append((group, l, keys, modes, sems[0], s_thru, l_thru))
        return token

    small = {k: w[k] for k in ("ffn1_norm", "mix_norm", "hgrn_lb_logits", "hgrn_out_norm", "attn_q_norm", "attn_k_norm", "ffn2_norm")}
    dx = _local_step(x[0], loss_target[0], small, fetch, emit)

    summed, after = {}, dx
    for group, l, keys, modes, sems, s_thru, l_thru in pending:
        lands = _xchg_wait(s_thru, l_thru, modes, sems, after, f"grads_wait_{group}{l}")
        for k, land in zip(keys, lands):
            summed[k, l] = _sum_slots(land[None])[0]
        after = summed[keys[-1], l]
    gsum = {k: jnp.stack([summed[k, l] for l in range(depth)]) for k in BIG}
    tot = summed["small", 0]

    grads = {}
    for k, r0 in SMALL_ROWS:
        shp = (depth,) + (w[k].shape[1:] if k != "lbsum" else (d,))
        grads[k] = tot[r0:r0 + depth, :int(np.prod(shp[1:]))].reshape(shp)
    _, lb_vjp = jax.vjp(_lower_bounds, hgrn_lb_logits)
    grads["hgrn_lb_logits"] = lb_vjp(grads.pop("lbsum"))[0]
    grads["ffn1_w_in"], grads["ffn1_w_out"] = tr(gsum["w1t"]), gsum["w1o"]
    grads["w_in"], grads["w_branch_a"] = tr(gsum["wint"]), gsum["wa"]
    grads["w_branch_b"] = tr(gsum["wbt"].reshape(depth, d // N_DEV, -1))
    grads["w_out"] = gsum["wo"]
    grads["ffn2_w_in"], grads["ffn2_w_out"] = tr(gsum["w2t"]), gsum["w2o"]

    upd = {k: _adamw(w[k], grads[k], m[k], v[k]) for k in names}
    return (tot[14, 0], dx[None], *[grads[k] for k in names], *[upd[k][0] for k in names],
            *[upd[k][1] for k in names], *[upd[k][2] for k in names])
```

```python
import functools

import jax
import jax.numpy as jnp
import numpy as np
from jax import lax
from jax.experimental import pallas as pl
from jax.experimental.pallas import tpu as pltpu

F32 = jnp.float32
BF16 = jnp.bfloat16

N_DEV = 8
EPS = 1e-6
HG_DK = 128
HG_CHUNK = 64
HG_SUB = 16
HG_HP = 8
ATT_PATTERNS = ((128, 1), (512, 4), (2048, 16))
ATT_GROUPS = 3
ATT_HEADS = 4
ATT_DH = 128
ATT_BLK = 128
ROPE_THETA = 10000.0
ADAM_LR, ADAM_B1, ADAM_B2, ADAM_EPS, ADAM_WD, ADAM_STEP = 0.001, 0.9, 0.999, 1e-08, 0.01, 10
VMEM_LIMIT_BYTES = 56 * 1024 * 1024
MXU_COLS = 256
MESH = pl.DeviceIdType.MESH


def _cparams(sem, **kw):
    return pltpu.CompilerParams(dimension_semantics=sem, vmem_limit_bytes=VMEM_LIMIT_BYTES, **kw)


def _sigmoid(x):
    return 1.0 / (1.0 + jnp.exp(-x))


def _mm(a_list, b_list, pairs, n_acc, fin, out_dtypes, *, m, n, k, ta=False, tb=False, bm, bn, bk,
        b_off=None, extras=(), e_off=None, n_outer=False, consts=(), a_pro=None, n_sums=0, chunk=0, a_cat=False, name):
    bm, bn, bk = min(bm, m), min(bn, n), min(bk, k)
    assert m % bm == 0 and n % bn == 0 and k % bk == 0, (name, m, n, k, bm, bn, bk)
    nk = k // bk
    assert not (a_pro and (nk > 1 or ta or n_outer)) and not (n_sums and (bn != n or n_outer)), name
    assert not (chunk and (nk > 1 or n_sums or chunk % 128)), name
    if a_cat:
        unit = bm if ta else bk
        widths = [a.shape[1] for a in a_list]
        assert all(w % unit == 0 for w in widths) and sum(widths) == (m if ta else k) and not a_pro, name
        cat_counts = [w // unit for w in widths]
        cat_starts = [sum(cat_counts[:i]) for i in range(len(widths))]
    b_off = b_off or [(0, 0)] * len(b_list)
    e_off = e_off or [0] * len(extras)
    na, nb, ne, nc, no = len(a_list), len(b_list), len(extras), len(consts), len(out_dtypes)
    nao = na if a_pro else 0
    dn = (((0,) if ta else (1,), (1,) if tb else (0,)), ((), ()))

    def body(*refs):
        refs = list(refs)
        a_refs, b_refs, e_refs, c_refs, o_refs, ao_refs, s_refs = (
            [refs.pop(0) for _ in range(cnt)] for cnt in (na, nb, ne, nc, no, nao, n_sums))
        acc_refs = refs
        kk = pl.program_id(2)
        first = pl.program_id(0) == 0
        cvals = [c[...] for c in c_refs]
        a_vals = [r[...] for r in a_refs]
        if a_cat:
            col = pl.program_id(1 if n_outer else 0) if ta else kk
            sel = a_vals[0]
            for start, v in zip(cat_starts[1:], a_vals[1:]):
                sel = jnp.where(col >= start, v, sel)
            a_vals = [sel]
        if a_pro:
            @pl.when(pl.program_id(1) == 0)
            def _():
                for r, v in zip(ao_refs, a_pro(a_vals, cvals)):
                    r[...] = v

            a_vals = [r[...] for r in ao_refs]
        if chunk:
            spans = [slice(lo, min(lo + chunk, bn)) for lo in range(0, bn, chunk)]
            chunks = []
            for cs in spans:
                parts = [None] * n_acc
                for ai, bi, ci in pairs:
                    p = lax.dot_general(a_vals[ai], b_refs[bi][cs, :] if tb else b_refs[bi][:, cs], dn,
                                        preferred_element_type=F32)
                    parts[ci] = p if parts[ci] is None else parts[ci] + p
                chunks.append(parts)
            for cs, parts in zip(spans, chunks):
                ex = [e[:, cs] for e in e_refs]
                outs = fin(parts, ex, cvals) if nc else fin(parts, ex)
                for o_ref, o in zip(o_refs, outs):
                    o_ref[:, cs] = o.astype(o_ref.dtype)
            return

        parts = [None] * n_acc
        for ai, bi, ci in pairs:
            p = lax.dot_general(a_vals[ai], b_refs[bi][...], dn, preferred_element_type=F32)
            parts[ci] = p if parts[ci] is None else parts[ci] + p

        def finish(accs):
            ex = [e[...] for e in e_refs]
            res = fin(accs, ex, cvals) if nc else fin(accs, ex)
            outs, sums = res if n_sums else (res, ())
            for o_ref, o in zip(o_refs, outs):
                o_ref[...] = o.astype(o_ref.dtype)
            if n_sums:
                @pl.when(first)
                def _():
                    for s_ref, s in zip(s_refs, sums):
                        s_ref[...] = s

                @pl.when(jnp.logical_not(first))
                def _():
                    for s_ref, s in zip(s_refs, sums):
                        s_ref[...] += s

        if nk == 1:
            finish(parts)
        else:
            @pl.when(kk == 0)
            def _():
                for c in range(n_acc):
                    acc_refs[c][...] = parts[c]

            @pl.when(kk > 0)
            def _():
                for c in range(n_acc):
                    acc_refs[c][...] += parts[c]

            @pl.when(kk == nk - 1)
            def _():
                finish([acc_refs[c][...] for c in range(n_acc)])

    def ij(f):
        return (lambda j, i, q: f(i, j, q)) if n_outer else f

    a_spec = pl.BlockSpec((bk, bm), ij(lambda i, j, q: (q, i))) if ta else pl.BlockSpec((bm, bk), ij(lambda i, j, q: (i, q)))
    a_specs = [a_spec] * na
    if a_cat:
        def part_spec(start, count):
            def col(c):
                return jnp.clip(c - start, 0, count - 1)
            if ta:
                return pl.BlockSpec((bk, bm), ij(lambda i, j, q: (q, col(i))))
            return pl.BlockSpec((bm, bk), ij(lambda i, j, q: (i, col(q))))
        a_specs = [part_spec(s, c) for s, c in zip(cat_starts, cat_counts)]

    b_mode = dict(pipeline_mode=pl.Buffered(1)) if (bn == n and nk == 1) else {}

    def b_spec(off):
        on, ok = off
        if tb:
            return pl.BlockSpec((bn, bk), ij(lambda i, j, q: (j + on, q + ok)), **b_mode)
        return pl.BlockSpec((bk, bn), ij(lambda i, j, q: (q + ok, j + on)), **b_mode)

    mn_spec = pl.BlockSpec((bm, bn), ij(lambda i, j, q: (i, j)))
    outs = pl.pallas_call(
        body,
        out_shape=[jax.ShapeDtypeStruct((m, n), d) for d in out_dtypes] + [jax.ShapeDtypeStruct((m, k), BF16)] * nao
        + [jax.ShapeDtypeStruct((8, n), F32)] * n_sums,
        grid=(n // bn, m // bm, nk) if n_outer else (m // bm, n // bn, nk),
        in_specs=a_specs + [b_spec(o) for o in b_off]
        + [pl.BlockSpec((bm, bn), ij(lambda i, j, q, o=o: (i, j + o))) for o in e_off]
        + [pl.BlockSpec(c.shape, lambda *_, nd=c.ndim: (0,) * nd) for c in consts],
        out_specs=[mn_spec] * no + [a_spec] * nao + [pl.BlockSpec((8, n), lambda *_: (0, 0))] * n_sums,
        scratch_shapes=[pltpu.VMEM((bm, bn), F32) for _ in range(n_acc if nk > 1 else 0)],
        compiler_params=_cparams(("arbitrary" if n_sums else "parallel", "parallel", "arbitrary")),
        name=name,
    )(*a_list, *b_list, *extras, *consts)
    return outs


def _first(accs, ex):
    return (accs[0],)


def _rowwise(fn, ins, consts, out_defs, sum_widths, *, bm, name):
    ins = [tuple(e) + (1,) * (4 - len(e)) for e in ins]
    out_defs = [tuple(e) + (1,) * (3 - len(e)) for e in out_defs]
    t = ins[0][0].shape[-2] * ins[0][3]
    bm = min(bm, t)
    assert t % bm == 0, (name, t, bm)
    ni, nc, no, ns = len(ins), len(consts), len(out_defs), len(sum_widths)
    strided = [w for _, w, _, d in ins if d > 1] + [w for w, _, d in out_defs if d > 1]

    def body(*refs):
        i_refs, c_refs = refs[:ni], refs[ni:ni + nc]
        o_refs, s_refs = refs[ni + nc:ni + nc + no], refs[ni + nc + no:ni + nc + no + ns]
        scratch = list(refs[ni + nc + no + ns:])
        vals = []
        for ref, (_, w, _, d) in zip(i_refs, ins):
            if d == 1:
                vals.append(ref[...])
                continue
            s = scratch.pop(0)
            for r in range(d):
                for c in range(w // 128):
                    s.at[c][pl.ds(r, bm // d, stride=d), :] = ref[r, :, c * 128:(c + 1) * 128].astype(F32)
            vals.append(jnp.concatenate([s[c] for c in range(w // 128)], axis=1))
        outs, sums = fn(vals, [r[...] for r in c_refs])
        for o_ref, o, (w, _, d) in zip(o_refs, outs, out_defs):
            if d == 1:
                o_ref[...] = o.astype(o_ref.dtype)
                continue
            s = scratch.pop(0)
            for c in range(w // 128):
                s[c] = o[:, c * 128:(c + 1) * 128].astype(F32)
            for r in range(d):
                for c in range(w // 128):
                    o_ref[r, :, c * 128:(c + 1) * 128] = s.at[c][pl.ds(r, bm // d, stride=d), :].astype(o_ref.dtype)
        if ns:
            first = pl.program_id(0) == 0

            @pl.when(first)
            def _():
                for s_ref, s in zip(s_refs, sums):
                    s_ref[...] = s

            @pl.when(jnp.logical_not(first))
            def _():
                for s_ref, s in zip(s_refs, sums):
                    s_ref[...] += s

    def win(width, cb, d):
        if d > 1:
            return pl.BlockSpec((d, bm // d, width), lambda i: (0, i, 0))
        return pl.BlockSpec((bm, width), lambda i: (i, cb))

    res = pl.pallas_call(
        body,
        out_shape=[jax.ShapeDtypeStruct((t, w) if d == 1 else (d, t // d, w), dt) for w, dt, d in out_defs]
        + [jax.ShapeDtypeStruct((8, w), F32) for w in sum_widths],
        grid=(t // bm,),
        in_specs=[win(w, cb, d) for _, w, cb, d in ins] + [pl.BlockSpec(c.shape, lambda i, nd=c.ndim: (0,) * nd) for c in consts],
        out_specs=[win(w, 0, d) for w, _, d in out_defs] + [pl.BlockSpec((8, w), lambda i: (0, 0)) for w in sum_widths],
        scratch_shapes=[pltpu.VMEM((w // 128, bm, 128), F32) for w in strided],
        compiler_params=_cparams(("arbitrary",) if ns else ("parallel",)),
        name=name,
    )(*[e[0] for e in ins], *consts)
    return res[:no], [jnp.sum(s, axis=0) for s in res[no:]]


def _colsum8(x):
    bm, w = x.shape
    return jnp.sum(x.reshape(bm // 8, 8, w), axis=0)


def _tri(n, upper=False):
    r = lax.broadcasted_iota(jnp.int32, (n, n), 0)
    c = lax.broadcasted_iota(jnp.int32, (n, n), 1)
    return (c >= r) if upper else (c <= r)


def _exact_tri_matmul(tri_bf16, x):
    x0 = x.astype(BF16)
    r1 = x - x0.astype(F32)
    x1 = r1.astype(BF16)
    x2 = (r1 - x1.astype(F32)).astype(BF16)
    w = x.shape[1]
    y = jnp.dot(tri_bf16, jnp.concatenate([x0, x1, x2], axis=1), preferred_element_type=F32)
    return y[:, :w] + y[:, w:2 * w] + y[:, 2 * w:]


def _dot_nt(a, b):
    return lax.dot_general(a, b, (((1,), (1,)), ((), ())), preferred_element_type=F32)


def _dot_tn(a, b):
    return lax.dot_general(a, b, (((0,), (0,)), ((), ())), preferred_element_type=F32)


def _dot(a, b):
    return jnp.dot(a, b, preferred_element_type=F32)


def _hg_gates(hq, hf, lb):
    sq = _sigmoid(hq)
    q = hq * sq
    sg = _sigmoid(hf)
    f = lb + (1.0 - lb) * sg
    return q, sq, sg, f


def _hg_heads(x, hp):
    return [x[:, h * HG_DK:(h + 1) * HG_DK] for h in range(hp)]


def _hg_intra_wide(q, kk, g, hp):
    c = q.shape[0]
    rows = lax.broadcasted_iota(jnp.int32, (c, 1), 0)
    a_rows = [[] for _ in range(hp)]
    qts, kts, eqs, eks = [], [], [], []
    for i in range(c // HG_SUB):
        lo = i * HG_SUB
        ref = g[lo - 1:lo, :] if i else jnp.zeros_like(g[0:1, :])
        eq = jnp.exp(g[lo:lo + HG_SUB, :] - ref)
        ek = jnp.exp(jnp.where(rows < lo + HG_SUB, ref - g, 0.0))
        qtb = (q[lo:lo + HG_SUB, :] * eq).astype(BF16)
        ktb = (kk * ek).astype(BF16)
        tpos = lo + lax.broadcasted_iota(jnp.int32, (HG_SUB, c), 0)
        spos = lax.broadcasted_iota(jnp.int32, (HG_SUB, c), 1)
        for h, (qh, kh) in enumerate(zip(_hg_heads(qtb, hp), _hg_heads(ktb, hp))):
            a_rows[h].append(jnp.where(spos <= tpos, _dot_nt(qh, kh), 0.0))
        qts.append(qtb), kts.append(ktb), eqs.append(eq), eks.append(ek)
    return [jnp.concatenate(r, axis=0) for r in a_rows], qts, kts, eqs, eks


def _hgrn_fwd(zh, lb3, *, tb=512):
    t = zh.shape[0]
    nh = lb3.shape[0]
    c = HG_CHUNK
    tb = min(tb, t)
    nchunk = tb // c
    hp = HG_HP if nh % HG_HP == 0 else 1
    wp = hp * HG_DK

    def body(hq_ref, hf_ref, hi_ref, lb_ref, o_ref, st_ref, state):
        @pl.when(pl.program_id(1) == 0)
        def _():
            state[...] = jnp.zeros_like(state)

        tril = _tri(c).astype(BF16)

        def chunk(ci, carry):
            sl = pl.ds(pl.multiple_of(ci * c, c), c)
            q, _, _, f = _hg_gates(hq_ref[sl, :], hf_ref[sl, :], lb_ref[...])
            kk = 1.0 - f
            g = _exact_tri_matmul(tril, jnp.log(f))
            a, _, _, _, _ = _hg_intra_wide(q, kk, g, hp)
            vb = hi_ref[sl, :].astype(BF16)
            glast = g[c - 1:c, :]
            qgb = (q * jnp.exp(g)).astype(BF16)
            kgb = (kk * jnp.exp(glast - g)).astype(BF16)
            dec = jnp.exp(glast)
            sts = [state[h] for h in range(hp)]
            for h in range(hp):
                st_ref[h, ci] = sts[h]
            vh, qgh, kgh, dech = _hg_heads(vb, hp), _hg_heads(qgb, hp), _hg_heads(kgb, hp), _hg_heads(dec, hp)
            o = [_dot(a[h].astype(BF16), vh[h]) + _dot_nt(qgh[h], sts[h].astype(BF16)) for h in range(hp)]
            new = [_dot_tn(vh[h], kgh[h]) for h in range(hp)]
            o_ref[sl, :] = jnp.concatenate(o, axis=1)
            for h in range(hp):
                state[h] = sts[h] * dech[h] + new[h]
            return carry

        lax.fori_loop(0, nchunk, chunk, 0)

    def col(cb):
        return pl.BlockSpec((tb, wp), lambda h, i: (i, cb * (nh // hp) + h))

    return pl.pallas_call(
        body,
        out_shape=[jax.ShapeDtypeStruct((t, nh * HG_DK), F32), jax.ShapeDtypeStruct((nh, t // c, HG_DK, HG_DK), F32)],
        grid=(nh // hp, t // tb),
        in_specs=[col(0), col(1), col(2), pl.BlockSpec((1, wp), lambda h, i: (0, h))],
        out_specs=[pl.BlockSpec((tb, wp), lambda h, i: (i, h)),
                   pl.BlockSpec((hp, nchunk, HG_DK, HG_DK), lambda h, i: (h, i, 0, 0))],
        scratch_shapes=[pltpu.VMEM((hp, HG_DK, HG_DK), F32)],
        compiler_params=_cparams(("parallel", "arbitrary")),
        name="hgrn_fwd",
    )(zh, zh, zh, lb3.reshape(1, -1))


def _hgrn_bwd(zh, lb3, states, d_o, *, tb=512):
    t = zh.shape[0]
    nh = lb3.shape[0]
    c = HG_CHUNK
    tb = min(tb, t)
    nchunk = tb // c
    nblk = t // tb
    hp = HG_HP if nh % HG_HP == 0 else 1
    wp = hp * HG_DK

    def body(hq_ref, hf_ref, hi_ref, lb_ref, st_ref, do_ref, dq_ref, df_ref, dv_ref, dlb_ref, dstate):
        @pl.when(pl.program_id(1) == 0)
        def _():
            dstate[...] = jnp.zeros_like(dstate)
            dlb_ref[...] = jnp.zeros_like(dlb_ref)

        tril = _tri(c).astype(BF16)
        triu = _tri(c, upper=True).astype(BF16)
        last_row = lax.broadcasted_iota(jnp.int32, (c, 1), 0) == c - 1
        heads = range(hp)

        def chunk(j, carry):
            ci = nchunk - 1 - j
            sl = pl.ds(pl.multiple_of(ci * c, c), c)
            lb = lb_ref[...]
            hq, hf = hq_ref[sl, :], hf_ref[sl, :]
            q, sq, sg, f = _hg_gates(hq, hf, lb)
            kk = 1.0 - f
            g = _exact_tri_matmul(tril, jnp.log(f))
            a, qts, kts, eqs, eks = _hg_intra_wide(q, kk, g, hp)
            glast = g[c - 1:c, :]
            eg, egl, dec = jnp.exp(g), jnp.exp(glast - g), jnp.exp(glast)
            vb, dob = hi_ref[sl, :].astype(BF16), do_ref[sl, :].astype(BF16)
            qgb, kgb = (q * eg).astype(BF16), (kk * egl).astype(BF16)
            sts = [st_ref[h, ci] for h in heads]
            dsts = [dstate[h] for h in heads]
            stb, dstb = [s.astype(BF16) for s in sts], [s.astype(BF16) for s in dsts]
            vh, doh, qgh, kgh = _hg_heads(vb, hp), _hg_heads(dob, hp), _hg_heads(qgb, hp), _hg_heads(kgb, hp)
            dv = [_dot_tn(a[h].astype(BF16), doh[h]) + _dot_nt(kgh[h], dstb[h]) for h in heads]
            da = [jnp.where(_tri(c), _dot_nt(doh[h], vh[h]), 0.0).astype(BF16) for h in heads]
            dq_inter = jnp.concatenate([_dot(doh[h], stb[h]) for h in heads], axis=1) * eg
            dk_state = jnp.concatenate([_dot(vh[h], dstb[h]) for h in heads], axis=1) * egl
            new_dst = [_dot_tn(doh[h], qgh[h]) for h in heads]
            xs, dk, dgk = [], dk_state, 0.0
            for i in range(c // HG_SUB):
                rs = slice(i * HG_SUB, (i + 1) * HG_SUB)
                kth, qth = _hg_heads(kts[i], hp), _hg_heads(qts[i], hp)
                xi = jnp.concatenate([_dot(da[h][rs, :], kth[h]) for h in heads], axis=1)
                yi = jnp.concatenate([_dot_tn(da[h][rs, :], qth[h]) for h in heads], axis=1)
                xs.append(xi)
                dk = dk + yi * eks[i]
                dgk = dgk + yi * kts[i].astype(F32)
            dq = jnp.concatenate([x * e for x, e in zip(xs, eqs)], axis=0) + dq_inter
            dgq = jnp.concatenate([x * qt.astype(F32) for x, qt in zip(xs, qts)], axis=0)
            dg = dgq - dgk + q * dq_inter - kk * dk_state
            sdot = jnp.concatenate([jnp.sum(sts[h] * dsts[h], axis=0, keepdims=True) for h in heads], axis=1)
            dgl = jnp.sum(kk * dk_state, axis=0, keepdims=True) + dec * sdot
            dg = dg + jnp.where(last_row, dgl, 0.0)
            dlogf = _exact_tri_matmul(triu, dg)
            dfv = dlogf / f - dk
            dq_ref[sl, :] = (dq * (sq * (1.0 + hq * (1.0 - sq)))).astype(dq_ref.dtype)
            df_ref[sl, :] = (dfv * (1.0 - lb) * sg * (1.0 - sg)).astype(df_ref.dtype)
            dv_ref[sl, :] = jnp.concatenate(dv, axis=1).astype(dv_ref.dtype)
            dlb_ref[...] += jnp.sum(dfv * (1.0 - sg), axis=0, keepdims=True)
            dech = _hg_heads(dec, hp)
            for h in heads:
                dstate[h] = dsts[h] * dech[h] + new_dst[h]
            return carry

        lax.fori_loop(0, nchunk, chunk, 0)

    def col(cb):
        return pl.BlockSpec((tb, wp), lambda h, i: (nblk - 1 - i, cb * (nh // hp) + h))

    ocol = pl.BlockSpec((tb, wp), lambda h, i: (nblk - 1 - i, h))
    lbspec = pl.BlockSpec((1, wp), lambda h, i: (0, h))
    w = nh * HG_DK
    dq, df, dv, dlb = pl.pallas_call(
        body,
        out_shape=[jax.ShapeDtypeStruct((t, w), BF16)] * 3 + [jax.ShapeDtypeStruct((1, w), F32)],
        grid=(nh // hp, nblk),
        in_specs=[col(0), col(1), col(2), lbspec,
                  pl.BlockSpec((hp, nchunk, HG_DK, HG_DK), lambda h, i: (h, nblk - 1 - i, 0, 0)), ocol],
        out_specs=[ocol, ocol, ocol, lbspec],
        scratch_shapes=[pltpu.VMEM((hp, HG_DK, HG_DK), F32)],
        compiler_params=_cparams(("parallel", "arbitrary")),
        name="hgrn_bwd",
    )(zh, zh, zh, lb3.reshape(1, -1), states, d_o)
    return dq, df, dv, dlb.reshape(w)


NEG = -1e30
ATT_GW = ATT_HEADS * ATT_DH


def _att_scores(q, kp, kc, has_prev):
    scale = ATT_DH ** -0.5
    i = lax.broadcasted_iota(jnp.int32, (ATT_BLK, ATT_BLK), 0)
    j = lax.broadcasted_iota(jnp.int32, (ATT_BLK, ATT_BLK), 1)
    s_p = jnp.where(jnp.logical_and(j >= i, has_prev), _dot_nt(q, kp) * scale, NEG)
    s_c = jnp.where(j <= i, _dot_nt(q, kc) * scale, NEG)
    return s_p, s_c


def _att_views(arrs, d):
    return [a.reshape(d, -1, ATT_GW) for a in arrs]


def _att_unview(a, d):
    return a.reshape(-1, ATT_GW) if d == 1 else a


ATT_QB = 4


def _attn_fwd(qb, kb, vb, g):
    d = ATT_PATTERNS[g][1]
    q2, k2, v2 = _att_views([qb, kb, vb], d)
    nblk = q2.shape[1] // ATT_BLK
    nq = ATT_QB if nblk % ATT_QB == 0 else 1
    rows = nq * ATT_BLK

    def body(q_ref, kc_ref, kp_ref, vc_ref, vp_ref, o_ref, l_ref):
        first = pl.program_id(1) == 0
        hss = [slice(h * ATT_DH, (h + 1) * ATT_DH) for h in range(ATT_HEADS)]
        for b in range(nq):
            rs = slice(b * ATT_BLK, (b + 1) * ATT_BLK)
            ps = slice((b - 1) * ATT_BLK, b * ATT_BLK)
            has_prev = jnp.logical_not(first) if b == 0 else True
            kv = [(kp_ref[:, hs], vp_ref[:, hs]) if b == 0 else (kc_ref[ps, hs], vc_ref[ps, hs]) for hs in hss]
            sc = [_att_scores(q_ref[rs, hs], kv[h][0], kc_ref[rs, hs], has_prev) for h, hs in enumerate(hss)]
            ms = [jnp.maximum(jnp.max(s_p, axis=1, keepdims=True), jnp.max(s_c, axis=1, keepdims=True)) for s_p, s_c in sc]
            ps_ = [(jnp.exp(s_p - m), jnp.exp(s_c - m)) for (s_p, s_c), m in zip(sc, ms)]
            ls = [jnp.sum(p_p, axis=1, keepdims=True) + jnp.sum(p_c, axis=1, keepdims=True) for p_p, p_c in ps_]
            os_ = [_dot(p_p.astype(BF16), kv[h][1]) + _dot(p_c.astype(BF16), vc_ref[rs, hss[h]]) for h, (p_p, p_c) in enumerate(ps_)]
            for h, hs in enumerate(hss):
                o_ref[rs, hs] = os_[h] / ls[h]
                l_ref[rs, hs] = jnp.broadcast_to(ms[h] + jnp.log(ls[h]), (ATT_BLK, ATT_DH))

    cur = pl.BlockSpec((None, rows, ATT_GW), lambda r, n: (r, n, 0))
    prev = pl.BlockSpec((None, ATT_BLK, ATT_GW), lambda r, n: (r, jnp.maximum(n * nq - 1, 0), 0))
    o, lse = pl.pallas_call(
        body,
        out_shape=[jax.ShapeDtypeStruct(q2.shape, F32)] * 2,
        grid=(d, nblk // nq),
        in_specs=[cur, cur, prev, cur, prev],
        out_specs=[cur, cur],
        compiler_params=_cparams(("parallel", "arbitrary")),
        name=f"attn_fwd_g{g}",
    )(q2, k2, k2, v2, v2)
    return _att_unview(o, d), _att_unview(lse, d)


def _attn_bwd(qb, kb, vb, o, lse, d_o, d_lse, g):
    d = ATT_PATTERNS[g][1]
    q2, k2, v2 = _att_views([qb, kb, vb], d)
    o2, l2, do2, dl2 = _att_views([o, lse, d_o, d_lse], d)
    nblk = q2.shape[1] // ATT_BLK
    nq = ATT_QB if nblk % ATT_QB == 0 else 1
    rows = nq * ATT_BLK
    ns = nblk // nq
    scale = ATT_DH ** -0.5

    def body(q_ref, kc_ref, kp_ref, vc_ref, vp_ref, o_ref, l_ref, do_ref, dl_ref, dq_ref, dk_ref, dv_ref, ck, cv):
        n = pl.program_id(1)

        @pl.when(n == 0)
        def _():
            ck[...] = jnp.zeros_like(ck)
            cv[...] = jnp.zeros_like(cv)

        first = n == ns - 1
        hss = [slice(h * ATT_DH, (h + 1) * ATT_DH) for h in range(ATT_HEADS)]
        heads = range(ATT_HEADS)
        pend_k, pend_v = [ck[:, hs] for hs in hss], [cv[:, hs] for hs in hss]
        for b in reversed(range(nq)):
            rs = slice(b * ATT_BLK, (b + 1) * ATT_BLK)
            ps = slice((b - 1) * ATT_BLK, b * ATT_BLK)
            has_prev = jnp.logical_not(first) if b == 0 else True
            q = [q_ref[rs, hs] for hs in hss]
            kc, vc = [kc_ref[rs, hs] for hs in hss], [vc_ref[rs, hs] for hs in hss]
            kp = [kp_ref[:, hs] if b == 0 else kc_ref[ps, hs] for hs in hss]
            vp = [vp_ref[:, hs] if b == 0 else vc_ref[ps, hs] for hs in hss]
            sc = [_att_scores(q[h], kp[h], kc[h], has_prev) for h in heads]
            dob = [do_ref[rs, hs].astype(BF16) for hs in hss]
            dp = [(_dot_nt(dob[h], vp[h]), _dot_nt(dob[h], vc[h])) for h in heads]
            delta = [jnp.sum(do_ref[rs, hs] * o_ref[rs, hs] - dl_ref[rs, hs], axis=1, keepdims=True) for hs in hss]
            pr = [(jnp.exp(sc[h][0] - l_ref[rs, hss[h]][:, 0:1]), jnp.exp(sc[h][1] - l_ref[rs, hss[h]][:, 0:1])) for h in heads]
            ds = [((pr[h][0] * (dp[h][0] - delta[h]) * scale).astype(BF16), (pr[h][1] * (dp[h][1] - delta[h]) * scale).astype(BF16))
                  for h in heads]
            pb = [(pr[h][0].astype(BF16), pr[h][1].astype(BF16)) for h in heads]
            dq = [_dot(ds[h][0], kp[h]) + _dot(ds[h][1], kc[h]) for h in heads]
            dk_c = [_dot_tn(ds[h][1], q[h]) for h in heads]
            dv_c = [_dot_tn(pb[h][1], dob[h]) for h in heads]
            dk_p = [_dot_tn(ds[h][0], q[h]) for h in heads]
            dv_p = [_dot_tn(pb[h][0], dob[h]) for h in heads]
            for h, hs in enumerate(hss):
                dq_ref[rs, hs] = dq[h]
                dk_ref[rs, hs] = pend_k[h] + dk_c[h]
                dv_ref[rs, hs] = pend_v[h] + dv_c[h]
            pend_k, pend_v = dk_p, dv_p
        for h, hs in enumerate(hss):
            ck[:, hs] = pend_k[h]
            cv[:, hs] = pend_v[h]

    cur = pl.BlockSpec((None, rows, ATT_GW), lambda r, n: (r, ns - 1 - n, 0))
    prev = pl.BlockSpec((None, ATT_BLK, ATT_GW), lambda r, n: (r, jnp.maximum((ns - 1 - n) * nq - 1, 0), 0))
    shp = jax.ShapeDtypeStruct(q2.shape, F32)
    dq, dk, dv = pl.pallas_call(
        body,
        out_shape=[shp, shp, shp],
        grid=(d, ns),
        in_specs=[cur, cur, prev, cur, prev, cur, cur, cur, cur],
        out_specs=[cur, cur, cur],
        scratch_shapes=[pltpu.VMEM((ATT_BLK, ATT_GW), F32), pltpu.VMEM((ATT_BLK, ATT_GW), F32)],
        compiler_params=_cparams(("parallel", "arbitrary")),
        name=f"attn_bwd_g{g}",
    )(q2, k2, k2, v2, v2, o2, l2, do2, dl2)
    return _att_unview(dq, d), _att_unview(dk, d), _att_unview(dv, d)


def _rms_parts(x, width):
    outs = []
    for lo in range(0, x.shape[1], width):
        xs = x[:, lo:lo + width].astype(F32)
        r = lax.rsqrt(jnp.mean(xs * xs, axis=1, keepdims=True) + EPS)
        outs.append((xs * r, r))
    return outs


def _rms_bwd_part(xh, r, dxh):
    return r * (dxh - xh * jnp.mean(dxh * xh, axis=1, keepdims=True))


def _norm_pro(a, consts):
    (xh, _), = _rms_parts(a[0], a[0].shape[1])
    return [(xh * consts[0]).astype(BF16)]


def _norm_bwd_fin(accs, ex, consts):
    xv, dres = ex
    (xh, r), = _rms_parts(xv, xv.shape[1])
    dx = dres + _rms_bwd_part(xh, r, accs[0] * consts[0])
    return [dx, dx], [_colsum8(accs[0] * xh)]


def _rot_sign():
    lane = lax.broadcasted_iota(jnp.int32, (1, ATT_DH), 1)
    return jnp.where(lane < ATT_DH // 2, -1.0, 1.0).astype(F32)


def _rope(y, cos, sin):
    return y * cos + pltpu.roll(y, ATT_DH // 2, axis=1) * _rot_sign() * sin


def _rope_t(dy, cos, sin):
    return dy * cos - pltpu.roll(dy * sin, ATT_DH // 2, axis=1) * _rot_sign()


def _qk_prep(zq, zk, zv, qn, kn, cos, sin):
    w = zq.shape[1]

    def fn(ins, consts):
        cs, sn = ins[3], ins[4]
        outs = []
        for z, gain in ((ins[0], consts[0]), (ins[1], consts[1])):
            for i, (xh, _) in enumerate(_rms_parts(z, ATT_DH)):
                outs.append(_rope(xh * gain[:, i * ATT_DH:(i + 1) * ATT_DH], cs, sn))
        outs += [ins[2][:, i * ATT_DH:(i + 1) * ATT_DH] for i in range(w // ATT_DH)]
        groups = [jnp.concatenate(outs[i:i + ATT_HEADS], axis=1) for i in range(0, len(outs), ATT_HEADS)]
        return groups, []

    outs, _ = _rowwise(fn, [(zq, w, 0), (zk, w, 0), (zv, w, 0), (cos, ATT_DH, 0), (sin, ATT_DH, 0)], [qn, kn],
                       [(ATT_GW, BF16, ATT_PATTERNS[g][1]) for g in range(ATT_GROUPS)] * 3, [], bm=256, name="qk_prep")
    return outs[0:3], outs[3:6], outs[6:9]


def _qk_prep_bwd(zq, zk, dq_g, dk_g, dv_g, qn, kn, cos, sin):
    w = zq.shape[1]

    def fn(ins, consts):
        cs, sn = ins[2], ins[3]
        outs, sums = [], []
        for z, gain, dparts in ((ins[0], consts[0], ins[4:7]), (ins[1], consts[1], ins[7:10])):
            dout = jnp.concatenate(dparts, axis=1)
            dz, dgain = [], []
            for i, (xh, r) in enumerate(_rms_parts(z, ATT_DH)):
                hs = slice(i * ATT_DH, (i + 1) * ATT_DH)
                dy = _rope_t(dout[:, hs], cs, sn)
                dgain.append(_colsum8(dy * xh))
                dz.append(_rms_bwd_part(xh, r, dy * gain[:, hs]))
            outs.append(jnp.concatenate(dz, axis=1))
            sums.append(jnp.concatenate(dgain, axis=1))
        outs.append(jnp.concatenate(ins[10:13], axis=1))
        return outs, sums

    ins = [(zq, w, 0), (zk, w, 0), (cos, ATT_DH, 0), (sin, ATT_DH, 0)]
    for parts in (dq_g, dk_g, dv_g):
        ins += [(a, ATT_GW, 0, ATT_PATTERNS[g][1]) for g, a in enumerate(parts)]
    (dzq, dzk, dzv), (dqn, dkn) = _rowwise(fn, ins, [qn, kn], [(w, BF16)] * 3, [w, w], bm=256, name="qk_prep_bwd")
    return dzq, dzk, dzv, dqn, dkn


def _post_a(o_raw, zh, gout):
    w = o_raw.shape[1]

    def fn(ins, consts):
        oh = jnp.concatenate([xh for xh, _ in _rms_parts(ins[0], HG_DK)], axis=1)
        hg = ins[1]
        return [oh * consts[0] * (hg * _sigmoid(hg))], []

    (y,), _ = _rowwise(fn, [(o_raw, w, 0), (zh, w, 3)], [gout.reshape(1, w)], [(w, BF16)], [], bm=512, name="post_a")
    return y


def _post_a_bwd(o_raw, zh, gout, dy):
    w = o_raw.shape[1]

    def fn(ins, consts):
        parts = _rms_parts(ins[0], HG_DK)
        oh = jnp.concatenate([xh for xh, _ in parts], axis=1)
        hg, dyv, gain = ins[1], ins[2], consts[0]
        sg = _sigmoid(hg)
        s = hg * sg
        doh = dyv * gain * s
        do = jnp.concatenate([_rms_bwd_part(xh, r, doh[:, i * HG_DK:(i + 1) * HG_DK]) for i, (xh, r) in enumerate(parts)], axis=1)
        dhg = dyv * oh * gain * (sg * (1.0 + hg * (1.0 - sg)))
        return [do, dhg], [_colsum8(dyv * oh * s)]

    (do, dhg), (dgain,) = _rowwise(fn, [(o_raw, w, 0), (zh, w, 3), (dy, w, 0)], [gout.reshape(1, w)],
                                   [(w, F32), (w, BF16)], [w], bm=512, name="post_a_bwd")
    return do, dhg, dgain


def _merge_alpha(lses):
    m = jnp.maximum(jnp.maximum(lses[0], lses[1]), lses[2])
    e = [jnp.exp(l - m) for l in lses]
    inv = 1.0 / (e[0] + e[1] + e[2])
    return [x * inv for x in e]


def _group_ins(parts):
    return [(a, ATT_GW, 0, ATT_PATTERNS[g][1]) for g, a in enumerate(parts)]


def _merge_b(o_g, lse_g):
    def fn(ins, consts):
        al = _merge_alpha(ins[3:6])
        return [al[0] * ins[0] + al[1] * ins[1] + al[2] * ins[2]], []

    (y,), _ = _rowwise(fn, _group_ins(o_g) + _group_ins(lse_g), [], [(ATT_GW, BF16)], [], bm=512, name="merge_b")
    return y


def _merge_b_bwd(o_g, lse_g, dy):
    def fn(ins, consts):
        al = _merge_alpha(ins[3:6])
        dyv = ins[6]
        dal = [dyv * ins[i] for i in range(3)]
        tot = al[0] * dal[0] + al[1] * dal[1] + al[2] * dal[2]
        return [al[i] * dyv for i in range(3)] + [al[i] * (dal[i] - tot) for i in range(3)], []

    outs, _ = _rowwise(fn, _group_ins(o_g) + _group_ins(lse_g) + [(dy, ATT_GW, 0)], [],
                       [(ATT_GW, F32, ATT_PATTERNS[g][1]) for g in range(ATT_GROUPS)] * 2, [], bm=512, name="merge_b_bwd")
    return outs[:3], outs[3:]


def _loss_head(y, target):
    d = y.shape[1]

    def fn(ins, consts):
        e = ins[0] - ins[1]
        return [e * (1.0 / d)] * 2, [_colsum8(e * e)]

    (dy, dyb), (sq,) = _rowwise(fn, [(y, d, 0), (target, d, 0)], [], [(d, F32), (d, BF16)], [d], bm=512, name="loss_head")
    return 0.5 * jnp.sum(sq) / d, dy, dyb


def _ffn_fwd(x, gain, wt, wo_fn, tag):
    t, d = x.shape
    f = wt.shape[0] // 2

    def act(accs, ex, consts):
        a, b = accs
        s = _sigmoid(a)
        sa = a * s
        return (sa * b, b, 0.5 * sa, 0.5 * (s + sa * (1.0 - s)))

    bn = FFN_BN if f % FFN_BN == 0 else 256
    u, b, sa, sp, h = _mm([x], [wt, wt], [(0, 0, 0), (0, 1, 1)], 2, act, [BF16] * 4, m=t, n=f, k=d, tb=True,
                          bm=512, bn=bn, bk=d, b_off=[(0, 0), (f // min(bn, f), 0)],
                          consts=[gain.reshape(1, d)], a_pro=_norm_pro, chunk=MXU_COLS, name=f"ffn_in_{tag}")
    wo = wo_fn(u)
    (y,) = _mm([u], [wo], [(0, 0, 0)], 1, lambda accs, ex: (ex[0] + 0.5 * accs[0],), [F32], m=t, n=d, k=f,
               bm=512, bn=d, bk=f, extras=[x], name=f"ffn_out_{tag}")
    return y, (x, h, u, b, sa, sp, wo)


def _ffn_bwd(dy, dyb, saved, gain, wt, tag, tok, emit):
    x, h, u, b, sa, sp, wo = saved
    t, d = x.shape
    f = wo.shape[0]

    def dact(accs, ex, consts):
        bv, sav, spv = (e.astype(F32) for e in ex)
        return (accs[0] * bv * spv, accs[0] * sav)

    bn = FFN_BN if f % FFN_BN == 0 else 256
    da, db = _mm([dyb], [wo], [(0, 0, 0)], 1, dact, [BF16, BF16], m=t, n=f, k=d, tb=True, bm=512, bn=bn, bk=d,
                 extras=[b, sa, sp], n_outer=True, chunk=MXU_COLS, consts=[tok], name=f"ffn_dact_{tag}")
    (dwo,) = _mm([u], [dyb], [(0, 0, 0)], 1, lambda accs, ex: (0.5 * accs[0],), [BF16], m=f, n=d, k=t, ta=True,
                 bm=1408, bn=d, bk=1024, name=f"ffn_dwo_{tag}")
    (dwt,) = _mm([da, db], [h], [(0, 0, 0)], 1, _first, [BF16], m=2 * f, n=d, k=t, ta=True, bm=min(1408, f), bn=d,
                 bk=1024, a_cat=True, name=f"ffn_dwt_{tag}")
    tok = emit(dwt, dwo)
    bk = min(FFN_BN, f)
    dx, dxb, dgain = _mm([da, db], [wt, wt], [(0, 0, 0), (1, 1, 0)], 1, _norm_bwd_fin, [F32, BF16], m=t, n=d, k=f,
                         bm=512, bn=d, bk=bk, b_off=[(0, 0), (0, f // bk)], extras=[x, dy],
                         consts=[gain.reshape(1, d), tok], n_sums=1, name=f"ffn_dh_{tag}")
    return dx, dxb, jnp.sum(dgain, axis=0), tok


FFN_BN = 2816
Z_SPLITS = (("h", 4096), ("q", 1536), ("k", 1536), ("v", 1536), ("g", 2048))


def _mix_fwd(x, p, cos, sin):
    t, d = x.shape
    z, off, hm = {}, 0, None
    for nm, width in Z_SPLITS:
        bn = 1024 if off % 1024 == 0 and width % 1024 == 0 else 512
        first = hm is None
        res = _mm([x if first else hm], [p["wint"]], [(0, 0, 0)], 1, (lambda accs, ex, consts: (accs[0],)) if first else _first,
                  [F32 if nm == "h" else BF16], m=t, n=width, k=d, tb=True, bm=1024, bn=bn, bk=d, b_off=[(off // bn, 0)],
                  consts=[p["gm"].reshape(1, d)] if first else (), a_pro=_norm_pro if first else None, name=f"mix_in_{nm}")
        z[nm] = res[0]
        hm = res[1] if first else hm
        off += width
    o_raw, states = _hgrn_fwd(z["h"], p["lb3"])
    qb, kb, vb = _qk_prep(z["q"], z["k"], z["v"], p["qn"], p["kn"], cos, sin)
    o_g, lse_g = zip(*[_attn_fwd(qb[g], kb[g], vb[g], g) for g in range(ATT_GROUPS)])
    oa = _post_a(o_raw, z["h"], p["gout"])
    ob = _merge_b(o_g, lse_g)
    late = p["late"](ob)
    p = dict(p, **late)
    (ya,) = _mm([oa], [p["wa"]], [(0, 0, 0)], 1, _first, [F32], m=t, n=d, k=oa.shape[1], bm=1024, bn=d, bk=oa.shape[1],
                name="branch_a")

    def gate(accs, ex):
        return (_sigmoid(ex[0].astype(F32)) * ex[2] + _sigmoid(ex[1].astype(F32)) * accs[0], accs[0])

    merged, yb = _mm([ob], [p["wbt"]], [(0, 0, 0)], 1, gate, [BF16, F32], m=t, n=d, k=ATT_GW, tb=True, bm=512, bn=d,
                     bk=ATT_GW, extras=[z["g"], z["g"], ya], e_off=[0, 1, 0], chunk=MXU_COLS, name="branch_b_gate")
    (y,) = _mm([merged], [p["wo"]], [(0, 0, 0)], 1, lambda accs, ex: (ex[0] + accs[0],), [F32], m=t, n=d, k=d,
               bm=1024, bn=d, bk=d, extras=[x], name="mix_out")
    return y, (x, hm, z, o_raw, states, qb, kb, vb, o_g, lse_g, oa, ob, ya, yb, merged, late)


def _mix_bwd(dy, dyb, saved, p, cos, sin, tok):
    x, hm, z, o_raw, states, qb, kb, vb, o_g, lse_g, oa, ob, ya, yb, merged, late = saved
    p = dict(p, **late)
    t, d = x.shape
    w = oa.shape[1]

    def dgate(accs, ex, consts):
        dm = accs[0]
        sa, sb = _sigmoid(ex[0].astype(F32)), _sigmoid(ex[1].astype(F32))
        return (sa * dm, sb * dm, dm * ex[2] * sa * (1.0 - sa), dm * ex[3] * sb * (1.0 - sb))

    dya, dyb_, dga, dgb = _mm([dyb], [p["wo"]], [(0, 0, 0)], 1, dgate, [BF16] * 4, m=t, n=d, k=d, tb=True, bm=512, bn=d,
                              bk=d, extras=[z["g"], z["g"], ya, yb], e_off=[0, 1, 0, 0], chunk=MXU_COLS, consts=[tok], name="mix_out_bwd")
    (dwo,) = _mm([merged], [dyb], [(0, 0, 0)], 1, _first, [BF16], m=d, n=d, k=t, ta=True, bm=d, bn=d, bk=1024, name="mix_dwo")
    (doa,) = _mm([dya], [p["wa"]], [(0, 0, 0)], 1, _first, [F32], m=t, n=w, k=d, tb=True, bm=1024, bn=w, bk=d, name="branch_a_bwd")
    (dwa,) = _mm([oa], [dya], [(0, 0, 0)], 1, _first, [BF16], m=w, n=d, k=t, ta=True, bm=w, bn=d, bk=1024, name="branch_a_dw")
    (dob,) = _mm([dyb_], [p["wbt"]], [(0, 0, 0)], 1, _first, [F32], m=t, n=ATT_GW, k=d, bm=1024, bn=ATT_GW, bk=d,
                 name="branch_b_bwd")
    (dwbt,) = _mm([dyb_], [ob], [(0, 0, 0)], 1, _first, [BF16], m=d, n=ATT_GW, k=t, ta=True, bm=d, bn=ATT_GW, bk=1024,
                  name="branch_b_dw")
    do_raw, dhg, dgout = _post_a_bwd(o_raw, z["h"], p["gout"], doa)
    do_g, dlse_g = _merge_b_bwd(o_g, lse_g, dob)
    dq_g, dk_g, dv_g = zip(*[_attn_bwd(qb[g], kb[g], vb[g], o_g[g], lse_g[g], do_g[g], dlse_g[g], g)
                             for g in range(ATT_GROUPS)])
    dzq, dzk, dzv, dqn, dkn = _qk_prep_bwd(z["q"], z["k"], dq_g, dk_g, dv_g, p["qn"], p["kn"], cos, sin)
    dhq, dhf, dhi, lbsum = _hgrn_bwd(z["h"], p["lb3"], states, do_raw)
    dz = jnp.concatenate([dhq, dhf, dhi, dhg, dzq, dzk, dzv, dga, dgb], axis=1)
    pw = dz.shape[1]
    (dwint,) = _mm([dz], [hm], [(0, 0, 0)], 1, _first, [BF16], m=pw, n=d, k=t, ta=True, bm=1536, bn=d, bk=1024, name="mix_in_dw")
    dx, dxb, dgm = _mm([dz], [p["wint"]], [(0, 0, 0)], 1, _norm_bwd_fin, [F32, BF16], m=t, n=d, k=pw, bm=1024, bn=d, bk=1536,
                       extras=[x, dy], consts=[p["gm"].reshape(1, d)], n_sums=1, name="mix_in_bwd")
    return dx, dxb, dict(gm=jnp.sum(dgm, axis=0), wint=dwint, lbsum=lbsum, gout=dgout, qn=dqn, kn=dkn, wa=dwa, wbt=dwbt, wo=dwo)


def _rope_tables(t):
    pos = jnp.arange(t, dtype=F32)
    inv = ROPE_THETA ** (-jnp.arange(0, ATT_DH, 2, dtype=F32) / ATT_DH)
    ang = pos[:, None] * inv[None, :]
    ang = jnp.concatenate([ang, ang], axis=-1)
    return jnp.cos(ang), jnp.sin(ang)


def _lower_bounds(logits):
    lb = jnp.cumsum(jax.nn.softmax(logits, axis=0), axis=0)
    return lb - lb[0:1]


def _head_gain(g):
    return jnp.tile(g[:, None, :], (1, ATT_HEADS, 1)).reshape(1, ATT_GROUPS * ATT_GW)


SMALL_GRADS = ("ffn1_norm", "mix_norm", "lbsum", "hgrn_out_norm", "attn_q_norm", "attn_k_norm", "ffn2_norm")


def _local_step(x, target, small, fetch, emit):
    t = x.shape[0]
    depth = small["ffn1_norm"].shape[0]
    cos, sin = _rope_tables(t)
    lb_all = _lower_bounds(small["hgrn_lb_logits"])
    saved = []
    for l in range(depth):
        w1t = fetch("w1t", l, x)["w1t"]
        x, s1 = _ffn_fwd(x, small["ffn1_norm"][l], w1t, lambda after, l=l: fetch("w1o", l, after)["w1o"], "1")
        p = dict(gm=small["mix_norm"][l], wint=fetch("wint", l, x)["wint"], lb3=lb_all[l].reshape(-1, 1, HG_DK),
                 gout=small["hgrn_out_norm"][l], qn=_head_gain(small["attn_q_norm"][l]),
                 kn=_head_gain(small["attn_k_norm"][l]), late=functools.partial(fetch, "mout", l))
        x, sm = _mix_fwd(x, p, cos, sin)
        w2t = fetch("w2t", l, x)["w2t"]
        x, s2 = _ffn_fwd(x, small["ffn2_norm"][l], w2t, lambda after, l=l: fetch("w2o", l, after)["w2o"], "2")
        saved.append((p, w1t, w2t, s1, sm, s2))
    loss, dx, dxb = _loss_head(x, target)
    gsmall = {k: [None] * depth for k in SMALL_GRADS}
    tok = jnp.zeros((8, 128), F32)
    for l in reversed(range(depth)):
        p, w1t, w2t, s1, sm, s2 = saved[l]
        dx, dxb, gsmall["ffn2_norm"][l], tok = _ffn_bwd(
            dx, dxb, s2, small["ffn2_norm"][l], w2t, "2", tok, lambda dwt, dwo, l=l: emit("ffn2", l, dict(w2t=dwt, w2o=dwo), None))
        dx, dxb, gm = _mix_bwd(dx, dxb, sm, p, cos, sin, tok)
        tok = emit("mix", l, {k: gm[k] for k in ("wint", "wa", "wbt", "wo")}, None)
        gsmall["mix_norm"][l], gsmall["lbsum"][l], gsmall["hgrn_out_norm"][l] = gm["gm"], gm["lbsum"], gm["gout"]
        for k, src in (("attn_q_norm", "qn"), ("attn_k_norm", "kn")):
            gsmall[k][l] = jnp.sum(gm[src].reshape(ATT_GROUPS, ATT_HEADS, ATT_DH), axis=1)
        dx, dxb, gsmall["ffn1_norm"][l], tok = _ffn_bwd(
            dx, dxb, s1, small["ffn1_norm"][l], w1t, "1", tok, lambda dwt, dwo, l=l: emit("ffn1", l, dict(w1t=dwt, w1o=dwo), None))
    emit("small", 0, {}, ({k: jnp.stack(v) for k, v in gsmall.items()}, loss))
    return dx


_HBM = pl.BlockSpec(memory_space=pltpu.HBM)
_SEM = pl.BlockSpec(memory_space=pltpu.SEMAPHORE)
_EFFECT = pltpu.SideEffectType.DATAFLOW_SIDE_EFFECTING


def _peer(p):
    x, y, c = lax.axis_index("x"), lax.axis_index("y"), lax.axis_index("c")
    me = 4 * x + 2 * y + c
    return (1 - x if p & 4 else x, 1 - y if p & 2 else y, 1 - c if p & 1 else c), jnp.bitwise_xor(me, p), me


def _xchg_copy(src, land, mode, send_sems, recv_sems, k, p, arriving):
    peer, peer_id, me = _peer(p)
    block = src if mode == "gather" else src.at[peer_id]
    return pltpu.make_async_remote_copy(
        src_ref=block, dst_ref=land.at[peer_id if arriving else me], send_sem=send_sems.at[k * (N_DEV - 1) + p - 1],
        recv_sem=recv_sems.at[k * (N_DEV - 1) + p - 1], device_id=peer, device_id_type=MESH)


def _xchg_start(srcs, modes, groups, name):
    n, ng = len(srcs), len(groups)

    def body(*refs):
        src = refs[:n]
        sems = refs[n:n + 2 * ng]
        land = refs[n + 2 * ng + n:n + 2 * ng + 2 * n]
        token = refs[n + 2 * ng + 2 * n]
        for gi, idx in enumerate(groups):
            for ki, k in enumerate(idx):
                for p in range(1, N_DEV):
                    _xchg_copy(src[k], land[k], modes[k], sems[2 * gi], sems[2 * gi + 1], ki, p, False).start()
        token[...] = jnp.zeros_like(token)

    sem_shapes = []
    for idx in groups:
        sem_shapes += [pltpu.SemaphoreType.DMA((len(idx) * (N_DEV - 1),))] * 2
    outs = pl.pallas_call(
        body,
        out_shape=sem_shapes + [pltpu.HBM(a.shape, a.dtype) for a in srcs]
        + [pltpu.HBM((N_DEV,) + a.shape[-2:], a.dtype) for a in srcs] + [jax.ShapeDtypeStruct((8, 128), F32)],
        in_specs=[_HBM] * n,
        out_specs=[_SEM] * (2 * ng) + [_HBM] * (2 * n) + [pl.BlockSpec(memory_space=pltpu.VMEM)],
        input_output_aliases={i: 2 * ng + i for i in range(n)},
        compiler_params=pltpu.CompilerParams(has_side_effects=_EFFECT),
        name=name,
    )(*[pltpu.with_memory_space_constraint(a, pltpu.HBM) for a in srcs])
    sems = [(outs[2 * gi], outs[2 * gi + 1]) for gi in range(ng)]
    return sems, outs[2 * ng:2 * ng + n], outs[2 * ng + n:2 * ng + 2 * n], outs[-1]


def _xchg_wait_call(srcs, lands, modes, sems, after, name):
    n = len(srcs)

    def body(*refs):
        src, land = refs[:n], refs[n:2 * n]
        send_sems, recv_sems = refs[2 * n], refs[2 * n + 1]
        for p in range(1, N_DEV):
            for k in range(n):
                cp = _xchg_copy(src[k], land[k], modes[k], send_sems, recv_sems, k, p, True)
                cp.wait_send()
                cp.wait_recv()

    outs = pl.pallas_call(
        body,
        out_shape=[pltpu.HBM(a.shape, a.dtype) for a in list(srcs) + list(lands)],
        in_specs=[_HBM] * (2 * n) + [_SEM, _SEM, pl.BlockSpec(memory_space=pl.ANY)],
        out_specs=[_HBM] * (2 * n),
        input_output_aliases={i: i for i in range(2 * n)},
        compiler_params=pltpu.CompilerParams(has_side_effects=_EFFECT),
        name=name,
    )(*srcs, *lands, sems[0], sems[1], after)
    return outs[:n], outs[n:]


def _xchg_wait(srcs, lands, modes, sems, after, name):
    srcs, lands = _xchg_wait_call(srcs, lands, modes, sems, after, name)
    me = 4 * lax.axis_index("x") + 2 * lax.axis_index("y") + lax.axis_index("c")
    done = []
    for a, land, mode in zip(srcs, lands, modes):
        own = a[None] if mode == "gather" else lax.dynamic_slice_in_dim(a, me, 1, axis=0)
        done.append(lax.dynamic_update_slice(land, own, (me, 0, 0)))
    return done


def _sum_slots(land):
    g, _, r, c = land.shape
    br = r // 2 if (r % 32 == 0 and r >= 256) else r

    def body(l_ref, o_ref):
        acc = l_ref[0, 0].astype(F32)
        for j in range(1, N_DEV):
            acc = acc + l_ref[0, j].astype(F32)
        o_ref[0] = acc

    return pl.pallas_call(
        body,
        out_shape=jax.ShapeDtypeStruct((g, r, c), F32),
        grid=(g, r // br),
        in_specs=[pl.BlockSpec((1, N_DEV, br, c), lambda i, j: (i, 0, j, 0))],
        out_specs=pl.BlockSpec((1, br, c), lambda i, j: (i, j, 0)),
        compiler_params=_cparams(("parallel", "parallel")),
        name="sum_slots",
    )(land)


def _adamw(w, g, m, v):
    shape = w.shape
    cols = shape[-1]
    rows = int(np.prod(shape[:-1]))
    bm = max(b for b in range(8, 513, 8) if rows % b == 0) if rows % 8 == 0 else rows
    c1 = 1.0 - ADAM_B1 ** ADAM_STEP
    c2 = 1.0 - ADAM_B2 ** ADAM_STEP

    def fn(ins, consts):
        wv, gv, mv, vv = ins
        m2 = ADAM_B1 * mv + (1.0 - ADAM_B1) * gv
        v2 = ADAM_B2 * vv + (1.0 - ADAM_B2) * (gv * gv)
        delta = -ADAM_LR * ((m2 / c1) / (jnp.sqrt(v2 / c2) + ADAM_EPS) + ADAM_WD * wv)
        return [delta, m2, v2], []

    outs, _ = _rowwise(fn, [(a.reshape(rows, cols), cols, 0) for a in (w, g, m, v)], [], [(cols, F32)] * 3, [],
                       bm=bm, name="adamw")
    return [o.reshape(shape) for o in outs]


BIG = ("w1t", "w1o", "wint", "wa", "wbt", "wo", "w2t", "w2o")
FETCH_GROUPS = dict(w1t=("w1t",), w1o=("w1o",), wint=("wint",), mout=("wa", "wbt", "wo"), w2t=("w2t",), w2o=("w2o",))
SMALL_ROWS = (("ffn1_norm", 0), ("mix_norm", 2), ("lbsum", 4), ("hgrn_out_norm", 6), ("ffn2_norm", 8),
              ("attn_q_norm", 10), ("attn_k_norm", 12))
SMALL_PACK_ROWS = 16


def kernel(x, ffn1_norm, ffn1_w_in, ffn1_w_out, mix_norm, w_in, hgrn_lb_logits, hgrn_out_norm, attn_q_norm, attn_k_norm, w_branch_a, w_branch_b, w_out, ffn2_norm, ffn2_w_in, ffn2_w_out, loss_target, m_ffn1_norm, m_ffn1_w_in, m_ffn1_w_out, m_mix_norm, m_w_in, m_hgrn_lb_logits, m_hgrn_out_norm, m_attn_q_norm, m_attn_k_norm, m_w_branch_a, m_w_branch_b, m_w_out, m_ffn2_norm, m_ffn2_w_in, m_ffn2_w_out, v_ffn1_norm, v_ffn1_w_in, v_ffn1_w_out, v_mix_norm, v_w_in, v_hgrn_lb_logits, v_hgrn_out_norm, v_attn_q_norm, v_attn_k_norm, v_w_branch_a, v_w_branch_b, v_w_out, v_ffn2_norm, v_ffn2_w_in, v_ffn2_w_out):
    names = ("ffn1_norm", "ffn1_w_in", "ffn1_w_out", "mix_norm", "w_in", "hgrn_lb_logits", "hgrn_out_norm", "attn_q_norm",
             "attn_k_norm", "w_branch_a", "w_branch_b", "w_out", "ffn2_norm", "ffn2_w_in", "ffn2_w_out")
    w = dict(zip(names, (ffn1_norm, ffn1_w_in, ffn1_w_out, mix_norm, w_in, hgrn_lb_logits, hgrn_out_norm, attn_q_norm,
                         attn_k_norm, w_branch_a, w_branch_b, w_out, ffn2_norm, ffn2_w_in, ffn2_w_out)))
    m = dict(zip(names, (m_ffn1_norm, m_ffn1_w_in, m_ffn1_w_out, m_mix_norm, m_w_in, m_hgrn_lb_logits, m_hgrn_out_norm,
                         m_attn_q_norm, m_attn_k_norm, m_w_branch_a, m_w_branch_b, m_w_out, m_ffn2_norm, m_ffn2_w_in, m_ffn2_w_out)))
    v = dict(zip(names, (v_ffn1_norm, v_ffn1_w_in, v_ffn1_w_out, v_mix_norm, v_w_in, v_hgrn_lb_logits, v_hgrn_out_norm,
                         v_attn_q_norm, v_attn_k_norm, v_w_branch_a, v_w_branch_b, v_w_out, v_ffn2_norm, v_ffn2_w_in, v_ffn2_w_out)))
    depth, d = ffn1_norm.shape

    def tr(a):
        return jnp.swapaxes(a, 1, 2)

    shard = dict(w1t=tr(ffn1_w_in), w1o=ffn1_w_out, wint=tr(w_in), wa=w_branch_a,
                 wbt=tr(w_branch_b).reshape(depth, -1, d), wo=w_out, w2t=tr(ffn2_w_in), w2o=ffn2_w_out)
    order = [(g, l) for l in range(depth) for g in FETCH_GROUPS]
    started = {}
    for name, part in (("gather_start", order),):
        flat = [(l, k) for g, l in part for k in FETCH_GROUPS[g]]
        groups, pos = [], 0
        for g, l in part:
            groups.append(list(range(pos, pos + len(FETCH_GROUPS[g]))))
            pos += len(FETCH_GROUPS[g])
        sems, srcs, lands, _ = _xchg_start([shard[k][l].astype(BF16) for l, k in flat], ["gather"] * len(flat), groups, name)
        for gi, key in enumerate(part):
            started[key] = ([srcs[i] for i in groups[gi]], [lands[i] for i in groups[gi]], sems[gi])

    def fetch(group, l, after):
        srcs, lands, sems = started[group, l]
        lands = _xchg_wait(srcs, lands, ["gather"] * len(srcs), sems, after, f"gather_wait_{group}{l}")
        out = {}
        for k, land in zip(FETCH_GROUPS[group], lands):
            out[k] = land.reshape(d, -1) if k == "wbt" else land.reshape(-1, d)
        return out

    pending = []

    def emit(group, l, g, final):
        keys = list(g)
        srcs = [g[k].reshape(N_DEV, -1, d) for k in keys]
        modes = ["scatter"] * len(keys)
        if final is not None:
            gsmall, loss = final
            pack = jnp.zeros((SMALL_PACK_ROWS, d), F32)
            for k, r0 in SMALL_ROWS:
                rows = gsmall[k].reshape(depth, -1)
                pack = pack.at[r0:r0 + depth, :rows.shape[1]].set(rows)
            srcs.append(pack.at[14, :].set(loss))
            modes.append("gather")
            keys.append("small")
        sems, s_thru, l_thru, token = _xchg_start(srcs, modes, [list(range(len(srcs)))], f"grads_start_{group}{l}")
        pending.append((group, l, keys, modes, sems[0], s_thru, l_thru))
        return token

    small = {k: w[k] for k in ("ffn1_norm", "mix_norm", "hgrn_lb_logits", "hgrn_out_norm", "attn_q_norm", "attn_k_norm", "ffn2_norm")}
    dx = _local_step(x[0], loss_target[0], small, fetch, emit)

    summed, after = {}, dx
    for group, l, keys, modes, sems, s_thru, l_thru in pending:
        lands = _xchg_wait(s_thru, l_thru, modes, sems, after, f"grads_wait_{group}{l}")
        for k, land in zip(keys, lands):
            summed[k, l] = _sum_slots(land[None])[0]
        after = summed[keys[-1], l]
    gsum = {k: jnp.stack([summed[k, l] for l in range(depth)]) for k in BIG}
    tot = summed["small", 0]

    grads = {}
    for k, r0 in SMALL_ROWS:
        shp = (depth,) + (w[k].shape[1:] if k != "lbsum" else (d,))
        grads[k] = tot[r0:r0 + depth, :int(np.prod(shp[1:]))].reshape(shp)
    _, lb_vjp = jax.vjp(_lower_bounds, hgrn_lb_logits)
    grads["hgrn_lb_logits"] = lb_vjp(grads.pop("lbsum"))[0]
    grads["ffn1_w_in"], grads["ffn1_w_out"] = tr(gsum["w1t"]), gsum["w1o"]
    grads["w_in"], grads["w_branch_a"] = tr(gsum["wint"]), gsum["wa"]
    grads["w_branch_b"] = tr(gsum["wbt"].reshape(depth, d // N_DEV, -1))
    grads["w_out"] = gsum["wo"]
    grads["ffn2_w_in"], grads["ffn2_w_out"] = tr(gsum["w2t"]), gsum["w2o"]

    upd = {k: _adamw(w[k], grads[k], m[k], v[k]) for k in names}
    return (tot[14, 0], dx[None], *[grads[k] for k in names], *[upd[k][0] for k in names],
            *[upd[k][1] for k in names], *[upd[k][2] for k in names])
```

```python
import functools

import jax
import jax.numpy as jnp
import numpy as np
from jax import lax
from jax.experimental import pallas as pl
from jax.experimental.pallas import tpu as pltpu

F32 = jnp.float32
BF16 = jnp.bfloat16

N_DEV = 8
EPS = 1e-6
HG_DK = 128
HG_CHUNK = 64
HG_SUB = 16
HG_HP = 8
ATT_PATTERNS = ((128, 1), (512, 4), (2048, 16))
ATT_GROUPS = 3
ATT_HEADS = 4
ATT_DH = 128
ATT_BLK = 128
ROPE_THETA = 10000.0
ADAM_LR, ADAM_B1, ADAM_B2, ADAM_EPS, ADAM_WD, ADAM_STEP = 0.001, 0.9, 0.999, 1e-08, 0.01, 10
VMEM_LIMIT_BYTES = 56 * 1024 * 1024
MXU_COLS = 256
MESH = pl.DeviceIdType.MESH


def _cparams(sem, **kw):
    return pltpu.CompilerParams(dimension_semantics=sem, vmem_limit_bytes=VMEM_LIMIT_BYTES, **kw)


def _sigmoid(x):
    return 1.0 / (1.0 + jnp.exp(-x))


def _mm(a_list, b_list, pairs, n_acc, fin, out_dtypes, *, m, n, k, ta=False, tb=False, bm, bn, bk,
        b_off=None, extras=(), e_off=None, n_outer=False, consts=(), a_pro=None, n_sums=0, chunk=0, a_cat=False, name):
    bm, bn, bk = min(bm, m), min(bn, n), min(bk, k)
    assert m % bm == 0 and n % bn == 0 and k % bk == 0, (name, m, n, k, bm, bn, bk)
    nk = k // bk
    assert not (a_pro and (nk > 1 or ta or n_outer)) and not (n_sums and (bn != n or n_outer)), name
    assert not (chunk and (nk > 1 or n_sums or chunk % 128)), name
    if a_cat:
        unit = bm if ta else bk
        widths = [a.shape[1] for a in a_list]
        assert all(w % unit == 0 for w in widths) and sum(widths) == (m if ta else k) and not a_pro, name
        cat_counts = [w // unit for w in widths]
        cat_starts = [sum(cat_counts[:i]) for i in range(len(widths))]
    b_off = b_off or [(0, 0)] * len(b_list)
    e_off = e_off or [0] * len(extras)
    na, nb, ne, nc, no = len(a_list), len(b_list), len(extras), len(consts), len(out_dtypes)
    nao = na if a_pro else 0
    dn = (((0,) if ta else (1,), (1,) if tb else (0,)), ((), ()))

    def body(*refs):
        refs = list(refs)
        a_refs, b_refs, e_refs, c_refs, o_refs, ao_refs, s_refs = (
            [refs.pop(0) for _ in range(cnt)] for cnt in (na, nb, ne, nc, no, nao, n_sums))
        acc_refs = refs
        kk = pl.program_id(2)
        first = pl.program_id(0) == 0
        cvals = [c[...] for c in c_refs]
        a_vals = [r[...] for r in a_refs]
        if a_cat:
            col = pl.program_id(1 if n_outer else 0) if ta else kk
            sel = a_vals[0]
            for start, v in zip(cat_starts[1:], a_vals[1:]):
                sel = jnp.where(col >= start, v, sel)
            a_vals = [sel]
        if a_pro:
            @pl.when(pl.program_id(1) == 0)
            def _():
                for r, v in zip(ao_refs, a_pro(a_vals, cvals)):
                    r[...] = v

            a_vals = [r[...] for r in ao_refs]
        if chunk:
            spans = [slice(lo, min(lo + chunk, bn)) for lo in range(0, bn, chunk)]
            chunks = []
            for cs in spans:
                parts = [None] * n_acc
                for ai, bi, ci in pairs:
                    p = lax.dot_general(a_vals[ai], b_refs[bi][cs, :] if tb else b_refs[bi][:, cs], dn,
                                        preferred_element_type=F32)
                    parts[ci] = p if parts[ci] is None else parts[ci] + p
                chunks.append(parts)
            for cs, parts in zip(spans, chunks):
                ex = [e[:, cs] for e in e_refs]
                outs = fin(parts, ex, cvals) if nc else fin(parts, ex)
                for o_ref, o in zip(o_refs, outs):
                    o_ref[:, cs] = o.astype(o_ref.dtype)
            return

        parts = [None] * n_acc
        for ai, bi, ci in pairs:
            p = lax.dot_general(a_vals[ai], b_refs[bi][...], dn, preferred_element_type=F32)
            parts[ci] = p if parts[ci] is None else parts[ci] + p

        def finish(accs):
            ex = [e[...] for e in e_refs]
            res = fin(accs, ex, cvals) if nc else fin(accs, ex)
            outs, sums = res if n_sums else (res, ())
            for o_ref, o in zip(o_refs, outs):
                o_ref[...] = o.astype(o_ref.dtype)
            if n_sums:
                @pl.when(first)
                def _():
                    for s_ref, s in zip(s_refs, sums):
                        s_ref[...] = s

                @pl.when(jnp.logical_not(first))
                def _():
                    for s_ref, s in zip(s_refs, sums):
                        s_ref[...] += s

        if nk == 1:
            finish(parts)
        else:
            @pl.when(kk == 0)
            def _():
                for c in range(n_acc):
                    acc_refs[c][...] = parts[c]

            @pl.when(kk > 0)
            def _():
                for c in range(n_acc):
                    acc_refs[c][...] += parts[c]

            @pl.when(kk == nk - 1)
            def _():
                finish([acc_refs[c][...] for c in range(n_acc)])

    def ij(f):
        return (lambda j, i, q: f(i, j, q)) if n_outer else f

    a_spec = pl.BlockSpec((bk, bm), ij(lambda i, j, q: (q, i))) if ta else pl.BlockSpec((bm, bk), ij(lambda i, j, q: (i, q)))
    a_specs = [a_spec] * na
    if a_cat:
        def part_spec(start, count):
            def col(c):
                return jnp.clip(c - start, 0, count - 1)
            if ta:
                return pl.BlockSpec((bk, bm), ij(lambda i, j, q: (q, col(i))))
            return pl.BlockSpec((bm, bk), ij(lambda i, j, q: (i, col(q))))
        a_specs = [part_spec(s, c) for s, c in zip(cat_starts, cat_counts)]

    b_mode = dict(pipeline_mode=pl.Buffered(1)) if (bn == n and nk == 1) else {}

    def b_spec(off):
        on, ok = off
        if tb:
            return pl.BlockSpec((bn, bk), ij(lambda i, j, q: (j + on, q + ok)), **b_mode)
        return pl.BlockSpec((bk, bn), ij(lambda i, j, q: (q + ok, j + on)), **b_mode)

    mn_spec = pl.BlockSpec((bm, bn), ij(lambda i, j, q: (i, j)))
    outs = pl.pallas_call(
        body,
        out_shape=[jax.ShapeDtypeStruct((m, n), d) for d in out_dtypes] + [jax.ShapeDtypeStruct((m, k), BF16)] * nao
        + [jax.ShapeDtypeStruct((8, n), F32)] * n_sums,
        grid=(n // bn, m // bm, nk) if n_outer else (m // bm, n // bn, nk),
        in_specs=a_specs + [b_spec(o) for o in b_off]
        + [pl.BlockSpec((bm, bn), ij(lambda i, j, q, o=o: (i, j + o))) for o in e_off]
        + [pl.BlockSpec(c.shape, lambda *_, nd=c.ndim: (0,) * nd) for c in consts],
        out_specs=[mn_spec] * no + [a_spec] * nao + [pl.BlockSpec((8, n), lambda *_: (0, 0))] * n_sums,
        scratch_shapes=[pltpu.VMEM((bm, bn), F32) for _ in range(n_acc if nk > 1 else 0)],
        compiler_params=_cparams(("arbitrary" if n_sums else "parallel", "parallel", "arbitrary")),
        name=name,
    )(*a_list, *b_list, *extras, *consts)
    return outs


def _first(accs, ex):
    return (accs[0],)


def _rowwise(fn, ins, consts, out_defs, sum_widths, *, bm, name):
    ins = [tuple(e) + (1,) * (4 - len(e)) for e in ins]
    out_defs = [tuple(e) + (1,) * (3 - len(e)) for e in out_defs]
    t = ins[0][0].shape[-2] * ins[0][3]
    bm = min(bm, t)
    assert t % bm == 0, (name, t, bm)
    ni, nc, no, ns = len(ins), len(consts), len(out_defs), len(sum_widths)
    strided = [w for _, w, _, d in ins if d > 1] + [w for w, _, d in out_defs if d > 1]

    def body(*refs):
        i_refs, c_refs = refs[:ni], refs[ni:ni + nc]
        o_refs, s_refs = refs[ni + nc:ni + nc + no], refs[ni + nc + no:ni + nc + no + ns]
        scratch = list(refs[ni + nc + no + ns:])
        vals = []
        for ref, (_, w, _, d) in zip(i_refs, ins):
            if d == 1:
                vals.append(ref[...])
                continue
            s = scratch.pop(0)
            for r in range(d):
                for c in range(w // 128):
                    s.at[c][pl.ds(r, bm // d, stride=d), :] = ref[r, :, c * 128:(c + 1) * 128].astype(F32)
            vals.append(jnp.concatenate([s[c] for c in range(w // 128)], axis=1))
        outs, sums = fn(vals, [r[...] for r in c_refs])
        for o_ref, o, (w, _, d) in zip(o_refs, outs, out_defs):
            if d == 1:
                o_ref[...] = o.astype(o_ref.dtype)
                continue
            s = scratch.pop(0)
            for c in range(w // 128):
                s[c] = o[:, c * 128:(c + 1) * 128].astype(F32)
            for r in range(d):
                for c in range(w // 128):
                    o_ref[r, :, c * 128:(c + 1) * 128] = s.at[c][pl.ds(r, bm // d, stride=d), :].astype(o_ref.dtype)
        if ns:
            first = pl.program_id(0) == 0

            @pl.when(first)
            def _():
                for s_ref, s in zip(s_refs, sums):
                    s_ref[...] = s

            @pl.when(jnp.logical_not(first))
            def _():
                for s_ref, s in zip(s_refs, sums):
                    s_ref[...] += s

    def win(width, cb, d):
        if d > 1:
            return pl.BlockSpec((d, bm // d, width), lambda i: (0, i, 0))
        return pl.BlockSpec((bm, width), lambda i: (i, cb))

    res = pl.pallas_call(
        body,
        out_shape=[jax.ShapeDtypeStruct((t, w) if d == 1 else (d, t // d, w), dt) for w, dt, d in out_defs]
        + [jax.ShapeDtypeStruct((8, w), F32) for w in sum_widths],
        grid=(t // bm,),
        in_specs=[win(w, cb, d) for _, w, cb, d in ins] + [pl.BlockSpec(c.shape, lambda i, nd=c.ndim: (0,) * nd) for c in consts],
        out_specs=[win(w, 0, d) for w, _, d in out_defs] + [pl.BlockSpec((8, w), lambda i: (0, 0)) for w in sum_widths],
        scratch_shapes=[pltpu.VMEM((w // 128, bm, 128), F32) for w in strided],
        compiler_params=_cparams(("arbitrary",) if ns else ("parallel",)),
        name=name,
    )(*[e[0] for e in ins], *consts)
    return res[:no], [jnp.sum(s, axis=0) for s in res[no:]]


def _colsum8(x):
    bm, w = x.shape
    return jnp.sum(x.reshape(bm // 8, 8, w), axis=0)


def _tri(n, upper=False):
    r = lax.broadcasted_iota(jnp.int32, (n, n), 0)
    c = lax.broadcasted_iota(jnp.int32, (n, n), 1)
    return (c >= r) if upper else (c <= r)


def _exact_tri_matmul(tri_bf16, x):
    x0 = x.astype(BF16)
    r1 = x - x0.astype(F32)
    x1 = r1.astype(BF16)
    x2 = (r1 - x1.astype(F32)).astype(BF16)
    w = x.shape[1]
    y = jnp.dot(tri_bf16, jnp.concatenate([x0, x1, x2], axis=1), preferred_element_type=F32)
    return y[:, :w] + y[:, w:2 * w] + y[:, 2 * w:]


def _dot_nt(a, b):
    return lax.dot_general(a, b, (((1,), (1,)), ((), ())), preferred_element_type=F32)


def _dot_tn(a, b):
    return lax.dot_general(a, b, (((0,), (0,)), ((), ())), preferred_element_type=F32)


def _dot(a, b):
    return jnp.dot(a, b, preferred_element_type=F32)


def _hg_gates(hq, hf, lb):
    sq = _sigmoid(hq)
    q = hq * sq
    sg = _sigmoid(hf)
    f = lb + (1.0 - lb) * sg
    return q, sq, sg, f


def _hg_heads(x, hp):
    return [x[:, h * HG_DK:(h + 1) * HG_DK] for h in range(hp)]


def _hg_intra_wide(q, kk, g, hp):
    c = q.shape[0]
    rows = lax.broadcasted_iota(jnp.int32, (c, 1), 0)
    a_rows = [[] for _ in range(hp)]
    qts, kts, eqs, eks = [], [], [], []
    for i in range(c // HG_SUB):
        lo = i * HG_SUB
        ref = g[lo - 1:lo, :] if i else jnp.zeros_like(g[0:1, :])
        eq = jnp.exp(g[lo:lo + HG_SUB, :] - ref)
        ek = jnp.exp(jnp.where(rows < lo + HG_SUB, ref - g, 0.0))
        qtb = (q[lo:lo + HG_SUB, :] * eq).astype(BF16)
        ktb = (kk * ek).astype(BF16)
        tpos = lo + lax.broadcasted_iota(jnp.int32, (HG_SUB, c), 0)
        spos = lax.broadcasted_iota(jnp.int32, (HG_SUB, c), 1)
        for h, (qh, kh) in enumerate(zip(_hg_heads(qtb, hp), _hg_heads(ktb, hp))):
            a_rows[h].append(jnp.where(spos <= tpos, _dot_nt(qh, kh), 0.0))
        qts.append(qtb), kts.append(ktb), eqs.append(eq), eks.append(ek)
    return [jnp.concatenate(r, axis=0) for r in a_rows], qts, kts, eqs, eks


def _hgrn_fwd(zh, lb3, *, tb=512):
    t = zh.shape[0]
    nh = lb3.shape[0]
    c = HG_CHUNK
    tb = min(tb, t)
    nchunk = tb // c
    hp = HG_HP if nh % HG_HP == 0 else 1
    wp = hp * HG_DK

    def body(hq_ref, hf_ref, hi_ref, lb_ref, o_ref, st_ref, state):
        @pl.when(pl.program_id(1) == 0)
        def _():
            state[...] = jnp.zeros_like(state)

        tril = _tri(c).astype(BF16)

        def chunk(ci, carry):
            sl = pl.ds(pl.multiple_of(ci * c, c), c)
            q, _, _, f = _hg_gates(hq_ref[sl, :], hf_ref[sl, :], lb_ref[...])
            kk = 1.0 - f
            g = _exact_tri_matmul(tril, jnp.log(f))
            a, _, _, _, _ = _hg_intra_wide(q, kk, g, hp)
            vb = hi_ref[sl, :].astype(BF16)
            glast = g[c - 1:c, :]
            qgb = (q * jnp.exp(g)).astype(BF16)
            kgb = (kk * jnp.exp(glast - g)).astype(BF16)
            dec = jnp.exp(glast)
            sts = [state[h] for h in range(hp)]
            for h in range(hp):
                st_ref[h, ci] = sts[h]
            vh, qgh, kgh, dech = _hg_heads(vb, hp), _hg_heads(qgb, hp), _hg_heads(kgb, hp), _hg_heads(dec, hp)
            o = [_dot(a[h].astype(BF16), vh[h]) + _dot_nt(qgh[h], sts[h].astype(BF16)) for h in range(hp)]
            new = [_dot_tn(vh[h], kgh[h]) for h in range(hp)]
            o_ref[sl, :] = jnp.concatenate(o, axis=1)
            for h in range(hp):
                state[h] = sts[h] * dech[h] + new[h]
            return carry

        lax.fori_loop(0, nchunk, chunk, 0)

    def col(cb):
        return pl.BlockSpec((tb, wp), lambda h, i: (i, cb * (nh // hp) + h))

    return pl.pallas_call(
        body,
        out_shape=[jax.ShapeDtypeStruct((t, nh * HG_DK), F32), jax.ShapeDtypeStruct((nh, t // c, HG_DK, HG_DK), F32)],
        grid=(nh // hp, t // tb),
        in_specs=[col(0), col(1), col(2), pl.BlockSpec((1, wp), lambda h, i: (0, h))],
        out_specs=[pl.BlockSpec((tb, wp), lambda h, i: (i, h)),
                   pl.BlockSpec((hp, nchunk, HG_DK, HG_DK), lambda h, i: (h, i, 0, 0))],
        scratch_shapes=[pltpu.VMEM((hp, HG_DK, HG_DK), F32)],
        compiler_params=_cparams(("parallel", "arbitrary")),
        name="hgrn_fwd",
    )(zh, zh, zh, lb3.reshape(1, -1))


def _hgrn_bwd(zh, lb3, states, d_o, *, tb=512):
    t = zh.shape[0]
    nh = lb3.shape[0]
    c = HG_CHUNK
    tb = min(tb, t)
    nchunk = tb // c
    nblk = t // tb
    hp = HG_HP if nh % HG_HP == 0 else 1
    wp = hp * HG_DK

    def body(hq_ref, hf_ref, hi_ref, lb_ref, st_ref, do_ref, dq_ref, df_ref, dv_ref, dlb_ref, dstate):
        @pl.when(pl.program_id(1) == 0)
        def _():
            dstate[...] = jnp.zeros_like(dstate)
            dlb_ref[...] = jnp.zeros_like(dlb_ref)

        tril = _tri(c).astype(BF16)
        triu = _tri(c, upper=True).astype(BF16)
        last_row = lax.broadcasted_iota(jnp.int32, (c, 1), 0) == c - 1
        heads = range(hp)

        def chunk(j, carry):
            ci = nchunk - 1 - j
            sl = pl.ds(pl.multiple_of(ci * c, c), c)
            lb = lb_ref[...]
            hq, hf = hq_ref[sl, :], hf_ref[sl, :]
            q, sq, sg, f = _hg_gates(hq, hf, lb)
            kk = 1.0 - f
            g = _exact_tri_matmul(tril, jnp.log(f))
            a, qts, kts, eqs, eks = _hg_intra_wide(q, kk, g, hp)
            glast = g[c - 1:c, :]
            eg, egl, dec = jnp.exp(g), jnp.exp(glast - g), jnp.exp(glast)
            vb, dob = hi_ref[sl, :].astype(BF16), do_ref[sl, :].astype(BF16)
            qgb, kgb = (q * eg).astype(BF16), (kk * egl).astype(BF16)
            sts = [st_ref[h, ci] for h in heads]
            dsts = [dstate[h] for h in heads]
            stb, dstb = [s.astype(BF16) for s in sts], [s.astype(BF16) for s in dsts]
            vh, doh, qgh, kgh = _hg_heads(vb, hp), _hg_heads(dob, hp), _hg_heads(qgb, hp), _hg_heads(kgb, hp)
            dv = [_dot_tn(a[h].astype(BF16), doh[h]) + _dot_nt(kgh[h], dstb[h]) for h in heads]
            da = [jnp.where(_tri(c), _dot_nt(doh[h], vh[h]), 0.0).astype(BF16) for h in heads]
            dq_inter = jnp.concatenate([_dot(doh[h], stb[h]) for h in heads], axis=1) * eg
            dk_state = jnp.concatenate([_dot(vh[h], dstb[h]) for h in heads], axis=1) * egl
            new_dst = [_dot_tn(doh[h], qgh[h]) for h in heads]
            xs, dk, dgk = [], dk_state, 0.0
            for i in range(c // HG_SUB):
                rs = slice(i * HG_SUB, (i + 1) * HG_SUB)
                kth, qth = _hg_heads(kts[i], hp), _hg_heads(qts[i], hp)
                xi = jnp.concatenate([_dot(da[h][rs, :], kth[h]) for h in heads], axis=1)
                yi = jnp.concatenate([_dot_tn(da[h][rs, :], qth[h]) for h in heads], axis=1)
                xs.append(xi)
                dk = dk + yi * eks[i]
                dgk = dgk + yi * kts[i].astype(F32)
            dq = jnp.concatenate([x * e for x, e in zip(xs, eqs)], axis=0) + dq_inter
            dgq = jnp.concatenate([x * qt.astype(F32) for x, qt in zip(xs, qts)], axis=0)
            dg = dgq - dgk + q * dq_inter - kk * dk_state
            sdot = jnp.concatenate([jnp.sum(sts[h] * dsts[h], axis=0, keepdims=True) for h in heads], axis=1)
            dgl = jnp.sum(kk * dk_state, axis=0, keepdims=True) + dec * sdot
            dg = dg + jnp.where(last_row, dgl, 0.0)
            dlogf = _exact_tri_matmul(triu, dg)
            dfv = dlogf / f - dk
            dq_ref[sl, :] = (dq * (sq * (1.0 + hq * (1.0 - sq)))).astype(dq_ref.dtype)
            df_ref[sl, :] = (dfv * (1.0 - lb) * sg * (1.0 - sg)).astype(df_ref.dtype)
            dv_ref[sl, :] = jnp.concatenate(dv, axis=1).astype(dv_ref.dtype)
            dlb_ref[...] += jnp.sum(dfv * (1.0 - sg), axis=0, keepdims=True)
            dech = _hg_heads(dec, hp)
            for h in heads:
                dstate[h] = dsts[h] * dech[h] + new_dst[h]
            return carry

        lax.fori_loop(0, nchunk, chunk, 0)

    def col(cb):
        return pl.BlockSpec((tb, wp), lambda h, i: (nblk - 1 - i, cb * (nh // hp) + h))

    ocol = pl.BlockSpec((tb, wp), lambda h, i: (nblk - 1 - i, h))
    lbspec = pl.BlockSpec((1, wp), lambda h, i: (0, h))
    w = nh * HG_DK
    dq, df, dv, dlb = pl.pallas_call(
        body,
        out_shape=[jax.ShapeDtypeStruct((t, w), BF16)] * 3 + [jax.ShapeDtypeStruct((1, w), F32)],
        grid=(nh // hp, nblk),
        in_specs=[col(0), col(1), col(2), lbspec,
                  pl.BlockSpec((hp, nchunk, HG_DK, HG_DK), lambda h, i: (h, nblk - 1 - i, 0, 0)), ocol],
        out_specs=[ocol, ocol, ocol, lbspec],
        scratch_shapes=[pltpu.VMEM((hp, HG_DK, HG_DK), F32)],
        compiler_params=_cparams(("parallel", "arbitrary")),
        name="hgrn_bwd",
    )(zh, zh, zh, lb3.reshape(1, -1), states, d_o)
    return dq, df, dv, dlb.reshape(w)


NEG = -1e30
ATT_GW = ATT_HEADS * ATT_DH


def _att_scores(q, kp, kc, has_prev):
    scale = ATT_DH ** -0.5
    i = lax.broadcasted_iota(jnp.int32, (ATT_BLK, ATT_BLK), 0)
    j = lax.broadcasted_iota(jnp.int32, (ATT_BLK, ATT_BLK), 1)
    s_p = jnp.where(jnp.logical_and(j >= i, has_prev), _dot_nt(q, kp) * scale, NEG)
    s_c = jnp.where(j <= i, _dot_nt(q, kc) * scale, NEG)
    return s_p, s_c


def _att_views(arrs, d):
    return [a.reshape(d, -1, ATT_GW) for a in arrs]


def _att_unview(a, d):
    return a.reshape(-1, ATT_GW) if d == 1 else a


ATT_QB = 4


def _attn_fwd(qb, kb, vb, g):
    d = ATT_PATTERNS[g][1]
    q2, k2, v2 = _att_views([qb, kb, vb], d)
    nblk = q2.shape[1] // ATT_BLK
    nq = ATT_QB if nblk % ATT_QB == 0 else 1
    rows = nq * ATT_BLK

    def body(q_ref, kc_ref, kp_ref, vc_ref, vp_ref, o_ref, l_ref):
        first = pl.program_id(1) == 0
        hss = [slice(h * ATT_DH, (h + 1) * ATT_DH) for h in range(ATT_HEADS)]
        for b in range(nq):
            rs = slice(b * ATT_BLK, (b + 1) * ATT_BLK)
            ps = slice((b - 1) * ATT_BLK, b * ATT_BLK)
            has_prev = jnp.logical_not(first) if b == 0 else True
            kv = [(kp_ref[:, hs], vp_ref[:, hs]) if b == 0 else (kc_ref[ps, hs], vc_ref[ps, hs]) for hs in hss]
            sc = [_att_scores(q_ref[rs, hs], kv[h][0], kc_ref[rs, hs], has_prev) for h, hs in enumerate(hss)]
            ms = [jnp.maximum(jnp.max(s_p, axis=1, keepdims=True), jnp.max(s_c, axis=1, keepdims=True)) for s_p, s_c in sc]
            ps_ = [(jnp.exp(s_p - m), jnp.exp(s_c - m)) for (s_p, s_c), m in zip(sc, ms)]
            ls = [jnp.sum(p_p, axis=1, keepdims=True) + jnp.sum(p_c, axis=1, keepdims=True) for p_p, p_c in ps_]
            os_ = [_dot(p_p.astype(BF16), kv[h][1]) + _dot(p_c.astype(BF16), vc_ref[rs, hss[h]]) for h, (p_p, p_c) in enumerate(ps_)]
            for h, hs in enumerate(hss):
                o_ref[rs, hs] = os_[h] / ls[h]
                l_ref[rs, hs] = jnp.broadcast_to(ms[h] + jnp.log(ls[h]), (ATT_BLK, ATT_DH))

    cur = pl.BlockSpec((None, rows, ATT_GW), lambda r, n: (r, n, 0))
    prev = pl.BlockSpec((None, ATT_BLK, ATT_GW), lambda r, n: (r, jnp.maximum(n * nq - 1, 0), 0))
    o, lse = pl.pallas_call(
        body,
        out_shape=[jax.ShapeDtypeStruct(q2.shape, F32)] * 2,
        grid=(d, nblk // nq),
        in_specs=[cur, cur, prev, cur, prev],
        out_specs=[cur, cur],
        compiler_params=_cparams(("parallel", "arbitrary")),
        name=f"attn_fwd_g{g}",
    )(q2, k2, k2, v2, v2)
    return _att_unview(o, d), _att_unview(lse, d)


def _attn_bwd(qb, kb, vb, o, lse, d_o, d_lse, g):
    d = ATT_PATTERNS[g][1]
    q2, k2, v2 = _att_views([qb, kb, vb], d)
    o2, l2, do2, dl2 = _att_views([o, lse, d_o, d_lse], d)
    nblk = q2.shape[1] // ATT_BLK
    nq = ATT_QB if nblk % ATT_QB == 0 else 1
    rows = nq * ATT_BLK
    ns = nblk // nq
    scale = ATT_DH ** -0.5

    def body(q_ref, kc_ref, kp_ref, vc_ref, vp_ref, o_ref, l_ref, do_ref, dl_ref, dq_ref, dk_ref, dv_ref, ck, cv):
        n = pl.program_id(1)

        @pl.when(n == 0)
        def _():
            ck[...] = jnp.zeros_like(ck)
            cv[...] = jnp.zeros_like(cv)

        first = n == ns - 1
        hss = [slice(h * ATT_DH, (h + 1) * ATT_DH) for h in range(ATT_HEADS)]
        heads = range(ATT_HEADS)
        pend_k, pend_v = [ck[:, hs] for hs in hss], [cv[:, hs] for hs in hss]
        for b in reversed(range(nq)):
            rs = slice(b * ATT_BLK, (b + 1) * ATT_BLK)
            ps = slice((b - 1) * ATT_BLK, b * ATT_BLK)
            has_prev = jnp.logical_not(first) if b == 0 else True
            q = [q_ref[rs, hs] for hs in hss]
            kc, vc = [kc_ref[rs, hs] for hs in hss], [vc_ref[rs, hs] for hs in hss]
            kp = [kp_ref[:, hs] if b == 0 else kc_ref[ps, hs] for hs in hss]
            vp = [vp_ref[:, hs] if b == 0 else vc_ref[ps, hs] for hs in hss]
            sc = [_att_scores(q[h], kp[h], kc[h], has_prev) for h in heads]
            dob = [do_ref[rs, hs].astype(BF16) for hs in hss]
            dp = [(_dot_nt(dob[h], vp[h]), _dot_nt(dob[h], vc[h])) for h in heads]
            delta = [jnp.sum(do_ref[rs, hs] * o_ref[rs, hs] - dl_ref[rs, hs], axis=1, keepdims=True) for hs in hss]
            pr = [(jnp.exp(sc[h][0] - l_ref[rs, hss[h]][:, 0:1]), jnp.exp(sc[h][1] - l_ref[rs, hss[h]][:, 0:1])) for h in heads]
            ds = [((pr[h][0] * (dp[h][0] - delta[h]) * scale).astype(BF16), (pr[h][1] * (dp[h][1] - delta[h]) * scale).astype(BF16))
                  for h in heads]
            pb = [(pr[h][0].astype(BF16), pr[h][1].astype(BF16)) for h in heads]
            dq = [_dot(ds[h][0], kp[h]) + _dot(ds[h][1], kc[h]) for h in heads]
            dk_c = [_dot_tn(ds[h][1], q[h]) for h in heads]
            dv_c = [_dot_tn(pb[h][1], dob[h]) for h in heads]
            dk_p = [_dot_tn(ds[h][0], q[h]) for h in heads]
            dv_p = [_dot_tn(pb[h][0], dob[h]) for h in heads]
            for h, hs in enumerate(hss):
                dq_ref[rs, hs] = dq[h]
                dk_ref[rs, hs] = pend_k[h] + dk_c[h]
                dv_ref[rs, hs] = pend_v[h] + dv_c[h]
            pend_k, pend_v = dk_p, dv_p
        for h, hs in enumerate(hss):
            ck[:, hs] = pend_k[h]
            cv[:, hs] = pend_v[h]

    cur = pl.BlockSpec((None, rows, ATT_GW), lambda r, n: (r, ns - 1 - n, 0))
    prev = pl.BlockSpec((None, ATT_BLK, ATT_GW), lambda r, n: (r, jnp.maximum((ns - 1 - n) * nq - 1, 0), 0))
    shp = jax.ShapeDtypeStruct(q2.shape, F32)
    dq, dk, dv = pl.pallas_call(
        body,
        out_shape=[shp, shp, shp],
        grid=(d, ns),
        in_specs=[cur, cur, prev, cur, prev, cur, cur, cur, cur],
        out_specs=[cur, cur, cur],
        scratch_shapes=[pltpu.VMEM((ATT_BLK, ATT_GW), F32), pltpu.VMEM((ATT_BLK, ATT_GW), F32)],
        compiler_params=_cparams(("parallel", "arbitrary")),
        name=f"attn_bwd_g{g}",
    )(q2, k2, k2, v2, v2, o2, l2, do2, dl2)
    return _att_unview(dq, d), _att_unview(dk, d), _att_unview(dv, d)


def _rms_parts(x, width):
    outs = []
    for lo in range(0, x.shape[1], width):
        xs = x[:, lo:lo + width].astype(F32)
        r = lax.rsqrt(jnp.mean(xs * xs, axis=1, keepdims=True) + EPS)
        outs.append((xs * r, r))
    return outs


def _rms_bwd_part(xh, r, dxh):
    return r * (dxh - xh * jnp.mean(dxh * xh, axis=1, keepdims=True))


def _norm_pro(a, consts):
    (xh, _), = _rms_parts(a[0], a[0].shape[1])
    return [(xh * consts[0]).astype(BF16)]


def _norm_bwd_fin(accs, ex, consts):
    xv, dres = ex
    (xh, r), = _rms_parts(xv, xv.shape[1])
    dx = dres + _rms_bwd_part(xh, r, accs[0] * consts[0])
    return [dx, dx], [_colsum8(accs[0] * xh)]


def _rot_sign():
    lane = lax.broadcasted_iota(jnp.int32, (1, ATT_DH), 1)
    return jnp.where(lane < ATT_DH // 2, -1.0, 1.0).astype(F32)


def _rope(y, cos, sin):
    return y * cos + pltpu.roll(y, ATT_DH // 2, axis=1) * _rot_sign() * sin


def _rope_t(dy, cos, sin):
    return dy * cos - pltpu.roll(dy * sin, ATT_DH // 2, axis=1) * _rot_sign()


def _qk_prep(zq, zk, zv, qn, kn, cos, sin):
    w = zq.shape[1]

    def fn(ins, consts):
        cs, sn = ins[3], ins[4]
        outs = []
        for z, gain in ((ins[0], consts[0]), (ins[1], consts[1])):
            for i, (xh, _) in enumerate(_rms_parts(z, ATT_DH)):
                outs.append(_rope(xh * gain[:, i * ATT_DH:(i + 1) * ATT_DH], cs, sn))
        outs += [ins[2][:, i * ATT_DH:(i + 1) * ATT_DH] for i in range(w // ATT_DH)]
        groups = [jnp.concatenate(outs[i:i + ATT_HEADS], axis=1) for i in range(0, len(outs), ATT_HEADS)]
        return groups, []

    outs, _ = _rowwise(fn, [(zq, w, 0), (zk, w, 0), (zv, w, 0), (cos, ATT_DH, 0), (sin, ATT_DH, 0)], [qn, kn],
                       [(ATT_GW, BF16, ATT_PATTERNS[g][1]) for g in range(ATT_GROUPS)] * 3, [], bm=256, name="qk_prep")
    return outs[0:3], outs[3:6], outs[6:9]


def _qk_prep_bwd(zq, zk, dq_g, dk_g, dv_g, qn, kn, cos, sin):
    w = zq.shape[1]

    def fn(ins, consts):
        cs, sn = ins[2], ins[3]
        outs, sums = [], []
        for z, gain, dparts in ((ins[0], consts[0], ins[4:7]), (ins[1], consts[1], ins[7:10])):
            dout = jnp.concatenate(dparts, axis=1)
            dz, dgain = [], []
            for i, (xh, r) in enumerate(_rms_parts(z, ATT_DH)):
                hs = slice(i * ATT_DH, (i + 1) * ATT_DH)
                dy = _rope_t(dout[:, hs], cs, sn)
                dgain.append(_colsum8(dy * xh))
                dz.append(_rms_bwd_part(xh, r, dy * gain[:, hs]))
            outs.append(jnp.concatenate(dz, axis=1))
            sums.append(jnp.concatenate(dgain, axis=1))
        outs.append(jnp.concatenate(ins[10:13], axis=1))
        return outs, sums

    ins = [(zq, w, 0), (zk, w, 0), (cos, ATT_DH, 0), (sin, ATT_DH, 0)]
    for parts in (dq_g, dk_g, dv_g):
        ins += [(a, ATT_GW, 0, ATT_PATTERNS[g][1]) for g, a in enumerate(parts)]
    (dzq, dzk, dzv), (dqn, dkn) = _rowwise(fn, ins, [qn, kn], [(w, BF16)] * 3, [w, w], bm=256, name="qk_prep_bwd")
    return dzq, dzk, dzv, dqn, dkn


def _post_a(o_raw, zh, gout):
    w = o_raw.shape[1]

    def fn(ins, consts):
        oh = jnp.concatenate([xh for xh, _ in _rms_parts(ins[0], HG_DK)], axis=1)
        hg = ins[1]
        return [oh * consts[0] * (hg * _sigmoid(hg))], []

    (y,), _ = _rowwise(fn, [(o_raw, w, 0), (zh, w, 3)], [gout.reshape(1, w)], [(w, BF16)], [], bm=512, name="post_a")
    return y


def _post_a_bwd(o_raw, zh, gout, dy):
    w = o_raw.shape[1]

    def fn(ins, consts):
        parts = _rms_parts(ins[0], HG_DK)
        oh = jnp.concatenate([xh for xh, _ in parts], axis=1)
        hg, dyv, gain = ins[1], ins[2], consts[0]
        sg = _sigmoid(hg)
        s = hg * sg
        doh = dyv * gain * s
        do = jnp.concatenate([_rms_bwd_part(xh, r, doh[:, i * HG_DK:(i + 1) * HG_DK]) for i, (xh, r) in enumerate(parts)], axis=1)
        dhg = dyv * oh * gain * (sg * (1.0 + hg * (1.0 - sg)))
        return [do, dhg], [_colsum8(dyv * oh * s)]

    (do, dhg), (dgain,) = _rowwise(fn, [(o_raw, w, 0), (zh, w, 3), (dy, w, 0)], [gout.reshape(1, w)],
                                   [(w, F32), (w, BF16)], [w], bm=512, name="post_a_bwd")
    return do, dhg, dgain


def _merge_alpha(lses):
    m = jnp.maximum(jnp.maximum(lses[0], lses[1]), lses[2])
    e = [jnp.exp(l - m) for l in lses]
    inv = 1.0 / (e[0] + e[1] + e[2])
    return [x * inv for x in e]


def _group_ins(parts):
    return [(a, ATT_GW, 0, ATT_PATTERNS[g][1]) for g, a in enumerate(parts)]


def _merge_b(o_g, lse_g):
    def fn(ins, consts):
        al = _merge_alpha(ins[3:6])
        return [al[0] * ins[0] + al[1] * ins[1] + al[2] * ins[2]], []

    (y,), _ = _rowwise(fn, _group_ins(o_g) + _group_ins(lse_g), [], [(ATT_GW, BF16)], [], bm=512, name="merge_b")
    return y


def _merge_b_bwd(o_g, lse_g, dy):
    def fn(ins, consts):
        al = _merge_alpha(ins[3:6])
        dyv = ins[6]
        dal = [dyv * ins[i] for i in range(3)]
        tot = al[0] * dal[0] + al[1] * dal[1] + al[2] * dal[2]
        return [al[i] * dyv for i in range(3)] + [al[i] * (dal[i] - tot) for i in range(3)], []

    outs, _ = _rowwise(fn, _group_ins(o_g) + _group_ins(lse_g) + [(dy, ATT_GW, 0)], [],
                       [(ATT_GW, F32, ATT_PATTERNS[g][1]) for g in range(ATT_GROUPS)] * 2, [], bm=512, name="merge_b_bwd")
    return outs[:3], outs[3:]


def _loss_head(y, target):
    d = y.shape[1]

    def fn(ins, consts):
        e = ins[0] - ins[1]
        return [e * (1.0 / d)] * 2, [_colsum8(e * e)]

    (dy, dyb), (sq,) = _rowwise(fn, [(y, d, 0), (target, d, 0)], [], [(d, F32), (d, BF16)], [d], bm=512, name="loss_head")
    return 0.5 * jnp.sum(sq) / d, dy, dyb


def _ffn_fwd(x, gain, wt, wo_fn, tag):
    t, d = x.shape
    f = wt.shape[0] // 2

    def act(accs, ex, consts):
        a, b = accs
        s = _sigmoid(a)
        sa = a * s
        return (sa * b, b, 0.5 * sa, 0.5 * (s + sa * (1.0 - s)))

    bn = FFN_BN if f % FFN_BN == 0 else 256
    wn = wt.T
    u, b, sa, sp, h = _mm([x], [wn, wn], [(0, 0, 0), (0, 1, 1)], 2, act, [BF16] * 4, m=t, n=f, k=d,
                          bm=512, bn=bn, bk=d, b_off=[(0, 0), (f // min(bn, f), 0)],
                          consts=[gain.reshape(1, d)], a_pro=_norm_pro, chunk=MXU_COLS, name=f"ffn_in_{tag}")
    wo = wo_fn(u)
    (y,) = _mm([u], [wo], [(0, 0, 0)], 1, lambda accs, ex: (ex[0] + 0.5 * accs[0],), [F32], m=t, n=d, k=f,
               bm=512, bn=d, bk=f, extras=[x], name=f"ffn_out_{tag}")
    return y, (x, h, u, b, sa, sp, wo)


def _ffn_bwd(dy, dyb, saved, gain, wt, tag, tok, emit):
    x, h, u, b, sa, sp, wo = saved
    t, d = x.shape
    f = wo.shape[0]

    def dact(accs, ex, consts):
        bv, sav, spv = (e.astype(F32) for e in ex)
        return (accs[0] * bv * spv, accs[0] * sav)

    bn = FFN_BN if f % FFN_BN == 0 else 256
    da, db = _mm([dyb], [wo], [(0, 0, 0)], 1, dact, [BF16, BF16], m=t, n=f, k=d, tb=True, bm=512, bn=bn, bk=d,
                 extras=[b, sa, sp], n_outer=True, chunk=MXU_COLS, consts=[tok], name=f"ffn_dact_{tag}")
    (dwo,) = _mm([u], [dyb], [(0, 0, 0)], 1, lambda accs, ex: (0.5 * accs[0],), [BF16], m=f, n=d, k=t, ta=True,
                 bm=1408, bn=d, bk=1024, name=f"ffn_dwo_{tag}")
    (dwt,) = _mm([da, db], [h], [(0, 0, 0)], 1, _first, [BF16], m=2 * f, n=d, k=t, ta=True, bm=min(1408, f), bn=d,
                 bk=1024, a_cat=True, name=f"ffn_dwt_{tag}")
    tok = emit(dwt, dwo)
    bk = min(FFN_BN, f)
    dx, dxb, dgain = _mm([da, db], [wt, wt], [(0, 0, 0), (1, 1, 0)], 1, _norm_bwd_fin, [F32, BF16], m=t, n=d, k=f,
                         bm=512, bn=d, bk=bk, b_off=[(0, 0), (0, f // bk)], extras=[x, dy],
                         consts=[gain.reshape(1, d), tok], n_sums=1, name=f"ffn_dh_{tag}")
    return dx, dxb, jnp.sum(dgain, axis=0), tok


FFN_BN = 2816
Z_SPLITS = (("h", 4096), ("q", 1536), ("k", 1536), ("v", 1536), ("g", 2048))


def _mix_fwd(x, p, cos, sin):
    t, d = x.shape
    z, off, hm = {}, 0, None
    winn = p["wint"].T
    for nm, width in Z_SPLITS:
        bn = 1024 if off % 1024 == 0 and width % 1024 == 0 else 512
        first = hm is None
        res = _mm([x if first else hm], [winn], [(0, 0, 0)], 1, (lambda accs, ex, consts: (accs[0],)) if first else _first,
                  [F32 if nm == "h" else BF16], m=t, n=width, k=d, bm=1024, bn=bn, bk=d, b_off=[(off // bn, 0)],
                  consts=[p["gm"].reshape(1, d)] if first else (), a_pro=_norm_pro if first else None, name=f"mix_in_{nm}")
        z[nm] = res[0]
        hm = res[1] if first else hm
        off += width
    o_raw, states = _hgrn_fwd(z["h"], p["lb3"])
    qb, kb, vb = _qk_prep(z["q"], z["k"], z["v"], p["qn"], p["kn"], cos, sin)
    o_g, lse_g = zip(*[_attn_fwd(qb[g], kb[g], vb[g], g) for g in range(ATT_GROUPS)])
    oa = _post_a(o_raw, z["h"], p["gout"])
    ob = _merge_b(o_g, lse_g)
    late = p["late"](ob)
    p = dict(p, **late)
    (ya,) = _mm([oa], [p["wa"]], [(0, 0, 0)], 1, _first, [F32], m=t, n=d, k=oa.shape[1], bm=1024, bn=d, bk=oa.shape[1],
                name="branch_a")

    def gate(accs, ex):
        return (_sigmoid(ex[0].astype(F32)) * ex[2] + _sigmoid(ex[1].astype(F32)) * accs[0], accs[0])

    merged, yb = _mm([ob], [p["wbt"]], [(0, 0, 0)], 1, gate, [BF16, F32], m=t, n=d, k=ATT_GW, tb=True, bm=512, bn=d,
                     bk=ATT_GW, extras=[z["g"], z["g"], ya], e_off=[0, 1, 0], chunk=MXU_COLS, name="branch_b_gate")
    (y,) = _mm([merged], [p["wo"]], [(0, 0, 0)], 1, lambda accs, ex: (ex[0] + accs[0],), [F32], m=t, n=d, k=d,
               bm=1024, bn=d, bk=d, extras=[x], name="mix_out")
    return y, (x, hm, z, o_raw, states, qb, kb, vb, o_g, lse_g, oa, ob, ya, yb, merged, late)


def _mix_bwd(dy, dyb, saved, p, cos, sin, tok):
    x, hm, z, o_raw, states, qb, kb, vb, o_g, lse_g, oa, ob, ya, yb, merged, late = saved
    p = dict(p, **late)
    t, d = x.shape
    w = oa.shape[1]

    def dgate(accs, ex, consts):
        dm = accs[0]
        sa, sb = _sigmoid(ex[0].astype(F32)), _sigmoid(ex[1].astype(F32))
        return (sa * dm, sb * dm, dm * ex[2] * sa * (1.0 - sa), dm * ex[3] * sb * (1.0 - sb))

    dya, dyb_, dga, dgb = _mm([dyb], [p["wo"]], [(0, 0, 0)], 1, dgate, [BF16] * 4, m=t, n=d, k=d, tb=True, bm=512, bn=d,
                              bk=d, extras=[z["g"], z["g"], ya, yb], e_off=[0, 1, 0, 0], chunk=MXU_COLS, consts=[tok], name="mix_out_bwd")
    (dwo,) = _mm([merged], [dyb], [(0, 0, 0)], 1, _first, [BF16], m=d, n=d, k=t, ta=True, bm=d, bn=d, bk=1024, name="mix_dwo")
    (doa,) = _mm([dya], [p["wa"]], [(0, 0, 0)], 1, _first, [F32], m=t, n=w, k=d, tb=True, bm=1024, bn=w, bk=d, name="branch_a_bwd")
    (dwa,) = _mm([oa], [dya], [(0, 0, 0)], 1, _first, [BF16], m=w, n=d, k=t, ta=True, bm=w, bn=d, bk=1024, name="branch_a_dw")
    (dob,) = _mm([dyb_], [p["wbt"]], [(0, 0, 0)], 1, _first, [F32], m=t, n=ATT_GW, k=d, bm=1024, bn=ATT_GW, bk=d,
                 name="branch_b_bwd")
    (dwbt,) = _mm([dyb_], [ob], [(0, 0, 0)], 1, _first, [BF16], m=d, n=ATT_GW, k=t, ta=True, bm=d, bn=ATT_GW, bk=1024,
                  name="branch_b_dw")
    do_raw, dhg, dgout = _post_a_bwd(o_raw, z["h"], p["gout"], doa)
    do_g, dlse_g = _merge_b_bwd(o_g, lse_g, dob)
    dq_g, dk_g, dv_g = zip(*[_attn_bwd(qb[g], kb[g], vb[g], o_g[g], lse_g[g], do_g[g], dlse_g[g], g)
                             for g in range(ATT_GROUPS)])
    dzq, dzk, dzv, dqn, dkn = _qk_prep_bwd(z["q"], z["k"], dq_g, dk_g, dv_g, p["qn"], p["kn"], cos, sin)
    dhq, dhf, dhi, lbsum = _hgrn_bwd(z["h"], p["lb3"], states, do_raw)
    dz = jnp.concatenate([dhq, dhf, dhi, dhg, dzq, dzk, dzv, dga, dgb], axis=1)
    pw = dz.shape[1]
    (dwint,) = _mm([dz], [hm], [(0, 0, 0)], 1, _first, [BF16], m=pw, n=d, k=t, ta=True, bm=1536, bn=d, bk=1024, name="mix_in_dw")
    dx, dxb, dgm = _mm([dz], [p["wint"]], [(0, 0, 0)], 1, _norm_bwd_fin, [F32, BF16], m=t, n=d, k=pw, bm=1024, bn=d, bk=1536,
                       extras=[x, dy], consts=[p["gm"].reshape(1, d)], n_sums=1, name="mix_in_bwd")
    return dx, dxb, dict(gm=jnp.sum(dgm, axis=0), wint=dwint, lbsum=lbsum, gout=dgout, qn=dqn, kn=dkn, wa=dwa, wbt=dwbt, wo=dwo)


def _rope_tables(t):
    pos = jnp.arange(t, dtype=F32)
    inv = ROPE_THETA ** (-jnp.arange(0, ATT_DH, 2, dtype=F32) / ATT_DH)
    ang = pos[:, None] * inv[None, :]
    ang = jnp.concatenate([ang, ang], axis=-1)
    return jnp.cos(ang), jnp.sin(ang)


def _lower_bounds(logits):
    lb = jnp.cumsum(jax.nn.softmax(logits, axis=0), axis=0)
    return lb - lb[0:1]


def _head_gain(g):
    return jnp.tile(g[:, None, :], (1, ATT_HEADS, 1)).reshape(1, ATT_GROUPS * ATT_GW)


SMALL_GRADS = ("ffn1_norm", "mix_norm", "lbsum", "hgrn_out_norm", "attn_q_norm", "attn_k_norm", "ffn2_norm")


def _local_step(x, target, small, fetch, emit):
    t = x.shape[0]
    depth = small["ffn1_norm"].shape[0]
    cos, sin = _rope_tables(t)
    lb_all = _lower_bounds(small["hgrn_lb_logits"])
    saved = []
    for l in range(depth):
        w1t = fetch("w1t", l, x)["w1t"]
        x, s1 = _ffn_fwd(x, small["ffn1_norm"][l], w1t, lambda after, l=l: fetch("w1o", l, after)["w1o"], "1")
        p = dict(gm=small["mix_norm"][l], wint=fetch("wint", l, x)["wint"], lb3=lb_all[l].reshape(-1, 1, HG_DK),
                 gout=small["hgrn_out_norm"][l], qn=_head_gain(small["attn_q_norm"][l]),
                 kn=_head_gain(small["attn_k_norm"][l]), late=functools.partial(fetch, "mout", l))
        x, sm = _mix_fwd(x, p, cos, sin)
        w2t = fetch("w2t", l, x)["w2t"]
        x, s2 = _ffn_fwd(x, small["ffn2_norm"][l], w2t, lambda after, l=l: fetch("w2o", l, after)["w2o"], "2")
        saved.append((p, w1t, w2t, s1, sm, s2))
    loss, dx, dxb = _loss_head(x, target)
    gsmall = {k: [None] * depth for k in SMALL_GRADS}
    tok = jnp.zeros((8, 128), F32)
    for l in reversed(range(depth)):
        p, w1t, w2t, s1, sm, s2 = saved[l]
        dx, dxb, gsmall["ffn2_norm"][l], tok = _ffn_bwd(
            dx, dxb, s2, small["ffn2_norm"][l], w2t, "2", tok, lambda dwt, dwo, l=l: emit("ffn2", l, dict(w2t=dwt, w2o=dwo), None))
        dx, dxb, gm = _mix_bwd(dx, dxb, sm, p, cos, sin, tok)
        tok = emit("mix", l, {k: gm[k] for k in ("wint", "wa", "wbt", "wo")}, None)
        gsmall["mix_norm"][l], gsmall["lbsum"][l], gsmall["hgrn_out_norm"][l] = gm["gm"], gm["lbsum"], gm["gout"]
        for k, src in (("attn_q_norm", "qn"), ("attn_k_norm", "kn")):
            gsmall[k][l] = jnp.sum(gm[src].reshape(ATT_GROUPS, ATT_HEADS, ATT_DH), axis=1)
        dx, dxb, gsmall["ffn1_norm"][l], tok = _ffn_bwd(
            dx, dxb, s1, small["ffn1_norm"][l], w1t, "1", tok, lambda dwt, dwo, l=l: emit("ffn1", l, dict(w1t=dwt, w1o=dwo), None))
    emit("small", 0, {}, ({k: jnp.stack(v) for k, v in gsmall.items()}, loss))
    return dx


_HBM = pl.BlockSpec(memory_space=pltpu.HBM)
_SEM = pl.BlockSpec(memory_space=pltpu.SEMAPHORE)
_EFFECT = pltpu.SideEffectType.DATAFLOW_SIDE_EFFECTING


def _peer(p):
    x, y, c = lax.axis_index("x"), lax.axis_index("y"), lax.axis_index("c")
    me = 4 * x + 2 * y + c
    return (1 - x if p & 4 else x, 1 - y if p & 2 else y, 1 - c if p & 1 else c), jnp.bitwise_xor(me, p), me


def _xchg_copy(src, land, mode, send_sems, recv_sems, k, p, arriving):
    peer, peer_id, me = _peer(p)
    block = src if mode == "gather" else src.at[peer_id]
    return pltpu.make_async_remote_copy(
        src_ref=block, dst_ref=land.at[peer_id if arriving else me], send_sem=send_sems.at[k * (N_DEV - 1) + p - 1],
        recv_sem=recv_sems.at[k * (N_DEV - 1) + p - 1], device_id=peer, device_id_type=MESH)


def _xchg_start(srcs, modes, groups, name):
    n, ng = len(srcs), len(groups)

    def body(*refs):
        src = refs[:n]
        sems = refs[n:n + 2 * ng]
        land = refs[n + 2 * ng + n:n + 2 * ng + 2 * n]
        token = refs[n + 2 * ng + 2 * n]
        for gi, idx in enumerate(groups):
            for ki, k in enumerate(idx):
                for p in range(1, N_DEV):
                    _xchg_copy(src[k], land[k], modes[k], sems[2 * gi], sems[2 * gi + 1], ki, p, False).start()
        token[...] = jnp.zeros_like(token)

    sem_shapes = []
    for idx in groups:
        sem_shapes += [pltpu.SemaphoreType.DMA((len(idx) * (N_DEV - 1),))] * 2
    outs = pl.pallas_call(
        body,
        out_shape=sem_shapes + [pltpu.HBM(a.shape, a.dtype) for a in srcs]
        + [pltpu.HBM((N_DEV,) + a.shape[-2:], a.dtype) for a in srcs] + [jax.ShapeDtypeStruct((8, 128), F32)],
        in_specs=[_HBM] * n,
        out_specs=[_SEM] * (2 * ng) + [_HBM] * (2 * n) + [pl.BlockSpec(memory_space=pltpu.VMEM)],
        input_output_aliases={i: 2 * ng + i for i in range(n)},
        compiler_params=pltpu.CompilerParams(has_side_effects=_EFFECT),
        name=name,
    )(*[pltpu.with_memory_space_constraint(a, pltpu.HBM) for a in srcs])
    sems = [(outs[2 * gi], outs[2 * gi + 1]) for gi in range(ng)]
    return sems, outs[2 * ng:2 * ng + n], outs[2 * ng + n:2 * ng + 2 * n], outs[-1]


def _xchg_wait_call(srcs, lands, modes, sems, after, name):
    n = len(srcs)

    def body(*refs):
        src, land = refs[:n], refs[n:2 * n]
        send_sems, recv_sems = refs[2 * n], refs[2 * n + 1]
        for p in range(1, N_DEV):
            for k in range(n):
                cp = _xchg_copy(src[k], land[k], modes[k], send_sems, recv_sems, k, p, True)
                cp.wait_send()
                cp.wait_recv()

    outs = pl.pallas_call(
        body,
        out_shape=[pltpu.HBM(a.shape, a.dtype) for a in list(srcs) + list(lands)],
        in_specs=[_HBM] * (2 * n) + [_SEM, _SEM, pl.BlockSpec(memory_space=pl.ANY)],
        out_specs=[_HBM] * (2 * n),
        input_output_aliases={i: i for i in range(2 * n)},
        compiler_params=pltpu.CompilerParams(has_side_effects=_EFFECT),
        name=name,
    )(*srcs, *lands, sems[0], sems[1], after)
    return outs[:n], outs[n:]


def _xchg_wait(srcs, lands, modes, sems, after, name):
    srcs, lands = _xchg_wait_call(srcs, lands, modes, sems, after, name)
    me = 4 * lax.axis_index("x") + 2 * lax.axis_index("y") + lax.axis_index("c")
    done = []
    for a, land, mode in zip(srcs, lands, modes):
        own = a[None] if mode == "gather" else lax.dynamic_slice_in_dim(a, me, 1, axis=0)
        done.append(lax.dynamic_update_slice(land, own, (me, 0, 0)))
    return done


def _sum_slots(land):
    g, _, r, c = land.shape
    br = r // 2 if (r % 32 == 0 and r >= 256) else r

    def body(l_ref, o_ref):
        acc = l_ref[0, 0].astype(F32)
        for j in range(1, N_DEV):
            acc = acc + l_ref[0, j].astype(F32)
        o_ref[0] = acc

    return pl.pallas_call(
        body,
        out_shape=jax.ShapeDtypeStruct((g, r, c), F32),
        grid=(g, r // br),
        in_specs=[pl.BlockSpec((1, N_DEV, br, c), lambda i, j: (i, 0, j, 0))],
        out_specs=pl.BlockSpec((1, br, c), lambda i, j: (i, j, 0)),
        compiler_params=_cparams(("parallel", "parallel")),
        name="sum_slots",
    )(land)


def _adamw(w, g, m, v):
    shape = w.shape
    cols = shape[-1]
    rows = int(np.prod(shape[:-1]))
    bm = max(b for b in range(8, 513, 8) if rows % b == 0) if rows % 8 == 0 else rows
    c1 = 1.0 - ADAM_B1 ** ADAM_STEP
    c2 = 1.0 - ADAM_B2 ** ADAM_STEP

    def fn(ins, consts):
        wv, gv, mv, vv = ins
        m2 = ADAM_B1 * mv + (1.0 - ADAM_B1) * gv
        v2 = ADAM_B2 * vv + (1.0 - ADAM_B2) * (gv * gv)
        delta = -ADAM_LR * ((m2 / c1) / (jnp.sqrt(v2 / c2) + ADAM_EPS) + ADAM_WD * wv)
        return [delta, m2, v2], []

    outs, _ = _rowwise(fn, [(a.reshape(rows, cols), cols, 0) for a in (w, g, m, v)], [], [(cols, F32)] * 3, [],
                       bm=bm, name="adamw")
    return [o.reshape(shape) for o in outs]


BIG = ("w1t", "w1o", "wint", "wa", "wbt", "wo", "w2t", "w2o")
FETCH_GROUPS = dict(w1t=("w1t",), w1o=("w1o",), wint=("wint",), mout=("wa", "wbt", "wo"), w2t=("w2t",), w2o=("w2o",))
SMALL_ROWS = (("ffn1_norm", 0), ("mix_norm", 2), ("lbsum", 4), ("hgrn_out_norm", 6), ("ffn2_norm", 8),
              ("attn_q_norm", 10), ("attn_k_norm", 12))
SMALL_PACK_ROWS = 16


def kernel(x, ffn1_norm, ffn1_w_in, ffn1_w_out, mix_norm, w_in, hgrn_lb_logits, hgrn_out_norm, attn_q_norm, attn_k_norm, w_branch_a, w_branch_b, w_out, ffn2_norm, ffn2_w_in, ffn2_w_out, loss_target, m_ffn1_norm, m_ffn1_w_in, m_ffn1_w_out, m_mix_norm, m_w_in, m_hgrn_lb_logits, m_hgrn_out_norm, m_attn_q_norm, m_attn_k_norm, m_w_branch_a, m_w_branch_b, m_w_out, m_ffn2_norm, m_ffn2_w_in, m_ffn2_w_out, v_ffn1_norm, v_ffn1_w_in, v_ffn1_w_out, v_mix_norm, v_w_in, v_hgrn_lb_logits, v_hgrn_out_norm, v_attn_q_norm, v_attn_k_norm, v_w_branch_a, v_w_branch_b, v_w_out, v_ffn2_norm, v_ffn2_w_in, v_ffn2_w_out):
    names = ("ffn1_norm", "ffn1_w_in", "ffn1_w_out", "mix_norm", "w_in", "hgrn_lb_logits", "hgrn_out_norm", "attn_q_norm",
             "attn_k_norm", "w_branch_a", "w_branch_b", "w_out", "ffn2_norm", "ffn2_w_in", "ffn2_w_out")
    w = dict(zip(names, (ffn1_norm, ffn1_w_in, ffn1_w_out, mix_norm, w_in, hgrn_lb_logits, hgrn_out_norm, attn_q_norm,
                         attn_k_norm, w_branch_a, w_branch_b, w_out, ffn2_norm, ffn2_w_in, ffn2_w_out)))
    m = dict(zip(names, (m_ffn1_norm, m_ffn1_w_in, m_ffn1_w_out, m_mix_norm, m_w_in, m_hgrn_lb_logits, m_hgrn_out_norm,
                         m_attn_q_norm, m_attn_k_norm, m_w_branch_a, m_w_branch_b, m_w_out, m_ffn2_norm, m_ffn2_w_in, m_ffn2_w_out)))
    v = dict(zip(names, (v_ffn1_norm, v_ffn1_w_in, v_ffn1_w_out, v_mix_norm, v_w_in, v_hgrn_lb_logits, v_hgrn_out_norm,
                         v_attn_q_norm, v_attn_k_norm, v_w_branch_a, v_w_branch_b, v_w_out, v_ffn2_norm, v_ffn2_w_in, v_ffn2_w_out)))
    depth, d = ffn1_norm.shape

    def tr(a):
        return jnp.swapaxes(a, 1, 2)

    shard = dict(w1t=tr(ffn1_w_in), w1o=ffn1_w_out, wint=tr(w_in), wa=w_branch_a,
                 wbt=tr(w_branch_b).reshape(depth, -1, d), wo=w_out, w2t=tr(ffn2_w_in), w2o=ffn2_w_out)
    order = [(g, l) for l in range(depth) for g in FETCH_GROUPS]
    started = {}
    for name, part in (("gather_start", order),):
        flat = [(l, k) for g, l in part for k in FETCH_GROUPS[g]]
        groups, pos = [], 0
        for g, l in part:
            groups.append(list(range(pos, pos + len(FETCH_GROUPS[g]))))
            pos += len(FETCH_GROUPS[g])
        sems, srcs, lands, _ = _xchg_start([shard[k][l].astype(BF16) for l, k in flat], ["gather"] * len(flat), groups, name)
        for gi, key in enumerate(part):
            started[key] = ([srcs[i] for i in groups[gi]], [lands[i] for i in groups[gi]], sems[gi])

    def fetch(group, l, after):
        srcs, lands, sems = started[group, l]
        lands = _xchg_wait(srcs, lands, ["gather"] * len(srcs), sems, after, f"gather_wait_{group}{l}")
        out = {}
        for k, land in zip(FETCH_GROUPS[group], lands):
            out[k] = land.reshape(d, -1) if k == "wbt" else land.reshape(-1, d)
        return out

    pending = []

    def emit(group, l, g, final):
        keys = list(g)
        srcs = [g[k].reshape(N_DEV, -1, d) for k in keys]
        modes = ["scatter"] * len(keys)
        if final is not None:
            gsmall, loss = final
            pack = jnp.zeros((SMALL_PACK_ROWS, d), F32)
            for k, r0 in SMALL_ROWS:
                rows = gsmall[k].reshape(depth, -1)
                pack = pack.at[r0:r0 + depth, :rows.shape[1]].set(rows)
            srcs.append(pack.at[14, :].set(loss))
            modes.append("gather")
            keys.append("small")
        sems, s_thru, l_thru, token = _xchg_start(srcs, modes, [list(range(len(srcs)))], f"grads_start_{group}{l}")
        pending.append((group, l, keys, modes, sems[0], s_thru, l_thru))
        return token

    small = {k: w[k] for k in ("ffn1_norm", "mix_norm", "hgrn_lb_logits", "hgrn_out_norm", "attn_q_norm", "attn_k_norm", "ffn2_norm")}
    dx = _local_step(x[0], loss_target[0], small, fetch, emit)

    summed, after = {}, dx
    for group, l, keys, modes, sems, s_thru, l_thru in pending:
        lands = _xchg_wait(s_thru, l_thru, modes, sems, after, f"grads_wait_{group}{l}")
        for k, land in zip(keys, lands):
            summed[k, l] = _sum_slots(land[None])[0]
        after = summed[keys[-1], l]
    gsum = {k: jnp.stack([summed[k, l] for l in range(depth)]) for k in BIG}
    tot = summed["small", 0]

    grads = {}
    for k, r0 in SMALL_ROWS:
        shp = (depth,) + (w[k].shape[1:] if k != "lbsum" else (d,))
        grads[k] = tot[r0:r0 + depth, :int(np.prod(shp[1:]))].reshape(shp)
    _, lb_vjp = jax.vjp(_lower_bounds, hgrn_lb_logits)
    grads["hgrn_lb_logits"] = lb_vjp(grads.pop("lbsum"))[0]
    grads["ffn1_w_in"], grads["ffn1_w_out"] = tr(gsum["w1t"]), gsum["w1o"]
    grads["w_in"], grads["w_branch_a"] = tr(gsum["wint"]), gsum["wa"]
    grads["w_branch_b"] = tr(gsum["wbt"].reshape(depth, d // N_DEV, -1))
    grads["w_out"] = gsum["wo"]
    grads["ffn2_w_in"], grads["ffn2_w_out"] = tr(gsum["w2t"]), gsum["w2o"]

    upd = {k: _adamw(w[k], grads[k], m[k], v[k]) for k in names}
    return (tot[14, 0], dx[None], *[grads[k] for k in names], *[upd[k][0] for k in names],
            *[upd[k][1] for k in names], *[upd[k][2] for k in names])
```

```python
import functools

import jax
import jax.numpy as jnp
import numpy as np
from jax import lax
from jax.experimental import pallas as pl
from jax.experimental.pallas import tpu as pltpu

F32 = jnp.float32
BF16 = jnp.bfloat16

N_DEV = 8
EPS = 1e-6
HG_DK = 128
HG_CHUNK = 64
HG_SUB = 16
HG_HP = 8
ATT_PATTERNS = ((128, 1), (512, 4), (2048, 16))
ATT_GROUPS = 3
ATT_HEADS = 4
ATT_DH = 128
ATT_BLK = 128
ROPE_THETA = 10000.0
ADAM_LR, ADAM_B1, ADAM_B2, ADAM_EPS, ADAM_WD, ADAM_STEP = 0.001, 0.9, 0.999, 1e-08, 0.01, 10
VMEM_LIMIT_BYTES = 56 * 1024 * 1024
MXU_COLS = 256
MESH = pl.DeviceIdType.MESH


def _cparams(sem, **kw):
    return pltpu.CompilerParams(dimension_semantics=sem, vmem_limit_bytes=VMEM_LIMIT_BYTES, **kw)


def _sigmoid(x):
    return 1.0 / (1.0 + jnp.exp(-x))


def _mm(a_list, b_list, pairs, n_acc, fin, out_dtypes, *, m, n, k, ta=False, tb=False, bm, bn, bk,
        b_off=None, extras=(), e_off=None, n_outer=False, consts=(), a_pro=None, n_sums=0, chunk=0, a_cat=False, name):
    bm, bn, bk = min(bm, m), min(bn, n), min(bk, k)
    assert m % bm == 0 and n % bn == 0 and k % bk == 0, (name, m, n, k, bm, bn, bk)
    nk = k // bk
    assert not (a_pro and (nk > 1 or ta or n_outer)) and not (n_sums and (bn != n or n_outer)), name
    assert not (chunk and (nk > 1 or n_sums or chunk % 128)), name
    if a_cat:
        unit = bm if ta else bk
        widths = [a.shape[1] for a in a_list]
        assert all(w % unit == 0 for w in widths) and sum(widths) == (m if ta else k) and not a_pro, name
        cat_counts = [w // unit for w in widths]
        cat_starts = [sum(cat_counts[:i]) for i in range(len(widths))]
    b_off = b_off or [(0, 0)] * len(b_list)
    e_off = e_off or [0] * len(extras)
    na, nb, ne, nc, no = len(a_list), len(b_list), len(extras), len(consts), len(out_dtypes)
    nao = na if a_pro else 0
    dn = (((0,) if ta else (1,), (1,) if tb else (0,)), ((), ()))

    def body(*refs):
        refs = list(refs)
        a_refs, b_refs, e_refs, c_refs, o_refs, ao_refs, s_refs = (
            [refs.pop(0) for _ in range(cnt)] for cnt in (na, nb, ne, nc, no, nao, n_sums))
        acc_refs = refs
        kk = pl.program_id(2)
        first = pl.program_id(0) == 0
        cvals = [c[...] for c in c_refs]
        a_vals = [r[...] for r in a_refs]
        if a_cat:
            col = pl.program_id(1 if n_outer else 0) if ta else kk
            sel = a_vals[0]
            for start, v in zip(cat_starts[1:], a_vals[1:]):
                sel = jnp.where(col >= start, v, sel)
            a_vals = [sel]
        if a_pro:
            @pl.when(pl.program_id(1) == 0)
            def _():
                for r, v in zip(ao_refs, a_pro(a_vals, cvals)):
                    r[...] = v

            a_vals = [r[...] for r in ao_refs]
        if chunk:
            spans = [slice(lo, min(lo + chunk, bn)) for lo in range(0, bn, chunk)]
            chunks = []
            for cs in spans:
                parts = [None] * n_acc
                for ai, bi, ci in pairs:
                    p = lax.dot_general(a_vals[ai], b_refs[bi][cs, :] if tb else b_refs[bi][:, cs], dn,
                                        preferred_element_type=F32)
                    parts[ci] = p if parts[ci] is None else parts[ci] + p
                chunks.append(parts)
            for cs, parts in zip(spans, chunks):
                ex = [e[:, cs] for e in e_refs]
                outs = fin(parts, ex, cvals) if nc else fin(parts, ex)
                for o_ref, o in zip(o_refs, outs):
                    o_ref[:, cs] = o.astype(o_ref.dtype)
            return

        parts = [None] * n_acc
        for ai, bi, ci in pairs:
            p = lax.dot_general(a_vals[ai], b_refs[bi][...], dn, preferred_element_type=F32)
            parts[ci] = p if parts[ci] is None else parts[ci] + p

        def finish(accs):
            ex = [e[...] for e in e_refs]
            res = fin(accs, ex, cvals) if nc else fin(accs, ex)
            outs, sums = res if n_sums else (res, ())
            for o_ref, o in zip(o_refs, outs):
                o_ref[...] = o.astype(o_ref.dtype)
            if n_sums:
                @pl.when(first)
                def _():
                    for s_ref, s in zip(s_refs, sums):
                        s_ref[...] = s

                @pl.when(jnp.logical_not(first))
                def _():
                    for s_ref, s in zip(s_refs, sums):
                        s_ref[...] += s

        if nk == 1:
            finish(parts)
        else:
            @pl.when(kk == 0)
            def _():
                for c in range(n_acc):
                    acc_refs[c][...] = parts[c]

            @pl.when(kk > 0)
            def _():
                for c in range(n_acc):
                    acc_refs[c][...] += parts[c]

            @pl.when(kk == nk - 1)
            def _():
                finish([acc_refs[c][...] for c in range(n_acc)])

    def ij(f):
        return (lambda j, i, q: f(i, j, q)) if n_outer else f

    a_spec = pl.BlockSpec((bk, bm), ij(lambda i, j, q: (q, i))) if ta else pl.BlockSpec((bm, bk), ij(lambda i, j, q: (i, q)))
    a_specs = [a_spec] * na
    if a_cat:
        def part_spec(start, count):
            def col(c):
                return jnp.clip(c - start, 0, count - 1)
            if ta:
                return pl.BlockSpec((bk, bm), ij(lambda i, j, q: (q, col(i))))
            return pl.BlockSpec((bm, bk), ij(lambda i, j, q: (i, col(q))))
        a_specs = [part_spec(s, c) for s, c in zip(cat_starts, cat_counts)]

    b_mode = dict(pipeline_mode=pl.Buffered(1)) if (bn == n and nk == 1) else {}

    def b_spec(off):
        on, ok = off
        if tb:
            return pl.BlockSpec((bn, bk), ij(lambda i, j, q: (j + on, q + ok)), **b_mode)
        return pl.BlockSpec((bk, bn), ij(lambda i, j, q: (q + ok, j + on)), **b_mode)

    mn_spec = pl.BlockSpec((bm, bn), ij(lambda i, j, q: (i, j)))
    outs = pl.pallas_call(
        body,
        out_shape=[jax.ShapeDtypeStruct((m, n), d) for d in out_dtypes] + [jax.ShapeDtypeStruct((m, k), BF16)] * nao
        + [jax.ShapeDtypeStruct((8, n), F32)] * n_sums,
        grid=(n // bn, m // bm, nk) if n_outer else (m // bm, n // bn, nk),
        in_specs=a_specs + [b_spec(o) for o in b_off]
        + [pl.BlockSpec((bm, bn), ij(lambda i, j, q, o=o: (i, j + o))) for o in e_off]
        + [pl.BlockSpec(c.shape, lambda *_, nd=c.ndim: (0,) * nd) for c in consts],
        out_specs=[mn_spec] * no + [a_spec] * nao + [pl.BlockSpec((8, n), lambda *_: (0, 0))] * n_sums,
        scratch_shapes=[pltpu.VMEM((bm, bn), F32) for _ in range(n_acc if nk > 1 else 0)],
        compiler_params=_cparams(("arbitrary" if n_sums else "parallel", "parallel", "arbitrary")),
        name=name,
    )(*a_list, *b_list, *extras, *consts)
    return outs


def _first(accs, ex):
    return (accs[0],)


def _rowwise(fn, ins, consts, out_defs, sum_widths, *, bm, name):
    ins = [tuple(e) + (1,) * (4 - len(e)) for e in ins]
    out_defs = [tuple(e) + (1,) * (3 - len(e)) for e in out_defs]
    t = ins[0][0].shape[-2] * ins[0][3]
    bm = min(bm, t)
    assert t % bm == 0, (name, t, bm)
    ni, nc, no, ns = len(ins), len(consts), len(out_defs), len(sum_widths)
    strided = [w for _, w, _, d in ins if d > 1] + [w for w, _, d in out_defs if d > 1]

    def body(*refs):
        i_refs, c_refs = refs[:ni], refs[ni:ni + nc]
        o_refs, s_refs = refs[ni + nc:ni + nc + no], refs[ni + nc + no:ni + nc + no + ns]
        scratch = list(refs[ni + nc + no + ns:])
        vals = []
        for ref, (_, w, _, d) in zip(i_refs, ins):
            if d == 1:
                vals.append(ref[...])
                continue
            s = scratch.pop(0)
            for r in range(d):
                for c in range(w // 128):
                    s.at[c][pl.ds(r, bm // d, stride=d), :] = ref[r, :, c * 128:(c + 1) * 128].astype(F32)
            vals.append(jnp.concatenate([s[c] for c in range(w // 128)], axis=1))
        outs, sums = fn(vals, [r[...] for r in c_refs])
        for o_ref, o, (w, _, d) in zip(o_refs, outs, out_defs):
            if d == 1:
                o_ref[...] = o.astype(o_ref.dtype)
                continue
            s = scratch.pop(0)
            for c in range(w // 128):
                s[c] = o[:, c * 128:(c + 1) * 128].astype(F32)
            for r in range(d):
                for c in range(w // 128):
                    o_ref[r, :, c * 128:(c + 1) * 128] = s.at[c][pl.ds(r, bm // d, stride=d), :].astype(o_ref.dtype)
        if ns:
            first = pl.program_id(0) == 0

            @pl.when(first)
            def _():
                for s_ref, s in zip(s_refs, sums):
                    s_ref[...] = s

            @pl.when(jnp.logical_not(first))
            def _():
                for s_ref, s in zip(s_refs, sums):
                    s_ref[...] += s

    def win(width, cb, d):
        if d > 1:
            return pl.BlockSpec((d, bm // d, width), lambda i: (0, i, 0))
        return pl.BlockSpec((bm, width), lambda i: (i, cb))

    res = pl.pallas_call(
        body,
        out_shape=[jax.ShapeDtypeStruct((t, w) if d == 1 else (d, t // d, w), dt) for w, dt, d in out_defs]
        + [jax.ShapeDtypeStruct((8, w), F32) for w in sum_widths],
        grid=(t // bm,),
        in_specs=[win(w, cb, d) for _, w, cb, d in ins] + [pl.BlockSpec(c.shape, lambda i, nd=c.ndim: (0,) * nd) for c in consts],
        out_specs=[win(w, 0, d) for w, _, d in out_defs] + [pl.BlockSpec((8, w), lambda i: (0, 0)) for w in sum_widths],
        scratch_shapes=[pltpu.VMEM((w // 128, bm, 128), F32) for w in strided],
        compiler_params=_cparams(("arbitrary",) if ns else ("parallel",)),
        name=name,
    )(*[e[0] for e in ins], *consts)
    return res[:no], [jnp.sum(s, axis=0) for s in res[no:]]


def _colsum8(x):
    bm, w = x.shape
    return jnp.sum(x.reshape(bm // 8, 8, w), axis=0)


def _tri(n, upper=False):
    r = lax.broadcasted_iota(jnp.int32, (n, n), 0)
    c = lax.broadcasted_iota(jnp.int32, (n, n), 1)
    return (c >= r) if upper else (c <= r)


def _exact_tri_matmul(tri_bf16, x):
    x0 = x.astype(BF16)
    r1 = x - x0.astype(F32)
    x1 = r1.astype(BF16)
    x2 = (r1 - x1.astype(F32)).astype(BF16)
    w = x.shape[1]
    y = jnp.dot(tri_bf16, jnp.concatenate([x0, x1, x2], axis=1), preferred_element_type=F32)
    return y[:, :w] + y[:, w:2 * w] + y[:, 2 * w:]


def _dot_nt(a, b):
    return lax.dot_general(a, b, (((1,), (1,)), ((), ())), preferred_element_type=F32)


def _dot_tn(a, b):
    return lax.dot_general(a, b, (((0,), (0,)), ((), ())), preferred_element_type=F32)


def _dot(a, b):
    return jnp.dot(a, b, preferred_element_type=F32)


def _hg_gates(hq, hf, lb):
    sq = _sigmoid(hq)
    q = hq * sq
    sg = _sigmoid(hf)
    f = lb + (1.0 - lb) * sg
    return q, sq, sg, f


def _hg_heads(x, hp):
    return [x[:, h * HG_DK:(h + 1) * HG_DK] for h in range(hp)]


def _hg_intra_wide(q, kk, g, hp):
    c = q.shape[0]
    rows = lax.broadcasted_iota(jnp.int32, (c, 1), 0)
    a_rows = [[] for _ in range(hp)]
    qts, kts, eqs, eks = [], [], [], []
    for i in range(c // HG_SUB):
        lo = i * HG_SUB
        ref = g[lo - 1:lo, :] if i else jnp.zeros_like(g[0:1, :])
        eq = jnp.exp(g[lo:lo + HG_SUB, :] - ref)
        ek = jnp.exp(jnp.where(rows < lo + HG_SUB, ref - g, 0.0))
        qtb = (q[lo:lo + HG_SUB, :] * eq).astype(BF16)
        ktb = (kk * ek).astype(BF16)
        tpos = lo + lax.broadcasted_iota(jnp.int32, (HG_SUB, c), 0)
        spos = lax.broadcasted_iota(jnp.int32, (HG_SUB, c), 1)
        for h, (qh, kh) in enumerate(zip(_hg_heads(qtb, hp), _hg_heads(ktb, hp))):
            a_rows[h].append(jnp.where(spos <= tpos, _dot_nt(qh, kh), 0.0))
        qts.append(qtb), kts.append(ktb), eqs.append(eq), eks.append(ek)
    return [jnp.concatenate(r, axis=0) for r in a_rows], qts, kts, eqs, eks


def _hgrn_fwd(zh, lb3, *, tb=512):
    t = zh.shape[0]
    nh = lb3.shape[0]
    c = HG_CHUNK
    tb = min(tb, t)
    nchunk = tb // c
    hp = HG_HP if nh % HG_HP == 0 else 1
    wp = hp * HG_DK

    def body(hq_ref, hf_ref, hi_ref, lb_ref, o_ref, st_ref, state):
        @pl.when(pl.program_id(1) == 0)
        def _():
            state[...] = jnp.zeros_like(state)

        tril = _tri(c).astype(BF16)

        def chunk(ci, carry):
            sl = pl.ds(pl.multiple_of(ci * c, c), c)
            q, _, _, f = _hg_gates(hq_ref[sl, :], hf_ref[sl, :], lb_ref[...])
            kk = 1.0 - f
            g = _exact_tri_matmul(tril, jnp.log(f))
            a, _, _, _, _ = _hg_intra_wide(q, kk, g, hp)
            vb = hi_ref[sl, :].astype(BF16)
            glast = g[c - 1:c, :]
            qgb = (q * jnp.exp(g)).astype(BF16)
            kgb = (kk * jnp.exp(glast - g)).astype(BF16)
            dec = jnp.exp(glast)
            sts = [state[h] for h in range(hp)]
            for h in range(hp):
                st_ref[h, ci] = sts[h]
            vh, qgh, kgh, dech = _hg_heads(vb, hp), _hg_heads(qgb, hp), _hg_heads(kgb, hp), _hg_heads(dec, hp)
            o = [_dot(a[h].astype(BF16), vh[h]) + _dot_nt(qgh[h], sts[h].astype(BF16)) for h in range(hp)]
            new = [_dot_tn(vh[h], kgh[h]) for h in range(hp)]
            o_ref[sl, :] = jnp.concatenate(o, axis=1)
            for h in range(hp):
                state[h] = sts[h] * dech[h] + new[h]
            return carry

        lax.fori_loop(0, nchunk, chunk, 0)

    def col(cb):
        return pl.BlockSpec((tb, wp), lambda h, i: (i, cb * (nh // hp) + h))

    return pl.pallas_call(
        body,
        out_shape=[jax.ShapeDtypeStruct((t, nh * HG_DK), F32), jax.ShapeDtypeStruct((nh, t // c, HG_DK, HG_DK), F32)],
        grid=(nh // hp, t // tb),
        in_specs=[col(0), col(1), col(2), pl.BlockSpec((1, wp), lambda h, i: (0, h))],
        out_specs=[pl.BlockSpec((tb, wp), lambda h, i: (i, h)),
                   pl.BlockSpec((hp, nchunk, HG_DK, HG_DK), lambda h, i: (h, i, 0, 0))],
        scratch_shapes=[pltpu.VMEM((hp, HG_DK, HG_DK), F32)],
        compiler_params=_cparams(("parallel", "arbitrary")),
        name="hgrn_fwd",
    )(zh, zh, zh, lb3.reshape(1, -1))


def _hgrn_bwd(zh, lb3, states, d_o, *, tb=512):
    t = zh.shape[0]
    nh = lb3.shape[0]
    c = HG_CHUNK
    tb = min(tb, t)
    nchunk = tb // c
    nblk = t // tb
    hp = HG_HP if nh % HG_HP == 0 else 1
    wp = hp * HG_DK

    def body(hq_ref, hf_ref, hi_ref, lb_ref, st_ref, do_ref, dq_ref, df_ref, dv_ref, dlb_ref, dstate):
        @pl.when(pl.program_id(1) == 0)
        def _():
            dstate[...] = jnp.zeros_like(dstate)
            dlb_ref[...] = jnp.zeros_like(dlb_ref)

        tril = _tri(c).astype(BF16)
        triu = _tri(c, upper=True).astype(BF16)
        last_row = lax.broadcasted_iota(jnp.int32, (c, 1), 0) == c - 1
        heads = range(hp)

        def chunk(j, carry):
            ci = nchunk - 1 - j
            sl = pl.ds(pl.multiple_of(ci * c, c), c)
            lb = lb_ref[...]
            hq, hf = hq_ref[sl, :], hf_ref[sl, :]
            q, sq, sg, f = _hg_gates(hq, hf, lb)
            kk = 1.0 - f
            g = _exact_tri_matmul(tril, jnp.log(f))
            a, qts, kts, eqs, eks = _hg_intra_wide(q, kk, g, hp)
            glast = g[c - 1:c, :]
            eg, egl, dec = jnp.exp(g), jnp.exp(glast - g), jnp.exp(glast)
            vb, dob = hi_ref[sl, :].astype(BF16), do_ref[sl, :].astype(BF16)
            qgb, kgb = (q * eg).astype(BF16), (kk * egl).astype(BF16)
            sts = [st_ref[h, ci] for h in heads]
            dsts = [dstate[h] for h in heads]
            stb, dstb = [s.astype(BF16) for s in sts], [s.astype(BF16) for s in dsts]
            vh, doh, qgh, kgh = _hg_heads(vb, hp), _hg_heads(dob, hp), _hg_heads(qgb, hp), _hg_heads(kgb, hp)
            dv = [_dot_tn(a[h].astype(BF16), doh[h]) + _dot_nt(kgh[h], dstb[h]) for h in heads]
            da = [jnp.where(_tri(c), _dot_nt(doh[h], vh[h]), 0.0).astype(BF16) for h in heads]
            dq_inter = jnp.concatenate([_dot(doh[h], stb[h]) for h in heads], axis=1) * eg
            dk_state = jnp.concatenate([_dot(vh[h], dstb[h]) for h in heads], axis=1) * egl
            new_dst = [_dot_tn(doh[h], qgh[h]) for h in heads]
            xs, dk, dgk = [], dk_state, 0.0
            for i in range(c // HG_SUB):
                rs = slice(i * HG_SUB, (i + 1) * HG_SUB)
                kth, qth = _hg_heads(kts[i], hp), _hg_heads(qts[i], hp)
                xi = jnp.concatenate([_dot(da[h][rs, :], kth[h]) for h in heads], axis=1)
                yi = jnp.concatenate([_dot_tn(da[h][rs, :], qth[h]) for h in heads], axis=1)
                xs.append(xi)
                dk = dk + yi * eks[i]
                dgk = dgk + yi * kts[i].astype(F32)
            dq = jnp.concatenate([x * e for x, e in zip(xs, eqs)], axis=0) + dq_inter
            dgq = jnp.concatenate([x * qt.astype(F32) for x, qt in zip(xs, qts)], axis=0)
            dg = dgq - dgk + q * dq_inter - kk * dk_state
            sdot = jnp.concatenate([jnp.sum(sts[h] * dsts[h], axis=0, keepdims=True) for h in heads], axis=1)
            dgl = jnp.sum(kk * dk_state, axis=0, keepdims=True) + dec * sdot
            dg = dg + jnp.where(last_row, dgl, 0.0)
            dlogf = _exact_tri_matmul(triu, dg)
            dfv = dlogf / f - dk
            dq_ref[sl, :] = (dq * (sq * (1.0 + hq * (1.0 - sq)))).astype(dq_ref.dtype)
            df_ref[sl, :] = (dfv * (1.0 - lb) * sg * (1.0 - sg)).astype(df_ref.dtype)
            dv_ref[sl, :] = jnp.concatenate(dv, axis=1).astype(dv_ref.dtype)
            dlb_ref[...] += jnp.sum(dfv * (1.0 - sg), axis=0, keepdims=True)
            dech = _hg_heads(dec, hp)
            for h in heads:
                dstate[h] = dsts[h] * dech[h] + new_dst[h]
            return carry

        lax.fori_loop(0, nchunk, chunk, 0)

    def col(cb):
        return pl.BlockSpec((tb, wp), lambda h, i: (nblk - 1 - i, cb * (nh // hp) + h))

    ocol = pl.BlockSpec((tb, wp), lambda h, i: (nblk - 1 - i, h))
    lbspec = pl.BlockSpec((1, wp), lambda h, i: (0, h))
    w = nh * HG_DK
    dq, df, dv, dlb = pl.pallas_call(
        body,
        out_shape=[jax.ShapeDtypeStruct((t, w), BF16)] * 3 + [jax.ShapeDtypeStruct((1, w), F32)],
        grid=(nh // hp, nblk),
        in_specs=[col(0), col(1), col(2), lbspec,
                  pl.BlockSpec((hp, nchunk, HG_DK, HG_DK), lambda h, i: (h, nblk - 1 - i, 0, 0)), ocol],
        out_specs=[ocol, ocol, ocol, lbspec],
        scratch_shapes=[pltpu.VMEM((hp, HG_DK, HG_DK), F32)],
        compiler_params=_cparams(("parallel", "arbitrary")),
        name="hgrn_bwd",
    )(zh, zh, zh, lb3.reshape(1, -1), states, d_o)
    return dq, df, dv, dlb.reshape(w)


NEG = -1e30
ATT_GW = ATT_HEADS * ATT_DH


def _att_scores(q, kp, kc, has_prev):
    scale = ATT_DH ** -0.5
    i = lax.broadcasted_iota(jnp.int32, (ATT_BLK, ATT_BLK), 0)
    j = lax.broadcasted_iota(jnp.int32, (ATT_BLK, ATT_BLK), 1)
    s_p = jnp.where(jnp.logical_and(j >= i, has_prev), _dot_nt(q, kp) * scale, NEG)
    s_c = jnp.where(j <= i, _dot_nt(q, kc) * scale, NEG)
    return s_p, s_c


def _att_views(arrs, d):
    return [a.reshape(d, -1, ATT_GW) for a in arrs]


def _att_unview(a, d):
    return a.reshape(-1, ATT_GW) if d == 1 else a


ATT_QB = 4


def _attn_fwd(qb, kb, vb, g):
    d = ATT_PATTERNS[g][1]
    q2, k2, v2 = _att_views([qb, kb, vb], d)
    nblk = q2.shape[1] // ATT_BLK
    nq = ATT_QB if nblk % ATT_QB == 0 else 1
    rows = nq * ATT_BLK

    def body(q_ref, kc_ref, kp_ref, vc_ref, vp_ref, o_ref, l_ref):
        first = pl.program_id(1) == 0
        hss = [slice(h * ATT_DH, (h + 1) * ATT_DH) for h in range(ATT_HEADS)]
        for b in range(nq):
            rs = slice(b * ATT_BLK, (b + 1) * ATT_BLK)
            ps = slice((b - 1) * ATT_BLK, b * ATT_BLK)
            has_prev = jnp.logical_not(first) if b == 0 else True
            kv = [(kp_ref[:, hs], vp_ref[:, hs]) if b == 0 else (kc_ref[ps, hs], vc_ref[ps, hs]) for hs in hss]
            sc = [_att_scores(q_ref[rs, hs], kv[h][0], kc_ref[rs, hs], has_prev) for h, hs in enumerate(hss)]
            ms = [jnp.maximum(jnp.max(s_p, axis=1, keepdims=True), jnp.max(s_c, axis=1, keepdims=True)) for s_p, s_c in sc]
            ps_ = [(jnp.exp(s_p - m), jnp.exp(s_c - m)) for (s_p, s_c), m in zip(sc, ms)]
            ls = [jnp.sum(p_p, axis=1, keepdims=True) + jnp.sum(p_c, axis=1, keepdims=True) for p_p, p_c in ps_]
            os_ = [_dot(p_p.astype(BF16), kv[h][1]) + _dot(p_c.astype(BF16), vc_ref[rs, hss[h]]) for h, (p_p, p_c) in enumerate(ps_)]
            for h, hs in enumerate(hss):
                o_ref[rs, hs] = os_[h] / ls[h]
                l_ref[rs, hs] = jnp.broadcast_to(ms[h] + jnp.log(ls[h]), (ATT_BLK, ATT_DH))

    cur = pl.BlockSpec((None, rows, ATT_GW), lambda r, n: (r, n, 0))
    prev = pl.BlockSpec((None, ATT_BLK, ATT_GW), lambda r, n: (r, jnp.maximum(n * nq - 1, 0), 0))
    o, lse = pl.pallas_call(
        body,
        out_shape=[jax.ShapeDtypeStruct(q2.shape, F32)] * 2,
        grid=(d, nblk // nq),
        in_specs=[cur, cur, prev, cur, prev],
        out_specs=[cur, cur],
        compiler_params=_cparams(("parallel", "arbitrary")),
        name=f"attn_fwd_g{g}",
    )(q2, k2, k2, v2, v2)
    return _att_unview(o, d), _att_unview(lse, d)


def _attn_bwd(qb, kb, vb, o, lse, d_o, d_lse, g):
    d = ATT_PATTERNS[g][1]
    q2, k2, v2 = _att_views([qb, kb, vb], d)
    o2, l2, do2, dl2 = _att_views([o, lse, d_o, d_lse], d)
    nblk = q2.shape[1] // ATT_BLK
    nq = ATT_QB if nblk % ATT_QB == 0 else 1
    rows = nq * ATT_BLK
    ns = nblk // nq
    scale = ATT_DH ** -0.5

    def body(q_ref, kc_ref, kp_ref, vc_ref, vp_ref, o_ref, l_ref, do_ref, dl_ref, dq_ref, dk_ref, dv_ref, ck, cv):
        n = pl.program_id(1)

        @pl.when(n == 0)
        def _():
            ck[...] = jnp.zeros_like(ck)
            cv[...] = jnp.zeros_like(cv)

        first = n == ns - 1
        hss = [slice(h * ATT_DH, (h + 1) * ATT_DH) for h in range(ATT_HEADS)]
        heads = range(ATT_HEADS)
        pend_k, pend_v = [ck[:, hs] for hs in hss], [cv[:, hs] for hs in hss]
        for b in reversed(range(nq)):
            rs = slice(b * ATT_BLK, (b + 1) * ATT_BLK)
            ps = slice((b - 1) * ATT_BLK, b * ATT_BLK)
            has_prev = jnp.logical_not(first) if b == 0 else True
            q = [q_ref[rs, hs] for hs in hss]
            kc, vc = [kc_ref[rs, hs] for hs in hss], [vc_ref[rs, hs] for hs in hss]
            kp = [kp_ref[:, hs] if b == 0 else kc_ref[ps, hs] for hs in hss]
            vp = [vp_ref[:, hs] if b == 0 else vc_ref[ps, hs] for hs in hss]
            sc = [_att_scores(q[h], kp[h], kc[h], has_prev) for h in heads]
            dob = [do_ref[rs, hs].astype(BF16) for hs in hss]
            dp = [(_dot_nt(dob[h], vp[h]), _dot_nt(dob[h], vc[h])) for h in heads]
            delta = [jnp.sum(do_ref[rs, hs] * o_ref[rs, hs] - dl_ref[rs, hs], axis=1, keepdims=True) for hs in hss]
            pr = [(jnp.exp(sc[h][0] - l_ref[rs, hss[h]][:, 0:1]), jnp.exp(sc[h][1] - l_ref[rs, hss[h]][:, 0:1])) for h in heads]
            ds = [((pr[h][0] * (dp[h][0] - delta[h]) * scale).astype(BF16), (pr[h][1] * (dp[h][1] - delta[h]) * scale).astype(BF16))
                  for h in heads]
            pb = [(pr[h][0].astype(BF16), pr[h][1].astype(BF16)) for h in heads]
            dq = [_dot(ds[h][0], kp[h]) + _dot(ds[h][1], kc[h]) for h in heads]
            dk_c = [_dot_tn(ds[h][1], q[h]) for h in heads]
            dv_c = [_dot_tn(pb[h][1], dob[h]) for h in heads]
            dk_p = [_dot_tn(ds[h][0], q[h]) for h in heads]
            dv_p = [_dot_tn(pb[h][0], dob[h]) for h in heads]
            for h, hs in enumerate(hss):
                dq_ref[rs, hs] = dq[h]
                dk_ref[rs, hs] = pend_k[h] + dk_c[h]
                dv_ref[rs, hs] = pend_v[h] + dv_c[h]
            pend_k, pend_v = dk_p, dv_p
        for h, hs in enumerate(hss):
            ck[:, hs] = pend_k[h]
            cv[:, hs] = pend_v[h]

    cur = pl.BlockSpec((None, rows, ATT_GW), lambda r, n: (r, ns - 1 - n, 0))
    prev = pl.BlockSpec((None, ATT_BLK, ATT_GW), lambda r, n: (r, jnp.maximum((ns - 1 - n) * nq - 1, 0), 0))
    shp = jax.ShapeDtypeStruct(q2.shape, F32)
    dq, dk, dv = pl.pallas_call(
        body,
        out_shape=[shp, shp, shp],
        grid=(d, ns),
        in_specs=[cur, cur, prev, cur, prev, cur, cur, cur, cur],
        out_specs=[cur, cur, cur],
        scratch_shapes=[pltpu.VMEM((ATT_BLK, ATT_GW), F32), pltpu.VMEM((ATT_BLK, ATT_GW), F32)],
        compiler_params=_cparams(("parallel", "arbitrary")),
        name=f"attn_bwd_g{g}",
    )(q2, k2, k2, v2, v2, o2, l2, do2, dl2)
    return _att_unview(dq, d), _att_unview(dk, d), _att_unview(dv, d)


def _rms_parts(x, width):
    outs = []
    for lo in range(0, x.shape[1], width):
        xs = x[:, lo:lo + width].astype(F32)
        r = lax.rsqrt(jnp.mean(xs * xs, axis=1, keepdims=True) + EPS)
        outs.append((xs * r, r))
    return outs


def _rms_bwd_part(xh, r, dxh):
    return r * (dxh - xh * jnp.mean(dxh * xh, axis=1, keepdims=True))


def _norm_pro(a, consts):
    (xh, _), = _rms_parts(a[0], a[0].shape[1])
    return [(xh * consts[0]).astype(BF16)]


def _norm_bwd_fin(accs, ex, consts):
    xv, dres = ex
    (xh, r), = _rms_parts(xv, xv.shape[1])
    dx = dres + _rms_bwd_part(xh, r, accs[0] * consts[0])
    return [dx, dx], [_colsum8(accs[0] * xh)]


def _rot_sign():
    lane = lax.broadcasted_iota(jnp.int32, (1, ATT_DH), 1)
    return jnp.where(lane < ATT_DH // 2, -1.0, 1.0).astype(F32)


def _rope(y, cos, sin):
    return y * cos + pltpu.roll(y, ATT_DH // 2, axis=1) * _rot_sign() * sin


def _rope_t(dy, cos, sin):
    return dy * cos - pltpu.roll(dy * sin, ATT_DH // 2, axis=1) * _rot_sign()


def _qk_prep(zq, zk, zv, qn, kn, cos, sin):
    w = zq.shape[1]

    def fn(ins, consts):
        cs, sn = ins[3], ins[4]
        outs = []
        for z, gain in ((ins[0], consts[0]), (ins[1], consts[1])):
            for i, (xh, _) in enumerate(_rms_parts(z, ATT_DH)):
                outs.append(_rope(xh * gain[:, i * ATT_DH:(i + 1) * ATT_DH], cs, sn))
        outs += [ins[2][:, i * ATT_DH:(i + 1) * ATT_DH] for i in range(w // ATT_DH)]
        groups = [jnp.concatenate(outs[i:i + ATT_HEADS], axis=1) for i in range(0, len(outs), ATT_HEADS)]
        return groups, []

    outs, _ = _rowwise(fn, [(zq, w, 0), (zk, w, 0), (zv, w, 0), (cos, ATT_DH, 0), (sin, ATT_DH, 0)], [qn, kn],
                       [(ATT_GW, BF16, ATT_PATTERNS[g][1]) for g in range(ATT_GROUPS)] * 3, [], bm=256, name="qk_prep")
    return outs[0:3], outs[3:6], outs[6:9]


def _qk_prep_bwd(zq, zk, dq_g, dk_g, dv_g, qn, kn, cos, sin):
    w = zq.shape[1]

    def fn(ins, consts):
        cs, sn = ins[2], ins[3]
        outs, sums = [], []
        for z, gain, dparts in ((ins[0], consts[0], ins[4:7]), (ins[1], consts[1], ins[7:10])):
            dout = jnp.concatenate(dparts, axis=1)
            dz, dgain = [], []
            for i, (xh, r) in enumerate(_rms_parts(z, ATT_DH)):
                hs = slice(i * ATT_DH, (i + 1) * ATT_DH)
                dy = _rope_t(dout[:, hs], cs, sn)
                dgain.append(_colsum8(dy * xh))
                dz.append(_rms_bwd_part(xh, r, dy * gain[:, hs]))
            outs.append(jnp.concatenate(dz, axis=1))
            sums.append(jnp.concatenate(dgain, axis=1))
        outs.append(jnp.concatenate(ins[10:13], axis=1))
        return outs, sums

    ins = [(zq, w, 0), (zk, w, 0), (cos, ATT_DH, 0), (sin, ATT_DH, 0)]
    for parts in (dq_g, dk_g, dv_g):
        ins += [(a, ATT_GW, 0, ATT_PATTERNS[g][1]) for g, a in enumerate(parts)]
    (dzq, dzk, dzv), (dqn, dkn) = _rowwise(fn, ins, [qn, kn], [(w, BF16)] * 3, [w, w], bm=256, name="qk_prep_bwd")
    return dzq, dzk, dzv, dqn, dkn


def _post_a(o_raw, zh, gout):
    w = o_raw.shape[1]

    def fn(ins, consts):
        oh = jnp.concatenate([xh for xh, _ in _rms_parts(ins[0], HG_DK)], axis=1)
        hg = ins[1]
        return [oh * consts[0] * (hg * _sigmoid(hg))], []

    (y,), _ = _rowwise(fn, [(o_raw, w, 0), (zh, w, 3)], [gout.reshape(1, w)], [(w, BF16)], [], bm=512, name="post_a")
    return y


def _post_a_bwd(o_raw, zh, gout, dy):
    w = o_raw.shape[1]

    def fn(ins, consts):
        parts = _rms_parts(ins[0], HG_DK)
        oh = jnp.concatenate([xh for xh, _ in parts], axis=1)
        hg, dyv, gain = ins[1], ins[2], consts[0]
        sg = _sigmoid(hg)
        s = hg * sg
        doh = dyv * gain * s
        do = jnp.concatenate([_rms_bwd_part(xh, r, doh[:, i * HG_DK:(i + 1) * HG_DK]) for i, (xh, r) in enumerate(parts)], axis=1)
        dhg = dyv * oh * gain * (sg * (1.0 + hg * (1.0 - sg)))
        return [do, dhg], [_colsum8(dyv * oh * s)]

    (do, dhg), (dgain,) = _rowwise(fn, [(o_raw, w, 0), (zh, w, 3), (dy, w, 0)], [gout.reshape(1, w)],
                                   [(w, F32), (w, BF16)], [w], bm=512, name="post_a_bwd")
    return do, dhg, dgain


def _merge_alpha(lses):
    m = jnp.maximum(jnp.maximum(lses[0], lses[1]), lses[2])
    e = [jnp.exp(l - m) for l in lses]
    inv = 1.0 / (e[0] + e[1] + e[2])
    return [x * inv for x in e]


def _group_ins(parts):
    return [(a, ATT_GW, 0, ATT_PATTERNS[g][1]) for g, a in enumerate(parts)]


def _merge_b(o_g, lse_g):
    def fn(ins, consts):
        al = _merge_alpha(ins[3:6])
        return [al[0] * ins[0] + al[1] * ins[1] + al[2] * ins[2]], []

    (y,), _ = _rowwise(fn, _group_ins(o_g) + _group_ins(lse_g), [], [(ATT_GW, BF16)], [], bm=512, name="merge_b")
    return y


def _merge_b_bwd(o_g, lse_g, dy):
    def fn(ins, consts):
        al = _merge_alpha(ins[3:6])
        dyv = ins[6]
        dal = [dyv * ins[i] for i in range(3)]
        tot = al[0] * dal[0] + al[1] * dal[1] + al[2] * dal[2]
        return [al[i] * dyv for i in range(3)] + [al[i] * (dal[i] - tot) for i in range(3)], []

    outs, _ = _rowwise(fn, _group_ins(o_g) + _group_ins(lse_g) + [(dy, ATT_GW, 0)], [],
                       [(ATT_GW, F32, ATT_PATTERNS[g][1]) for g in range(ATT_GROUPS)] * 2, [], bm=512, name="merge_b_bwd")
    return outs[:3], outs[3:]


def _loss_head(y, target):
    d = y.shape[1]

    def fn(ins, consts):
        e = ins[0] - ins[1]
        return [e * (1.0 / d)] * 2, [_colsum8(e * e)]

    (dy, dyb), (sq,) = _rowwise(fn, [(y, d, 0), (target, d, 0)], [], [(d, F32), (d, BF16)], [d], bm=512, name="loss_head")
    return 0.5 * jnp.sum(sq) / d, dy, dyb


def _ffn_fwd(x, gain, wt, wo_fn, tag):
    t, d = x.shape
    f = wt.shape[0] // 2

    def act(accs, ex, consts):
        a, b = accs
        s = _sigmoid(a)
        sa = a * s
        return (sa * b, b, 0.5 * sa, 0.5 * (s + sa * (1.0 - s)))

    bn = FFN_BN if f % FFN_BN == 0 else 256
    u, b, sa, sp, h = _mm([x], [wt, wt], [(0, 0, 0), (0, 1, 1)], 2, act, [BF16] * 4, m=t, n=f, k=d, tb=True,
                          bm=512, bn=bn, bk=d, b_off=[(0, 0), (f // min(bn, f), 0)],
                          consts=[gain.reshape(1, d)], a_pro=_norm_pro, chunk=MXU_COLS, name=f"ffn_in_{tag}")
    wo = wo_fn(u)
    (y,) = _mm([u], [wo], [(0, 0, 0)], 1, lambda accs, ex: (ex[0] + 0.5 * accs[0],), [F32], m=t, n=d, k=f,
               bm=512, bn=d, bk=f, extras=[x], name=f"ffn_out_{tag}")
    return y, (x, h, u, b, sa, sp, wo)


def _ffn_bwd(dy, dyb, saved, gain, wt, tag, tok, emit):
    x, h, u, b, sa, sp, wo = saved
    t, d = x.shape
    f = wo.shape[0]

    def dact(accs, ex, consts):
        bv, sav, spv = (e.astype(F32) for e in ex)
        return (accs[0] * bv * spv, accs[0] * sav)

    bn = FFN_BN if f % FFN_BN == 0 else 256
    da, db = _mm([dyb], [wo], [(0, 0, 0)], 1, dact, [BF16, BF16], m=t, n=f, k=d, tb=True, bm=512, bn=bn, bk=d,
                 extras=[b, sa, sp], n_outer=True, chunk=MXU_COLS, consts=[tok], name=f"ffn_dact_{tag}")
    (dwo,) = _mm([u], [dyb], [(0, 0, 0)], 1, lambda accs, ex: (0.5 * accs[0],), [BF16], m=f, n=d, k=t, ta=True,
                 bm=1408, bn=d, bk=1024, name=f"ffn_dwo_{tag}")
    (dwt,) = _mm([da, db], [h], [(0, 0, 0)], 1, _first, [BF16], m=2 * f, n=d, k=t, ta=True, bm=min(1408, f), bn=d,
                 bk=1024, a_cat=True, name=f"ffn_dwt_{tag}")
    tok = emit(dwt, dwo)
    bk = min(FFN_BN, f)
    dx, dxb, dgain = _mm([da, db], [wt, wt], [(0, 0, 0), (1, 1, 0)], 1, _norm_bwd_fin, [F32, BF16], m=t, n=d, k=f,
                         bm=512, bn=d, bk=bk, b_off=[(0, 0), (0, f // bk)], extras=[x, dy],
                         consts=[gain.reshape(1, d), tok], n_sums=1, name=f"ffn_dh_{tag}")
    return dx, dxb, jnp.sum(dgain, axis=0), tok


FFN_BN = 2816
Z_SPLITS = (("h", 4096), ("q", 1536), ("k", 1536), ("v", 1536), ("g", 2048))


def _mix_fwd(x, p, cos, sin):
    t, d = x.shape
    z, off, hm = {}, 0, None
    for nm, width in Z_SPLITS:
        bn = 1024 if off % 1024 == 0 and width % 1024 == 0 else 512
        first = hm is None
        res = _mm([x if first else hm], [p["wint"]], [(0, 0, 0)], 1, (lambda accs, ex, consts: (accs[0],)) if first else _first,
                  [F32 if nm == "h" else BF16], m=t, n=width, k=d, tb=True, bm=1024 if first else 2048, bn=bn, bk=d,
                  b_off=[(off // bn, 0)],
                  consts=[p["gm"].reshape(1, d)] if first else (), a_pro=_norm_pro if first else None, name=f"mix_in_{nm}")
        z[nm] = res[0]
        hm = res[1] if first else hm
        off += width
    o_raw, states = _hgrn_fwd(z["h"], p["lb3"])
    qb, kb, vb = _qk_prep(z["q"], z["k"], z["v"], p["qn"], p["kn"], cos, sin)
    o_g, lse_g = zip(*[_attn_fwd(qb[g], kb[g], vb[g], g) for g in range(ATT_GROUPS)])
    oa = _post_a(o_raw, z["h"], p["gout"])
    ob = _merge_b(o_g, lse_g)
    late = p["late"](ob)
    p = dict(p, **late)
    (ya,) = _mm([oa], [p["wa"]], [(0, 0, 0)], 1, _first, [F32], m=t, n=d, k=oa.shape[1], bm=1024, bn=d, bk=oa.shape[1],
                name="branch_a")

    def gate(accs, ex):
        return (_sigmoid(ex[0].astype(F32)) * ex[2] + _sigmoid(ex[1].astype(F32)) * accs[0], accs[0])

    merged, yb = _mm([ob], [p["wbt"]], [(0, 0, 0)], 1, gate, [BF16, F32], m=t, n=d, k=ATT_GW, tb=True, bm=512, bn=d,
                     bk=ATT_GW, extras=[z["g"], z["g"], ya], e_off=[0, 1, 0], chunk=MXU_COLS, name="branch_b_gate")
    (y,) = _mm([merged], [p["wo"]], [(0, 0, 0)], 1, lambda accs, ex: (ex[0] + accs[0],), [F32], m=t, n=d, k=d,
               bm=1024, bn=d, bk=d, extras=[x], name="mix_out")
    return y, (x, hm, z, o_raw, states, qb, kb, vb, o_g, lse_g, oa, ob, ya, yb, merged, late)


def _mix_bwd(dy, dyb, saved, p, cos, sin, tok):
    x, hm, z, o_raw, states, qb, kb, vb, o_g, lse_g, oa, ob, ya, yb, merged, late = saved
    p = dict(p, **late)
    t, d = x.shape
    w = oa.shape[1]

    def dgate(accs, ex, consts):
        dm = accs[0]
        sa, sb = _sigmoid(ex[0].astype(F32)), _sigmoid(ex[1].astype(F32))
        return (sa * dm, sb * dm, dm * ex[2] * sa * (1.0 - sa), dm * ex[3] * sb * (1.0 - sb))

    dya, dyb_, dga, dgb = _mm([dyb], [p["wo"]], [(0, 0, 0)], 1, dgate, [BF16] * 4, m=t, n=d, k=d, tb=True, bm=512, bn=d,
                              bk=d, extras=[z["g"], z["g"], ya, yb], e_off=[0, 1, 0, 0], chunk=MXU_COLS, consts=[tok], name="mix_out_bwd")
    (dwo,) = _mm([merged], [dyb], [(0, 0, 0)], 1, _first, [BF16], m=d, n=d, k=t, ta=True, bm=d, bn=d, bk=1024, name="mix_dwo")
    (doa,) = _mm([dya], [p["wa"]], [(0, 0, 0)], 1, _first, [F32], m=t, n=w, k=d, tb=True, bm=1024, bn=w, bk=d, name="branch_a_bwd")
    (dwa,) = _mm([oa], [dya], [(0, 0, 0)], 1, _first, [BF16], m=w, n=d, k=t, ta=True, bm=w, bn=d, bk=1024, name="branch_a_dw")
    (dob,) = _mm([dyb_], [p["wbt"]], [(0, 0, 0)], 1, _first, [F32], m=t, n=ATT_GW, k=d, bm=1024, bn=ATT_GW, bk=d,
                 name="branch_b_bwd")
    (dwbt,) = _mm([dyb_], [ob], [(0, 0, 0)], 1, _first, [BF16], m=d, n=ATT_GW, k=t, ta=True, bm=d, bn=ATT_GW, bk=1024,
                  name="branch_b_dw")
    do_raw, dhg, dgout = _post_a_bwd(o_raw, z["h"], p["gout"], doa)
    do_g, dlse_g = _merge_b_bwd(o_g, lse_g, dob)
    dq_g, dk_g, dv_g = zip(*[_attn_bwd(qb[g], kb[g], vb[g], o_g[g], lse_g[g], do_g[g], dlse_g[g], g)
                             for g in range(ATT_GROUPS)])
    dzq, dzk, dzv, dqn, dkn = _qk_prep_bwd(z["q"], z["k"], dq_g, dk_g, dv_g, p["qn"], p["kn"], cos, sin)
    dhq, dhf, dhi, lbsum = _hgrn_bwd(z["h"], p["lb3"], states, do_raw)
    dz = jnp.concatenate([dhq, dhf, dhi, dhg, dzq, dzk, dzv, dga, dgb], axis=1)
    pw = dz.shape[1]
    (dwint,) = _mm([dz], [hm], [(0, 0, 0)], 1, _first, [BF16], m=pw, n=d, k=t, ta=True, bm=1536, bn=d, bk=2048, name="mix_in_dw")
    dx, dxb, dgm = _mm([dz], [p["wint"]], [(0, 0, 0)], 1, _norm_bwd_fin, [F32, BF16], m=t, n=d, k=pw, bm=1024, bn=d, bk=1536,
                       extras=[x, dy], consts=[p["gm"].reshape(1, d)], n_sums=1, name="mix_in_bwd")
    return dx, dxb, dict(gm=jnp.sum(dgm, axis=0), wint=dwint, lbsum=lbsum, gout=dgout, qn=dqn, kn=dkn, wa=dwa, wbt=dwbt, wo=dwo)


def _rope_tables(t):
    pos = jnp.arange(t, dtype=F32)
    inv = ROPE_THETA ** (-jnp.arange(0, ATT_DH, 2, dtype=F32) / ATT_DH)
    ang = pos[:, None] * inv[None, :]
    ang = jnp.concatenate([ang, ang], axis=-1)
    return jnp.cos(ang), jnp.sin(ang)


def _lower_bounds(logits):
    lb = jnp.cumsum(jax.nn.softmax(logits, axis=0), axis=0)
    return lb - lb[0:1]


def _head_gain(g):
    return jnp.tile(g[:, None, :], (1, ATT_HEADS, 1)).reshape(1, ATT_GROUPS * ATT_GW)


SMALL_GRADS = ("ffn1_norm", "mix_norm", "lbsum", "hgrn_out_norm", "attn_q_norm", "attn_k_norm", "ffn2_norm")


def _local_step(x, target, small, fetch, emit):
    t = x.shape[0]
    depth = small["ffn1_norm"].shape[0]
    cos, sin = _rope_tables(t)
    lb_all = _lower_bounds(small["hgrn_lb_logits"])
    saved = []
    for l in range(depth):
        w1t = fetch("w1t", l, x)["w1t"]
        x, s1 = _ffn_fwd(x, small["ffn1_norm"][l], w1t, lambda after, l=l: fetch("w1o", l, after)["w1o"], "1")
        p = dict(gm=small["mix_norm"][l], wint=fetch("wint", l, x)["wint"], lb3=lb_all[l].reshape(-1, 1, HG_DK),
                 gout=small["hgrn_out_norm"][l], qn=_head_gain(small["attn_q_norm"][l]),
                 kn=_head_gain(small["attn_k_norm"][l]), late=functools.partial(fetch, "mout", l))
        x, sm = _mix_fwd(x, p, cos, sin)
        w2t = fetch("w2t", l, x)["w2t"]
        x, s2 = _ffn_fwd(x, small["ffn2_norm"][l], w2t, lambda after, l=l: fetch("w2o", l, after)["w2o"], "2")
        saved.append((p, w1t, w2t, s1, sm, s2))
    loss, dx, dxb = _loss_head(x, target)
    gsmall = {k: [None] * depth for k in SMALL_GRADS}
    tok = jnp.zeros((8, 128), F32)
    for l in reversed(range(depth)):
        p, w1t, w2t, s1, sm, s2 = saved[l]
        dx, dxb, gsmall["ffn2_norm"][l], tok = _ffn_bwd(
            dx, dxb, s2, small["ffn2_norm"][l], w2t, "2", tok, lambda dwt, dwo, l=l: emit("ffn2", l, dict(w2t=dwt, w2o=dwo), None))
        dx, dxb, gm = _mix_bwd(dx, dxb, sm, p, cos, sin, tok)
        tok = emit("mix", l, {k: gm[k] for k in ("wint", "wa", "wbt", "wo")}, None)
        gsmall["mix_norm"][l], gsmall["lbsum"][l], gsmall["hgrn_out_norm"][l] = gm["gm"], gm["lbsum"], gm["gout"]
        for k, src in (("attn_q_norm", "qn"), ("attn_k_norm", "kn")):
            gsmall[k][l] = jnp.sum(gm[src].reshape(ATT_GROUPS, ATT_HEADS, ATT_DH), axis=1)
        dx, dxb, gsmall["ffn1_norm"][l], tok = _ffn_bwd(
            dx, dxb, s1, small["ffn1_norm"][l], w1t, "1", tok, lambda dwt, dwo, l=l: emit("ffn1", l, dict(w1t=dwt, w1o=dwo), None))
    emit("small", 0, {}, ({k: jnp.stack(v) for k, v in gsmall.items()}, loss))
    return dx


_HBM = pl.BlockSpec(memory_space=pltpu.HBM)
_SEM = pl.BlockSpec(memory_space=pltpu.SEMAPHORE)
_EFFECT = pltpu.SideEffectType.DATAFLOW_SIDE_EFFECTING


def _peer(p):
    x, y, c = lax.axis_index("x"), lax.axis_index("y"), lax.axis_index("c")
    me = 4 * x + 2 * y + c
    return (1 - x if p & 4 else x, 1 - y if p & 2 else y, 1 - c if p & 1 else c), jnp.bitwise_xor(me, p), me


def _xchg_copy(src, land, mode, send_sems, recv_sems, k, p, arriving):
    peer, peer_id, me = _peer(p)
    block = src if mode == "gather" else src.at[peer_id]
    return pltpu.make_async_remote_copy(
        src_ref=block, dst_ref=land.at[peer_id if arriving else me], send_sem=send_sems.at[k * (N_DEV - 1) + p - 1],
        recv_sem=recv_sems.at[k * (N_DEV - 1) + p - 1], device_id=peer, device_id_type=MESH)


def _xchg_start(srcs, modes, groups, name):
    n, ng = len(srcs), len(groups)

    def body(*refs):
        src = refs[:n]
        sems = refs[n:n + 2 * ng]
        land = refs[n + 2 * ng + n:n + 2 * ng + 2 * n]
        token = refs[n + 2 * ng + 2 * n]
        for gi, idx in enumerate(groups):
            for ki, k in enumerate(idx):
                for p in range(1, N_DEV):
                    _xchg_copy(src[k], land[k], modes[k], sems[2 * gi], sems[2 * gi + 1], ki, p, False).start()
        token[...] = jnp.zeros_like(token)

    sem_shapes = []
    for idx in groups:
        sem_shapes += [pltpu.SemaphoreType.DMA((len(idx) * (N_DEV - 1),))] * 2
    outs = pl.pallas_call(
        body,
        out_shape=sem_shapes + [pltpu.HBM(a.shape, a.dtype) for a in srcs]
        + [pltpu.HBM((N_DEV,) + a.shape[-2:], a.dtype) for a in srcs] + [jax.ShapeDtypeStruct((8, 128), F32)],
        in_specs=[_HBM] * n,
        out_specs=[_SEM] * (2 * ng) + [_HBM] * (2 * n) + [pl.BlockSpec(memory_space=pltpu.VMEM)],
        input_output_aliases={i: 2 * ng + i for i in range(n)},
        compiler_params=pltpu.CompilerParams(has_side_effects=_EFFECT),
        name=name,
    )(*[pltpu.with_memory_space_constraint(a, pltpu.HBM) for a in srcs])
    sems = [(outs[2 * gi], outs[2 * gi + 1]) for gi in range(ng)]
    return sems, outs[2 * ng:2 * ng + n], outs[2 * ng + n:2 * ng + 2 * n], outs[-1]


def _xchg_wait_call(srcs, lands, modes, sems, after, name):
    n = len(srcs)

    def body(*refs):
        src, land = refs[:n], refs[n:2 * n]
        send_sems, recv_sems = refs[2 * n], refs[2 * n + 1]
        for p in range(1, N_DEV):
            for k in range(n):
                cp = _xchg_copy(src[k], land[k], modes[k], send_sems, recv_sems, k, p, True)
                cp.wait_send()
                cp.wait_recv()

    outs = pl.pallas_call(
        body,
        out_shape=[pltpu.HBM(a.shape, a.dtype) for a in list(srcs) + list(lands)],
        in_specs=[_HBM] * (2 * n) + [_SEM, _SEM, pl.BlockSpec(memory_space=pl.ANY)],
        out_specs=[_HBM] * (2 * n),
        input_output_aliases={i: i for i in range(2 * n)},
        compiler_params=pltpu.CompilerParams(has_side_effects=_EFFECT),
        name=name,
    )(*srcs, *lands, sems[0], sems[1], after)
    return outs[:n], outs[n:]


def _xchg_wait(srcs, lands, modes, sems, after, name):
    srcs, lands = _xchg_wait_call(srcs, lands, modes, sems, after, name)
    me = 4 * lax.axis_index("x") + 2 * lax.axis_index("y") + lax.axis_index("c")
    done = []
    for a, land, mode in zip(srcs, lands, modes):
        own = a[None] if mode == "gather" else lax.dynamic_slice_in_dim(a, me, 1, axis=0)
        done.append(lax.dynamic_update_slice(land, own, (me, 0, 0)))
    return done


def _sum_slots(land):
    g, _, r, c = land.shape
    br = r // 2 if (r % 32 == 0 and r >= 256) else r

    def body(l_ref, o_ref):
        acc = l_ref[0, 0].astype(F32)
        for j in range(1, N_DEV):
            acc = acc + l_ref[0, j].astype(F32)
        o_ref[0] = acc

    return pl.pallas_call(
        body,
        out_shape=jax.ShapeDtypeStruct((g, r, c), F32),
        grid=(g, r // br),
        in_specs=[pl.BlockSpec((1, N_DEV, br, c), lambda i, j: (i, 0, j, 0))],
        out_specs=pl.BlockSpec((1, br, c), lambda i, j: (i, j, 0)),
        compiler_params=_cparams(("parallel", "parallel")),
        name="sum_slots",
    )(land)


def _adamw(w, g, m, v):
    shape = w.shape
    cols = shape[-1]
    rows = int(np.prod(shape[:-1]))
    bm = max(b for b in range(8, 513, 8) if rows % b == 0) if rows % 8 == 0 else rows
    c1 = 1.0 - ADAM_B1 ** ADAM_STEP
    c2 = 1.0 - ADAM_B2 ** ADAM_STEP

    def fn(ins, consts):
        wv, gv, mv, vv = ins
        m2 = ADAM_B1 * mv + (1.0 - ADAM_B1) * gv
        v2 = ADAM_B2 * vv + (1.0 - ADAM_B2) * (gv * gv)
        delta = -ADAM_LR * ((m2 / c1) / (jnp.sqrt(v2 / c2) + ADAM_EPS) + ADAM_WD * wv)
        return [delta, m2, v2], []

    outs, _ = _rowwise(fn, [(a.reshape(rows, cols), cols, 0) for a in (w, g, m, v)], [], [(cols, F32)] * 3, [],
                       bm=bm, name="adamw")
    return [o.reshape(shape) for o in outs]


BIG = ("w1t", "w1o", "wint", "wa", "wbt", "wo", "w2t", "w2o")
FETCH_GROUPS = dict(w1t=("w1t",), w1o=("w1o",), wint=("wint",), mout=("wa", "wbt", "wo"), w2t=("w2t",), w2o=("w2o",))
SMALL_ROWS = (("ffn1_norm", 0), ("mix_norm", 2), ("lbsum", 4), ("hgrn_out_norm", 6), ("ffn2_norm", 8),
              ("attn_q_norm", 10), ("attn_k_norm", 12))
SMALL_PACK_ROWS = 16


def kernel(x, ffn1_norm, ffn1_w_in, ffn1_w_out, mix_norm, w_in, hgrn_lb_logits, hgrn_out_norm, attn_q_norm, attn_k_norm, w_branch_a, w_branch_b, w_out, ffn2_norm, ffn2_w_in, ffn2_w_out, loss_target, m_ffn1_norm, m_ffn1_w_in, m_ffn1_w_out, m_mix_norm, m_w_in, m_hgrn_lb_logits, m_hgrn_out_norm, m_attn_q_norm, m_attn_k_norm, m_w_branch_a, m_w_branch_b, m_w_out, m_ffn2_norm, m_ffn2_w_in, m_ffn2_w_out, v_ffn1_norm, v_ffn1_w_in, v_ffn1_w_out, v_mix_norm, v_w_in, v_hgrn_lb_logits, v_hgrn_out_norm, v_attn_q_norm, v_attn_k_norm, v_w_branch_a, v_w_branch_b, v_w_out, v_ffn2_norm, v_ffn2_w_in, v_ffn2_w_out):
    names = ("ffn1_norm", "ffn1_w_in", "ffn1_w_out", "mix_norm", "w_in", "hgrn_lb_logits", "hgrn_out_norm", "attn_q_norm",
             "attn_k_norm", "w_branch_a", "w_branch_b", "w_out", "ffn2_norm", "ffn2_w_in", "ffn2_w_out")
    w = dict(zip(names, (ffn1_norm, ffn1_w_in, ffn1_w_out, mix_norm, w_in, hgrn_lb_logits, hgrn_out_norm, attn_q_norm,
                         attn_k_norm, w_branch_a, w_branch_b, w_out, ffn2_norm, ffn2_w_in, ffn2_w_out)))
    m = dict(zip(names, (m_ffn1_norm, m_ffn1_w_in, m_ffn1_w_out, m_mix_norm, m_w_in, m_hgrn_lb_logits, m_hgrn_out_norm,
                         m_attn_q_norm, m_attn_k_norm, m_w_branch_a, m_w_branch_b, m_w_out, m_ffn2_norm, m_ffn2_w_in, m_ffn2_w_out)))
    v = dict(zip(names, (v_ffn1_norm, v_ffn1_w_in, v_ffn1_w_out, v_mix_norm, v_w_in, v_hgrn_lb_logits, v_hgrn_out_norm,
                         v_attn_q_norm, v_attn_k_norm, v_w_branch_a, v_w_branch_b, v_w_out, v_ffn2_norm, v_ffn2_w_in, v_ffn2_w_out)))
    depth, d = ffn1_norm.shape

    def tr(a):
        return jnp.swapaxes(a, 1, 2)

    shard = dict(w1t=tr(ffn1_w_in), w1o=ffn1_w_out, wint=tr(w_in), wa=w_branch_a,
                 wbt=tr(w_branch_b).reshape(depth, -1, d), wo=w_out, w2t=tr(ffn2_w_in), w2o=ffn2_w_out)
    order = [(g, l) for l in range(depth) for g in FETCH_GROUPS]
    started = {}
    for name, part in (("gather_start", order),):
        flat = [(l, k) for g, l in part for k in FETCH_GROUPS[g]]
        groups, pos = [], 0
        for g, l in part:
            groups.append(list(range(pos, pos + len(FETCH_GROUPS[g]))))
            pos += len(FETCH_GROUPS[g])
        sems, srcs, lands, _ = _xchg_start([shard[k][l].astype(BF16) for l, k in flat], ["gather"] * len(flat), groups, name)
        for gi, key in enumerate(part):
            started[key] = ([srcs[i] for i in groups[gi]], [lands[i] for i in groups[gi]], sems[gi])

    def fetch(group, l, after):
        srcs, lands, sems = started[group, l]
        lands = _xchg_wait(srcs, lands, ["gather"] * len(srcs), sems, after, f"gather_wait_{group}{l}")
        out = {}
        for k, land in zip(FETCH_GROUPS[group], lands):
            out[k] = land.reshape(d, -1) if k == "wbt" else land.reshape(-1, d)
        return out

    pending = []

    def emit(group, l, g, final):
        keys = list(g)
        srcs = [g[k].reshape(N_DEV, -1, d) for k in keys]
        modes = ["scatter"] * len(keys)
        if final is not None:
            gsmall, loss = final
            pack = jnp.zeros((SMALL_PACK_ROWS, d), F32)
            for k, r0 in SMALL_ROWS:
                rows = gsmall[k].reshape(depth, -1)
                pack = pack.at[r0:r0 + depth, :rows.shape[1]].set(rows)
            srcs.append(pack.at[14, :].set(loss))
            modes.append("gather")
            keys.append("small")
        sems, s_thru, l_thru, token = _xchg_start(srcs, modes, [list(range(len(srcs)))], f"grads_start_{group}{l}")
        pending.append((group, l, keys, modes, sems[0], s_thru, l_thru))
        return token

    small = {k: w[k] for k in ("ffn1_norm", "mix_norm", "hgrn_lb_logits", "hgrn_out_norm", "attn_q_norm", "attn_k_norm", "ffn2_norm")}
    dx = _local_step(x[0], loss_target[0], small, fetch, emit)

    summed, after = {}, dx
    for group, l, keys, modes, sems, s_thru, l_thru in pending:
        lands = _xchg_wait(s_thru, l_thru, modes, sems, after, f"grads_wait_{group}{l}")
        for k, land in zip(keys, lands):
            summed[k, l] = _sum_slots(land[None])[0]
        after = summed[keys[-1], l]
    gsum = {k: jnp.stack([summed[k, l] for l in range(depth)]) for k in BIG}
    tot = summed["small", 0]

    grads = {}
    for k, r0 in SMALL_ROWS:
        shp = (depth,) + (w[k].shape[1:] if k != "lbsum" else (d,))
        grads[k] = tot[r0:r0 + depth, :int(np.prod(shp[1:]))].reshape(shp)
    _, lb_vjp = jax.vjp(_lower_bounds, hgrn_lb_logits)
    grads["hgrn_lb_logits"] = lb_vjp(grads.pop("lbsum"))[0]
    grads["ffn1_w_in"], grads["ffn1_w_out"] = tr(gsum["w1t"]), gsum["w1o"]
    grads["w_in"], grads["w_branch_a"] = tr(gsum["wint"]), gsum["wa"]
    grads["w_branch_b"] = tr(gsum["wbt"].reshape(depth, d // N_DEV, -1))
    grads["w_out"] = gsum["wo"]
    grads["ffn2_w_in"], grads["ffn2_w_out"] = tr(gsum["w2t"]), gsum["w2o"]

    upd = {k: _adamw(w[k], grads[k], m[k], v[k]) for k in names}
    return (tot[14, 0], dx[None], *[grads[k] for k in names], *[upd[k][0] for k in names],
            *[upd[k][1] for k in names], *[upd[k][2] for k in names])
```

```python
import functools

import jax
import jax.numpy as jnp
import numpy as np
from jax import lax
from jax.experimental import pallas as pl
from jax.experimental.pallas import tpu as pltpu

F32 = jnp.float32
BF16 = jnp.bfloat16

N_DEV = 8
EPS = 1e-6
HG_DK = 128
HG_CHUNK = 64
HG_SUB = 16
HG_HP = 8
ATT_PATTERNS = ((128, 1), (512, 4), (2048, 16))
ATT_GROUPS = 3
ATT_HEADS = 4
ATT_DH = 128
ATT_BLK = 128
ROPE_THETA = 10000.0
ADAM_LR, ADAM_B1, ADAM_B2, ADAM_EPS, ADAM_WD, ADAM_STEP = 0.001, 0.9, 0.999, 1e-08, 0.01, 10
VMEM_LIMIT_BYTES = 56 * 1024 * 1024
MXU_COLS = 256
MESH = pl.DeviceIdType.MESH


def _cparams(sem, **kw):
    return pltpu.CompilerParams(dimension_semantics=sem, vmem_limit_bytes=VMEM_LIMIT_BYTES, **kw)


def _sigmoid(x):
    return 1.0 / (1.0 + jnp.exp(-x))


def _mm(a_list, b_list, pairs, n_acc, fin, out_dtypes, *, m, n, k, ta=False, tb=False, bm, bn, bk,
        b_off=None, extras=(), e_off=None, n_outer=False, consts=(), a_pro=None, n_sums=0, chunk=0, a_cat=False, name):
    bm, bn, bk = min(bm, m), min(bn, n), min(bk, k)
    assert m % bm == 0 and n % bn == 0 and k % bk == 0, (name, m, n, k, bm, bn, bk)
    nk = k // bk
    assert not (a_pro and (nk > 1 or ta or n_outer)) and not (n_sums and (bn != n or n_outer)), name
    assert not (chunk and (nk > 1 or n_sums or chunk % 128)), name
    if a_cat:
        unit = bm if ta else bk
        widths = [a.shape[1] for a in a_list]
        assert all(w % unit == 0 for w in widths) and sum(widths) == (m if ta else k) and not a_pro, name
        cat_counts = [w // unit for w in widths]
        cat_starts = [sum(cat_counts[:i]) for i in range(len(widths))]
    b_off = b_off or [(0, 0)] * len(b_list)
    e_off = e_off or [0] * len(extras)
    na, nb, ne, nc, no = len(a_list), len(b_list), len(extras), len(consts), len(out_dtypes)
    nao = na if a_pro else 0
    dn = (((0,) if ta else (1,), (1,) if tb else (0,)), ((), ()))

    def body(*refs):
        refs = list(refs)
        a_refs, b_refs, e_refs, c_refs, o_refs, ao_refs, s_refs = (
            [refs.pop(0) for _ in range(cnt)] for cnt in (na, nb, ne, nc, no, nao, n_sums))
        acc_refs = refs
        kk = pl.program_id(2)
        first = pl.program_id(0) == 0
        cvals = [c[...] for c in c_refs]
        a_vals = [r[...] for r in a_refs]
        if a_cat:
            col = pl.program_id(1 if n_outer else 0) if ta else kk
            sel = a_vals[0]
            for start, v in zip(cat_starts[1:], a_vals[1:]):
                sel = jnp.where(col >= start, v, sel)
            a_vals = [sel]
        if a_pro:
            @pl.when(pl.program_id(1) == 0)
            def _():
                for r, v in zip(ao_refs, a_pro(a_vals, cvals)):
                    r[...] = v

            a_vals = [r[...] for r in ao_refs]
        if chunk:
            spans = [slice(lo, min(lo + chunk, bn)) for lo in range(0, bn, chunk)]
            chunks = []
            for cs in spans:
                parts = [None] * n_acc
                for ai, bi, ci in pairs:
                    p = lax.dot_general(a_vals[ai], b_refs[bi][cs, :] if tb else b_refs[bi][:, cs], dn,
                                        preferred_element_type=F32)
                    parts[ci] = p if parts[ci] is None else parts[ci] + p
                chunks.append(parts)
            for cs, parts in zip(spans, chunks):
                ex = [e[:, cs] for e in e_refs]
                outs = fin(parts, ex, cvals) if nc else fin(parts, ex)
                for o_ref, o in zip(o_refs, outs):
                    o_ref[:, cs] = o.astype(o_ref.dtype)
            return

        parts = [None] * n_acc
        for ai, bi, ci in pairs:
            p = lax.dot_general(a_vals[ai], b_refs[bi][...], dn, preferred_element_type=F32)
            parts[ci] = p if parts[ci] is None else parts[ci] + p

        def finish(accs):
            ex = [e[...] for e in e_refs]
            res = fin(accs, ex, cvals) if nc else fin(accs, ex)
            outs, sums = res if n_sums else (res, ())
            for o_ref, o in zip(o_refs, outs):
                o_ref[...] = o.astype(o_ref.dtype)
            if n_sums:
                @pl.when(first)
                def _():
                    for s_ref, s in zip(s_refs, sums):
                        s_ref[...] = s

                @pl.when(jnp.logical_not(first))
                def _():
                    for s_ref, s in zip(s_refs, sums):
                        s_ref[...] += s

        if nk == 1:
            finish(parts)
        else:
            @pl.when(kk == 0)
            def _():
                for c in range(n_acc):
                    acc_refs[c][...] = parts[c]

            @pl.when(kk > 0)
            def _():
                for c in range(n_acc):
                    acc_refs[c][...] += parts[c]

            @pl.when(kk == nk - 1)
            def _():
                finish([acc_refs[c][...] for c in range(n_acc)])

    def ij(f):
        return (lambda j, i, q: f(i, j, q)) if n_outer else f

    a_spec = pl.BlockSpec((bk, bm), ij(lambda i, j, q: (q, i))) if ta else pl.BlockSpec((bm, bk), ij(lambda i, j, q: (i, q)))
    a_specs = [a_spec] * na
    if a_cat:
        def part_spec(start, count):
            def col(c):
                return jnp.clip(c - start, 0, count - 1)
            if ta:
                return pl.BlockSpec((bk, bm), ij(lambda i, j, q: (q, col(i))))
            return pl.BlockSpec((bm, bk), ij(lambda i, j, q: (i, col(q))))
        a_specs = [part_spec(s, c) for s, c in zip(cat_starts, cat_counts)]

    b_mode = dict(pipeline_mode=pl.Buffered(1)) if (bn == n and nk == 1) else {}

    def b_spec(off):
        on, ok = off
        if tb:
            return pl.BlockSpec((bn, bk), ij(lambda i, j, q: (j + on, q + ok)), **b_mode)
        return pl.BlockSpec((bk, bn), ij(lambda i, j, q: (q + ok, j + on)), **b_mode)

    mn_spec = pl.BlockSpec((bm, bn), ij(lambda i, j, q: (i, j)))
    outs = pl.pallas_call(
        body,
        out_shape=[jax.ShapeDtypeStruct((m, n), d) for d in out_dtypes] + [jax.ShapeDtypeStruct((m, k), BF16)] * nao
        + [jax.ShapeDtypeStruct((8, n), F32)] * n_sums,
        grid=(n // bn, m // bm, nk) if n_outer else (m // bm, n // bn, nk),
        in_specs=a_specs + [b_spec(o) for o in b_off]
        + [pl.BlockSpec((bm, bn), ij(lambda i, j, q, o=o: (i, j + o))) for o in e_off]
        + [pl.BlockSpec(c.shape, lambda *_, nd=c.ndim: (0,) * nd) for c in consts],
        out_specs=[mn_spec] * no + [a_spec] * nao + [pl.BlockSpec((8, n), lambda *_: (0, 0))] * n_sums,
        scratch_shapes=[pltpu.VMEM((bm, bn), F32) for _ in range(n_acc if nk > 1 else 0)],
        compiler_params=_cparams(("arbitrary" if n_sums else "parallel", "parallel", "arbitrary")),
        name=name,
    )(*a_list, *b_list, *extras, *consts)
    return outs


def _first(accs, ex):
    return (accs[0],)


def _rowwise(fn, ins, consts, out_defs, sum_widths, *, bm, name):
    ins = [tuple(e) + (1,) * (4 - len(e)) for e in ins]
    out_defs = [tuple(e) + (1,) * (3 - len(e)) for e in out_defs]
    t = ins[0][0].shape[-2] * ins[0][3]
    bm = min(bm, t)
    assert t % bm == 0, (name, t, bm)
    ni, nc, no, ns = len(ins), len(consts), len(out_defs), len(sum_widths)
    strided = [w for _, w, _, d in ins if d > 1] + [w for w, _, d in out_defs if d > 1]

    def body(*refs):
        i_refs, c_refs = refs[:ni], refs[ni:ni + nc]
        o_refs, s_refs = refs[ni + nc:ni + nc + no], refs[ni + nc + no:ni + nc + no + ns]
        scratch = list(refs[ni + nc + no + ns:])
        vals = []
        for ref, (_, w, _, d) in zip(i_refs, ins):
            if d == 1:
                vals.append(ref[...])
                continue
            s = scratch.pop(0)
            for r in range(d):
                for c in range(w // 128):
                    s.at[c][pl.ds(r, bm // d, stride=d), :] = ref[r, :, c * 128:(c + 1) * 128].astype(F32)
            vals.append(jnp.concatenate([s[c] for c in range(w // 128)], axis=1))
        outs, sums = fn(vals, [r[...] for r in c_refs])
        for o_ref, o, (w, _, d) in zip(o_refs, outs, out_defs):
            if d == 1:
                o_ref[...] = o.astype(o_ref.dtype)
                continue
            s = scratch.pop(0)
            for c in range(w // 128):
                s[c] = o[:, c * 128:(c + 1) * 128].astype(F32)
            for r in range(d):
                for c in range(w // 128):
                    o_ref[r, :, c * 128:(c + 1) * 128] = s.at[c][pl.ds(r, bm // d, stride=d), :].astype(o_ref.dtype)
        if ns:
            first = pl.program_id(0) == 0

            @pl.when(first)
            def _():
                for s_ref, s in zip(s_refs, sums):
                    s_ref[...] = s

            @pl.when(jnp.logical_not(first))
            def _():
                for s_ref, s in zip(s_refs, sums):
                    s_ref[...] += s

    def win(width, cb, d):
        if d > 1:
            return pl.BlockSpec((d, bm // d, width), lambda i: (0, i, 0))
        return pl.BlockSpec((bm, width), lambda i: (i, cb))

    res = pl.pallas_call(
        body,
        out_shape=[jax.ShapeDtypeStruct((t, w) if d == 1 else (d, t // d, w), dt) for w, dt, d in out_defs]
        + [jax.ShapeDtypeStruct((8, w), F32) for w in sum_widths],
        grid=(t // bm,),
        in_specs=[win(w, cb, d) for _, w, cb, d in ins] + [pl.BlockSpec(c.shape, lambda i, nd=c.ndim: (0,) * nd) for c in consts],
        out_specs=[win(w, 0, d) for w, _, d in out_defs] + [pl.BlockSpec((8, w), lambda i: (0, 0)) for w in sum_widths],
        scratch_shapes=[pltpu.VMEM((w // 128, bm, 128), F32) for w in strided],
        compiler_params=_cparams(("arbitrary",) if ns else ("parallel",)),
        name=name,
    )(*[e[0] for e in ins], *consts)
    return res[:no], [jnp.sum(s, axis=0) for s in res[no:]]


def _colsum8(x):
    bm, w = x.shape
    return jnp.sum(x.reshape(bm // 8, 8, w), axis=0)


def _tri(n, upper=False):
    r = lax.broadcasted_iota(jnp.int32, (n, n), 0)
    c = lax.broadcasted_iota(jnp.int32, (n, n), 1)
    return (c >= r) if upper else (c <= r)


def _exact_tri_matmul(tri_bf16, x):
    x0 = x.astype(BF16)
    r1 = x - x0.astype(F32)
    x1 = r1.astype(BF16)
    x2 = (r1 - x1.astype(F32)).astype(BF16)
    w = x.shape[1]
    y = jnp.dot(tri_bf16, jnp.concatenate([x0, x1, x2], axis=1), preferred_element_type=F32)
    return y[:, :w] + y[:, w:2 * w] + y[:, 2 * w:]


def _dot_nt(a, b):
    return lax.dot_general(a, b, (((1,), (1,)), ((), ())), preferred_element_type=F32)


def _dot_tn(a, b):
    return lax.dot_general(a, b, (((0,), (0,)), ((), ())), preferred_element_type=F32)


def _dot(a, b):
    return jnp.dot(a, b, preferred_element_type=F32)


def _hg_gates(hq, hf, lb):
    sq = _sigmoid(hq)
    q = hq * sq
    sg = _sigmoid(hf)
    f = lb + (1.0 - lb) * sg
    return q, sq, sg, f


def _hg_heads(x, hp):
    return [x[:, h * HG_DK:(h + 1) * HG_DK] for h in range(hp)]


def _hg_intra_wide(q, kk, g, hp):
    c = q.shape[0]
    rows = lax.broadcasted_iota(jnp.int32, (c, 1), 0)
    a_rows = [[] for _ in range(hp)]
    qts, kts, eqs, eks = [], [], [], []
    for i in range(c // HG_SUB):
        lo = i * HG_SUB
        ref = g[lo - 1:lo, :] if i else jnp.zeros_like(g[0:1, :])
        eq = jnp.exp(g[lo:lo + HG_SUB, :] - ref)
        ek = jnp.exp(jnp.where(rows < lo + HG_SUB, ref - g, 0.0))
        qtb = (q[lo:lo + HG_SUB, :] * eq).astype(BF16)
        ktb = (kk * ek).astype(BF16)
        tpos = lo + lax.broadcasted_iota(jnp.int32, (HG_SUB, c), 0)
        spos = lax.broadcasted_iota(jnp.int32, (HG_SUB, c), 1)
        for h, (qh, kh) in enumerate(zip(_hg_heads(qtb, hp), _hg_heads(ktb, hp))):
            a_rows[h].append(jnp.where(spos <= tpos, _dot_nt(qh, kh), 0.0))
        qts.append(qtb), kts.append(ktb), eqs.append(eq), eks.append(ek)
    return [jnp.concatenate(r, axis=0) for r in a_rows], qts, kts, eqs, eks


def _hgrn_fwd(zh, lb3, *, tb=512):
    t = zh.shape[0]
    nh = lb3.shape[0]
    c = HG_CHUNK
    tb = min(tb, t)
    nchunk = tb // c
    hp = HG_HP if nh % HG_HP == 0 else 1
    wp = hp * HG_DK

    def body(hq_ref, hf_ref, hi_ref, lb_ref, o_ref, st_ref, state):
        @pl.when(pl.program_id(1) == 0)
        def _():
            state[...] = jnp.zeros_like(state)

        tril = _tri(c).astype(BF16)

        def chunk(ci, carry):
            sl = pl.ds(pl.multiple_of(ci * c, c), c)
            q, _, _, f = _hg_gates(hq_ref[sl, :], hf_ref[sl, :], lb_ref[...])
            kk = 1.0 - f
            g = _exact_tri_matmul(tril, jnp.log(f))
            a, _, _, _, _ = _hg_intra_wide(q, kk, g, hp)
            vb = hi_ref[sl, :].astype(BF16)
            glast = g[c - 1:c, :]
            qgb = (q * jnp.exp(g)).astype(BF16)
            kgb = (kk * jnp.exp(glast - g)).astype(BF16)
            dec = jnp.exp(glast)
            sts = [state[h] for h in range(hp)]
            for h in range(hp):
                st_ref[h, ci] = sts[h]
            vh, qgh, kgh, dech = _hg_heads(vb, hp), _hg_heads(qgb, hp), _hg_heads(kgb, hp), _hg_heads(dec, hp)
            o = [_dot(a[h].astype(BF16), vh[h]) + _dot_nt(qgh[h], sts[h].astype(BF16)) for h in range(hp)]
            new = [_dot_tn(vh[h], kgh[h]) for h in range(hp)]
            o_ref[sl, :] = jnp.concatenate(o, axis=1)
            for h in range(hp):
                state[h] = sts[h] * dech[h] + new[h]
            return carry

        lax.fori_loop(0, nchunk, chunk, 0)

    def col(cb):
        return pl.BlockSpec((tb, wp), lambda h, i: (i, cb * (nh // hp) + h))

    return pl.pallas_call(
        body,
        out_shape=[jax.ShapeDtypeStruct((t, nh * HG_DK), F32), jax.ShapeDtypeStruct((nh, t // c, HG_DK, HG_DK), F32)],
        grid=(nh // hp, t // tb),
        in_specs=[col(0), col(1), col(2), pl.BlockSpec((1, wp), lambda h, i: (0, h))],
        out_specs=[pl.BlockSpec((tb, wp), lambda h, i: (i, h)),
                   pl.BlockSpec((hp, nchunk, HG_DK, HG_DK), lambda h, i: (h, i, 0, 0))],
        scratch_shapes=[pltpu.VMEM((hp, HG_DK, HG_DK), F32)],
        compiler_params=_cparams(("parallel", "arbitrary")),
        name="hgrn_fwd",
    )(zh, zh, zh, lb3.reshape(1, -1))


def _hgrn_bwd(zh, lb3, states, d_o, *, tb=512):
    t = zh.shape[0]
    nh = lb3.shape[0]
    c = HG_CHUNK
    tb = min(tb, t)
    nchunk = tb // c
    nblk = t // tb
    hp = HG_HP if nh % HG_HP == 0 else 1
    wp = hp * HG_DK

    def body(hq_ref, hf_ref, hi_ref, lb_ref, st_ref, do_ref, dq_ref, df_ref, dv_ref, dlb_ref, dstate):
        @pl.when(pl.program_id(1) == 0)
        def _():
            dstate[...] = jnp.zeros_like(dstate)
            dlb_ref[...] = jnp.zeros_like(dlb_ref)

        tril = _tri(c).astype(BF16)
        triu = _tri(c, upper=True).astype(BF16)
        last_row = lax.broadcasted_iota(jnp.int32, (c, 1), 0) == c - 1
        heads = range(hp)

        def chunk(j, carry):
            ci = nchunk - 1 - j
            sl = pl.ds(pl.multiple_of(ci * c, c), c)
            lb = lb_ref[...]
            hq, hf = hq_ref[sl, :], hf_ref[sl, :]
            q, sq, sg, f = _hg_gates(hq, hf, lb)
            kk = 1.0 - f
            g = _exact_tri_matmul(tril, jnp.log(f))
            a, qts, kts, eqs, eks = _hg_intra_wide(q, kk, g, hp)
            glast = g[c - 1:c, :]
            eg, egl, dec = jnp.exp(g), jnp.exp(glast - g), jnp.exp(glast)
            vb, dob = hi_ref[sl, :].astype(BF16), do_ref[sl, :].astype(BF16)
            qgb, kgb = (q * eg).astype(BF16), (kk * egl).astype(BF16)
            sts = [st_ref[h, ci] for h in heads]
            dsts = [dstate[h] for h in heads]
            stb, dstb = [s.astype(BF16) for s in sts], [s.astype(BF16) for s in dsts]
            vh, doh, qgh, kgh = _hg_heads(vb, hp), _hg_heads(dob, hp), _hg_heads(qgb, hp), _hg_heads(kgb, hp)
            dv = [_dot_tn(a[h].astype(BF16), doh[h]) + _dot_nt(kgh[h], dstb[h]) for h in heads]
            da = [jnp.where(_tri(c), _dot_nt(doh[h], vh[h]), 0.0).astype(BF16) for h in heads]
            dq_inter = jnp.concatenate([_dot(doh[h], stb[h]) for h in heads], axis=1) * eg
            dk_state = jnp.concatenate([_dot(vh[h], dstb[h]) for h in heads], axis=1) * egl
            new_dst = [_dot_tn(doh[h], qgh[h]) for h in heads]
            xs, dk, dgk = [], dk_state, 0.0
            for i in range(c // HG_SUB):
                rs = slice(i * HG_SUB, (i + 1) * HG_SUB)
                kth, qth = _hg_heads(kts[i], hp), _hg_heads(qts[i], hp)
                xi = jnp.concatenate([_dot(da[h][rs, :], kth[h]) for h in heads], axis=1)
                yi = jnp.concatenate([_dot_tn(da[h][rs, :], qth[h]) for h in heads], axis=1)
                xs.append(xi)
                dk = dk + yi * eks[i]
                dgk = dgk + yi * kts[i].astype(F32)
            dq = jnp.concatenate([x * e for x, e in zip(xs, eqs)], axis=0) + dq_inter
            dgq = jnp.concatenate([x * qt.astype(F32) for x, qt in zip(xs, qts)], axis=0)
            dg = dgq - dgk + q * dq_inter - kk * dk_state
            sdot = jnp.concatenate([jnp.sum(sts[h] * dsts[h], axis=0, keepdims=True) for h in heads], axis=1)
            dgl = jnp.sum(kk * dk_state, axis=0, keepdims=True) + dec * sdot
            dg = dg + jnp.where(last_row, dgl, 0.0)
            dlogf = _exact_tri_matmul(triu, dg)
            dfv = dlogf / f - dk
            dq_ref[sl, :] = (dq * (sq * (1.0 + hq * (1.0 - sq)))).astype(dq_ref.dtype)
            df_ref[sl, :] = (dfv * (1.0 - lb) * sg * (1.0 - sg)).astype(df_ref.dtype)
            dv_ref[sl, :] = jnp.concatenate(dv, axis=1).astype(dv_ref.dtype)
            dlb_ref[...] += jnp.sum(dfv * (1.0 - sg), axis=0, keepdims=True)
            dech = _hg_heads(dec, hp)
            for h in heads:
                dstate[h] = dsts[h] * dech[h] + new_dst[h]
            return carry

        lax.fori_loop(0, nchunk, chunk, 0)

    def col(cb):
        return pl.BlockSpec((tb, wp), lambda h, i: (nblk - 1 - i, cb * (nh // hp) + h))

    ocol = pl.BlockSpec((tb, wp), lambda h, i: (nblk - 1 - i, h))
    lbspec = pl.BlockSpec((1, wp), lambda h, i: (0, h))
    w = nh * HG_DK
    dq, df, dv, dlb = pl.pallas_call(
        body,
        out_shape=[jax.ShapeDtypeStruct((t, w), BF16)] * 3 + [jax.ShapeDtypeStruct((1, w), F32)],
        grid=(nh // hp, nblk),
        in_specs=[col(0), col(1), col(2), lbspec,
                  pl.BlockSpec((hp, nchunk, HG_DK, HG_DK), lambda h, i: (h, nblk - 1 - i, 0, 0)), ocol],
        out_specs=[ocol, ocol, ocol, lbspec],
        scratch_shapes=[pltpu.VMEM((hp, HG_DK, HG_DK), F32)],
        compiler_params=_cparams(("parallel", "arbitrary")),
        name="hgrn_bwd",
    )(zh, zh, zh, lb3.reshape(1, -1), states, d_o)
    return dq, df, dv, dlb.reshape(w)


NEG = -1e30
ATT_GW = ATT_HEADS * ATT_DH


def _att_scores(q, kp, kc, has_prev):
    scale = ATT_DH ** -0.5
    i = lax.broadcasted_iota(jnp.int32, (ATT_BLK, ATT_BLK), 0)
    j = lax.broadcasted_iota(jnp.int32, (ATT_BLK, ATT_BLK), 1)
    s_p = jnp.where(jnp.logical_and(j >= i, has_prev), _dot_nt(q, kp) * scale, NEG)
    s_c = jnp.where(j <= i, _dot_nt(q, kc) * scale, NEG)
    return s_p, s_c


def _att_views(arrs, d):
    return [a.reshape(d, -1, ATT_GW) for a in arrs]


def _att_unview(a, d):
    return a.reshape(-1, ATT_GW) if d == 1 else a


ATT_QB = 4


def _attn_fwd(qb, kb, vb, g):
    d = ATT_PATTERNS[g][1]
    q2, k2, v2 = _att_views([qb, kb, vb], d)
    nblk = q2.shape[1] // ATT_BLK
    nq = ATT_QB if nblk % ATT_QB == 0 else 1
    rows = nq * ATT_BLK

    def body(q_ref, kc_ref, kp_ref, vc_ref, vp_ref, o_ref, l_ref):
        first = pl.program_id(1) == 0
        hss = [slice(h * ATT_DH, (h + 1) * ATT_DH) for h in range(ATT_HEADS)]
        for b in range(nq):
            rs = slice(b * ATT_BLK, (b + 1) * ATT_BLK)
            ps = slice((b - 1) * ATT_BLK, b * ATT_BLK)
            has_prev = jnp.logical_not(first) if b == 0 else True
            kv = [(kp_ref[:, hs], vp_ref[:, hs]) if b == 0 else (kc_ref[ps, hs], vc_ref[ps, hs]) for hs in hss]
            sc = [_att_scores(q_ref[rs, hs], kv[h][0], kc_ref[rs, hs], has_prev) for h, hs in enumerate(hss)]
            ms = [jnp.maximum(jnp.max(s_p, axis=1, keepdims=True), jnp.max(s_c, axis=1, keepdims=True)) for s_p, s_c in sc]
            ps_ = [(jnp.exp(s_p - m), jnp.exp(s_c - m)) for (s_p, s_c), m in zip(sc, ms)]
            ls = [jnp.sum(p_p, axis=1, keepdims=True) + jnp.sum(p_c, axis=1, keepdims=True) for p_p, p_c in ps_]
            os_ = [_dot(p_p.astype(BF16), kv[h][1]) + _dot(p_c.astype(BF16), vc_ref[rs, hss[h]]) for h, (p_p, p_c) in enumerate(ps_)]
            for h, hs in enumerate(hss):
                o_ref[rs, hs] = os_[h] / ls[h]
                l_ref[rs, hs] = jnp.broadcast_to(ms[h] + jnp.log(ls[h]), (ATT_BLK, ATT_DH))

    cur = pl.BlockSpec((None, rows, ATT_GW), lambda r, n: (r, n, 0))
    prev = pl.BlockSpec((None, ATT_BLK, ATT_GW), lambda r, n: (r, jnp.maximum(n * nq - 1, 0), 0))
    o, lse = pl.pallas_call(
        body,
        out_shape=[jax.ShapeDtypeStruct(q2.shape, F32)] * 2,
        grid=(d, nblk // nq),
        in_specs=[cur, cur, prev, cur, prev],
        out_specs=[cur, cur],
        compiler_params=_cparams(("parallel", "arbitrary")),
        name=f"attn_fwd_g{g}",
    )(q2, k2, k2, v2, v2)
    return _att_unview(o, d), _att_unview(lse, d)


def _attn_bwd(qb, kb, vb, o, lse, d_o, d_lse, g):
    d = ATT_PATTERNS[g][1]
    q2, k2, v2 = _att_views([qb, kb, vb], d)
    o2, l2, do2, dl2 = _att_views([o, lse, d_o, d_lse], d)
    nblk = q2.shape[1] // ATT_BLK
    nq = ATT_QB if nblk % ATT_QB == 0 else 1
    rows = nq * ATT_BLK
    ns = nblk // nq
    scale = ATT_DH ** -0.5

    def body(q_ref, kc_ref, kp_ref, vc_ref, vp_ref, o_ref, l_ref, do_ref, dl_ref, dq_ref, dk_ref, dv_ref, ck, cv):
        n = pl.program_id(1)

        @pl.when(n == 0)
        def _():
            ck[...] = jnp.zeros_like(ck)
            cv[...] = jnp.zeros_like(cv)

        first = n == ns - 1
        hss = [slice(h * ATT_DH, (h + 1) * ATT_DH) for h in range(ATT_HEADS)]
        heads = range(ATT_HEADS)
        pend_k, pend_v = [ck[:, hs] for hs in hss], [cv[:, hs] for hs in hss]
        for b in reversed(range(nq)):
            rs = slice(b * ATT_BLK, (b + 1) * ATT_BLK)
            ps = slice((b - 1) * ATT_BLK, b * ATT_BLK)
            has_prev = jnp.logical_not(first) if b == 0 else True
            q = [q_ref[rs, hs] for hs in hss]
            kc, vc = [kc_ref[rs, hs] for hs in hss], [vc_ref[rs, hs] for hs in hss]
            kp = [kp_ref[:, hs] if b == 0 else kc_ref[ps, hs] for hs in hss]
            vp = [vp_ref[:, hs] if b == 0 else vc_ref[ps, hs] for hs in hss]
            sc = [_att_scores(q[h], kp[h], kc[h], has_prev) for h in heads]
            dob = [do_ref[rs, hs].astype(BF16) for hs in hss]
            dp = [(_dot_nt(dob[h], vp[h]), _dot_nt(dob[h], vc[h])) for h in heads]
            delta = [jnp.sum(do_ref[rs, hs] * o_ref[rs, hs] - dl_ref[rs, hs], axis=1, keepdims=True) for hs in hss]
            pr = [(jnp.exp(sc[h][0] - l_ref[rs, hss[h]][:, 0:1]), jnp.exp(sc[h][1] - l_ref[rs, hss[h]][:, 0:1])) for h in heads]
            ds = [((pr[h][0] * (dp[h][0] - delta[h]) * scale).astype(BF16), (pr[h][1] * (dp[h][1] - delta[h]) * scale).astype(BF16))
                  for h in heads]
            pb = [(pr[h][0].astype(BF16), pr[h][1].astype(BF16)) for h in heads]
            dq = [_dot(ds[h][0], kp[h]) + _dot(ds[h][1], kc[h]) for h in heads]
            dk_c = [_dot_tn(ds[h][1], q[h]) for h in heads]
            dv_c = [_dot_tn(pb[h][1], dob[h]) for h in heads]
            dk_p = [_dot_tn(ds[h][0], q[h]) for h in heads]
            dv_p = [_dot_tn(pb[h][0], dob[h]) for h in heads]
            for h, hs in enumerate(hss):
                dq_ref[rs, hs] = dq[h]
                dk_ref[rs, hs] = pend_k[h] + dk_c[h]
                dv_ref[rs, hs] = pend_v[h] + dv_c[h]
            pend_k, pend_v = dk_p, dv_p
        for h, hs in enumerate(hss):
            ck[:, hs] = pend_k[h]
            cv[:, hs] = pend_v[h]

    cur = pl.BlockSpec((None, rows, ATT_GW), lambda r, n: (r, ns - 1 - n, 0))
    prev = pl.BlockSpec((None, ATT_BLK, ATT_GW), lambda r, n: (r, jnp.maximum((ns - 1 - n) * nq - 1, 0), 0))
    shp = jax.ShapeDtypeStruct(q2.shape, F32)
    dq, dk, dv = pl.pallas_call(
        body,
        out_shape=[shp, shp, shp],
        grid=(d, ns),
        in_specs=[cur, cur, prev, cur, prev, cur, cur, cur, cur],
        out_specs=[cur, cur, cur],
        scratch_shapes=[pltpu.VMEM((ATT_BLK, ATT_GW), F32), pltpu.VMEM((ATT_BLK, ATT_GW), F32)],
        compiler_params=_cparams(("parallel", "arbitrary")),
        name=f"attn_bwd_g{g}",
    )(q2, k2, k2, v2, v2, o2, l2, do2, dl2)
    return _att_unview(dq, d), _att_unview(dk, d), _att_unview(dv, d)


def _rms_parts(x, width):
    outs = []
    for lo in range(0, x.shape[1], width):
        xs = x[:, lo:lo + width].astype(F32)
        r = lax.rsqrt(jnp.mean(xs * xs, axis=1, keepdims=True) + EPS)
        outs.append((xs * r, r))
    return outs


def _rms_bwd_part(xh, r, dxh):
    return r * (dxh - xh * jnp.mean(dxh * xh, axis=1, keepdims=True))


def _norm_pro(a, consts):
    (xh, _), = _rms_parts(a[0], a[0].shape[1])
    return [(xh * consts[0]).astype(BF16)]


def _norm_bwd_fin(accs, ex, consts):
    xv, dres = ex
    (xh, r), = _rms_parts(xv, xv.shape[1])
    dx = dres + _rms_bwd_part(xh, r, accs[0] * consts[0])
    return [dx, dx], [_colsum8(accs[0] * xh)]


def _rot_sign():
    lane = lax.broadcasted_iota(jnp.int32, (1, ATT_DH), 1)
    return jnp.where(lane < ATT_DH // 2, -1.0, 1.0).astype(F32)


def _rope(y, cos, sin):
    return y * cos + pltpu.roll(y, ATT_DH // 2, axis=1) * _rot_sign() * sin


def _rope_t(dy, cos, sin):
    return dy * cos - pltpu.roll(dy * sin, ATT_DH // 2, axis=1) * _rot_sign()


def _qk_prep(zq, zk, zv, qn, kn, cos, sin):
    w = zq.shape[1]

    def fn(ins, consts):
        cs, sn = ins[3], ins[4]
        outs = []
        for z, gain in ((ins[0], consts[0]), (ins[1], consts[1])):
            for i, (xh, _) in enumerate(_rms_parts(z, ATT_DH)):
                outs.append(_rope(xh * gain[:, i * ATT_DH:(i + 1) * ATT_DH], cs, sn))
        outs += [ins[2][:, i * ATT_DH:(i + 1) * ATT_DH] for i in range(w // ATT_DH)]
        groups = [jnp.concatenate(outs[i:i + ATT_HEADS], axis=1) for i in range(0, len(outs), ATT_HEADS)]
        return groups, []

    outs, _ = _rowwise(fn, [(zq, w, 0), (zk, w, 0), (zv, w, 0), (cos, ATT_DH, 0), (sin, ATT_DH, 0)], [qn, kn],
                       [(ATT_GW, BF16, ATT_PATTERNS[g][1]) for g in range(ATT_GROUPS)] * 3, [], bm=256, name="qk_prep")
    return outs[0:3], outs[3:6], outs[6:9]


def _qk_prep_bwd(zq, zk, dq_g, dk_g, dv_g, qn, kn, cos, sin):
    w = zq.shape[1]

    def fn(ins, consts):
        cs, sn = ins[2], ins[3]
        outs, sums = [], []
        for z, gain, dparts in ((ins[0], consts[0], ins[4:7]), (ins[1], consts[1], ins[7:10])):
            dout = jnp.concatenate(dparts, axis=1)
            dz, dgain = [], []
            for i, (xh, r) in enumerate(_rms_parts(z, ATT_DH)):
                hs = slice(i * ATT_DH, (i + 1) * ATT_DH)
                dy = _rope_t(dout[:, hs], cs, sn)
                dgain.append(_colsum8(dy * xh))
                dz.append(_rms_bwd_part(xh, r, dy * gain[:, hs]))
            outs.append(jnp.concatenate(dz, axis=1))
            sums.append(jnp.concatenate(dgain, axis=1))
        outs.append(jnp.concatenate(ins[10:13], axis=1))
        return outs, sums

    ins = [(zq, w, 0), (zk, w, 0), (cos, ATT_DH, 0), (sin, ATT_DH, 0)]
    for parts in (dq_g, dk_g, dv_g):
        ins += [(a, ATT_GW, 0, ATT_PATTERNS[g][1]) for g, a in enumerate(parts)]
    (dzq, dzk, dzv), (dqn, dkn) = _rowwise(fn, ins, [qn, kn], [(w, BF16)] * 3, [w, w], bm=256, name="qk_prep_bwd")
    return dzq, dzk, dzv, dqn, dkn


def _post_a(o_raw, zh, gout):
    w = o_raw.shape[1]

    def fn(ins, consts):
        oh = jnp.concatenate([xh for xh, _ in _rms_parts(ins[0], HG_DK)], axis=1)
        hg = ins[1]
        return [oh * consts[0] * (hg * _sigmoid(hg))], []

    (y,), _ = _rowwise(fn, [(o_raw, w, 0), (zh, w, 3)], [gout.reshape(1, w)], [(w, BF16)], [], bm=512, name="post_a")
    return y


def _post_a_bwd(o_raw, zh, gout, dy):
    w = o_raw.shape[1]

    def fn(ins, consts):
        parts = _rms_parts(ins[0], HG_DK)
        oh = jnp.concatenate([xh for xh, _ in parts], axis=1)
        hg, dyv, gain = ins[1], ins[2], consts[0]
        sg = _sigmoid(hg)
        s = hg * sg
        doh = dyv * gain * s
        do = jnp.concatenate([_rms_bwd_part(xh, r, doh[:, i * HG_DK:(i + 1) * HG_DK]) for i, (xh, r) in enumerate(parts)], axis=1)
        dhg = dyv * oh * gain * (sg * (1.0 + hg * (1.0 - sg)))
        return [do, dhg], [_colsum8(dyv * oh * s)]

    (do, dhg), (dgain,) = _rowwise(fn, [(o_raw, w, 0), (zh, w, 3), (dy, w, 0)], [gout.reshape(1, w)],
                                   [(w, F32), (w, BF16)], [w], bm=512, name="post_a_bwd")
    return do, dhg, dgain


def _merge_alpha(lses):
    m = jnp.maximum(jnp.maximum(lses[0], lses[1]), lses[2])
    e = [jnp.exp(l - m) for l in lses]
    inv = 1.0 / (e[0] + e[1] + e[2])
    return [x * inv for x in e]


def _group_ins(parts):
    return [(a, ATT_GW, 0, ATT_PATTERNS[g][1]) for g, a in enumerate(parts)]


def _merge_b(o_g, lse_g):
    def fn(ins, consts):
        al = _merge_alpha(ins[3:6])
        return [al[0] * ins[0] + al[1] * ins[1] + al[2] * ins[2]], []

    (y,), _ = _rowwise(fn, _group_ins(o_g) + _group_ins(lse_g), [], [(ATT_GW, BF16)], [], bm=512, name="merge_b")
    return y


def _merge_b_bwd(o_g, lse_g, dy):
    def fn(ins, consts):
        al = _merge_alpha(ins[3:6])
        dyv = ins[6]
        dal = [dyv * ins[i] for i in range(3)]
        tot = al[0] * dal[0] + al[1] * dal[1] + al[2] * dal[2]
        return [al[i] * dyv for i in range(3)] + [al[i] * (dal[i] - tot) for i in range(3)], []

    outs, _ = _rowwise(fn, _group_ins(o_g) + _group_ins(lse_g) + [(dy, ATT_GW, 0)], [],
                       [(ATT_GW, F32, ATT_PATTERNS[g][1]) for g in range(ATT_GROUPS)] * 2, [], bm=512, name="merge_b_bwd")
    return outs[:3], outs[3:]


def _loss_head(y, target):
    d = y.shape[1]

    def fn(ins, consts):
        e = ins[0] - ins[1]
        return [e * (1.0 / d)] * 2, [_colsum8(e * e)]

    (dy, dyb), (sq,) = _rowwise(fn, [(y, d, 0), (target, d, 0)], [], [(d, F32), (d, BF16)], [d], bm=512, name="loss_head")
    return 0.5 * jnp.sum(sq) / d, dy, dyb


def _ffn_fwd(x, gain, wt, wo_fn, tag):
    t, d = x.shape
    f = wt.shape[0] // 2

    def act(accs, ex, consts):
        a, b = accs
        s = _sigmoid(a)
        sa = a * s
        return (sa * b, b, 0.5 * sa, 0.5 * (s + sa * (1.0 - s)))

    bn = FFN_BN if f % FFN_BN == 0 else 256
    u, b, sa, sp, h = _mm([x], [wt, wt], [(0, 0, 0), (0, 1, 1)], 2, act, [BF16] * 4, m=t, n=f, k=d, tb=True,
                          bm=512, bn=bn, bk=d, b_off=[(0, 0), (f // min(bn, f), 0)],
                          consts=[gain.reshape(1, d)], a_pro=_norm_pro, chunk=MXU_COLS, name=f"ffn_in_{tag}")
    wo = wo_fn(u)
    (y,) = _mm([u], [wo], [(0, 0, 0)], 1, lambda accs, ex: (ex[0] + 0.5 * accs[0],), [F32], m=t, n=d, k=f,
               bm=512, bn=d, bk=f, extras=[x], name=f"ffn_out_{tag}")
    return y, (x, h, u, b, sa, sp, wo)


def _ffn_bwd(dy, dyb, saved, gain, wt, tag, tok, emit):
    x, h, u, b, sa, sp, wo = saved
    t, d = x.shape
    f = wo.shape[0]

    def dact(accs, ex, consts):
        bv, sav, spv = (e.astype(F32) for e in ex)
        return (accs[0] * bv * spv, accs[0] * sav)

    bn = FFN_BN if f % FFN_BN == 0 else 256
    da, db = _mm([dyb], [wo], [(0, 0, 0)], 1, dact, [BF16, BF16], m=t, n=f, k=d, tb=True, bm=512, bn=bn, bk=d,
                 extras=[b, sa, sp], n_outer=True, chunk=MXU_COLS, consts=[tok], name=f"ffn_dact_{tag}")
    (dwo,) = _mm([u], [dyb], [(0, 0, 0)], 1, lambda accs, ex: (0.5 * accs[0],), [BF16], m=f, n=d, k=t, ta=True,
                 bm=1408, bn=d, bk=2048, name=f"ffn_dwo_{tag}")
    (dwt,) = _mm([da, db], [h], [(0, 0, 0)], 1, _first, [BF16], m=2 * f, n=d, k=t, ta=True, bm=min(1408, f), bn=d,
                 bk=2048, a_cat=True, name=f"ffn_dwt_{tag}")
    tok = emit(dwt, dwo)
    bk = min(FFN_BN, f)
    dx, dxb, dgain = _mm([da, db], [wt, wt], [(0, 0, 0), (1, 1, 0)], 1, _norm_bwd_fin, [F32, BF16], m=t, n=d, k=f,
                         bm=512, bn=d, bk=bk, b_off=[(0, 0), (0, f // bk)], extras=[x, dy],
                         consts=[gain.reshape(1, d), tok], n_sums=1, name=f"ffn_dh_{tag}")
    return dx, dxb, jnp.sum(dgain, axis=0), tok


FFN_BN = 2816
Z_SPLITS = (("h", 4096), ("q", 1536), ("k", 1536), ("v", 1536), ("g", 2048))


def _mix_fwd(x, p, cos, sin):
    t, d = x.shape
    z, off, hm = {}, 0, None
    for nm, width in Z_SPLITS:
        bn = 1024 if off % 1024 == 0 and width % 1024 == 0 else 512
        first = hm is None
        res = _mm([x if first else hm], [p["wint"]], [(0, 0, 0)], 1, (lambda accs, ex, consts: (accs[0],)) if first else _first,
                  [F32 if nm == "h" else BF16], m=t, n=width, k=d, tb=True, bm=1024 if first else 2048, bn=bn, bk=d,
                  b_off=[(off // bn, 0)],
                  consts=[p["gm"].reshape(1, d)] if first else (), a_pro=_norm_pro if first else None, name=f"mix_in_{nm}")
        z[nm] = res[0]
        hm = res[1] if first else hm
        off += width
    o_raw, states = _hgrn_fwd(z["h"], p["lb3"])
    qb, kb, vb = _qk_prep(z["q"], z["k"], z["v"], p["qn"], p["kn"], cos, sin)
    o_g, lse_g = zip(*[_attn_fwd(qb[g], kb[g], vb[g], g) for g in range(ATT_GROUPS)])
    oa = _post_a(o_raw, z["h"], p["gout"])
    ob = _merge_b(o_g, lse_g)
    late = p["late"](ob)
    p = dict(p, **late)
    (ya,) = _mm([oa], [p["wa"]], [(0, 0, 0)], 1, _first, [F32], m=t, n=d, k=oa.shape[1], bm=1024, bn=d, bk=oa.shape[1],
                name="branch_a")

    def gate(accs, ex):
        return (_sigmoid(ex[0].astype(F32)) * ex[2] + _sigmoid(ex[1].astype(F32)) * accs[0], accs[0])

    merged, yb = _mm([ob], [p["wbt"]], [(0, 0, 0)], 1, gate, [BF16, F32], m=t, n=d, k=ATT_GW, tb=True, bm=512, bn=d,
                     bk=ATT_GW, extras=[z["g"], z["g"], ya], e_off=[0, 1, 0], chunk=MXU_COLS, name="branch_b_gate")
    (y,) = _mm([merged], [p["wo"]], [(0, 0, 0)], 1, lambda accs, ex: (ex[0] + accs[0],), [F32], m=t, n=d, k=d,
               bm=1024, bn=d, bk=d, extras=[x], name="mix_out")
    return y, (x, hm, z, o_raw, states, qb, kb, vb, o_g, lse_g, oa, ob, ya, yb, merged, late)


def _mix_bwd(dy, dyb, saved, p, cos, sin, tok):
    x, hm, z, o_raw, states, qb, kb, vb, o_g, lse_g, oa, ob, ya, yb, merged, late = saved
    p = dict(p, **late)
    t, d = x.shape
    w = oa.shape[1]

    def dgate(accs, ex, consts):
        dm = accs[0]
        sa, sb = _sigmoid(ex[0].astype(F32)), _sigmoid(ex[1].astype(F32))
        return (sa * dm, sb * dm, dm * ex[2] * sa * (1.0 - sa), dm * ex[3] * sb * (1.0 - sb))

    dya, dyb_, dga, dgb = _mm([dyb], [p["wo"]], [(0, 0, 0)], 1, dgate, [BF16] * 4, m=t, n=d, k=d, tb=True, bm=512, bn=d,
                              bk=d, extras=[z["g"], z["g"], ya, yb], e_off=[0, 1, 0, 0], chunk=MXU_COLS, consts=[tok], name="mix_out_bwd")
    (dwo,) = _mm([merged], [dyb], [(0, 0, 0)], 1, _first, [BF16], m=d, n=d, k=t, ta=True, bm=d, bn=d, bk=1024, name="mix_dwo")
    (doa,) = _mm([dya], [p["wa"]], [(0, 0, 0)], 1, _first, [F32], m=t, n=w, k=d, tb=True, bm=1024, bn=w, bk=d, name="branch_a_bwd")
    (dwa,) = _mm([oa], [dya], [(0, 0, 0)], 1, _first, [BF16], m=w, n=d, k=t, ta=True, bm=w, bn=d, bk=1024, name="branch_a_dw")
    (dob,) = _mm([dyb_], [p["wbt"]], [(0, 0, 0)], 1, _first, [F32], m=t, n=ATT_GW, k=d, bm=1024, bn=ATT_GW, bk=d,
                 name="branch_b_bwd")
    (dwbt,) = _mm([dyb_], [ob], [(0, 0, 0)], 1, _first, [BF16], m=d, n=ATT_GW, k=t, ta=True, bm=d, bn=ATT_GW, bk=1024,
                  name="branch_b_dw")
    do_raw, dhg, dgout = _post_a_bwd(o_raw, z["h"], p["gout"], doa)
    do_g, dlse_g = _merge_b_bwd(o_g, lse_g, dob)
    dq_g, dk_g, dv_g = zip(*[_attn_bwd(qb[g], kb[g], vb[g], o_g[g], lse_g[g], do_g[g], dlse_g[g], g)
                             for g in range(ATT_GROUPS)])
    dzq, dzk, dzv, dqn, dkn = _qk_prep_bwd(z["q"], z["k"], dq_g, dk_g, dv_g, p["qn"], p["kn"], cos, sin)
    dhq, dhf, dhi, lbsum = _hgrn_bwd(z["h"], p["lb3"], states, do_raw)
    dz = jnp.concatenate([dhq, dhf, dhi, dhg, dzq, dzk, dzv, dga, dgb], axis=1)
    pw = dz.shape[1]
    (dwint,) = _mm([dz], [hm], [(0, 0, 0)], 1, _first, [BF16], m=pw, n=d, k=t, ta=True, bm=1536, bn=d, bk=2048, name="mix_in_dw")
    dx, dxb, dgm = _mm([dz], [p["wint"]], [(0, 0, 0)], 1, _norm_bwd_fin, [F32, BF16], m=t, n=d, k=pw, bm=1024, bn=d, bk=1536,
                       extras=[x, dy], consts=[p["gm"].reshape(1, d)], n_sums=1, name="mix_in_bwd")
    return dx, dxb, dict(gm=jnp.sum(dgm, axis=0), wint=dwint, lbsum=lbsum, gout=dgout, qn=dqn, kn=dkn, wa=dwa, wbt=dwbt, wo=dwo)


def _rope_tables(t):
    pos = jnp.arange(t, dtype=F32)
    inv = ROPE_THETA ** (-jnp.arange(0, ATT_DH, 2, dtype=F32) / ATT_DH)
    ang = pos[:, None] * inv[None, :]
    ang = jnp.concatenate([ang, ang], axis=-1)
    return jnp.cos(ang), jnp.sin(ang)


def _lower_bounds(logits):
    lb = jnp.cumsum(jax.nn.softmax(logits, axis=0), axis=0)
    return lb - lb[0:1]


def _head_gain(g):
    return jnp.tile(g[:, None, :], (1, ATT_HEADS, 1)).reshape(1, ATT_GROUPS * ATT_GW)


SMALL_GRADS = ("ffn1_norm", "mix_norm", "lbsum", "hgrn_out_norm", "attn_q_norm", "attn_k_norm", "ffn2_norm")


def _local_step(x, target, small, fetch, emit):
    t = x.shape[0]
    depth = small["ffn1_norm"].shape[0]
    cos, sin = _rope_tables(t)
    lb_all = _lower_bounds(small["hgrn_lb_logits"])
    saved = []
    for l in range(depth):
        w1t = fetch("w1t", l, x)["w1t"]
        x, s1 = _ffn_fwd(x, small["ffn1_norm"][l], w1t, lambda after, l=l: fetch("w1o", l, after)["w1o"], "1")
        p = dict(gm=small["mix_norm"][l], wint=fetch("wint", l, x)["wint"], lb3=lb_all[l].reshape(-1, 1, HG_DK),
                 gout=small["hgrn_out_norm"][l], qn=_head_gain(small["attn_q_norm"][l]),
                 kn=_head_gain(small["attn_k_norm"][l]), late=functools.partial(fetch, "mout", l))
        x, sm = _mix_fwd(x, p, cos, sin)
        w2t = fetch("w2t", l, x)["w2t"]
        x, s2 = _ffn_fwd(x, small["ffn2_norm"][l], w2t, lambda after, l=l: fetch("w2o", l, after)["w2o"], "2")
        saved.append((p, w1t, w2t, s1, sm, s2))
    loss, dx, dxb = _loss_head(x, target)
    gsmall = {k: [None] * depth for k in SMALL_GRADS}
    tok = jnp.zeros((8, 128), F32)
    for l in reversed(range(depth)):
        p, w1t, w2t, s1, sm, s2 = saved[l]
        dx, dxb, gsmall["ffn2_norm"][l], tok = _ffn_bwd(
            dx, dxb, s2, small["ffn2_norm"][l], w2t, "2", tok, lambda dwt, dwo, l=l: emit("ffn2", l, dict(w2t=dwt, w2o=dwo), None))
        dx, dxb, gm = _mix_bwd(dx, dxb, sm, p, cos, sin, tok)
        tok = emit("mix", l, {k: gm[k] for k in ("wint", "wa", "wbt", "wo")}, None)
        gsmall["mix_norm"][l], gsmall["lbsum"][l], gsmall["hgrn_out_norm"][l] = gm["gm"], gm["lbsum"], gm["gout"]
        for k, src in (("attn_q_norm", "qn"), ("attn_k_norm", "kn")):
            gsmall[k][l] = jnp.sum(gm[src].reshape(ATT_GROUPS, ATT_HEADS, ATT_DH), axis=1)
        dx, dxb, gsmall["ffn1_norm"][l], tok = _ffn_bwd(
            dx, dxb, s1, small["ffn1_norm"][l], w1t, "1", tok, lambda dwt, dwo, l=l: emit("ffn1", l, dict(w1t=dwt, w1o=dwo), None))
    emit("small", 0, {}, ({k: jnp.stack(v) for k, v in gsmall.items()}, loss))
    return dx


_HBM = pl.BlockSpec(memory_space=pltpu.HBM)
_SEM = pl.BlockSpec(memory_space=pltpu.SEMAPHORE)
_EFFECT = pltpu.SideEffectType.DATAFLOW_SIDE_EFFECTING


def _peer(p):
    x, y, c = lax.axis_index("x"), lax.axis_index("y"), lax.axis_index("c")
    me = 4 * x + 2 * y + c
    return (1 - x if p & 4 else x, 1 - y if p & 2 else y, 1 - c if p & 1 else c), jnp.bitwise_xor(me, p), me


def _xchg_copy(src, land, mode, send_sems, recv_sems, k, p, arriving):
    peer, peer_id, me = _peer(p)
    block = src if mode == "gather" else src.at[peer_id]
    return pltpu.make_async_remote_copy(
        src_ref=block, dst_ref=land.at[peer_id if arriving else me], send_sem=send_sems.at[k * (N_DEV - 1) + p - 1],
        recv_sem=recv_sems.at[k * (N_DEV - 1) + p - 1], device_id=peer, device_id_type=MESH)


def _xchg_start(srcs, modes, groups, name):
    n, ng = len(srcs), len(groups)

    def body(*refs):
        src = refs[:n]
        sems = refs[n:n + 2 * ng]
        land = refs[n + 2 * ng + n:n + 2 * ng + 2 * n]
        token = refs[n + 2 * ng + 2 * n]
        for gi, idx in enumerate(groups):
            for ki, k in enumerate(idx):
                for p in range(1, N_DEV):
                    _xchg_copy(src[k], land[k], modes[k], sems[2 * gi], sems[2 * gi + 1], ki, p, False).start()
        token[...] = jnp.zeros_like(token)

    sem_shapes = []
    for idx in groups:
        sem_shapes += [pltpu.SemaphoreType.DMA((len(idx) * (N_DEV - 1),))] * 2
    outs = pl.pallas_call(
        body,
        out_shape=sem_shapes + [pltpu.HBM(a.shape, a.dtype) for a in srcs]
        + [pltpu.HBM((N_DEV,) + a.shape[-2:], a.dtype) for a in srcs] + [jax.ShapeDtypeStruct((8, 128), F32)],
        in_specs=[_HBM] * n,
        out_specs=[_SEM] * (2 * ng) + [_HBM] * (2 * n) + [pl.BlockSpec(memory_space=pltpu.VMEM)],
        input_output_aliases={i: 2 * ng + i for i in range(n)},
        compiler_params=pltpu.CompilerParams(has_side_effects=_EFFECT),
        name=name,
    )(*[pltpu.with_memory_space_constraint(a, pltpu.HBM) for a in srcs])
    sems = [(outs[2 * gi], outs[2 * gi + 1]) for gi in range(ng)]
    return sems, outs[2 * ng:2 * ng + n], outs[2 * ng + n:2 * ng + 2 * n], outs[-1]


def _xchg_wait_call(srcs, lands, modes, sems, after, name):
    n = len(srcs)

    def body(*refs):
        src, land = refs[:n], refs[n:2 * n]
        send_sems, recv_sems = refs[2 * n], refs[2 * n + 1]
        for p in range(1, N_DEV):
            for k in range(n):
                cp = _xchg_copy(src[k], land[k], modes[k], send_sems, recv_sems, k, p, True)
                cp.wait_send()
                cp.wait_recv()

    outs = pl.pallas_call(
        body,
        out_shape=[pltpu.HBM(a.shape, a.dtype) for a in list(srcs) + list(lands)],
        in_specs=[_HBM] * (2 * n) + [_SEM, _SEM, pl.BlockSpec(memory_space=pl.ANY)],
        out_specs=[_HBM] * (2 * n),
        input_output_aliases={i: i for i in range(2 * n)},
        compiler_params=pltpu.CompilerParams(has_side_effects=_EFFECT),
        name=name,
    )(*srcs, *lands, sems[0], sems[1], after)
    return outs[:n], outs[n:]


def _xchg_wait(srcs, lands, modes, sems, after, name):
    srcs, lands = _xchg_wait_call(srcs, lands, modes, sems, after, name)
    me = 4 * lax.axis_index("x") + 2 * lax.axis_index("y") + lax.axis_index("c")
    done = []
    for a, land, mode in zip(srcs, lands, modes):
        own = a[None] if mode == "gather" else lax.dynamic_slice_in_dim(a, me, 1, axis=0)
        done.append(lax.dynamic_update_slice(land, own, (me, 0, 0)))
    return done


def _sum_slots(land):
    g, _, r, c = land.shape
    br = r // 2 if (r % 32 == 0 and r >= 256) else r

    def body(l_ref, o_ref):
        acc = l_ref[0, 0].astype(F32)
        for j in range(1, N_DEV):
            acc = acc + l_ref[0, j].astype(F32)
        o_ref[0] = acc

    return pl.pallas_call(
        body,
        out_shape=jax.ShapeDtypeStruct((g, r, c), F32),
        grid=(g, r // br),
        in_specs=[pl.BlockSpec((1, N_DEV, br, c), lambda i, j: (i, 0, j, 0))],
        out_specs=pl.BlockSpec((1, br, c), lambda i, j: (i, j, 0)),
        compiler_params=_cparams(("parallel", "parallel")),
        name="sum_slots",
    )(land)


def _adamw(w, g, m, v):
    shape = w.shape
    cols = shape[-1]
    rows = int(np.prod(shape[:-1]))
    bm = max(b for b in range(8, 513, 8) if rows % b == 0) if rows % 8 == 0 else rows
    c1 = 1.0 - ADAM_B1 ** ADAM_STEP
    c2 = 1.0 - ADAM_B2 ** ADAM_STEP

    def fn(ins, consts):
        wv, gv, mv, vv = ins
        m2 = ADAM_B1 * mv + (1.0 - ADAM_B1) * gv
        v2 = ADAM_B2 * vv + (1.0 - ADAM_B2) * (gv * gv)
        delta = -ADAM_LR * ((m2 / c1) / (jnp.sqrt(v2 / c2) + ADAM_EPS) + ADAM_WD * wv)
        return [delta, m2, v2], []

    outs, _ = _rowwise(fn, [(a.reshape(rows, cols), cols, 0) for a in (w, g, m, v)], [], [(cols, F32)] * 3, [],
                       bm=bm, name="adamw")
    return [o.reshape(shape) for o in outs]


BIG = ("w1t", "w1o", "wint", "wa", "wbt", "wo", "w2t", "w2o")
FETCH_GROUPS = dict(w1t=("w1t",), w1o=("w1o",), wint=("wint",), mout=("wa", "wbt", "wo"), w2t=("w2t",), w2o=("w2o",))
SMALL_ROWS = (("ffn1_norm", 0), ("mix_norm", 2), ("lbsum", 4), ("hgrn_out_norm", 6), ("ffn2_norm", 8),
              ("attn_q_norm", 10), ("attn_k_norm", 12))
SMALL_PACK_ROWS = 16


def kernel(x, ffn1_norm, ffn1_w_in, ffn1_w_out, mix_norm, w_in, hgrn_lb_logits, hgrn_out_norm, attn_q_norm, attn_k_norm, w_branch_a, w_branch_b, w_out, ffn2_norm, ffn2_w_in, ffn2_w_out, loss_target, m_ffn1_norm, m_ffn1_w_in, m_ffn1_w_out, m_mix_norm, m_w_in, m_hgrn_lb_logits, m_hgrn_out_norm, m_attn_q_norm, m_attn_k_norm, m_w_branch_a, m_w_branch_b, m_w_out, m_ffn2_norm, m_ffn2_w_in, m_ffn2_w_out, v_ffn1_norm, v_ffn1_w_in, v_ffn1_w_out, v_mix_norm, v_w_in, v_hgrn_lb_logits, v_hgrn_out_norm, v_attn_q_norm, v_attn_k_norm, v_w_branch_a, v_w_branch_b, v_w_out, v_ffn2_norm, v_ffn2_w_in, v_ffn2_w_out):
    names = ("ffn1_norm", "ffn1_w_in", "ffn1_w_out", "mix_norm", "w_in", "hgrn_lb_logits", "hgrn_out_norm", "attn_q_norm",
             "attn_k_norm", "w_branch_a", "w_branch_b", "w_out", "ffn2_norm", "ffn2_w_in", "ffn2_w_out")
    w = dict(zip(names, (ffn1_norm, ffn1_w_in, ffn1_w_out, mix_norm, w_in, hgrn_lb_logits, hgrn_out_norm, attn_q_norm,
                         attn_k_norm, w_branch_a, w_branch_b, w_out, ffn2_norm, ffn2_w_in, ffn2_w_out)))
    m = dict(zip(names, (m_ffn1_norm, m_ffn1_w_in, m_ffn1_w_out, m_mix_norm, m_w_in, m_hgrn_lb_logits, m_hgrn_out_norm,
                         m_attn_q_norm, m_attn_k_norm, m_w_branch_a, m_w_branch_b, m_w_out, m_ffn2_norm, m_ffn2_w_in, m_ffn2_w_out)))
    v = dict(zip(names, (v_ffn1_norm, v_ffn1_w_in, v_ffn1_w_out, v_mix_norm, v_w_in, v_hgrn_lb_logits, v_hgrn_out_norm,
                         v_attn_q_norm, v_attn_k_norm, v_w_branch_a, v_w_branch_b, v_w_out, v_ffn2_norm, v_ffn2_w_in, v_ffn2_w_out)))
    depth, d = ffn1_norm.shape

    def tr(a):
        return jnp.swapaxes(a, 1, 2)

    shard = dict(w1t=tr(ffn1_w_in), w1o=ffn1_w_out, wint=tr(w_in), wa=w_branch_a,
                 wbt=tr(w_branch_b).reshape(depth, -1, d), wo=w_out, w2t=tr(ffn2_w_in), w2o=ffn2_w_out)
    order = [(g, l) for l in range(depth) for g in FETCH_GROUPS]
    started = {}
    for name, part in (("gather_start", order),):
        flat = [(l, k) for g, l in part for k in FETCH_GROUPS[g]]
        groups, pos = [], 0
        for g, l in part:
            groups.append(list(range(pos, pos + len(FETCH_GROUPS[g]))))
            pos += len(FETCH_GROUPS[g])
        sems, srcs, lands, _ = _xchg_start([shard[k][l].astype(BF16) for l, k in flat], ["gather"] * len(flat), groups, name)
        for gi, key in enumerate(part):
            started[key] = ([srcs[i] for i in groups[gi]], [lands[i] for i in groups[gi]], sems[gi])

    def fetch(group, l, after):
        srcs, lands, sems = started[group, l]
        lands = _xchg_wait(srcs, lands, ["gather"] * len(srcs), sems, after, f"gather_wait_{group}{l}")
        out = {}
        for k, land in zip(FETCH_GROUPS[group], lands):
            out[k] = land.reshape(d, -1) if k == "wbt" else land.reshape(-1, d)
        return out

    pending = []

    def emit(group, l, g, final):
        keys = list(g)
        srcs = [g[k].reshape(N_DEV, -1, d) for k in keys]
        modes = ["scatter"] * len(keys)
        if final is not None:
            gsmall, loss = final
            pack = jnp.zeros((SMALL_PACK_ROWS, d), F32)
            for k, r0 in SMALL_ROWS:
                rows = gsmall[k].reshape(depth, -1)
                pack = pack.at[r0:r0 + depth, :rows.shape[1]].set(rows)
            srcs.append(pack.at[14, :].set(loss))
            modes.append("gather")
            keys.append("small")
        sems, s_thru, l_thru, token = _xchg_start(srcs, modes, [list(range(len(srcs)))], f"grads_start_{group}{l}")
        pending.append((group, l, keys, modes, sems[0], s_thru, l_thru))
        return token

    small = {k: w[k] for k in ("ffn1_norm", "mix_norm", "hgrn_lb_logits", "hgrn_out_norm", "attn_q_norm", "attn_k_norm", "ffn2_norm")}
    dx = _local_step(x[0], loss_target[0], small, fetch, emit)

    summed, after = {}, dx
    for group, l, keys, modes, sems, s_thru, l_thru in pending:
        lands = _xchg_wait(s_thru, l_thru, modes, sems, after, f"grads_wait_{group}{l}")
        for k, land in zip(keys, lands):
            summed[k, l] = _sum_slots(land[None])[0]
        after = summed[keys[-1], l]
    gsum = {k: jnp.stack([summed[k, l] for l in range(depth)]) for k in BIG}
    tot = summed["small", 0]

    grads = {}
    for k, r0 in SMALL_ROWS:
        shp = (depth,) + (w[k].shape[1:] if k != "lbsum" else (d,))
        grads[k] = tot[r0:r0 + depth, :int(np.prod(shp[1:]))].reshape(shp)
    _, lb_vjp = jax.vjp(_lower_bounds, hgrn_lb_logits)
    grads["hgrn_lb_logits"] = lb_vjp(grads.pop("lbsum"))[0]
    grads["ffn1_w_in"], grads["ffn1_w_out"] = tr(gsum["w1t"]), gsum["w1o"]
    grads["w_in"], grads["w_branch_a"] = tr(gsum["wint"]), gsum["wa"]
    grads["w_branch_b"] = tr(gsum["wbt"].reshape(depth, d // N_DEV, -1))
    grads["w_out"] = gsum["wo"]
    grads["ffn2_w_in"], grads["ffn2_w_out"] = tr(gsum["w2t"]), gsum["w2o"]

    upd = {k: _adamw(w[k], grads[k], m[k], v[k]) for k in names}
    return (tot[14, 0], dx[None], *[grads[k] for k in names], *[upd[k][0] for k in names],
            *[upd[k][1] for k in names], *[upd[k][2] for k in names])
```

```python
import functools

import jax
import jax.numpy as jnp
import numpy as np
from jax import lax
from jax.experimental import pallas as pl
from jax.experimental.pallas import tpu as pltpu

F32 = jnp.float32
BF16 = jnp.bfloat16

N_DEV = 8
EPS = 1e-6
HG_DK = 128
HG_CHUNK = 64
HG_SUB = 16
HG_HP = 8
ATT_PATTERNS = ((128, 1), (512, 4), (2048, 16))
ATT_GROUPS = 3
ATT_HEADS = 4
ATT_DH = 128
ATT_BLK = 128
ROPE_THETA = 10000.0
ADAM_LR, ADAM_B1, ADAM_B2, ADAM_EPS, ADAM_WD, ADAM_STEP = 0.001, 0.9, 0.999, 1e-08, 0.01, 10
VMEM_LIMIT_BYTES = 56 * 1024 * 1024
MXU_COLS = 256
MESH = pl.DeviceIdType.MESH


def _cparams(sem, **kw):
    return pltpu.CompilerParams(dimension_semantics=sem, vmem_limit_bytes=VMEM_LIMIT_BYTES, **kw)


def _sigmoid(x):
    return 1.0 / (1.0 + jnp.exp(-x))


def _mm(a_list, b_list, pairs, n_acc, fin, out_dtypes, *, m, n, k, ta=False, tb=False, bm, bn, bk,
        b_off=None, extras=(), e_off=None, n_outer=False, consts=(), a_pro=None, n_sums=0, chunk=0, a_cat=False, name):
    bm, bn, bk = min(bm, m), min(bn, n), min(bk, k)
    assert m % bm == 0 and n % bn == 0 and k % bk == 0, (name, m, n, k, bm, bn, bk)
    nk = k // bk
    assert not (a_pro and (nk > 1 or ta or n_outer)) and not (n_sums and (bn != n or n_outer)), name
    assert not (chunk and (nk > 1 or n_sums or chunk % 128)), name
    if a_cat:
        unit = bm if ta else bk
        widths = [a.shape[1] for a in a_list]
        assert all(w % unit == 0 for w in widths) and sum(widths) == (m if ta else k) and not a_pro, name
        cat_counts = [w // unit for w in widths]
        cat_starts = [sum(cat_counts[:i]) for i in range(len(widths))]
    b_off = b_off or [(0, 0)] * len(b_list)
    e_off = e_off or [0] * len(extras)
    na, nb, ne, nc, no = len(a_list), len(b_list), len(extras), len(consts), len(out_dtypes)
    nao = na if a_pro else 0
    dn = (((0,) if ta else (1,), (1,) if tb else (0,)), ((), ()))

    def body(*refs):
        refs = list(refs)
        a_refs, b_refs, e_refs, c_refs, o_refs, ao_refs, s_refs = (
            [refs.pop(0) for _ in range(cnt)] for cnt in (na, nb, ne, nc, no, nao, n_sums))
        acc_refs = refs
        kk = pl.program_id(2)
        first = pl.program_id(0) == 0
        cvals = [c[...] for c in c_refs]
        a_vals = [r[...] for r in a_refs]
        if a_cat:
            col = pl.program_id(1 if n_outer else 0) if ta else kk
            sel = a_vals[0]
            for start, v in zip(cat_starts[1:], a_vals[1:]):
                sel = jnp.where(col >= start, v, sel)
            a_vals = [sel]
        if a_pro:
            @pl.when(pl.program_id(1) == 0)
            def _():
                for r, v in zip(ao_refs, a_pro(a_vals, cvals)):
                    r[...] = v

            a_vals = [r[...] for r in ao_refs]
        if chunk:
            spans = [slice(lo, min(lo + chunk, bn)) for lo in range(0, bn, chunk)]
            chunks = []
            for cs in spans:
                parts = [None] * n_acc
                for ai, bi, ci in pairs:
                    p = lax.dot_general(a_vals[ai], b_refs[bi][cs, :] if tb else b_refs[bi][:, cs], dn,
                                        preferred_element_type=F32)
                    parts[ci] = p if parts[ci] is None else parts[ci] + p
                chunks.append(parts)
            for cs, parts in zip(spans, chunks):
                ex = [e[:, cs] for e in e_refs]
                outs = fin(parts, ex, cvals) if nc else fin(parts, ex)
                for o_ref, o in zip(o_refs, outs):
                    o_ref[:, cs] = o.astype(o_ref.dtype)
            return

        parts = [None] * n_acc
        for ai, bi, ci in pairs:
            p = lax.dot_general(a_vals[ai], b_refs[bi][...], dn, preferred_element_type=F32)
            parts[ci] = p if parts[ci] is None else parts[ci] + p

        def finish(accs):
            ex = [e[...] for e in e_refs]
            res = fin(accs, ex, cvals) if nc else fin(accs, ex)
            outs, sums = res if n_sums else (res, ())
            for o_ref, o in zip(o_refs, outs):
                o_ref[...] = o.astype(o_ref.dtype)
            if n_sums:
                @pl.when(first)
                def _():
                    for s_ref, s in zip(s_refs, sums):
                        s_ref[...] = s

                @pl.when(jnp.logical_not(first))
                def _():
                    for s_ref, s in zip(s_refs, sums):
                        s_ref[...] += s

        if nk == 1:
            finish(parts)
        else:
            @pl.when(kk == 0)
            def _():
                for c in range(n_acc):
                    acc_refs[c][...] = parts[c]

            @pl.when(kk > 0)
            def _():
                for c in range(n_acc):
                    acc_refs[c][...] += parts[c]

            @pl.when(kk == nk - 1)
            def _():
                finish([acc_refs[c][...] for c in range(n_acc)])

    def ij(f):
        return (lambda j, i, q: f(i, j, q)) if n_outer else f

    a_spec = pl.BlockSpec((bk, bm), ij(lambda i, j, q: (q, i))) if ta else pl.BlockSpec((bm, bk), ij(lambda i, j, q: (i, q)))
    a_specs = [a_spec] * na
    if a_cat:
        def part_spec(start, count):
            def col(c):
                return jnp.clip(c - start, 0, count - 1)
            if ta:
                return pl.BlockSpec((bk, bm), ij(lambda i, j, q: (q, col(i))))
            return pl.BlockSpec((bm, bk), ij(lambda i, j, q: (i, col(q))))
        a_specs = [part_spec(s, c) for s, c in zip(cat_starts, cat_counts)]

    b_mode = dict(pipeline_mode=pl.Buffered(1)) if (bn == n and nk == 1) else {}

    def b_spec(off):
        on, ok = off
        if tb:
            return pl.BlockSpec((bn, bk), ij(lambda i, j, q: (j + on, q + ok)), **b_mode)
        return pl.BlockSpec((bk, bn), ij(lambda i, j, q: (q + ok, j + on)), **b_mode)

    mn_spec = pl.BlockSpec((bm, bn), ij(lambda i, j, q: (i, j)))
    outs = pl.pallas_call(
        body,
        out_shape=[jax.ShapeDtypeStruct((m, n), d) for d in out_dtypes] + [jax.ShapeDtypeStruct((m, k), BF16)] * nao
        + [jax.ShapeDtypeStruct((8, n), F32)] * n_sums,
        grid=(n // bn, m // bm, nk) if n_outer else (m // bm, n // bn, nk),
        in_specs=a_specs + [b_spec(o) for o in b_off]
        + [pl.BlockSpec((bm, bn), ij(lambda i, j, q, o=o: (i, j + o))) for o in e_off]
        + [pl.BlockSpec(c.shape, lambda *_, nd=c.ndim: (0,) * nd) for c in consts],
        out_specs=[mn_spec] * no + [a_spec] * nao + [pl.BlockSpec((8, n), lambda *_: (0, 0))] * n_sums,
        scratch_shapes=[pltpu.VMEM((bm, bn), F32) for _ in range(n_acc if nk > 1 else 0)],
        compiler_params=_cparams(("arbitrary" if n_sums else "parallel", "parallel", "arbitrary")),
        name=name,
    )(*a_list, *b_list, *extras, *consts)
    return outs


def _first(accs, ex):
    return (accs[0],)


def _rowwise(fn, ins, consts, out_defs, sum_widths, *, bm, name):
    ins = [tuple(e) + (1,) * (4 - len(e)) for e in ins]
    out_defs = [tuple(e) + (1,) * (3 - len(e)) for e in out_defs]
    t = ins[0][0].shape[-2] * ins[0][3]
    bm = min(bm, t)
    assert t % bm == 0, (name, t, bm)
    ni, nc, no, ns = len(ins), len(consts), len(out_defs), len(sum_widths)
    strided = [w for _, w, _, d in ins if d > 1] + [w for w, _, d in out_defs if d > 1]

    def body(*refs):
        i_refs, c_refs = refs[:ni], refs[ni:ni + nc]
        o_refs, s_refs = refs[ni + nc:ni + nc + no], refs[ni + nc + no:ni + nc + no + ns]
        scratch = list(refs[ni + nc + no + ns:])
        vals = []
        for ref, (_, w, _, d) in zip(i_refs, ins):
            if d == 1:
                vals.append(ref[...])
                continue
            s = scratch.pop(0)
            for r in range(d):
                for c in range(w // 128):
                    s.at[c][pl.ds(r, bm // d, stride=d), :] = ref[r, :, c * 128:(c + 1) * 128].astype(F32)
            vals.append(jnp.concatenate([s[c] for c in range(w // 128)], axis=1))
        outs, sums = fn(vals, [r[...] for r in c_refs])
        for o_ref, o, (w, _, d) in zip(o_refs, outs, out_defs):
            if d == 1:
                o_ref[...] = o.astype(o_ref.dtype)
                continue
            s = scratch.pop(0)
            for c in range(w // 128):
                s[c] = o[:, c * 128:(c + 1) * 128].astype(F32)
            for r in range(d):
                for c in range(w // 128):
                    o_ref[r, :, c * 128:(c + 1) * 128] = s.at[c][pl.ds(r, bm // d, stride=d), :].astype(o_ref.dtype)
        if ns:
            first = pl.program_id(0) == 0

            @pl.when(first)
            def _():
                for s_ref, s in zip(s_refs, sums):
                    s_ref[...] = s

            @pl.when(jnp.logical_not(first))
            def _():
                for s_ref, s in zip(s_refs, sums):
                    s_ref[...] += s

    def win(width, cb, d):
        if d > 1:
            return pl.BlockSpec((d, bm // d, width), lambda i: (0, i, 0))
        return pl.BlockSpec((bm, width), lambda i: (i, cb))

    res = pl.pallas_call(
        body,
        out_shape=[jax.ShapeDtypeStruct((t, w) if d == 1 else (d, t // d, w), dt) for w, dt, d in out_defs]
        + [jax.ShapeDtypeStruct((8, w), F32) for w in sum_widths],
        grid=(t // bm,),
        in_specs=[win(w, cb, d) for _, w, cb, d in ins] + [pl.BlockSpec(c.shape, lambda i, nd=c.ndim: (0,) * nd) for c in consts],
        out_specs=[win(w, 0, d) for w, _, d in out_defs] + [pl.BlockSpec((8, w), lambda i: (0, 0)) for w in sum_widths],
        scratch_shapes=[pltpu.VMEM((w // 128, bm, 128), F32) for w in strided],
        compiler_params=_cparams(("arbitrary",) if ns else ("parallel",)),
        name=name,
    )(*[e[0] for e in ins], *consts)
    return res[:no], [jnp.sum(s, axis=0) for s in res[no:]]


def _colsum8(x):
    bm, w = x.shape
    return jnp.sum(x.reshape(bm // 8, 8, w), axis=0)


def _tri(n, upper=False):
    r = lax.broadcasted_iota(jnp.int32, (n, n), 0)
    c = lax.broadcasted_iota(jnp.int32, (n, n), 1)
    return (c >= r) if upper else (c <= r)


def _exact_tri_matmul(tri_bf16, x):
    x0 = x.astype(BF16)
    r1 = x - x0.astype(F32)
    x1 = r1.astype(BF16)
    x2 = (r1 - x1.astype(F32)).astype(BF16)
    w = x.shape[1]
    y = jnp.dot(tri_bf16, jnp.concatenate([x0, x1, x2], axis=1), preferred_element_type=F32)
    return y[:, :w] + y[:, w:2 * w] + y[:, 2 * w:]


def _dot_nt(a, b):
    return lax.dot_general(a, b, (((1,), (1,)), ((), ())), preferred_element_type=F32)


def _dot_tn(a, b):
    return lax.dot_general(a, b, (((0,), (0,)), ((), ())), preferred_element_type=F32)


def _dot(a, b):
    return jnp.dot(a, b, preferred_element_type=F32)


def _hg_gates(hq, hf, lb):
    sq = _sigmoid(hq)
    q = hq * sq
    sg = _sigmoid(hf)
    f = lb + (1.0 - lb) * sg
    return q, sq, sg, f


def _hg_heads(x, hp):
    return [x[:, h * HG_DK:(h + 1) * HG_DK] for h in range(hp)]


def _hg_intra_wide(q, kk, g, hp):
    c = q.shape[0]
    rows = lax.broadcasted_iota(jnp.int32, (c, 1), 0)
    a_rows = [[] for _ in range(hp)]
    qts, kts, eqs, eks = [], [], [], []
    for i in range(c // HG_SUB):
        lo = i * HG_SUB
        ref = g[lo - 1:lo, :] if i else jnp.zeros_like(g[0:1, :])
        eq = jnp.exp(g[lo:lo + HG_SUB, :] - ref)
        ek = jnp.exp(jnp.where(rows < lo + HG_SUB, ref - g, 0.0))
        qtb = (q[lo:lo + HG_SUB, :] * eq).astype(BF16)
        ktb = (kk * ek).astype(BF16)
        tpos = lo + lax.broadcasted_iota(jnp.int32, (HG_SUB, c), 0)
        spos = lax.broadcasted_iota(jnp.int32, (HG_SUB, c), 1)
        for h, (qh, kh) in enumerate(zip(_hg_heads(qtb, hp), _hg_heads(ktb, hp))):
            a_rows[h].append(jnp.where(spos <= tpos, _dot_nt(qh, kh), 0.0))
        qts.append(qtb), kts.append(ktb), eqs.append(eq), eks.append(ek)
    return [jnp.concatenate(r, axis=0) for r in a_rows], qts, kts, eqs, eks


def _hgrn_fwd(zh, lb3, *, tb=512):
    t = zh.shape[0]
    nh = lb3.shape[0]
    c = HG_CHUNK
    tb = min(tb, t)
    nchunk = tb // c
    hp = HG_HP if nh % HG_HP == 0 else 1
    wp = hp * HG_DK

    def body(hq_ref, hf_ref, hi_ref, lb_ref, o_ref, st_ref, state):
        @pl.when(pl.program_id(1) == 0)
        def _():
            state[...] = jnp.zeros_like(state)

        tril = _tri(c).astype(BF16)

        def chunk(ci, carry):
            sl = pl.ds(pl.multiple_of(ci * c, c), c)
            q, _, _, f = _hg_gates(hq_ref[sl, :], hf_ref[sl, :], lb_ref[...])
            kk = 1.0 - f
            g = _exact_tri_matmul(tril, jnp.log(f))
            a, _, _, _, _ = _hg_intra_wide(q, kk, g, hp)
            vb = hi_ref[sl, :].astype(BF16)
            glast = g[c - 1:c, :]
            qgb = (q * jnp.exp(g)).astype(BF16)
            kgb = (kk * jnp.exp(glast - g)).astype(BF16)
            dec = jnp.exp(glast)
            sts = [state[h] for h in range(hp)]
            for h in range(hp):
                st_ref[h, ci] = sts[h]
            vh, qgh, kgh, dech = _hg_heads(vb, hp), _hg_heads(qgb, hp), _hg_heads(kgb, hp), _hg_heads(dec, hp)
            o = [_dot(a[h].astype(BF16), vh[h]) + _dot_nt(qgh[h], sts[h].astype(BF16)) for h in range(hp)]
            new = [_dot_tn(vh[h], kgh[h]) for h in range(hp)]
            o_ref[sl, :] = jnp.concatenate(o, axis=1)
            for h in range(hp):
                state[h] = sts[h] * dech[h] + new[h]
            return carry

        lax.fori_loop(0, nchunk, chunk, 0)

    def col(cb):
        return pl.BlockSpec((tb, wp), lambda h, i: (i, cb * (nh // hp) + h))

    return pl.pallas_call(
        body,
        out_shape=[jax.ShapeDtypeStruct((t, nh * HG_DK), F32), jax.ShapeDtypeStruct((nh, t // c, HG_DK, HG_DK), F32)],
        grid=(nh // hp, t // tb),
        in_specs=[col(0), col(1), col(2), pl.BlockSpec((1, wp), lambda h, i: (0, h))],
        out_specs=[pl.BlockSpec((tb, wp), lambda h, i: (i, h)),
                   pl.BlockSpec((hp, nchunk, HG_DK, HG_DK), lambda h, i: (h, i, 0, 0))],
        scratch_shapes=[pltpu.VMEM((hp, HG_DK, HG_DK), F32)],
        compiler_params=_cparams(("parallel", "arbitrary")),
        name="hgrn_fwd",
    )(zh, zh, zh, lb3.reshape(1, -1))


def _hgrn_bwd(zh, lb3, states, d_o, *, tb=512):
    t = zh.shape[0]
    nh = lb3.shape[0]
    c = HG_CHUNK
    tb = min(tb, t)
    nchunk = tb // c
    nblk = t // tb
    hp = HG_HP if nh % HG_HP == 0 else 1
    wp = hp * HG_DK

    def body(hq_ref, hf_ref, hi_ref, lb_ref, st_ref, do_ref, dq_ref, df_ref, dv_ref, dlb_ref, dstate):
        @pl.when(pl.program_id(1) == 0)
        def _():
            dstate[...] = jnp.zeros_like(dstate)
            dlb_ref[...] = jnp.zeros_like(dlb_ref)

        tril = _tri(c).astype(BF16)
        triu = _tri(c, upper=True).astype(BF16)
        last_row = lax.broadcasted_iota(jnp.int32, (c, 1), 0) == c - 1
        heads = range(hp)

        def chunk(j, carry):
            ci = nchunk - 1 - j
            sl = pl.ds(pl.multiple_of(ci * c, c), c)
            lb = lb_ref[...]
            hq, hf = hq_ref[sl, :], hf_ref[sl, :]
            q, sq, sg, f = _hg_gates(hq, hf, lb)
            kk = 1.0 - f
            g = _exact_tri_matmul(tril, jnp.log(f))
            a, qts, kts, eqs, eks = _hg_intra_wide(q, kk, g, hp)
            glast = g[c - 1:c, :]
            eg, egl, dec = jnp.exp(g), jnp.exp(glast - g), jnp.exp(glast)
            vb, dob = hi_ref[sl, :].astype(BF16), do_ref[sl, :].astype(BF16)
            qgb, kgb = (q * eg).astype(BF16), (kk * egl).astype(BF16)
            sts = [st_ref[h, ci] for h in heads]
            dsts = [dstate[h] for h in heads]
            stb, dstb = [s.astype(BF16) for s in sts], [s.astype(BF16) for s in dsts]
            vh, doh, qgh, kgh = _hg_heads(vb, hp), _hg_heads(dob, hp), _hg_heads(qgb, hp), _hg_heads(kgb, hp)
            dv = [_dot_tn(a[h].astype(BF16), doh[h]) + _dot_nt(kgh[h], dstb[h]) for h in heads]
            da = [jnp.where(_tri(c), _dot_nt(doh[h], vh[h]), 0.0).astype(BF16) for h in heads]
            dq_inter = jnp.concatenate([_dot(doh[h], stb[h]) for h in heads], axis=1) * eg
            dk_state = jnp.concatenate([_dot(vh[h], dstb[h]) for h in heads], axis=1) * egl
            new_dst = [_dot_tn(doh[h], qgh[h]) for h in heads]
            xs, dk, dgk = [], dk_state, 0.0
            for i in range(c // HG_SUB):
                rs = slice(i * HG_SUB, (i + 1) * HG_SUB)
                kth, qth = _hg_heads(kts[i], hp), _hg_heads(qts[i], hp)
                xi = jnp.concatenate([_dot(da[h][rs, :], kth[h]) for h in heads], axis=1)
                yi = jnp.concatenate([_dot_tn(da[h][rs, :], qth[h]) for h in heads], axis=1)
                xs.append(xi)
                dk = dk + yi * eks[i]
                dgk = dgk + yi * kts[i].astype(F32)
            dq = jnp.concatenate([x * e for x, e in zip(xs, eqs)], axis=0) + dq_inter
            dgq = jnp.concatenate([x * qt.astype(F32) for x, qt in zip(xs, qts)], axis=0)
            dg = dgq - dgk + q * dq_inter - kk * dk_state
            sdot = jnp.concatenate([jnp.sum(sts[h] * dsts[h], axis=0, keepdims=True) for h in heads], axis=1)
            dgl = jnp.sum(kk * dk_state, axis=0, keepdims=True) + dec * sdot
            dg = dg + jnp.where(last_row, dgl, 0.0)
            dlogf = _exact_tri_matmul(triu, dg)
            dfv = dlogf / f - dk
            dq_ref[sl, :] = (dq * (sq * (1.0 + hq * (1.0 - sq)))).astype(dq_ref.dtype)
            df_ref[sl, :] = (dfv * (1.0 - lb) * sg * (1.0 - sg)).astype(df_ref.dtype)
            dv_ref[sl, :] = jnp.concatenate(dv, axis=1).astype(dv_ref.dtype)
            dlb_ref[...] += jnp.sum(dfv * (1.0 - sg), axis=0, keepdims=True)
            dech = _hg_heads(dec, hp)
            for h in heads:
                dstate[h] = dsts[h] * dech[h] + new_dst[h]
            return carry

        lax.fori_loop(0, nchunk, chunk, 0)

    def col(cb):
        return pl.BlockSpec((tb, wp), lambda h, i: (nblk - 1 - i, cb * (nh // hp) + h))

    ocol = pl.BlockSpec((tb, wp), lambda h, i: (nblk - 1 - i, h))
    lbspec = pl.BlockSpec((1, wp), lambda h, i: (0, h))
    w = nh * HG_DK
    dq, df, dv, dlb = pl.pallas_call(
        body,
        out_shape=[jax.ShapeDtypeStruct((t, w), BF16)] * 3 + [jax.ShapeDtypeStruct((1, w), F32)],
        grid=(nh // hp, nblk),
        in_specs=[col(0), col(1), col(2), lbspec,
                  pl.BlockSpec((hp, nchunk, HG_DK, HG_DK), lambda h, i: (h, nblk - 1 - i, 0, 0)), ocol],
        out_specs=[ocol, ocol, ocol, lbspec],
        scratch_shapes=[pltpu.VMEM((hp, HG_DK, HG_DK), F32)],
        compiler_params=_cparams(("parallel", "arbitrary")),
        name="hgrn_bwd",
    )(zh, zh, zh, lb3.reshape(1, -1), states, d_o)
    return dq, df, dv, dlb.reshape(w)


NEG = -1e30
ATT_GW = ATT_HEADS * ATT_DH


def _att_scores(q, kp, kc, has_prev):
    scale = ATT_DH ** -0.5
    i = lax.broadcasted_iota(jnp.int32, (ATT_BLK, ATT_BLK), 0)
    j = lax.broadcasted_iota(jnp.int32, (ATT_BLK, ATT_BLK), 1)
    s_p = jnp.where(jnp.logical_and(j >= i, has_prev), _dot_nt(q, kp) * scale, NEG)
    s_c = jnp.where(j <= i, _dot_nt(q, kc) * scale, NEG)
    return s_p, s_c


def _att_views(arrs, d):
    return [a.reshape(d, -1, ATT_GW) for a in arrs]


def _att_unview(a, d):
    return a.reshape(-1, ATT_GW) if d == 1 else a


ATT_QB = 4


def _attn_fwd(qb, kb, vb, g):
    d = ATT_PATTERNS[g][1]
    q2, k2, v2 = _att_views([qb, kb, vb], d)
    nblk = q2.shape[1] // ATT_BLK
    nq = ATT_QB if nblk % ATT_QB == 0 else 1
    rows = nq * ATT_BLK

    def body(q_ref, kc_ref, kp_ref, vc_ref, vp_ref, o_ref, l_ref):
        first = pl.program_id(1) == 0
        hss = [slice(h * ATT_DH, (h + 1) * ATT_DH) for h in range(ATT_HEADS)]
        for b in range(nq):
            rs = slice(b * ATT_BLK, (b + 1) * ATT_BLK)
            ps = slice((b - 1) * ATT_BLK, b * ATT_BLK)
            has_prev = jnp.logical_not(first) if b == 0 else True
            kv = [(kp_ref[:, hs], vp_ref[:, hs]) if b == 0 else (kc_ref[ps, hs], vc_ref[ps, hs]) for hs in hss]
            sc = [_att_scores(q_ref[rs, hs], kv[h][0], kc_ref[rs, hs], has_prev) for h, hs in enumerate(hss)]
            ms = [jnp.maximum(jnp.max(s_p, axis=1, keepdims=True), jnp.max(s_c, axis=1, keepdims=True)) for s_p, s_c in sc]
            ps_ = [(jnp.exp(s_p - m), jnp.exp(s_c - m)) for (s_p, s_c), m in zip(sc, ms)]
            ls = [jnp.sum(p_p, axis=1, keepdims=True) + jnp.sum(p_c, axis=1, keepdims=True) for p_p, p_c in ps_]
            os_ = [_dot(p_p.astype(BF16), kv[h][1]) + _dot(p_c.astype(BF16), vc_ref[rs, hss[h]]) for h, (p_p, p_c) in enumerate(ps_)]
            for h, hs in enumerate(hss):
                o_ref[rs, hs] = os_[h] / ls[h]
                l_ref[rs, hs] = jnp.broadcast_to(ms[h] + jnp.log(ls[h]), (ATT_BLK, ATT_DH))

    cur = pl.BlockSpec((None, rows, ATT_GW), lambda r, n: (r, n, 0))
    prev = pl.BlockSpec((None, ATT_BLK, ATT_GW), lambda r, n: (r, jnp.maximum(n * nq - 1, 0), 0))
    o, lse = pl.pallas_call(
        body,
        out_shape=[jax.ShapeDtypeStruct(q2.shape, F32)] * 2,
        grid=(d, nblk // nq),
        in_specs=[cur, cur, prev, cur, prev],
        out_specs=[cur, cur],
        compiler_params=_cparams(("parallel", "arbitrary")),
        name=f"attn_fwd_g{g}",
    )(q2, k2, k2, v2, v2)
    return _att_unview(o, d), _att_unview(lse, d)


def _attn_bwd(qb, kb, vb, o, lse, d_o, d_lse, g):
    d = ATT_PATTERNS[g][1]
    q2, k2, v2 = _att_views([qb, kb, vb], d)
    o2, l2, do2, dl2 = _att_views([o, lse, d_o, d_lse], d)
    nblk = q2.shape[1] // ATT_BLK
    nq = ATT_QB if nblk % ATT_QB == 0 else 1
    rows = nq * ATT_BLK
    ns = nblk // nq
    scale = ATT_DH ** -0.5

    def body(q_ref, kc_ref, kp_ref, vc_ref, vp_ref, o_ref, l_ref, do_ref, dl_ref, dq_ref, dk_ref, dv_ref, ck, cv):
        n = pl.program_id(1)

        @pl.when(n == 0)
        def _():
            ck[...] = jnp.zeros_like(ck)
            cv[...] = jnp.zeros_like(cv)

        first = n == ns - 1
        hss = [slice(h * ATT_DH, (h + 1) * ATT_DH) for h in range(ATT_HEADS)]
        heads = range(ATT_HEADS)
        pend_k, pend_v = [ck[:, hs] for hs in hss], [cv[:, hs] for hs in hss]
        for b in reversed(range(nq)):
            rs = slice(b * ATT_BLK, (b + 1) * ATT_BLK)
            ps = slice((b - 1) * ATT_BLK, b * ATT_BLK)
            has_prev = jnp.logical_not(first) if b == 0 else True
            q = [q_ref[rs, hs] for hs in hss]
            kc, vc = [kc_ref[rs, hs] for hs in hss], [vc_ref[rs, hs] for hs in hss]
            kp = [kp_ref[:, hs] if b == 0 else kc_ref[ps, hs] for hs in hss]
            vp = [vp_ref[:, hs] if b == 0 else vc_ref[ps, hs] for hs in hss]
            sc = [_att_scores(q[h], kp[h], kc[h], has_prev) for h in heads]
            dob = [do_ref[rs, hs].astype(BF16) for hs in hss]
            dp = [(_dot_nt(dob[h], vp[h]), _dot_nt(dob[h], vc[h])) for h in heads]
            delta = [jnp.sum(do_ref[rs, hs] * o_ref[rs, hs] - dl_ref[rs, hs], axis=1, keepdims=True) for hs in hss]
            pr = [(jnp.exp(sc[h][0] - l_ref[rs, hss[h]][:, 0:1]), jnp.exp(sc[h][1] - l_ref[rs, hss[h]][:, 0:1])) for h in heads]
            ds = [((pr[h][0] * (dp[h][0] - delta[h]) * scale).astype(BF16), (pr[h][1] * (dp[h][1] - delta[h]) * scale).astype(BF16))
                  for h in heads]
            pb = [(pr[h][0].astype(BF16), pr[h][1].astype(BF16)) for h in heads]
            dq = [_dot(ds[h][0], kp[h]) + _dot(ds[h][1], kc[h]) for h in heads]
            dk_c = [_dot_tn(ds[h][1], q[h]) for h in heads]
            dv_c = [_dot_tn(pb[h][1], dob[h]) for h in heads]
            dk_p = [_dot_tn(ds[h][0], q[h]) for h in heads]
            dv_p = [_dot_tn(pb[h][0], dob[h]) for h in heads]
            for h, hs in enumerate(hss):
                dq_ref[rs, hs] = dq[h]
                dk_ref[rs, hs] = pend_k[h] + dk_c[h]
                dv_ref[rs, hs] = pend_v[h] + dv_c[h]
            pend_k, pend_v = dk_p, dv_p
        for h, hs in enumerate(hss):
            ck[:, hs] = pend_k[h]
            cv[:, hs] = pend_v[h]

    cur = pl.BlockSpec((None, rows, ATT_GW), lambda r, n: (r, ns - 1 - n, 0))
    prev = pl.BlockSpec((None, ATT_BLK, ATT_GW), lambda r, n: (r, jnp.maximum((ns - 1 - n) * nq - 1, 0), 0))
    shp = jax.ShapeDtypeStruct(q2.shape, F32)
    dq, dk, dv = pl.pallas_call(
        body,
        out_shape=[shp, shp, shp],
        grid=(d, ns),
        in_specs=[cur, cur, prev, cur, prev, cur, cur, cur, cur],
        out_specs=[cur, cur, cur],
        scratch_shapes=[pltpu.VMEM((ATT_BLK, ATT_GW), F32), pltpu.VMEM((ATT_BLK, ATT_GW), F32)],
        compiler_params=_cparams(("parallel", "arbitrary")),
        name=f"attn_bwd_g{g}",
    )(q2, k2, k2, v2, v2, o2, l2, do2, dl2)
    return _att_unview(dq, d), _att_unview(dk, d), _att_unview(dv, d)


def _rms_parts(x, width):
    outs = []
    for lo in range(0, x.shape[1], width):
        xs = x[:, lo:lo + width].astype(F32)
        r = lax.rsqrt(jnp.mean(xs * xs, axis=1, keepdims=True) + EPS)
        outs.append((xs * r, r))
    return outs


def _rms_bwd_part(xh, r, dxh):
    return r * (dxh - xh * jnp.mean(dxh * xh, axis=1, keepdims=True))


def _norm_pro(a, consts):
    (xh, _), = _rms_parts(a[0], a[0].shape[1])
    return [(xh * consts[0]).astype(BF16)]


def _norm_bwd_fin(accs, ex, consts):
    xv, dres = ex
    (xh, r), = _rms_parts(xv, xv.shape[1])
    dx = dres + _rms_bwd_part(xh, r, accs[0] * consts[0])
    return [dx, dx], [_colsum8(accs[0] * xh)]


def _rot_sign():
    lane = lax.broadcasted_iota(jnp.int32, (1, ATT_DH), 1)
    return jnp.where(lane < ATT_DH // 2, -1.0, 1.0).astype(F32)


def _rope(y, cos, sin):
    return y * cos + pltpu.roll(y, ATT_DH // 2, axis=1) * _rot_sign() * sin


def _rope_t(dy, cos, sin):
    return dy * cos - pltpu.roll(dy * sin, ATT_DH // 2, axis=1) * _rot_sign()


def _qk_prep(zq, zk, zv, qn, kn, cos, sin):
    w = zq.shape[1]

    def fn(ins, consts):
        cs, sn = ins[3], ins[4]
        outs = []
        for z, gain in ((ins[0], consts[0]), (ins[1], consts[1])):
            for i, (xh, _) in enumerate(_rms_parts(z, ATT_DH)):
                outs.append(_rope(xh * gain[:, i * ATT_DH:(i + 1) * ATT_DH], cs, sn))
        outs += [ins[2][:, i * ATT_DH:(i + 1) * ATT_DH] for i in range(w // ATT_DH)]
        groups = [jnp.concatenate(outs[i:i + ATT_HEADS], axis=1) for i in range(0, len(outs), ATT_HEADS)]
        return groups, []

    outs, _ = _rowwise(fn, [(zq, w, 0), (zk, w, 0), (zv, w, 0), (cos, ATT_DH, 0), (sin, ATT_DH, 0)], [qn, kn],
                       [(ATT_GW, BF16, ATT_PATTERNS[g][1]) for g in range(ATT_GROUPS)] * 3, [], bm=256, name="qk_prep")
    return outs[0:3], outs[3:6], outs[6:9]


def _qk_prep_bwd(zq, zk, dq_g, dk_g, dv_g, qn, kn, cos, sin):
    w = zq.shape[1]

    def fn(ins, consts):
        cs, sn = ins[2], ins[3]
        outs, sums = [], []
        for z, gain, dparts in ((ins[0], consts[0], ins[4:7]), (ins[1], consts[1], ins[7:10])):
            dout = jnp.concatenate(dparts, axis=1)
            dz, dgain = [], []
            for i, (xh, r) in enumerate(_rms_parts(z, ATT_DH)):
                hs = slice(i * ATT_DH, (i + 1) * ATT_DH)
                dy = _rope_t(dout[:, hs], cs, sn)
                dgain.append(_colsum8(dy * xh))
                dz.append(_rms_bwd_part(xh, r, dy * gain[:, hs]))
            outs.append(jnp.concatenate(dz, axis=1))
            sums.append(jnp.concatenate(dgain, axis=1))
        outs.append(jnp.concatenate(ins[10:13], axis=1))
        return outs, sums

    ins = [(zq, w, 0), (zk, w, 0), (cos, ATT_DH, 0), (sin, ATT_DH, 0)]
    for parts in (dq_g, dk_g, dv_g):
        ins += [(a, ATT_GW, 0, ATT_PATTERNS[g][1]) for g, a in enumerate(parts)]
    (dzq, dzk, dzv), (dqn, dkn) = _rowwise(fn, ins, [qn, kn], [(w, BF16)] * 3, [w, w], bm=256, name="qk_prep_bwd")
    return dzq, dzk, dzv, dqn, dkn


def _post_a(o_raw, zh, gout):
    w = o_raw.shape[1]

    def fn(ins, consts):
        oh = jnp.concatenate([xh for xh, _ in _rms_parts(ins[0], HG_DK)], axis=1)
        hg = ins[1]
        return [oh * consts[0] * (hg * _sigmoid(hg))], []

    (y,), _ = _rowwise(fn, [(o_raw, w, 0), (zh, w, 3)], [gout.reshape(1, w)], [(w, BF16)], [], bm=512, name="post_a")
    return y


def _post_a_bwd(o_raw, zh, gout, dy):
    w = o_raw.shape[1]

    def fn(ins, consts):
        parts = _rms_parts(ins[0], HG_DK)
        oh = jnp.concatenate([xh for xh, _ in parts], axis=1)
        hg, dyv, gain = ins[1], ins[2], consts[0]
        sg = _sigmoid(hg)
        s = hg * sg
        doh = dyv * gain * s
        do = jnp.concatenate([_rms_bwd_part(xh, r, doh[:, i * HG_DK:(i + 1) * HG_DK]) for i, (xh, r) in enumerate(parts)], axis=1)
        dhg = dyv * oh * gain * (sg * (1.0 + hg * (1.0 - sg)))
        return [do, dhg], [_colsum8(dyv * oh * s)]

    (do, dhg), (dgain,) = _rowwise(fn, [(o_raw, w, 0), (zh, w, 3), (dy, w, 0)], [gout.reshape(1, w)],
                                   [(w, F32), (w, BF16)], [w], bm=512, name="post_a_bwd")
    return do, dhg, dgain


def _merge_alpha(lses):
    m = jnp.maximum(jnp.maximum(lses[0], lses[1]), lses[2])
    e = [jnp.exp(l - m) for l in lses]
    inv = 1.0 / (e[0] + e[1] + e[2])
    return [x * inv for x in e]


def _group_ins(parts):
    return [(a, ATT_GW, 0, ATT_PATTERNS[g][1]) for g, a in enumerate(parts)]


def _merge_b(o_g, lse_g):
    def fn(ins, consts):
        al = _merge_alpha(ins[3:6])
        return [al[0] * ins[0] + al[1] * ins[1] + al[2] * ins[2]], []

    (y,), _ = _rowwise(fn, _group_ins(o_g) + _group_ins(lse_g), [], [(ATT_GW, BF16)], [], bm=512, name="merge_b")
    return y


def _merge_b_bwd(o_g, lse_g, dy):
    def fn(ins, consts):
        al = _merge_alpha(ins[3:6])
        dyv = ins[6]
        dal = [dyv * ins[i] for i in range(3)]
        tot = al[0] * dal[0] + al[1] * dal[1] + al[2] * dal[2]
        return [al[i] * dyv for i in range(3)] + [al[i] * (dal[i] - tot) for i in range(3)], []

    outs, _ = _rowwise(fn, _group_ins(o_g) + _group_ins(lse_g) + [(dy, ATT_GW, 0)], [],
                       [(ATT_GW, F32, ATT_PATTERNS[g][1]) for g in range(ATT_GROUPS)] * 2, [], bm=512, name="merge_b_bwd")
    return outs[:3], outs[3:]


def _loss_head(y, target):
    d = y.shape[1]

    def fn(ins, consts):
        e = ins[0] - ins[1]
        return [e * (1.0 / d)] * 2, [_colsum8(e * e)]

    (dy, dyb), (sq,) = _rowwise(fn, [(y, d, 0), (target, d, 0)], [], [(d, F32), (d, BF16)], [d], bm=512, name="loss_head")
    return 0.5 * jnp.sum(sq) / d, dy, dyb


def _ffn_fwd(x, gain, wt, wo_fn, tag):
    t, d = x.shape
    f = wt.shape[0] // 2

    def act(accs, ex, consts):
        a, b = accs
        s = _sigmoid(a)
        sa = a * s
        return (sa * b, b, 0.5 * sa, 0.5 * (s + sa * (1.0 - s)))

    bn = FFN_BN if f % FFN_BN == 0 else 256
    u, b, sa, sp, h = _mm([x], [wt, wt], [(0, 0, 0), (0, 1, 1)], 2, act, [BF16] * 4, m=t, n=f, k=d, tb=True,
                          bm=512, bn=bn, bk=d, b_off=[(0, 0), (f // min(bn, f), 0)],
                          consts=[gain.reshape(1, d)], a_pro=_norm_pro, chunk=MXU_COLS, name=f"ffn_in_{tag}")
    wo = wo_fn(u)
    (y,) = _mm([u], [wo], [(0, 0, 0)], 1, lambda accs, ex: (ex[0] + 0.5 * accs[0],), [F32], m=t, n=d, k=f,
               bm=1024, bn=d, bk=f, extras=[x], name=f"ffn_out_{tag}")
    return y, (x, h, u, b, sa, sp, wo)


def _ffn_bwd(dy, dyb, saved, gain, wt, tag, tok, emit):
    x, h, u, b, sa, sp, wo = saved
    t, d = x.shape
    f = wo.shape[0]

    def dact(accs, ex, consts):
        bv, sav, spv = (e.astype(F32) for e in ex)
        return (accs[0] * bv * spv, accs[0] * sav)

    bn = FFN_BN if f % FFN_BN == 0 else 256
    da, db = _mm([dyb], [wo], [(0, 0, 0)], 1, dact, [BF16, BF16], m=t, n=f, k=d, tb=True, bm=512, bn=bn, bk=d,
                 extras=[b, sa, sp], n_outer=True, chunk=MXU_COLS, consts=[tok], name=f"ffn_dact_{tag}")
    (dwo,) = _mm([u], [dyb], [(0, 0, 0)], 1, lambda accs, ex: (0.5 * accs[0],), [BF16], m=f, n=d, k=t, ta=True,
                 bm=1408, bn=d, bk=2048, name=f"ffn_dwo_{tag}")
    (dwt,) = _mm([da, db], [h], [(0, 0, 0)], 1, _first, [BF16], m=2 * f, n=d, k=t, ta=True, bm=min(1408, f), bn=d,
                 bk=2048, a_cat=True, name=f"ffn_dwt_{tag}")
    tok = emit(dwt, dwo)
    bk = min(FFN_BN, f)
    dx, dxb, dgain = _mm([da, db], [wt, wt], [(0, 0, 0), (1, 1, 0)], 1, _norm_bwd_fin, [F32, BF16], m=t, n=d, k=f,
                         bm=512, bn=d, bk=bk, b_off=[(0, 0), (0, f // bk)], extras=[x, dy],
                         consts=[gain.reshape(1, d), tok], n_sums=1, name=f"ffn_dh_{tag}")
    return dx, dxb, jnp.sum(dgain, axis=0), tok


FFN_BN = 2816
Z_SPLITS = (("h", 4096), ("q", 1536), ("k", 1536), ("v", 1536), ("g", 2048))


def _mix_fwd(x, p, cos, sin):
    t, d = x.shape
    z, off, hm = {}, 0, None
    for nm, width in Z_SPLITS:
        bn = 1024 if off % 1024 == 0 and width % 1024 == 0 else 512
        first = hm is None
        res = _mm([x if first else hm], [p["wint"]], [(0, 0, 0)], 1, (lambda accs, ex, consts: (accs[0],)) if first else _first,
                  [F32 if nm == "h" else BF16], m=t, n=width, k=d, tb=True, bm=1024 if first else 2048, bn=bn, bk=d,
                  b_off=[(off // bn, 0)],
                  consts=[p["gm"].reshape(1, d)] if first else (), a_pro=_norm_pro if first else None, name=f"mix_in_{nm}")
        z[nm] = res[0]
        hm = res[1] if first else hm
        off += width
    o_raw, states = _hgrn_fwd(z["h"], p["lb3"])
    qb, kb, vb = _qk_prep(z["q"], z["k"], z["v"], p["qn"], p["kn"], cos, sin)
    o_g, lse_g = zip(*[_attn_fwd(qb[g], kb[g], vb[g], g) for g in range(ATT_GROUPS)])
    oa = _post_a(o_raw, z["h"], p["gout"])
    ob = _merge_b(o_g, lse_g)
    late = p["late"](ob)
    p = dict(p, **late)
    (ya,) = _mm([oa], [p["wa"]], [(0, 0, 0)], 1, _first, [F32], m=t, n=d, k=oa.shape[1], bm=1024, bn=d, bk=oa.shape[1],
                name="branch_a")

    def gate(accs, ex):
        return (_sigmoid(ex[0].astype(F32)) * ex[2] + _sigmoid(ex[1].astype(F32)) * accs[0], accs[0])

    merged, yb = _mm([ob], [p["wbt"]], [(0, 0, 0)], 1, gate, [BF16, F32], m=t, n=d, k=ATT_GW, tb=True, bm=512, bn=d,
                     bk=ATT_GW, extras=[z["g"], z["g"], ya], e_off=[0, 1, 0], chunk=MXU_COLS, name="branch_b_gate")
    (y,) = _mm([merged], [p["wo"]], [(0, 0, 0)], 1, lambda accs, ex: (ex[0] + accs[0],), [F32], m=t, n=d, k=d,
               bm=1024, bn=d, bk=d, extras=[x], name="mix_out")
    return y, (x, hm, z, o_raw, states, qb, kb, vb, o_g, lse_g, oa, ob, ya, yb, merged, late)


def _mix_bwd(dy, dyb, saved, p, cos, sin, tok):
    x, hm, z, o_raw, states, qb, kb, vb, o_g, lse_g, oa, ob, ya, yb, merged, late = saved
    p = dict(p, **late)
    t, d = x.shape
    w = oa.shape[1]

    def dgate(accs, ex, consts):
        dm = accs[0]
        sa, sb = _sigmoid(ex[0].astype(F32)), _sigmoid(ex[1].astype(F32))
        return (sa * dm, sb * dm, dm * ex[2] * sa * (1.0 - sa), dm * ex[3] * sb * (1.0 - sb))

    dya, dyb_, dga, dgb = _mm([dyb], [p["wo"]], [(0, 0, 0)], 1, dgate, [BF16] * 4, m=t, n=d, k=d, tb=True, bm=512, bn=d,
                              bk=d, extras=[z["g"], z["g"], ya, yb], e_off=[0, 1, 0, 0], chunk=MXU_COLS, consts=[tok], name="mix_out_bwd")
    (dwo,) = _mm([merged], [dyb], [(0, 0, 0)], 1, _first, [BF16], m=d, n=d, k=t, ta=True, bm=d, bn=d, bk=2048, name="mix_dwo")
    (doa,) = _mm([dya], [p["wa"]], [(0, 0, 0)], 1, _first, [F32], m=t, n=w, k=d, tb=True, bm=1024, bn=w, bk=d, name="branch_a_bwd")
    (dwa,) = _mm([oa], [dya], [(0, 0, 0)], 1, _first, [BF16], m=w, n=d, k=t, ta=True, bm=w, bn=d, bk=2048, name="branch_a_dw")
    (dob,) = _mm([dyb_], [p["wbt"]], [(0, 0, 0)], 1, _first, [F32], m=t, n=ATT_GW, k=d, bm=1024, bn=ATT_GW, bk=d,
                 name="branch_b_bwd")
    (dwbt,) = _mm([dyb_], [ob], [(0, 0, 0)], 1, _first, [BF16], m=d, n=ATT_GW, k=t, ta=True, bm=d, bn=ATT_GW, bk=2048,
                  name="branch_b_dw")
    do_raw, dhg, dgout = _post_a_bwd(o_raw, z["h"], p["gout"], doa)
    do_g, dlse_g = _merge_b_bwd(o_g, lse_g, dob)
    dq_g, dk_g, dv_g = zip(*[_attn_bwd(qb[g], kb[g], vb[g], o_g[g], lse_g[g], do_g[g], dlse_g[g], g)
                             for g in range(ATT_GROUPS)])
    dzq, dzk, dzv, dqn, dkn = _qk_prep_bwd(z["q"], z["k"], dq_g, dk_g, dv_g, p["qn"], p["kn"], cos, sin)
    dhq, dhf, dhi, lbsum = _hgrn_bwd(z["h"], p["lb3"], states, do_raw)
    dz = jnp.concatenate([dhq, dhf, dhi, dhg, dzq, dzk, dzv, dga, dgb], axis=1)
    pw = dz.shape[1]
    (dwint,) = _mm([dz], [hm], [(0, 0, 0)], 1, _first, [BF16], m=pw, n=d, k=t, ta=True, bm=1536, bn=d, bk=2048, name="mix_in_dw")
    dx, dxb, dgm = _mm([dz], [p["wint"]], [(0, 0, 0)], 1, _norm_bwd_fin, [F32, BF16], m=t, n=d, k=pw, bm=1024, bn=d, bk=1536,
                       extras=[x, dy], consts=[p["gm"].reshape(1, d)], n_sums=1, name="mix_in_bwd")
    return dx, dxb, dict(gm=jnp.sum(dgm, axis=0), wint=dwint, lbsum=lbsum, gout=dgout, qn=dqn, kn=dkn, wa=dwa, wbt=dwbt, wo=dwo)


def _rope_tables(t):
    pos = jnp.arange(t, dtype=F32)
    inv = ROPE_THETA ** (-jnp.arange(0, ATT_DH, 2, dtype=F32) / ATT_DH)
    ang = pos[:, None] * inv[None, :]
    ang = jnp.concatenate([ang, ang], axis=-1)
    return jnp.cos(ang), jnp.sin(ang)


def _lower_bounds(logits):
    lb = jnp.cumsum(jax.nn.softmax(logits, axis=0), axis=0)
    return lb - lb[0:1]


def _head_gain(g):
    return jnp.tile(g[:, None, :], (1, ATT_HEADS, 1)).reshape(1, ATT_GROUPS * ATT_GW)


SMALL_GRADS = ("ffn1_norm", "mix_norm", "lbsum", "hgrn_out_norm", "attn_q_norm", "attn_k_norm", "ffn2_norm")


def _local_step(x, target, small, fetch, emit):
    t = x.shape[0]
    depth = small["ffn1_norm"].shape[0]
    cos, sin = _rope_tables(t)
    lb_all = _lower_bounds(small["hgrn_lb_logits"])
    saved = []
    for l in range(depth):
        w1t = fetch("w1t", l, x)["w1t"]
        x, s1 = _ffn_fwd(x, small["ffn1_norm"][l], w1t, lambda after, l=l: fetch("w1o", l, after)["w1o"], "1")
        p = dict(gm=small["mix_norm"][l], wint=fetch("wint", l, x)["wint"], lb3=lb_all[l].reshape(-1, 1, HG_DK),
                 gout=small["hgrn_out_norm"][l], qn=_head_gain(small["attn_q_norm"][l]),
                 kn=_head_gain(small["attn_k_norm"][l]), late=functools.partial(fetch, "mout", l))
        x, sm = _mix_fwd(x, p, cos, sin)
        w2t = fetch("w2t", l, x)["w2t"]
        x, s2 = _ffn_fwd(x, small["ffn2_norm"][l], w2t, lambda after, l=l: fetch("w2o", l, after)["w2o"], "2")
        saved.append((p, w1t, w2t, s1, sm, s2))
    loss, dx, dxb = _loss_head(x, target)
    gsmall = {k: [None] * depth for k in SMALL_GRADS}
    tok = jnp.zeros((8, 128), F32)
    for l in reversed(range(depth)):
        p, w1t, w2t, s1, sm, s2 = saved[l]
        dx, dxb, gsmall["ffn2_norm"][l], tok = _ffn_bwd(
            dx, dxb, s2, small["ffn2_norm"][l], w2t, "2", tok, lambda dwt, dwo, l=l: emit("ffn2", l, dict(w2t=dwt, w2o=dwo), None))
        dx, dxb, gm = _mix_bwd(dx, dxb, sm, p, cos, sin, tok)
        tok = emit("mix", l, {k: gm[k] for k in ("wint", "wa", "wbt", "wo")}, None)
        gsmall["mix_norm"][l], gsmall["lbsum"][l], gsmall["hgrn_out_norm"][l] = gm["gm"], gm["lbsum"], gm["gout"]
        for k, src in (("attn_q_norm", "qn"), ("attn_k_norm", "kn")):
            gsmall[k][l] = jnp.sum(gm[src].reshape(ATT_GROUPS, ATT_HEADS, ATT_DH), axis=1)
        dx, dxb, gsmall["ffn1_norm"][l], tok = _ffn_bwd(
            dx, dxb, s1, small["ffn1_norm"][l], w1t, "1", tok, lambda dwt, dwo, l=l: emit("ffn1", l, dict(w1t=dwt, w1o=dwo), None))
    emit("small", 0, {}, ({k: jnp.stack(v) for k, v in gsmall.items()}, loss))
    return dx


_HBM = pl.BlockSpec(memory_space=pltpu.HBM)
_SEM = pl.BlockSpec(memory_space=pltpu.SEMAPHORE)
_EFFECT = pltpu.SideEffectType.DATAFLOW_SIDE_EFFECTING


def _peer(p):
    x, y, c = lax.axis_index("x"), lax.axis_index("y"), lax.axis_index("c")
    me = 4 * x + 2 * y + c
    return (1 - x if p & 4 else x, 1 - y if p & 2 else y, 1 - c if p & 1 else c), jnp.bitwise_xor(me, p), me


def _xchg_copy(src, land, mode, send_sems, recv_sems, k, p, arriving):
    peer, peer_id, me = _peer(p)
    block = src if mode == "gather" else src.at[peer_id]
    return pltpu.make_async_remote_copy(
        src_ref=block, dst_ref=land.at[peer_id if arriving else me], send_sem=send_sems.at[k * (N_DEV - 1) + p - 1],
        recv_sem=recv_sems.at[k * (N_DEV - 1) + p - 1], device_id=peer, device_id_type=MESH)


def _xchg_start(srcs, modes, groups, name):
    n, ng = len(srcs), len(groups)

    def body(*refs):
        src = refs[:n]
        sems = refs[n:n + 2 * ng]
        land = refs[n + 2 * ng + n:n + 2 * ng + 2 * n]
        token = refs[n + 2 * ng + 2 * n]
        for gi, idx in enumerate(groups):
            for ki, k in enumerate(idx):
                for p in range(1, N_DEV):
                    _xchg_copy(src[k], land[k], modes[k], sems[2 * gi], sems[2 * gi + 1], ki, p, False).start()
        token[...] = jnp.zeros_like(token)

    sem_shapes = []
    for idx in groups:
        sem_shapes += [pltpu.SemaphoreType.DMA((len(idx) * (N_DEV - 1),))] * 2
    outs = pl.pallas_call(
        body,
        out_shape=sem_shapes + [pltpu.HBM(a.shape, a.dtype) for a in srcs]
        + [pltpu.HBM((N_DEV,) + a.shape[-2:], a.dtype) for a in srcs] + [jax.ShapeDtypeStruct((8, 128), F32)],
        in_specs=[_HBM] * n,
        out_specs=[_SEM] * (2 * ng) + [_HBM] * (2 * n) + [pl.BlockSpec(memory_space=pltpu.VMEM)],
        input_output_aliases={i: 2 * ng + i for i in range(n)},
        compiler_params=pltpu.CompilerParams(has_side_effects=_EFFECT),
        name=name,
    )(*[pltpu.with_memory_space_constraint(a, pltpu.HBM) for a in srcs])
    sems = [(outs[2 * gi], outs[2 * gi + 1]) for gi in range(ng)]
    return sems, outs[2 * ng:2 * ng + n], outs[2 * ng + n:2 * ng + 2 * n], outs[-1]


def _xchg_wait_call(srcs, lands, modes, sems, after, name):
    n = len(srcs)

    def body(*refs):
        src, land = refs[:n], refs[n:2 * n]
        send_sems, recv_sems = refs[2 * n], refs[2 * n + 1]
        for p in range(1, N_DEV):
            for k in range(n):
                cp = _xchg_copy(src[k], land[k], modes[k], send_sems, recv_sems, k, p, True)
                cp.wait_send()
                cp.wait_recv()

    outs = pl.pallas_call(
        body,
        out_shape=[pltpu.HBM(a.shape, a.dtype) for a in list(srcs) + list(lands)],
        in_specs=[_HBM] * (2 * n) + [_SEM, _SEM, pl.BlockSpec(memory_space=pl.ANY)],
        out_specs=[_HBM] * (2 * n),
        input_output_aliases={i: i for i in range(2 * n)},
        compiler_params=pltpu.CompilerParams(has_side_effects=_EFFECT),
        name=name,
    )(*srcs, *lands, sems[0], sems[1], after)
    return outs[:n], outs[n:]


def _xchg_wait(srcs, lands, modes, sems, after, name):
    srcs, lands = _xchg_wait_call(srcs, lands, modes, sems, after, name)
    me = 4 * lax.axis_index("x") + 2 * lax.axis_index("y") + lax.axis_index("c")
    done = []
    for a, land, mode in zip(srcs, lands, modes):
        own = a[None] if mode == "gather" else lax.dynamic_slice_in_dim(a, me, 1, axis=0)
        done.append(lax.dynamic_update_slice(land, own, (me, 0, 0)))
    return done


def _sum_slots(land):
    g, _, r, c = land.shape
    br = r // 2 if (r % 32 == 0 and r >= 256) else r

    def body(l_ref, o_ref):
        acc = l_ref[0, 0].astype(F32)
        for j in range(1, N_DEV):
            acc = acc + l_ref[0, j].astype(F32)
        o_ref[0] = acc

    return pl.pallas_call(
        body,
        out_shape=jax.ShapeDtypeStruct((g, r, c), F32),
        grid=(g, r // br),
        in_specs=[pl.BlockSpec((1, N_DEV, br, c), lambda i, j: (i, 0, j, 0))],
        out_specs=pl.BlockSpec((1, br, c), lambda i, j: (i, j, 0)),
        compiler_params=_cparams(("parallel", "parallel")),
        name="sum_slots",
    )(land)


def _adamw(w, g, m, v):
    shape = w.shape
    cols = shape[-1]
    rows = int(np.prod(shape[:-1]))
    bm = max(b for b in range(8, 513, 8) if rows % b == 0) if rows % 8 == 0 else rows
    c1 = 1.0 - ADAM_B1 ** ADAM_STEP
    c2 = 1.0 - ADAM_B2 ** ADAM_STEP

    def fn(ins, consts):
        wv, gv, mv, vv = ins
        m2 = ADAM_B1 * mv + (1.0 - ADAM_B1) * gv
        v2 = ADAM_B2 * vv + (1.0 - ADAM_B2) * (gv * gv)
        delta = -ADAM_LR * ((m2 / c1) / (jnp.sqrt(v2 / c2) + ADAM_EPS) + ADAM_WD * wv)
        return [delta, m2, v2], []

    outs, _ = _rowwise(fn, [(a.reshape(rows, cols), cols, 0) for a in (w, g, m, v)], [], [(cols, F32)] * 3, [],
                       bm=bm, name="adamw")
    return [o.reshape(shape) for o in outs]


BIG = ("w1t", "w1o", "wint", "wa", "wbt", "wo", "w2t", "w2o")
FETCH_GROUPS = dict(w1t=("w1t",), w1o=("w1o",), wint=("wint",), mout=("wa", "wbt", "wo"), w2t=("w2t",), w2o=("w2o",))
SMALL_ROWS = (("ffn1_norm", 0), ("mix_norm", 2), ("lbsum", 4), ("hgrn_out_norm", 6), ("ffn2_norm", 8),
              ("attn_q_norm", 10), ("attn_k_norm", 12))
SMALL_PACK_ROWS = 16


def kernel(x, ffn1_norm, ffn1_w_in, ffn1_w_out, mix_norm, w_in, hgrn_lb_logits, hgrn_out_norm, attn_q_norm, attn_k_norm, w_branch_a, w_branch_b, w_out, ffn2_norm, ffn2_w_in, ffn2_w_out, loss_target, m_ffn1_norm, m_ffn1_w_in, m_ffn1_w_out, m_mix_norm, m_w_in, m_hgrn_lb_logits, m_hgrn_out_norm, m_attn_q_norm, m_attn_k_norm, m_w_branch_a, m_w_branch_b, m_w_out, m_ffn2_norm, m_ffn2_w_in, m_ffn2_w_out, v_ffn1_norm, v_ffn1_w_in, v_ffn1_w_out, v_mix_norm, v_w_in, v_hgrn_lb_logits, v_hgrn_out_norm, v_attn_q_norm, v_attn_k_norm, v_w_branch_a, v_w_branch_b, v_w_out, v_ffn2_norm, v_ffn2_w_in, v_ffn2_w_out):
    names = ("ffn1_norm", "ffn1_w_in", "ffn1_w_out", "mix_norm", "w_in", "hgrn_lb_logits", "hgrn_out_norm", "attn_q_norm",
             "attn_k_norm", "w_branch_a", "w_branch_b", "w_out", "ffn2_norm", "ffn2_w_in", "ffn2_w_out")
    w = dict(zip(names, (ffn1_norm, ffn1_w_in, ffn1_w_out, mix_norm, w_in, hgrn_lb_logits, hgrn_out_norm, attn_q_norm,
                         attn_k_norm, w_branch_a, w_branch_b, w_out, ffn2_norm, ffn2_w_in, ffn2_w_out)))
    m = dict(zip(names, (m_ffn1_norm, m_ffn1_w_in, m_ffn1_w_out, m_mix_norm, m_w_in, m_hgrn_lb_logits, m_hgrn_out_norm,
                         m_attn_q_norm, m_attn_k_norm, m_w_branch_a, m_w_branch_b, m_w_out, m_ffn2_norm, m_ffn2_w_in, m_ffn2_w_out)))
    v = dict(zip(names, (v_ffn1_norm, v_ffn1_w_in, v_ffn1_w_out, v_mix_norm, v_w_in, v_hgrn_lb_logits, v_hgrn_out_norm,
                         v_attn_q_norm, v_attn_k_norm, v_w_branch_a, v_w_branch_b, v_w_out, v_ffn2_norm, v_ffn2_w_in, v_ffn2_w_out)))
    depth, d = ffn1_norm.shape

    def tr(a):
        return jnp.swapaxes(a, 1, 2)

    shard = dict(w1t=tr(ffn1_w_in), w1o=ffn1_w_out, wint=tr(w_in), wa=w_branch_a,
                 wbt=tr(w_branch_b).reshape(depth, -1, d), wo=w_out, w2t=tr(ffn2_w_in), w2o=ffn2_w_out)
    order = [(g, l) for l in range(depth) for g in FETCH_GROUPS]
    started = {}
    for name, part in (("gather_start", order),):
        flat = [(l, k) for g, l in part for k in FETCH_GROUPS[g]]
        groups, pos = [], 0
        for g, l in part:
            groups.append(list(range(pos, pos + len(FETCH_GROUPS[g]))))
            pos += len(FETCH_GROUPS[g])
        sems, srcs, lands, _ = _xchg_start([shard[k][l].astype(BF16) for l, k in flat], ["gather"] * len(flat), groups, name)
        for gi, key in enumerate(part):
            started[key] = ([srcs[i] for i in groups[gi]], [lands[i] for i in groups[gi]], sems[gi])

    def fetch(group, l, after):
        srcs, lands, sems = started[group, l]
        lands = _xchg_wait(srcs, lands, ["gather"] * len(srcs), sems, after, f"gather_wait_{group}{l}")
        out = {}
        for k, land in zip(FETCH_GROUPS[group], lands):
            out[k] = land.reshape(d, -1) if k == "wbt" else land.reshape(-1, d)
        return out

    pending = []

    def emit(group, l, g, final):
        keys = list(g)
        srcs = [g[k].reshape(N_DEV, -1, d) for k in keys]
        modes = ["scatter"] * len(keys)
        if final is not None:
            gsmall, loss = final
            pack = jnp.zeros((SMALL_PACK_ROWS, d), F32)
            for k, r0 in SMALL_ROWS:
                rows = gsmall[k].reshape(depth, -1)
                pack = pack.at[r0:r0 + depth, :rows.shape[1]].set(rows)
            srcs.append(pack.at[14, :].set(loss))
            modes.append("gather")
            keys.append("small")
        sems, s_thru, l_thru, token = _xchg_start(srcs, modes, [list(range(len(srcs)))], f"grads_start_{group}{l}")
        pending.append((group, l, keys, modes, sems[0], s_thru, l_thru))
        return token

    small = {k: w[k] for k in ("ffn1_norm", "mix_norm", "hgrn_lb_logits", "hgrn_out_norm", "attn_q_norm", "attn_k_norm", "ffn2_norm")}
    dx = _local_step(x[0], loss_target[0], small, fetch, emit)

    summed, after = {}, dx
    for group, l, keys, modes, sems, s_thru, l_thru in pending:
        lands = _xchg_wait(s_thru, l_thru, modes, sems, after, f"grads_wait_{group}{l}")
        for k, land in zip(keys, lands):
            summed[k, l] = _sum_slots(land[None])[0]
        after = summed[keys[-1], l]
    gsum = {k: jnp.stack([summed[k, l] for l in range(depth)]) for k in BIG}
    tot = summed["small", 0]

    grads = {}
    for k, r0 in SMALL_ROWS:
        shp = (depth,) + (w[k].shape[1:] if k != "lbsum" else (d,))
        grads[k] = tot[r0:r0 + depth, :int(np.prod(shp[1:]))].reshape(shp)
    _, lb_vjp = jax.vjp(_lower_bounds, hgrn_lb_logits)
    grads["hgrn_lb_logits"] = lb_vjp(grads.pop("lbsum"))[0]
    grads["ffn1_w_in"], grads["ffn1_w_out"] = tr(gsum["w1t"]), gsum["w1o"]
    grads["w_in"], grads["w_branch_a"] = tr(gsum["wint"]), gsum["wa"]
    grads["w_branch_b"] = tr(gsum["wbt"].reshape(depth, d // N_DEV, -1))
    grads["w_out"] = gsum["wo"]
    grads["ffn2_w_in"], grads["ffn2_w_out"] = tr(gsum["w2t"]), gsum["w2o"]

    upd = {k: _adamw(w[k], grads[k], m[k], v[k]) for k in names}
    return (tot[14, 0], dx[None], *[grads[k] for k in names], *[upd[k][0] for k in names],
            *[upd[k][1] for k in names], *[upd[k][2] for k in names])
```

```python
import functools

import jax
import jax.numpy as jnp
import numpy as np
from jax import lax
from jax.experimental import pallas as pl
from jax.experimental.pallas import tpu as pltpu

F32 = jnp.float32
BF16 = jnp.bfloat16

N_DEV = 8
EPS = 1e-6
HG_DK = 128
HG_CHUNK = 64
HG_SUB = 16
HG_HP = 8
ATT_PATTERNS = ((128, 1), (512, 4), (2048, 16))
ATT_GROUPS = 3
ATT_HEADS = 4
ATT_DH = 128
ATT_BLK = 128
ROPE_THETA = 10000.0
ADAM_LR, ADAM_B1, ADAM_B2, ADAM_EPS, ADAM_WD, ADAM_STEP = 0.001, 0.9, 0.999, 1e-08, 0.01, 10
VMEM_LIMIT_BYTES = 56 * 1024 * 1024
MXU_COLS = 256
MESH = pl.DeviceIdType.MESH


def _cparams(sem, **kw):
    return pltpu.CompilerParams(dimension_semantics=sem, vmem_limit_bytes=VMEM_LIMIT_BYTES, **kw)


def _sigmoid(x):
    return 1.0 / (1.0 + jnp.exp(-x))


def _mm(a_list, b_list, pairs, n_acc, fin, out_dtypes, *, m, n, k, ta=False, tb=False, bm, bn, bk,
        b_off=None, extras=(), e_off=None, n_outer=False, consts=(), a_pro=None, n_sums=0, chunk=0, a_cat=False, name):
    bm, bn, bk = min(bm, m), min(bn, n), min(bk, k)
    assert m % bm == 0 and n % bn == 0 and k % bk == 0, (name, m, n, k, bm, bn, bk)
    nk = k // bk
    assert not (a_pro and (nk > 1 or ta or n_outer)) and not (n_sums and (bn != n or n_outer)), name
    assert not (chunk and (nk > 1 or n_sums or chunk % 128)), name
    if a_cat:
        unit = bm if ta else bk
        widths = [a.shape[1] for a in a_list]
        assert all(w % unit == 0 for w in widths) and sum(widths) == (m if ta else k) and not a_pro, name
        cat_counts = [w // unit for w in widths]
        cat_starts = [sum(cat_counts[:i]) for i in range(len(widths))]
    b_off = b_off or [(0, 0)] * len(b_list)
    e_off = e_off or [0] * len(extras)
    na, nb, ne, nc, no = len(a_list), len(b_list), len(extras), len(consts), len(out_dtypes)
    nao = na if a_pro else 0
    dn = (((0,) if ta else (1,), (1,) if tb else (0,)), ((), ()))

    def body(*refs):
        refs = list(refs)
        a_refs, b_refs, e_refs, c_refs, o_refs, ao_refs, s_refs = (
            [refs.pop(0) for _ in range(cnt)] for cnt in (na, nb, ne, nc, no, nao, n_sums))
        acc_refs = refs
        kk = pl.program_id(2)
        first = pl.program_id(0) == 0
        cvals = [c[...] for c in c_refs]
        a_vals = [r[...] for r in a_refs]
        if a_cat:
            col = pl.program_id(1 if n_outer else 0) if ta else kk
            sel = a_vals[0]
            for start, v in zip(cat_starts[1:], a_vals[1:]):
                sel = jnp.where(col >= start, v, sel)
            a_vals = [sel]
        if a_pro:
            @pl.when(pl.program_id(1) == 0)
            def _():
                for r, v in zip(ao_refs, a_pro(a_vals, cvals)):
                    r[...] = v

            a_vals = [r[...] for r in ao_refs]
        if chunk:
            spans = [slice(lo, min(lo + chunk, bn)) for lo in range(0, bn, chunk)]
            chunks = []
            for cs in spans:
                parts = [None] * n_acc
                for ai, bi, ci in pairs:
                    p = lax.dot_general(a_vals[ai], b_refs[bi][cs, :] if tb else b_refs[bi][:, cs], dn,
                                        preferred_element_type=F32)
                    parts[ci] = p if parts[ci] is None else parts[ci] + p
                chunks.append(parts)
            for cs, parts in zip(spans, chunks):
                ex = [e[:, cs] for e in e_refs]
                outs = fin(parts, ex, cvals) if nc else fin(parts, ex)
                for o_ref, o in zip(o_refs, outs):
                    o_ref[:, cs] = o.astype(o_ref.dtype)
            return

        parts = [None] * n_acc
        for ai, bi, ci in pairs:
            p = lax.dot_general(a_vals[ai], b_refs[bi][...], dn, preferred_element_type=F32)
            parts[ci] = p if parts[ci] is None else parts[ci] + p

        def finish(accs):
            ex = [e[...] for e in e_refs]
            res = fin(accs, ex, cvals) if nc else fin(accs, ex)
            outs, sums = res if n_sums else (res, ())
            for o_ref, o in zip(o_refs, outs):
                o_ref[...] = o.astype(o_ref.dtype)
            if n_sums:
                @pl.when(first)
                def _():
                    for s_ref, s in zip(s_refs, sums):
                        s_ref[...] = s

                @pl.when(jnp.logical_not(first))
                def _():
                    for s_ref, s in zip(s_refs, sums):
                        s_ref[...] += s

        if nk == 1:
            finish(parts)
        else:
            @pl.when(kk == 0)
            def _():
                for c in range(n_acc):
                    acc_refs[c][...] = parts[c]

            @pl.when(kk > 0)
            def _():
                for c in range(n_acc):
                    acc_refs[c][...] += parts[c]

            @pl.when(kk == nk - 1)
            def _():
                finish([acc_refs[c][...] for c in range(n_acc)])

    def ij(f):
        return (lambda j, i, q: f(i, j, q)) if n_outer else f

    a_spec = pl.BlockSpec((bk, bm), ij(lambda i, j, q: (q, i))) if ta else pl.BlockSpec((bm, bk), ij(lambda i, j, q: (i, q)))
    a_specs = [a_spec] * na
    if a_cat:
        def part_spec(start, count):
            def col(c):
                return jnp.clip(c - start, 0, count - 1)
            if ta:
                return pl.BlockSpec((bk, bm), ij(lambda i, j, q: (q, col(i))))
            return pl.BlockSpec((bm, bk), ij(lambda i, j, q: (i, col(q))))
        a_specs = [part_spec(s, c) for s, c in zip(cat_starts, cat_counts)]

    b_mode = dict(pipeline_mode=pl.Buffered(1)) if (bn == n and nk == 1) else {}

    def b_spec(off):
        on, ok = off
        if tb:
            return pl.BlockSpec((bn, bk), ij(lambda i, j, q: (j + on, q + ok)), **b_mode)
        return pl.BlockSpec((bk, bn), ij(lambda i, j, q: (q + ok, j + on)), **b_mode)

    mn_spec = pl.BlockSpec((bm, bn), ij(lambda i, j, q: (i, j)))
    outs = pl.pallas_call(
        body,
        out_shape=[jax.ShapeDtypeStruct((m, n), d) for d in out_dtypes] + [jax.ShapeDtypeStruct((m, k), BF16)] * nao
        + [jax.ShapeDtypeStruct((8, n), F32)] * n_sums,
        grid=(n // bn, m // bm, nk) if n_outer else (m // bm, n // bn, nk),
        in_specs=a_specs + [b_spec(o) for o in b_off]
        + [pl.BlockSpec((bm, bn), ij(lambda i, j, q, o=o: (i, j + o))) for o in e_off]
        + [pl.BlockSpec(c.shape, lambda *_, nd=c.ndim: (0,) * nd) for c in consts],
        out_specs=[mn_spec] * no + [a_spec] * nao + [pl.BlockSpec((8, n), lambda *_: (0, 0))] * n_sums,
        scratch_shapes=[pltpu.VMEM((bm, bn), F32) for _ in range(n_acc if nk > 1 else 0)],
        compiler_params=_cparams(("arbitrary" if n_sums else "parallel", "parallel", "arbitrary")),
        name=name,
    )(*a_list, *b_list, *extras, *consts)
    return outs


def _first(accs, ex):
    return (accs[0],)


def _rowwise(fn, ins, consts, out_defs, sum_widths, *, bm, name):
    ins = [tuple(e) + (1,) * (4 - len(e)) for e in ins]
    out_defs = [tuple(e) + (1,) * (3 - len(e)) for e in out_defs]
    t = ins[0][0].shape[-2] * ins[0][3]
    bm = min(bm, t)
    assert t % bm == 0, (name, t, bm)
    ni, nc, no, ns = len(ins), len(consts), len(out_defs), len(sum_widths)
    strided = [w for _, w, _, d in ins if d > 1] + [w for w, _, d in out_defs if d > 1]

    def body(*refs):
        i_refs, c_refs = refs[:ni], refs[ni:ni + nc]
        o_refs, s_refs = refs[ni + nc:ni + nc + no], refs[ni + nc + no:ni + nc + no + ns]
        scratch = list(refs[ni + nc + no + ns:])
        vals = []
        for ref, (_, w, _, d) in zip(i_refs, ins):
            if d == 1:
                vals.append(ref[...])
                continue
            s = scratch.pop(0)
            for r in range(d):
                for c in range(w // 128):
                    s.at[c][pl.ds(r, bm // d, stride=d), :] = ref[r, :, c * 128:(c + 1) * 128].astype(F32)
            vals.append(jnp.concatenate([s[c] for c in range(w // 128)], axis=1))
        outs, sums = fn(vals, [r[...] for r in c_refs])
        for o_ref, o, (w, _, d) in zip(o_refs, outs, out_defs):
            if d == 1:
                o_ref[...] = o.astype(o_ref.dtype)
                continue
            s = scratch.pop(0)
            for c in range(w // 128):
                s[c] = o[:, c * 128:(c + 1) * 128].astype(F32)
            for r in range(d):
                for c in range(w // 128):
                    o_ref[r, :, c * 128:(c + 1) * 128] = s.at[c][pl.ds(r, bm // d, stride=d), :].astype(o_ref.dtype)
        if ns:
            first = pl.program_id(0) == 0

            @pl.when(first)
            def _():
                for s_ref, s in zip(s_refs, sums):
                    s_ref[...] = s

            @pl.when(jnp.logical_not(first))
            def _():
                for s_ref, s in zip(s_refs, sums):
                    s_ref[...] += s

    def win(width, cb, d):
        if d > 1:
            return pl.BlockSpec((d, bm // d, width), lambda i: (0, i, 0))
        return pl.BlockSpec((bm, width), lambda i: (i, cb))

    res = pl.pallas_call(
        body,
        out_shape=[jax.ShapeDtypeStruct((t, w) if d == 1 else (d, t // d, w), dt) for w, dt, d in out_defs]
        + [jax.ShapeDtypeStruct((8, w), F32) for w in sum_widths],
        grid=(t // bm,),
        in_specs=[win(w, cb, d) for _, w, cb, d in ins] + [pl.BlockSpec(c.shape, lambda i, nd=c.ndim: (0,) * nd) for c in consts],
        out_specs=[win(w, 0, d) for w, _, d in out_defs] + [pl.BlockSpec((8, w), lambda i: (0, 0)) for w in sum_widths],
        scratch_shapes=[pltpu.VMEM((w // 128, bm, 128), F32) for w in strided],
        compiler_params=_cparams(("arbitrary",) if ns else ("parallel",)),
        name=name,
    )(*[e[0] for e in ins], *consts)
    return res[:no], [jnp.sum(s, axis=0) for s in res[no:]]


def _colsum8(x):
    bm, w = x.shape
    return jnp.sum(x.reshape(bm // 8, 8, w), axis=0)


def _tri(n, upper=False):
    r = lax.broadcasted_iota(jnp.int32, (n, n), 0)
    c = lax.broadcasted_iota(jnp.int32, (n, n), 1)
    return (c >= r) if upper else (c <= r)


def _exact_tri_matmul(tri_bf16, x):
    x0 = x.astype(BF16)
    r1 = x - x0.astype(F32)
    x1 = r1.astype(BF16)
    x2 = (r1 - x1.astype(F32)).astype(BF16)
    w = x.shape[1]
    y = jnp.dot(tri_bf16, jnp.concatenate([x0, x1, x2], axis=1), preferred_element_type=F32)
    return y[:, :w] + y[:, w:2 * w] + y[:, 2 * w:]


def _dot_nt(a, b):
    return lax.dot_general(a, b, (((1,), (1,)), ((), ())), preferred_element_type=F32)


def _dot_tn(a, b):
    return lax.dot_general(a, b, (((0,), (0,)), ((), ())), preferred_element_type=F32)


def _dot(a, b):
    return jnp.dot(a, b, preferred_element_type=F32)


def _hg_gates(hq, hf, lb):
    sq = _sigmoid(hq)
    q = hq * sq
    sg = _sigmoid(hf)
    f = lb + (1.0 - lb) * sg
    return q, sq, sg, f


def _hg_heads(x, hp):
    return [x[:, h * HG_DK:(h + 1) * HG_DK] for h in range(hp)]


def _hg_intra_wide(q, kk, g, hp):
    c = q.shape[0]
    rows = lax.broadcasted_iota(jnp.int32, (c, 1), 0)
    a_rows = [[] for _ in range(hp)]
    qts, kts, eqs, eks = [], [], [], []
    for i in range(c // HG_SUB):
        lo = i * HG_SUB
        ref = g[lo - 1:lo, :] if i else jnp.zeros_like(g[0:1, :])
        eq = jnp.exp(g[lo:lo + HG_SUB, :] - ref)
        ek = jnp.exp(jnp.where(rows < lo + HG_SUB, ref - g, 0.0))
        qtb = (q[lo:lo + HG_SUB, :] * eq).astype(BF16)
        ktb = (kk * ek).astype(BF16)
        tpos = lo + lax.broadcasted_iota(jnp.int32, (HG_SUB, c), 0)
        spos = lax.broadcasted_iota(jnp.int32, (HG_SUB, c), 1)
        for h, (qh, kh) in enumerate(zip(_hg_heads(qtb, hp), _hg_heads(ktb, hp))):
            a_rows[h].append(jnp.where(spos <= tpos, _dot_nt(qh, kh), 0.0))
        qts.append(qtb), kts.append(ktb), eqs.append(eq), eks.append(ek)
    return [jnp.concatenate(r, axis=0) for r in a_rows], qts, kts, eqs, eks


def _hgrn_fwd(zh, lb3, *, tb=512):
    t = zh.shape[0]
    nh = lb3.shape[0]
    c = HG_CHUNK
    tb = min(tb, t)
    nchunk = tb // c
    hp = HG_HP if nh % HG_HP == 0 else 1
    wp = hp * HG_DK

    def body(hq_ref, hf_ref, hi_ref, lb_ref, o_ref, st_ref, state):
        @pl.when(pl.program_id(1) == 0)
        def _():
            state[...] = jnp.zeros_like(state)

        tril = _tri(c).astype(BF16)

        def chunk(ci, carry):
            sl = pl.ds(pl.multiple_of(ci * c, c), c)
            q, _, _, f = _hg_gates(hq_ref[sl, :], hf_ref[sl, :], lb_ref[...])
            kk = 1.0 - f
            g = _exact_tri_matmul(tril, jnp.log(f))
            a, _, _, _, _ = _hg_intra_wide(q, kk, g, hp)
            vb = hi_ref[sl, :].astype(BF16)
            glast = g[c - 1:c, :]
            qgb = (q * jnp.exp(g)).astype(BF16)
            kgb = (kk * jnp.exp(glast - g)).astype(BF16)
            dec = jnp.exp(glast)
            sts = [state[h] for h in range(hp)]
            for h in range(hp):
                st_ref[h, ci] = sts[h]
            vh, qgh, kgh, dech = _hg_heads(vb, hp), _hg_heads(qgb, hp), _hg_heads(kgb, hp), _hg_heads(dec, hp)
            o = [_dot(a[h].astype(BF16), vh[h]) + _dot_nt(qgh[h], sts[h].astype(BF16)) for h in range(hp)]
            new = [_dot_tn(vh[h], kgh[h]) for h in range(hp)]
            o_ref[sl, :] = jnp.concatenate(o, axis=1)
            for h in range(hp):
                state[h] = sts[h] * dech[h] + new[h]
            return carry

        lax.fori_loop(0, nchunk, chunk, 0)

    def col(cb):
        return pl.BlockSpec((tb, wp), lambda h, i: (i, cb * (nh // hp) + h))

    return pl.pallas_call(
        body,
        out_shape=[jax.ShapeDtypeStruct((t, nh * HG_DK), F32), jax.ShapeDtypeStruct((nh, t // c, HG_DK, HG_DK), F32)],
        grid=(nh // hp, t // tb),
        in_specs=[col(0), col(1), col(2), pl.BlockSpec((1, wp), lambda h, i: (0, h))],
        out_specs=[pl.BlockSpec((tb, wp), lambda h, i: (i, h)),
                   pl.BlockSpec((hp, nchunk, HG_DK, HG_DK), lambda h, i: (h, i, 0, 0))],
        scratch_shapes=[pltpu.VMEM((hp, HG_DK, HG_DK), F32)],
        compiler_params=_cparams(("parallel", "arbitrary")),
        name="hgrn_fwd",
    )(zh, zh, zh, lb3.reshape(1, -1))


def _hgrn_bwd(zh, lb3, states, d_o, *, tb=512):
    t = zh.shape[0]
    nh = lb3.shape[0]
    c = HG_CHUNK
    tb = min(tb, t)
    nchunk = tb // c
    nblk = t // tb
    hp = HG_HP if nh % HG_HP == 0 else 1
    wp = hp * HG_DK

    def body(hq_ref, hf_ref, hi_ref, lb_ref, st_ref, do_ref, dq_ref, df_ref, dv_ref, dlb_ref, dstate):
        @pl.when(pl.program_id(1) == 0)
        def _():
            dstate[...] = jnp.zeros_like(dstate)
            dlb_ref[...] = jnp.zeros_like(dlb_ref)

        tril = _tri(c).astype(BF16)
        triu = _tri(c, upper=True).astype(BF16)
        last_row = lax.broadcasted_iota(jnp.int32, (c, 1), 0) == c - 1
        heads = range(hp)

        def chunk(j, carry):
            ci = nchunk - 1 - j
            sl = pl.ds(pl.multiple_of(ci * c, c), c)
            lb = lb_ref[...]
            hq, hf = hq_ref[sl, :], hf_ref[sl, :]
            q, sq, sg, f = _hg_gates(hq, hf, lb)
            kk = 1.0 - f
            g = _exact_tri_matmul(tril, jnp.log(f))
            a, qts, kts, eqs, eks = _hg_intra_wide(q, kk, g, hp)
            glast = g[c - 1:c, :]
            eg, egl, dec = jnp.exp(g), jnp.exp(glast - g), jnp.exp(glast)
            vb, dob = hi_ref[sl, :].astype(BF16), do_ref[sl, :].astype(BF16)
            qgb, kgb = (q * eg).astype(BF16), (kk * egl).astype(BF16)
            sts = [st_ref[h, ci] for h in heads]
            dsts = [dstate[h] for h in heads]
            stb, dstb = [s.astype(BF16) for s in sts], [s.astype(BF16) for s in dsts]
            vh, doh, qgh, kgh = _hg_heads(vb, hp), _hg_heads(dob, hp), _hg_heads(qgb, hp), _hg_heads(kgb, hp)
            dv = [_dot_tn(a[h].astype(BF16), doh[h]) + _dot_nt(kgh[h], dstb[h]) for h in heads]
            da = [jnp.where(_tri(c), _dot_nt(doh[h], vh[h]), 0.0).astype(BF16) for h in heads]
            dq_inter = jnp.concatenate([_dot(doh[h], stb[h]) for h in heads], axis=1) * eg
            dk_state = jnp.concatenate([_dot(vh[h], dstb[h]) for h in heads], axis=1) * egl
            new_dst = [_dot_tn(doh[h], qgh[h]) for h in heads]
            xs, dk, dgk = [], dk_state, 0.0
            for i in range(c // HG_SUB):
                rs = slice(i * HG_SUB, (i + 1) * HG_SUB)
                kth, qth = _hg_heads(kts[i], hp), _hg_heads(qts[i], hp)
                xi = jnp.concatenate([_dot(da[h][rs, :], kth[h]) for h in heads], axis=1)
                yi = jnp.concatenate([_dot_tn(da[h][rs, :], qth[h]) for h in heads], axis=1)
                xs.append(xi)
                dk = dk + yi * eks[i]
                dgk = dgk + yi * kts[i].astype(F32)
            dq = jnp.concatenate([x * e for x, e in zip(xs, eqs)], axis=0) + dq_inter
            dgq = jnp.concatenate([x * qt.astype(F32) for x, qt in zip(xs, qts)], axis=0)
            dg = dgq - dgk + q * dq_inter - kk * dk_state
            sdot = jnp.concatenate([jnp.sum(sts[h] * dsts[h], axis=0, keepdims=True) for h in heads], axis=1)
            dgl = jnp.sum(kk * dk_state, axis=0, keepdims=True) + dec * sdot
            dg = dg + jnp.where(last_row, dgl, 0.0)
            dlogf = _exact_tri_matmul(triu, dg)
            dfv = dlogf / f - dk
            dq_ref[sl, :] = (dq * (sq * (1.0 + hq * (1.0 - sq)))).astype(dq_ref.dtype)
            df_ref[sl, :] = (dfv * (1.0 - lb) * sg * (1.0 - sg)).astype(df_ref.dtype)
            dv_ref[sl, :] = jnp.concatenate(dv, axis=1).astype(dv_ref.dtype)
            dlb_ref[...] += jnp.sum(dfv * (1.0 - sg), axis=0, keepdims=True)
            dech = _hg_heads(dec, hp)
            for h in heads:
                dstate[h] = dsts[h] * dech[h] + new_dst[h]
            return carry

        lax.fori_loop(0, nchunk, chunk, 0)

    def col(cb):
        return pl.BlockSpec((tb, wp), lambda h, i: (nblk - 1 - i, cb * (nh // hp) + h))

    ocol = pl.BlockSpec((tb, wp), lambda h, i: (nblk - 1 - i, h))
    lbspec = pl.BlockSpec((1, wp), lambda h, i: (0, h))
    w = nh * HG_DK
    dq, df, dv, dlb = pl.pallas_call(
        body,
        out_shape=[jax.ShapeDtypeStruct((t, w), BF16)] * 3 + [jax.ShapeDtypeStruct((1, w), F32)],
        grid=(nh // hp, nblk),
        in_specs=[col(0), col(1), col(2), lbspec,
                  pl.BlockSpec((hp, nchunk, HG_DK, HG_DK), lambda h, i: (h, nblk - 1 - i, 0, 0)), ocol],
        out_specs=[ocol, ocol, ocol, lbspec],
        scratch_shapes=[pltpu.VMEM((hp, HG_DK, HG_DK), F32)],
        compiler_params=_cparams(("parallel", "arbitrary")),
        name="hgrn_bwd",
    )(zh, zh, zh, lb3.reshape(1, -1), states, d_o)
    return dq, df, dv, dlb.reshape(w)


NEG = -1e30
ATT_GW = ATT_HEADS * ATT_DH


def _att_scores(q, kp, kc, has_prev):
    scale = ATT_DH ** -0.5
    i = lax.broadcasted_iota(jnp.int32, (ATT_BLK, ATT_BLK), 0)
    j = lax.broadcasted_iota(jnp.int32, (ATT_BLK, ATT_BLK), 1)
    s_p = jnp.where(jnp.logical_and(j >= i, has_prev), _dot_nt(q, kp) * scale, NEG)
    s_c = jnp.where(j <= i, _dot_nt(q, kc) * scale, NEG)
    return s_p, s_c


def _att_views(arrs, d):
    return [a.reshape(d, -1, ATT_GW) for a in arrs]


def _att_unview(a, d):
    return a.reshape(-1, ATT_GW) if d == 1 else a


ATT_QB = 4


def _attn_fwd(qb, kb, vb, g):
    d = ATT_PATTERNS[g][1]
    q2, k2, v2 = _att_views([qb, kb, vb], d)
    nblk = q2.shape[1] // ATT_BLK
    nq = ATT_QB if nblk % ATT_QB == 0 else 1
    rows = nq * ATT_BLK

    def body(q_ref, kc_ref, kp_ref, vc_ref, vp_ref, o_ref, l_ref):
        first = pl.program_id(1) == 0
        hss = [slice(h * ATT_DH, (h + 1) * ATT_DH) for h in range(ATT_HEADS)]
        for b in range(nq):
            rs = slice(b * ATT_BLK, (b + 1) * ATT_BLK)
            ps = slice((b - 1) * ATT_BLK, b * ATT_BLK)
            has_prev = jnp.logical_not(first) if b == 0 else True
            kv = [(kp_ref[:, hs], vp_ref[:, hs]) if b == 0 else (kc_ref[ps, hs], vc_ref[ps, hs]) for hs in hss]
            sc = [_att_scores(q_ref[rs, hs], kv[h][0], kc_ref[rs, hs], has_prev) for h, hs in enumerate(hss)]
            ms = [jnp.maximum(jnp.max(s_p, axis=1, keepdims=True), jnp.max(s_c, axis=1, keepdims=True)) for s_p, s_c in sc]
            ps_ = [(jnp.exp(s_p - m), jnp.exp(s_c - m)) for (s_p, s_c), m in zip(sc, ms)]
            ls = [jnp.sum(p_p, axis=1, keepdims=True) + jnp.sum(p_c, axis=1, keepdims=True) for p_p, p_c in ps_]
            os_ = [_dot(p_p.astype(BF16), kv[h][1]) + _dot(p_c.astype(BF16), vc_ref[rs, hss[h]]) for h, (p_p, p_c) in enumerate(ps_)]
            for h, hs in enumerate(hss):
                o_ref[rs, hs] = os_[h] / ls[h]
                l_ref[rs, hs] = jnp.broadcast_to(ms[h] + jnp.log(ls[h]), (ATT_BLK, ATT_DH))

    cur = pl.BlockSpec((None, rows, ATT_GW), lambda r, n: (r, n, 0))
    prev = pl.BlockSpec((None, ATT_BLK, ATT_GW), lambda r, n: (r, jnp.maximum(n * nq - 1, 0), 0))
    o, lse = pl.pallas_call(
        body,
        out_shape=[jax.ShapeDtypeStruct(q2.shape, F32)] * 2,
        grid=(d, nblk // nq),
        in_specs=[cur, cur, prev, cur, prev],
        out_specs=[cur, cur],
        compiler_params=_cparams(("parallel", "arbitrary")),
        name=f"attn_fwd_g{g}",
    )(q2, k2, k2, v2, v2)
    return _att_unview(o, d), _att_unview(lse, d)


def _attn_bwd(qb, kb, vb, o, lse, d_o, d_lse, g):
    d = ATT_PATTERNS[g][1]
    q2, k2, v2 = _att_views([qb, kb, vb], d)
    o2, l2, do2, dl2 = _att_views([o, lse, d_o, d_lse], d)
    nblk = q2.shape[1] // ATT_BLK
    nq = ATT_QB if nblk % ATT_QB == 0 else 1
    rows = nq * ATT_BLK
    ns = nblk // nq
    scale = ATT_DH ** -0.5

    def body(q_ref, kc_ref, kp_ref, vc_ref, vp_ref, o_ref, l_ref, do_ref, dl_ref, dq_ref, dk_ref, dv_ref, ck, cv):
        n = pl.program_id(1)

        @pl.when(n == 0)
        def _():
            ck[...] = jnp.zeros_like(ck)
            cv[...] = jnp.zeros_like(cv)

        first = n == ns - 1
        hss = [slice(h * ATT_DH, (h + 1) * ATT_DH) for h in range(ATT_HEADS)]
        heads = range(ATT_HEADS)
        pend_k, pend_v = [ck[:, hs] for hs in hss], [cv[:, hs] for hs in hss]
        for b in reversed(range(nq)):
            rs = slice(b * ATT_BLK, (b + 1) * ATT_BLK)
            ps = slice((b - 1) * ATT_BLK, b * ATT_BLK)
            has_prev = jnp.logical_not(first) if b == 0 else True
            q = [q_ref[rs, hs] for hs in hss]
            kc, vc = [kc_ref[rs, hs] for hs in hss], [vc_ref[rs, hs] for hs in hss]
            kp = [kp_ref[:, hs] if b == 0 else kc_ref[ps, hs] for hs in hss]
            vp = [vp_ref[:, hs] if b == 0 else vc_ref[ps, hs] for hs in hss]
            sc = [_att_scores(q[h], kp[h], kc[h], has_prev) for h in heads]
            dob = [do_ref[rs, hs].astype(BF16) for hs in hss]
            dp = [(_dot_nt(dob[h], vp[h]), _dot_nt(dob[h], vc[h])) for h in heads]
            delta = [jnp.sum(do_ref[rs, hs] * o_ref[rs, hs] - dl_ref[rs, hs], axis=1, keepdims=True) for hs in hss]
            pr = [(jnp.exp(sc[h][0] - l_ref[rs, hss[h]][:, 0:1]), jnp.exp(sc[h][1] - l_ref[rs, hss[h]][:, 0:1])) for h in heads]
            ds = [((pr[h][0] * (dp[h][0] - delta[h]) * scale).astype(BF16), (pr[h][1] * (dp[h][1] - delta[h]) * scale).astype(BF16))
                  for h in heads]
            pb = [(pr[h][0].astype(BF16), pr[h][1].astype(BF16)) for h in heads]
            dq = [_dot(ds[h][0], kp[h]) + _dot(ds[h][1], kc[h]) for h in heads]
            dk_c = [_dot_tn(ds[h][1], q[h]) for h in heads]
            dv_c = [_dot_tn(pb[h][1], dob[h]) for h in heads]
            dk_p = [_dot_tn(ds[h][0], q[h]) for h in heads]
            dv_p = [_dot_tn(pb[h][0], dob[h]) for h in heads]
            for h, hs in enumerate(hss):
                dq_ref[rs, hs] = dq[h]
                dk_ref[rs, hs] = pend_k[h] + dk_c[h]
                dv_ref[rs, hs] = pend_v[h] + dv_c[h]
            pend_k, pend_v = dk_p, dv_p
        for h, hs in enumerate(hss):
            ck[:, hs] = pend_k[h]
            cv[:, hs] = pend_v[h]

    cur = pl.BlockSpec((None, rows, ATT_GW), lambda r, n: (r, ns - 1 - n, 0))
    prev = pl.BlockSpec((None, ATT_BLK, ATT_GW), lambda r, n: (r, jnp.maximum((ns - 1 - n) * nq - 1, 0), 0))
    shp = jax.ShapeDtypeStruct(q2.shape, F32)
    dq, dk, dv = pl.pallas_call(
        body,
        out_shape=[shp, shp, shp],
        grid=(d, ns),
        in_specs=[cur, cur, prev, cur, prev, cur, cur, cur, cur],
        out_specs=[cur, cur, cur],
        scratch_shapes=[pltpu.VMEM((ATT_BLK, ATT_GW), F32), pltpu.VMEM((ATT_BLK, ATT_GW), F32)],
        compiler_params=_cparams(("parallel", "arbitrary")),
        name=f"attn_bwd_g{g}",
    )(q2, k2, k2, v2, v2, o2, l2, do2, dl2)
    return _att_unview(dq, d), _att_unview(dk, d), _att_unview(dv, d)


def _rms_parts(x, width):
    outs = []
    for lo in range(0, x.shape[1], width):
        xs = x[:, lo:lo + width].astype(F32)
        r = lax.rsqrt(jnp.mean(xs * xs, axis=1, keepdims=True) + EPS)
        outs.append((xs * r, r))
    return outs


def _rms_bwd_part(xh, r, dxh):
    return r * (dxh - xh * jnp.mean(dxh * xh, axis=1, keepdims=True))


def _norm_pro(a, consts):
    (xh, _), = _rms_parts(a[0], a[0].shape[1])
    return [(xh * consts[0]).astype(BF16)]


def _norm_bwd_fin(accs, ex, consts):
    xv, dres = ex
    (xh, r), = _rms_parts(xv, xv.shape[1])
    dx = dres + _rms_bwd_part(xh, r, accs[0] * consts[0])
    return [dx, dx], [_colsum8(accs[0] * xh)]


def _rot_sign():
    lane = lax.broadcasted_iota(jnp.int32, (1, ATT_DH), 1)
    return jnp.where(lane < ATT_DH // 2, -1.0, 1.0).astype(F32)


def _rope(y, cos, sin):
    return y * cos + pltpu.roll(y, ATT_DH // 2, axis=1) * _rot_sign() * sin


def _rope_t(dy, cos, sin):
    return dy * cos - pltpu.roll(dy * sin, ATT_DH // 2, axis=1) * _rot_sign()


def _qk_prep(zq, zk, zv, qn, kn, cos, sin):
    w = zq.shape[1]

    def fn(ins, consts):
        cs, sn = ins[3], ins[4]
        outs = []
        for z, gain in ((ins[0], consts[0]), (ins[1], consts[1])):
            for i, (xh, _) in enumerate(_rms_parts(z, ATT_DH)):
                outs.append(_rope(xh * gain[:, i * ATT_DH:(i + 1) * ATT_DH], cs, sn))
        outs += [ins[2][:, i * ATT_DH:(i + 1) * ATT_DH] for i in range(w // ATT_DH)]
        groups = [jnp.concatenate(outs[i:i + ATT_HEADS], axis=1) for i in range(0, len(outs), ATT_HEADS)]
        return groups, []

    outs, _ = _rowwise(fn, [(zq, w, 0), (zk, w, 0), (zv, w, 0), (cos, ATT_DH, 0), (sin, ATT_DH, 0)], [qn, kn],
                       [(ATT_GW, BF16, ATT_PATTERNS[g][1]) for g in range(ATT_GROUPS)] * 3, [], bm=512, name="qk_prep")
    return outs[0:3], outs[3:6], outs[6:9]


def _qk_prep_bwd(zq, zk, dq_g, dk_g, dv_g, qn, kn, cos, sin):
    w = zq.shape[1]

    def fn(ins, consts):
        cs, sn = ins[2], ins[3]
        outs, sums = [], []
        for z, gain, dparts in ((ins[0], consts[0], ins[4:7]), (ins[1], consts[1], ins[7:10])):
            dout = jnp.concatenate(dparts, axis=1)
            dz, dgain = [], []
            for i, (xh, r) in enumerate(_rms_parts(z, ATT_DH)):
                hs = slice(i * ATT_DH, (i + 1) * ATT_DH)
                dy = _rope_t(dout[:, hs], cs, sn)
                dgain.append(_colsum8(dy * xh))
                dz.append(_rms_bwd_part(xh, r, dy * gain[:, hs]))
            outs.append(jnp.concatenate(dz, axis=1))
            sums.append(jnp.concatenate(dgain, axis=1))
        outs.append(jnp.concatenate(ins[10:13], axis=1))
        return outs, sums

    ins = [(zq, w, 0), (zk, w, 0), (cos, ATT_DH, 0), (sin, ATT_DH, 0)]
    for parts in (dq_g, dk_g, dv_g):
        ins += [(a, ATT_GW, 0, ATT_PATTERNS[g][1]) for g, a in enumerate(parts)]
    (dzq, dzk, dzv), (dqn, dkn) = _rowwise(fn, ins, [qn, kn], [(w, BF16)] * 3, [w, w], bm=512, name="qk_prep_bwd")
    return dzq, dzk, dzv, dqn, dkn


def _post_a(o_raw, zh, gout):
    w = o_raw.shape[1]

    def fn(ins, consts):
        oh = jnp.concatenate([xh for xh, _ in _rms_parts(ins[0], HG_DK)], axis=1)
        hg = ins[1]
        return [oh * consts[0] * (hg * _sigmoid(hg))], []

    (y,), _ = _rowwise(fn, [(o_raw, w, 0), (zh, w, 3)], [gout.reshape(1, w)], [(w, BF16)], [], bm=512, name="post_a")
    return y


def _post_a_bwd(o_raw, zh, gout, dy):
    w = o_raw.shape[1]

    def fn(ins, consts):
        parts = _rms_parts(ins[0], HG_DK)
        oh = jnp.concatenate([xh for xh, _ in parts], axis=1)
        hg, dyv, gain = ins[1], ins[2], consts[0]
        sg = _sigmoid(hg)
        s = hg * sg
        doh = dyv * gain * s
        do = jnp.concatenate([_rms_bwd_part(xh, r, doh[:, i * HG_DK:(i + 1) * HG_DK]) for i, (xh, r) in enumerate(parts)], axis=1)
        dhg = dyv * oh * gain * (sg * (1.0 + hg * (1.0 - sg)))
        return [do, dhg], [_colsum8(dyv * oh * s)]

    (do, dhg), (dgain,) = _rowwise(fn, [(o_raw, w, 0), (zh, w, 3), (dy, w, 0)], [gout.reshape(1, w)],
                                   [(w, F32), (w, BF16)], [w], bm=512, name="post_a_bwd")
    return do, dhg, dgain


def _merge_alpha(lses):
    m = jnp.maximum(jnp.maximum(lses[0], lses[1]), lses[2])
    e = [jnp.exp(l - m) for l in lses]
    inv = 1.0 / (e[0] + e[1] + e[2])
    return [x * inv for x in e]


def _group_ins(parts):
    return [(a, ATT_GW, 0, ATT_PATTERNS[g][1]) for g, a in enumerate(parts)]


def _merge_b(o_g, lse_g):
    def fn(ins, consts):
        al = _merge_alpha(ins[3:6])
        return [al[0] * ins[0] + al[1] * ins[1] + al[2] * ins[2]], []

    (y,), _ = _rowwise(fn, _group_ins(o_g) + _group_ins(lse_g), [], [(ATT_GW, BF16)], [], bm=512, name="merge_b")
    return y


def _merge_b_bwd(o_g, lse_g, dy):
    def fn(ins, consts):
        al = _merge_alpha(ins[3:6])
        dyv = ins[6]
        dal = [dyv * ins[i] for i in range(3)]
        tot = al[0] * dal[0] + al[1] * dal[1] + al[2] * dal[2]
        return [al[i] * dyv for i in range(3)] + [al[i] * (dal[i] - tot) for i in range(3)], []

    outs, _ = _rowwise(fn, _group_ins(o_g) + _group_ins(lse_g) + [(dy, ATT_GW, 0)], [],
                       [(ATT_GW, F32, ATT_PATTERNS[g][1]) for g in range(ATT_GROUPS)] * 2, [], bm=512, name="merge_b_bwd")
    return outs[:3], outs[3:]


def _loss_head(y, target):
    d = y.shape[1]

    def fn(ins, consts):
        e = ins[0] - ins[1]
        return [e * (1.0 / d)] * 2, [_colsum8(e * e)]

    (dy, dyb), (sq,) = _rowwise(fn, [(y, d, 0), (target, d, 0)], [], [(d, F32), (d, BF16)], [d], bm=512, name="loss_head")
    return 0.5 * jnp.sum(sq) / d, dy, dyb


def _ffn_fwd(x, gain, wt, wo_fn, tag):
    t, d = x.shape
    f = wt.shape[0] // 2

    def act(accs, ex, consts):
        a, b = accs
        s = _sigmoid(a)
        sa = a * s
        return (sa * b, b, 0.5 * sa, 0.5 * (s + sa * (1.0 - s)))

    bn = FFN_BN if f % FFN_BN == 0 else 256
    u, b, sa, sp, h = _mm([x], [wt, wt], [(0, 0, 0), (0, 1, 1)], 2, act, [BF16] * 4, m=t, n=f, k=d, tb=True,
                          bm=512, bn=bn, bk=d, b_off=[(0, 0), (f // min(bn, f), 0)],
                          consts=[gain.reshape(1, d)], a_pro=_norm_pro, chunk=MXU_COLS, name=f"ffn_in_{tag}")
    wo = wo_fn(u)
    (y,) = _mm([u], [wo], [(0, 0, 0)], 1, lambda accs, ex: (ex[0] + 0.5 * accs[0],), [F32], m=t, n=d, k=f,
               bm=512, bn=d, bk=f, extras=[x], name=f"ffn_out_{tag}")
    return y, (x, h, u, b, sa, sp, wo)


def _ffn_bwd(dy, dyb, saved, gain, wt, tag, tok, emit):
    x, h, u, b, sa, sp, wo = saved
    t, d = x.shape
    f = wo.shape[0]

    def dact(accs, ex, consts):
        bv, sav, spv = (e.astype(F32) for e in ex)
        return (accs[0] * bv * spv, accs[0] * sav)

    bn = FFN_BN if f % FFN_BN == 0 else 256
    da, db = _mm([dyb], [wo], [(0, 0, 0)], 1, dact, [BF16, BF16], m=t, n=f, k=d, tb=True, bm=512, bn=bn, bk=d,
                 extras=[b, sa, sp], n_outer=True, chunk=MXU_COLS, consts=[tok], name=f"ffn_dact_{tag}")
    (dwo,) = _mm([u], [dyb], [(0, 0, 0)], 1, lambda accs, ex: (0.5 * accs[0],), [BF16], m=f, n=d, k=t, ta=True,
                 bm=1408, bn=d, bk=2048, name=f"ffn_dwo_{tag}")
    (dwt,) = _mm([da, db], [h], [(0, 0, 0)], 1, _first, [BF16], m=2 * f, n=d, k=t, ta=True, bm=min(1408, f), bn=d,
                 bk=2048, a_cat=True, name=f"ffn_dwt_{tag}")
    tok = emit(dwt, dwo)
    bk = min(FFN_BN, f)
    dx, dxb, dgain = _mm([da, db], [wt, wt], [(0, 0, 0), (1, 1, 0)], 1, _norm_bwd_fin, [F32, BF16], m=t, n=d, k=f,
                         bm=512, bn=d, bk=bk, b_off=[(0, 0), (0, f // bk)], extras=[x, dy],
                         consts=[gain.reshape(1, d), tok], n_sums=1, name=f"ffn_dh_{tag}")
    return dx, dxb, jnp.sum(dgain, axis=0), tok


FFN_BN = 2816
Z_SPLITS = (("h", 4096), ("q", 1536), ("k", 1536), ("v", 1536), ("g", 2048))


def _mix_fwd(x, p, cos, sin):
    t, d = x.shape
    z, off, hm = {}, 0, None
    for nm, width in Z_SPLITS:
        bn = 1024 if off % 1024 == 0 and width % 1024 == 0 else 512
        first = hm is None
        res = _mm([x if first else hm], [p["wint"]], [(0, 0, 0)], 1, (lambda accs, ex, consts: (accs[0],)) if first else _first,
                  [F32 if nm == "h" else BF16], m=t, n=width, k=d, tb=True, bm=1024 if first else 2048, bn=bn, bk=d,
                  b_off=[(off // bn, 0)],
                  consts=[p["gm"].reshape(1, d)] if first else (), a_pro=_norm_pro if first else None, name=f"mix_in_{nm}")
        z[nm] = res[0]
        hm = res[1] if first else hm
        off += width
    o_raw, states = _hgrn_fwd(z["h"], p["lb3"])
    qb, kb, vb = _qk_prep(z["q"], z["k"], z["v"], p["qn"], p["kn"], cos, sin)
    o_g, lse_g = zip(*[_attn_fwd(qb[g], kb[g], vb[g], g) for g in range(ATT_GROUPS)])
    oa = _post_a(o_raw, z["h"], p["gout"])
    ob = _merge_b(o_g, lse_g)
    late = p["late"](ob)
    p = dict(p, **late)
    (ya,) = _mm([oa], [p["wa"]], [(0, 0, 0)], 1, _first, [F32], m=t, n=d, k=oa.shape[1], bm=1024, bn=d, bk=oa.shape[1],
                name="branch_a")

    def gate(accs, ex):
        return (_sigmoid(ex[0].astype(F32)) * ex[2] + _sigmoid(ex[1].astype(F32)) * accs[0], accs[0])

    merged, yb = _mm([ob], [p["wbt"]], [(0, 0, 0)], 1, gate, [BF16, F32], m=t, n=d, k=ATT_GW, tb=True, bm=512, bn=d,
                     bk=ATT_GW, extras=[z["g"], z["g"], ya], e_off=[0, 1, 0], chunk=MXU_COLS, name="branch_b_gate")
    (y,) = _mm([merged], [p["wo"]], [(0, 0, 0)], 1, lambda accs, ex: (ex[0] + accs[0],), [F32], m=t, n=d, k=d,
               bm=1024, bn=d, bk=d, extras=[x], name="mix_out")
    return y, (x, hm, z, o_raw, states, qb, kb, vb, o_g, lse_g, oa, ob, ya, yb, merged, late)


def _mix_bwd(dy, dyb, saved, p, cos, sin, tok):
    x, hm, z, o_raw, states, qb, kb, vb, o_g, lse_g, oa, ob, ya, yb, merged, late = saved
    p = dict(p, **late)
    t, d = x.shape
    w = oa.shape[1]

    def dgate(accs, ex, consts):
        dm = accs[0]
        sa, sb = _sigmoid(ex[0].astype(F32)), _sigmoid(ex[1].astype(F32))
        return (sa * dm, sb * dm, dm * ex[2] * sa * (1.0 - sa), dm * ex[3] * sb * (1.0 - sb))

    dya, dyb_, dga, dgb = _mm([dyb], [p["wo"]], [(0, 0, 0)], 1, dgate, [BF16] * 4, m=t, n=d, k=d, tb=True, bm=512, bn=d,
                              bk=d, extras=[z["g"], z["g"], ya, yb], e_off=[0, 1, 0, 0], chunk=MXU_COLS, consts=[tok], name="mix_out_bwd")
    (dwo,) = _mm([merged], [dyb], [(0, 0, 0)], 1, _first, [BF16], m=d, n=d, k=t, ta=True, bm=d, bn=d, bk=1024, name="mix_dwo")
    (doa,) = _mm([dya], [p["wa"]], [(0, 0, 0)], 1, _first, [F32], m=t, n=w, k=d, tb=True, bm=1024, bn=w, bk=d, name="branch_a_bwd")
    (dwa,) = _mm([oa], [dya], [(0, 0, 0)], 1, _first, [BF16], m=w, n=d, k=t, ta=True, bm=w, bn=d, bk=1024, name="branch_a_dw")
    (dob,) = _mm([dyb_], [p["wbt"]], [(0, 0, 0)], 1, _first, [F32], m=t, n=ATT_GW, k=d, bm=1024, bn=ATT_GW, bk=d,
                 name="branch_b_bwd")
    (dwbt,) = _mm([dyb_], [ob], [(0, 0, 0)], 1, _first, [BF16], m=d, n=ATT_GW, k=t, ta=True, bm=d, bn=ATT_GW, bk=1024,
                  name="branch_b_dw")
    do_raw, dhg, dgout = _post_a_bwd(o_raw, z["h"], p["gout"], doa)
    do_g, dlse_g = _merge_b_bwd(o_g, lse_g, dob)
    dq_g, dk_g, dv_g = zip(*[_attn_bwd(qb[g], kb[g], vb[g], o_g[g], lse_g[g], do_g[g], dlse_g[g], g)
                             for g in range(ATT_GROUPS)])
    dzq, dzk, dzv, dqn, dkn = _qk_prep_bwd(z["q"], z["k"], dq_g, dk_g, dv_g, p["qn"], p["kn"], cos, sin)
    dhq, dhf, dhi, lbsum = _hgrn_bwd(z["h"], p["lb3"], states, do_raw)
    dz = jnp.concatenate([dhq, dhf, dhi, dhg, dzq, dzk, dzv, dga, dgb], axis=1)
    pw = dz.shape[1]
    (dwint,) = _mm([dz], [hm], [(0, 0, 0)], 1, _first, [BF16], m=pw, n=d, k=t, ta=True, bm=1536, bn=d, bk=2048, name="mix_in_dw")
    dx, dxb, dgm = _mm([dz], [p["wint"]], [(0, 0, 0)], 1, _norm_bwd_fin, [F32, BF16], m=t, n=d, k=pw, bm=1024, bn=d, bk=1536,
                       extras=[x, dy], consts=[p["gm"].reshape(1, d)], n_sums=1, name="mix_in_bwd")
    return dx, dxb, dict(gm=jnp.sum(dgm, axis=0), wint=dwint, lbsum=lbsum, gout=dgout, qn=dqn, kn=dkn, wa=dwa, wbt=dwbt, wo=dwo)


def _rope_tables(t):
    pos = jnp.arange(t, dtype=F32)
    inv = ROPE_THETA ** (-jnp.arange(0, ATT_DH, 2, dtype=F32) / ATT_DH)
    ang = pos[:, None] * inv[None, :]
    ang = jnp.concatenate([ang, ang], axis=-1)
    return jnp.cos(ang), jnp.sin(ang)


def _lower_bounds(logits):
    lb = jnp.cumsum(jax.nn.softmax(logits, axis=0), axis=0)
    return lb - lb[0:1]


def _head_gain(g):
    return jnp.tile(g[:, None, :], (1, ATT_HEADS, 1)).reshape(1, ATT_GROUPS * ATT_GW)


SMALL_GRADS = ("ffn1_norm", "mix_norm", "lbsum", "hgrn_out_norm", "attn_q_norm", "attn_k_norm", "ffn2_norm")


def _local_step(x, target, small, fetch, emit):
    t = x.shape[0]
    depth = small["ffn1_norm"].shape[0]
    cos, sin = _rope_tables(t)
    lb_all = _lower_bounds(small["hgrn_lb_logits"])
    saved = []
    for l in range(depth):
        w1t = fetch("w1t", l, x)["w1t"]
        x, s1 = _ffn_fwd(x, small["ffn1_norm"][l], w1t, lambda after, l=l: fetch("w1o", l, after)["w1o"], "1")
        p = dict(gm=small["mix_norm"][l], wint=fetch("wint", l, x)["wint"], lb3=lb_all[l].reshape(-1, 1, HG_DK),
                 gout=small["hgrn_out_norm"][l], qn=_head_gain(small["attn_q_norm"][l]),
                 kn=_head_gain(small["attn_k_norm"][l]), late=functools.partial(fetch, "mout", l))
        x, sm = _mix_fwd(x, p, cos, sin)
        w2t = fetch("w2t", l, x)["w2t"]
        x, s2 = _ffn_fwd(x, small["ffn2_norm"][l], w2t, lambda after, l=l: fetch("w2o", l, after)["w2o"], "2")
        saved.append((p, w1t, w2t, s1, sm, s2))
    loss, dx, dxb = _loss_head(x, target)
    gsmall = {k: [None] * depth for k in SMALL_GRADS}
    tok = jnp.zeros((8, 128), F32)
    for l in reversed(range(depth)):
        p, w1t, w2t, s1, sm, s2 = saved[l]
        dx, dxb, gsmall["ffn2_norm"][l], tok = _ffn_bwd(
            dx, dxb, s2, small["ffn2_norm"][l], w2t, "2", tok, lambda dwt, dwo, l=l: emit("ffn2", l, dict(w2t=dwt, w2o=dwo), None))
        dx, dxb, gm = _mix_bwd(dx, dxb, sm, p, cos, sin, tok)
        tok = emit("mix", l, {k: gm[k] for k in ("wint", "wa", "wbt", "wo")}, None)
        gsmall["mix_norm"][l], gsmall["lbsum"][l], gsmall["hgrn_out_norm"][l] = gm["gm"], gm["lbsum"], gm["gout"]
        for k, src in (("attn_q_norm", "qn"), ("attn_k_norm", "kn")):
            gsmall[k][l] = jnp.sum(gm[src].reshape(ATT_GROUPS, ATT_HEADS, ATT_DH), axis=1)
        dx, dxb, gsmall["ffn1_norm"][l], tok = _ffn_bwd(
            dx, dxb, s1, small["ffn1_norm"][l], w1t, "1", tok, lambda dwt, dwo, l=l: emit("ffn1", l, dict(w1t=dwt, w1o=dwo), None))
    emit("small", 0, {}, ({k: jnp.stack(v) for k, v in gsmall.items()}, loss))
    return dx


_HBM = pl.BlockSpec(memory_space=pltpu.HBM)
_SEM = pl.BlockSpec(memory_space=pltpu.SEMAPHORE)
_EFFECT = pltpu.SideEffectType.DATAFLOW_SIDE_EFFECTING


def _peer(p):
    x, y, c = lax.axis_index("x"), lax.axis_index("y"), lax.axis_index("c")
    me = 4 * x + 2 * y + c
    return (1 - x if p & 4 else x, 1 - y if p & 2 else y, 1 - c if p & 1 else c), jnp.bitwise_xor(me, p), me


def _xchg_copy(src, land, mode, send_sems, recv_sems, k, p, arriving):
    peer, peer_id, me = _peer(p)
    block = src if mode == "gather" else src.at[peer_id]
    return pltpu.make_async_remote_copy(
        src_ref=block, dst_ref=land.at[peer_id if arriving else me], send_sem=send_sems.at[k * (N_DEV - 1) + p - 1],
        recv_sem=recv_sems.at[k * (N_DEV - 1) + p - 1], device_id=peer, device_id_type=MESH)


def _xchg_start(srcs, modes, groups, name):
    n, ng = len(srcs), len(groups)

    def body(*refs):
        src = refs[:n]
        sems = refs[n:n + 2 * ng]
        land = refs[n + 2 * ng + n:n + 2 * ng + 2 * n]
        token = refs[n + 2 * ng + 2 * n]
        for gi, idx in enumerate(groups):
            for ki, k in enumerate(idx):
                for p in range(1, N_DEV):
                    _xchg_copy(src[k], land[k], modes[k], sems[2 * gi], sems[2 * gi + 1], ki, p, False).start()
        token[...] = jnp.zeros_like(token)

    sem_shapes = []
    for idx in groups:
        sem_shapes += [pltpu.SemaphoreType.DMA((len(idx) * (N_DEV - 1),))] * 2
    outs = pl.pallas_call(
        body,
        out_shape=sem_shapes + [pltpu.HBM(a.shape, a.dtype) for a in srcs]
        + [pltpu.HBM((N_DEV,) + a.shape[-2:], a.dtype) for a in srcs] + [jax.ShapeDtypeStruct((8, 128), F32)],
        in_specs=[_HBM] * n,
        out_specs=[_SEM] * (2 * ng) + [_HBM] * (2 * n) + [pl.BlockSpec(memory_space=pltpu.VMEM)],
        input_output_aliases={i: 2 * ng + i for i in range(n)},
        compiler_params=pltpu.CompilerParams(has_side_effects=_EFFECT),
        name=name,
    )(*[pltpu.with_memory_space_constraint(a, pltpu.HBM) for a in srcs])
    sems = [(outs[2 * gi], outs[2 * gi + 1]) for gi in range(ng)]
    return sems, outs[2 * ng:2 * ng + n], outs[2 * ng + n:2 * ng + 2 * n], outs[-1]


def _xchg_wait_call(srcs, lands, modes, sems, after, name):
    n = len(srcs)

    def body(*refs):
        src, land = refs[:n], refs[n:2 * n]
        send_sems, recv_sems = refs[2 * n], refs[2 * n + 1]
        for p in range(1, N_DEV):
            for k in range(n):
                cp = _xchg_copy(src[k], land[k], modes[k], send_sems, recv_sems, k, p, True)
                cp.wait_send()
                cp.wait_recv()

    outs = pl.pallas_call(
        body,
        out_shape=[pltpu.HBM(a.shape, a.dtype) for a in list(srcs) + list(lands)],
        in_specs=[_HBM] * (2 * n) + [_SEM, _SEM, pl.BlockSpec(memory_space=pl.ANY)],
        out_specs=[_HBM] * (2 * n),
        input_output_aliases={i: i for i in range(2 * n)},
        compiler_params=pltpu.CompilerParams(has_side_effects=_EFFECT),
        name=name,
    )(*srcs, *lands, sems[0], sems[1], after)
    return outs[:n], outs[n:]


def _xchg_wait(srcs, lands, modes, sems, after, name):
    srcs, lands = _xchg_wait_call(srcs, lands, modes, sems, after, name)
    me = 4 * lax.axis_index("x") + 2 * lax.axis_index("y") + lax.axis_index("c")
    done = []
    for a, land, mode in zip(srcs, lands, modes):
        own = a[None] if mode == "gather" else lax.dynamic_slice_in_dim(a, me, 1, axis=0)
        done.append(lax.dynamic_update_slice(land, own, (me, 0, 0)))
    return done


def _sum_slots(land):
    g, _, r, c = land.shape
    br = r // 2 if (r % 32 == 0 and r >= 256) else r

    def body(l_ref, o_ref):
        acc = l_ref[0, 0].astype(F32)
        for j in range(1, N_DEV):
            acc = acc + l_ref[0, j].astype(F32)
        o_ref[0] = acc

    return pl.pallas_call(
        body,
        out_shape=jax.ShapeDtypeStruct((g, r, c), F32),
        grid=(g, r // br),
        in_specs=[pl.BlockSpec((1, N_DEV, br, c), lambda i, j: (i, 0, j, 0))],
        out_specs=pl.BlockSpec((1, br, c), lambda i, j: (i, j, 0)),
        compiler_params=_cparams(("parallel", "parallel")),
        name="sum_slots",
    )(land)


def _adamw(w, g, m, v):
    shape = w.shape
    cols = shape[-1]
    rows = int(np.prod(shape[:-1]))
    bm = max(b for b in range(8, 513, 8) if rows % b == 0) if rows % 8 == 0 else rows
    c1 = 1.0 - ADAM_B1 ** ADAM_STEP
    c2 = 1.0 - ADAM_B2 ** ADAM_STEP

    def fn(ins, consts):
        wv, gv, mv, vv = ins
        m2 = ADAM_B1 * mv + (1.0 - ADAM_B1) * gv
        v2 = ADAM_B2 * vv + (1.0 - ADAM_B2) * (gv * gv)
        delta = -ADAM_LR * ((m2 / c1) / (jnp.sqrt(v2 / c2) + ADAM_EPS) + ADAM_WD * wv)
        return [delta, m2, v2], []

    outs, _ = _rowwise(fn, [(a.reshape(rows, cols), cols, 0) for a in (w, g, m, v)], [], [(cols, F32)] * 3, [],
                       bm=bm, name="adamw")
    return [o.reshape(shape) for o in outs]


BIG = ("w1t", "w1o", "wint", "wa", "wbt", "wo", "w2t", "w2o")
FETCH_GROUPS = dict(w1t=("w1t",), w1o=("w1o",), wint=("wint",), mout=("wa", "wbt", "wo"), w2t=("w2t",), w2o=("w2o",))
SMALL_ROWS = (("ffn1_norm", 0), ("mix_norm", 2), ("lbsum", 4), ("hgrn_out_norm", 6), ("ffn2_norm", 8),
              ("attn_q_norm", 10), ("attn_k_norm", 12))
SMALL_PACK_ROWS = 16


def kernel(x, ffn1_norm, ffn1_w_in, ffn1_w_out, mix_norm, w_in, hgrn_lb_logits, hgrn_out_norm, attn_q_norm, attn_k_norm, w_branch_a, w_branch_b, w_out, ffn2_norm, ffn2_w_in, ffn2_w_out, loss_target, m_ffn1_norm, m_ffn1_w_in, m_ffn1_w_out, m_mix_norm, m_w_in, m_hgrn_lb_logits, m_hgrn_out_norm, m_attn_q_norm, m_attn_k_norm, m_w_branch_a, m_w_branch_b, m_w_out, m_ffn2_norm, m_ffn2_w_in, m_ffn2_w_out, v_ffn1_norm, v_ffn1_w_in, v_ffn1_w_out, v_mix_norm, v_w_in, v_hgrn_lb_logits, v_hgrn_out_norm, v_attn_q_norm, v_attn_k_norm, v_w_branch_a, v_w_branch_b, v_w_out, v_ffn2_norm, v_ffn2_w_in, v_ffn2_w_out):
    names = ("ffn1_norm", "ffn1_w_in", "ffn1_w_out", "mix_norm", "w_in", "hgrn_lb_logits", "hgrn_out_norm", "attn_q_norm",
             "attn_k_norm", "w_branch_a", "w_branch_b", "w_out", "ffn2_norm", "ffn2_w_in", "ffn2_w_out")
    w = dict(zip(names, (ffn1_norm, ffn1_w_in, ffn1_w_out, mix_norm, w_in, hgrn_lb_logits, hgrn_out_norm, attn_q_norm,
                         attn_k_norm, w_branch_a, w_branch_b, w_out, ffn2_norm, ffn2_w_in, ffn2_w_out)))
    m = dict(zip(names, (m_ffn1_norm, m_ffn1_w_in, m_ffn1_w_out, m_mix_norm, m_w_in, m_hgrn_lb_logits, m_hgrn_out_norm,
                         m_attn_q_norm, m_attn_k_norm, m_w_branch_a, m_w_branch_b, m_w_out, m_ffn2_norm, m_ffn2_w_in, m_ffn2_w_out)))
    v = dict(zip(names, (v_ffn1_norm, v_ffn1_w_in, v_ffn1_w_out, v_mix_norm, v_w_in, v_hgrn_lb_logits, v_hgrn_out_norm,
                         v_attn_q_norm, v_attn_k_norm, v_w_branch_a, v_w_branch_b, v_w_out, v_ffn2_norm, v_ffn2_w_in, v_ffn2_w_out)))
    depth, d = ffn1_norm.shape

    def tr(a):
        return jnp.swapaxes(a, 1, 2)

    shard = dict(w1t=tr(ffn1_w_in), w1o=ffn1_w_out, wint=tr(w_in), wa=w_branch_a,
                 wbt=tr(w_branch_b).reshape(depth, -1, d), wo=w_out, w2t=tr(ffn2_w_in), w2o=ffn2_w_out)
    order = [(g, l) for l in range(depth) for g in FETCH_GROUPS]
    started = {}
    for name, part in (("gather_start", order),):
        flat = [(l, k) for g, l in part for k in FETCH_GROUPS[g]]
        groups, pos = [], 0
        for g, l in part:
            groups.append(list(range(pos, pos + len(FETCH_GROUPS[g]))))
            pos += len(FETCH_GROUPS[g])
        sems, srcs, lands, _ = _xchg_start([shard[k][l].astype(BF16) for l, k in flat], ["gather"] * len(flat), groups, name)
        for gi, key in enumerate(part):
            started[key] = ([srcs[i] for i in groups[gi]], [lands[i] for i in groups[gi]], sems[gi])

    def fetch(group, l, after):
        srcs, lands, sems = started[group, l]
        lands = _xchg_wait(srcs, lands, ["gather"] * len(srcs), sems, after, f"gather_wait_{group}{l}")
        out = {}
        for k, land in zip(FETCH_GROUPS[group], lands):
            out[k] = land.reshape(d, -1) if k == "wbt" else land.reshape(-1, d)
        return out

    pending = []

    def emit(group, l, g, final):
        keys = list(g)
        srcs = [g[k].reshape(N_DEV, -1, d) for k in keys]
        modes = ["scatter"] * len(keys)
        if final is not None:
            gsmall, loss = final
            pack = jnp.zeros((SMALL_PACK_ROWS, d), F32)
            for k, r0 in SMALL_ROWS:
                rows = gsmall[k].reshape(depth, -1)
                pack = pack.at[r0:r0 + depth, :rows.shape[1]].set(rows)
            srcs.append(pack.at[14, :].set(loss))
            modes.append("gather")
            keys.append("small")
        sems, s_thru, l_thru, token = _xchg_start(srcs, modes, [list(range(len(srcs)))], f"grads_start_{group}{l}")
        pending.append((group, l, keys, modes, sems[0], s_thru, l_thru))
        return token

    small = {k: w[k] for k in ("ffn1_norm", "mix_norm", "hgrn_lb_logits", "hgrn_out_norm", "attn_q_norm", "attn_k_norm", "ffn2_norm")}
    dx = _local_step(x[0], loss_target[0], small, fetch, emit)

    summed, after = {}, dx
    for group, l, keys, modes, sems, s_thru, l_thru in pending:
        lands = _xchg_wait(s_thru, l_thru, modes, sems, after, f"grads_wait_{group}{l}")
        for k, land in zip(keys, lands):
            summed[k, l] = _sum_slots(land[None])[0]
        after = summed[keys[-1], l]
    gsum = {k: jnp.stack([summed[k, l] for l in range(depth)]) for k in BIG}
    tot = summed["small", 0]

    grads = {}
    for k, r0 in SMALL_ROWS:
        shp = (depth,) + (w[k].shape[1:] if k != "lbsum" else (d,))
        grads[k] = tot[r0:r0 + depth, :int(np.prod(shp[1:]))].reshape(shp)
    _, lb_vjp = jax.vjp(_lower_bounds, hgrn_lb_logits)
    grads["hgrn_lb_logits"] = lb_vjp(grads.pop("lbsum"))[0]
    grads["ffn1_w_in"], grads["ffn1_w_out"] = tr(gsum["w1t"]), gsum["w1o"]
    grads["w_in"], grads["w_branch_a"] = tr(gsum["wint"]), gsum["wa"]
    grads["w_branch_b"] = tr(gsum["wbt"].reshape(depth, d // N_DEV, -1))
    grads["w_out"] = gsum["wo"]
    grads["ffn2_w_in"], grads["ffn2_w_out"] = tr(gsum["w2t"]), gsum["w2o"]

    upd = {k: _adamw(w[k], grads[k], m[k], v[k]) for k in names}
    return (tot[14, 0], dx[None], *[grads[k] for k in names], *[upd[k][0] for k in names],
            *[upd[k][1] for k in names], *[upd[k][2] for k in names])
```
